```python
import jax, jax.numpy as jnp
from jax import lax
import numpy as np

D_MODEL = 1024
BATCH = 16
SEQ = 2048
DEPTH = 2

MEM_LEN = 256
CHUNK = 128
SG_GROUPS = 8
SG_WIDTH = D_MODEL
SG_GDIM = SG_WIDTH // SG_GROUPS
SSM_INNER = 2 * D_MODEL
SSM_HEADDIM = 64
SSM_HEADS = SSM_INNER // SSM_HEADDIM
SSM_STATE = 128
SSM_GROUPS = 4
SSM_RPG = SSM_HEADS // SSM_GROUPS
SSM_CONV = 4
SSM_CONV_DIM = SSM_INNER + 2 * SSM_GROUPS * SSM_STATE
X_HEADS = 4
X_HEADDIM = D_MODEL // X_HEADS
FFN_HIDDEN = -(-(8 * D_MODEL) // (3 * 256)) * 256
ALPHA = float((2 * DEPTH) ** 0.25)
BETA = float((8 * DEPTH) ** -0.25)
LN_EPS = 1e-5
RMS_EPS = 1e-5

U_END = SG_WIDTH
V_END = U_END + SG_WIDTH
Z_END = V_END + SSM_INNER
XBC_END = Z_END + SSM_CONV_DIM
DT_END = XBC_END + SSM_HEADS
GA_END = DT_END + D_MODEL
IN_COLS = GA_END + D_MODEL
IN_SPLITS = (U_END, V_END, Z_END, XBC_END, DT_END, GA_END)

kernel_name = "hybrid_gmlp_ssd_gated_deepnorm"


def layer_norm(x, g, b):
    xf = x.astype(jnp.float32)
    mu = jnp.mean(xf, axis=-1, keepdims=True)
    var = jnp.mean(jnp.square(xf - mu), axis=-1, keepdims=True)
    return ((xf - mu) * lax.rsqrt(var + LN_EPS) * g.astype(jnp.float32) + b.astype(jnp.float32)).astype(x.dtype)


def spatial_gating(u, v, ln_g, ln_b, w_s, b_s):
    bn, s, _ = v.shape
    u = jax.nn.gelu(u, approximate=False)
    v = layer_norm(jax.nn.gelu(v, approximate=False), ln_g, ln_b)
    vc = v.reshape(bn, s // CHUNK, CHUNK, SG_GROUPS, SG_GDIM)
    causal = jnp.tril(jnp.ones((CHUNK, CHUNK), dtype=bool))
    w = jnp.where(causal[None], w_s, jnp.zeros((), w_s.dtype))
    mixed = jnp.einsum('gts,bcsgd->bctgd', w, vc) + jnp.transpose(b_s)[None, None, :, :, None]
    return u * mixed.reshape(bn, s, SG_WIDTH)


def ssd_branch(z, xbc, dt, conv_w, conv_b, dt_bias, a_log, d_skip, norm_g):
    bn, s, _ = xbc.shape
    nc = s // CHUNK
    xbc = lax.conv_general_dilated(xbc, conv_w[:, None, :], window_strides=(1,),
                                   padding=[(SSM_CONV - 1, 0)],
                                   dimension_numbers=('NWC', 'WIO', 'NWC'),
                                   feature_group_count=SSM_CONV_DIM) + conv_b
    xbc = jax.nn.silu(xbc).astype(jnp.float32)
    xs = xbc[..., :SSM_INNER]
    bm = xbc[..., SSM_INNER:SSM_INNER + SSM_GROUPS * SSM_STATE]
    cm = xbc[..., SSM_INNER + SSM_GROUPS * SSM_STATE:]
    dt = jax.nn.softplus(dt.astype(jnp.float32) + dt_bias.astype(jnp.float32))
    a = -jnp.exp(a_log.astype(jnp.float32)).reshape(SSM_GROUPS, SSM_RPG)

    x = xs.reshape(bn, nc, CHUNK, SSM_GROUPS, SSM_RPG, SSM_HEADDIM)
    dtc = dt.reshape(bn, nc, CHUNK, SSM_GROUPS, SSM_RPG)
    bm = bm.reshape(bn, nc, CHUNK, SSM_GROUPS, SSM_STATE)
    cm = cm.reshape(bn, nc, CHUNK, SSM_GROUPS, SSM_STATE)
    xdt = x * dtc[..., None]
    da = jnp.moveaxis(dtc * a, 2, -1)
    da_cs = jnp.cumsum(da, axis=-1)

    causal = jnp.tril(jnp.ones((CHUNK, CHUNK), dtype=bool))
    seg = da_cs[..., :, None] - da_cs[..., None, :]
    decay = jnp.exp(jnp.where(causal, seg, -jnp.inf))
    cb = jnp.einsum('bclgn,bcsgn->bcgls', cm, bm)
    y_diag = jnp.einsum('bcgls,bcgrls,bcsgrp->bclgrp', cb, decay, xdt)

    decay_states = jnp.exp(da_cs[..., -1:] - da_cs)
    states = jnp.einsum('bclgn,bcgrl,bclgrp->bcgrpn', bm, decay_states, xdt)
    chunk_decay = jnp.exp(da_cs[..., -1])

    def step(carry, inp):
        st, dec = inp
        return carry * dec[..., None, None] + st, carry

    init = jnp.zeros((bn, SSM_GROUPS, SSM_RPG, SSM_HEADDIM, SSM_STATE), jnp.float32)
    _, prev = lax.scan(step, init, (jnp.moveaxis(states, 1, 0), jnp.moveaxis(chunk_decay, 1, 0)))
    prev = jnp.moveaxis(prev, 0, 1)

    y_off = jnp.einsum('bclgn,bcgrpn,bcgrl->bclgrp', cm, prev, jnp.exp(da_cs))
    y = y_diag + y_off + x * d_skip.astype(jnp.float32).reshape(SSM_GROUPS, SSM_RPG)[..., None]
    y = y.reshape(bn, s, SSM_INNER) * jax.nn.silu(z.astype(jnp.float32))
    yg = y.reshape(bn, s, SSM_GROUPS, SSM_INNER // SSM_GROUPS)
    yg = yg * lax.rsqrt(jnp.mean(jnp.square(yg), axis=-1, keepdims=True) + RMS_EPS)
    y = yg.reshape(bn, s, SSM_INNER) * norm_g.astype(jnp.float32)
    return y.astype(z.dtype)


def mixer(h, w_in, sg_ln_g, sg_ln_b, sg_w, sg_b, conv_w, conv_b, dt_bias, a_log,
          d_skip, ssm_norm_g, p_a, p_b, w_mix_o):
    proj = h @ w_in
    u, v, z, xbc, dt, g_a, g_b = jnp.split(proj, IN_SPLITS, axis=-1)
    br_a = spatial_gating(u, v, sg_ln_g, sg_ln_b, sg_w, sg_b) @ p_a
    br_b = ssd_branch(z, xbc, dt, conv_w, conv_b, dt_bias, a_log, d_skip, ssm_norm_g) @ p_b
    merged = jax.nn.sigmoid(g_a) * br_a + jax.nn.sigmoid(g_b) * br_b
    return merged @ w_mix_o


def cross_attention(h, mem_n, w_xq, w_xkv, w_xo):
    bn, s, _ = h.shape
    q = (h @ w_xq).reshape(bn, s, X_HEADS, X_HEADDIM)
    k, v = jnp.split(mem_n @ w_xkv, 2, axis=-1)
    k = k.reshape(bn, -1, X_HEADS, X_HEADDIM)
    v = v.reshape(bn, -1, X_HEADS, X_HEADDIM)
    scores = jnp.einsum('bshd,bmhd->bhsm', q, k).astype(jnp.float32) * (X_HEADDIM ** -0.5)
    p = jax.nn.softmax(scores, axis=-1).astype(v.dtype)
    o = jnp.einsum('bhsm,bmhd->bshd', p, v).reshape(bn, s, D_MODEL)
    return o @ w_xo


def swiglu(h, w_ffn_in, w_ffn_out):
    g, u = jnp.split(h @ w_ffn_in, 2, axis=-1)
    return (jax.nn.silu(g) * u) @ w_ffn_out


def _fwd_setup_inputs(seed: int = 0) -> dict:
    key = jax.random.key(seed)
    ks = iter(jax.random.split(key, 40))
    f32 = jnp.float32

    def nrm(shape, scale):
        return jax.random.normal(next(ks), shape, f32) * scale

    x = jax.random.normal(next(ks), (BATCH, SEQ, D_MODEL), f32)
    mem = jax.random.normal(next(ks), (BATCH, MEM_LEN, D_MODEL), f32)
    mem_ln_g = 1.0 + nrm((D_MODEL,), 0.02)
    mem_ln_b = nrm((D_MODEL,), 0.02)
    w_in = nrm((DEPTH, D_MODEL, IN_COLS), D_MODEL ** -0.5)
    sg_ln_g = 1.0 + nrm((DEPTH, SG_WIDTH), 0.02)
    sg_ln_b = nrm((DEPTH, SG_WIDTH), 0.02)
    sg_w = nrm((DEPTH, SG_GROUPS, CHUNK, CHUNK), CHUNK ** -0.5)
    sg_b = 1.0 + nrm((DEPTH, SG_GROUPS, CHUNK), 0.01)
    conv_w = nrm((DEPTH, SSM_CONV, SSM_CONV_DIM), SSM_CONV ** -0.5)
    conv_b = nrm((DEPTH, SSM_CONV_DIM), 0.02)
    dt0 = jnp.exp(jax.random.uniform(next(ks), (DEPTH, SSM_HEADS), f32,
                                     float(np.log(1e-3)), float(np.log(1e-1))))
    dt_bias = dt0 + jnp.log(-jnp.expm1(-dt0))
    a_log = jnp.log(jax.random.uniform(next(ks), (DEPTH, SSM_HEADS), f32, 1.0, 16.0))
    d_skip = 1.0 + nrm((DEPTH, SSM_HEADS), 0.02)
    ssm_norm_g = 1.0 + nrm((DEPTH, SSM_INNER), 0.02)
    p_a = nrm((DEPTH, SG_WIDTH, D_MODEL), BETA * SG_WIDTH ** -0.5)
    p_b = nrm((DEPTH, SSM_INNER, D_MODEL), BETA * SSM_INNER ** -0.5)
    w_mix_o = nrm((DEPTH, D_MODEL, D_MODEL), BETA * D_MODEL ** -0.5)
    w_xq = nrm((DEPTH, D_MODEL, D_MODEL), D_MODEL ** -0.5)
    w_xkv = jnp.concatenate([nrm((DEPTH, D_MODEL, D_MODEL), D_MODEL ** -0.5),
                             nrm((DEPTH, D_MODEL, D_MODEL), BETA * D_MODEL ** -0.5)], axis=-1)
    w_xo = nrm((DEPTH, D_MODEL, D_MODEL), BETA * D_MODEL ** -0.5)
    w_ffn_in = nrm((DEPTH, D_MODEL, 2 * FFN_HIDDEN), BETA * D_MODEL ** -0.5)
    w_ffn_out = nrm((DEPTH, FFN_HIDDEN, D_MODEL), BETA * FFN_HIDDEN ** -0.5)
    ln_g = 1.0 + nrm((DEPTH, 3, D_MODEL), 0.02)
    ln_b = nrm((DEPTH, 3, D_MODEL), 0.02)
    return {"x": x, "mem": mem, "mem_ln_g": mem_ln_g, "mem_ln_b": mem_ln_b,
            "w_in": w_in, "sg_ln_g": sg_ln_g, "sg_ln_b": sg_ln_b, "sg_w": sg_w, "sg_b": sg_b,
            "conv_w": conv_w, "conv_b": conv_b, "dt_bias": dt_bias, "a_log": a_log,
            "d_skip": d_skip, "ssm_norm_g": ssm_norm_g, "p_a": p_a, "p_b": p_b,
            "w_mix_o": w_mix_o, "w_xq": w_xq, "w_xkv": w_xkv, "w_xo": w_xo,
            "w_ffn_in": w_ffn_in, "w_ffn_out": w_ffn_out, "ln_g": ln_g, "ln_b": ln_b}


def _fwd_reference(x, mem, mem_ln_g, mem_ln_b, w_in, sg_ln_g, sg_ln_b, sg_w, sg_b, conv_w, conv_b,
              dt_bias, a_log, d_skip, ssm_norm_g, p_a, p_b, w_mix_o, w_xq, w_xkv, w_xo,
              w_ffn_in, w_ffn_out, ln_g, ln_b):
    mem_n = layer_norm(mem, mem_ln_g, mem_ln_b)
    for i in range(DEPTH):
        y = mixer(x, w_in[i], sg_ln_g[i], sg_ln_b[i], sg_w[i], sg_b[i], conv_w[i], conv_b[i],
                  dt_bias[i], a_log[i], d_skip[i], ssm_norm_g[i], p_a[i], p_b[i], w_mix_o[i])
        x = layer_norm(ALPHA * x + y, ln_g[i, 0], ln_b[i, 0])
        y = cross_attention(x, mem_n, w_xq[i], w_xkv[i], w_xo[i])
        x = layer_norm(ALPHA * x + y, ln_g[i, 1], ln_b[i, 1])
        y = swiglu(x, w_ffn_in[i], w_ffn_out[i])
        x = layer_norm(ALPHA * x + y, ln_g[i, 2], ln_b[i, 2])
    return x


import jax as _jax
import jax.numpy as _jnp

TWIN_FORMAT = 'train_step'
FWD_PARAMS = ['x', 'mem', 'mem_ln_g', 'mem_ln_b', 'w_in', 'sg_ln_g', 'sg_ln_b', 'sg_w', 'sg_b', 'conv_w', 'conv_b', 'dt_bias', 'a_log', 'd_skip', 'ssm_norm_g', 'p_a', 'p_b', 'w_mix_o', 'w_xq', 'w_xkv', 'w_xo', 'w_ffn_in', 'w_ffn_out', 'ln_g', 'ln_b']
TWIN_WEIGHTS = ['mem_ln_g', 'mem_ln_b', 'w_in', 'sg_ln_g', 'sg_ln_b', 'sg_w', 'sg_b', 'conv_w', 'conv_b', 'dt_bias', 'a_log', 'd_skip', 'ssm_norm_g', 'p_a', 'p_b', 'w_mix_o', 'w_xq', 'w_xkv', 'w_xo', 'w_ffn_in', 'w_ffn_out', 'ln_g', 'ln_b']
TWIN_DIFF_INPUT = 'x'
TWIN_INPUTS = ['x', 'mem', 'mem_ln_g', 'mem_ln_b', 'w_in', 'sg_ln_g', 'sg_ln_b', 'sg_w', 'sg_b', 'conv_w', 'conv_b', 'dt_bias', 'a_log', 'd_skip', 'ssm_norm_g', 'p_a', 'p_b', 'w_mix_o', 'w_xq', 'w_xkv', 'w_xo', 'w_ffn_in', 'w_ffn_out', 'ln_g', 'ln_b', 'loss_target', 'm_mem_ln_g', 'm_mem_ln_b', 'm_w_in', 'm_sg_ln_g', 'm_sg_ln_b', 'm_sg_w', 'm_sg_b', 'm_conv_w', 'm_conv_b', 'm_dt_bias', 'm_a_log', 'm_d_skip', 'm_ssm_norm_g', 'm_p_a', 'm_p_b', 'm_w_mix_o', 'm_w_xq', 'm_w_xkv', 'm_w_xo', 'm_w_ffn_in', 'm_w_ffn_out', 'm_ln_g', 'm_ln_b', 'v_mem_ln_g', 'v_mem_ln_b', 'v_w_in', 'v_sg_ln_g', 'v_sg_ln_b', 'v_sg_w', 'v_sg_b', 'v_conv_w', 'v_conv_b', 'v_dt_bias', 'v_a_log', 'v_d_skip', 'v_ssm_norm_g', 'v_p_a', 'v_p_b', 'v_w_mix_o', 'v_w_xq', 'v_w_xkv', 'v_w_xo', 'v_w_ffn_in', 'v_w_ffn_out', 'v_ln_g', 'v_ln_b']
TWIN_OUTPUTS = ['loss', 'grad_x', 'grad_mem_ln_g', 'grad_mem_ln_b', 'grad_w_in', 'grad_sg_ln_g', 'grad_sg_ln_b', 'grad_sg_w', 'grad_sg_b', 'grad_conv_w', 'grad_conv_b', 'grad_dt_bias', 'grad_a_log', 'grad_d_skip', 'grad_ssm_norm_g', 'grad_p_a', 'grad_p_b', 'grad_w_mix_o', 'grad_w_xq', 'grad_w_xkv', 'grad_w_xo', 'grad_w_ffn_in', 'grad_w_ffn_out', 'grad_ln_g', 'grad_ln_b', 'delta_mem_ln_g', 'delta_mem_ln_b', 'delta_w_in', 'delta_sg_ln_g', 'delta_sg_ln_b', 'delta_sg_w', 'delta_sg_b', 'delta_conv_w', 'delta_conv_b', 'delta_dt_bias', 'delta_a_log', 'delta_d_skip', 'delta_ssm_norm_g', 'delta_p_a', 'delta_p_b', 'delta_w_mix_o', 'delta_w_xq', 'delta_w_xkv', 'delta_w_xo', 'delta_w_ffn_in', 'delta_w_ffn_out', 'delta_ln_g', 'delta_ln_b', 'new_m_mem_ln_g', 'new_m_mem_ln_b', 'new_m_w_in', 'new_m_sg_ln_g', 'new_m_sg_ln_b', 'new_m_sg_w', 'new_m_sg_b', 'new_m_conv_w', 'new_m_conv_b', 'new_m_dt_bias', 'new_m_a_log', 'new_m_d_skip', 'new_m_ssm_norm_g', 'new_m_p_a', 'new_m_p_b', 'new_m_w_mix_o', 'new_m_w_xq', 'new_m_w_xkv', 'new_m_w_xo', 'new_m_w_ffn_in', 'new_m_w_ffn_out', 'new_m_ln_g', 'new_m_ln_b', 'new_v_mem_ln_g', 'new_v_mem_ln_b', 'new_v_w_in', 'new_v_sg_ln_g', 'new_v_sg_ln_b', 'new_v_sg_w', 'new_v_sg_b', 'new_v_conv_w', 'new_v_conv_b', 'new_v_dt_bias', 'new_v_a_log', 'new_v_d_skip', 'new_v_ssm_norm_g', 'new_v_p_a', 'new_v_p_b', 'new_v_w_mix_o', 'new_v_w_xq', 'new_v_w_xkv', 'new_v_w_xo', 'new_v_w_ffn_in', 'new_v_w_ffn_out', 'new_v_ln_g', 'new_v_ln_b']
TWIN_LEAF_KINDS = {'loss': 'loss', 'grad_x': 'grad_x', 'grad_mem_ln_g': 'grad_w', 'grad_mem_ln_b': 'grad_w', 'grad_w_in': 'grad_w', 'grad_sg_ln_g': 'grad_w', 'grad_sg_ln_b': 'grad_w', 'grad_sg_w': 'grad_w', 'grad_sg_b': 'grad_w', 'grad_conv_w': 'grad_w', 'grad_conv_b': 'grad_w', 'grad_dt_bias': 'grad_w', 'grad_a_log': 'grad_w', 'grad_d_skip': 'grad_w', 'grad_ssm_norm_g': 'grad_w', 'grad_p_a': 'grad_w', 'grad_p_b': 'grad_w', 'grad_w_mix_o': 'grad_w', 'grad_w_xq': 'grad_w', 'grad_w_xkv': 'grad_w', 'grad_w_xo': 'grad_w', 'grad_w_ffn_in': 'grad_w', 'grad_w_ffn_out': 'grad_w', 'grad_ln_g': 'grad_w', 'grad_ln_b': 'grad_w', 'delta_mem_ln_g': 'delta_w', 'delta_mem_ln_b': 'delta_w', 'delta_w_in': 'delta_w', 'delta_sg_ln_g': 'delta_w', 'delta_sg_ln_b': 'delta_w', 'delta_sg_w': 'delta_w', 'delta_sg_b': 'delta_w', 'delta_conv_w': 'delta_w', 'delta_conv_b': 'delta_w', 'delta_dt_bias': 'delta_w', 'delta_a_log': 'delta_w', 'delta_d_skip': 'delta_w', 'delta_ssm_norm_g': 'delta_w', 'delta_p_a': 'delta_w', 'delta_p_b': 'delta_w', 'delta_w_mix_o': 'delta_w', 'delta_w_xq': 'delta_w', 'delta_w_xkv': 'delta_w', 'delta_w_xo': 'delta_w', 'delta_w_ffn_in': 'delta_w', 'delta_w_ffn_out': 'delta_w', 'delta_ln_g': 'delta_w', 'delta_ln_b': 'delta_w', 'new_m_mem_ln_g': 'new_m', 'new_m_mem_ln_b': 'new_m', 'new_m_w_in': 'new_m', 'new_m_sg_ln_g': 'new_m', 'new_m_sg_ln_b': 'new_m', 'new_m_sg_w': 'new_m', 'new_m_sg_b': 'new_m', 'new_m_conv_w': 'new_m', 'new_m_conv_b': 'new_m', 'new_m_dt_bias': 'new_m', 'new_m_a_log': 'new_m', 'new_m_d_skip': 'new_m', 'new_m_ssm_norm_g': 'new_m', 'new_m_p_a': 'new_m', 'new_m_p_b': 'new_m', 'new_m_w_mix_o': 'new_m', 'new_m_w_xq': 'new_m', 'new_m_w_xkv': 'new_m', 'new_m_w_xo': 'new_m', 'new_m_w_ffn_in': 'new_m', 'new_m_w_ffn_out': 'new_m', 'new_m_ln_g': 'new_m', 'new_m_ln_b': 'new_m', 'new_v_mem_ln_g': 'new_v', 'new_v_mem_ln_b': 'new_v', 'new_v_w_in': 'new_v', 'new_v_sg_ln_g': 'new_v', 'new_v_sg_ln_b': 'new_v', 'new_v_sg_w': 'new_v', 'new_v_sg_b': 'new_v', 'new_v_conv_w': 'new_v', 'new_v_conv_b': 'new_v', 'new_v_dt_bias': 'new_v', 'new_v_a_log': 'new_v', 'new_v_d_skip': 'new_v', 'new_v_ssm_norm_g': 'new_v', 'new_v_p_a': 'new_v', 'new_v_p_b': 'new_v', 'new_v_w_mix_o': 'new_v', 'new_v_w_xq': 'new_v', 'new_v_w_xkv': 'new_v', 'new_v_w_xo': 'new_v', 'new_v_w_ffn_in': 'new_v', 'new_v_w_ffn_out': 'new_v', 'new_v_ln_g': 'new_v', 'new_v_ln_b': 'new_v'}


def _forward(args):
    return _fwd_reference(*[args[k] for k in FWD_PARAMS])


def _output_shape():
    out = _jax.eval_shape(lambda: _forward(_fwd_setup_inputs(0)))
    return out.shape, out.dtype

N_MICROBATCH = 1
ADAM_LR = 0.001
ADAM_B1 = 0.9
ADAM_B2 = 0.999
ADAM_EPS = 1e-08
ADAM_WD = 0.01
ADAM_STEP = 10
PER_EXAMPLE_BATCH_AXIS = {'x': 0, 'mem': 0, 'loss_target': 0}
SHARED_INPUTS = []
_WEIGHT_DTYPES = {'mem_ln_g': _jnp.float32, 'mem_ln_b': _jnp.float32, 'w_in': _jnp.float32, 'sg_ln_g': _jnp.float32, 'sg_ln_b': _jnp.float32, 'sg_w': _jnp.float32, 'sg_b': _jnp.float32, 'conv_w': _jnp.float32, 'conv_b': _jnp.float32, 'dt_bias': _jnp.float32, 'a_log': _jnp.float32, 'd_skip': _jnp.float32, 'ssm_norm_g': _jnp.float32, 'p_a': _jnp.float32, 'p_b': _jnp.float32, 'w_mix_o': _jnp.float32, 'w_xq': _jnp.float32, 'w_xkv': _jnp.float32, 'w_xo': _jnp.float32, 'w_ffn_in': _jnp.float32, 'w_ffn_out': _jnp.float32, 'ln_g': _jnp.float32, 'ln_b': _jnp.float32}
MOMENT_SCALE = {'mem_ln_g': 7.102760e-03, 'mem_ln_b': 1.251695e-01, 'w_in': 1.046386e-02, 'sg_ln_g': 8.428866e-03, 'sg_ln_b': 7.905857e-03, 'sg_w': 7.845750e-03, 'sg_b': 1.081385e-02, 'conv_w': 1.073997e-02, 'conv_b': 1.575498e-02, 'dt_bias': 2.057708e-02, 'a_log': 3.506448e-02, 'd_skip': 9.107233e-02, 'ssm_norm_g': 1.251450e-02, 'p_a': 3.210793e-02, 'p_b': 3.554330e-02, 'w_mix_o': 4.777686e-02, 'w_xq': 3.290858e-03, 'w_xkv': 6.202276e-03, 'w_xo': 7.998264e-03, 'w_ffn_in': 1.026287e-02, 'w_ffn_out': 1.688561e-02, 'ln_g': 1.313262e+01, 'ln_b': 8.077772e-01}


def _to_microbatches(a, axis):
    t = _jnp.moveaxis(a, axis, 0)
    t = t.reshape((N_MICROBATCH, t.shape[0] // N_MICROBATCH) + t.shape[1:])
    return _jnp.moveaxis(t, 1, axis + 1)


def setup_inputs(seed: int = 0) -> dict:
    inp = _fwd_setup_inputs(seed)
    key = _jax.random.fold_in(_jax.random.key(seed), 7919)
    shape, _ = _output_shape()
    out = dict(inp)
    out["loss_target"] = _jax.random.normal(_jax.random.fold_in(key, 0), shape, _jnp.float32)
    for i, name in enumerate(TWIN_WEIGHTS):
        w = inp[name].astype(_jnp.float32)
        if MOMENT_SCALE is None:
            s = _jnp.sqrt(_jnp.mean(_jnp.square(w)) + 1e-30)
        else:
            s = MOMENT_SCALE[name]
        km, kv = _jax.random.split(_jax.random.fold_in(key, i + 1))
        out[name] = w
        out["m_" + name] = s * _jax.random.normal(km, w.shape, _jnp.float32)
        out["v_" + name] = (s * s) * _jax.random.uniform(kv, w.shape, _jnp.float32, 0.5, 1.5)
    if N_MICROBATCH > 1:
        for name, axis in PER_EXAMPLE_BATCH_AXIS.items():
            out[name] = _to_microbatches(out[name], axis)
    return {'x': out['x'], 'mem': out['mem'], 'mem_ln_g': out['mem_ln_g'], 'mem_ln_b': out['mem_ln_b'], 'w_in': out['w_in'], 'sg_ln_g': out['sg_ln_g'], 'sg_ln_b': out['sg_ln_b'], 'sg_w': out['sg_w'], 'sg_b': out['sg_b'], 'conv_w': out['conv_w'], 'conv_b': out['conv_b'], 'dt_bias': out['dt_bias'], 'a_log': out['a_log'], 'd_skip': out['d_skip'], 'ssm_norm_g': out['ssm_norm_g'], 'p_a': out['p_a'], 'p_b': out['p_b'], 'w_mix_o': out['w_mix_o'], 'w_xq': out['w_xq'], 'w_xkv': out['w_xkv'], 'w_xo': out['w_xo'], 'w_ffn_in': out['w_ffn_in'], 'w_ffn_out': out['w_ffn_out'], 'ln_g': out['ln_g'], 'ln_b': out['ln_b'], 'loss_target': out['loss_target'], 'm_mem_ln_g': out['m_mem_ln_g'], 'm_mem_ln_b': out['m_mem_ln_b'], 'm_w_in': out['m_w_in'], 'm_sg_ln_g': out['m_sg_ln_g'], 'm_sg_ln_b': out['m_sg_ln_b'], 'm_sg_w': out['m_sg_w'], 'm_sg_b': out['m_sg_b'], 'm_conv_w': out['m_conv_w'], 'm_conv_b': out['m_conv_b'], 'm_dt_bias': out['m_dt_bias'], 'm_a_log': out['m_a_log'], 'm_d_skip': out['m_d_skip'], 'm_ssm_norm_g': out['m_ssm_norm_g'], 'm_p_a': out['m_p_a'], 'm_p_b': out['m_p_b'], 'm_w_mix_o': out['m_w_mix_o'], 'm_w_xq': out['m_w_xq'], 'm_w_xkv': out['m_w_xkv'], 'm_w_xo': out['m_w_xo'], 'm_w_ffn_in': out['m_w_ffn_in'], 'm_w_ffn_out': out['m_w_ffn_out'], 'm_ln_g': out['m_ln_g'], 'm_ln_b': out['m_ln_b'], 'v_mem_ln_g': out['v_mem_ln_g'], 'v_mem_ln_b': out['v_mem_ln_b'], 'v_w_in': out['v_w_in'], 'v_sg_ln_g': out['v_sg_ln_g'], 'v_sg_ln_b': out['v_sg_ln_b'], 'v_sg_w': out['v_sg_w'], 'v_sg_b': out['v_sg_b'], 'v_conv_w': out['v_conv_w'], 'v_conv_b': out['v_conv_b'], 'v_dt_bias': out['v_dt_bias'], 'v_a_log': out['v_a_log'], 'v_d_skip': out['v_d_skip'], 'v_ssm_norm_g': out['v_ssm_norm_g'], 'v_p_a': out['v_p_a'], 'v_p_b': out['v_p_b'], 'v_w_mix_o': out['v_w_mix_o'], 'v_w_xq': out['v_w_xq'], 'v_w_xkv': out['v_w_xkv'], 'v_w_xo': out['v_w_xo'], 'v_w_ffn_in': out['v_w_ffn_in'], 'v_w_ffn_out': out['v_w_ffn_out'], 'v_ln_g': out['v_ln_g'], 'v_ln_b': out['v_ln_b']}


def _loss(weights, diff, rest, loss_target):
    with _jax.named_scope("forward"):
        args = {**rest, TWIN_DIFF_INPUT: diff, **{k: w.astype(_WEIGHT_DTYPES[k]) for k, w in weights.items()}}
        y = _forward(args)
    with _jax.named_scope("loss_head"):
        err = _jnp.square(y.astype(_jnp.float32) - loss_target)
        return 0.5 * _jnp.sum(_jnp.mean(err, axis=-1)) if err.ndim else 0.5 * err


def _adamw(w, g, m, v):
    m = ADAM_B1 * m + (1.0 - ADAM_B1) * g
    v = ADAM_B2 * v + (1.0 - ADAM_B2) * _jnp.square(g)
    m_hat = m / (1.0 - ADAM_B1 ** ADAM_STEP)
    v_hat = v / (1.0 - ADAM_B2 ** ADAM_STEP)
    delta = -ADAM_LR * (m_hat / (_jnp.sqrt(v_hat) + ADAM_EPS) + ADAM_WD * w)
    return delta, m, v


def reference(x, mem, mem_ln_g, mem_ln_b, w_in, sg_ln_g, sg_ln_b, sg_w, sg_b, conv_w, conv_b, dt_bias, a_log, d_skip, ssm_norm_g, p_a, p_b, w_mix_o, w_xq, w_xkv, w_xo, w_ffn_in, w_ffn_out, ln_g, ln_b, loss_target, m_mem_ln_g, m_mem_ln_b, m_w_in, m_sg_ln_g, m_sg_ln_b, m_sg_w, m_sg_b, m_conv_w, m_conv_b, m_dt_bias, m_a_log, m_d_skip, m_ssm_norm_g, m_p_a, m_p_b, m_w_mix_o, m_w_xq, m_w_xkv, m_w_xo, m_w_ffn_in, m_w_ffn_out, m_ln_g, m_ln_b, v_mem_ln_g, v_mem_ln_b, v_w_in, v_sg_ln_g, v_sg_ln_b, v_sg_w, v_sg_b, v_conv_w, v_conv_b, v_dt_bias, v_a_log, v_d_skip, v_ssm_norm_g, v_p_a, v_p_b, v_w_mix_o, v_w_xq, v_w_xkv, v_w_xo, v_w_ffn_in, v_w_ffn_out, v_ln_g, v_ln_b):
    given = dict(x=x, mem=mem, mem_ln_g=mem_ln_g, mem_ln_b=mem_ln_b, w_in=w_in, sg_ln_g=sg_ln_g, sg_ln_b=sg_ln_b, sg_w=sg_w, sg_b=sg_b, conv_w=conv_w, conv_b=conv_b, dt_bias=dt_bias, a_log=a_log, d_skip=d_skip, ssm_norm_g=ssm_norm_g, p_a=p_a, p_b=p_b, w_mix_o=w_mix_o, w_xq=w_xq, w_xkv=w_xkv, w_xo=w_xo, w_ffn_in=w_ffn_in, w_ffn_out=w_ffn_out, ln_g=ln_g, ln_b=ln_b, loss_target=loss_target, m_mem_ln_g=m_mem_ln_g, m_mem_ln_b=m_mem_ln_b, m_w_in=m_w_in, m_sg_ln_g=m_sg_ln_g, m_sg_ln_b=m_sg_ln_b, m_sg_w=m_sg_w, m_sg_b=m_sg_b, m_conv_w=m_conv_w, m_conv_b=m_conv_b, m_dt_bias=m_dt_bias, m_a_log=m_a_log, m_d_skip=m_d_skip, m_ssm_norm_g=m_ssm_norm_g, m_p_a=m_p_a, m_p_b=m_p_b, m_w_mix_o=m_w_mix_o, m_w_xq=m_w_xq, m_w_xkv=m_w_xkv, m_w_xo=m_w_xo, m_w_ffn_in=m_w_ffn_in, m_w_ffn_out=m_w_ffn_out, m_ln_g=m_ln_g, m_ln_b=m_ln_b, v_mem_ln_g=v_mem_ln_g, v_mem_ln_b=v_mem_ln_b, v_w_in=v_w_in, v_sg_ln_g=v_sg_ln_g, v_sg_ln_b=v_sg_ln_b, v_sg_w=v_sg_w, v_sg_b=v_sg_b, v_conv_w=v_conv_w, v_conv_b=v_conv_b, v_dt_bias=v_dt_bias, v_a_log=v_a_log, v_d_skip=v_d_skip, v_ssm_norm_g=v_ssm_norm_g, v_p_a=v_p_a, v_p_b=v_p_b, v_w_mix_o=v_w_mix_o, v_w_xq=v_w_xq, v_w_xkv=v_w_xkv, v_w_xo=v_w_xo, v_w_ffn_in=v_w_ffn_in, v_w_ffn_out=v_w_ffn_out, v_ln_g=v_ln_g, v_ln_b=v_ln_b)
    weights = {n: given[n] for n in TWIN_WEIGHTS}
    shared = {n: given[n] for n in SHARED_INPUTS}
    per_example = {n: given[n] for n in ['x', 'mem']}
    grad_fn = _jax.value_and_grad(_loss, argnums=(0, 1))

    def one_microbatch(ex, loss_target):
        ex = dict(ex)
        diff = ex.pop(TWIN_DIFF_INPUT)
        return grad_fn(weights, diff, {**shared, **ex}, loss_target)

    if N_MICROBATCH == 1:
        loss, (grad_w, grad_x) = one_microbatch(per_example, given["loss_target"])
    else:
        def body(carry, xs):
            loss_sum, grad_sum = carry
            l_k, (gw_k, gx_k) = one_microbatch(xs[0], xs[1])
            with _jax.named_scope("update"):
                return (loss_sum + l_k, _jax.tree.map(_jnp.add, grad_sum, gw_k)), gx_k

        init = (_jnp.zeros((), _jnp.float32), _jax.tree.map(_jnp.zeros_like, weights))
        (loss, grad_w), grad_x = _jax.lax.scan(body, init, (per_example, given["loss_target"]))
    with _jax.named_scope("update"):
        delta_w, new_m, new_v = {}, {}, {}
        for n in TWIN_WEIGHTS:
            delta_w[n], new_m[n], new_v[n] = _adamw(weights[n], grad_w[n], given["m_" + n], given["v_" + n])
    return (loss, grad_x, *[grad_w[n] for n in TWIN_WEIGHTS], *[delta_w[n] for n in TWIN_WEIGHTS],
            *[new_m[n] for n in TWIN_WEIGHTS], *[new_v[n] for n in TWIN_WEIGHTS])
```

```python
import functools
import math

import jax
import jax.numpy as jnp
from jax import lax
from jax.experimental import pallas as pl
from jax.experimental.pallas import tpu as pltpu

F32 = jnp.float32
BF16 = jnp.bfloat16
HIGHEST = lax.Precision.HIGHEST

N_DEV = 8
D_MODEL = 1024
DEPTH = 2
MEM_LEN = 256
CHUNK = 128
SG_GROUPS = 8
SSM_INNER = 2048
SSM_HEADDIM = 64
SSM_HEADS = 32
SSM_STATE = 128
SSM_GROUPS = 4
SSM_RPG = 8
SSM_CONV = 4
SSM_CONV_DIM = 3072
X_HEADS = 4
X_HEADDIM = 256
FFN_HIDDEN = 2816
ALPHA = float((2 * DEPTH) ** 0.25)
LN_EPS = 1e-5
RMS_EPS = 1e-5
IN_COLS = 9248
DT_COL = 7168
GA_COL = 7200
DT_PAD = 512
P_COLS = DT_COL + DT_PAD + 2 * D_MODEL
P_GA = DT_COL + DT_PAD
P_GB = P_GA + D_MODEL

ADAM_LR = 0.001
ADAM_B1 = 0.9
ADAM_B2 = 0.999
ADAM_EPS = 1e-08
ADAM_WD = 0.01
ADAM_STEP = 10

VMEM_LIMIT = 48 * 1024 * 1024

BIG = ("w_in", "p_a", "p_b", "w_mix_o", "w_xq", "w_xkv", "w_xo", "w_ffn_in", "w_ffn_out")
BIG_COL_SHARDED = ("w_in", "w_xkv", "w_ffn_in")
BIG_ROWS_PAD = 6656
SMALL_ROW_TILE = 256
SMALL_REP = ("mem_ln_g", "mem_ln_b", "sg_ln_g", "sg_ln_b", "sg_w", "sg_b", "conv_b", "dt_bias", "a_log", "d_skip",
             "ssm_norm_g")
SMALL_SH = ("conv_w", "ln_g", "ln_b")
WEIGHTS = ("mem_ln_g", "mem_ln_b", "w_in", "sg_ln_g", "sg_ln_b", "sg_w", "sg_b", "conv_w", "conv_b", "dt_bias", "a_log",
           "d_skip", "ssm_norm_g", "p_a", "p_b", "w_mix_o", "w_xq", "w_xkv", "w_xo", "w_ffn_in", "w_ffn_out", "ln_g", "ln_b")


def _layer_norm(x, g, b):
    mu = jnp.mean(x, axis=-1, keepdims=True)
    xc = x - mu
    var = jnp.mean(xc * xc, axis=-1, keepdims=True)
    return xc * lax.rsqrt(var + LN_EPS) * g + b


def _gelu(x):
    return 0.5 * x * (1.0 + lax.erf(x * (1.0 / math.sqrt(2.0))))


def _silu(x):
    return x * jax.nn.sigmoid(x)


def _softplus(x):
    return jnp.maximum(x, 0.0) + jnp.log1p(jnp.exp(-jnp.abs(x)))


def _causal_mask():
    r = lax.broadcasted_iota(jnp.int32, (CHUNK, CHUNK), 0)
    c = lax.broadcasted_iota(jnp.int32, (CHUNK, CHUNK), 1)
    return r >= c


def _sgu_block(u, v, ln_g, ln_b, w, sb):
    gu = _gelu(u)
    vn = _layer_norm(_gelu(v), ln_g, ln_b)
    causal = _causal_mask()
    width = D_MODEL // SG_GROUPS
    outs = []
    for g in range(SG_GROUPS):
        wg = jnp.where(causal, w[g], 0.0).astype(BF16)
        mixed = jnp.dot(wg, vn[:, g * width:(g + 1) * width].astype(BF16), preferred_element_type=F32)
        outs.append(mixed + sb[:, g:g + 1])
    return (gu * jnp.concatenate(outs, axis=1),)


def _ssd_block(xc, dtraw, z, prev, dt_bias, a_log, d_skip, norm_g):
    prev = [prev[g * SSM_STATE:(g + 1) * SSM_STATE, :] for g in range(SSM_GROUPS)]
    xs = xc[:, :SSM_INNER]
    bm = xc[:, SSM_INNER:SSM_INNER + SSM_GROUPS * SSM_STATE]
    cm = xc[:, SSM_INNER + SSM_GROUPS * SSM_STATE:]
    dt = _softplus(dtraw + dt_bias)
    da = dt * (-jnp.exp(a_log))
    causal = _causal_mask()
    tril = jnp.where(causal, 1.0, 0.0).astype(F32)
    cs = jnp.dot(tril, da, precision=HIGHEST, preferred_element_type=F32)
    cs_t = cs.T
    cs_last = cs[CHUNK - 1:CHUNK, :]
    decay_states = jnp.exp(cs_last - cs)
    decay_in = jnp.exp(cs)
    chunk_decay = jnp.exp(cs_last)
    gw = SSM_RPG * SSM_HEADDIM
    ys, new_states = [], []
    for g in range(SSM_GROUPS):
        bg = bm[:, g * SSM_STATE:(g + 1) * SSM_STATE].astype(BF16)
        cg = cm[:, g * SSM_STATE:(g + 1) * SSM_STATE].astype(BF16)
        cb = lax.dot_general(cg, bg, (((1,), (1,)), ((), ())), preferred_element_type=F32)
        y_off = jnp.dot(cg, prev[g].astype(BF16), preferred_element_type=F32)
        y_heads, xdt_heads, dec_heads = [], [], []
        for r in range(SSM_RPG):
            h = g * SSM_RPG + r
            xh = xs[:, h * SSM_HEADDIM:(h + 1) * SSM_HEADDIM]
            xdt = xh * dt[:, h:h + 1]
            seg = cs[:, h:h + 1] - cs_t[h:h + 1, :]
            decay = jnp.exp(jnp.where(causal, seg, -1e30))
            y_diag = jnp.dot((cb * decay).astype(BF16), xdt.astype(BF16), preferred_element_type=F32)
            y_in = y_off[:, r * SSM_HEADDIM:(r + 1) * SSM_HEADDIM] * decay_in[:, h:h + 1]
            y_heads.append(y_diag + y_in + xh * d_skip[:, h:h + 1])
            xdt_heads.append(xdt * decay_states[:, h:h + 1])
            dec_heads.append(jnp.broadcast_to(chunk_decay[:, h:h + 1], (1, SSM_HEADDIM)))
        xdt_s = jnp.concatenate(xdt_heads, axis=1).astype(BF16)
        st = lax.dot_general(bg, xdt_s, (((0,), (0,)), ((), ())), preferred_element_type=F32)
        new_states.append(prev[g] * jnp.concatenate(dec_heads, axis=1) + st)
        y = jnp.concatenate(y_heads, axis=1) * _silu(z[:, g * gw:(g + 1) * gw])
        ys.append(y * lax.rsqrt(jnp.mean(y * y, axis=-1, keepdims=True) + RMS_EPS))
    return jnp.concatenate(ys, axis=1) * norm_g, jnp.concatenate(new_states, axis=0)


def _conv_block(x, w, b):
    rows = lax.broadcasted_iota(jnp.int32, x.shape, 0)
    acc = x * w[SSM_CONV - 1:SSM_CONV, :] + b
    for k in range(SSM_CONV - 1):
        shift = SSM_CONV - 1 - k
        acc = acc + _shift_rows(x, rows, shift) * w[k:k + 1, :]
    return (_silu(acc),)


@functools.partial(jax.custom_vjp, nondiff_argnums=(2,))
def _shift_rows(x, rows, shift):
    return jnp.where(rows >= shift, pltpu.roll(x, shift, 0), 0.0)


def _shift_rows_fwd(x, rows, shift):
    return _shift_rows(x, rows, shift), rows


def _shift_rows_bwd(shift, rows, g):
    n = g.shape[0]
    return jnp.where(rows < n - shift, pltpu.roll(g, n - shift, 0), 0.0), None


_shift_rows.defvjp(_shift_rows_fwd, _shift_rows_bwd)


def _merge_block(ga, gb, br_a, br_b):
    return (jax.nn.sigmoid(ga) * br_a + jax.nn.sigmoid(gb) * br_b,)


def _lnres_block(x, y, g, b):
    return (_layer_norm(ALPHA * x + y, g, b),)


def _memln_block(x, g, b):
    return (_layer_norm(x, g, b),)


def _attn_block(q, kv):
    outs = []
    for h in range(X_HEADS):
        qh = q[:, h * X_HEADDIM:(h + 1) * X_HEADDIM].astype(BF16)
        kh = kv[:, h * X_HEADDIM:(h + 1) * X_HEADDIM].astype(BF16)
        vh = kv[:, D_MODEL + h * X_HEADDIM:D_MODEL + (h + 1) * X_HEADDIM].astype(BF16)
        s = lax.dot_general(qh, kh, (((1,), (1,)), ((), ())), preferred_element_type=F32) * (X_HEADDIM ** -0.5)
        s = s - lax.stop_gradient(jnp.max(s, axis=-1, keepdims=True))
        e = jnp.exp(s)
        p = e / jnp.sum(e, axis=-1, keepdims=True)
        outs.append(jnp.dot(p.astype(BF16), vh, preferred_element_type=F32))
    return (jnp.concatenate(outs, axis=1),)


def _swiglu_block(gu):
    return (_silu(gu[:, :FFN_HIDDEN]) * gu[:, FFN_HIDDEN:],)


def _params(grid):
    return pltpu.CompilerParams(dimension_semantics=("arbitrary",) * len(grid), vmem_limit_bytes=VMEM_LIMIT)


def _stage_fwd(name, f, grid, ins, in_specs, out_shapes, out_specs, interpret=False):
    n_in = len(ins)

    def body(*refs):
        res = f(*[r[...] for r in refs[:n_in]])
        for o_ref, val in zip(refs[n_in:], res):
            o_ref[...] = val.astype(o_ref.dtype)

    return pl.pallas_call(body, grid=grid, in_specs=in_specs, out_specs=out_specs, out_shape=out_shapes, name=name,
                          compiler_params=_params(grid), interpret=interpret)(*ins)


def _stage_bwd(name, f, grid, ins, in_specs, cts, ct_specs, grads, interpret=False):
    n_in = len(ins)
    flat_cts = [c for group in cts for c in group]
    flat_ct_specs = [s for group, spec in zip(cts, ct_specs) for s in (spec,) * len(group)]
    n_ct = len(flat_cts)
    diff = [g[0] for g in grads]

    def body(*refs):
        vals = [r[...] for r in refs[:n_in]]
        ct_refs = refs[n_in:n_in + n_ct]
        g_refs = refs[n_in + n_ct:]
        ct_vals, pos = [], 0
        for group in cts:
            acc = ct_refs[pos][...].astype(F32)
            for j in range(1, len(group)):
                acc = acc + ct_refs[pos + j][...].astype(F32)
            ct_vals.append(acc)
            pos += len(group)

        def g_fn(*dvals):
            full = list(vals)
            for i, dv in zip(diff, dvals):
                full[i] = dv
            return f(*full)

        _, vjp = jax.vjp(g_fn, *[vals[i] for i in diff])
        gvals = vjp(tuple(ct_vals))
        for gspec, g_ref, gval in zip(grads, g_refs, gvals):
            acc_axes = gspec[1]
            if not acc_axes:
                g_ref[...] = gval.astype(g_ref.dtype)
            else:
                first = pl.program_id(acc_axes[0]) == 0
                for ax in acc_axes[1:]:
                    first = first & (pl.program_id(ax) == 0)

                @pl.when(first)
                def _():
                    g_ref[...] = jnp.zeros_like(g_ref)

                g_ref[...] += gval.astype(g_ref.dtype)

    out_shapes, out_specs = [], []
    for gspec in grads:
        shape, spec = gspec[3] if len(gspec) > 3 else (ins[gspec[0]].shape, in_specs[gspec[0]])
        out_shapes.append(jax.ShapeDtypeStruct(shape, gspec[2]))
        out_specs.append(spec)
    return pl.pallas_call(body, grid=grid, in_specs=list(in_specs) + flat_ct_specs, out_specs=out_specs,
                          out_shape=out_shapes, name=name, compiler_params=_params(grid), interpret=interpret)(*ins, *flat_cts)


def _pick_tile(n, candidates):
    for c in candidates:
        if n % c == 0:
            return c
    return n


def _matmul(name, a, b, *, ta=False, tb=False, add=None, out_dtype=F32, interpret=False):
    if ta:
        k_dim, m = a.shape
    else:
        m, k_dim = a.shape
    n = b.shape[0] if tb else b.shape[1]
    assert (b.shape[1] if tb else b.shape[0]) == k_dim and not (ta and tb)
    tm = _pick_tile(m, (512, 256, 128))
    tn = _pick_tile(n, (512, 256, 128))
    tk = k_dim if k_dim <= 2816 and not ta else _pick_tile(k_dim, (2048, 1408, 1024, 512, 256, 128))
    nk = k_dim // tk
    grid = (m // tm, n // tn, nk)
    a_spec = pl.BlockSpec((tk, tm), lambda i, j, k: (k, i)) if ta else pl.BlockSpec((tm, tk), lambda i, j, k: (i, k))
    b_spec = pl.BlockSpec((tn, tk), lambda i, j, k: (j, k)) if tb else pl.BlockSpec((tk, tn), lambda i, j, k: (k, j))
    o_spec = pl.BlockSpec((tm, tn), lambda i, j, k: (i, j))
    dims = (((0 if ta else 1,), (1 if tb else 0,)), ((), ()))
    has_add = add is not None

    def body(*refs):
        a_ref, b_ref = refs[0], refs[1]
        add_ref = refs[2] if has_add else None
        o_ref, acc_ref = refs[-2], refs[-1]
        k = pl.program_id(2)
        part = lax.dot_general(a_ref[...].astype(BF16), b_ref[...].astype(BF16), dims, preferred_element_type=F32)

        @pl.when(k == 0)
        def _():
            acc_ref[...] = part

        @pl.when(k > 0)
        def _():
            acc_ref[...] += part

        @pl.when(k == nk - 1)
        def _():
            res = acc_ref[...]
            if has_add:
                res = res + add_ref[...].astype(F32)
            o_ref[...] = res.astype(o_ref.dtype)

    ins = [a, b] + ([add] if has_add else [])
    in_specs = [a_spec, b_spec] + ([o_spec] if has_add else [])
    return pl.pallas_call(
        body, grid=grid, in_specs=in_specs, out_specs=o_spec, out_shape=jax.ShapeDtypeStruct((m, n), out_dtype),
        scratch_shapes=[pltpu.VMEM((tm, tn), F32)], name=name,
        compiler_params=pltpu.CompilerParams(dimension_semantics=("parallel", "parallel", "arbitrary"),
                                             vmem_limit_bytes=VMEM_LIMIT), interpret=interpret)(*ins)


def _ssd_fwd(xc, proj, dt_bias, a_log, d_skip, norm_g, nb, nc, interpret=False):
    t = xc.shape[0]
    row = lambda b, c: (b * nc + c, 0)
    par = lambda shape: pl.BlockSpec(shape, lambda b, c: (0, 0))

    def body(xc_ref, dt_ref, z_ref, dtb_ref, al_ref, ds_ref, ng_ref, y_ref, prev_ref, st_ref):
        @pl.when(pl.program_id(1) == 0)
        def _():
            st_ref[...] = jnp.zeros_like(st_ref)

        prev = st_ref[...]
        prev_ref[0, 0] = prev
        y, new_state = _ssd_block(xc_ref[...], dt_ref[...], z_ref[...], prev, dtb_ref[...], al_ref[...], ds_ref[...],
                                  ng_ref[...])
        y_ref[...] = y.astype(y_ref.dtype)
        st_ref[...] = new_state

    state = (SSM_GROUPS * SSM_STATE, SSM_RPG * SSM_HEADDIM)
    return pl.pallas_call(
        body, grid=(nb, nc),
        in_specs=[pl.BlockSpec((CHUNK, SSM_CONV_DIM), row),
                  pl.BlockSpec((CHUNK, 128), lambda b, c: (b * nc + c, DT_COL // 128)),
                  pl.BlockSpec((CHUNK, SSM_INNER), lambda b, c: (b * nc + c, 1)),
                  par((1, 128)), par((1, 128)), par((1, 128)), par((1, SSM_INNER))],
        out_specs=[pl.BlockSpec((CHUNK, SSM_INNER), row), pl.BlockSpec((1, 1) + state, lambda b, c: (b, c, 0, 0))],
        out_shape=[jax.ShapeDtypeStruct((t, SSM_INNER), BF16), jax.ShapeDtypeStruct((nb, nc) + state, F32)],
        scratch_shapes=[pltpu.VMEM(state, F32)], compiler_params=_params((nb, nc)), name="ssd_fwd",
        interpret=interpret)(xc, proj, proj, dt_bias, a_log, d_skip, norm_g)


def _ssd_bwd(xc, proj, prevs, dt_bias, a_log, d_skip, norm_g, dy, nb, nc, interpret=False):
    t = xc.shape[0]
    row = lambda b, c: (b * nc + (nc - 1 - c), 0)
    par = lambda shape: pl.BlockSpec(shape, lambda b, c: (0, 0))

    def body(xc_ref, dt_ref, z_ref, prev_ref, dtb_ref, al_ref, ds_ref, ng_ref, dy_ref,
             dxc_ref, ddt_ref, dz_ref, ddtb_ref, dal_ref, dds_ref, dng_ref, dst_ref):
        @pl.when(pl.program_id(1) == 0)
        def _():
            dst_ref[...] = jnp.zeros_like(dst_ref)

        @pl.when((pl.program_id(0) == 0) & (pl.program_id(1) == 0))
        def _():
            ddtb_ref[...] = jnp.zeros_like(ddtb_ref)
            dal_ref[...] = jnp.zeros_like(dal_ref)
            dds_ref[...] = jnp.zeros_like(dds_ref)
            dng_ref[...] = jnp.zeros_like(dng_ref)

        _, vjp = jax.vjp(_ssd_block, xc_ref[...], dt_ref[...], z_ref[...], prev_ref[0, 0], dtb_ref[...], al_ref[...],
                         ds_ref[...], ng_ref[...])
        dxc, ddt, dz, dprev, ddtb, dal, dds, dng = vjp((dy_ref[...].astype(F32), dst_ref[...]))
        dxc_ref[...] = dxc
        ddt_ref[...] = ddt
        dz_ref[...] = dz
        dst_ref[...] = dprev
        ddtb_ref[...] += ddtb
        dal_ref[...] += dal
        dds_ref[...] += dds
        dng_ref[...] += dng

    state = (SSM_GROUPS * SSM_STATE, SSM_RPG * SSM_HEADDIM)
    return pl.pallas_call(
        body, grid=(nb, nc),
        in_specs=[pl.BlockSpec((CHUNK, SSM_CONV_DIM), row),
                  pl.BlockSpec((CHUNK, 128), lambda b, c: (b * nc + (nc - 1 - c), DT_COL // 128)),
                  pl.BlockSpec((CHUNK, SSM_INNER), lambda b, c: (b * nc + (nc - 1 - c), 1)),
                  pl.BlockSpec((1, 1) + state, lambda b, c: (b, nc - 1 - c, 0, 0)),
                  par((1, 128)), par((1, 128)), par((1, 128)), par((1, SSM_INNER)),
                  pl.BlockSpec((CHUNK, SSM_INNER), row)],
        out_specs=[pl.BlockSpec((CHUNK, SSM_CONV_DIM), row), pl.BlockSpec((CHUNK, 128), row),
                   pl.BlockSpec((CHUNK, SSM_INNER), row), par((1, 128)), par((1, 128)), par((1, 128)),
                   par((1, SSM_INNER))],
        out_shape=[jax.ShapeDtypeStruct((t, SSM_CONV_DIM), F32), jax.ShapeDtypeStruct((t, 128), F32),
                   jax.ShapeDtypeStruct((t, SSM_INNER), F32), jax.ShapeDtypeStruct((1, 128), F32),
                   jax.ShapeDtypeStruct((1, 128), F32), jax.ShapeDtypeStruct((1, 128), F32),
                   jax.ShapeDtypeStruct((1, SSM_INNER), F32)],
        scratch_shapes=[pltpu.VMEM(state, F32)], compiler_params=_params((nb, nc)), name="ssd_bwd",
        interpret=interpret)(xc, proj, proj, prevs, dt_bias, a_log, d_skip, norm_g, dy)


def _loss_head(y, target, interpret=False):
    t, d = y.shape
    tm = 256
    blk = pl.BlockSpec((tm, d), lambda i: (i, 0))

    def body(y_ref, t_ref, loss_ref, dy_ref):
        err = y_ref[...] - t_ref[...]
        dy_ref[...] = err * (1.0 / d)

        @pl.when(pl.program_id(0) == 0)
        def _():
            loss_ref[...] = jnp.zeros_like(loss_ref)

        loss_ref[...] += 0.5 * jnp.sum(jnp.mean(err * err, axis=-1, keepdims=True), axis=0, keepdims=True)

    return pl.pallas_call(body, grid=(t // tm,), in_specs=[blk, blk],
                          out_specs=[pl.BlockSpec((1, 1), lambda i: (0, 0)), blk],
                          out_shape=[jax.ShapeDtypeStruct((1, 1), F32), jax.ShapeDtypeStruct((t, d), F32)],
                          compiler_params=_params((1,)), name="loss_head", interpret=interpret)(y, target)


def _adamw(name, parts, w, m, v, tr, interpret=False):
    n_parts, rows, cols = parts.shape
    assert rows % tr == 0
    blk = pl.BlockSpec((tr, cols), lambda i: (i, 0))

    def body(p_ref, w_ref, m_ref, v_ref, g_out, d_out, m_out, v_out):
        g = p_ref[0].astype(F32)
        for p in range(1, n_parts):
            g = g + p_ref[p].astype(F32)
        m_new = ADAM_B1 * m_ref[...] + (1.0 - ADAM_B1) * g
        v_new = ADAM_B2 * v_ref[...] + (1.0 - ADAM_B2) * jnp.square(g)
        m_hat = m_new / (1.0 - ADAM_B1 ** ADAM_STEP)
        v_hat = v_new / (1.0 - ADAM_B2 ** ADAM_STEP)
        g_out[...] = g
        d_out[...] = -ADAM_LR * (m_hat / (jnp.sqrt(v_hat) + ADAM_EPS) + ADAM_WD * w_ref[...])
        m_out[...] = m_new
        v_out[...] = v_new

    return pl.pallas_call(body, grid=(rows // tr,),
                          in_specs=[pl.BlockSpec((n_parts, tr, cols), lambda i: (0, i, 0)), blk, blk, blk],
                          out_specs=[blk] * 4, out_shape=[jax.ShapeDtypeStruct((rows, cols), F32)] * 4,
                          compiler_params=_params((1,)), name=name, interpret=interpret)(parts, w, m, v)


def _sum_parts(name, parts, interpret=False):
    n_parts, rows, cols = parts.shape
    tr = _pick_tile(rows, (512, 256, 128, 64, 32, 16, 8))

    def body(p_ref, o_ref):
        acc = p_ref[0]
        for p in range(1, n_parts):
            acc = acc + p_ref[p]
        o_ref[...] = acc

    return pl.pallas_call(body, grid=(rows // tr,),
                          in_specs=[pl.BlockSpec((n_parts, tr, cols), lambda i: (0, i, 0))],
                          out_specs=pl.BlockSpec((tr, cols), lambda i: (i, 0)),
                          out_shape=jax.ShapeDtypeStruct((rows, cols), parts.dtype),
                          compiler_params=_params((1,)), name=name, interpret=interpret)(parts)


def _flip(coord, bit):
    return 1 - coord if bit else coord


def _all_gather(name, block):
    rows, cols = block.shape

    def body(x_ref, out_ref, send_sems, recv_sems, local_sem):
        x, y, c = lax.axis_index("x"), lax.axis_index("y"), lax.axis_index("c")
        me, sibling = (x, y, c), (x, y, 1 - c)
        chips = [(1 - x, y), (x, 1 - y), (1 - x, 1 - y)]

        def slot(px, py, pc):
            return out_ref.at[4 * px + 2 * py + pc]

        def copy(k, blk, to, src=None):
            return pltpu.make_async_remote_copy(
                src_ref=slot(*blk) if src is None else src, dst_ref=slot(*blk), send_sem=send_sems.at[k],
                recv_sem=recv_sems.at[k], device_id=to, device_id_type=pl.DeviceIdType.MESH)

        mine = pltpu.make_async_copy(x_ref, slot(*me), local_sem)
        mine.start()
        first = [copy(0, me, sibling, src=x_ref)]
        first += [copy(1 + j, me, (*chip, c), src=x_ref) for j, chip in enumerate(chips)]
        for cp in first:
            cp.start()
        passed = [copy(4 + j, (*chip, c), sibling) for j, chip in enumerate(chips)]
        for j, chip in enumerate(chips):
            copy(1 + j, (*chip, c), me).wait_recv()
            passed[j].start()
        copy(0, sibling, me).wait_recv()
        for j, chip in enumerate(chips):
            copy(4 + j, (*chip, 1 - c), me).wait_recv()
        for cp in first + passed:
            cp.wait_send()
        mine.wait()

    return pl.pallas_call(
        body, out_shape=jax.ShapeDtypeStruct((N_DEV, rows, cols), block.dtype),
        in_specs=[pl.BlockSpec(memory_space=pl.ANY)], out_specs=pl.BlockSpec(memory_space=pl.ANY),
        scratch_shapes=[pltpu.SemaphoreType.DMA((7,)), pltpu.SemaphoreType.DMA((7,)), pltpu.SemaphoreType.DMA],
        name=name)(block)


def _all_to_all(name, blocks):
    _, rows, cols = blocks.shape

    def body(x_ref, out_ref, send_sems, recv_sems, local_sem):
        x, y, c = lax.axis_index("x"), lax.axis_index("y"), lax.axis_index("c")
        mine_idx = 4 * x + 2 * y + c
        mine = pltpu.make_async_copy(x_ref.at[mine_idx], out_ref.at[mine_idx], local_sem)
        mine.start()
        copies = []
        for k in range(1, N_DEV):
            px, py, pc = _flip(x, k & 4), _flip(y, k & 2), _flip(c, k & 1)
            copies.append(pltpu.make_async_remote_copy(
                src_ref=x_ref.at[4 * px + 2 * py + pc], dst_ref=out_ref.at[mine_idx], send_sem=send_sems.at[k - 1],
                recv_sem=recv_sems.at[k - 1], device_id=(px, py, pc), device_id_type=pl.DeviceIdType.MESH))
        for cp in copies:
            cp.start()
        for cp in copies:
            cp.wait_recv()
        for cp in copies:
            cp.wait_send()
        mine.wait()

    return pl.pallas_call(
        body, out_shape=jax.ShapeDtypeStruct(blocks.shape, blocks.dtype),
        in_specs=[pl.BlockSpec(memory_space=pl.ANY)], out_specs=pl.BlockSpec(memory_space=pl.ANY),
        scratch_shapes=[pltpu.SemaphoreType.DMA((7,)), pltpu.SemaphoreType.DMA((7,)), pltpu.SemaphoreType.DMA],
        name=name)(blocks)


def _pack_w_in(w):
    pad = jnp.zeros((w.shape[0], DT_PAD - SSM_HEADS), w.dtype)
    return jnp.concatenate([w[:, :DT_COL], w[:, DT_COL:GA_COL], pad, w[:, GA_COL:]], axis=1)


def _unpack_w_in(w):
    return jnp.concatenate([w[:, :DT_COL + SSM_HEADS], w[:, P_GA:]], axis=1)


def _pad_heads(v):
    return jnp.pad(v, (0, 128 - SSM_HEADS)).reshape(1, 128)


def _local_step(x, mem, target, wts, interpret=False):
    nb, s, d = x.shape
    t = nb * s
    nc = s // CHUNK
    it = interpret
    mm = functools.partial(_matmul, interpret=it)
    h = x.reshape(t, d)
    memf = mem.reshape(nb * MEM_LEN, d)
    vec = lambda a: a.reshape(1, -1)

    full1 = lambda shape: pl.BlockSpec(shape, lambda i: (0,) * len(shape))
    full2 = lambda shape: pl.BlockSpec(shape, lambda i, j: (0,) * len(shape))
    row1 = lambda tm, w: pl.BlockSpec((tm, w), lambda i: (i, 0))

    mem_specs = [row1(256, d), full1((1, d)), full1((1, d))]
    mem_ins = [memf, vec(wts["mem_ln_g"]), vec(wts["mem_ln_b"])]
    (mem_n,) = _stage_fwd("memln_fwd", _memln_block, (nb * MEM_LEN // 256,), mem_ins, mem_specs,
                          [jax.ShapeDtypeStruct((nb * MEM_LEN, d), F32)], [row1(256, d)], interpret=it)

    saved = []
    for l in range(DEPTH):
        sv = {"h": h}
        w_p = wts["w_in"][l]
        proj = mm("mm_in", h, w_p)
        sv["proj"] = proj
        sgu_ins = [proj, proj, vec(wts["sg_ln_g"][l]), vec(wts["sg_ln_b"][l]), wts["sg_w"][l], wts["sg_b"][l].T]
        sgu_specs = [pl.BlockSpec((CHUNK, d), lambda i: (i, 0)), pl.BlockSpec((CHUNK, d), lambda i: (i, 1)),
                     full1((1, d)), full1((1, d)), full1((SG_GROUPS, CHUNK, CHUNK)), full1((CHUNK, SG_GROUPS))]
        (a_out,) = _stage_fwd("sgu_fwd", _sgu_block, (t // CHUNK,), sgu_ins, sgu_specs,
                              [jax.ShapeDtypeStruct((t, d), BF16)], [row1(CHUNK, d)], interpret=it)
        sv["sgu"] = (sgu_ins, sgu_specs)
        sv["a_out"] = a_out
        cw = 256
        conv_ins = [proj, wts["conv_w"][l], vec(wts["conv_b"][l])]
        conv_specs = [pl.BlockSpec((s, cw), lambda j, b: (b, 4096 // cw + j)), pl.BlockSpec((SSM_CONV, cw), lambda j, b: (0, j)),
                      pl.BlockSpec((1, cw), lambda j, b: (0, j))]
        conv_out_spec = pl.BlockSpec((s, cw), lambda j, b: (b, j))
        (xc,) = _stage_fwd("conv_fwd", _conv_block, (SSM_CONV_DIM // cw, nb), conv_ins, conv_specs,
                           [jax.ShapeDtypeStruct((t, SSM_CONV_DIM), F32)], [conv_out_spec], interpret=it)
        sv["conv"] = (conv_ins, conv_specs, conv_out_spec)
        ssd_par = [_pad_heads(wts["dt_bias"][l]), _pad_heads(wts["a_log"][l]), _pad_heads(wts["d_skip"][l]),
                   vec(wts["ssm_norm_g"][l])]
        y_ssd, prevs = _ssd_fwd(xc, proj, *ssd_par, nb, nc, interpret=it)
        sv["ssd"] = (xc, prevs, ssd_par)
        sv["y_ssd"] = y_ssd
        br_a = mm("mm_sq", a_out, wts["p_a"][l])
        br_b = mm("mm_pb", y_ssd, wts["p_b"][l])
        mw = 512
        merge_ins = [proj, proj, br_a, br_b]
        merge_specs = [pl.BlockSpec((512, mw), lambda i, j: (i, P_GA // mw + j)),
                       pl.BlockSpec((512, mw), lambda i, j: (i, P_GB // mw + j)),
                       pl.BlockSpec((512, mw), lambda i, j: (i, j)), pl.BlockSpec((512, mw), lambda i, j: (i, j))]
        merge_out_spec = pl.BlockSpec((512, mw), lambda i, j: (i, j))
        (merged,) = _stage_fwd("merge_fwd", _merge_block, (t // 512, d // mw), merge_ins, merge_specs,
                               [jax.ShapeDtypeStruct((t, d), BF16)], [merge_out_spec], interpret=it)
        sv["merge"] = (merge_ins, merge_specs, merge_out_spec)
        sv["merged"] = merged
        y1 = mm("mm_sq", merged, wts["w_mix_o"][l])
        ln_specs = [row1(256, d), row1(256, d), full1((1, d)), full1((1, d))]
        ln_out = [jax.ShapeDtypeStruct((t, d), F32)]
        ln1_ins = [h, y1, vec(wts["ln_g"][l, 0]), vec(wts["ln_b"][l, 0])]
        (h1,) = _stage_fwd("lnres_fwd", _lnres_block, (t // 256,), ln1_ins, ln_specs, ln_out, [row1(256, d)], interpret=it)
        sv["ln1"] = ln1_ins
        q = mm("mm_sq", h1, wts["w_xq"][l])
        kv = mm("mm_kv", mem_n, wts["w_xkv"][l])
        tq = 512
        attn_ins = [q, kv]
        attn_specs = [pl.BlockSpec((tq, d), lambda b, i: (b * (s // tq) + i, 0)),
                      pl.BlockSpec((MEM_LEN, 2 * d), lambda b, i: (b, 0))]
        attn_out_spec = pl.BlockSpec((tq, d), lambda b, i: (b * (s // tq) + i, 0))
        (o,) = _stage_fwd("attn_fwd", _attn_block, (nb, s // tq), attn_ins, attn_specs,
                          [jax.ShapeDtypeStruct((t, d), BF16)], [attn_out_spec], interpret=it)
        sv["attn"] = (attn_ins, attn_specs, attn_out_spec)
        sv["o"] = o
        y2 = mm("mm_sq", o, wts["w_xo"][l])
        ln2_ins = [h1, y2, vec(wts["ln_g"][l, 1]), vec(wts["ln_b"][l, 1])]
        (h2,) = _stage_fwd("lnres_fwd", _lnres_block, (t // 256,), ln2_ins, ln_specs, ln_out, [row1(256, d)], interpret=it)
        sv["ln2"] = ln2_ins
        gu = mm("mm_ffn_in", h2, wts["w_ffn_in"][l])
        sw_specs = [row1(128, 2 * FFN_HIDDEN)]
        (act,) = _stage_fwd("swiglu_fwd", _swiglu_block, (t // 128,), [gu], sw_specs,
                            [jax.ShapeDtypeStruct((t, FFN_HIDDEN), BF16)], [row1(128, FFN_HIDDEN)], interpret=it)
        sv["gu"] = gu
        sv["act"] = act
        y3 = mm("mm_ffn_out", act, wts["w_ffn_out"][l])
        ln3_ins = [h2, y3, vec(wts["ln_g"][l, 2]), vec(wts["ln_b"][l, 2])]
        (h,) = _stage_fwd("lnres_fwd", _lnres_block, (t // 256,), ln3_ins, ln_specs, ln_out, [row1(256, d)], interpret=it)
        sv["ln3"] = ln3_ins
        saved.append(sv)

    loss, dh = _loss_head(h, target.reshape(t, d), interpret=it)
    dh_parts = (dh,)

    grads = {n: [None] * DEPTH for n in WEIGHTS if n not in ("mem_ln_g", "mem_ln_b")}
    dmem_n = []
    ln_grads = [(0, (), F32), (1, (), F32), (2, (0,), F32), (3, (0,), F32)]
    ln_specs = [row1(256, d), row1(256, d), full1((1, d)), full1((1, d))]
    for l in reversed(range(DEPTH)):
        sv = saved[l]
        dln_g, dln_b = [None] * 3, [None] * 3
        dres, dy3, dln_g[2], dln_b[2] = _stage_bwd("lnres_bwd", _lnres_block, (t // 256,), sv["ln3"], ln_specs,
                                                   [dh_parts], [row1(256, d)], ln_grads, interpret=it)
        grads["w_ffn_out"][l] = mm("mm_ffn_out_dw", sv["act"], dy3, ta=True, out_dtype=BF16)
        dact = mm("mm_ffn_out_dx", dy3, wts["w_ffn_out"][l], tb=True)
        (dgu,) = _stage_bwd("swiglu_bwd", _swiglu_block, (t // 128,), [sv["gu"]], [row1(128, 2 * FFN_HIDDEN)],
                            [(dact,)], [row1(128, FFN_HIDDEN)], [(0, (), F32)], interpret=it)
        h2 = sv["ln3"][0]
        grads["w_ffn_in"][l] = mm("mm_ffn_in_dw", h2, dgu, ta=True, out_dtype=BF16)
        dh2 = mm("mm_ffn_in_dx", dgu, wts["w_ffn_in"][l], tb=True, add=dres)
        dres, dy2, dln_g[1], dln_b[1] = _stage_bwd("lnres_bwd", _lnres_block, (t // 256,), sv["ln2"], ln_specs,
                                                   [(dh2,)], [row1(256, d)], ln_grads, interpret=it)
        grads["w_xo"][l] = mm("mm_sq_dw", sv["o"], dy2, ta=True, out_dtype=BF16)
        do = mm("mm_sq_dx", dy2, wts["w_xo"][l], tb=True)
        attn_ins, attn_specs, attn_out_spec = sv["attn"]
        dq, dkv = _stage_bwd("attn_bwd", _attn_block, (nb, s // 512), attn_ins, attn_specs, [(do,)], [attn_out_spec],
                             [(0, (), F32), (1, (1,), F32)], interpret=it)
        h1 = sv["ln2"][0]
        grads["w_xq"][l] = mm("mm_sq_dw", h1, dq, ta=True, out_dtype=BF16)
        dh1 = mm("mm_sq_dx", dq, wts["w_xq"][l], tb=True, add=dres)
        grads["w_xkv"][l] = mm("mm_kv_dw", mem_n, dkv, ta=True, out_dtype=BF16)
        dmem_n.append(mm("mm_kv_dx", dkv, wts["w_xkv"][l], tb=True))
        dres, dy1, dln_g[0], dln_b[0] = _stage_bwd("lnres_bwd", _lnres_block, (t // 256,), sv["ln1"], ln_specs,
                                                   [(dh1,)], [row1(256, d)], ln_grads, interpret=it)
        grads["ln_g"][l] = jnp.concatenate(dln_g, axis=0)
        grads["ln_b"][l] = jnp.concatenate(dln_b, axis=0)
        grads["w_mix_o"][l] = mm("mm_sq_dw", sv["merged"], dy1, ta=True, out_dtype=BF16)
        dmerged = mm("mm_sq_dx", dy1, wts["w_mix_o"][l], tb=True)
        merge_ins, merge_specs, merge_out_spec = sv["merge"]
        dga, dgb, dbr_a, dbr_b = _stage_bwd("merge_bwd", _merge_block, (t // 512, d // 512), merge_ins, merge_specs,
                                            [(dmerged,)], [merge_out_spec],
                                            [(i, (), F32, ((t, d), merge_out_spec)) for i in range(4)], interpret=it)
        grads["p_a"][l] = mm("mm_sq_dw", sv["a_out"], dbr_a, ta=True, out_dtype=BF16)
        da_out = mm("mm_sq_dx", dbr_a, wts["p_a"][l], tb=True)
        grads["p_b"][l] = mm("mm_pb_dw", sv["y_ssd"], dbr_b, ta=True, out_dtype=BF16)
        dy_ssd = mm("mm_pb_dx", dbr_b, wts["p_b"][l], tb=True)
        sgu_ins, sgu_specs = sv["sgu"]
        du, dv, dsg_ln_g, dsg_ln_b, dsg_w, dsg_b = _stage_bwd(
            "sgu_bwd", _sgu_block, (t // CHUNK,), sgu_ins, sgu_specs, [(da_out,)], [row1(CHUNK, d)],
            [(0, (), F32, ((t, d), row1(CHUNK, d))), (1, (), F32, ((t, d), row1(CHUNK, d))), (2, (0,), F32), (3, (0,), F32),
             (4, (0,), F32), (5, (0,), F32)], interpret=it)
        grads["sg_ln_g"][l], grads["sg_ln_b"][l], grads["sg_w"][l], grads["sg_b"][l] = dsg_ln_g[0], dsg_ln_b[0], dsg_w, dsg_b.T
        xc, prevs, ssd_par = sv["ssd"]
        dxc, ddt, dz, ddtb, dal, dds, dng = _ssd_bwd(xc, sv["proj"], prevs, *ssd_par, dy_ssd, nb, nc, interpret=it)
        grads["dt_bias"][l], grads["a_log"][l], grads["d_skip"][l] = ddtb[0, :SSM_HEADS], dal[0, :SSM_HEADS], dds[0, :SSM_HEADS]
        grads["ssm_norm_g"][l] = dng[0]
        conv_ins, conv_specs, conv_out_spec = sv["conv"]
        dxbc, dconv_w, dconv_b = _stage_bwd("conv_bwd", _conv_block, (SSM_CONV_DIM // 256, nb), conv_ins, conv_specs,
                                            [(dxc,)], [conv_out_spec],
                                            [(0, (), F32, ((t, SSM_CONV_DIM), conv_out_spec)), (1, (1,), F32), (2, (1,), F32)],
                                            interpret=it)
        grads["conv_w"][l], grads["conv_b"][l] = dconv_w, dconv_b[0]
        ddt = jnp.pad(ddt, ((0, 0), (0, DT_PAD - 128)))
        dproj = jnp.concatenate([du, dv, dz, dxbc, ddt, dga, dgb], axis=1)
        grads["w_in"][l] = mm("mm_in_dw", sv["h"], dproj, ta=True, out_dtype=BF16)
        dh = mm("mm_in_dx", dproj, wts["w_in"][l], tb=True, add=dres)
        dh_parts = (dh,)

    grad_x = dh.reshape(nb, s, d)
    dmg, dmb = _stage_bwd("memln_bwd", _memln_block, (nb * MEM_LEN // 256,), mem_ins, mem_specs, [tuple(dmem_n)],
                          [row1(256, d)], [(1, (0,), F32), (2, (0,), F32)], interpret=it)
    out = {n: jnp.stack(g, axis=0) for n, g in grads.items()}
    out["mem_ln_g"], out["mem_ln_b"] = dmg[0], dmb[0]
    return loss, grad_x, out


def _pack_rows(arrays, cols, pad_to):
    flat = jnp.concatenate([a.reshape(-1, cols) for a in arrays], axis=0)
    return jnp.pad(flat, ((0, pad_to - flat.shape[0]), (0, 0)))


def _pack_rows_leading(arrays, cols, pad_to):
    flat = jnp.concatenate([a.reshape(N_DEV, -1, cols) for a in arrays], axis=1)
    return jnp.pad(flat, ((0, 0), (0, pad_to - flat.shape[1]), (0, 0)))


def _unpack_rows(packed, shapes, cols):
    out, r = [], 0
    lead = packed.shape[:-2]
    for shape in shapes:
        n = math.prod(shape) // cols
        out.append(packed[..., r:r + n, :].reshape(lead + tuple(shape)))
        r += n
    return out


def _pack_flat(arrays, rows):
    flat = jnp.concatenate([a.reshape(-1) for a in arrays])
    return jnp.pad(flat, (0, rows * 128 - flat.shape[0])).reshape(rows, 128)


def _unpack_flat(packed, shapes):
    lead = packed.shape[:-2]
    flat = packed.reshape(lead + (-1,))
    out, pos = [], 0
    for shape in shapes:
        n = math.prod(shape)
        out.append(flat[..., pos:pos + n].reshape(lead + tuple(shape)))
        pos += n
    return out


def _gather_sharded(stacked, col_sharded):
    _, dep, a, b = stacked.shape
    if col_sharded:
        return stacked.transpose(1, 2, 0, 3).reshape(dep, a, N_DEV * b)
    return stacked.transpose(1, 0, 2, 3).reshape(dep, N_DEV * a, b)


def _scatter_sharded(full, col_sharded):
    dep, a, b = full.shape
    if col_sharded:
        return full.reshape(dep, a, N_DEV, b // N_DEV).transpose(2, 0, 1, 3)
    return full.reshape(dep, N_DEV, a // N_DEV, b).transpose(1, 0, 2, 3)


def _small_pad(n_elems):
    return -(-n_elems // (128 * SMALL_ROW_TILE)) * SMALL_ROW_TILE


def kernel(x, mem, mem_ln_g, mem_ln_b, w_in, sg_ln_g, sg_ln_b, sg_w, sg_b, conv_w, conv_b, dt_bias, a_log, d_skip, ssm_norm_g, p_a, p_b, w_mix_o, w_xq, w_xkv, w_xo, w_ffn_in, w_ffn_out, ln_g, ln_b, loss_target, m_mem_ln_g, m_mem_ln_b, m_w_in, m_sg_ln_g, m_sg_ln_b, m_sg_w, m_sg_b, m_conv_w, m_conv_b, m_dt_bias, m_a_log, m_d_skip, m_ssm_norm_g, m_p_a, m_p_b, m_w_mix_o, m_w_xq, m_w_xkv, m_w_xo, m_w_ffn_in, m_w_ffn_out, m_ln_g, m_ln_b, v_mem_ln_g, v_mem_ln_b, v_w_in, v_sg_ln_g, v_sg_ln_b, v_sg_w, v_sg_b, v_conv_w, v_conv_b, v_dt_bias, v_a_log, v_d_skip, v_ssm_norm_g, v_p_a, v_p_b, v_w_mix_o, v_w_xq, v_w_xkv, v_w_xo, v_w_ffn_in, v_w_ffn_out, v_ln_g, v_ln_b):
    args = dict(locals())
    w = {n: args[n] for n in WEIGHTS}
    m = {n: args["m_" + n] for n in WEIGHTS}
    v = {n: args["v_" + n] for n in WEIGHTS}
    me = 4 * lax.axis_index("x") + 2 * lax.axis_index("y") + lax.axis_index("c")

    big_shapes = [w[n].shape for n in BIG]
    gathered = _all_gather("gather_weights", _pack_rows([w[n].astype(BF16) for n in BIG], D_MODEL, BIG_ROWS_PAD))
    full = {}
    for n, shards in zip(BIG, _unpack_rows(gathered, big_shapes, D_MODEL)):
        full[n] = _gather_sharded(shards, n in BIG_COL_SHARDED)
    full["w_in"] = jnp.stack([_pack_w_in(full["w_in"][l]) for l in range(DEPTH)], axis=0)
    sh_shapes = [w[n].shape for n in SMALL_SH]
    sh_rows = _small_pad(sum(math.prod(s) for s in sh_shapes))
    gathered_small = _all_gather("gather_small", _pack_flat([w[n] for n in SMALL_SH], sh_rows))
    for n, shards in zip(SMALL_SH, _unpack_flat(gathered_small, sh_shapes)):
        full[n] = _gather_sharded(shards, True)
    for n in SMALL_REP:
        full[n] = w[n]

    loss, grad_x, g_full = _local_step(x, mem, loss_target, full)
    loss = lax.psum(loss[0, 0], ("x", "y", "c"))

    g_full["w_in"] = jnp.stack([_unpack_w_in(g_full["w_in"][l]) for l in range(DEPTH)], axis=0)
    to_send = _pack_rows_leading([_scatter_sharded(g_full[n], n in BIG_COL_SHARDED) for n in BIG], D_MODEL, BIG_ROWS_PAD)
    received = _all_to_all("scatter_grads", to_send)
    small_names = SMALL_REP + SMALL_SH
    small_shapes = [g_full[n].shape for n in small_names]
    small_rows = _small_pad(sum(math.prod(s) for s in small_shapes))
    small_all = _all_gather("gather_small_grads", _pack_flat([g_full[n] for n in small_names], small_rows))
    small_sum = _sum_parts("sum_small_grads", small_all)
    g_small = dict(zip(small_names, _unpack_flat(small_sum, small_shapes)))
    for n in SMALL_SH:
        width = w[n].shape[-1]
        g_small[n] = lax.dynamic_slice_in_dim(g_small[n], me * width, width, axis=-1)

    pack_big = lambda d: _pack_rows([d[n] for n in BIG], D_MODEL, BIG_ROWS_PAD)
    big_out = _adamw("adamw_big", received, pack_big(w), pack_big(m), pack_big(v), 256)
    small_own_shapes = [w[n].shape for n in small_names]
    own_rows = _small_pad(sum(math.prod(s) for s in small_own_shapes))
    pack_small = lambda d: _pack_flat([d[n] for n in small_names], own_rows)
    small_out = _adamw("adamw_small", pack_small(g_small)[None], pack_small(w), pack_small(m), pack_small(v), SMALL_ROW_TILE)

    results = []
    for big_packed, small_packed in zip(big_out, small_out):
        vals = dict(zip(BIG, _unpack_rows(big_packed, big_shapes, D_MODEL)))
        vals.update(zip(small_names, _unpack_flat(small_packed, small_own_shapes)))
        results.extend(vals[n] for n in WEIGHTS)
    return (loss, grad_x, *results)
```

```python
import functools
import math

import jax
import jax.numpy as jnp
from jax import lax
from jax.experimental import pallas as pl
from jax.experimental.pallas import tpu as pltpu

F32 = jnp.float32
BF16 = jnp.bfloat16
HIGHEST = lax.Precision.HIGHEST

N_DEV = 8
D_MODEL = 1024
DEPTH = 2
MEM_LEN = 256
CHUNK = 128
SG_GROUPS = 8
SSM_INNER = 2048
SSM_HEADDIM = 64
SSM_HEADS = 32
SSM_STATE = 128
SSM_GROUPS = 4
SSM_RPG = 8
SSM_CONV = 4
SSM_CONV_DIM = 3072
X_HEADS = 4
X_HEADDIM = 256
FFN_HIDDEN = 2816
ALPHA = float((2 * DEPTH) ** 0.25)
LN_EPS = 1e-5
RMS_EPS = 1e-5
XBC_COL = 4096
DT_COL = 7168
GA_COL = 7200
DT_PAD = 512
P_GA = DT_COL + DT_PAD
P_GB = P_GA + D_MODEL

ADAM_LR = 0.001
ADAM_B1 = 0.9
ADAM_B2 = 0.999
ADAM_EPS = 1e-08
ADAM_WD = 0.01
ADAM_STEP = 10

VMEM_LIMIT = 48 * 1024 * 1024
SMALL_ROW_TILE = 256

BIG = ("w_in", "p_a", "p_b", "w_mix_o", "w_xq", "w_xkv", "w_xo", "w_ffn_in", "w_ffn_out")
BIG_COL_SHARDED = ("w_in", "w_xkv", "w_ffn_in")
SMALL_REP = ("mem_ln_g", "mem_ln_b", "sg_ln_g", "sg_ln_b", "sg_w", "sg_b", "conv_b", "dt_bias", "a_log", "d_skip",
             "ssm_norm_g")
SMALL_SH = ("conv_w", "ln_g", "ln_b")
WEIGHTS = ("mem_ln_g", "mem_ln_b", "w_in", "sg_ln_g", "sg_ln_b", "sg_w", "sg_b", "conv_w", "conv_b", "dt_bias", "a_log",
           "d_skip", "ssm_norm_g", "p_a", "p_b", "w_mix_o", "w_xq", "w_xkv", "w_xo", "w_ffn_in", "w_ffn_out", "ln_g", "ln_b")

REST = ("p_a", "p_b", "w_mix_o", "w_xq", "w_xkv", "w_xo", "w_ffn_in", "w_ffn_out")
GATHER_PLAN = {
    ("mm_in", 0): [(n, 0) for n in REST],
    ("ssd_fwd", 0): [("w_in", 1)],
    ("mm_ffn_in", 0): [(n, 1) for n in ("p_a", "p_b", "w_mix_o", "w_xq", "w_xkv", "w_xo")],
    ("mm_in", 1): [("w_ffn_in", 1), ("w_ffn_out", 1)],
}
SCATTER_PLAN = {
    ("ssd_bwd", 1): [(n, 1) for n in REST],
    ("mm_in_dx", 1): [("w_in", 1)],
    ("ssd_bwd", 0): [(n, 0) for n in REST],
    ("mm_in_dx", 0): [("w_in", 0)],
}


def _layer_norm(x, g, b):
    mu = jnp.mean(x, axis=-1, keepdims=True)
    xc = x - mu
    var = jnp.mean(xc * xc, axis=-1, keepdims=True)
    return xc * lax.rsqrt(var + LN_EPS) * g + b


def _gelu(x):
    return 0.5 * x * (1.0 + lax.erf(x * (1.0 / math.sqrt(2.0))))


def _silu(x):
    return x * jax.nn.sigmoid(x)


def _softplus(x):
    return jnp.maximum(x, 0.0) + jnp.log1p(jnp.exp(-jnp.abs(x)))


def _causal_mask():
    r = lax.broadcasted_iota(jnp.int32, (CHUNK, CHUNK), 0)
    c = lax.broadcasted_iota(jnp.int32, (CHUNK, CHUNK), 1)
    return r >= c


def _sgu_block(u, v, ln_g, ln_b, w, sb):
    gu = _gelu(u)
    vn = _layer_norm(_gelu(v), ln_g, ln_b)
    causal = _causal_mask()
    width = D_MODEL // SG_GROUPS
    outs = []
    for g in range(SG_GROUPS):
        wg = jnp.where(causal, w[g], 0.0).astype(BF16)
        mixed = jnp.dot(wg, vn[:, g * width:(g + 1) * width].astype(BF16), preferred_element_type=F32)
        outs.append(mixed + sb[:, g:g + 1])
    return (gu * jnp.concatenate(outs, axis=1),)


def _ssd_block(xc, dtraw, z, prev, dt_bias, a_log, d_skip, norm_g):
    prev = [prev[g * SSM_STATE:(g + 1) * SSM_STATE, :] for g in range(SSM_GROUPS)]
    xs = xc[:, :SSM_INNER]
    bm = xc[:, SSM_INNER:SSM_INNER + SSM_GROUPS * SSM_STATE]
    cm = xc[:, SSM_INNER + SSM_GROUPS * SSM_STATE:]
    dt = _softplus(dtraw + dt_bias)
    da = dt * (-jnp.exp(a_log))
    causal = _causal_mask()
    tril = jnp.where(causal, 1.0, 0.0).astype(F32)
    cs = jnp.dot(tril, da, precision=HIGHEST, preferred_element_type=F32)
    cs_t = cs.T
    cs_last = cs[CHUNK - 1:CHUNK, :]
    decay_states = jnp.exp(cs_last - cs)
    decay_in = jnp.exp(cs)
    chunk_decay = jnp.exp(cs_last)
    gw = SSM_RPG * SSM_HEADDIM
    ys, new_states = [], []
    for g in range(SSM_GROUPS):
        bg = bm[:, g * SSM_STATE:(g + 1) * SSM_STATE].astype(BF16)
        cg = cm[:, g * SSM_STATE:(g + 1) * SSM_STATE].astype(BF16)
        cb = lax.dot_general(cg, bg, (((1,), (1,)), ((), ())), preferred_element_type=F32)
        y_off = jnp.dot(cg, prev[g].astype(BF16), preferred_element_type=F32)
        y_heads, xdt_heads, dec_heads = [], [], []
        for r in range(SSM_RPG):
            h = g * SSM_RPG + r
            xh = xs[:, h * SSM_HEADDIM:(h + 1) * SSM_HEADDIM]
            xdt = xh * dt[:, h:h + 1]
            seg = cs[:, h:h + 1] - cs_t[h:h + 1, :]
            decay = jnp.exp(jnp.where(causal, seg, -1e30))
            y_diag = jnp.dot((cb * decay).astype(BF16), xdt.astype(BF16), preferred_element_type=F32)
            y_in = y_off[:, r * SSM_HEADDIM:(r + 1) * SSM_HEADDIM] * decay_in[:, h:h + 1]
            y_heads.append(y_diag + y_in + xh * d_skip[:, h:h + 1])
            xdt_heads.append(xdt * decay_states[:, h:h + 1])
            dec_heads.append(jnp.broadcast_to(chunk_decay[:, h:h + 1], (1, SSM_HEADDIM)))
        xdt_s = jnp.concatenate(xdt_heads, axis=1).astype(BF16)
        st = lax.dot_general(bg, xdt_s, (((0,), (0,)), ((), ())), preferred_element_type=F32)
        new_states.append(prev[g] * jnp.concatenate(dec_heads, axis=1) + st)
        y = jnp.concatenate(y_heads, axis=1) * _silu(z[:, g * gw:(g + 1) * gw])
        ys.append(y * lax.rsqrt(jnp.mean(y * y, axis=-1, keepdims=True) + RMS_EPS))
    return jnp.concatenate(ys, axis=1) * norm_g, jnp.concatenate(new_states, axis=0)


def _conv_block(x, w, b):
    rows = lax.broadcasted_iota(jnp.int32, x.shape, 0)
    acc = x * w[SSM_CONV - 1:SSM_CONV, :] + b
    for k in range(SSM_CONV - 1):
        shift = SSM_CONV - 1 - k
        acc = acc + _shift_rows(x, rows, shift) * w[k:k + 1, :]
    return (_silu(acc),)


@functools.partial(jax.custom_vjp, nondiff_argnums=(2,))
def _shift_rows(x, rows, shift):
    return jnp.where(rows >= shift, pltpu.roll(x, shift, 0), 0.0)


def _shift_rows_fwd(x, rows, shift):
    return _shift_rows(x, rows, shift), rows


def _shift_rows_bwd(shift, rows, g):
    n = g.shape[0]
    return jnp.where(rows < n - shift, pltpu.roll(g, n - shift, 0), 0.0), None


_shift_rows.defvjp(_shift_rows_fwd, _shift_rows_bwd)


def _merge_block(ga, gb, br_a, br_b):
    return (jax.nn.sigmoid(ga) * br_a + jax.nn.sigmoid(gb) * br_b,)


def _lnres_block(x, y, g, b):
    return (_layer_norm(ALPHA * x + y, g, b),)


def _lnres_block_twice(x, y, g, b):
    out = _layer_norm(ALPHA * x + y, g, b)
    return out, out


def _memln_block(x, g, b):
    return (_layer_norm(x, g, b),)


def _attn_block(q, kv):
    outs = []
    for h in range(X_HEADS):
        qh = q[:, h * X_HEADDIM:(h + 1) * X_HEADDIM].astype(BF16)
        kh = kv[:, h * X_HEADDIM:(h + 1) * X_HEADDIM].astype(BF16)
        vh = kv[:, D_MODEL + h * X_HEADDIM:D_MODEL + (h + 1) * X_HEADDIM].astype(BF16)
        s = lax.dot_general(qh, kh, (((1,), (1,)), ((), ())), preferred_element_type=F32) * (X_HEADDIM ** -0.5)
        s = s - lax.stop_gradient(jnp.max(s, axis=-1, keepdims=True))
        e = jnp.exp(s)
        p = e / jnp.sum(e, axis=-1, keepdims=True)
        outs.append(jnp.dot(p.astype(BF16), vh, preferred_element_type=F32))
    return (jnp.concatenate(outs, axis=1),)


def _swiglu_block(gu):
    return (_silu(gu[:, :FFN_HIDDEN]) * gu[:, FFN_HIDDEN:],)


class _Comm:
    def __init__(self):
        self.gathers = []
        self.scatters = []

    def operands(self):
        ins = [a for a, _ in self.gathers] + list(self.scatters)
        shapes = [jax.ShapeDtypeStruct((N_DEV,) + (a.shape if idx is None else a.shape[1:]), a.dtype)
                  for a, idx in self.gathers]
        shapes += [jax.ShapeDtypeStruct(a.shape, a.dtype) for a in self.scatters]
        scratch = []
        for n in (len(self.gathers), len(self.scatters)):
            if n:
                scratch += [pltpu.SemaphoreType.DMA((7 * n,)), pltpu.SemaphoreType.DMA((7 * n,)),
                            pltpu.SemaphoreType.DMA((n,))]
        return ins, shapes, scratch

    def _split(self, in_refs, out_refs, sems):
        ng = len(self.gathers)
        g_sems = sems[:3] if ng else None
        s_sems = sems[3:] if ng else sems
        return in_refs[:ng], in_refs[ng:], out_refs[:ng], out_refs[ng:], g_sems, s_sems

    def _gather_copies(self, i, src_ref, out_ref, sems):
        send_sems, recv_sems, local_sems = sems
        x, y, c = lax.axis_index("x"), lax.axis_index("y"), lax.axis_index("c")
        me, sibling = (x, y, c), (x, y, 1 - c)
        chips = [(1 - x, y), (x, 1 - y), (1 - x, 1 - y)]
        idx = self.gathers[i][1]
        src = src_ref if idx is None else src_ref.at[idx]

        def slot(px, py, pc):
            return out_ref.at[4 * px + 2 * py + pc]

        def copy(k, blk, to, from_src=False):
            return pltpu.make_async_remote_copy(
                src_ref=src if from_src else slot(*blk), dst_ref=slot(*blk), send_sem=send_sems.at[7 * i + k],
                recv_sem=recv_sems.at[7 * i + k], device_id=to, device_id_type=pl.DeviceIdType.MESH)

        mine = pltpu.make_async_copy(src, slot(*me), local_sems.at[i])
        first = [copy(0, me, sibling, True)] + [copy(1 + j, me, (*chip, c), True) for j, chip in enumerate(chips)]
        passed = [copy(4 + j, (*chip, c), sibling) for j, chip in enumerate(chips)]
        arrivals = [copy(1 + j, (*chip, c), me) for j, chip in enumerate(chips)]
        from_sibling = [copy(0, sibling, me)] + [copy(4 + j, (*chip, 1 - c), me) for j, chip in enumerate(chips)]
        return mine, first, passed, arrivals, from_sibling

    def _scatter_copies(self, i, src_ref, out_ref, sems):
        send_sems, recv_sems, local_sems = sems
        x, y, c = lax.axis_index("x"), lax.axis_index("y"), lax.axis_index("c")
        me = 4 * x + 2 * y + c
        mine = pltpu.make_async_copy(src_ref.at[me], out_ref.at[me], local_sems.at[i])
        copies = []
        for k in range(1, N_DEV):
            px = 1 - x if k & 4 else x
            py = 1 - y if k & 2 else y
            pc = 1 - c if k & 1 else c
            copies.append(pltpu.make_async_remote_copy(
                src_ref=src_ref.at[4 * px + 2 * py + pc], dst_ref=out_ref.at[me], send_sem=send_sems.at[7 * i + k - 1],
                recv_sem=recv_sems.at[7 * i + k - 1], device_id=(px, py, pc), device_id_type=pl.DeviceIdType.MESH))
        return mine, copies

    def start(self, in_refs, out_refs, sems):
        g_in, s_in, g_out, s_out, g_sems, s_sems = self._split(in_refs, out_refs, sems)
        for i in range(len(self.gathers)):
            mine, first, _, _, _ = self._gather_copies(i, g_in[i], g_out[i], g_sems)
            mine.start()
            for cp in first:
                cp.start()
        for i in range(len(self.scatters)):
            mine, copies = self._scatter_copies(i, s_in[i], s_out[i], s_sems)
            mine.start()
            for cp in copies:
                cp.start()

    def finish(self, in_refs, out_refs, sems):
        g_in, s_in, g_out, s_out, g_sems, s_sems = self._split(in_refs, out_refs, sems)
        parts = [self._gather_copies(i, g_in[i], g_out[i], g_sems) for i in range(len(self.gathers))]
        for j in range(3):
            for _, _, passed, arrivals, _ in parts:
                arrivals[j].wait_recv()
                passed[j].start()
        for mine, first, passed, _, from_sibling in parts:
            for cp in from_sibling:
                cp.wait_recv()
            for cp in first + passed:
                cp.wait_send()
            mine.wait()
        for i in range(len(self.scatters)):
            mine, copies = self._scatter_copies(i, s_in[i], s_out[i], s_sems)
            for cp in copies:
                cp.wait_recv()
            for cp in copies:
                cp.wait_send()
            mine.wait()


def _params(grid):
    return pltpu.CompilerParams(dimension_semantics=("arbitrary",) * len(grid), vmem_limit_bytes=VMEM_LIMIT)


def _call(name, body, *, grid, ins, in_specs, out_shape, out_specs, scratch=(), comm=None):
    n_in, n_out, n_scr = len(ins), len(out_shape), len(scratch)
    if comm is None:
        outs = pl.pallas_call(body, grid=grid, in_specs=list(in_specs), out_specs=list(out_specs),
                              out_shape=list(out_shape), scratch_shapes=list(scratch), name=name,
                              compiler_params=_params(grid))(*ins)
        return list(outs), []
    c_ins, c_shapes, c_scratch = comm.operands()
    nci, nco = len(c_ins), len(c_shapes)
    anywhere = pl.BlockSpec(memory_space=pl.ANY)

    def carrier(*refs):
        main_in, comm_in = refs[:n_in], refs[n_in:n_in + nci]
        o0 = n_in + nci
        main_out, comm_out = refs[o0:o0 + n_out], refs[o0 + n_out:o0 + n_out + nco]
        s0 = o0 + n_out + nco
        main_scr, comm_scr = refs[s0:s0 + n_scr], refs[s0 + n_scr:]
        first = pl.program_id(0) == 0
        last = pl.program_id(0) == grid[0] - 1
        for ax in range(1, len(grid)):
            first = first & (pl.program_id(ax) == 0)
            last = last & (pl.program_id(ax) == grid[ax] - 1)

        @pl.when(first)
        def _():
            comm.start(comm_in, comm_out, comm_scr)

        body(*main_in, *main_out, *main_scr)

        @pl.when(last)
        def _():
            comm.finish(comm_in, comm_out, comm_scr)

    outs = pl.pallas_call(carrier, grid=grid, in_specs=list(in_specs) + [anywhere] * nci,
                          out_specs=list(out_specs) + [anywhere] * nco, out_shape=list(out_shape) + c_shapes,
                          scratch_shapes=list(scratch) + c_scratch, name=name, compiler_params=_params(grid))(*ins, *c_ins)
    return list(outs[:n_out]), list(outs[n_out:])


def _comm_only(name, comm):
    c_ins, c_shapes, c_scratch = comm.operands()
    nci, nco = len(c_ins), len(c_shapes)
    anywhere = pl.BlockSpec(memory_space=pl.ANY)

    def body(*refs):
        comm.start(refs[:nci], refs[nci:nci + nco], refs[nci + nco:])
        comm.finish(refs[:nci], refs[nci:nci + nco], refs[nci + nco:])

    return list(pl.pallas_call(body, in_specs=[anywhere] * nci, out_specs=[anywhere] * nco, out_shape=c_shapes,
                               scratch_shapes=c_scratch, name=name)(*c_ins))


def _stage_fwd(name, f, grid, ins, in_specs, out_shapes, out_specs):
    n_in = len(ins)

    def body(*refs):
        res = f(*[r[...] for r in refs[:n_in]])
        for o_ref, val in zip(refs[n_in:], res):
            o_ref[...] = val.astype(o_ref.dtype)

    return _call(name, body, grid=grid, ins=ins, in_specs=in_specs, out_shape=out_shapes, out_specs=out_specs)[0]


def _stage_bwd(name, f, grid, ins, in_specs, cts, ct_specs, grads):
    n_in = len(ins)
    flat_cts = [c for group in cts for c in group]
    flat_ct_specs = [s for group, spec in zip(cts, ct_specs) for s in (spec,) * len(group)]
    n_ct = len(flat_cts)
    diff = [g[0] for g in grads]

    def body(*refs):
        vals = [r[...] for r in refs[:n_in]]
        ct_refs = refs[n_in:n_in + n_ct]
        g_refs = refs[n_in + n_ct:]
        ct_vals, pos = [], 0
        for group in cts:
            acc = ct_refs[pos][...].astype(F32)
            for j in range(1, len(group)):
                acc = acc + ct_refs[pos + j][...].astype(F32)
            ct_vals.append(acc)
            pos += len(group)

        def g_fn(*dvals):
            full = list(vals)
            for i, dv in zip(diff, dvals):
                full[i] = dv
            return f(*full)

        _, vjp = jax.vjp(g_fn, *[vals[i] for i in diff])
        gvals = vjp(tuple(ct_vals))
        for gspec, g_ref, gval in zip(grads, g_refs, gvals):
            acc_axes = gspec[1]
            if not acc_axes:
                g_ref[...] = gval.astype(g_ref.dtype)
            else:
                first = pl.program_id(acc_axes[0]) == 0
                for ax in acc_axes[1:]:
                    first = first & (pl.program_id(ax) == 0)

                @pl.when(first)
                def _():
                    g_ref[...] = jnp.zeros_like(g_ref)

                g_ref[...] += gval.astype(g_ref.dtype)

    out_shapes, out_specs = [], []
    for gspec in grads:
        shape, spec = gspec[3] if len(gspec) > 3 else (ins[gspec[0]].shape, in_specs[gspec[0]])
        out_shapes.append(jax.ShapeDtypeStruct(shape, gspec[2]))
        out_specs.append(spec)
    return _call(name, body, grid=grid, ins=list(ins) + flat_cts, in_specs=list(in_specs) + flat_ct_specs,
                 out_shape=out_shapes, out_specs=out_specs)[0]


def _pick_tile(n, candidates):
    for c in candidates:
        if n % c == 0:
            return c
    return n


def _matmul(name, a, b, *, ta=False, tb=False, add=None, out_dtype=F32, comm=None):
    if ta:
        k_dim, m = a.shape
    else:
        m, k_dim = a.shape
    n = b.shape[0] if tb else b.shape[1]
    assert (b.shape[1] if tb else b.shape[0]) == k_dim and not (ta and tb)
    tm = _pick_tile(m, (1024, 512, 256, 128))
    tn = _pick_tile(n, (1024, 1408, 512, 256, 128))
    if ta:
        tk = _pick_tile(k_dim, (1024, 512, 256, 128))
    elif k_dim <= 2816:
        tk = k_dim
    else:
        tk = _pick_tile(k_dim, (1408, 512, 256, 128))
    nk = k_dim // tk
    grid = (m // tm, n // tn, nk)
    a_spec = pl.BlockSpec((tk, tm), lambda i, j, k: (k, i)) if ta else pl.BlockSpec((tm, tk), lambda i, j, k: (i, k))
    b_spec = pl.BlockSpec((tn, tk), lambda i, j, k: (j, k)) if tb else pl.BlockSpec((tk, tn), lambda i, j, k: (k, j))
    o_spec = pl.BlockSpec((tm, tn), lambda i, j, k: (i, j))
    dims = (((0 if ta else 1,), (1 if tb else 0,)), ((), ()))
    has_add = add is not None

    def body(*refs):
        a_ref, b_ref = refs[0], refs[1]
        add_ref = refs[2] if has_add else None
        o_ref, acc_ref = refs[-2], refs[-1]
        k = pl.program_id(2)
        part = lax.dot_general(a_ref[...].astype(BF16), b_ref[...].astype(BF16), dims, preferred_element_type=F32)

        def finish(res):
            if has_add:
                res = res + add_ref[...].astype(F32)
            o_ref[...] = res.astype(o_ref.dtype)

        if nk == 1:
            finish(part)
        else:
            @pl.when(k == 0)
            def _():
                acc_ref[...] = part

            @pl.when((k > 0) & (k < nk - 1))
            def _():
                acc_ref[...] += part

            @pl.when(k == nk - 1)
            def _():
                finish(acc_ref[...] + part)

    ins = [a, b] + ([add] if has_add else [])
    in_specs = [a_spec, b_spec] + ([o_spec] if has_add else [])
    acc_shape = (tm, tn) if nk > 1 else (8, 128)
    outs, comm_outs = _call(name, body, grid=grid, ins=ins, in_specs=in_specs,
                            out_shape=[jax.ShapeDtypeStruct((m, n), out_dtype)], out_specs=[o_spec],
                            scratch=[pltpu.VMEM(acc_shape, F32)], comm=comm)
    return outs[0], comm_outs


SSD_STATE = (SSM_GROUPS * SSM_STATE, SSM_RPG * SSM_HEADDIM)


def _ssd_fwd(xc, proj, dt_bias, a_log, d_skip, norm_g, nb, nc, comm=None):
    t = xc.shape[0]
    row = lambda b, c: (b * nc + c, 0)
    par = lambda shape: pl.BlockSpec(shape, lambda b, c: (0, 0))

    def body(xc_ref, dt_ref, z_ref, dtb_ref, al_ref, ds_ref, ng_ref, y_ref, prev_ref, st_ref):
        @pl.when(pl.program_id(1) == 0)
        def _():
            st_ref[...] = jnp.zeros_like(st_ref)

        prev = st_ref[...]
        prev_ref[0, 0] = prev
        y, new_state = _ssd_block(xc_ref[...], dt_ref[...], z_ref[...], prev, dtb_ref[...], al_ref[...], ds_ref[...],
                                  ng_ref[...])
        y_ref[...] = y.astype(y_ref.dtype)
        st_ref[...] = new_state

    return _call(
        "ssd_fwd", body, grid=(nb, nc), ins=[xc, proj, proj, dt_bias, a_log, d_skip, norm_g],
        in_specs=[pl.BlockSpec((CHUNK, SSM_CONV_DIM), row),
                  pl.BlockSpec((CHUNK, 128), lambda b, c: (b * nc + c, DT_COL // 128)),
                  pl.BlockSpec((CHUNK, SSM_INNER), lambda b, c: (b * nc + c, 1)),
                  par((1, 128)), par((1, 128)), par((1, 128)), par((1, SSM_INNER))],
        out_specs=[pl.BlockSpec((CHUNK, SSM_INNER), row), pl.BlockSpec((1, 1) + SSD_STATE, lambda b, c: (b, c, 0, 0))],
        out_shape=[jax.ShapeDtypeStruct((t, SSM_INNER), BF16), jax.ShapeDtypeStruct((nb, nc) + SSD_STATE, F32)],
        scratch=[pltpu.VMEM(SSD_STATE, F32)], comm=comm)


def _ssd_bwd(xc, proj, prevs, dt_bias, a_log, d_skip, norm_g, dy, nb, nc, comm=None):
    t = xc.shape[0]
    row = lambda b, c: (b * nc + (nc - 1 - c), 0)
    par = lambda shape: pl.BlockSpec(shape, lambda b, c: (0, 0))

    def body(xc_ref, dt_ref, z_ref, prev_ref, dtb_ref, al_ref, ds_ref, ng_ref, dy_ref,
             dxc_ref, ddt_ref, dz_ref, ddtb_ref, dal_ref, dds_ref, dng_ref, dst_ref):
        @pl.when(pl.program_id(1) == 0)
        def _():
            dst_ref[...] = jnp.zeros_like(dst_ref)

        @pl.when((pl.program_id(0) == 0) & (pl.program_id(1) == 0))
        def _():
            ddtb_ref[...] = jnp.zeros_like(ddtb_ref)
            dal_ref[...] = jnp.zeros_like(dal_ref)
            dds_ref[...] = jnp.zeros_like(dds_ref)
            dng_ref[...] = jnp.zeros_like(dng_ref)

        _, vjp = jax.vjp(_ssd_block, xc_ref[...], dt_ref[...], z_ref[...], prev_ref[0, 0], dtb_ref[...], al_ref[...],
                         ds_ref[...], ng_ref[...])
        dxc, ddt, dz, dprev, ddtb, dal, dds, dng = vjp((dy_ref[...].astype(F32), dst_ref[...]))
        dxc_ref[...] = dxc
        ddt_ref[:, :128] = ddt.astype(ddt_ref.dtype)
        ddt_ref[:, 128:] = jnp.zeros((CHUNK, DT_PAD - 128), ddt_ref.dtype)
        dz_ref[...] = dz.astype(dz_ref.dtype)
        dst_ref[...] = dprev
        ddtb_ref[...] += ddtb
        dal_ref[...] += dal
        dds_ref[...] += dds
        dng_ref[...] += dng

    return _call(
        "ssd_bwd", body, grid=(nb, nc), ins=[xc, proj, proj, prevs, dt_bias, a_log, d_skip, norm_g, dy],
        in_specs=[pl.BlockSpec((CHUNK, SSM_CONV_DIM), row),
                  pl.BlockSpec((CHUNK, 128), lambda b, c: (b * nc + (nc - 1 - c), DT_COL // 128)),
                  pl.BlockSpec((CHUNK, SSM_INNER), lambda b, c: (b * nc + (nc - 1 - c), 1)),
                  pl.BlockSpec((1, 1) + SSD_STATE, lambda b, c: (b, nc - 1 - c, 0, 0)),
                  par((1, 128)), par((1, 128)), par((1, 128)), par((1, SSM_INNER)),
                  pl.BlockSpec((CHUNK, SSM_INNER), row)],
        out_specs=[pl.BlockSpec((CHUNK, SSM_CONV_DIM), row), pl.BlockSpec((CHUNK, DT_PAD), row),
                   pl.BlockSpec((CHUNK, SSM_INNER), row), par((1, 128)), par((1, 128)), par((1, 128)),
                   par((1, SSM_INNER))],
        out_shape=[jax.ShapeDtypeStruct((t, SSM_CONV_DIM), F32), jax.ShapeDtypeStruct((t, DT_PAD), BF16),
                   jax.ShapeDtypeStruct((t, SSM_INNER), BF16), jax.ShapeDtypeStruct((1, 128), F32),
                   jax.ShapeDtypeStruct((1, 128), F32), jax.ShapeDtypeStruct((1, 128), F32),
                   jax.ShapeDtypeStruct((1, SSM_INNER), F32)],
        scratch=[pltpu.VMEM(SSD_STATE, F32)], comm=comm)


def _loss_head(y, target):
    t, d = y.shape
    tm = _pick_tile(t, (256,))
    blk = pl.BlockSpec((tm, d), lambda i: (i, 0))

    def body(y_ref, t_ref, loss_ref, dy_ref):
        err = y_ref[...] - t_ref[...]
        dy_ref[...] = err * (1.0 / d)

        @pl.when(pl.program_id(0) == 0)
        def _():
            loss_ref[...] = jnp.zeros_like(loss_ref)

        loss_ref[...] += 0.5 * jnp.sum(jnp.mean(err * err, axis=-1, keepdims=True), axis=0, keepdims=True)

    return _call("loss_head", body, grid=(t // tm,), ins=[y, target], in_specs=[blk, blk],
                 out_specs=[pl.BlockSpec((1, 1), lambda i: (0, 0)), blk],
                 out_shape=[jax.ShapeDtypeStruct((1, 1), F32), jax.ShapeDtypeStruct((t, d), F32)])[0]


def _adamw_math(g, w, m, v):
    m_new = ADAM_B1 * m + (1.0 - ADAM_B1) * g
    v_new = ADAM_B2 * v + (1.0 - ADAM_B2) * jnp.square(g)
    m_hat = m_new / (1.0 - ADAM_B1 ** ADAM_STEP)
    v_hat = v_new / (1.0 - ADAM_B2 ** ADAM_STEP)
    delta = -ADAM_LR * (m_hat / (jnp.sqrt(v_hat) + ADAM_EPS) + ADAM_WD * w)
    return delta, m_new, v_new


def _adamw_sharded(name, parts, w, m, v):
    _, a, b = w.shape
    tr = _pick_tile(a, (128,))
    nt = a // tr
    part_specs = [pl.BlockSpec((N_DEV, tr, b),
                               (lambda l, i, _k=k: (0, jnp.where(l == _k, i, jnp.where(l > _k, nt - 1, 0)), 0)))
                  for k in range(DEPTH)]
    blk = pl.BlockSpec((1, tr, b), lambda l, i: (l, i, 0))

    def body(*refs):
        p_refs = refs[:DEPTH]
        w_ref, m_ref, v_ref, g_out, d_out, m_out, v_out = refs[DEPTH:]
        for k in range(DEPTH):
            @pl.when(pl.program_id(0) == k)
            def _(p_ref=p_refs[k]):
                g = p_ref[0].astype(F32)
                for p in range(1, N_DEV):
                    g = g + p_ref[p].astype(F32)
                delta, m_new, v_new = _adamw_math(g, w_ref[0], m_ref[0], v_ref[0])
                g_out[0] = g
                d_out[0] = delta
                m_out[0] = m_new
                v_out[0] = v_new

    return _call(name, body, grid=(DEPTH, nt), ins=list(parts) + [w, m, v], in_specs=part_specs + [blk, blk, blk],
                 out_specs=[blk] * 4, out_shape=[jax.ShapeDtypeStruct(w.shape, F32)] * 4)[0]


def _adamw_small(name, g, w, m, v):
    full = pl.BlockSpec(w.shape, lambda i: (0, 0))

    def body(g_ref, w_ref, m_ref, v_ref, d_out, m_out, v_out):
        delta, m_new, v_new = _adamw_math(g_ref[...], w_ref[...], m_ref[...], v_ref[...])
        d_out[...] = delta
        m_out[...] = m_new
        v_out[...] = v_new

    return _call(name, body, grid=(1,), ins=[g, w, m, v], in_specs=[full] * 4, out_specs=[full] * 3,
                 out_shape=[jax.ShapeDtypeStruct(w.shape, F32)] * 3)[0]


def _sum_parts(name, parts):
    n_parts, rows, cols = parts.shape
    tr = _pick_tile(rows, (512, 256, 128, 64, 32, 16, 8))

    def body(p_ref, o_ref):
        acc = p_ref[0]
        for p in range(1, n_parts):
            acc = acc + p_ref[p]
        o_ref[...] = acc

    return _call(name, body, grid=(rows // tr,), ins=[parts],
                 in_specs=[pl.BlockSpec((n_parts, tr, cols), lambda i: (0, i, 0))],
                 out_specs=[pl.BlockSpec((tr, cols), lambda i: (i, 0))],
                 out_shape=[jax.ShapeDtypeStruct((rows, cols), parts.dtype)])[0][0]


def _pack_w_in(w):
    pad = jnp.zeros((w.shape[0], DT_PAD - SSM_HEADS), w.dtype)
    return jnp.concatenate([w[:, :DT_COL], w[:, DT_COL:GA_COL], pad, w[:, GA_COL:]], axis=1)


def _unpack_w_in(w):
    return jnp.concatenate([w[:, :DT_COL + SSM_HEADS], w[:, P_GA:]], axis=1)


def _pad_heads(v):
    return jnp.pad(v, (0, 128 - SSM_HEADS)).reshape(1, 128)


def _run_step(x, mem, target, small, ex):
    nb, s, d = x.shape
    t = nb * s
    nc = s // CHUNK
    rows = _pick_tile(t, (256,))
    rows_wide = _pick_tile(t, (512, 256))
    tq = _pick_tile(s, (512, 256))
    vec = lambda a: a.reshape(1, -1)
    full1 = lambda shape: pl.BlockSpec(shape, lambda i: (0,) * len(shape))
    row1 = lambda tm, w: pl.BlockSpec((tm, w), lambda i: (i, 0))
    sds = jax.ShapeDtypeStruct

    def mm(call, l, a, b, **kw):
        comm = ex.before(call, l)
        out, comm_outs = _matmul(call, a, b, comm=comm, **kw)
        if comm is not None:
            ex.after(call, l, comm_outs)
        return out

    mem_specs = [row1(256, d), full1((1, d)), full1((1, d))]
    mem_ins = [mem.reshape(nb * MEM_LEN, d), vec(small["mem_ln_g"]), vec(small["mem_ln_b"])]
    (mem_n,) = _stage_fwd("memln_fwd", _memln_block, (nb * MEM_LEN // 256,), mem_ins, mem_specs,
                          [sds((nb * MEM_LEN, d), BF16)], [row1(256, d)])

    h = x.reshape(t, d)
    h_bf = h.astype(BF16)
    ln_specs = [row1(rows, d), row1(rows, d), full1((1, d)), full1((1, d))]
    ln_outs = [sds((t, d), F32), sds((t, d), BF16)]
    ln_out_specs = [row1(rows, d), row1(rows, d)]
    saved = []
    for l in range(DEPTH):
        sv = {"h_bf": h_bf}
        proj = mm("mm_in", l, h_bf, ex.weight("w_in", l))
        sv["proj"] = proj
        sgu_ins = [proj, proj, vec(small["sg_ln_g"][l]), vec(small["sg_ln_b"][l]), small["sg_w"][l], small["sg_b"][l].T]
        sgu_specs = [pl.BlockSpec((CHUNK, d), lambda i: (i, 0)), pl.BlockSpec((CHUNK, d), lambda i: (i, 1)),
                     full1((1, d)), full1((1, d)), full1((SG_GROUPS, CHUNK, CHUNK)), full1((CHUNK, SG_GROUPS))]
        (a_out,) = _stage_fwd("sgu_fwd", _sgu_block, (t // CHUNK,), sgu_ins, sgu_specs, [sds((t, d), BF16)],
                              [row1(CHUNK, d)])
        sv["sgu"] = (sgu_ins, sgu_specs)
        sv["a_out"] = a_out
        cw = 256
        conv_ins = [proj, small["conv_w"][l], vec(small["conv_b"][l])]
        conv_specs = [pl.BlockSpec((s, cw), lambda j, b: (b, XBC_COL // cw + j)),
                      pl.BlockSpec((SSM_CONV, cw), lambda j, b: (0, j)), pl.BlockSpec((1, cw), lambda j, b: (0, j))]
        conv_out_spec = pl.BlockSpec((s, cw), lambda j, b: (b, j))
        (xc,) = _stage_fwd("conv_fwd", _conv_block, (SSM_CONV_DIM // cw, nb), conv_ins, conv_specs,
                           [sds((t, SSM_CONV_DIM), F32)], [conv_out_spec])
        sv["conv"] = (conv_ins, conv_specs, conv_out_spec)
        ssd_par = [_pad_heads(small["dt_bias"][l]), _pad_heads(small["a_log"][l]), _pad_heads(small["d_skip"][l]),
                   vec(small["ssm_norm_g"][l])]
        comm = ex.before("ssd_fwd", l)
        (y_ssd, prevs), comm_outs = _ssd_fwd(xc, proj, *ssd_par, nb, nc, comm=comm)
        if comm is not None:
            ex.after("ssd_fwd", l, comm_outs)
        sv["ssd"] = (xc, prevs, ssd_par)
        sv["y_ssd"] = y_ssd
        br_a = mm("mm_sq", l, a_out, ex.weight("p_a", l))
        br_b = mm("mm_pb", l, y_ssd, ex.weight("p_b", l))
        mw = 512
        merge_ins = [proj, proj, br_a, br_b]
        merge_out_spec = pl.BlockSpec((rows_wide, mw), lambda i, j: (i, j))
        merge_specs = [pl.BlockSpec((rows_wide, mw), lambda i, j: (i, P_GA // mw + j)),
                       pl.BlockSpec((rows_wide, mw), lambda i, j: (i, P_GB // mw + j)), merge_out_spec, merge_out_spec]
        (merged,) = _stage_fwd("merge_fwd", _merge_block, (t // rows_wide, d // mw), merge_ins, merge_specs,
                               [sds((t, d), BF16)], [merge_out_spec])
        sv["merge"] = (merge_ins, merge_specs, merge_out_spec)
        sv["merged"] = merged
        y1 = mm("mm_sq", l, merged, ex.weight("w_mix_o", l))
        ln1_ins = [h, y1, vec(small["ln_g"][l, 0]), vec(small["ln_b"][l, 0])]
        h1, h1_bf = _stage_fwd("lnres_fwd", _lnres_block_twice, (t // rows,), ln1_ins, ln_specs, ln_outs, ln_out_specs)
        sv["ln1"] = ln1_ins
        q = mm("mm_sq", l, h1_bf, ex.weight("w_xq", l))
        kv = mm("mm_kv", l, mem_n, ex.weight("w_xkv", l))
        attn_ins = [q, kv]
        attn_out_spec = pl.BlockSpec((tq, d), lambda b, i: (b * (s // tq) + i, 0))
        attn_specs = [attn_out_spec, pl.BlockSpec((MEM_LEN, 2 * d), lambda b, i: (b, 0))]
        (o,) = _stage_fwd("attn_fwd", _attn_block, (nb, s // tq), attn_ins, attn_specs, [sds((t, d), BF16)],
                          [attn_out_spec])
        sv["attn"] = (attn_ins, attn_specs, attn_out_spec)
        sv["o"] = o
        sv["h1_bf"] = h1_bf
        y2 = mm("mm_sq", l, o, ex.weight("w_xo", l))
        ln2_ins = [h1, y2, vec(small["ln_g"][l, 1]), vec(small["ln_b"][l, 1])]
        h2, h2_bf = _stage_fwd("lnres_fwd", _lnres_block_twice, (t // rows,), ln2_ins, ln_specs, ln_outs, ln_out_specs)
        sv["ln2"] = ln2_ins
        sv["h2_bf"] = h2_bf
        gu = mm("mm_ffn_in", l, h2_bf, ex.weight("w_ffn_in", l))
        (act,) = _stage_fwd("swiglu_fwd", _swiglu_block, (t // 128,), [gu], [row1(128, 2 * FFN_HIDDEN)],
                            [sds((t, FFN_HIDDEN), BF16)], [row1(128, FFN_HIDDEN)])
        sv["gu"] = gu
        sv["act"] = act
        y3 = mm("mm_ffn_out", l, act, ex.weight("w_ffn_out", l))
        ln3_ins = [h2, y3, vec(small["ln_g"][l, 2]), vec(small["ln_b"][l, 2])]
        h, h_bf = _stage_fwd("lnres_fwd", _lnres_block_twice, (t // rows,), ln3_ins, ln_specs, ln_outs, ln_out_specs)
        sv["ln3"] = ln3_ins
        saved.append(sv)

    loss, dh = _loss_head(h, target.reshape(t, d))

    g_small = {n: [None] * DEPTH for n in SMALL_REP + SMALL_SH if n not in ("mem_ln_g", "mem_ln_b")}
    dmem_n = []
    ln_grads = [(0, (), F32), (1, (), BF16), (2, (0,), F32), (3, (0,), F32)]
    for l in reversed(range(DEPTH)):
        sv = saved[l]
        dln_g, dln_b = [None] * 3, [None] * 3
        dres, dy3, dln_g[2], dln_b[2] = _stage_bwd("lnres_bwd", _lnres_block, (t // rows,), sv["ln3"], ln_specs,
                                                   [(dh,)], [row1(rows, d)], ln_grads)
        ex.grad("w_ffn_out", l, mm("mm_ffn_out_dw", l, sv["act"], dy3, ta=True, out_dtype=BF16))
        dact = mm("mm_ffn_out_dx", l, dy3, ex.weight("w_ffn_out", l), tb=True)
        (dgu,) = _stage_bwd("swiglu_bwd", _swiglu_block, (t // 128,), [sv["gu"]], [row1(128, 2 * FFN_HIDDEN)],
                            [(dact,)], [row1(128, FFN_HIDDEN)], [(0, (), BF16)])
        ex.grad("w_ffn_in", l, mm("mm_ffn_in_dw", l, sv["h2_bf"], dgu, ta=True, out_dtype=BF16))
        dh2 = mm("mm_ffn_in_dx", l, dgu, ex.weight("w_ffn_in", l), tb=True, add=dres)
        dres, dy2, dln_g[1], dln_b[1] = _stage_bwd("lnres_bwd", _lnres_block, (t // rows,), sv["ln2"], ln_specs,
                                                   [(dh2,)], [row1(rows, d)], ln_grads)
        ex.grad("w_xo", l, mm("mm_sq_dw", l, sv["o"], dy2, ta=True, out_dtype=BF16))
        do = mm("mm_sq_dx", l, dy2, ex.weight("w_xo", l), tb=True)
        attn_ins, attn_specs, attn_out_spec = sv["attn"]
        dq, dkv = _stage_bwd("attn_bwd", _attn_block, (nb, s // tq), attn_ins, attn_specs, [(do,)], [attn_out_spec],
                             [(0, (), BF16), (1, (1,), F32)])
        ex.grad("w_xq", l, mm("mm_sq_dw", l, sv["h1_bf"], dq, ta=True, out_dtype=BF16))
        dh1 = mm("mm_sq_dx", l, dq, ex.weight("w_xq", l), tb=True, add=dres)
        ex.grad("w_xkv", l, mm("mm_kv_dw", l, mem_n, dkv, ta=True, out_dtype=BF16))
        dmem_n.append(mm("mm_kv_dx", l, dkv, ex.weight("w_xkv", l), tb=True))
        dres, dy1, dln_g[0], dln_b[0] = _stage_bwd("lnres_bwd", _lnres_block, (t // rows,), sv["ln1"], ln_specs,
                                                   [(dh1,)], [row1(rows, d)], ln_grads)
        g_small["ln_g"][l] = jnp.concatenate(dln_g, axis=0)
        g_small["ln_b"][l] = jnp.concatenate(dln_b, axis=0)
        ex.grad("w_mix_o", l, mm("mm_sq_dw", l, sv["merged"], dy1, ta=True, out_dtype=BF16))
        dmerged = mm("mm_sq_dx", l, dy1, ex.weight("w_mix_o", l), tb=True)
        merge_ins, merge_specs, merge_out_spec = sv["merge"]
        dga, dgb, dbr_a, dbr_b = _stage_bwd("merge_bwd", _merge_block, (t // rows_wide, d // 512), merge_ins, merge_specs,
                                            [(dmerged,)], [merge_out_spec],
                                            [(i, (), BF16, ((t, d), merge_out_spec)) for i in range(4)])
        ex.grad("p_a", l, mm("mm_sq_dw", l, sv["a_out"], dbr_a, ta=True, out_dtype=BF16))
        da_out = mm("mm_sq_dx", l, dbr_a, ex.weight("p_a", l), tb=True)
        ex.grad("p_b", l, mm("mm_pb_dw", l, sv["y_ssd"], dbr_b, ta=True, out_dtype=BF16))
        dy_ssd = mm("mm_pb_dx", l, dbr_b, ex.weight("p_b", l), tb=True)
        sgu_ins, sgu_specs = sv["sgu"]
        du, dv, dsg_ln_g, dsg_ln_b, dsg_w, dsg_b = _stage_bwd(
            "sgu_bwd", _sgu_block, (t // CHUNK,), sgu_ins, sgu_specs, [(da_out,)], [row1(CHUNK, d)],
            [(0, (), BF16, ((t, d), row1(CHUNK, d))), (1, (), BF16, ((t, d), row1(CHUNK, d))), (2, (0,), F32),
             (3, (0,), F32), (4, (0,), F32), (5, (0,), F32)])
        g_small["sg_ln_g"][l], g_small["sg_ln_b"][l], g_small["sg_w"][l], g_small["sg_b"][l] = (
            dsg_ln_g[0], dsg_ln_b[0], dsg_w, dsg_b.T)
        xc, prevs, ssd_par = sv["ssd"]
        comm = ex.before("ssd_bwd", l)
        (dxc, ddt, dz, ddtb, dal, dds, dng), comm_outs = _ssd_bwd(xc, sv["proj"], prevs, *ssd_par, dy_ssd, nb, nc,
                                                                  comm=comm)
        if comm is not None:
            ex.after("ssd_bwd", l, comm_outs)
        g_small["dt_bias"][l], g_small["a_log"][l], g_small["d_skip"][l] = (
            ddtb[0, :SSM_HEADS], dal[0, :SSM_HEADS], dds[0, :SSM_HEADS])
        g_small["ssm_norm_g"][l] = dng[0]
        conv_ins, conv_specs, conv_out_spec = sv["conv"]
        dxbc, dconv_w, dconv_b = _stage_bwd("conv_bwd", _conv_block, (SSM_CONV_DIM // 256, nb), conv_ins, conv_specs,
                                            [(dxc,)], [conv_out_spec],
                                            [(0, (), BF16, ((t, SSM_CONV_DIM), conv_out_spec)), (1, (1,), F32),
                                             (2, (1,), F32)])
        g_small["conv_w"][l], g_small["conv_b"][l] = dconv_w, dconv_b[0]
        dproj = jnp.concatenate([du, dv, dz, dxbc, ddt, dga, dgb], axis=1)
        ex.grad("w_in", l, mm("mm_in_dw", l, sv["h_bf"], dproj, ta=True, out_dtype=BF16))
        if l == 0:
            dmg, dmb = _stage_bwd("memln_bwd", _memln_block, (nb * MEM_LEN // 256,), mem_ins, mem_specs,
                                  [tuple(dmem_n)], [row1(256, d)], [(1, (0,), F32), (2, (0,), F32)])
            done = {n: jnp.stack(g, axis=0) for n, g in g_small.items()}
            done["mem_ln_g"], done["mem_ln_b"] = dmg[0], dmb[0]
            ex.small_grads(done)
        dh = mm("mm_in_dx", l, dproj, ex.weight("w_in", l), tb=True, add=dres)

    return loss, dh.reshape(nb, s, d)


def _pack_flat(arrays, rows):
    flat = jnp.concatenate([a.reshape(-1) for a in arrays])
    return jnp.pad(flat, (0, rows * 128 - flat.shape[0])).reshape(rows, 128)


def _unpack_flat(packed, shapes):
    lead = packed.shape[:-2]
    flat = packed.reshape(lead + (-1,))
    out, pos = [], 0
    for shape in shapes:
        n = math.prod(shape)
        out.append(flat[..., pos:pos + n].reshape(lead + tuple(shape)))
        pos += n
    return out


def _small_rows(n_elems):
    return -(-n_elems // (128 * SMALL_ROW_TILE)) * SMALL_ROW_TILE


def _from_shards(name, gathered):
    _, a, b = gathered.shape
    if name in BIG_COL_SHARDED:
        w = gathered.transpose(1, 0, 2).reshape(a, N_DEV * b)
        return _pack_w_in(w) if name == "w_in" else w
    return gathered.reshape(N_DEV * a, b)


def _to_shards(name, g):
    if name in BIG_COL_SHARDED:
        g = _unpack_w_in(g) if name == "w_in" else g
        a, nb = g.shape
        return g.reshape(a, N_DEV, nb // N_DEV).transpose(1, 0, 2)
    a, b = g.shape
    return g.reshape(N_DEV, a // N_DEV, b)


class _MeshExchange:
    def __init__(self, shards_bf16, first):
        self.shards = shards_bf16
        self.full = dict(first)
        self.grads = {}
        self.received = {}
        self.small = None
        self.small_gathered = None

    def weight(self, name, l):
        return self.full[(name, l)]

    def grad(self, name, l, g):
        self.grads[(name, l)] = g

    def small_grads(self, done):
        self.small = done

    def before(self, call, l):
        comm = _Comm()
        for name, layer in GATHER_PLAN.get((call, l), ()):
            comm.gathers.append((self.shards[name], layer))
        for name, layer in SCATTER_PLAN.get((call, l), ()):
            comm.scatters.append(_to_shards(name, self.grads[(name, layer)]))
        if (call, l) == ("mm_in_dx", 0):
            names = SMALL_REP + SMALL_SH
            rows = _small_rows(sum(math.prod(self.small[n].shape) for n in names))
            comm.gathers.append((_pack_flat([self.small[n] for n in names], rows), None))
        return comm if comm.gathers or comm.scatters else None

    def after(self, call, l, outs):
        gathers = list(GATHER_PLAN.get((call, l), ()))
        for (name, layer), out in zip(gathers, outs):
            self.full[(name, layer)] = _from_shards(name, out)
        outs = outs[len(gathers):]
        if (call, l) == ("mm_in_dx", 0):
            self.small_gathered = outs[0]
            outs = outs[1:]
        for (name, layer), out in zip(SCATTER_PLAN.get((call, l), ()), outs):
            self.received[(name, layer)] = out


def kernel(x, mem, mem_ln_g, mem_ln_b, w_in, sg_ln_g, sg_ln_b, sg_w, sg_b, conv_w, conv_b, dt_bias, a_log, d_skip, ssm_norm_g, p_a, p_b, w_mix_o, w_xq, w_xkv, w_xo, w_ffn_in, w_ffn_out, ln_g, ln_b, loss_target, m_mem_ln_g, m_mem_ln_b, m_w_in, m_sg_ln_g, m_sg_ln_b, m_sg_w, m_sg_b, m_conv_w, m_conv_b, m_dt_bias, m_a_log, m_d_skip, m_ssm_norm_g, m_p_a, m_p_b, m_w_mix_o, m_w_xq, m_w_xkv, m_w_xo, m_w_ffn_in, m_w_ffn_out, m_ln_g, m_ln_b, v_mem_ln_g, v_mem_ln_b, v_w_in, v_sg_ln_g, v_sg_ln_b, v_sg_w, v_sg_b, v_conv_w, v_conv_b, v_dt_bias, v_a_log, v_d_skip, v_ssm_norm_g, v_p_a, v_p_b, v_w_mix_o, v_w_xq, v_w_xkv, v_w_xo, v_w_ffn_in, v_w_ffn_out, v_ln_g, v_ln_b):
    args = dict(locals())
    w = {n: args[n] for n in WEIGHTS}
    m = {n: args["m_" + n] for n in WEIGHTS}
    v = {n: args["v_" + n] for n in WEIGHTS}
    me = 4 * lax.axis_index("x") + 2 * lax.axis_index("y") + lax.axis_index("c")

    shards = {n: w[n].astype(BF16) for n in BIG}
    sh_shapes = [w[n].shape for n in SMALL_SH]
    first = _Comm()
    first.gathers.append((shards["w_in"], 0))
    first.gathers.append((_pack_flat([w[n] for n in SMALL_SH], _small_rows(sum(math.prod(s) for s in sh_shapes))), None))
    w_in0, small_sh = _comm_only("gather_first", first)
    small = {n: w[n] for n in SMALL_REP}
    for n, sh in zip(SMALL_SH, _unpack_flat(small_sh, sh_shapes)):
        small[n] = sh.transpose(1, 2, 0, 3).reshape(sh.shape[1], sh.shape[2], N_DEV * sh.shape[3])

    ex = _MeshExchange(shards, {("w_in", 0): _from_shards("w_in", w_in0)})
    loss, grad_x = _run_step(x, mem, loss_target, small, ex)
    loss = lax.psum(loss[0, 0], ("x", "y", "c"))

    out = {}
    for n in BIG:
        out[n] = _adamw_sharded("adamw_" + n, [ex.received[(n, l)] for l in range(DEPTH)], w[n], m[n], v[n])
    names = SMALL_REP + SMALL_SH
    g_small = dict(zip(names, _unpack_flat(_sum_parts("sum_small_grads", ex.small_gathered),
                                           [ex.small[n].shape for n in names])))
    for n in names:
        g = g_small[n]
        if n in SMALL_SH:
            width = w[n].shape[-1]
            g = lax.dynamic_slice_in_dim(g, me * width, width, axis=-1)
        two_d = (-1, w[n].shape[-1])
        res = _adamw_small("adamw_" + n, g.reshape(two_d), w[n].reshape(two_d), m[n].reshape(two_d), v[n].reshape(two_d))
        out[n] = [g] + [r.reshape(w[n].shape) for r in res]

    results = []
    for k in range(4):
        results.extend(out[n][k] for n in WEIGHTS)
    return (loss, grad_x, *results)
```

```python
import functools
import math

import jax
import jax.numpy as jnp
from jax import lax
from jax.experimental import pallas as pl
from jax.experimental.pallas import tpu as pltpu

F32 = jnp.float32
BF16 = jnp.bfloat16
HIGHEST = lax.Precision.HIGHEST

N_DEV = 8
D_MODEL = 1024
DEPTH = 2
MEM_LEN = 256
CHUNK = 128
SG_GROUPS = 8
SSM_INNER = 2048
SSM_HEADDIM = 64
SSM_HEADS = 32
SSM_STATE = 128
SSM_GROUPS = 4
SSM_RPG = 8
SSM_CONV = 4
SSM_CONV_DIM = 3072
X_HEADS = 4
X_HEADDIM = 256
FFN_HIDDEN = 2816
ALPHA = float((2 * DEPTH) ** 0.25)
LN_EPS = 1e-5
RMS_EPS = 1e-5
XBC_COL = 4096
DT_COL = 7168
GA_COL = 7200
DT_PAD = 512
P_GA = DT_COL + DT_PAD
P_GB = P_GA + D_MODEL

ADAM_LR = 0.001
ADAM_B1 = 0.9
ADAM_B2 = 0.999
ADAM_EPS = 1e-08
ADAM_WD = 0.01
ADAM_STEP = 10

VMEM_LIMIT = 48 * 1024 * 1024
SMALL_ROW_TILE = 256

BIG = ("w_in", "p_a", "p_b", "w_mix_o", "w_xq", "w_xkv", "w_xo", "w_ffn_in", "w_ffn_out")
BIG_COL_SHARDED = ("w_in", "w_xkv", "w_ffn_in")
SMALL_REP = ("mem_ln_g", "mem_ln_b", "sg_ln_g", "sg_ln_b", "sg_w", "sg_b", "conv_b", "dt_bias", "a_log", "d_skip",
             "ssm_norm_g")
SMALL_SH = ("conv_w", "ln_g", "ln_b")
WEIGHTS = ("mem_ln_g", "mem_ln_b", "w_in", "sg_ln_g", "sg_ln_b", "sg_w", "sg_b", "conv_w", "conv_b", "dt_bias", "a_log",
           "d_skip", "ssm_norm_g", "p_a", "p_b", "w_mix_o", "w_xq", "w_xkv", "w_xo", "w_ffn_in", "w_ffn_out", "ln_g", "ln_b")

REST = ("p_a", "p_b", "w_mix_o", "w_xq", "w_xkv", "w_xo", "w_ffn_in", "w_ffn_out")
GATHER_PLAN = {
    ("mm_in", 0): [(n, 0) for n in REST],
    ("ssd_fwd", 0): [("w_in", 1)],
    ("mm_ffn_in", 0): [(n, 1) for n in ("p_a", "p_b", "w_mix_o", "w_xq", "w_xkv", "w_xo")],
    ("mm_in", 1): [("w_ffn_in", 1), ("w_ffn_out", 1)],
}
SCATTER_PLAN = {
    ("ssd_bwd", 1): [(n, 1) for n in REST],
    ("mm_in_dx", 1): [("w_in", 1)],
    ("ssd_bwd", 0): [(n, 0) for n in REST],
    ("mm_in_dx", 0): [("w_in", 0)],
}


def _layer_norm(x, g, b):
    mu = jnp.mean(x, axis=-1, keepdims=True)
    xc = x - mu
    var = jnp.mean(xc * xc, axis=-1, keepdims=True)
    return xc * lax.rsqrt(var + LN_EPS) * g + b


def _gelu(x):
    return 0.5 * x * (1.0 + lax.erf(x * (1.0 / math.sqrt(2.0))))


def _silu(x):
    return x * jax.nn.sigmoid(x)


def _softplus(x):
    return jnp.maximum(x, 0.0) + jnp.log1p(jnp.exp(-jnp.abs(x)))


def _causal_mask():
    r = lax.broadcasted_iota(jnp.int32, (CHUNK, CHUNK), 0)
    c = lax.broadcasted_iota(jnp.int32, (CHUNK, CHUNK), 1)
    return r >= c


def _sgu_block(u, v, ln_g, ln_b, w, sb):
    gu = _gelu(u)
    vn = _layer_norm(_gelu(v), ln_g, ln_b)
    causal = _causal_mask()
    width = D_MODEL // SG_GROUPS
    outs = []
    for g in range(SG_GROUPS):
        wg = jnp.where(causal, w[g], 0.0).astype(BF16)
        mixed = jnp.dot(wg, vn[:, g * width:(g + 1) * width].astype(BF16), preferred_element_type=F32)
        outs.append(mixed + sb[:, g:g + 1])
    return (gu * jnp.concatenate(outs, axis=1),)


GROUP_W = SSM_RPG * SSM_HEADDIM
NT_DIMS = (((1,), (1,)), ((), ()))
TN_DIMS = (((0,), (0,)), ((), ()))


def _mxu(a, b, dims=(((1,), (0,)), ((), ()))):
    return lax.dot_general(a.astype(BF16), b.astype(BF16), dims, preferred_element_type=F32)


def _expand_heads(q):
    low = lax.broadcasted_iota(jnp.int32, (q.shape[0], 128), 1) < SSM_HEADDIM
    return jnp.concatenate([jnp.where(low, q[:, 2 * j:2 * j + 1], q[:, 2 * j + 1:2 * j + 2])
                            for j in range(SSM_HEADS // 2)], axis=1)


def _reduce_heads(v):
    lane = lax.broadcasted_iota(jnp.int32, (v.shape[0], 128), 1)
    low = lane < SSM_HEADDIM
    out = jnp.zeros((v.shape[0], 128), F32)
    for j in range(SSM_HEADS // 2):
        blk = v[:, 128 * j:128 * (j + 1)]
        s_low = jnp.sum(jnp.where(low, blk, 0.0), axis=-1, keepdims=True)
        s_high = jnp.sum(jnp.where(low, 0.0, blk), axis=-1, keepdims=True)
        out = out + jnp.where(lane == 2 * j, s_low, 0.0) + jnp.where(lane == 2 * j + 1, s_high, 0.0)
    return out


def _ssd_common(xc, dtraw, dt_bias, a_log):
    xs = xc[:, :SSM_INNER]
    pre = dtraw + dt_bias
    dt = _softplus(pre)
    a = -jnp.exp(a_log)
    causal = _causal_mask()
    tril = jnp.where(causal, 1.0, 0.0).astype(F32)
    cs = jnp.dot(tril, dt * a, precision=HIGHEST, preferred_element_type=F32)
    cs_last = cs[CHUNK - 1:CHUNK, :]
    decay_in = jnp.exp(cs)
    decay_st = jnp.exp(cs_last - cs)
    chunk_decay = jnp.exp(cs_last)
    dt_x = _expand_heads(dt)
    w_st_x = _expand_heads(dt * decay_st)
    return dict(xs=xs, pre=pre, dt=dt, a=a, causal=causal, tril=tril, cs=cs, cs_t=cs.T, decay_in=decay_in,
                decay_st=decay_st, chunk_decay=chunk_decay, dt_x=dt_x, w_st_x=w_st_x,
                decay_in_x=_expand_heads(decay_in), chunk_decay_x=_expand_heads(chunk_decay),
                xdt=xs * dt_x, x_st=(xs * w_st_x).astype(BF16),
                low=lax.broadcasted_iota(jnp.int32, (CHUNK, 128), 1) < SSM_HEADDIM)


def _pair_decay(c, h):
    return jnp.exp(jnp.where(c["causal"], c["cs"][:, h:h + 1] - c["cs_t"][h:h + 1, :], -1e30))


def _ssd_forward(xc, dtraw, z, prev, dt_bias, a_log, d_skip_x, norm_g):
    c = _ssd_common(xc, dtraw, dt_bias, a_log)
    y_groups, new_states = [], []
    for g in range(SSM_GROUPS):
        lanes = slice(g * GROUP_W, (g + 1) * GROUP_W)
        bg = xc[:, SSM_INNER + g * SSM_STATE:SSM_INNER + (g + 1) * SSM_STATE].astype(BF16)
        cg = xc[:, SSM_INNER + (SSM_GROUPS + g) * SSM_STATE:SSM_INNER + (SSM_GROUPS + g + 1) * SSM_STATE].astype(BF16)
        pg = prev[g * SSM_STATE:(g + 1) * SSM_STATE, :]
        cb = _mxu(cg, bg, NT_DIMS)
        y_in = _mxu(cg, pg) * c["decay_in_x"][:, lanes]
        new_states.append(pg * c["chunk_decay_x"][:, lanes] + _mxu(bg, c["x_st"][:, lanes], TN_DIMS))
        pairs = []
        for j in range(SSM_RPG // 2):
            h0 = g * SSM_RPG + 2 * j
            xp = c["xdt"][:, 128 * (h0 // 2):128 * (h0 // 2 + 1)]
            pairs.append(_mxu(cb * _pair_decay(c, h0), jnp.where(c["low"], xp, 0.0))
                         + _mxu(cb * _pair_decay(c, h0 + 1), jnp.where(c["low"], 0.0, xp)))
        y_groups.append(jnp.concatenate(pairs, axis=1) + y_in)
    y_pre = jnp.concatenate(y_groups, axis=1) + c["xs"] * d_skip_x
    gated = y_pre * _silu(z)
    normed = [gated[:, g * GROUP_W:(g + 1) * GROUP_W] for g in range(SSM_GROUPS)]
    normed = [yg * lax.rsqrt(jnp.mean(yg * yg, axis=-1, keepdims=True) + RMS_EPS) for yg in normed]
    return jnp.concatenate(normed, axis=1) * norm_g, y_pre, jnp.concatenate(new_states, axis=0)


def _ssd_backward(xc, dtraw, z, prev, y_pre, dt_bias, a_log, d_skip_x, norm_g, dout, dnew):
    c = _ssd_common(xc, dtraw, dt_bias, a_log)
    xs = c["xs"]
    sig = jax.nn.sigmoid(z)
    silu_z = z * sig
    gated = y_pre * silu_z
    d_gated, normed = [], []
    for g in range(SSM_GROUPS):
        lanes = slice(g * GROUP_W, (g + 1) * GROUP_W)
        yg = gated[:, lanes]
        r = lax.rsqrt(jnp.mean(yg * yg, axis=-1, keepdims=True) + RMS_EPS)
        n = yg * r
        gh = dout[:, lanes] * norm_g[:, lanes]
        d_gated.append(r * (gh - n * jnp.mean(gh * n, axis=-1, keepdims=True)))
        normed.append(n)
    d_gated = jnp.concatenate(d_gated, axis=1)
    dnorm_g = jnp.sum(dout * jnp.concatenate(normed, axis=1), axis=0, keepdims=True)
    dy = d_gated * silu_z
    dz = d_gated * y_pre * (sig * (1.0 + z * (1.0 - sig)))
    dxs = dy * d_skip_x
    dd_skip = _reduce_heads(jnp.sum(dy * xs, axis=0, keepdims=True))

    lane = lax.broadcasted_iota(jnp.int32, (CHUNK, 128), 1)
    sub = lax.broadcasted_iota(jnp.int32, (8, 128), 0)
    dcs = jnp.zeros((CHUNK, 128), F32)
    col_slabs = []
    dxdt, dx_st, d_decay_in_x, dprev, d_chunk_decay_x, db_all, dc_all = [], [], [], [], [], [], []
    for g in range(SSM_GROUPS):
        lanes = slice(g * GROUP_W, (g + 1) * GROUP_W)
        bg = xc[:, SSM_INNER + g * SSM_STATE:SSM_INNER + (g + 1) * SSM_STATE].astype(BF16)
        cg = xc[:, SSM_INNER + (SSM_GROUPS + g) * SSM_STATE:SSM_INNER + (SSM_GROUPS + g + 1) * SSM_STATE].astype(BF16)
        pg = prev[g * SSM_STATE:(g + 1) * SSM_STATE, :]
        dng = dnew[g * SSM_STATE:(g + 1) * SSM_STATE, :]
        dy_g = dy[:, lanes]
        cb = _mxu(cg, bg, NT_DIMS)
        t1 = (dy_g * c["decay_in_x"][:, lanes]).astype(BF16)
        d_decay_in_x.append(dy_g * _mxu(cg, pg))
        dc = _mxu(t1, pg, NT_DIMS)
        dprev.append(_mxu(cg, t1, TN_DIMS) + dng * c["chunk_decay_x"][:, lanes])
        d_chunk_decay_x.append(jnp.sum(dng * pg, axis=0, keepdims=True))
        db = _mxu(c["x_st"][:, lanes], dng, NT_DIMS)
        dx_st.append(_mxu(bg, dng))
        dcb = jnp.zeros((CHUNK, CHUNK), F32)
        cols = []
        for j in range(SSM_RPG // 2):
            h0 = g * SSM_RPG + 2 * j
            blk = slice(128 * (h0 // 2), 128 * (h0 // 2 + 1))
            xp = c["xdt"][:, blk]
            dyp = dy[:, blk].astype(BF16)
            pair_dx = []
            for k, xk in enumerate((jnp.where(c["low"], xp, 0.0), jnp.where(c["low"], 0.0, xp))):
                dec = _pair_decay(c, h0 + k)
                pair_dx.append(_mxu(cb * dec, dyp, TN_DIMS))
                dm = _mxu(dyp, xk, NT_DIMS)
                dml = dm * dec
                dcb = dcb + dml
                dseg = dml * cb
                dcs = dcs + jnp.where(lane == h0 + k, jnp.sum(dseg, axis=-1, keepdims=True), 0.0)
                cols.append(jnp.sum(dseg, axis=0, keepdims=True))
            dxdt.append(jnp.where(c["low"], pair_dx[0], pair_dx[1]))
        slab = jnp.zeros((8, 128), F32)
        for r in range(SSM_RPG):
            slab = slab + jnp.where(sub == r, cols[r], 0.0)
        col_slabs.append(slab)
        dc_all.append(dc + _mxu(dcb, bg))
        db_all.append(db + _mxu(dcb, cg, TN_DIMS))
    dxdt = jnp.concatenate(dxdt, axis=1)
    dx_st = jnp.concatenate(dx_st, axis=1)
    col_rows = jnp.concatenate(col_slabs + [jnp.zeros((CHUNK - SSM_HEADS, 128), F32)], axis=0)
    dcs = dcs - col_rows.T
    dxs = dxs + dxdt * c["dt_x"] + dx_st * c["w_st_x"]
    ddt = _reduce_heads(dxdt * xs)
    dw_st = _reduce_heads(dx_st * xs)
    dcs = dcs + _reduce_heads(jnp.concatenate(d_decay_in_x, axis=1)) * c["decay_in"]
    ddt = ddt + dw_st * c["decay_st"]
    d_log_st = dw_st * c["dt"] * c["decay_st"]
    dcs = dcs - d_log_st
    dcs_last = (jnp.sum(d_log_st, axis=0, keepdims=True)
                + _reduce_heads(jnp.concatenate(d_chunk_decay_x, axis=1)) * c["chunk_decay"])
    row = lax.broadcasted_iota(jnp.int32, (CHUNK, 128), 0)
    dcs = dcs + jnp.where(row == CHUNK - 1, dcs_last, 0.0)
    dda = lax.dot_general(c["tril"], dcs, TN_DIMS, precision=HIGHEST, preferred_element_type=F32)
    ddt = ddt + dda * c["a"]
    da_log = jnp.sum(dda * c["dt"], axis=0, keepdims=True) * c["a"]
    dpre = ddt * jax.nn.sigmoid(c["pre"])
    dxc = jnp.concatenate([dxs] + db_all + dc_all, axis=1)
    return (dxc, dpre, dz, jnp.concatenate(dprev, axis=0), jnp.sum(dpre, axis=0, keepdims=True), da_log, dd_skip,
            dnorm_g)


def _conv_block(x, w, b):
    rows = lax.broadcasted_iota(jnp.int32, x.shape, 0)
    acc = x * w[SSM_CONV - 1:SSM_CONV, :] + b
    for k in range(SSM_CONV - 1):
        shift = SSM_CONV - 1 - k
        acc = acc + _shift_rows(x, rows, shift) * w[k:k + 1, :]
    return (_silu(acc),)


@functools.partial(jax.custom_vjp, nondiff_argnums=(2,))
def _shift_rows(x, rows, shift):
    return jnp.where(rows >= shift, pltpu.roll(x, shift, 0), 0.0)


def _shift_rows_fwd(x, rows, shift):
    return _shift_rows(x, rows, shift), rows


def _shift_rows_bwd(shift, rows, g):
    n = g.shape[0]
    return jnp.where(rows < n - shift, pltpu.roll(g, n - shift, 0), 0.0), None


_shift_rows.defvjp(_shift_rows_fwd, _shift_rows_bwd)


def _merge_block(ga, gb, br_a, br_b):
    return (jax.nn.sigmoid(ga) * br_a + jax.nn.sigmoid(gb) * br_b,)


def _lnres_block(x, y, g, b):
    return (_layer_norm(ALPHA * x + y, g, b),)


def _lnres_block_twice(x, y, g, b):
    out = _layer_norm(ALPHA * x + y, g, b)
    return out, out


def _memln_block(x, g, b):
    return (_layer_norm(x, g, b),)


def _attn_block(q, kv):
    outs = []
    for h in range(X_HEADS):
        qh = q[:, h * X_HEADDIM:(h + 1) * X_HEADDIM].astype(BF16)
        kh = kv[:, h * X_HEADDIM:(h + 1) * X_HEADDIM].astype(BF16)
        vh = kv[:, D_MODEL + h * X_HEADDIM:D_MODEL + (h + 1) * X_HEADDIM].astype(BF16)
        s = lax.dot_general(qh, kh, (((1,), (1,)), ((), ())), preferred_element_type=F32) * (X_HEADDIM ** -0.5)
        s = s - lax.stop_gradient(jnp.max(s, axis=-1, keepdims=True))
        e = jnp.exp(s)
        p = e / jnp.sum(e, axis=-1, keepdims=True)
        outs.append(jnp.dot(p.astype(BF16), vh, preferred_element_type=F32))
    return (jnp.concatenate(outs, axis=1),)


def _swiglu_block(gu):
    return (_silu(gu[:, :FFN_HIDDEN]) * gu[:, FFN_HIDDEN:],)


class _Comm:
    def __init__(self):
        self.gathers = []
        self.scatters = []

    def operands(self):
        ins = [a for a, _ in self.gathers] + list(self.scatters)
        shapes = [jax.ShapeDtypeStruct((N_DEV,) + (a.shape if idx is None else a.shape[1:]), a.dtype)
                  for a, idx in self.gathers]
        shapes += [jax.ShapeDtypeStruct(a.shape, a.dtype) for a in self.scatters]
        scratch = []
        for n in (len(self.gathers), len(self.scatters)):
            if n:
                scratch += [pltpu.SemaphoreType.DMA((7 * n,)), pltpu.SemaphoreType.DMA((7 * n,)),
                            pltpu.SemaphoreType.DMA((n,))]
        return ins, shapes, scratch

    def _split(self, in_refs, out_refs, sems):
        ng = len(self.gathers)
        g_sems = sems[:3] if ng else None
        s_sems = sems[3:] if ng else sems
        return in_refs[:ng], in_refs[ng:], out_refs[:ng], out_refs[ng:], g_sems, s_sems

    def _gather_copies(self, i, src_ref, out_ref, sems):
        send_sems, recv_sems, local_sems = sems
        x, y, c = lax.axis_index("x"), lax.axis_index("y"), lax.axis_index("c")
        me, sibling = (x, y, c), (x, y, 1 - c)
        chips = [(1 - x, y), (x, 1 - y), (1 - x, 1 - y)]
        idx = self.gathers[i][1]
        src = src_ref if idx is None else src_ref.at[idx]

        def slot(px, py, pc):
            return out_ref.at[4 * px + 2 * py + pc]

        def copy(k, blk, to, from_src=False):
            return pltpu.make_async_remote_copy(
                src_ref=src if from_src else slot(*blk), dst_ref=slot(*blk), send_sem=send_sems.at[7 * i + k],
                recv_sem=recv_sems.at[7 * i + k], device_id=to, device_id_type=pl.DeviceIdType.MESH)

        mine = pltpu.make_async_copy(src, slot(*me), local_sems.at[i])
        first = [copy(0, me, sibling, True)] + [copy(1 + j, me, (*chip, c), True) for j, chip in enumerate(chips)]
        passed = [copy(4 + j, (*chip, c), sibling) for j, chip in enumerate(chips)]
        arrivals = [copy(1 + j, (*chip, c), me) for j, chip in enumerate(chips)]
        from_sibling = [copy(0, sibling, me)] + [copy(4 + j, (*chip, 1 - c), me) for j, chip in enumerate(chips)]
        return mine, first, passed, arrivals, from_sibling

    def _scatter_copies(self, i, src_ref, out_ref, sems):
        send_sems, recv_sems, local_sems = sems
        x, y, c = lax.axis_index("x"), lax.axis_index("y"), lax.axis_index("c")
        me = 4 * x + 2 * y + c
        mine = pltpu.make_async_copy(src_ref.at[me], out_ref.at[me], local_sems.at[i])
        copies = []
        for k in range(1, N_DEV):
            px = 1 - x if k & 4 else x
            py = 1 - y if k & 2 else y
            pc = 1 - c if k & 1 else c
            copies.append(pltpu.make_async_remote_copy(
                src_ref=src_ref.at[4 * px + 2 * py + pc], dst_ref=out_ref.at[me], send_sem=send_sems.at[7 * i + k - 1],
                recv_sem=recv_sems.at[7 * i + k - 1], device_id=(px, py, pc), device_id_type=pl.DeviceIdType.MESH))
        return mine, copies

    def start(self, in_refs, out_refs, sems):
        g_in, s_in, g_out, s_out, g_sems, s_sems = self._split(in_refs, out_refs, sems)
        for i in range(len(self.gathers)):
            mine, first, _, _, _ = self._gather_copies(i, g_in[i], g_out[i], g_sems)
            mine.start()
            for cp in first:
                cp.start()
        for i in range(len(self.scatters)):
            mine, copies = self._scatter_copies(i, s_in[i], s_out[i], s_sems)
            mine.start()
            for cp in copies:
                cp.start()

    def finish(self, in_refs, out_refs, sems):
        g_in, s_in, g_out, s_out, g_sems, s_sems = self._split(in_refs, out_refs, sems)
        parts = [self._gather_copies(i, g_in[i], g_out[i], g_sems) for i in range(len(self.gathers))]
        for j in range(3):
            for _, _, passed, arrivals, _ in parts:
                arrivals[j].wait_recv()
                passed[j].start()
        for mine, first, passed, _, from_sibling in parts:
            for cp in from_sibling:
                cp.wait_recv()
            for cp in first + passed:
                cp.wait_send()
            mine.wait()
        for i in range(len(self.scatters)):
            mine, copies = self._scatter_copies(i, s_in[i], s_out[i], s_sems)
            for cp in copies:
                cp.wait_recv()
            for cp in copies:
                cp.wait_send()
            mine.wait()


def _params(grid):
    return pltpu.CompilerParams(dimension_semantics=("arbitrary",) * len(grid), vmem_limit_bytes=VMEM_LIMIT)


def _call(name, body, *, grid, ins, in_specs, out_shape, out_specs, scratch=(), comm=None):
    n_in, n_out, n_scr = len(ins), len(out_shape), len(scratch)
    if comm is None:
        outs = pl.pallas_call(body, grid=grid, in_specs=list(in_specs), out_specs=list(out_specs),
                              out_shape=list(out_shape), scratch_shapes=list(scratch), name=name,
                              compiler_params=_params(grid))(*ins)
        return list(outs), []
    c_ins, c_shapes, c_scratch = comm.operands()
    nci, nco = len(c_ins), len(c_shapes)
    anywhere = pl.BlockSpec(memory_space=pl.ANY)

    def carrier(*refs):
        main_in, comm_in = refs[:n_in], refs[n_in:n_in + nci]
        o0 = n_in + nci
        main_out, comm_out = refs[o0:o0 + n_out], refs[o0 + n_out:o0 + n_out + nco]
        s0 = o0 + n_out + nco
        main_scr, comm_scr = refs[s0:s0 + n_scr], refs[s0 + n_scr:]
        first = pl.program_id(0) == 0
        last = pl.program_id(0) == grid[0] - 1
        for ax in range(1, len(grid)):
            first = first & (pl.program_id(ax) == 0)
            last = last & (pl.program_id(ax) == grid[ax] - 1)

        @pl.when(first)
        def _():
            comm.start(comm_in, comm_out, comm_scr)

        body(*main_in, *main_out, *main_scr)

        @pl.when(last)
        def _():
            comm.finish(comm_in, comm_out, comm_scr)

    outs = pl.pallas_call(carrier, grid=grid, in_specs=list(in_specs) + [anywhere] * nci,
                          out_specs=list(out_specs) + [anywhere] * nco, out_shape=list(out_shape) + c_shapes,
                          scratch_shapes=list(scratch) + c_scratch, name=name, compiler_params=_params(grid))(*ins, *c_ins)
    return list(outs[:n_out]), list(outs[n_out:])


def _comm_only(name, comm):
    c_ins, c_shapes, c_scratch = comm.operands()
    nci, nco = len(c_ins), len(c_shapes)
    anywhere = pl.BlockSpec(memory_space=pl.ANY)

    def body(*refs):
        comm.start(refs[:nci], refs[nci:nci + nco], refs[nci + nco:])
        comm.finish(refs[:nci], refs[nci:nci + nco], refs[nci + nco:])

    return list(pl.pallas_call(body, in_specs=[anywhere] * nci, out_specs=[anywhere] * nco, out_shape=c_shapes,
                               scratch_shapes=c_scratch, name=name)(*c_ins))


def _stage_fwd(name, f, grid, ins, in_specs, out_shapes, out_specs):
    n_in = len(ins)

    def body(*refs):
        res = f(*[r[...] for r in refs[:n_in]])
        for o_ref, val in zip(refs[n_in:], res):
            o_ref[...] = val.astype(o_ref.dtype)

    return _call(name, body, grid=grid, ins=ins, in_specs=in_specs, out_shape=out_shapes, out_specs=out_specs)[0]


def _stage_bwd(name, f, grid, ins, in_specs, cts, ct_specs, grads):
    n_in = len(ins)
    flat_cts = [c for group in cts for c in group]
    flat_ct_specs = [s for group, spec in zip(cts, ct_specs) for s in (spec,) * len(group)]
    n_ct = len(flat_cts)
    diff = [g[0] for g in grads]

    def body(*refs):
        vals = [r[...] for r in refs[:n_in]]
        ct_refs = refs[n_in:n_in + n_ct]
        g_refs = refs[n_in + n_ct:]
        ct_vals, pos = [], 0
        for group in cts:
            acc = ct_refs[pos][...].astype(F32)
            for j in range(1, len(group)):
                acc = acc + ct_refs[pos + j][...].astype(F32)
            ct_vals.append(acc)
            pos += len(group)

        def g_fn(*dvals):
            full = list(vals)
            for i, dv in zip(diff, dvals):
                full[i] = dv
            return f(*full)

        _, vjp = jax.vjp(g_fn, *[vals[i] for i in diff])
        gvals = vjp(tuple(ct_vals))
        for gspec, g_ref, gval in zip(grads, g_refs, gvals):
            acc_axes = gspec[1]
            if not acc_axes:
                g_ref[...] = gval.astype(g_ref.dtype)
            else:
                first = pl.program_id(acc_axes[0]) == 0
                for ax in acc_axes[1:]:
                    first = first & (pl.program_id(ax) == 0)

                @pl.when(first)
                def _():
                    g_ref[...] = jnp.zeros_like(g_ref)

                g_ref[...] += gval.astype(g_ref.dtype)

    out_shapes, out_specs = [], []
    for gspec in grads:
        shape, spec = gspec[3] if len(gspec) > 3 else (ins[gspec[0]].shape, in_specs[gspec[0]])
        out_shapes.append(jax.ShapeDtypeStruct(shape, gspec[2]))
        out_specs.append(spec)
    return _call(name, body, grid=grid, ins=list(ins) + flat_cts, in_specs=list(in_specs) + flat_ct_specs,
                 out_shape=out_shapes, out_specs=out_specs)[0]


def _pick_tile(n, candidates):
    for c in candidates:
        if n % c == 0:
            return c
    return n


def _matmul(name, a, b, *, ta=False, tb=False, add=None, out_dtype=F32, comm=None):
    if ta:
        k_dim, m = a.shape
    else:
        m, k_dim = a.shape
    n = b.shape[0] if tb else b.shape[1]
    assert (b.shape[1] if tb else b.shape[0]) == k_dim and not (ta and tb)
    tm = _pick_tile(m, (1024, 512, 256, 128))
    tn = _pick_tile(n, (1024, 1408, 512, 256, 128))
    if ta:
        tk = _pick_tile(k_dim, (1024, 512, 256, 128))
    elif k_dim <= 2816:
        tk = k_dim
    else:
        tk = _pick_tile(k_dim, (1408, 512, 256, 128))
    nk = k_dim // tk
    grid = (m // tm, n // tn, nk)
    a_spec = pl.BlockSpec((tk, tm), lambda i, j, k: (k, i)) if ta else pl.BlockSpec((tm, tk), lambda i, j, k: (i, k))
    b_spec = pl.BlockSpec((tn, tk), lambda i, j, k: (j, k)) if tb else pl.BlockSpec((tk, tn), lambda i, j, k: (k, j))
    o_spec = pl.BlockSpec((tm, tn), lambda i, j, k: (i, j))
    dims = (((0 if ta else 1,), (1 if tb else 0,)), ((), ()))
    has_add = add is not None

    def body(*refs):
        a_ref, b_ref = refs[0], refs[1]
        add_ref = refs[2] if has_add else None
        o_ref, acc_ref = refs[-2], refs[-1]
        k = pl.program_id(2)
        part = lax.dot_general(a_ref[...].astype(BF16), b_ref[...].astype(BF16), dims, preferred_element_type=F32)

        def finish(res):
            if has_add:
                res = res + add_ref[...].astype(F32)
            o_ref[...] = res.astype(o_ref.dtype)

        if nk == 1:
            finish(part)
        else:
            @pl.when(k == 0)
            def _():
                acc_ref[...] = part

            @pl.when((k > 0) & (k < nk - 1))
            def _():
                acc_ref[...] += part

            @pl.when(k == nk - 1)
            def _():
                finish(acc_ref[...] + part)

    ins = [a, b] + ([add] if has_add else [])
    in_specs = [a_spec, b_spec] + ([o_spec] if has_add else [])
    acc_shape = (tm, tn) if nk > 1 else (8, 128)
    outs, comm_outs = _call(name, body, grid=grid, ins=ins, in_specs=in_specs,
                            out_shape=[jax.ShapeDtypeStruct((m, n), out_dtype)], out_specs=[o_spec],
                            scratch=[pltpu.VMEM(acc_shape, F32)], comm=comm)
    return outs[0], comm_outs


SSD_STATE = (SSM_GROUPS * SSM_STATE, SSM_RPG * SSM_HEADDIM)


def _ssd_fwd(xc, proj, dt_bias, a_log, d_skip, norm_g, nb, nc, comm=None):
    t = xc.shape[0]
    row = lambda b, c: (b * nc + c, 0)
    par = lambda shape: pl.BlockSpec(shape, lambda b, c: (0, 0))

    def body(xc_ref, dt_ref, z_ref, dtb_ref, al_ref, ds_ref, ng_ref, y_ref, ypre_ref, prev_ref, st_ref):
        @pl.when(pl.program_id(1) == 0)
        def _():
            st_ref[...] = jnp.zeros_like(st_ref)

        prev = st_ref[...]
        prev_ref[0, 0] = prev
        y, y_pre, new_state = _ssd_forward(xc_ref[...], dt_ref[...], z_ref[...], prev, dtb_ref[...], al_ref[...],
                                           ds_ref[...], ng_ref[...])
        y_ref[...] = y.astype(y_ref.dtype)
        ypre_ref[...] = y_pre
        st_ref[...] = new_state

    return _call(
        "ssd_fwd", body, grid=(nb, nc), ins=[xc, proj, proj, dt_bias, a_log, d_skip, norm_g],
        in_specs=[pl.BlockSpec((CHUNK, SSM_CONV_DIM), row),
                  pl.BlockSpec((CHUNK, 128), lambda b, c: (b * nc + c, DT_COL // 128)),
                  pl.BlockSpec((CHUNK, SSM_INNER), lambda b, c: (b * nc + c, 1)),
                  par((1, 128)), par((1, 128)), par((1, SSM_INNER)), par((1, SSM_INNER))],
        out_specs=[pl.BlockSpec((CHUNK, SSM_INNER), row), pl.BlockSpec((CHUNK, SSM_INNER), row),
                   pl.BlockSpec((1, 1) + SSD_STATE, lambda b, c: (b, c, 0, 0))],
        out_shape=[jax.ShapeDtypeStruct((t, SSM_INNER), BF16), jax.ShapeDtypeStruct((t, SSM_INNER), F32),
                   jax.ShapeDtypeStruct((nb, nc) + SSD_STATE, F32)],
        scratch=[pltpu.VMEM(SSD_STATE, F32)], comm=comm)


def _ssd_bwd(xc, proj, prevs, y_pre, dt_bias, a_log, d_skip, norm_g, dy, nb, nc, comm=None):
    t = xc.shape[0]
    row = lambda b, c: (b * nc + (nc - 1 - c), 0)
    par = lambda shape: pl.BlockSpec(shape, lambda b, c: (0, 0))

    def body(xc_ref, dt_ref, z_ref, prev_ref, ypre_ref, dtb_ref, al_ref, ds_ref, ng_ref, dy_ref,
             dxc_ref, ddt_ref, dz_ref, ddtb_ref, dal_ref, dds_ref, dng_ref, dst_ref):
        @pl.when(pl.program_id(1) == 0)
        def _():
            dst_ref[...] = jnp.zeros_like(dst_ref)

        @pl.when((pl.program_id(0) == 0) & (pl.program_id(1) == 0))
        def _():
            ddtb_ref[...] = jnp.zeros_like(ddtb_ref)
            dal_ref[...] = jnp.zeros_like(dal_ref)
            dds_ref[...] = jnp.zeros_like(dds_ref)
            dng_ref[...] = jnp.zeros_like(dng_ref)

        dxc, ddt, dz, dprev, ddtb, dal, dds, dng = _ssd_backward(
            xc_ref[...], dt_ref[...], z_ref[...], prev_ref[0, 0], ypre_ref[...], dtb_ref[...], al_ref[...], ds_ref[...],
            ng_ref[...], dy_ref[...].astype(F32), dst_ref[...])
        dxc_ref[...] = dxc
        ddt_ref[:, :128] = ddt.astype(ddt_ref.dtype)
        ddt_ref[:, 128:] = jnp.zeros((CHUNK, DT_PAD - 128), ddt_ref.dtype)
        dz_ref[...] = dz.astype(dz_ref.dtype)
        dst_ref[...] = dprev
        ddtb_ref[...] += ddtb
        dal_ref[...] += dal
        dds_ref[...] += dds
        dng_ref[...] += dng

    return _call(
        "ssd_bwd", body, grid=(nb, nc), ins=[xc, proj, proj, prevs, y_pre, dt_bias, a_log, d_skip, norm_g, dy],
        in_specs=[pl.BlockSpec((CHUNK, SSM_CONV_DIM), row),
                  pl.BlockSpec((CHUNK, 128), lambda b, c: (b * nc + (nc - 1 - c), DT_COL // 128)),
                  pl.BlockSpec((CHUNK, SSM_INNER), lambda b, c: (b * nc + (nc - 1 - c), 1)),
                  pl.BlockSpec((1, 1) + SSD_STATE, lambda b, c: (b, nc - 1 - c, 0, 0)),
                  pl.BlockSpec((CHUNK, SSM_INNER), row),
                  par((1, 128)), par((1, 128)), par((1, SSM_INNER)), par((1, SSM_INNER)),
                  pl.BlockSpec((CHUNK, SSM_INNER), row)],
        out_specs=[pl.BlockSpec((CHUNK, SSM_CONV_DIM), row), pl.BlockSpec((CHUNK, DT_PAD), row),
                   pl.BlockSpec((CHUNK, SSM_INNER), row), par((1, 128)), par((1, 128)), par((1, 128)),
                   par((1, SSM_INNER))],
        out_shape=[jax.ShapeDtypeStruct((t, SSM_CONV_DIM), F32), jax.ShapeDtypeStruct((t, DT_PAD), BF16),
                   jax.ShapeDtypeStruct((t, SSM_INNER), BF16), jax.ShapeDtypeStruct((1, 128), F32),
                   jax.ShapeDtypeStruct((1, 128), F32), jax.ShapeDtypeStruct((1, 128), F32),
                   jax.ShapeDtypeStruct((1, SSM_INNER), F32)],
        scratch=[pltpu.VMEM(SSD_STATE, F32)], comm=comm)


def _loss_head(y, target):
    t, d = y.shape
    tm = _pick_tile(t, (256,))
    blk = pl.BlockSpec((tm, d), lambda i: (i, 0))

    def body(y_ref, t_ref, loss_ref, dy_ref):
        err = y_ref[...] - t_ref[...]
        dy_ref[...] = err * (1.0 / d)

        @pl.when(pl.program_id(0) == 0)
        def _():
            loss_ref[...] = jnp.zeros_like(loss_ref)

        loss_ref[...] += 0.5 * jnp.sum(jnp.mean(err * err, axis=-1, keepdims=True), axis=0, keepdims=True)

    return _call("loss_head", body, grid=(t // tm,), ins=[y, target], in_specs=[blk, blk],
                 out_specs=[pl.BlockSpec((1, 1), lambda i: (0, 0)), blk],
                 out_shape=[jax.ShapeDtypeStruct((1, 1), F32), jax.ShapeDtypeStruct((t, d), F32)])[0]


def _adamw_math(g, w, m, v):
    m_new = ADAM_B1 * m + (1.0 - ADAM_B1) * g
    v_new = ADAM_B2 * v + (1.0 - ADAM_B2) * jnp.square(g)
    m_hat = m_new / (1.0 - ADAM_B1 ** ADAM_STEP)
    v_hat = v_new / (1.0 - ADAM_B2 ** ADAM_STEP)
    delta = -ADAM_LR * (m_hat / (jnp.sqrt(v_hat) + ADAM_EPS) + ADAM_WD * w)
    return delta, m_new, v_new


def _adamw_sharded(name, parts, w, m, v):
    _, a, b = w.shape
    tr = _pick_tile(a, (128,))
    nt = a // tr
    part_specs = [pl.BlockSpec((N_DEV, tr, b),
                               (lambda l, i, _k=k: (0, jnp.where(l == _k, i, jnp.where(l > _k, nt - 1, 0)), 0)))
                  for k in range(DEPTH)]
    blk = pl.BlockSpec((1, tr, b), lambda l, i: (l, i, 0))

    def body(*refs):
        p_refs = refs[:DEPTH]
        w_ref, m_ref, v_ref, g_out, d_out, m_out, v_out = refs[DEPTH:]
        for k in range(DEPTH):
            @pl.when(pl.program_id(0) == k)
            def _(p_ref=p_refs[k]):
                g = p_ref[0].astype(F32)
                for p in range(1, N_DEV):
                    g = g + p_ref[p].astype(F32)
                delta, m_new, v_new = _adamw_math(g, w_ref[0], m_ref[0], v_ref[0])
                g_out[0] = g
                d_out[0] = delta
                m_out[0] = m_new
                v_out[0] = v_new

    return _call(name, body, grid=(DEPTH, nt), ins=list(parts) + [w, m, v], in_specs=part_specs + [blk, blk, blk],
                 out_specs=[blk] * 4, out_shape=[jax.ShapeDtypeStruct(w.shape, F32)] * 4)[0]


def _adamw_small(name, g, w, m, v):
    full = pl.BlockSpec(w.shape, lambda i: (0, 0))

    def body(g_ref, w_ref, m_ref, v_ref, d_out, m_out, v_out):
        delta, m_new, v_new = _adamw_math(g_ref[...], w_ref[...], m_ref[...], v_ref[...])
        d_out[...] = delta
        m_out[...] = m_new
        v_out[...] = v_new

    return _call(name, body, grid=(1,), ins=[g, w, m, v], in_specs=[full] * 4, out_specs=[full] * 3,
                 out_shape=[jax.ShapeDtypeStruct(w.shape, F32)] * 3)[0]


def _sum_parts(name, parts):
    n_parts, rows, cols = parts.shape
    tr = _pick_tile(rows, (512, 256, 128, 64, 32, 16, 8))

    def body(p_ref, o_ref):
        acc = p_ref[0]
        for p in range(1, n_parts):
            acc = acc + p_ref[p]
        o_ref[...] = acc

    return _call(name, body, grid=(rows // tr,), ins=[parts],
                 in_specs=[pl.BlockSpec((n_parts, tr, cols), lambda i: (0, i, 0))],
                 out_specs=[pl.BlockSpec((tr, cols), lambda i: (i, 0))],
                 out_shape=[jax.ShapeDtypeStruct((rows, cols), parts.dtype)])[0][0]


def _pack_w_in(w):
    pad = jnp.zeros((w.shape[0], DT_PAD - SSM_HEADS), w.dtype)
    return jnp.concatenate([w[:, :DT_COL], w[:, DT_COL:GA_COL], pad, w[:, GA_COL:]], axis=1)


def _unpack_w_in(w):
    return jnp.concatenate([w[:, :DT_COL + SSM_HEADS], w[:, P_GA:]], axis=1)


def _pad_heads(v):
    return jnp.pad(v, (0, 128 - SSM_HEADS)).reshape(1, 128)


def _run_step(x, mem, target, small, ex):
    nb, s, d = x.shape
    t = nb * s
    nc = s // CHUNK
    rows = _pick_tile(t, (256,))
    rows_wide = _pick_tile(t, (512, 256))
    tq = _pick_tile(s, (512, 256))
    vec = lambda a: a.reshape(1, -1)
    full1 = lambda shape: pl.BlockSpec(shape, lambda i: (0,) * len(shape))
    row1 = lambda tm, w: pl.BlockSpec((tm, w), lambda i: (i, 0))
    sds = jax.ShapeDtypeStruct

    def mm(call, l, a, b, **kw):
        comm = ex.before(call, l)
        out, comm_outs = _matmul(call, a, b, comm=comm, **kw)
        if comm is not None:
            ex.after(call, l, comm_outs)
        return out

    mem_specs = [row1(256, d), full1((1, d)), full1((1, d))]
    mem_ins = [mem.reshape(nb * MEM_LEN, d), vec(small["mem_ln_g"]), vec(small["mem_ln_b"])]
    (mem_n,) = _stage_fwd("memln_fwd", _memln_block, (nb * MEM_LEN // 256,), mem_ins, mem_specs,
                          [sds((nb * MEM_LEN, d), BF16)], [row1(256, d)])

    h = x.reshape(t, d)
    h_bf = h.astype(BF16)
    ln_specs = [row1(rows, d), row1(rows, d), full1((1, d)), full1((1, d))]
    ln_outs = [sds((t, d), F32), sds((t, d), BF16)]
    ln_out_specs = [row1(rows, d), row1(rows, d)]
    saved = []
    for l in range(DEPTH):
        sv = {"h_bf": h_bf}
        proj = mm("mm_in", l, h_bf, ex.weight("w_in", l))
        sv["proj"] = proj
        sgu_ins = [proj, proj, vec(small["sg_ln_g"][l]), vec(small["sg_ln_b"][l]), small["sg_w"][l], small["sg_b"][l].T]
        sgu_specs = [pl.BlockSpec((CHUNK, d), lambda i: (i, 0)), pl.BlockSpec((CHUNK, d), lambda i: (i, 1)),
                     full1((1, d)), full1((1, d)), full1((SG_GROUPS, CHUNK, CHUNK)), full1((CHUNK, SG_GROUPS))]
        (a_out,) = _stage_fwd("sgu_fwd", _sgu_block, (t // CHUNK,), sgu_ins, sgu_specs, [sds((t, d), BF16)],
                              [row1(CHUNK, d)])
        sv["sgu"] = (sgu_ins, sgu_specs)
        sv["a_out"] = a_out
        cw = 256
        conv_ins = [proj, small["conv_w"][l], vec(small["conv_b"][l])]
        conv_specs = [pl.BlockSpec((s, cw), lambda j, b: (b, XBC_COL // cw + j)),
                      pl.BlockSpec((SSM_CONV, cw), lambda j, b: (0, j)), pl.BlockSpec((1, cw), lambda j, b: (0, j))]
        conv_out_spec = pl.BlockSpec((s, cw), lambda j, b: (b, j))
        (xc,) = _stage_fwd("conv_fwd", _conv_block, (SSM_CONV_DIM // cw, nb), conv_ins, conv_specs,
                           [sds((t, SSM_CONV_DIM), F32)], [conv_out_spec])
        sv["conv"] = (conv_ins, conv_specs, conv_out_spec)
        ssd_par = [_pad_heads(small["dt_bias"][l]), _pad_heads(small["a_log"][l]),
                   vec(jnp.repeat(small["d_skip"][l], SSM_HEADDIM)), vec(small["ssm_norm_g"][l])]
        comm = ex.before("ssd_fwd", l)
        (y_ssd, y_pre, prevs), comm_outs = _ssd_fwd(xc, proj, *ssd_par, nb, nc, comm=comm)
        if comm is not None:
            ex.after("ssd_fwd", l, comm_outs)
        sv["ssd"] = (xc, prevs, y_pre, ssd_par)
        sv["y_ssd"] = y_ssd
        br_a = mm("mm_sq", l, a_out, ex.weight("p_a", l))
        br_b = mm("mm_pb", l, y_ssd, ex.weight("p_b", l))
        mw = 512
        merge_ins = [proj, proj, br_a, br_b]
        merge_out_spec = pl.BlockSpec((rows_wide, mw), lambda i, j: (i, j))
        merge_specs = [pl.BlockSpec((rows_wide, mw), lambda i, j: (i, P_GA // mw + j)),
                       pl.BlockSpec((rows_wide, mw), lambda i, j: (i, P_GB // mw + j)), merge_out_spec, merge_out_spec]
        (merged,) = _stage_fwd("merge_fwd", _merge_block, (t // rows_wide, d // mw), merge_ins, merge_specs,
                               [sds((t, d), BF16)], [merge_out_spec])
        sv["merge"] = (merge_ins, merge_specs, merge_out_spec)
        sv["merged"] = merged
        y1 = mm("mm_sq", l, merged, ex.weight("w_mix_o", l))
        ln1_ins = [h, y1, vec(small["ln_g"][l, 0]), vec(small["ln_b"][l, 0])]
        h1, h1_bf = _stage_fwd("lnres_fwd", _lnres_block_twice, (t // rows,), ln1_ins, ln_specs, ln_outs, ln_out_specs)
        sv["ln1"] = ln1_ins
        q = mm("mm_sq", l, h1_bf, ex.weight("w_xq", l))
        kv = mm("mm_kv", l, mem_n, ex.weight("w_xkv", l))
        attn_ins = [q, kv]
        attn_out_spec = pl.BlockSpec((tq, d), lambda b, i: (b * (s // tq) + i, 0))
        attn_specs = [attn_out_spec, pl.BlockSpec((MEM_LEN, 2 * d), lambda b, i: (b, 0))]
        (o,) = _stage_fwd("attn_fwd", _attn_block, (nb, s // tq), attn_ins, attn_specs, [sds((t, d), BF16)],
                          [attn_out_spec])
        sv["attn"] = (attn_ins, attn_specs, attn_out_spec)
        sv["o"] = o
        sv["h1_bf"] = h1_bf
        y2 = mm("mm_sq", l, o, ex.weight("w_xo", l))
        ln2_ins = [h1, y2, vec(small["ln_g"][l, 1]), vec(small["ln_b"][l, 1])]
        h2, h2_bf = _stage_fwd("lnres_fwd", _lnres_block_twice, (t // rows,), ln2_ins, ln_specs, ln_outs, ln_out_specs)
        sv["ln2"] = ln2_ins
        sv["h2_bf"] = h2_bf
        gu = mm("mm_ffn_in", l, h2_bf, ex.weight("w_ffn_in", l))
        (act,) = _stage_fwd("swiglu_fwd", _swiglu_block, (t // 128,), [gu], [row1(128, 2 * FFN_HIDDEN)],
                            [sds((t, FFN_HIDDEN), BF16)], [row1(128, FFN_HIDDEN)])
        sv["gu"] = gu
        sv["act"] = act
        y3 = mm("mm_ffn_out", l, act, ex.weight("w_ffn_out", l))
        ln3_ins = [h2, y3, vec(small["ln_g"][l, 2]), vec(small["ln_b"][l, 2])]
        h, h_bf = _stage_fwd("lnres_fwd", _lnres_block_twice, (t // rows,), ln3_ins, ln_specs, ln_outs, ln_out_specs)
        sv["ln3"] = ln3_ins
        saved.append(sv)

    loss, dh = _loss_head(h, target.reshape(t, d))

    g_small = {n: [None] * DEPTH for n in SMALL_REP + SMALL_SH if n not in ("mem_ln_g", "mem_ln_b")}
    dmem_n = []
    ln_grads = [(0, (), F32), (1, (), BF16), (2, (0,), F32), (3, (0,), F32)]
    for l in reversed(range(DEPTH)):
        sv = saved[l]
        dln_g, dln_b = [None] * 3, [None] * 3
        dres, dy3, dln_g[2], dln_b[2] = _stage_bwd("lnres_bwd", _lnres_block, (t // rows,), sv["ln3"], ln_specs,
                                                   [(dh,)], [row1(rows, d)], ln_grads)
        ex.grad("w_ffn_out", l, mm("mm_ffn_out_dw", l, sv["act"], dy3, ta=True, out_dtype=BF16))
        dact = mm("mm_ffn_out_dx", l, dy3, ex.weight("w_ffn_out", l), tb=True)
        (dgu,) = _stage_bwd("swiglu_bwd", _swiglu_block, (t // 128,), [sv["gu"]], [row1(128, 2 * FFN_HIDDEN)],
                            [(dact,)], [row1(128, FFN_HIDDEN)], [(0, (), BF16)])
        ex.grad("w_ffn_in", l, mm("mm_ffn_in_dw", l, sv["h2_bf"], dgu, ta=True, out_dtype=BF16))
        dh2 = mm("mm_ffn_in_dx", l, dgu, ex.weight("w_ffn_in", l), tb=True, add=dres)
        dres, dy2, dln_g[1], dln_b[1] = _stage_bwd("lnres_bwd", _lnres_block, (t // rows,), sv["ln2"], ln_specs,
                                                   [(dh2,)], [row1(rows, d)], ln_grads)
        ex.grad("w_xo", l, mm("mm_sq_dw", l, sv["o"], dy2, ta=True, out_dtype=BF16))
        do = mm("mm_sq_dx", l, dy2, ex.weight("w_xo", l), tb=True)
        attn_ins, attn_specs, attn_out_spec = sv["attn"]
        dq, dkv = _stage_bwd("attn_bwd", _attn_block, (nb, s // tq), attn_ins, attn_specs, [(do,)], [attn_out_spec],
                             [(0, (), BF16), (1, (1,), F32)])
        ex.grad("w_xq", l, mm("mm_sq_dw", l, sv["h1_bf"], dq, ta=True, out_dtype=BF16))
        dh1 = mm("mm_sq_dx", l, dq, ex.weight("w_xq", l), tb=True, add=dres)
        ex.grad("w_xkv", l, mm("mm_kv_dw", l, mem_n, dkv, ta=True, out_dtype=BF16))
        dmem_n.append(mm("mm_kv_dx", l, dkv, ex.weight("w_xkv", l), tb=True))
        dres, dy1, dln_g[0], dln_b[0] = _stage_bwd("lnres_bwd", _lnres_block, (t // rows,), sv["ln1"], ln_specs,
                                                   [(dh1,)], [row1(rows, d)], ln_grads)
        g_small["ln_g"][l] = jnp.concatenate(dln_g, axis=0)
        g_small["ln_b"][l] = jnp.concatenate(dln_b, axis=0)
        ex.grad("w_mix_o", l, mm("mm_sq_dw", l, sv["merged"], dy1, ta=True, out_dtype=BF16))
        dmerged = mm("mm_sq_dx", l, dy1, ex.weight("w_mix_o", l), tb=True)
        merge_ins, merge_specs, merge_out_spec = sv["merge"]
        dga, dgb, dbr_a, dbr_b = _stage_bwd("merge_bwd", _merge_block, (t // rows_wide, d // 512), merge_ins, merge_specs,
                                            [(dmerged,)], [merge_out_spec],
                                            [(i, (), BF16, ((t, d), merge_out_spec)) for i in range(4)])
        ex.grad("p_a", l, mm("mm_sq_dw", l, sv["a_out"], dbr_a, ta=True, out_dtype=BF16))
        da_out = mm("mm_sq_dx", l, dbr_a, ex.weight("p_a", l), tb=True)
        ex.grad("p_b", l, mm("mm_pb_dw", l, sv["y_ssd"], dbr_b, ta=True, out_dtype=BF16))
        dy_ssd = mm("mm_pb_dx", l, dbr_b, ex.weight("p_b", l), tb=True)
        sgu_ins, sgu_specs = sv["sgu"]
        du, dv, dsg_ln_g, dsg_ln_b, dsg_w, dsg_b = _stage_bwd(
            "sgu_bwd", _sgu_block, (t // CHUNK,), sgu_ins, sgu_specs, [(da_out,)], [row1(CHUNK, d)],
            [(0, (), BF16, ((t, d), row1(CHUNK, d))), (1, (), BF16, ((t, d), row1(CHUNK, d))), (2, (0,), F32),
             (3, (0,), F32), (4, (0,), F32), (5, (0,), F32)])
        g_small["sg_ln_g"][l], g_small["sg_ln_b"][l], g_small["sg_w"][l], g_small["sg_b"][l] = (
            dsg_ln_g[0], dsg_ln_b[0], dsg_w, dsg_b.T)
        xc, prevs, y_pre, ssd_par = sv["ssd"]
        comm = ex.before("ssd_bwd", l)
        (dxc, ddt, dz, ddtb, dal, dds, dng), comm_outs = _ssd_bwd(xc, sv["proj"], prevs, y_pre, *ssd_par, dy_ssd, nb, nc,
                                                                  comm=comm)
        if comm is not None:
            ex.after("ssd_bwd", l, comm_outs)
        g_small["dt_bias"][l], g_small["a_log"][l], g_small["d_skip"][l] = (
            ddtb[0, :SSM_HEADS], dal[0, :SSM_HEADS], dds[0, :SSM_HEADS])
        g_small["ssm_norm_g"][l] = dng[0]
        conv_ins, conv_specs, conv_out_spec = sv["conv"]
        dxbc, dconv_w, dconv_b = _stage_bwd("conv_bwd", _conv_block, (SSM_CONV_DIM // 256, nb), conv_ins, conv_specs,
                                            [(dxc,)], [conv_out_spec],
                                            [(0, (), BF16, ((t, SSM_CONV_DIM), conv_out_spec)), (1, (1,), F32),
                                             (2, (1,), F32)])
        g_small["conv_w"][l], g_small["conv_b"][l] = dconv_w, dconv_b[0]
        dproj = jnp.concatenate([du, dv, dz, dxbc, ddt, dga, dgb], axis=1)
        ex.grad("w_in", l, mm("mm_in_dw", l, sv["h_bf"], dproj, ta=True, out_dtype=BF16))
        if l == 0:
            dmg, dmb = _stage_bwd("memln_bwd", _memln_block, (nb * MEM_LEN // 256,), mem_ins, mem_specs,
                                  [tuple(dmem_n)], [row1(256, d)], [(1, (0,), F32), (2, (0,), F32)])
            done = {n: jnp.stack(g, axis=0) for n, g in g_small.items()}
            done["mem_ln_g"], done["mem_ln_b"] = dmg[0], dmb[0]
            ex.small_grads(done)
        dh = mm("mm_in_dx", l, dproj, ex.weight("w_in", l), tb=True, add=dres)

    return loss, dh.reshape(nb, s, d)


def _pack_flat(arrays, rows):
    flat = jnp.concatenate([a.reshape(-1) for a in arrays])
    return jnp.pad(flat, (0, rows * 128 - flat.shape[0])).reshape(rows, 128)


def _unpack_flat(packed, shapes):
    lead = packed.shape[:-2]
    flat = packed.reshape(lead + (-1,))
    out, pos = [], 0
    for shape in shapes:
        n = math.prod(shape)
        out.append(flat[..., pos:pos + n].reshape(lead + tuple(shape)))
        pos += n
    return out


def _small_rows(n_elems):
    return -(-n_elems // (128 * SMALL_ROW_TILE)) * SMALL_ROW_TILE


def _from_shards(name, gathered):
    _, a, b = gathered.shape
    if name in BIG_COL_SHARDED:
        w = gathered.transpose(1, 0, 2).reshape(a, N_DEV * b)
        return _pack_w_in(w) if name == "w_in" else w
    return gathered.reshape(N_DEV * a, b)


def _to_shards(name, g):
    if name in BIG_COL_SHARDED:
        g = _unpack_w_in(g) if name == "w_in" else g
        a, nb = g.shape
        return g.reshape(a, N_DEV, nb // N_DEV).transpose(1, 0, 2)
    a, b = g.shape
    return g.reshape(N_DEV, a // N_DEV, b)


class _MeshExchange:
    def __init__(self, shards_bf16, first):
        self.shards = shards_bf16
        self.full = dict(first)
        self.grads = {}
        self.received = {}
        self.small = None
        self.small_gathered = None

    def weight(self, name, l):
        return self.full[(name, l)]

    def grad(self, name, l, g):
        self.grads[(name, l)] = g

    def small_grads(self, done):
        self.small = done

    def before(self, call, l):
        comm = _Comm()
        for name, layer in GATHER_PLAN.get((call, l), ()):
            comm.gathers.append((self.shards[name], layer))
        for name, layer in SCATTER_PLAN.get((call, l), ()):
            comm.scatters.append(_to_shards(name, self.grads[(name, layer)]))
        if (call, l) == ("mm_in_dx", 0):
            names = SMALL_REP + SMALL_SH
            rows = _small_rows(sum(math.prod(self.small[n].shape) for n in names))
            comm.gathers.append((_pack_flat([self.small[n] for n in names], rows), None))
        return comm if comm.gathers or comm.scatters else None

    def after(self, call, l, outs):
        gathers = list(GATHER_PLAN.get((call, l), ()))
        for (name, layer), out in zip(gathers, outs):
            self.full[(name, layer)] = _from_shards(name, out)
        outs = outs[len(gathers):]
        if (call, l) == ("mm_in_dx", 0):
            self.small_gathered = outs[0]
            outs = outs[1:]
        for (name, layer), out in zip(SCATTER_PLAN.get((call, l), ()), outs):
            self.received[(name, layer)] = out


def kernel(x, mem, mem_ln_g, mem_ln_b, w_in, sg_ln_g, sg_ln_b, sg_w, sg_b, conv_w, conv_b, dt_bias, a_log, d_skip, ssm_norm_g, p_a, p_b, w_mix_o, w_xq, w_xkv, w_xo, w_ffn_in, w_ffn_out, ln_g, ln_b, loss_target, m_mem_ln_g, m_mem_ln_b, m_w_in, m_sg_ln_g, m_sg_ln_b, m_sg_w, m_sg_b, m_conv_w, m_conv_b, m_dt_bias, m_a_log, m_d_skip, m_ssm_norm_g, m_p_a, m_p_b, m_w_mix_o, m_w_xq, m_w_xkv, m_w_xo, m_w_ffn_in, m_w_ffn_out, m_ln_g, m_ln_b, v_mem_ln_g, v_mem_ln_b, v_w_in, v_sg_ln_g, v_sg_ln_b, v_sg_w, v_sg_b, v_conv_w, v_conv_b, v_dt_bias, v_a_log, v_d_skip, v_ssm_norm_g, v_p_a, v_p_b, v_w_mix_o, v_w_xq, v_w_xkv, v_w_xo, v_w_ffn_in, v_w_ffn_out, v_ln_g, v_ln_b):
    args = dict(locals())
    w = {n: args[n] for n in WEIGHTS}
    m = {n: args["m_" + n] for n in WEIGHTS}
    v = {n: args["v_" + n] for n in WEIGHTS}
    me = 4 * lax.axis_index("x") + 2 * lax.axis_index("y") + lax.axis_index("c")

    shards = {n: w[n].astype(BF16) for n in BIG}
    sh_shapes = [w[n].shape for n in SMALL_SH]
    first = _Comm()
    first.gathers.append((shards["w_in"], 0))
    first.gathers.append((_pack_flat([w[n] for n in SMALL_SH], _small_rows(sum(math.prod(s) for s in sh_shapes))), None))
    w_in0, small_sh = _comm_only("gather_first", first)
    small = {n: w[n] for n in SMALL_REP}
    for n, sh in zip(SMALL_SH, _unpack_flat(small_sh, sh_shapes)):
        small[n] = sh.transpose(1, 2, 0, 3).reshape(sh.shape[1], sh.shape[2], N_DEV * sh.shape[3])

    ex = _MeshExchange(shards, {("w_in", 0): _from_shards("w_in", w_in0)})
    loss, grad_x = _run_step(x, mem, loss_target, small, ex)
    loss = lax.psum(loss[0, 0], ("x", "y", "c"))

    out = {}
    for n in BIG:
        out[n] = _adamw_sharded("adamw_" + n, [ex.received[(n, l)] for l in range(DEPTH)], w[n], m[n], v[n])
    names = SMALL_REP + SMALL_SH
    g_small = dict(zip(names, _unpack_flat(_sum_parts("sum_small_grads", ex.small_gathered),
                                           [ex.small[n].shape for n in names])))
    for n in names:
        g = g_small[n]
        if n in SMALL_SH:
            width = w[n].shape[-1]
            g = lax.dynamic_slice_in_dim(g, me * width, width, axis=-1)
        two_d = (-1, w[n].shape[-1])
        res = _adamw_small("adamw_" + n, g.reshape(two_d), w[n].reshape(two_d), m[n].reshape(two_d), v[n].reshape(two_d))
        out[n] = [g] + [r.reshape(w[n].shape) for r in res]

    results = []
    for k in range(4):
        results.extend(out[n][k] for n in WEIGHTS)
    return (loss, grad_x, *results)
```

```python
import functools
import math

import jax
import jax.numpy as jnp
from jax import lax
from jax.experimental import pallas as pl
from jax.experimental.pallas import tpu as pltpu

F32 = jnp.float32
BF16 = jnp.bfloat16
HIGHEST = lax.Precision.HIGHEST

N_DEV = 8
D_MODEL = 1024
DEPTH = 2
MEM_LEN = 256
CHUNK = 128
SG_GROUPS = 8
SSM_INNER = 2048
SSM_HEADDIM = 64
SSM_HEADS = 32
SSM_STATE = 128
SSM_GROUPS = 4
SSM_RPG = 8
SSM_CONV = 4
SSM_CONV_DIM = 3072
X_HEADS = 4
X_HEADDIM = 256
FFN_HIDDEN = 2816
ALPHA = float((2 * DEPTH) ** 0.25)
LN_EPS = 1e-5
RMS_EPS = 1e-5
XBC_COL = 4096
DT_COL = 7168
GA_COL = 7200
DT_PAD = 512
P_GA = DT_COL + DT_PAD
P_GB = P_GA + D_MODEL

ADAM_LR = 0.001
ADAM_B1 = 0.9
ADAM_B2 = 0.999
ADAM_EPS = 1e-08
ADAM_WD = 0.01
ADAM_STEP = 10

VMEM_LIMIT = 48 * 1024 * 1024
SMALL_ROW_TILE = 256

BIG = ("w_in", "p_a", "p_b", "w_mix_o", "w_xq", "w_xkv", "w_xo", "w_ffn_in", "w_ffn_out")
BIG_COL_SHARDED = ("w_in", "w_xkv", "w_ffn_in")
SMALL_REP = ("mem_ln_g", "mem_ln_b", "sg_ln_g", "sg_ln_b", "sg_w", "sg_b", "conv_b", "dt_bias", "a_log", "d_skip",
             "ssm_norm_g")
SMALL_SH = ("conv_w", "ln_g", "ln_b")
WEIGHTS = ("mem_ln_g", "mem_ln_b", "w_in", "sg_ln_g", "sg_ln_b", "sg_w", "sg_b", "conv_w", "conv_b", "dt_bias", "a_log",
           "d_skip", "ssm_norm_g", "p_a", "p_b", "w_mix_o", "w_xq", "w_xkv", "w_xo", "w_ffn_in", "w_ffn_out", "ln_g", "ln_b")

REST = ("p_a", "p_b", "w_mix_o", "w_xq", "w_xkv", "w_xo", "w_ffn_in", "w_ffn_out")
GATHER_PLAN = {
    ("mm_in", 0): [(n, 0) for n in REST],
    ("ssd_fwd", 0): [("w_in", 1)],
    ("mm_ffn_in", 0): [(n, 1) for n in ("p_a", "p_b", "w_mix_o", "w_xq", "w_xkv", "w_xo")],
    ("mm_in", 1): [("w_ffn_in", 1), ("w_ffn_out", 1)],
}
SCATTER_PLAN = {
    ("ssd_bwd", 1): [(n, 1) for n in REST],
    ("mm_in_dx", 1): [("w_in", 1)],
    ("ssd_bwd", 0): [(n, 0) for n in REST],
    ("mm_in_dx", 0): [("w_in", 0)],
}


def _layer_norm(x, g, b):
    mu = jnp.mean(x, axis=-1, keepdims=True)
    xc = x - mu
    var = jnp.mean(xc * xc, axis=-1, keepdims=True)
    return xc * lax.rsqrt(var + LN_EPS) * g + b


def _gelu(x):
    return 0.5 * x * (1.0 + lax.erf(x * (1.0 / math.sqrt(2.0))))


def _silu(x):
    return x * jax.nn.sigmoid(x)


def _softplus(x):
    return jnp.maximum(x, 0.0) + jnp.log1p(jnp.exp(-jnp.abs(x)))


def _causal_mask():
    r = lax.broadcasted_iota(jnp.int32, (CHUNK, CHUNK), 0)
    c = lax.broadcasted_iota(jnp.int32, (CHUNK, CHUNK), 1)
    return r >= c


def _sgu_block(u, v, ln_g, ln_b, w, sb):
    gu = _gelu(u)
    vn = _layer_norm(_gelu(v), ln_g, ln_b)
    causal = _causal_mask()
    width = D_MODEL // SG_GROUPS
    outs = []
    for g in range(SG_GROUPS):
        wg = jnp.where(causal, w[g], 0.0).astype(BF16)
        mixed = jnp.dot(wg, vn[:, g * width:(g + 1) * width].astype(BF16), preferred_element_type=F32)
        outs.append(mixed + sb[:, g:g + 1])
    return (gu * jnp.concatenate(outs, axis=1),)


GROUP_W = SSM_RPG * SSM_HEADDIM
NT_DIMS = (((1,), (1,)), ((), ()))
TN_DIMS = (((0,), (0,)), ((), ()))


def _mxu(a, b, dims=(((1,), (0,)), ((), ()))):
    return lax.dot_general(a.astype(BF16), b.astype(BF16), dims, preferred_element_type=F32)


def _head_expander():
    return (jnp.arange(SSM_INNER)[None, :] // SSM_HEADDIM == jnp.arange(128)[:, None]).astype(BF16)


def _bf16_terms(x, n):
    terms = []
    for _ in range(n):
        t = x.astype(BF16)
        terms.append(t)
        x = x - t.astype(F32)
    return terms


def _expand_heads(q, e):
    return sum(jnp.dot(t, e, preferred_element_type=F32) for t in _bf16_terms(q, 3))


def _reduce_heads(v, e):
    return sum(lax.dot_general(t, e, NT_DIMS, preferred_element_type=F32) for t in _bf16_terms(v, 2))


def _ssd_common(xc, dtraw, dt_bias, a_log, e):
    xs = xc[:, :SSM_INNER]
    pre = dtraw + dt_bias
    dt = _softplus(pre)
    a = -jnp.exp(a_log)
    r_i = lax.broadcasted_iota(jnp.int32, (CHUNK, CHUNK), 0)
    c_i = lax.broadcasted_iota(jnp.int32, (CHUNK, CHUNK), 1)
    tril = jnp.where(r_i >= c_i, 1.0, 0.0).astype(F32)
    cs = jnp.dot(tril, dt * a, precision=HIGHEST, preferred_element_type=F32)
    cs_last = cs[CHUNK - 1:CHUNK, :]
    decay_in = jnp.exp(cs)
    decay_st = jnp.exp(cs_last - cs)
    dt_x = _expand_heads(dt, e)
    w_st_x = _expand_heads(dt * decay_st, e)
    decay_in_x = _expand_heads(decay_in, e)
    return dict(xs=xs, pre=pre, dt=dt, a=a, lower=r_i >= c_i, upper=c_i >= r_i, cs=cs, cs_t=cs.T, decay_in=decay_in,
                decay_st=decay_st, chunk_decay=jnp.exp(cs_last), dt_x=dt_x, w_st_x=w_st_x, decay_in_x=decay_in_x,
                chunk_decay_x=decay_in_x[CHUNK - 1:CHUNK, :], xdt=xs * dt_x, x_st=(xs * w_st_x).astype(BF16),
                low=lax.broadcasted_iota(jnp.int32, (CHUNK, 128), 1) < SSM_HEADDIM)


def _pair_decay(c, h):
    return jnp.exp(jnp.where(c["lower"], c["cs"][:, h:h + 1] - c["cs_t"][h:h + 1, :], -1e30))


def _pair_decay_t(c, h):
    return jnp.exp(jnp.where(c["upper"], c["cs_t"][h:h + 1, :] - c["cs"][:, h:h + 1], -1e30))


def _ssd_forward(xc, dtraw, z, prev, dt_bias, a_log, d_skip_x, norm_g, e):
    c = _ssd_common(xc, dtraw, dt_bias, a_log, e)
    y_groups, new_states = [], []
    for g in range(SSM_GROUPS):
        lanes = slice(g * GROUP_W, (g + 1) * GROUP_W)
        bg = xc[:, SSM_INNER + g * SSM_STATE:SSM_INNER + (g + 1) * SSM_STATE]
        cg = xc[:, SSM_INNER + (SSM_GROUPS + g) * SSM_STATE:SSM_INNER + (SSM_GROUPS + g + 1) * SSM_STATE].astype(BF16)
        pg = prev[g * SSM_STATE:(g + 1) * SSM_STATE, :]
        cb = _mxu(cg, bg, NT_DIMS)
        y_in = _mxu(cg, pg) * c["decay_in_x"][:, lanes]
        new_states.append(pg * c["chunk_decay_x"][:, lanes] + _mxu(bg.T, c["x_st"][:, lanes]))
        pairs = []
        for j in range(SSM_RPG // 2):
            h0 = g * SSM_RPG + 2 * j
            xp = c["xdt"][:, 128 * (h0 // 2):128 * (h0 // 2 + 1)]
            pairs.append(_mxu(cb * _pair_decay(c, h0), jnp.where(c["low"], xp, 0.0))
                         + _mxu(cb * _pair_decay(c, h0 + 1), jnp.where(c["low"], 0.0, xp)))
        y_groups.append(jnp.concatenate(pairs, axis=1) + y_in)
    y_pre = jnp.concatenate(y_groups, axis=1) + c["xs"] * d_skip_x
    gated = y_pre * _silu(z)
    normed = [gated[:, g * GROUP_W:(g + 1) * GROUP_W] for g in range(SSM_GROUPS)]
    normed = [yg * lax.rsqrt(jnp.mean(yg * yg, axis=-1, keepdims=True) + RMS_EPS) for yg in normed]
    return jnp.concatenate(normed, axis=1) * norm_g, y_pre, jnp.concatenate(new_states, axis=0)


def _ssd_backward(xc, dtraw, z, prev, y_pre, dt_bias, a_log, d_skip_x, norm_g, e, dout, dnew):
    c = _ssd_common(xc, dtraw, dt_bias, a_log, e)
    xs = c["xs"]
    sig = jax.nn.sigmoid(z)
    silu_z = z * sig
    gated = y_pre * silu_z
    d_gated, normed = [], []
    for g in range(SSM_GROUPS):
        lanes = slice(g * GROUP_W, (g + 1) * GROUP_W)
        yg = gated[:, lanes]
        r = lax.rsqrt(jnp.mean(yg * yg, axis=-1, keepdims=True) + RMS_EPS)
        n = yg * r
        gh = dout[:, lanes] * norm_g[:, lanes]
        d_gated.append(r * (gh - n * jnp.mean(gh * n, axis=-1, keepdims=True)))
        normed.append(n)
    d_gated = jnp.concatenate(d_gated, axis=1)
    dnorm_g = jnp.sum(dout * jnp.concatenate(normed, axis=1), axis=0, keepdims=True)
    dy = d_gated * silu_z
    dz = d_gated * y_pre * (sig * (1.0 + z * (1.0 - sig)))
    dxs = dy * d_skip_x
    dd_skip = jnp.sum(_reduce_heads(dy * xs, e), axis=0, keepdims=True)

    lane = lax.broadcasted_iota(jnp.int32, (CHUNK, 128), 1)
    sub = lax.broadcasted_iota(jnp.int32, (8, 128), 0)
    dcs_neg = jnp.zeros((CHUNK, 128), F32)
    row_slabs = []
    dxdt, dx_st, d_decay_in_x, dprev, d_chunk_decay_x, db_all, dc_all = [], [], [], [], [], [], []
    for g in range(SSM_GROUPS):
        lanes = slice(g * GROUP_W, (g + 1) * GROUP_W)
        bg = xc[:, SSM_INNER + g * SSM_STATE:SSM_INNER + (g + 1) * SSM_STATE].astype(BF16)
        cg_f = xc[:, SSM_INNER + (SSM_GROUPS + g) * SSM_STATE:SSM_INNER + (SSM_GROUPS + g + 1) * SSM_STATE]
        cg = cg_f.astype(BF16)
        pg = prev[g * SSM_STATE:(g + 1) * SSM_STATE, :]
        dng = dnew[g * SSM_STATE:(g + 1) * SSM_STATE, :]
        dy_g = dy[:, lanes]
        cb_t = _mxu(bg, cg, NT_DIMS)
        t1 = (dy_g * c["decay_in_x"][:, lanes]).astype(BF16)
        d_decay_in_x.append(dy_g * _mxu(cg, pg))
        dc = _mxu(t1, pg, NT_DIMS)
        dprev.append(_mxu(cg_f.T, t1) + dng * c["chunk_decay_x"][:, lanes])
        d_chunk_decay_x.append(dng * pg)
        db = _mxu(c["x_st"][:, lanes], dng, NT_DIMS)
        dx_st.append(_mxu(bg, dng))
        dcb_t = jnp.zeros((CHUNK, CHUNK), F32)
        rows = []
        for j in range(SSM_RPG // 2):
            h0 = g * SSM_RPG + 2 * j
            blk = slice(128 * (h0 // 2), 128 * (h0 // 2 + 1))
            xp = c["xdt"][:, blk]
            dyp = dy[:, blk].astype(BF16)
            pair_dx = []
            for k, xk in enumerate((jnp.where(c["low"], xp, 0.0), jnp.where(c["low"], 0.0, xp))):
                dec_t = _pair_decay_t(c, h0 + k)
                pair_dx.append(_mxu(cb_t * dec_t, dyp))
                dml_t = _mxu(xk, dyp, NT_DIMS) * dec_t
                dcb_t = dcb_t + dml_t
                dseg_t = dml_t * cb_t
                dcs_neg = dcs_neg + jnp.where(lane == h0 + k, jnp.sum(dseg_t, axis=-1, keepdims=True), 0.0)
                rows.append(jnp.sum(dseg_t, axis=0, keepdims=True))
            dxdt.append(jnp.where(c["low"], pair_dx[0], pair_dx[1]))
        slab = jnp.zeros((8, 128), F32)
        for r in range(SSM_RPG):
            slab = slab + jnp.where(sub == r, rows[r], 0.0)
        row_slabs.append(slab)
        dc_all.append(dc + _mxu(dcb_t.T, bg))
        db_all.append(db + _mxu(dcb_t, cg))
    dxdt = jnp.concatenate(dxdt, axis=1)
    dx_st = jnp.concatenate(dx_st, axis=1)
    by_head = jnp.concatenate(row_slabs + [jnp.zeros((CHUNK - SSM_HEADS, 128), F32)], axis=0)
    dcs = by_head.T - dcs_neg
    dxs = dxs + dxdt * c["dt_x"] + dx_st * c["w_st_x"]
    ddt = _reduce_heads(dxdt * xs, e)
    dw_st = _reduce_heads(dx_st * xs, e)
    dcs = dcs + _reduce_heads(jnp.concatenate(d_decay_in_x, axis=1), e) * c["decay_in"]
    ddt = ddt + dw_st * c["decay_st"]
    d_log_st = dw_st * c["dt"] * c["decay_st"]
    dcs = dcs - d_log_st
    d_chunk_decay = jnp.sum(_reduce_heads(jnp.concatenate(d_chunk_decay_x, axis=1), e), axis=0, keepdims=True)
    dcs_last = jnp.sum(d_log_st, axis=0, keepdims=True) + d_chunk_decay * c["chunk_decay"]
    row = lax.broadcasted_iota(jnp.int32, (CHUNK, 128), 0)
    dcs = dcs + jnp.where(row == CHUNK - 1, dcs_last, 0.0)
    triu = jnp.where(c["upper"], 1.0, 0.0).astype(F32)
    dda = jnp.dot(triu, dcs, precision=HIGHEST, preferred_element_type=F32)
    ddt = ddt + dda * c["a"]
    da_log = jnp.sum(dda * c["dt"], axis=0, keepdims=True) * c["a"]
    dpre = ddt * jax.nn.sigmoid(c["pre"])
    dxc = jnp.concatenate([dxs] + db_all + dc_all, axis=1)
    return (dxc, dpre, dz, jnp.concatenate(dprev, axis=0), jnp.sum(dpre, axis=0, keepdims=True), da_log, dd_skip,
            dnorm_g)


def _conv_block(x, w, b):
    rows = lax.broadcasted_iota(jnp.int32, x.shape, 0)
    acc = x * w[SSM_CONV - 1:SSM_CONV, :] + b
    for k in range(SSM_CONV - 1):
        shift = SSM_CONV - 1 - k
        acc = acc + _shift_rows(x, rows, shift) * w[k:k + 1, :]
    return (_silu(acc),)


@functools.partial(jax.custom_vjp, nondiff_argnums=(2,))
def _shift_rows(x, rows, shift):
    return jnp.where(rows >= shift, pltpu.roll(x, shift, 0), 0.0)


def _shift_rows_fwd(x, rows, shift):
    return _shift_rows(x, rows, shift), rows


def _shift_rows_bwd(shift, rows, g):
    n = g.shape[0]
    return jnp.where(rows < n - shift, pltpu.roll(g, n - shift, 0), 0.0), None


_shift_rows.defvjp(_shift_rows_fwd, _shift_rows_bwd)


def _merge_block(ga, gb, br_a, br_b):
    return (jax.nn.sigmoid(ga) * br_a + jax.nn.sigmoid(gb) * br_b,)


def _lnres_block(x, y, g, b):
    return (_layer_norm(ALPHA * x + y, g, b),)


def _lnres_block_twice(x, y, g, b):
    out = _layer_norm(ALPHA * x + y, g, b)
    return out, out


def _memln_block(x, g, b):
    return (_layer_norm(x, g, b),)


def _attn_block(q, kv):
    outs = []
    for h in range(X_HEADS):
        qh = q[:, h * X_HEADDIM:(h + 1) * X_HEADDIM].astype(BF16)
        kh = kv[:, h * X_HEADDIM:(h + 1) * X_HEADDIM].astype(BF16)
        vh = kv[:, D_MODEL + h * X_HEADDIM:D_MODEL + (h + 1) * X_HEADDIM].astype(BF16)
        s = lax.dot_general(qh, kh, (((1,), (1,)), ((), ())), preferred_element_type=F32) * (X_HEADDIM ** -0.5)
        s = s - lax.stop_gradient(jnp.max(s, axis=-1, keepdims=True))
        e = jnp.exp(s)
        p = e / jnp.sum(e, axis=-1, keepdims=True)
        outs.append(jnp.dot(p.astype(BF16), vh, preferred_element_type=F32))
    return (jnp.concatenate(outs, axis=1),)


def _swiglu_block(gu):
    return (_silu(gu[:, :FFN_HIDDEN]) * gu[:, FFN_HIDDEN:],)


class _Comm:
    def __init__(self):
        self.gathers = []
        self.scatters = []

    def operands(self):
        ins = [a for a, _ in self.gathers] + list(self.scatters)
        shapes = [jax.ShapeDtypeStruct((N_DEV,) + (a.shape if idx is None else a.shape[1:]), a.dtype)
                  for a, idx in self.gathers]
        shapes += [jax.ShapeDtypeStruct(a.shape, a.dtype) for a in self.scatters]
        scratch = []
        for n in (len(self.gathers), len(self.scatters)):
            if n:
                scratch += [pltpu.SemaphoreType.DMA((7 * n,)), pltpu.SemaphoreType.DMA((7 * n,)),
                            pltpu.SemaphoreType.DMA((n,))]
        return ins, shapes, scratch

    def _split(self, in_refs, out_refs, sems):
        ng = len(self.gathers)
        g_sems = sems[:3] if ng else None
        s_sems = sems[3:] if ng else sems
        return in_refs[:ng], in_refs[ng:], out_refs[:ng], out_refs[ng:], g_sems, s_sems

    def _gather_copies(self, i, src_ref, out_ref, sems):
        send_sems, recv_sems, local_sems = sems
        x, y, c = lax.axis_index("x"), lax.axis_index("y"), lax.axis_index("c")
        me, sibling = (x, y, c), (x, y, 1 - c)
        chips = [(1 - x, y), (x, 1 - y), (1 - x, 1 - y)]
        idx = self.gathers[i][1]
        src = src_ref if idx is None else src_ref.at[idx]

        def slot(px, py, pc):
            return out_ref.at[4 * px + 2 * py + pc]

        def copy(k, blk, to, from_src=False):
            return pltpu.make_async_remote_copy(
                src_ref=src if from_src else slot(*blk), dst_ref=slot(*blk), send_sem=send_sems.at[7 * i + k],
                recv_sem=recv_sems.at[7 * i + k], device_id=to, device_id_type=pl.DeviceIdType.MESH)

        mine = pltpu.make_async_copy(src, slot(*me), local_sems.at[i])
        first = [copy(0, me, sibling, True)] + [copy(1 + j, me, (*chip, c), True) for j, chip in enumerate(chips)]
        passed = [copy(4 + j, (*chip, c), sibling) for j, chip in enumerate(chips)]
        arrivals = [copy(1 + j, (*chip, c), me) for j, chip in enumerate(chips)]
        from_sibling = [copy(0, sibling, me)] + [copy(4 + j, (*chip, 1 - c), me) for j, chip in enumerate(chips)]
        return mine, first, passed, arrivals, from_sibling

    def _scatter_copies(self, i, src_ref, out_ref, sems):
        send_sems, recv_sems, local_sems = sems
        x, y, c = lax.axis_index("x"), lax.axis_index("y"), lax.axis_index("c")
        me = 4 * x + 2 * y + c
        mine = pltpu.make_async_copy(src_ref.at[me], out_ref.at[me], local_sems.at[i])
        copies = []
        for k in range(1, N_DEV):
            px = 1 - x if k & 4 else x
            py = 1 - y if k & 2 else y
            pc = 1 - c if k & 1 else c
            copies.append(pltpu.make_async_remote_copy(
                src_ref=src_ref.at[4 * px + 2 * py + pc], dst_ref=out_ref.at[me], send_sem=send_sems.at[7 * i + k - 1],
                recv_sem=recv_sems.at[7 * i + k - 1], device_id=(px, py, pc), device_id_type=pl.DeviceIdType.MESH))
        return mine, copies

    def start(self, in_refs, out_refs, sems):
        g_in, s_in, g_out, s_out, g_sems, s_sems = self._split(in_refs, out_refs, sems)
        for i in range(len(self.gathers)):
            mine, first, _, _, _ = self._gather_copies(i, g_in[i], g_out[i], g_sems)
            mine.start()
            for cp in first:
                cp.start()
        for i in range(len(self.scatters)):
            mine, copies = self._scatter_copies(i, s_in[i], s_out[i], s_sems)
            mine.start()
            for cp in copies:
                cp.start()

    def finish(self, in_refs, out_refs, sems):
        g_in, s_in, g_out, s_out, g_sems, s_sems = self._split(in_refs, out_refs, sems)
        parts = [self._gather_copies(i, g_in[i], g_out[i], g_sems) for i in range(len(self.gathers))]
        for j in range(3):
            for _, _, passed, arrivals, _ in parts:
                arrivals[j].wait_recv()
                passed[j].start()
        for mine, first, passed, _, from_sibling in parts:
            for cp in from_sibling:
                cp.wait_recv()
            for cp in first + passed:
                cp.wait_send()
            mine.wait()
        for i in range(len(self.scatters)):
            mine, copies = self._scatter_copies(i, s_in[i], s_out[i], s_sems)
            for cp in copies:
                cp.wait_recv()
            for cp in copies:
                cp.wait_send()
            mine.wait()


def _params(grid):
    return pltpu.CompilerParams(dimension_semantics=("arbitrary",) * len(grid), vmem_limit_bytes=VMEM_LIMIT)


def _call(name, body, *, grid, ins, in_specs, out_shape, out_specs, scratch=(), comm=None):
    n_in, n_out, n_scr = len(ins), len(out_shape), len(scratch)
    if comm is None:
        outs = pl.pallas_call(body, grid=grid, in_specs=list(in_specs), out_specs=list(out_specs),
                              out_shape=list(out_shape), scratch_shapes=list(scratch), name=name,
                              compiler_params=_params(grid))(*ins)
        return list(outs), []
    c_ins, c_shapes, c_scratch = comm.operands()
    nci, nco = len(c_ins), len(c_shapes)
    anywhere = pl.BlockSpec(memory_space=pl.ANY)

    def carrier(*refs):
        main_in, comm_in = refs[:n_in], refs[n_in:n_in + nci]
        o0 = n_in + nci
        main_out, comm_out = refs[o0:o0 + n_out], refs[o0 + n_out:o0 + n_out + nco]
        s0 = o0 + n_out + nco
        main_scr, comm_scr = refs[s0:s0 + n_scr], refs[s0 + n_scr:]
        first = pl.program_id(0) == 0
        last = pl.program_id(0) == grid[0] - 1
        for ax in range(1, len(grid)):
            first = first & (pl.program_id(ax) == 0)
            last = last & (pl.program_id(ax) == grid[ax] - 1)

        @pl.when(first)
        def _():
            comm.start(comm_in, comm_out, comm_scr)

        body(*main_in, *main_out, *main_scr)

        @pl.when(last)
        def _():
            comm.finish(comm_in, comm_out, comm_scr)

    outs = pl.pallas_call(carrier, grid=grid, in_specs=list(in_specs) + [anywhere] * nci,
                          out_specs=list(out_specs) + [anywhere] * nco, out_shape=list(out_shape) + c_shapes,
                          scratch_shapes=list(scratch) + c_scratch, name=name, compiler_params=_params(grid))(*ins, *c_ins)
    return list(outs[:n_out]), list(outs[n_out:])


def _comm_only(name, comm):
    c_ins, c_shapes, c_scratch = comm.operands()
    nci, nco = len(c_ins), len(c_shapes)
    anywhere = pl.BlockSpec(memory_space=pl.ANY)

    def body(*refs):
        comm.start(refs[:nci], refs[nci:nci + nco], refs[nci + nco:])
        comm.finish(refs[:nci], refs[nci:nci + nco], refs[nci + nco:])

    return list(pl.pallas_call(body, in_specs=[anywhere] * nci, out_specs=[anywhere] * nco, out_shape=c_shapes,
                               scratch_shapes=c_scratch, name=name)(*c_ins))


def _stage_fwd(name, f, grid, ins, in_specs, out_shapes, out_specs):
    n_in = len(ins)

    def body(*refs):
        res = f(*[r[...] for r in refs[:n_in]])
        for o_ref, val in zip(refs[n_in:], res):
            o_ref[...] = val.astype(o_ref.dtype)

    return _call(name, body, grid=grid, ins=ins, in_specs=in_specs, out_shape=out_shapes, out_specs=out_specs)[0]


def _stage_bwd(name, f, grid, ins, in_specs, cts, ct_specs, grads):
    n_in = len(ins)
    flat_cts = [c for group in cts for c in group]
    flat_ct_specs = [s for group, spec in zip(cts, ct_specs) for s in (spec,) * len(group)]
    n_ct = len(flat_cts)
    diff = [g[0] for g in grads]

    def body(*refs):
        vals = [r[...] for r in refs[:n_in]]
        ct_refs = refs[n_in:n_in + n_ct]
        g_refs = refs[n_in + n_ct:]
        ct_vals, pos = [], 0
        for group in cts:
            acc = ct_refs[pos][...].astype(F32)
            for j in range(1, len(group)):
                acc = acc + ct_refs[pos + j][...].astype(F32)
            ct_vals.append(acc)
            pos += len(group)

        def g_fn(*dvals):
            full = list(vals)
            for i, dv in zip(diff, dvals):
                full[i] = dv
            return f(*full)

        _, vjp = jax.vjp(g_fn, *[vals[i] for i in diff])
        gvals = vjp(tuple(ct_vals))
        for gspec, g_ref, gval in zip(grads, g_refs, gvals):
            acc_axes = gspec[1]
            if not acc_axes:
                g_ref[...] = gval.astype(g_ref.dtype)
            else:
                first = pl.program_id(acc_axes[0]) == 0
                for ax in acc_axes[1:]:
                    first = first & (pl.program_id(ax) == 0)

                @pl.when(first)
                def _():
                    g_ref[...] = jnp.zeros_like(g_ref)

                g_ref[...] += gval.astype(g_ref.dtype)

    out_shapes, out_specs = [], []
    for gspec in grads:
        shape, spec = gspec[3] if len(gspec) > 3 else (ins[gspec[0]].shape, in_specs[gspec[0]])
        out_shapes.append(jax.ShapeDtypeStruct(shape, gspec[2]))
        out_specs.append(spec)
    return _call(name, body, grid=grid, ins=list(ins) + flat_cts, in_specs=list(in_specs) + flat_ct_specs,
                 out_shape=out_shapes, out_specs=out_specs)[0]


def _pick_tile(n, candidates):
    for c in candidates:
        if n % c == 0:
            return c
    return n


def _matmul(name, a, b, *, ta=False, tb=False, add=None, out_dtype=F32, comm=None):
    if ta:
        k_dim, m = a.shape
    else:
        m, k_dim = a.shape
    n = b.shape[0] if tb else b.shape[1]
    assert (b.shape[1] if tb else b.shape[0]) == k_dim and not (ta and tb)
    tm = _pick_tile(m, (1024, 512, 256, 128))
    tn = _pick_tile(n, (1024, 1408, 2432, 512, 256, 128))
    if ta:
        tk = _pick_tile(k_dim, (1024, 512, 256, 128))
    elif k_dim <= 2816:
        tk = k_dim
    else:
        tk = _pick_tile(k_dim, (1408, 512, 256, 128))
    nk = k_dim // tk
    grid = (m // tm, n // tn, nk)
    a_spec = pl.BlockSpec((tk, tm), lambda i, j, k: (k, i)) if ta else pl.BlockSpec((tm, tk), lambda i, j, k: (i, k))
    b_spec = pl.BlockSpec((tn, tk), lambda i, j, k: (j, k)) if tb else pl.BlockSpec((tk, tn), lambda i, j, k: (k, j))
    o_spec = pl.BlockSpec((tm, tn), lambda i, j, k: (i, j))
    dims = (((0 if ta else 1,), (1 if tb else 0,)), ((), ()))
    has_add = add is not None

    def body(*refs):
        a_ref, b_ref = refs[0], refs[1]
        add_ref = refs[2] if has_add else None
        o_ref, acc_ref = refs[-2], refs[-1]
        k = pl.program_id(2)
        part = lax.dot_general(a_ref[...].astype(BF16), b_ref[...].astype(BF16), dims, preferred_element_type=F32)

        def finish(res):
            if has_add:
                res = res + add_ref[...].astype(F32)
            o_ref[...] = res.astype(o_ref.dtype)

        if nk == 1:
            finish(part)
        else:
            @pl.when(k == 0)
            def _():
                acc_ref[...] = part

            @pl.when((k > 0) & (k < nk - 1))
            def _():
                acc_ref[...] += part

            @pl.when(k == nk - 1)
            def _():
                finish(acc_ref[...] + part)

    ins = [a, b] + ([add] if has_add else [])
    in_specs = [a_spec, b_spec] + ([o_spec] if has_add else [])
    acc_shape = (tm, tn) if nk > 1 else (8, 128)
    outs, comm_outs = _call(name, body, grid=grid, ins=ins, in_specs=in_specs,
                            out_shape=[jax.ShapeDtypeStruct((m, n), out_dtype)], out_specs=[o_spec],
                            scratch=[pltpu.VMEM(acc_shape, F32)], comm=comm)
    return outs[0], comm_outs


SSD_STATE = (SSM_GROUPS * SSM_STATE, SSM_RPG * SSM_HEADDIM)


def _ssd_fwd(xc, proj, dt_bias, a_log, d_skip, norm_g, nb, nc, comm=None):
    t = xc.shape[0]
    row = lambda b, c: (b * nc + c, 0)
    par = lambda shape: pl.BlockSpec(shape, lambda b, c: (0, 0))

    def body(xc_ref, dt_ref, z_ref, dtb_ref, al_ref, ds_ref, ng_ref, e_ref, y_ref, ypre_ref, prev_ref, st_ref):
        @pl.when(pl.program_id(1) == 0)
        def _():
            st_ref[...] = jnp.zeros_like(st_ref)

        prev = st_ref[...]
        prev_ref[0, 0] = prev
        y, y_pre, new_state = _ssd_forward(xc_ref[...], dt_ref[...], z_ref[...], prev, dtb_ref[...], al_ref[...],
                                           ds_ref[...], ng_ref[...], e_ref[...])
        y_ref[...] = y.astype(y_ref.dtype)
        ypre_ref[...] = y_pre
        st_ref[...] = new_state

    return _call(
        "ssd_fwd", body, grid=(nb, nc), ins=[xc, proj, proj, dt_bias, a_log, d_skip, norm_g, _head_expander()],
        in_specs=[pl.BlockSpec((CHUNK, SSM_CONV_DIM), row),
                  pl.BlockSpec((CHUNK, 128), lambda b, c: (b * nc + c, DT_COL // 128)),
                  pl.BlockSpec((CHUNK, SSM_INNER), lambda b, c: (b * nc + c, 1)),
                  par((1, 128)), par((1, 128)), par((1, SSM_INNER)), par((1, SSM_INNER)), par((128, SSM_INNER))],
        out_specs=[pl.BlockSpec((CHUNK, SSM_INNER), row), pl.BlockSpec((CHUNK, SSM_INNER), row),
                   pl.BlockSpec((1, 1) + SSD_STATE, lambda b, c: (b, c, 0, 0))],
        out_shape=[jax.ShapeDtypeStruct((t, SSM_INNER), BF16), jax.ShapeDtypeStruct((t, SSM_INNER), F32),
                   jax.ShapeDtypeStruct((nb, nc) + SSD_STATE, F32)],
        scratch=[pltpu.VMEM(SSD_STATE, F32)], comm=comm)


def _ssd_bwd(xc, proj, prevs, y_pre, dt_bias, a_log, d_skip, norm_g, dy, nb, nc, comm=None):
    t = xc.shape[0]
    row = lambda b, c: (b * nc + (nc - 1 - c), 0)
    par = lambda shape: pl.BlockSpec(shape, lambda b, c: (0, 0))

    def body(xc_ref, dt_ref, z_ref, prev_ref, ypre_ref, dtb_ref, al_ref, ds_ref, ng_ref, e_ref, dy_ref,
             dxc_ref, ddt_ref, dz_ref, ddtb_ref, dal_ref, dds_ref, dng_ref, dst_ref):
        @pl.when(pl.program_id(1) == 0)
        def _():
            dst_ref[...] = jnp.zeros_like(dst_ref)

        @pl.when((pl.program_id(0) == 0) & (pl.program_id(1) == 0))
        def _():
            ddtb_ref[...] = jnp.zeros_like(ddtb_ref)
            dal_ref[...] = jnp.zeros_like(dal_ref)
            dds_ref[...] = jnp.zeros_like(dds_ref)
            dng_ref[...] = jnp.zeros_like(dng_ref)

        dxc, ddt, dz, dprev, ddtb, dal, dds, dng = _ssd_backward(
            xc_ref[...], dt_ref[...], z_ref[...], prev_ref[0, 0], ypre_ref[...], dtb_ref[...], al_ref[...], ds_ref[...],
            ng_ref[...], e_ref[...], dy_ref[...].astype(F32), dst_ref[...])
        dxc_ref[...] = dxc
        ddt_ref[:, :128] = ddt.astype(ddt_ref.dtype)
        ddt_ref[:, 128:] = jnp.zeros((CHUNK, DT_PAD - 128), ddt_ref.dtype)
        dz_ref[...] = dz.astype(dz_ref.dtype)
        dst_ref[...] = dprev
        ddtb_ref[...] += ddtb
        dal_ref[...] += dal
        dds_ref[...] += dds
        dng_ref[...] += dng

    return _call(
        "ssd_bwd", body, grid=(nb, nc),
        ins=[xc, proj, proj, prevs, y_pre, dt_bias, a_log, d_skip, norm_g, _head_expander(), dy],
        in_specs=[pl.BlockSpec((CHUNK, SSM_CONV_DIM), row),
                  pl.BlockSpec((CHUNK, 128), lambda b, c: (b * nc + (nc - 1 - c), DT_COL // 128)),
                  pl.BlockSpec((CHUNK, SSM_INNER), lambda b, c: (b * nc + (nc - 1 - c), 1)),
                  pl.BlockSpec((1, 1) + SSD_STATE, lambda b, c: (b, nc - 1 - c, 0, 0)),
                  pl.BlockSpec((CHUNK, SSM_INNER), row),
                  par((1, 128)), par((1, 128)), par((1, SSM_INNER)), par((1, SSM_INNER)), par((128, SSM_INNER)),
                  pl.BlockSpec((CHUNK, SSM_INNER), row)],
        out_specs=[pl.BlockSpec((CHUNK, SSM_CONV_DIM), row), pl.BlockSpec((CHUNK, DT_PAD), row),
                   pl.BlockSpec((CHUNK, SSM_INNER), row), par((1, 128)), par((1, 128)), par((1, 128)),
                   par((1, SSM_INNER))],
        out_shape=[jax.ShapeDtypeStruct((t, SSM_CONV_DIM), F32), jax.ShapeDtypeStruct((t, DT_PAD), BF16),
                   jax.ShapeDtypeStruct((t, SSM_INNER), BF16), jax.ShapeDtypeStruct((1, 128), F32),
                   jax.ShapeDtypeStruct((1, 128), F32), jax.ShapeDtypeStruct((1, 128), F32),
                   jax.ShapeDtypeStruct((1, SSM_INNER), F32)],
        scratch=[pltpu.VMEM(SSD_STATE, F32)], comm=comm)


def _loss_head(y, target):
    t, d = y.shape
    tm = _pick_tile(t, (256,))
    blk = pl.BlockSpec((tm, d), lambda i: (i, 0))

    def body(y_ref, t_ref, loss_ref, dy_ref):
        err = y_ref[...] - t_ref[...]
        dy_ref[...] = err * (1.0 / d)

        @pl.when(pl.program_id(0) == 0)
        def _():
            loss_ref[...] = jnp.zeros_like(loss_ref)

        loss_ref[...] += 0.5 * jnp.sum(jnp.mean(err * err, axis=-1, keepdims=True), axis=0, keepdims=True)

    return _call("loss_head", body, grid=(t // tm,), ins=[y, target], in_specs=[blk, blk],
                 out_specs=[pl.BlockSpec((1, 1), lambda i: (0, 0)), blk],
                 out_shape=[jax.ShapeDtypeStruct((1, 1), F32), jax.ShapeDtypeStruct((t, d), F32)])[0]


def _adamw_math(g, w, m, v):
    m_new = ADAM_B1 * m + (1.0 - ADAM_B1) * g
    v_new = ADAM_B2 * v + (1.0 - ADAM_B2) * jnp.square(g)
    m_hat = m_new / (1.0 - ADAM_B1 ** ADAM_STEP)
    v_hat = v_new / (1.0 - ADAM_B2 ** ADAM_STEP)
    delta = -ADAM_LR * (m_hat / (jnp.sqrt(v_hat) + ADAM_EPS) + ADAM_WD * w)
    return delta, m_new, v_new


def _adamw_sharded(name, parts, w, m, v):
    _, a, b = w.shape
    tr = _pick_tile(a, (128,))
    nt = a // tr
    part_specs = [pl.BlockSpec((N_DEV, tr, b),
                               (lambda l, i, _k=k: (0, jnp.where(l == _k, i, jnp.where(l > _k, nt - 1, 0)), 0)))
                  for k in range(DEPTH)]
    blk = pl.BlockSpec((1, tr, b), lambda l, i: (l, i, 0))

    def body(*refs):
        p_refs = refs[:DEPTH]
        w_ref, m_ref, v_ref, g_out, d_out, m_out, v_out = refs[DEPTH:]
        for k in range(DEPTH):
            @pl.when(pl.program_id(0) == k)
            def _(p_ref=p_refs[k]):
                g = p_ref[0].astype(F32)
                for p in range(1, N_DEV):
                    g = g + p_ref[p].astype(F32)
                delta, m_new, v_new = _adamw_math(g, w_ref[0], m_ref[0], v_ref[0])
                g_out[0] = g
                d_out[0] = delta
                m_out[0] = m_new
                v_out[0] = v_new

    return _call(name, body, grid=(DEPTH, nt), ins=list(parts) + [w, m, v], in_specs=part_specs + [blk, blk, blk],
                 out_specs=[blk] * 4, out_shape=[jax.ShapeDtypeStruct(w.shape, F32)] * 4)[0]


def _adamw_small(name, g, w, m, v):
    full = pl.BlockSpec(w.shape, lambda i: (0, 0))

    def body(g_ref, w_ref, m_ref, v_ref, d_out, m_out, v_out):
        delta, m_new, v_new = _adamw_math(g_ref[...], w_ref[...], m_ref[...], v_ref[...])
        d_out[...] = delta
        m_out[...] = m_new
        v_out[...] = v_new

    return _call(name, body, grid=(1,), ins=[g, w, m, v], in_specs=[full] * 4, out_specs=[full] * 3,
                 out_shape=[jax.ShapeDtypeStruct(w.shape, F32)] * 3)[0]


def _sum_parts(name, parts):
    n_parts, rows, cols = parts.shape
    tr = _pick_tile(rows, (512, 256, 128, 64, 32, 16, 8))

    def body(p_ref, o_ref):
        acc = p_ref[0]
        for p in range(1, n_parts):
            acc = acc + p_ref[p]
        o_ref[...] = acc

    return _call(name, body, grid=(rows // tr,), ins=[parts],
                 in_specs=[pl.BlockSpec((n_parts, tr, cols), lambda i: (0, i, 0))],
                 out_specs=[pl.BlockSpec((tr, cols), lambda i: (i, 0))],
                 out_shape=[jax.ShapeDtypeStruct((rows, cols), parts.dtype)])[0][0]


def _pack_w_in(w):
    pad = jnp.zeros((w.shape[0], DT_PAD - SSM_HEADS), w.dtype)
    return jnp.concatenate([w[:, :DT_COL], w[:, DT_COL:GA_COL], pad, w[:, GA_COL:]], axis=1)


def _unpack_w_in(w):
    return jnp.concatenate([w[:, :DT_COL + SSM_HEADS], w[:, P_GA:]], axis=1)


def _pad_heads(v):
    return jnp.pad(v, (0, 128 - SSM_HEADS)).reshape(1, 128)


def _run_step(x, mem, target, small, ex):
    nb, s, d = x.shape
    t = nb * s
    nc = s // CHUNK
    rows = _pick_tile(t, (256,))
    rows_wide = _pick_tile(t, (512, 256))
    tq = _pick_tile(s, (512, 256))
    vec = lambda a: a.reshape(1, -1)
    full1 = lambda shape: pl.BlockSpec(shape, lambda i: (0,) * len(shape))
    row1 = lambda tm, w: pl.BlockSpec((tm, w), lambda i: (i, 0))
    sds = jax.ShapeDtypeStruct

    def mm(call, l, a, b, **kw):
        comm = ex.before(call, l)
        out, comm_outs = _matmul(call, a, b, comm=comm, **kw)
        if comm is not None:
            ex.after(call, l, comm_outs)
        return out

    mem_specs = [row1(256, d), full1((1, d)), full1((1, d))]
    mem_ins = [mem.reshape(nb * MEM_LEN, d), vec(small["mem_ln_g"]), vec(small["mem_ln_b"])]
    (mem_n,) = _stage_fwd("memln_fwd", _memln_block, (nb * MEM_LEN // 256,), mem_ins, mem_specs,
                          [sds((nb * MEM_LEN, d), BF16)], [row1(256, d)])

    h = x.reshape(t, d)
    h_bf = h.astype(BF16)
    ln_specs = [row1(rows, d), row1(rows, d), full1((1, d)), full1((1, d))]
    ln_outs = [sds((t, d), F32), sds((t, d), BF16)]
    ln_out_specs = [row1(rows, d), row1(rows, d)]
    saved = []
    for l in range(DEPTH):
        sv = {"h_bf": h_bf}
        proj = mm("mm_in", l, h_bf, ex.weight("w_in", l))
        sv["proj"] = proj
        sgu_ins = [proj, proj, vec(small["sg_ln_g"][l]), vec(small["sg_ln_b"][l]), small["sg_w"][l], small["sg_b"][l].T]
        sgu_specs = [pl.BlockSpec((CHUNK, d), lambda i: (i, 0)), pl.BlockSpec((CHUNK, d), lambda i: (i, 1)),
                     full1((1, d)), full1((1, d)), full1((SG_GROUPS, CHUNK, CHUNK)), full1((CHUNK, SG_GROUPS))]
        (a_out,) = _stage_fwd("sgu_fwd", _sgu_block, (t // CHUNK,), sgu_ins, sgu_specs, [sds((t, d), BF16)],
                              [row1(CHUNK, d)])
        sv["sgu"] = (sgu_ins, sgu_specs)
        sv["a_out"] = a_out
        cw = 256
        conv_ins = [proj, small["conv_w"][l], vec(small["conv_b"][l])]
        conv_specs = [pl.BlockSpec((s, cw), lambda j, b: (b, XBC_COL // cw + j)),
                      pl.BlockSpec((SSM_CONV, cw), lambda j, b: (0, j)), pl.BlockSpec((1, cw), lambda j, b: (0, j))]
        conv_out_spec = pl.BlockSpec((s, cw), lambda j, b: (b, j))
        (xc,) = _stage_fwd("conv_fwd", _conv_block, (SSM_CONV_DIM // cw, nb), conv_ins, conv_specs,
                           [sds((t, SSM_CONV_DIM), F32)], [conv_out_spec])
        sv["conv"] = (conv_ins, conv_specs, conv_out_spec)
        ssd_par = [_pad_heads(small["dt_bias"][l]), _pad_heads(small["a_log"][l]),
                   vec(jnp.repeat(small["d_skip"][l], SSM_HEADDIM)), vec(small["ssm_norm_g"][l])]
        comm = ex.before("ssd_fwd", l)
        (y_ssd, y_pre, prevs), comm_outs = _ssd_fwd(xc, proj, *ssd_par, nb, nc, comm=comm)
        if comm is not None:
            ex.after("ssd_fwd", l, comm_outs)
        sv["ssd"] = (xc, prevs, y_pre, ssd_par)
        sv["y_ssd"] = y_ssd
        br_a = mm("mm_sq", l, a_out, ex.weight("p_a", l))
        br_b = mm("mm_pb", l, y_ssd, ex.weight("p_b", l))
        mw = 512
        merge_ins = [proj, proj, br_a, br_b]
        merge_out_spec = pl.BlockSpec((rows_wide, mw), lambda i, j: (i, j))
        merge_specs = [pl.BlockSpec((rows_wide, mw), lambda i, j: (i, P_GA // mw + j)),
                       pl.BlockSpec((rows_wide, mw), lambda i, j: (i, P_GB // mw + j)), merge_out_spec, merge_out_spec]
        (merged,) = _stage_fwd("merge_fwd", _merge_block, (t // rows_wide, d // mw), merge_ins, merge_specs,
                               [sds((t, d), BF16)], [merge_out_spec])
        sv["merge"] = (merge_ins, merge_specs, merge_out_spec)
        sv["merged"] = merged
        y1 = mm("mm_sq", l, merged, ex.weight("w_mix_o", l))
        ln1_ins = [h, y1, vec(small["ln_g"][l, 0]), vec(small["ln_b"][l, 0])]
        h1, h1_bf = _stage_fwd("lnres_fwd", _lnres_block_twice, (t // rows,), ln1_ins, ln_specs, ln_outs, ln_out_specs)
        sv["ln1"] = ln1_ins
        q = mm("mm_sq", l, h1_bf, ex.weight("w_xq", l))
        kv = mm("mm_kv", l, mem_n, ex.weight("w_xkv", l))
        attn_ins = [q, kv]
        attn_out_spec = pl.BlockSpec((tq, d), lambda b, i: (b * (s // tq) + i, 0))
        attn_specs = [attn_out_spec, pl.BlockSpec((MEM_LEN, 2 * d), lambda b, i: (b, 0))]
        (o,) = _stage_fwd("attn_fwd", _attn_block, (nb, s // tq), attn_ins, attn_specs, [sds((t, d), BF16)],
                          [attn_out_spec])
        sv["attn"] = (attn_ins, attn_specs, attn_out_spec)
        sv["o"] = o
        sv["h1_bf"] = h1_bf
        y2 = mm("mm_sq", l, o, ex.weight("w_xo", l))
        ln2_ins = [h1, y2, vec(small["ln_g"][l, 1]), vec(small["ln_b"][l, 1])]
        h2, h2_bf = _stage_fwd("lnres_fwd", _lnres_block_twice, (t // rows,), ln2_ins, ln_specs, ln_outs, ln_out_specs)
        sv["ln2"] = ln2_ins
        sv["h2_bf"] = h2_bf
        gu = mm("mm_ffn_in", l, h2_bf, ex.weight("w_ffn_in", l))
        (act,) = _stage_fwd("swiglu_fwd", _swiglu_block, (t // 128,), [gu], [row1(128, 2 * FFN_HIDDEN)],
                            [sds((t, FFN_HIDDEN), BF16)], [row1(128, FFN_HIDDEN)])
        sv["gu"] = gu
        sv["act"] = act
        y3 = mm("mm_ffn_out", l, act, ex.weight("w_ffn_out", l))
        ln3_ins = [h2, y3, vec(small["ln_g"][l, 2]), vec(small["ln_b"][l, 2])]
        h, h_bf = _stage_fwd("lnres_fwd", _lnres_block_twice, (t // rows,), ln3_ins, ln_specs, ln_outs, ln_out_specs)
        sv["ln3"] = ln3_ins
        saved.append(sv)

    loss, dh = _loss_head(h, target.reshape(t, d))

    g_small = {n: [None] * DEPTH for n in SMALL_REP + SMALL_SH if n not in ("mem_ln_g", "mem_ln_b")}
    dmem_n = []
    ln_grads = [(0, (), F32), (1, (), BF16), (2, (0,), F32), (3, (0,), F32)]
    for l in reversed(range(DEPTH)):
        sv = saved[l]
        dln_g, dln_b = [None] * 3, [None] * 3
        dres, dy3, dln_g[2], dln_b[2] = _stage_bwd("lnres_bwd", _lnres_block, (t // rows,), sv["ln3"], ln_specs,
                                                   [(dh,)], [row1(rows, d)], ln_grads)
        ex.grad("w_ffn_out", l, mm("mm_ffn_out_dw", l, sv["act"], dy3, ta=True, out_dtype=BF16))
        dact = mm("mm_ffn_out_dx", l, dy3, ex.weight("w_ffn_out", l), tb=True)
        (dgu,) = _stage_bwd("swiglu_bwd", _swiglu_block, (t // 128,), [sv["gu"]], [row1(128, 2 * FFN_HIDDEN)],
                            [(dact,)], [row1(128, FFN_HIDDEN)], [(0, (), BF16)])
        ex.grad("w_ffn_in", l, mm("mm_ffn_in_dw", l, sv["h2_bf"], dgu, ta=True, out_dtype=BF16))
        dh2 = mm("mm_ffn_in_dx", l, dgu, ex.weight("w_ffn_in", l), tb=True, add=dres)
        dres, dy2, dln_g[1], dln_b[1] = _stage_bwd("lnres_bwd", _lnres_block, (t // rows,), sv["ln2"], ln_specs,
                                                   [(dh2,)], [row1(rows, d)], ln_grads)
        ex.grad("w_xo", l, mm("mm_sq_dw", l, sv["o"], dy2, ta=True, out_dtype=BF16))
        do = mm("mm_sq_dx", l, dy2, ex.weight("w_xo", l), tb=True)
        attn_ins, attn_specs, attn_out_spec = sv["attn"]
        dq, dkv = _stage_bwd("attn_bwd", _attn_block, (nb, s // tq), attn_ins, attn_specs, [(do,)], [attn_out_spec],
                             [(0, (), BF16), (1, (1,), F32)])
        ex.grad("w_xq", l, mm("mm_sq_dw", l, sv["h1_bf"], dq, ta=True, out_dtype=BF16))
        dh1 = mm("mm_sq_dx", l, dq, ex.weight("w_xq", l), tb=True, add=dres)
        ex.grad("w_xkv", l, mm("mm_kv_dw", l, mem_n, dkv, ta=True, out_dtype=BF16))
        dmem_n.append(mm("mm_kv_dx", l, dkv, ex.weight("w_xkv", l), tb=True))
        dres, dy1, dln_g[0], dln_b[0] = _stage_bwd("lnres_bwd", _lnres_block, (t // rows,), sv["ln1"], ln_specs,
                                                   [(dh1,)], [row1(rows, d)], ln_grads)
        g_small["ln_g"][l] = jnp.concatenate(dln_g, axis=0)
        g_small["ln_b"][l] = jnp.concatenate(dln_b, axis=0)
        ex.grad("w_mix_o", l, mm("mm_sq_dw", l, sv["merged"], dy1, ta=True, out_dtype=BF16))
        dmerged = mm("mm_sq_dx", l, dy1, ex.weight("w_mix_o", l), tb=True)
        merge_ins, merge_specs, merge_out_spec = sv["merge"]
        dga, dgb, dbr_a, dbr_b = _stage_bwd("merge_bwd", _merge_block, (t // rows_wide, d // 512), merge_ins, merge_specs,
                                            [(dmerged,)], [merge_out_spec],
                                            [(i, (), BF16, ((t, d), merge_out_spec)) for i in range(4)])
        ex.grad("p_a", l, mm("mm_sq_dw", l, sv["a_out"], dbr_a, ta=True, out_dtype=BF16))
        da_out = mm("mm_sq_dx", l, dbr_a, ex.weight("p_a", l), tb=True)
        ex.grad("p_b", l, mm("mm_pb_dw", l, sv["y_ssd"], dbr_b, ta=True, out_dtype=BF16))
        dy_ssd = mm("mm_pb_dx", l, dbr_b, ex.weight("p_b", l), tb=True)
        sgu_ins, sgu_specs = sv["sgu"]
        du, dv, dsg_ln_g, dsg_ln_b, dsg_w, dsg_b = _stage_bwd(
            "sgu_bwd", _sgu_block, (t // CHUNK,), sgu_ins, sgu_specs, [(da_out,)], [row1(CHUNK, d)],
            [(0, (), BF16, ((t, d), row1(CHUNK, d))), (1, (), BF16, ((t, d), row1(CHUNK, d))), (2, (0,), F32),
             (3, (0,), F32), (4, (0,), F32), (5, (0,), F32)])
        g_small["sg_ln_g"][l], g_small["sg_ln_b"][l], g_small["sg_w"][l], g_small["sg_b"][l] = (
            dsg_ln_g[0], dsg_ln_b[0], dsg_w, dsg_b.T)
        xc, prevs, y_pre, ssd_par = sv["ssd"]
        comm = ex.before("ssd_bwd", l)
        (dxc, ddt, dz, ddtb, dal, dds, dng), comm_outs = _ssd_bwd(xc, sv["proj"], prevs, y_pre, *ssd_par, dy_ssd, nb, nc,
                                                                  comm=comm)
        if comm is not None:
            ex.after("ssd_bwd", l, comm_outs)
        g_small["dt_bias"][l], g_small["a_log"][l], g_small["d_skip"][l] = (
            ddtb[0, :SSM_HEADS], dal[0, :SSM_HEADS], dds[0, :SSM_HEADS])
        g_small["ssm_norm_g"][l] = dng[0]
        conv_ins, conv_specs, conv_out_spec = sv["conv"]
        dxbc, dconv_w, dconv_b = _stage_bwd("conv_bwd", _conv_block, (SSM_CONV_DIM // 256, nb), conv_ins, conv_specs,
                                            [(dxc,)], [conv_out_spec],
                                            [(0, (), BF16, ((t, SSM_CONV_DIM), conv_out_spec)), (1, (1,), F32),
                                             (2, (1,), F32)])
        g_small["conv_w"][l], g_small["conv_b"][l] = dconv_w, dconv_b[0]
        dproj = jnp.concatenate([du, dv, dz, dxbc, ddt, dga, dgb], axis=1)
        ex.grad("w_in", l, mm("mm_in_dw", l, sv["h_bf"], dproj, ta=True, out_dtype=BF16))
        if l == 0:
            dmg, dmb = _stage_bwd("memln_bwd", _memln_block, (nb * MEM_LEN // 256,), mem_ins, mem_specs,
                                  [tuple(dmem_n)], [row1(256, d)], [(1, (0,), F32), (2, (0,), F32)])
            done = {n: jnp.stack(g, axis=0) for n, g in g_small.items()}
            done["mem_ln_g"], done["mem_ln_b"] = dmg[0], dmb[0]
            ex.small_grads(done)
        dh = mm("mm_in_dx", l, dproj, ex.weight("w_in", l), tb=True, add=dres)

    return loss, dh.reshape(nb, s, d)


def _pack_flat(arrays, rows):
    flat = jnp.concatenate([a.reshape(-1) for a in arrays])
    return jnp.pad(flat, (0, rows * 128 - flat.shape[0])).reshape(rows, 128)


def _unpack_flat(packed, shapes):
    lead = packed.shape[:-2]
    flat = packed.reshape(lead + (-1,))
    out, pos = [], 0
    for shape in shapes:
        n = math.prod(shape)
        out.append(flat[..., pos:pos + n].reshape(lead + tuple(shape)))
        pos += n
    return out


def _small_rows(n_elems):
    return -(-n_elems // (128 * SMALL_ROW_TILE)) * SMALL_ROW_TILE


def _from_shards(name, gathered):
    _, a, b = gathered.shape
    if name in BIG_COL_SHARDED:
        w = gathered.transpose(1, 0, 2).reshape(a, N_DEV * b)
        return _pack_w_in(w) if name == "w_in" else w
    return gathered.reshape(N_DEV * a, b)


def _to_shards(name, g):
    if name in BIG_COL_SHARDED:
        g = _unpack_w_in(g) if name == "w_in" else g
        a, nb = g.shape
        return g.reshape(a, N_DEV, nb // N_DEV).transpose(1, 0, 2)
    a, b = g.shape
    return g.reshape(N_DEV, a // N_DEV, b)


class _MeshExchange:
    def __init__(self, shards_bf16, first):
        self.shards = shards_bf16
        self.full = dict(first)
        self.grads = {}
        self.received = {}
        self.small = None
        self.small_gathered = None

    def weight(self, name, l):
        return self.full[(name, l)]

    def grad(self, name, l, g):
        self.grads[(name, l)] = g

    def small_grads(self, done):
        self.small = done

    def before(self, call, l):
        comm = _Comm()
        for name, layer in GATHER_PLAN.get((call, l), ()):
            comm.gathers.append((self.shards[name], layer))
        for name, layer in SCATTER_PLAN.get((call, l), ()):
            comm.scatters.append(_to_shards(name, self.grads[(name, layer)]))
        if (call, l) == ("mm_in_dx", 0):
            names = SMALL_REP + SMALL_SH
            rows = _small_rows(sum(math.prod(self.small[n].shape) for n in names))
            comm.gathers.append((_pack_flat([self.small[n] for n in names], rows), None))
        return comm if comm.gathers or comm.scatters else None

    def after(self, call, l, outs):
        gathers = list(GATHER_PLAN.get((call, l), ()))
        for (name, layer), out in zip(gathers, outs):
            self.full[(name, layer)] = _from_shards(name, out)
        outs = outs[len(gathers):]
        if (call, l) == ("mm_in_dx", 0):
            self.small_gathered = outs[0]
            outs = outs[1:]
        for (name, layer), out in zip(SCATTER_PLAN.get((call, l), ()), outs):
            self.received[(name, layer)] = out


def kernel(x, mem, mem_ln_g, mem_ln_b, w_in, sg_ln_g, sg_ln_b, sg_w, sg_b, conv_w, conv_b, dt_bias, a_log, d_skip, ssm_norm_g, p_a, p_b, w_mix_o, w_xq, w_xkv, w_xo, w_ffn_in, w_ffn_out, ln_g, ln_b, loss_target, m_mem_ln_g, m_mem_ln_b, m_w_in, m_sg_ln_g, m_sg_ln_b, m_sg_w, m_sg_b, m_conv_w, m_conv_b, m_dt_bias, m_a_log, m_d_skip, m_ssm_norm_g, m_p_a, m_p_b, m_w_mix_o, m_w_xq, m_w_xkv, m_w_xo, m_w_ffn_in, m_w_ffn_out, m_ln_g, m_ln_b, v_mem_ln_g, v_mem_ln_b, v_w_in, v_sg_ln_g, v_sg_ln_b, v_sg_w, v_sg_b, v_conv_w, v_conv_b, v_dt_bias, v_a_log, v_d_skip, v_ssm_norm_g, v_p_a, v_p_b, v_w_mix_o, v_w_xq, v_w_xkv, v_w_xo, v_w_ffn_in, v_w_ffn_out, v_ln_g, v_ln_b):
    args = dict(locals())
    w = {n: args[n] for n in WEIGHTS}
    m = {n: args["m_" + n] for n in WEIGHTS}
    v = {n: args["v_" + n] for n in WEIGHTS}
    me = 4 * lax.axis_index("x") + 2 * lax.axis_index("y") + lax.axis_index("c")

    shards = {n: w[n].astype(BF16) for n in BIG}
    sh_shapes = [w[n].shape for n in SMALL_SH]
    first = _Comm()
    first.gathers.append((shards["w_in"], 0))
    first.gathers.append((_pack_flat([w[n] for n in SMALL_SH], _small_rows(sum(math.prod(s) for s in sh_shapes))), None))
    w_in0, small_sh = _comm_only("gather_first", first)
    small = {n: w[n] for n in SMALL_REP}
    for n, sh in zip(SMALL_SH, _unpack_flat(small_sh, sh_shapes)):
        small[n] = sh.transpose(1, 2, 0, 3).reshape(sh.shape[1], sh.shape[2], N_DEV * sh.shape[3])

    ex = _MeshExchange(shards, {("w_in", 0): _from_shards("w_in", w_in0)})
    loss, grad_x = _run_step(x, mem, loss_target, small, ex)
    loss = lax.psum(loss[0, 0], ("x", "y", "c"))

    out = {}
    for n in BIG:
        out[n] = _adamw_sharded("adamw_" + n, [ex.received[(n, l)] for l in range(DEPTH)], w[n], m[n], v[n])
    names = SMALL_REP + SMALL_SH
    g_small = dict(zip(names, _unpack_flat(_sum_parts("sum_small_grads", ex.small_gathered),
                                           [ex.small[n].shape for n in names])))
    for n in names:
        g = g_small[n]
        if n in SMALL_SH:
            width = w[n].shape[-1]
            g = lax.dynamic_slice_in_dim(g, me * width, width, axis=-1)
        two_d = (-1, w[n].shape[-1])
        res = _adamw_small("adamw_" + n, g.reshape(two_d), w[n].reshape(two_d), m[n].reshape(two_d), v[n].reshape(two_d))
        out[n] = [g] + [r.reshape(w[n].shape) for r in res]

    results = []
    for k in range(4):
        results.extend(out[n][k] for n in WEIGHTS)
    return (loss, grad_x, *results)
```

```python
import functools
import math

import jax
import jax.numpy as jnp
from jax import lax
from jax.experimental import pallas as pl
from jax.experimental.pallas import tpu as pltpu

F32 = jnp.float32
BF16 = jnp.bfloat16
HIGHEST = lax.Precision.HIGHEST

N_DEV = 8
D_MODEL = 1024
DEPTH = 2
MEM_LEN = 256
CHUNK = 128
SG_GROUPS = 8
SSM_INNER = 2048
SSM_HEADDIM = 64
SSM_HEADS = 32
SSM_STATE = 128
SSM_GROUPS = 4
SSM_RPG = 8
SSM_CONV = 4
SSM_CONV_DIM = 3072
X_HEADS = 4
X_HEADDIM = 256
FFN_HIDDEN = 2816
ALPHA = float((2 * DEPTH) ** 0.25)
LN_EPS = 1e-5
RMS_EPS = 1e-5
XBC_COL = 4096
DT_COL = 7168
GA_COL = 7200
DT_PAD = 512
P_GA = DT_COL + DT_PAD
P_GB = P_GA + D_MODEL

ADAM_LR = 0.001
ADAM_B1 = 0.9
ADAM_B2 = 0.999
ADAM_EPS = 1e-08
ADAM_WD = 0.01
ADAM_STEP = 10

VMEM_LIMIT = 48 * 1024 * 1024
SMALL_ROW_TILE = 256

BIG = ("w_in", "p_a", "p_b", "w_mix_o", "w_xq", "w_xkv", "w_xo", "w_ffn_in", "w_ffn_out")
BIG_COL_SHARDED = ("w_in", "w_xkv", "w_ffn_in")
SMALL_REP = ("mem_ln_g", "mem_ln_b", "sg_ln_g", "sg_ln_b", "sg_w", "sg_b", "conv_b", "dt_bias", "a_log", "d_skip",
             "ssm_norm_g")
SMALL_SH = ("conv_w", "ln_g", "ln_b")
WEIGHTS = ("mem_ln_g", "mem_ln_b", "w_in", "sg_ln_g", "sg_ln_b", "sg_w", "sg_b", "conv_w", "conv_b", "dt_bias", "a_log",
           "d_skip", "ssm_norm_g", "p_a", "p_b", "w_mix_o", "w_xq", "w_xkv", "w_xo", "w_ffn_in", "w_ffn_out", "ln_g", "ln_b")

REST = ("p_a", "p_b", "w_mix_o", "w_xq", "w_xkv", "w_xo", "w_ffn_in", "w_ffn_out")
GATHER_PLAN = {
    ("mm_in", 0): [(n, 0) for n in REST],
    ("ssd_fwd", 0): [("w_in", 1)],
    ("mm_ffn_in", 0): [(n, 1) for n in ("p_a", "p_b", "w_mix_o", "w_xq", "w_xkv", "w_xo")],
    ("mm_in", 1): [("w_ffn_in", 1), ("w_ffn_out", 1)],
}
SCATTER_PLAN = {}
for _l in range(DEPTH):
    SCATTER_PLAN[("swiglu_bwd", _l)] = [("w_ffn_out", _l)]
    SCATTER_PLAN[("sgu_bwd", _l)] = [("w_mix_o", _l), ("p_a", _l)]
    SCATTER_PLAN[("ssd_bwd", _l)] = [("w_ffn_in", _l), ("w_xkv", _l), ("w_xq", _l), ("w_xo", _l)]
    SCATTER_PLAN[("conv_bwd", _l)] = [("p_b", _l)]
    SCATTER_PLAN[("mm_in_dx", _l)] = [("w_in", _l)]


def _layer_norm(x, g, b):
    mu = jnp.mean(x, axis=-1, keepdims=True)
    xc = x - mu
    var = jnp.mean(xc * xc, axis=-1, keepdims=True)
    return xc * lax.rsqrt(var + LN_EPS) * g + b


def _gelu(x):
    return 0.5 * x * (1.0 + lax.erf(x * (1.0 / math.sqrt(2.0))))


def _silu(x):
    return x * jax.nn.sigmoid(x)


def _softplus(x):
    return jnp.maximum(x, 0.0) + jnp.log1p(jnp.exp(-jnp.abs(x)))


def _causal_mask():
    r = lax.broadcasted_iota(jnp.int32, (CHUNK, CHUNK), 0)
    c = lax.broadcasted_iota(jnp.int32, (CHUNK, CHUNK), 1)
    return r >= c


def _sgu_block(u, v, ln_g, ln_b, w, sb):
    gu = _gelu(u)
    vn = _layer_norm(_gelu(v), ln_g, ln_b)
    causal = _causal_mask()
    width = D_MODEL // SG_GROUPS
    outs = []
    for g in range(SG_GROUPS):
        wg = jnp.where(causal, w[g], 0.0).astype(BF16)
        mixed = jnp.dot(wg, vn[:, g * width:(g + 1) * width].astype(BF16), preferred_element_type=F32)
        outs.append(mixed + sb[:, g:g + 1])
    return (gu * jnp.concatenate(outs, axis=1),)


GROUP_W = SSM_RPG * SSM_HEADDIM
NT_DIMS = (((1,), (1,)), ((), ()))
TN_DIMS = (((0,), (0,)), ((), ()))


def _mxu(a, b, dims=(((1,), (0,)), ((), ()))):
    return lax.dot_general(a.astype(BF16), b.astype(BF16), dims, preferred_element_type=F32)


def _head_expander():
    return (jnp.arange(SSM_INNER)[None, :] // SSM_HEADDIM == jnp.arange(128)[:, None]).astype(BF16)


def _bf16_terms(x, n):
    terms = []
    for _ in range(n):
        t = x.astype(BF16)
        terms.append(t)
        x = x - t.astype(F32)
    return terms


def _expand_heads(q, e):
    return sum(jnp.dot(t, e, preferred_element_type=F32) for t in _bf16_terms(q, 3))


def _reduce_heads(v, e):
    return sum(lax.dot_general(t, e, NT_DIMS, preferred_element_type=F32) for t in _bf16_terms(v, 2))


def _ssd_common(xc, dtraw, dt_bias, a_log, e):
    xs = xc[:, :SSM_INNER]
    pre = dtraw + dt_bias
    dt = _softplus(pre)
    a = -jnp.exp(a_log)
    r_i = lax.broadcasted_iota(jnp.int32, (CHUNK, CHUNK), 0)
    c_i = lax.broadcasted_iota(jnp.int32, (CHUNK, CHUNK), 1)
    tril = jnp.where(r_i >= c_i, 1.0, 0.0).astype(F32)
    cs = jnp.dot(tril, dt * a, precision=HIGHEST, preferred_element_type=F32)
    cs_last = cs[CHUNK - 1:CHUNK, :]
    decay_in = jnp.exp(cs)
    decay_st = jnp.exp(cs_last - cs)
    dt_x = _expand_heads(dt, e)
    w_st_x = _expand_heads(dt * decay_st, e)
    decay_in_x = _expand_heads(decay_in, e)
    return dict(xs=xs, pre=pre, dt=dt, a=a, lower=r_i >= c_i, upper=c_i >= r_i, cs=cs, cs_t=cs.T, decay_in=decay_in,
                decay_st=decay_st, chunk_decay=jnp.exp(cs_last), dt_x=dt_x, w_st_x=w_st_x, decay_in_x=decay_in_x,
                chunk_decay_x=decay_in_x[CHUNK - 1:CHUNK, :], xdt=xs * dt_x, x_st=(xs * w_st_x).astype(BF16),
                low=lax.broadcasted_iota(jnp.int32, (CHUNK, 128), 1) < SSM_HEADDIM)


def _pair_decay(c, h):
    return jnp.exp(jnp.where(c["lower"], c["cs"][:, h:h + 1] - c["cs_t"][h:h + 1, :], -1e30))


def _pair_decay_t(c, h):
    return jnp.exp(jnp.where(c["upper"], c["cs_t"][h:h + 1, :] - c["cs"][:, h:h + 1], -1e30))


def _ssd_forward(xc, dtraw, z, prev, dt_bias, a_log, d_skip_x, norm_g, e):
    c = _ssd_common(xc, dtraw, dt_bias, a_log, e)
    y_groups, new_states = [], []
    for g in range(SSM_GROUPS):
        lanes = slice(g * GROUP_W, (g + 1) * GROUP_W)
        bg = xc[:, SSM_INNER + g * SSM_STATE:SSM_INNER + (g + 1) * SSM_STATE]
        cg = xc[:, SSM_INNER + (SSM_GROUPS + g) * SSM_STATE:SSM_INNER + (SSM_GROUPS + g + 1) * SSM_STATE].astype(BF16)
        pg = prev[g * SSM_STATE:(g + 1) * SSM_STATE, :]
        cb = _mxu(cg, bg, NT_DIMS)
        y_in = _mxu(cg, pg) * c["decay_in_x"][:, lanes]
        new_states.append(pg * c["chunk_decay_x"][:, lanes] + _mxu(bg.T, c["x_st"][:, lanes]))
        pairs = []
        for j in range(SSM_RPG // 2):
            h0 = g * SSM_RPG + 2 * j
            xp = c["xdt"][:, 128 * (h0 // 2):128 * (h0 // 2 + 1)]
            pairs.append(_mxu(cb * _pair_decay(c, h0), jnp.where(c["low"], xp, 0.0))
                         + _mxu(cb * _pair_decay(c, h0 + 1), jnp.where(c["low"], 0.0, xp)))
        y_groups.append(jnp.concatenate(pairs, axis=1) + y_in)
    y_pre = jnp.concatenate(y_groups, axis=1) + c["xs"] * d_skip_x
    gated = y_pre * _silu(z)
    normed = [gated[:, g * GROUP_W:(g + 1) * GROUP_W] for g in range(SSM_GROUPS)]
    normed = [yg * lax.rsqrt(jnp.mean(yg * yg, axis=-1, keepdims=True) + RMS_EPS) for yg in normed]
    return jnp.concatenate(normed, axis=1) * norm_g, y_pre, jnp.concatenate(new_states, axis=0)


def _ssd_backward(xc, dtraw, z, prev, y_pre, dt_bias, a_log, d_skip_x, norm_g, e, dout, dnew):
    c = _ssd_common(xc, dtraw, dt_bias, a_log, e)
    xs = c["xs"]
    sig = jax.nn.sigmoid(z)
    silu_z = z * sig
    gated = y_pre * silu_z
    d_gated, normed = [], []
    for g in range(SSM_GROUPS):
        lanes = slice(g * GROUP_W, (g + 1) * GROUP_W)
        yg = gated[:, lanes]
        r = lax.rsqrt(jnp.mean(yg * yg, axis=-1, keepdims=True) + RMS_EPS)
        n = yg * r
        gh = dout[:, lanes] * norm_g[:, lanes]
        d_gated.append(r * (gh - n * jnp.mean(gh * n, axis=-1, keepdims=True)))
        normed.append(n)
    d_gated = jnp.concatenate(d_gated, axis=1)
    dnorm_g = jnp.sum(dout * jnp.concatenate(normed, axis=1), axis=0, keepdims=True)
    dy = d_gated * silu_z
    dz = d_gated * y_pre * (sig * (1.0 + z * (1.0 - sig)))
    dxs = dy * d_skip_x
    dd_skip = jnp.sum(_reduce_heads(dy * xs, e), axis=0, keepdims=True)

    lane = lax.broadcasted_iota(jnp.int32, (CHUNK, 128), 1)
    sub = lax.broadcasted_iota(jnp.int32, (8, 128), 0)
    dcs_neg = jnp.zeros((CHUNK, 128), F32)
    row_slabs = []
    dxdt, dx_st, d_decay_in_x, dprev, d_chunk_decay_x, db_all, dc_all = [], [], [], [], [], [], []
    for g in range(SSM_GROUPS):
        lanes = slice(g * GROUP_W, (g + 1) * GROUP_W)
        bg = xc[:, SSM_INNER + g * SSM_STATE:SSM_INNER + (g + 1) * SSM_STATE].astype(BF16)
        cg_f = xc[:, SSM_INNER + (SSM_GROUPS + g) * SSM_STATE:SSM_INNER + (SSM_GROUPS + g + 1) * SSM_STATE]
        cg = cg_f.astype(BF16)
        pg = prev[g * SSM_STATE:(g + 1) * SSM_STATE, :]
        dng = dnew[g * SSM_STATE:(g + 1) * SSM_STATE, :]
        dy_g = dy[:, lanes]
        cb_t = _mxu(bg, cg, NT_DIMS)
        t1 = (dy_g * c["decay_in_x"][:, lanes]).astype(BF16)
        d_decay_in_x.append(dy_g * _mxu(cg, pg))
        dc = _mxu(t1, pg, NT_DIMS)
        dprev.append(_mxu(cg_f.T, t1) + dng * c["chunk_decay_x"][:, lanes])
        d_chunk_decay_x.append(dng * pg)
        db = _mxu(c["x_st"][:, lanes], dng, NT_DIMS)
        dx_st.append(_mxu(bg, dng))
        dcb_t = jnp.zeros((CHUNK, CHUNK), F32)
        rows = []
        for j in range(SSM_RPG // 2):
            h0 = g * SSM_RPG + 2 * j
            blk = slice(128 * (h0 // 2), 128 * (h0 // 2 + 1))
            xp = c["xdt"][:, blk]
            dyp = dy[:, blk].astype(BF16)
            pair_dx = []
            for k, xk in enumerate((jnp.where(c["low"], xp, 0.0), jnp.where(c["low"], 0.0, xp))):
                dec_t = _pair_decay_t(c, h0 + k)
                pair_dx.append(_mxu(cb_t * dec_t, dyp))
                dml_t = _mxu(xk, dyp, NT_DIMS) * dec_t
                dcb_t = dcb_t + dml_t
                dseg_t = dml_t * cb_t
                dcs_neg = dcs_neg + jnp.where(lane == h0 + k, jnp.sum(dseg_t, axis=-1, keepdims=True), 0.0)
                rows.append(jnp.sum(dseg_t, axis=0, keepdims=True))
            dxdt.append(jnp.where(c["low"], pair_dx[0], pair_dx[1]))
        slab = jnp.zeros((8, 128), F32)
        for r in range(SSM_RPG):
            slab = slab + jnp.where(sub == r, rows[r], 0.0)
        row_slabs.append(slab)
        dc_all.append(dc + _mxu(dcb_t.T, bg))
        db_all.append(db + _mxu(dcb_t, cg))
    dxdt = jnp.concatenate(dxdt, axis=1)
    dx_st = jnp.concatenate(dx_st, axis=1)
    by_head = jnp.concatenate(row_slabs + [jnp.zeros((CHUNK - SSM_HEADS, 128), F32)], axis=0)
    dcs = by_head.T - dcs_neg
    dxs = dxs + dxdt * c["dt_x"] + dx_st * c["w_st_x"]
    ddt = _reduce_heads(dxdt * xs, e)
    dw_st = _reduce_heads(dx_st * xs, e)
    dcs = dcs + _reduce_heads(jnp.concatenate(d_decay_in_x, axis=1), e) * c["decay_in"]
    ddt = ddt + dw_st * c["decay_st"]
    d_log_st = dw_st * c["dt"] * c["decay_st"]
    dcs = dcs - d_log_st
    d_chunk_decay = jnp.sum(_reduce_heads(jnp.concatenate(d_chunk_decay_x, axis=1), e), axis=0, keepdims=True)
    dcs_last = jnp.sum(d_log_st, axis=0, keepdims=True) + d_chunk_decay * c["chunk_decay"]
    row = lax.broadcasted_iota(jnp.int32, (CHUNK, 128), 0)
    dcs = dcs + jnp.where(row == CHUNK - 1, dcs_last, 0.0)
    triu = jnp.where(c["upper"], 1.0, 0.0).astype(F32)
    dda = jnp.dot(triu, dcs, precision=HIGHEST, preferred_element_type=F32)
    ddt = ddt + dda * c["a"]
    da_log = jnp.sum(dda * c["dt"], axis=0, keepdims=True) * c["a"]
    dpre = ddt * jax.nn.sigmoid(c["pre"])
    dxc = jnp.concatenate([dxs] + db_all + dc_all, axis=1)
    return (dxc, dpre, dz, jnp.concatenate(dprev, axis=0), jnp.sum(dpre, axis=0, keepdims=True), da_log, dd_skip,
            dnorm_g)


def _conv_block(x, w, b):
    rows = lax.broadcasted_iota(jnp.int32, x.shape, 0)
    acc = x * w[SSM_CONV - 1:SSM_CONV, :] + b
    for k in range(SSM_CONV - 1):
        shift = SSM_CONV - 1 - k
        acc = acc + _shift_rows(x, rows, shift) * w[k:k + 1, :]
    return (_silu(acc),)


@functools.partial(jax.custom_vjp, nondiff_argnums=(2,))
def _shift_rows(x, rows, shift):
    return jnp.where(rows >= shift, pltpu.roll(x, shift, 0), 0.0)


def _shift_rows_fwd(x, rows, shift):
    return _shift_rows(x, rows, shift), rows


def _shift_rows_bwd(shift, rows, g):
    n = g.shape[0]
    return jnp.where(rows < n - shift, pltpu.roll(g, n - shift, 0), 0.0), None


_shift_rows.defvjp(_shift_rows_fwd, _shift_rows_bwd)


def _merge_block(ga, gb, br_a, br_b):
    return (jax.nn.sigmoid(ga) * br_a + jax.nn.sigmoid(gb) * br_b,)


def _lnres_block(x, y, g, b):
    return (_layer_norm(ALPHA * x + y, g, b),)


def _lnres_block_twice(x, y, g, b):
    out = _layer_norm(ALPHA * x + y, g, b)
    return out, out


def _memln_block(x, g, b):
    return (_layer_norm(x, g, b),)


def _attn_block(q, kv):
    outs = []
    for h in range(X_HEADS):
        qh = q[:, h * X_HEADDIM:(h + 1) * X_HEADDIM].astype(BF16)
        kh = kv[:, h * X_HEADDIM:(h + 1) * X_HEADDIM].astype(BF16)
        vh = kv[:, D_MODEL + h * X_HEADDIM:D_MODEL + (h + 1) * X_HEADDIM].astype(BF16)
        s = lax.dot_general(qh, kh, (((1,), (1,)), ((), ())), preferred_element_type=F32) * (X_HEADDIM ** -0.5)
        s = s - lax.stop_gradient(jnp.max(s, axis=-1, keepdims=True))
        e = jnp.exp(s)
        p = e / jnp.sum(e, axis=-1, keepdims=True)
        outs.append(jnp.dot(p.astype(BF16), vh, preferred_element_type=F32))
    return (jnp.concatenate(outs, axis=1),)


def _swiglu_block(gu):
    return (_silu(gu[:, :FFN_HIDDEN]) * gu[:, FFN_HIDDEN:],)


class _Comm:
    def __init__(self):
        self.gathers = []
        self.scatters = []

    def operands(self):
        ins = [a for a, _ in self.gathers] + list(self.scatters)
        shapes = [jax.ShapeDtypeStruct((N_DEV,) + (a.shape if idx is None else a.shape[1:]), a.dtype)
                  for a, idx in self.gathers]
        shapes += [jax.ShapeDtypeStruct(a.shape, a.dtype) for a in self.scatters]
        scratch = []
        for n in (len(self.gathers), len(self.scatters)):
            if n:
                scratch += [pltpu.SemaphoreType.DMA((7 * n,)), pltpu.SemaphoreType.DMA((7 * n,)),
                            pltpu.SemaphoreType.DMA((n,))]
        return ins, shapes, scratch

    def _split(self, in_refs, out_refs, sems):
        ng = len(self.gathers)
        g_sems = sems[:3] if ng else None
        s_sems = sems[3:] if ng else sems
        return in_refs[:ng], in_refs[ng:], out_refs[:ng], out_refs[ng:], g_sems, s_sems

    def _gather_copies(self, i, src_ref, out_ref, sems):
        send_sems, recv_sems, local_sems = sems
        x, y, c = lax.axis_index("x"), lax.axis_index("y"), lax.axis_index("c")
        me, sibling = (x, y, c), (x, y, 1 - c)
        chips = [(1 - x, y), (x, 1 - y), (1 - x, 1 - y)]
        idx = self.gathers[i][1]
        src = src_ref if idx is None else src_ref.at[idx]

        def slot(px, py, pc):
            return out_ref.at[4 * px + 2 * py + pc]

        def copy(k, blk, to, from_src=False):
            return pltpu.make_async_remote_copy(
                src_ref=src if from_src else slot(*blk), dst_ref=slot(*blk), send_sem=send_sems.at[7 * i + k],
                recv_sem=recv_sems.at[7 * i + k], device_id=to, device_id_type=pl.DeviceIdType.MESH)

        mine = pltpu.make_async_copy(src, slot(*me), local_sems.at[i])
        first = [copy(0, me, sibling, True)] + [copy(1 + j, me, (*chip, c), True) for j, chip in enumerate(chips)]
        passed = [copy(4 + j, (*chip, c), sibling) for j, chip in enumerate(chips)]
        arrivals = [copy(1 + j, (*chip, c), me) for j, chip in enumerate(chips)]
        from_sibling = [copy(0, sibling, me)] + [copy(4 + j, (*chip, 1 - c), me) for j, chip in enumerate(chips)]
        return mine, first, passed, arrivals, from_sibling

    def _scatter_copies(self, i, src_ref, out_ref, sems):
        send_sems, recv_sems, local_sems = sems
        x, y, c = lax.axis_index("x"), lax.axis_index("y"), lax.axis_index("c")
        me = 4 * x + 2 * y + c
        mine = pltpu.make_async_copy(src_ref.at[me], out_ref.at[me], local_sems.at[i])
        copies = []
        for k in range(1, N_DEV):
            px = 1 - x if k & 4 else x
            py = 1 - y if k & 2 else y
            pc = 1 - c if k & 1 else c
            copies.append(pltpu.make_async_remote_copy(
                src_ref=src_ref.at[4 * px + 2 * py + pc], dst_ref=out_ref.at[me], send_sem=send_sems.at[7 * i + k - 1],
                recv_sem=recv_sems.at[7 * i + k - 1], device_id=(px, py, pc), device_id_type=pl.DeviceIdType.MESH))
        return mine, copies

    def start(self, in_refs, out_refs, sems):
        g_in, s_in, g_out, s_out, g_sems, s_sems = self._split(in_refs, out_refs, sems)
        for i in range(len(self.gathers)):
            mine, first, _, _, _ = self._gather_copies(i, g_in[i], g_out[i], g_sems)
            mine.start()
            for cp in first:
                cp.start()
        for i in range(len(self.scatters)):
            mine, copies = self._scatter_copies(i, s_in[i], s_out[i], s_sems)
            mine.start()
            for cp in copies:
                cp.start()

    def finish(self, in_refs, out_refs, sems):
        g_in, s_in, g_out, s_out, g_sems, s_sems = self._split(in_refs, out_refs, sems)
        parts = [self._gather_copies(i, g_in[i], g_out[i], g_sems) for i in range(len(self.gathers))]
        for j in range(3):
            for _, _, passed, arrivals, _ in parts:
                arrivals[j].wait_recv()
                passed[j].start()
        for mine, first, passed, _, from_sibling in parts:
            for cp in from_sibling:
                cp.wait_recv()
            for cp in first + passed:
                cp.wait_send()
            mine.wait()
        for i in range(len(self.scatters)):
            mine, copies = self._scatter_copies(i, s_in[i], s_out[i], s_sems)
            for cp in copies:
                cp.wait_recv()
            for cp in copies:
                cp.wait_send()
            mine.wait()


def _params(grid):
    return pltpu.CompilerParams(dimension_semantics=("arbitrary",) * len(grid), vmem_limit_bytes=VMEM_LIMIT)


def _call(name, body, *, grid, ins, in_specs, out_shape, out_specs, scratch=(), comm=None):
    n_in, n_out, n_scr = len(ins), len(out_shape), len(scratch)
    if comm is None:
        outs = pl.pallas_call(body, grid=grid, in_specs=list(in_specs), out_specs=list(out_specs),
                              out_shape=list(out_shape), scratch_shapes=list(scratch), name=name,
                              compiler_params=_params(grid))(*ins)
        return list(outs), []
    c_ins, c_shapes, c_scratch = comm.operands()
    nci, nco = len(c_ins), len(c_shapes)
    anywhere = pl.BlockSpec(memory_space=pl.ANY)

    def carrier(*refs):
        main_in, comm_in = refs[:n_in], refs[n_in:n_in + nci]
        o0 = n_in + nci
        main_out, comm_out = refs[o0:o0 + n_out], refs[o0 + n_out:o0 + n_out + nco]
        s0 = o0 + n_out + nco
        main_scr, comm_scr = refs[s0:s0 + n_scr], refs[s0 + n_scr:]
        first = pl.program_id(0) == 0
        last = pl.program_id(0) == grid[0] - 1
        for ax in range(1, len(grid)):
            first = first & (pl.program_id(ax) == 0)
            last = last & (pl.program_id(ax) == grid[ax] - 1)

        @pl.when(first)
        def _():
            comm.start(comm_in, comm_out, comm_scr)

        body(*main_in, *main_out, *main_scr)

        @pl.when(last)
        def _():
            comm.finish(comm_in, comm_out, comm_scr)

    outs = pl.pallas_call(carrier, grid=grid, in_specs=list(in_specs) + [anywhere] * nci,
                          out_specs=list(out_specs) + [anywhere] * nco, out_shape=list(out_shape) + c_shapes,
                          scratch_shapes=list(scratch) + c_scratch, name=name, compiler_params=_params(grid))(*ins, *c_ins)
    return list(outs[:n_out]), list(outs[n_out:])


def _comm_only(name, comm):
    c_ins, c_shapes, c_scratch = comm.operands()
    nci, nco = len(c_ins), len(c_shapes)
    anywhere = pl.BlockSpec(memory_space=pl.ANY)

    def body(*refs):
        comm.start(refs[:nci], refs[nci:nci + nco], refs[nci + nco:])
        comm.finish(refs[:nci], refs[nci:nci + nco], refs[nci + nco:])

    return list(pl.pallas_call(body, in_specs=[anywhere] * nci, out_specs=[anywhere] * nco, out_shape=c_shapes,
                               scratch_shapes=c_scratch, name=name)(*c_ins))


def _stage_fwd(name, f, grid, ins, in_specs, out_shapes, out_specs):
    n_in = len(ins)

    def body(*refs):
        res = f(*[r[...].astype(F32) for r in refs[:n_in]])
        for o_ref, val in zip(refs[n_in:], res):
            o_ref[...] = val.astype(o_ref.dtype)

    return _call(name, body, grid=grid, ins=ins, in_specs=in_specs, out_shape=out_shapes, out_specs=out_specs)[0]


def _stage_bwd(name, f, grid, ins, in_specs, cts, ct_specs, grads, comm=None):
    n_in = len(ins)
    flat_cts = [c for group in cts for c in group]
    flat_ct_specs = [s for group, spec in zip(cts, ct_specs) for s in (spec,) * len(group)]
    n_ct = len(flat_cts)
    diff = [g[0] for g in grads]

    def body(*refs):
        vals = [r[...].astype(F32) for r in refs[:n_in]]
        ct_refs = refs[n_in:n_in + n_ct]
        g_refs = refs[n_in + n_ct:]
        ct_vals, pos = [], 0
        for group in cts:
            acc = ct_refs[pos][...].astype(F32)
            for j in range(1, len(group)):
                acc = acc + ct_refs[pos + j][...].astype(F32)
            ct_vals.append(acc)
            pos += len(group)

        def g_fn(*dvals):
            full = list(vals)
            for i, dv in zip(diff, dvals):
                full[i] = dv
            return f(*full)

        _, vjp = jax.vjp(g_fn, *[vals[i] for i in diff])
        gvals = vjp(tuple(ct_vals))
        for gspec, g_ref, gval in zip(grads, g_refs, gvals):
            acc_axes = gspec[1]
            if not acc_axes:
                g_ref[...] = gval.astype(g_ref.dtype)
            else:
                first = pl.program_id(acc_axes[0]) == 0
                for ax in acc_axes[1:]:
                    first = first & (pl.program_id(ax) == 0)

                @pl.when(first)
                def _():
                    g_ref[...] = jnp.zeros_like(g_ref)

                g_ref[...] += gval.astype(g_ref.dtype)

    out_shapes, out_specs = [], []
    for gspec in grads:
        shape, spec = gspec[3] if len(gspec) > 3 else (ins[gspec[0]].shape, in_specs[gspec[0]])
        out_shapes.append(jax.ShapeDtypeStruct(shape, gspec[2]))
        out_specs.append(spec)
    return _call(name, body, grid=grid, ins=list(ins) + flat_cts, in_specs=list(in_specs) + flat_ct_specs,
                 out_shape=out_shapes, out_specs=out_specs, comm=comm)


def _pick_tile(n, candidates):
    for c in candidates:
        if n % c == 0:
            return c
    return n


def _matmul(name, a, b, *, ta=False, tb=False, add=None, out_dtype=F32, comm=None):
    if ta:
        k_dim, m = a.shape
    else:
        m, k_dim = a.shape
    n = b.shape[0] if tb else b.shape[1]
    assert (b.shape[1] if tb else b.shape[0]) == k_dim and not (ta and tb)
    tm = _pick_tile(m, (1024, 512, 256, 128))
    tn = _pick_tile(n, (1024, 1408, 2432, 512, 256, 128))
    if ta:
        tk = _pick_tile(k_dim, (1024, 512, 256, 128))
    elif k_dim <= 2816:
        tk = k_dim
    else:
        tk = _pick_tile(k_dim, (1408, 512, 256, 128))
    nk = k_dim // tk
    grid = (m // tm, n // tn, nk)
    a_spec = pl.BlockSpec((tk, tm), lambda i, j, k: (k, i)) if ta else pl.BlockSpec((tm, tk), lambda i, j, k: (i, k))
    b_spec = pl.BlockSpec((tn, tk), lambda i, j, k: (j, k)) if tb else pl.BlockSpec((tk, tn), lambda i, j, k: (k, j))
    o_spec = pl.BlockSpec((tm, tn), lambda i, j, k: (i, j))
    dims = (((0 if ta else 1,), (1 if tb else 0,)), ((), ()))
    has_add = add is not None

    def body(*refs):
        a_ref, b_ref = refs[0], refs[1]
        add_ref = refs[2] if has_add else None
        o_ref, acc_ref = refs[-2], refs[-1]
        k = pl.program_id(2)
        part = lax.dot_general(a_ref[...].astype(BF16), b_ref[...].astype(BF16), dims, preferred_element_type=F32)

        def finish(res):
            if has_add:
                res = res + add_ref[...].astype(F32)
            o_ref[...] = res.astype(o_ref.dtype)

        if nk == 1:
            finish(part)
        else:
            @pl.when(k == 0)
            def _():
                acc_ref[...] = part

            @pl.when((k > 0) & (k < nk - 1))
            def _():
                acc_ref[...] += part

            @pl.when(k == nk - 1)
            def _():
                finish(acc_ref[...] + part)

    ins = [a, b] + ([add] if has_add else [])
    in_specs = [a_spec, b_spec] + ([o_spec] if has_add else [])
    acc_shape = (tm, tn) if nk > 1 else (8, 128)
    outs, comm_outs = _call(name, body, grid=grid, ins=ins, in_specs=in_specs,
                            out_shape=[jax.ShapeDtypeStruct((m, n), out_dtype)], out_specs=[o_spec],
                            scratch=[pltpu.VMEM(acc_shape, F32)], comm=comm)
    return outs[0], comm_outs


SSD_STATE = (SSM_GROUPS * SSM_STATE, SSM_RPG * SSM_HEADDIM)


def _ssd_fwd(xc, dt_raw, proj, dt_bias, a_log, d_skip, norm_g, nb, nc, comm=None):
    t = xc.shape[0]
    row = lambda b, c: (b * nc + c, 0)
    par = lambda shape: pl.BlockSpec(shape, lambda b, c: (0, 0))

    def body(xc_ref, dt_ref, z_ref, dtb_ref, al_ref, ds_ref, ng_ref, e_ref, y_ref, ypre_ref, prev_ref, st_ref):
        @pl.when(pl.program_id(1) == 0)
        def _():
            st_ref[...] = jnp.zeros_like(st_ref)

        prev = st_ref[...]
        prev_ref[0, 0] = prev
        y, y_pre, new_state = _ssd_forward(xc_ref[...], dt_ref[...], z_ref[...].astype(F32), prev, dtb_ref[...],
                                           al_ref[...], ds_ref[...], ng_ref[...], e_ref[...])
        y_ref[...] = y.astype(y_ref.dtype)
        ypre_ref[...] = y_pre
        st_ref[...] = new_state

    return _call(
        "ssd_fwd", body, grid=(nb, nc), ins=[xc, dt_raw, proj, dt_bias, a_log, d_skip, norm_g, _head_expander()],
        in_specs=[pl.BlockSpec((CHUNK, SSM_CONV_DIM), row), pl.BlockSpec((CHUNK, 128), row),
                  pl.BlockSpec((CHUNK, SSM_INNER), lambda b, c: (b * nc + c, 1)),
                  par((1, 128)), par((1, 128)), par((1, SSM_INNER)), par((1, SSM_INNER)), par((128, SSM_INNER))],
        out_specs=[pl.BlockSpec((CHUNK, SSM_INNER), row), pl.BlockSpec((CHUNK, SSM_INNER), row),
                   pl.BlockSpec((1, 1) + SSD_STATE, lambda b, c: (b, c, 0, 0))],
        out_shape=[jax.ShapeDtypeStruct((t, SSM_INNER), BF16), jax.ShapeDtypeStruct((t, SSM_INNER), F32),
                   jax.ShapeDtypeStruct((nb, nc) + SSD_STATE, F32)],
        scratch=[pltpu.VMEM(SSD_STATE, F32)], comm=comm)


def _ssd_bwd(xc, dt_raw, proj, prevs, y_pre, dt_bias, a_log, d_skip, norm_g, dy, nb, nc, comm=None):
    t = xc.shape[0]
    row = lambda b, c: (b * nc + (nc - 1 - c), 0)
    par = lambda shape: pl.BlockSpec(shape, lambda b, c: (0, 0))

    def body(xc_ref, dt_ref, z_ref, prev_ref, ypre_ref, dtb_ref, al_ref, ds_ref, ng_ref, e_ref, dy_ref,
             dxc_ref, ddt_ref, dz_ref, ddtb_ref, dal_ref, dds_ref, dng_ref, dst_ref):
        @pl.when(pl.program_id(1) == 0)
        def _():
            dst_ref[...] = jnp.zeros_like(dst_ref)

        @pl.when((pl.program_id(0) == 0) & (pl.program_id(1) == 0))
        def _():
            ddtb_ref[...] = jnp.zeros_like(ddtb_ref)
            dal_ref[...] = jnp.zeros_like(dal_ref)
            dds_ref[...] = jnp.zeros_like(dds_ref)
            dng_ref[...] = jnp.zeros_like(dng_ref)

        dxc, ddt, dz, dprev, ddtb, dal, dds, dng = _ssd_backward(
            xc_ref[...], dt_ref[...], z_ref[...].astype(F32), prev_ref[0, 0], ypre_ref[...], dtb_ref[...], al_ref[...],
            ds_ref[...], ng_ref[...], e_ref[...], dy_ref[...].astype(F32), dst_ref[...])
        dxc_ref[...] = dxc
        ddt_ref[:, :128] = ddt.astype(ddt_ref.dtype)
        ddt_ref[:, 128:] = jnp.zeros((CHUNK, DT_PAD - 128), ddt_ref.dtype)
        dz_ref[...] = dz.astype(dz_ref.dtype)
        dst_ref[...] = dprev
        ddtb_ref[...] += ddtb
        dal_ref[...] += dal
        dds_ref[...] += dds
        dng_ref[...] += dng

    return _call(
        "ssd_bwd", body, grid=(nb, nc),
        ins=[xc, dt_raw, proj, prevs, y_pre, dt_bias, a_log, d_skip, norm_g, _head_expander(), dy],
        in_specs=[pl.BlockSpec((CHUNK, SSM_CONV_DIM), row), pl.BlockSpec((CHUNK, 128), row),
                  pl.BlockSpec((CHUNK, SSM_INNER), lambda b, c: (b * nc + (nc - 1 - c), 1)),
                  pl.BlockSpec((1, 1) + SSD_STATE, lambda b, c: (b, nc - 1 - c, 0, 0)),
                  pl.BlockSpec((CHUNK, SSM_INNER), row),
                  par((1, 128)), par((1, 128)), par((1, SSM_INNER)), par((1, SSM_INNER)), par((128, SSM_INNER)),
                  pl.BlockSpec((CHUNK, SSM_INNER), row)],
        out_specs=[pl.BlockSpec((CHUNK, SSM_CONV_DIM), row), pl.BlockSpec((CHUNK, DT_PAD), row),
                   pl.BlockSpec((CHUNK, SSM_INNER), row), par((1, 128)), par((1, 128)), par((1, 128)),
                   par((1, SSM_INNER))],
        out_shape=[jax.ShapeDtypeStruct((t, SSM_CONV_DIM), F32), jax.ShapeDtypeStruct((t, DT_PAD), BF16),
                   jax.ShapeDtypeStruct((t, SSM_INNER), BF16), jax.ShapeDtypeStruct((1, 128), F32),
                   jax.ShapeDtypeStruct((1, 128), F32), jax.ShapeDtypeStruct((1, 128), F32),
                   jax.ShapeDtypeStruct((1, SSM_INNER), F32)],
        scratch=[pltpu.VMEM(SSD_STATE, F32)], comm=comm)


def _loss_head(y, target):
    t, d = y.shape
    tm = _pick_tile(t, (256,))
    blk = pl.BlockSpec((tm, d), lambda i: (i, 0))

    def body(y_ref, t_ref, loss_ref, dy_ref):
        err = y_ref[...] - t_ref[...]
        dy_ref[...] = err * (1.0 / d)

        @pl.when(pl.program_id(0) == 0)
        def _():
            loss_ref[...] = jnp.zeros_like(loss_ref)

        loss_ref[...] += 0.5 * jnp.sum(jnp.mean(err * err, axis=-1, keepdims=True), axis=0, keepdims=True)

    return _call("loss_head", body, grid=(t // tm,), ins=[y, target], in_specs=[blk, blk],
                 out_specs=[pl.BlockSpec((1, 1), lambda i: (0, 0)), blk],
                 out_shape=[jax.ShapeDtypeStruct((1, 1), F32), jax.ShapeDtypeStruct((t, d), F32)])[0]


def _adamw_math(g, w, m, v):
    m_new = ADAM_B1 * m + (1.0 - ADAM_B1) * g
    v_new = ADAM_B2 * v + (1.0 - ADAM_B2) * jnp.square(g)
    m_hat = m_new / (1.0 - ADAM_B1 ** ADAM_STEP)
    v_hat = v_new / (1.0 - ADAM_B2 ** ADAM_STEP)
    delta = -ADAM_LR * (m_hat / (jnp.sqrt(v_hat) + ADAM_EPS) + ADAM_WD * w)
    return delta, m_new, v_new


def _adamw_sharded(name, parts, w, m, v):
    _, a, b = w.shape
    tr = _pick_tile(a, (128,))
    nt = a // tr
    part_specs = [pl.BlockSpec((N_DEV, tr, b),
                               (lambda l, i, _k=k: (0, jnp.where(l == _k, i, jnp.where(l > _k, nt - 1, 0)), 0)))
                  for k in range(DEPTH)]
    blk = pl.BlockSpec((1, tr, b), lambda l, i: (l, i, 0))

    def body(*refs):
        p_refs = refs[:DEPTH]
        w_ref, m_ref, v_ref, g_out, d_out, m_out, v_out = refs[DEPTH:]
        for k in range(DEPTH):
            @pl.when(pl.program_id(0) == k)
            def _(p_ref=p_refs[k]):
                g = p_ref[0].astype(F32)
                for p in range(1, N_DEV):
                    g = g + p_ref[p].astype(F32)
                delta, m_new, v_new = _adamw_math(g, w_ref[0], m_ref[0], v_ref[0])
                g_out[0] = g
                d_out[0] = delta
                m_out[0] = m_new
                v_out[0] = v_new

    return _call(name, body, grid=(DEPTH, nt), ins=list(parts) + [w, m, v], in_specs=part_specs + [blk, blk, blk],
                 out_specs=[blk] * 4, out_shape=[jax.ShapeDtypeStruct(w.shape, F32)] * 4)[0]


def _adamw_small(name, g, w, m, v):
    full = pl.BlockSpec(w.shape, lambda i: (0, 0))

    def body(g_ref, w_ref, m_ref, v_ref, d_out, m_out, v_out):
        delta, m_new, v_new = _adamw_math(g_ref[...], w_ref[...], m_ref[...], v_ref[...])
        d_out[...] = delta
        m_out[...] = m_new
        v_out[...] = v_new

    return _call(name, body, grid=(1,), ins=[g, w, m, v], in_specs=[full] * 4, out_specs=[full] * 3,
                 out_shape=[jax.ShapeDtypeStruct(w.shape, F32)] * 3)[0]


def _sum_parts(name, parts):
    n_parts, rows, cols = parts.shape
    tr = _pick_tile(rows, (512, 256, 128, 64, 32, 16, 8))

    def body(p_ref, o_ref):
        acc = p_ref[0]
        for p in range(1, n_parts):
            acc = acc + p_ref[p]
        o_ref[...] = acc

    return _call(name, body, grid=(rows // tr,), ins=[parts],
                 in_specs=[pl.BlockSpec((n_parts, tr, cols), lambda i: (0, i, 0))],
                 out_specs=[pl.BlockSpec((tr, cols), lambda i: (i, 0))],
                 out_shape=[jax.ShapeDtypeStruct((rows, cols), parts.dtype)])[0][0]


def _pack_w_in(w):
    pad = jnp.zeros((w.shape[0], DT_PAD - SSM_HEADS), w.dtype)
    return jnp.concatenate([w[:, :DT_COL], w[:, DT_COL:GA_COL], pad, w[:, GA_COL:]], axis=1)


def _unpack_w_in(w):
    return jnp.concatenate([w[:, :DT_COL + SSM_HEADS], w[:, P_GA:]], axis=1)


def _pad_heads(v):
    return jnp.pad(v, (0, 128 - SSM_HEADS)).reshape(1, 128)


def _run_step(x, mem, target, small, ex):
    nb, s, d = x.shape
    t = nb * s
    nc = s // CHUNK
    rows = _pick_tile(t, (256,))
    rows_wide = _pick_tile(t, (512, 256))
    tq = _pick_tile(s, (512, 256))
    vec = lambda a: a.reshape(1, -1)
    full1 = lambda shape: pl.BlockSpec(shape, lambda i: (0,) * len(shape))
    row1 = lambda tm, w: pl.BlockSpec((tm, w), lambda i: (i, 0))
    sds = jax.ShapeDtypeStruct

    def mm(call, l, a, b, **kw):
        comm = ex.before(call, l)
        out, comm_outs = _matmul(call, a, b, comm=comm, **kw)
        if comm is not None:
            ex.after(call, l, comm_outs)
        return out

    def stage_bwd(call, l, *args):
        comm = ex.before(call, l)
        outs, comm_outs = _stage_bwd(call, *args, comm=comm)
        if comm is not None:
            ex.after(call, l, comm_outs)
        return outs

    mem_specs = [row1(256, d), full1((1, d)), full1((1, d))]
    mem_ins = [mem.reshape(nb * MEM_LEN, d), vec(small["mem_ln_g"]), vec(small["mem_ln_b"])]
    (mem_n,) = _stage_fwd("memln_fwd", _memln_block, (nb * MEM_LEN // 256,), mem_ins, mem_specs,
                          [sds((nb * MEM_LEN, d), BF16)], [row1(256, d)])

    h = x.reshape(t, d)
    h_bf = h.astype(BF16)
    ln_specs = [row1(rows, d), row1(rows, d), full1((1, d)), full1((1, d))]
    ln_outs = [sds((t, d), F32), sds((t, d), BF16)]
    ln_out_specs = [row1(rows, d), row1(rows, d)]
    saved = []
    for l in range(DEPTH):
        sv = {"h_bf": h_bf}
        proj = mm("mm_in", l, h_bf, ex.weight("w_in", l), out_dtype=BF16)
        dt_raw = mm("mm_dt", l, h_bf, ex.weight("w_in", l)[:, DT_COL:DT_COL + 128])
        sv["proj"] = proj
        sgu_ins = [proj, proj, vec(small["sg_ln_g"][l]), vec(small["sg_ln_b"][l]), small["sg_w"][l], small["sg_b"][l].T]
        sgu_specs = [pl.BlockSpec((CHUNK, d), lambda i: (i, 0)), pl.BlockSpec((CHUNK, d), lambda i: (i, 1)),
                     full1((1, d)), full1((1, d)), full1((SG_GROUPS, CHUNK, CHUNK)), full1((CHUNK, SG_GROUPS))]
        (a_out,) = _stage_fwd("sgu_fwd", _sgu_block, (t // CHUNK,), sgu_ins, sgu_specs, [sds((t, d), BF16)],
                              [row1(CHUNK, d)])
        sv["sgu"] = (sgu_ins, sgu_specs)
        sv["a_out"] = a_out
        cw = 256
        conv_ins = [proj, small["conv_w"][l], vec(small["conv_b"][l])]
        conv_specs = [pl.BlockSpec((s, cw), lambda j, b: (b, XBC_COL // cw + j)),
                      pl.BlockSpec((SSM_CONV, cw), lambda j, b: (0, j)), pl.BlockSpec((1, cw), lambda j, b: (0, j))]
        conv_out_spec = pl.BlockSpec((s, cw), lambda j, b: (b, j))
        (xc,) = _stage_fwd("conv_fwd", _conv_block, (SSM_CONV_DIM // cw, nb), conv_ins, conv_specs,
                           [sds((t, SSM_CONV_DIM), F32)], [conv_out_spec])
        sv["conv"] = (conv_ins, conv_specs, conv_out_spec)
        ssd_par = [_pad_heads(small["dt_bias"][l]), _pad_heads(small["a_log"][l]),
                   vec(jnp.repeat(small["d_skip"][l], SSM_HEADDIM)), vec(small["ssm_norm_g"][l])]
        comm = ex.before("ssd_fwd", l)
        (y_ssd, y_pre, prevs), comm_outs = _ssd_fwd(xc, dt_raw, proj, *ssd_par, nb, nc, comm=comm)
        if comm is not None:
            ex.after("ssd_fwd", l, comm_outs)
        sv["ssd"] = (xc, dt_raw, prevs, y_pre, ssd_par)
        sv["y_ssd"] = y_ssd
        br_a = mm("mm_sq", l, a_out, ex.weight("p_a", l), out_dtype=BF16)
        br_b = mm("mm_pb", l, y_ssd, ex.weight("p_b", l), out_dtype=BF16)
        mw = 512
        merge_ins = [proj, proj, br_a, br_b]
        merge_out_spec = pl.BlockSpec((rows_wide, mw), lambda i, j: (i, j))
        merge_specs = [pl.BlockSpec((rows_wide, mw), lambda i, j: (i, P_GA // mw + j)),
                       pl.BlockSpec((rows_wide, mw), lambda i, j: (i, P_GB // mw + j)), merge_out_spec, merge_out_spec]
        (merged,) = _stage_fwd("merge_fwd", _merge_block, (t // rows_wide, d // mw), merge_ins, merge_specs,
                               [sds((t, d), BF16)], [merge_out_spec])
        sv["merge"] = (merge_ins, merge_specs, merge_out_spec)
        sv["merged"] = merged
        y1 = mm("mm_sq", l, merged, ex.weight("w_mix_o", l), out_dtype=BF16)
        ln1_ins = [h, y1, vec(small["ln_g"][l, 0]), vec(small["ln_b"][l, 0])]
        h1, h1_bf = _stage_fwd("lnres_fwd", _lnres_block_twice, (t // rows,), ln1_ins, ln_specs, ln_outs, ln_out_specs)
        sv["ln1"] = ln1_ins
        q = mm("mm_sq", l, h1_bf, ex.weight("w_xq", l), out_dtype=BF16)
        kv = mm("mm_kv", l, mem_n, ex.weight("w_xkv", l), out_dtype=BF16)
        attn_ins = [q, kv]
        attn_out_spec = pl.BlockSpec((tq, d), lambda b, i: (b * (s // tq) + i, 0))
        attn_specs = [attn_out_spec, pl.BlockSpec((MEM_LEN, 2 * d), lambda b, i: (b, 0))]
        (o,) = _stage_fwd("attn_fwd", _attn_block, (nb, s // tq), attn_ins, attn_specs, [sds((t, d), BF16)],
                          [attn_out_spec])
        sv["attn"] = (attn_ins, attn_specs, attn_out_spec)
        sv["o"] = o
        sv["h1_bf"] = h1_bf
        y2 = mm("mm_sq", l, o, ex.weight("w_xo", l), out_dtype=BF16)
        ln2_ins = [h1, y2, vec(small["ln_g"][l, 1]), vec(small["ln_b"][l, 1])]
        h2, h2_bf = _stage_fwd("lnres_fwd", _lnres_block_twice, (t // rows,), ln2_ins, ln_specs, ln_outs, ln_out_specs)
        sv["ln2"] = ln2_ins
        sv["h2_bf"] = h2_bf
        gu = mm("mm_ffn_in", l, h2_bf, ex.weight("w_ffn_in", l), out_dtype=BF16)
        (act,) = _stage_fwd("swiglu_fwd", _swiglu_block, (t // 128,), [gu], [row1(128, 2 * FFN_HIDDEN)],
                            [sds((t, FFN_HIDDEN), BF16)], [row1(128, FFN_HIDDEN)])
        sv["gu"] = gu
        sv["act"] = act
        y3 = mm("mm_ffn_out", l, act, ex.weight("w_ffn_out", l), out_dtype=BF16)
        ln3_ins = [h2, y3, vec(small["ln_g"][l, 2]), vec(small["ln_b"][l, 2])]
        h, h_bf = _stage_fwd("lnres_fwd", _lnres_block_twice, (t // rows,), ln3_ins, ln_specs, ln_outs, ln_out_specs)
        sv["ln3"] = ln3_ins
        saved.append(sv)

    loss, dh = _loss_head(h, target.reshape(t, d))

    g_small = {n: [None] * DEPTH for n in SMALL_REP + SMALL_SH if n not in ("mem_ln_g", "mem_ln_b")}
    dmem_n = []
    ln_grads = [(0, (), F32), (1, (), BF16), (2, (0,), F32), (3, (0,), F32)]
    for l in reversed(range(DEPTH)):
        sv = saved[l]
        dln_g, dln_b = [None] * 3, [None] * 3
        dres, dy3, dln_g[2], dln_b[2] = stage_bwd("lnres_bwd", l, _lnres_block, (t // rows,), sv["ln3"], ln_specs,
                                                  [(dh,)], [row1(rows, d)], ln_grads)
        ex.grad("w_ffn_out", l, mm("mm_ffn_out_dw", l, sv["act"], dy3, ta=True, out_dtype=BF16))
        dact = mm("mm_ffn_out_dx", l, dy3, ex.weight("w_ffn_out", l), tb=True, out_dtype=BF16)
        (dgu,) = stage_bwd("swiglu_bwd", l, _swiglu_block, (t // 128,), [sv["gu"]], [row1(128, 2 * FFN_HIDDEN)],
                           [(dact,)], [row1(128, FFN_HIDDEN)], [(0, (), BF16)])
        ex.grad("w_ffn_in", l, mm("mm_ffn_in_dw", l, sv["h2_bf"], dgu, ta=True, out_dtype=BF16))
        dh2 = mm("mm_ffn_in_dx", l, dgu, ex.weight("w_ffn_in", l), tb=True, add=dres)
        dres, dy2, dln_g[1], dln_b[1] = stage_bwd("lnres_bwd", l, _lnres_block, (t // rows,), sv["ln2"], ln_specs,
                                                  [(dh2,)], [row1(rows, d)], ln_grads)
        ex.grad("w_xo", l, mm("mm_sq_dw", l, sv["o"], dy2, ta=True, out_dtype=BF16))
        do = mm("mm_sq_dx", l, dy2, ex.weight("w_xo", l), tb=True, out_dtype=BF16)
        attn_ins, attn_specs, attn_out_spec = sv["attn"]
        dq, dkv = stage_bwd("attn_bwd", l, _attn_block, (nb, s // tq), attn_ins, attn_specs, [(do,)], [attn_out_spec],
                            [(0, (), BF16), (1, (1,), F32)])
        ex.grad("w_xq", l, mm("mm_sq_dw", l, sv["h1_bf"], dq, ta=True, out_dtype=BF16))
        dh1 = mm("mm_sq_dx", l, dq, ex.weight("w_xq", l), tb=True, add=dres)
        ex.grad("w_xkv", l, mm("mm_kv_dw", l, mem_n, dkv, ta=True, out_dtype=BF16))
        dmem_n.append(mm("mm_kv_dx", l, dkv, ex.weight("w_xkv", l), tb=True))
        dres, dy1, dln_g[0], dln_b[0] = stage_bwd("lnres_bwd", l, _lnres_block, (t // rows,), sv["ln1"], ln_specs,
                                                  [(dh1,)], [row1(rows, d)], ln_grads)
        g_small["ln_g"][l] = jnp.concatenate(dln_g, axis=0)
        g_small["ln_b"][l] = jnp.concatenate(dln_b, axis=0)
        ex.grad("w_mix_o", l, mm("mm_sq_dw", l, sv["merged"], dy1, ta=True, out_dtype=BF16))
        dmerged = mm("mm_sq_dx", l, dy1, ex.weight("w_mix_o", l), tb=True, out_dtype=BF16)
        merge_ins, merge_specs, merge_out_spec = sv["merge"]
        dga, dgb, dbr_a, dbr_b = stage_bwd("merge_bwd", l, _merge_block, (t // rows_wide, d // 512), merge_ins,
                                           merge_specs, [(dmerged,)], [merge_out_spec],
                                           [(i, (), BF16, ((t, d), merge_out_spec)) for i in range(4)])
        ex.grad("p_a", l, mm("mm_sq_dw", l, sv["a_out"], dbr_a, ta=True, out_dtype=BF16))
        da_out = mm("mm_sq_dx", l, dbr_a, ex.weight("p_a", l), tb=True, out_dtype=BF16)
        ex.grad("p_b", l, mm("mm_pb_dw", l, sv["y_ssd"], dbr_b, ta=True, out_dtype=BF16))
        dy_ssd = mm("mm_pb_dx", l, dbr_b, ex.weight("p_b", l), tb=True, out_dtype=BF16)
        sgu_ins, sgu_specs = sv["sgu"]
        du, dv, dsg_ln_g, dsg_ln_b, dsg_w, dsg_b = stage_bwd(
            "sgu_bwd", l, _sgu_block, (t // CHUNK,), sgu_ins, sgu_specs, [(da_out,)], [row1(CHUNK, d)],
            [(0, (), BF16, ((t, d), row1(CHUNK, d))), (1, (), BF16, ((t, d), row1(CHUNK, d))), (2, (0,), F32),
             (3, (0,), F32), (4, (0,), F32), (5, (0,), F32)])
        g_small["sg_ln_g"][l], g_small["sg_ln_b"][l], g_small["sg_w"][l], g_small["sg_b"][l] = (
            dsg_ln_g[0], dsg_ln_b[0], dsg_w, dsg_b.T)
        xc, dt_raw, prevs, y_pre, ssd_par = sv["ssd"]
        comm = ex.before("ssd_bwd", l)
        (dxc, ddt, dz, ddtb, dal, dds, dng), comm_outs = _ssd_bwd(xc, dt_raw, sv["proj"], prevs, y_pre, *ssd_par, dy_ssd,
                                                                  nb, nc, comm=comm)
        if comm is not None:
            ex.after("ssd_bwd", l, comm_outs)
        g_small["dt_bias"][l], g_small["a_log"][l], g_small["d_skip"][l] = (
            ddtb[0, :SSM_HEADS], dal[0, :SSM_HEADS], dds[0, :SSM_HEADS])
        g_small["ssm_norm_g"][l] = dng[0]
        conv_ins, conv_specs, conv_out_spec = sv["conv"]
        dxbc, dconv_w, dconv_b = stage_bwd("conv_bwd", l, _conv_block, (SSM_CONV_DIM // 256, nb), conv_ins, conv_specs,
                                           [(dxc,)], [conv_out_spec],
                                           [(0, (), BF16, ((t, SSM_CONV_DIM), conv_out_spec)), (1, (1,), F32),
                                            (2, (1,), F32)])
        g_small["conv_w"][l], g_small["conv_b"][l] = dconv_w, dconv_b[0]
        dproj = jnp.concatenate([du, dv, dz, dxbc, ddt, dga, dgb], axis=1)
        ex.grad("w_in", l, mm("mm_in_dw", l, sv["h_bf"], dproj, ta=True, out_dtype=BF16))
        if l == 0:
            dmg, dmb = stage_bwd("memln_bwd", l, _memln_block, (nb * MEM_LEN // 256,), mem_ins, mem_specs,
                                 [tuple(dmem_n)], [row1(256, d)], [(1, (0,), F32), (2, (0,), F32)])
            done = {n: jnp.stack(g, axis=0) for n, g in g_small.items()}
            done["mem_ln_g"], done["mem_ln_b"] = dmg[0], dmb[0]
            ex.small_grads(done)
        dh = mm("mm_in_dx", l, dproj, ex.weight("w_in", l), tb=True, add=dres)

    return loss, dh.reshape(nb, s, d)


def _pack_flat(arrays, rows):
    flat = jnp.concatenate([a.reshape(-1) for a in arrays])
    return jnp.pad(flat, (0, rows * 128 - flat.shape[0])).reshape(rows, 128)


def _unpack_flat(packed, shapes):
    lead = packed.shape[:-2]
    flat = packed.reshape(lead + (-1,))
    out, pos = [], 0
    for shape in shapes:
        n = math.prod(shape)
        out.append(flat[..., pos:pos + n].reshape(lead + tuple(shape)))
        pos += n
    return out


def _small_rows(n_elems):
    return -(-n_elems // (128 * SMALL_ROW_TILE)) * SMALL_ROW_TILE


def _from_shards(name, gathered):
    _, a, b = gathered.shape
    if name in BIG_COL_SHARDED:
        w = gathered.transpose(1, 0, 2).reshape(a, N_DEV * b)
        return _pack_w_in(w) if name == "w_in" else w
    return gathered.reshape(N_DEV * a, b)


def _to_shards(name, g):
    if name in BIG_COL_SHARDED:
        g = _unpack_w_in(g) if name == "w_in" else g
        a, nb = g.shape
        return g.reshape(a, N_DEV, nb // N_DEV).transpose(1, 0, 2)
    a, b = g.shape
    return g.reshape(N_DEV, a // N_DEV, b)


class _MeshExchange:
    def __init__(self, shards_bf16, first):
        self.shards = shards_bf16
        self.full = dict(first)
        self.grads = {}
        self.received = {}
        self.small = None
        self.small_gathered = None

    def weight(self, name, l):
        return self.full[(name, l)]

    def grad(self, name, l, g):
        self.grads[(name, l)] = g

    def small_grads(self, done):
        self.small = done

    def before(self, call, l):
        comm = _Comm()
        for name, layer in GATHER_PLAN.get((call, l), ()):
            comm.gathers.append((self.shards[name], layer))
        for name, layer in SCATTER_PLAN.get((call, l), ()):
            comm.scatters.append(_to_shards(name, self.grads[(name, layer)]))
        if (call, l) == ("mm_in_dx", 0):
            names = SMALL_REP + SMALL_SH
            rows = _small_rows(sum(math.prod(self.small[n].shape) for n in names))
            comm.gathers.append((_pack_flat([self.small[n] for n in names], rows), None))
        return comm if comm.gathers or comm.scatters else None

    def after(self, call, l, outs):
        gathers = list(GATHER_PLAN.get((call, l), ()))
        for (name, layer), out in zip(gathers, outs):
            self.full[(name, layer)] = _from_shards(name, out)
        outs = outs[len(gathers):]
        if (call, l) == ("mm_in_dx", 0):
            self.small_gathered = outs[0]
            outs = outs[1:]
        for (name, layer), out in zip(SCATTER_PLAN.get((call, l), ()), outs):
            self.received[(name, layer)] = out


def kernel(x, mem, mem_ln_g, mem_ln_b, w_in, sg_ln_g, sg_ln_b, sg_w, sg_b, conv_w, conv_b, dt_bias, a_log, d_skip, ssm_norm_g, p_a, p_b, w_mix_o, w_xq, w_xkv, w_xo, w_ffn_in, w_ffn_out, ln_g, ln_b, loss_target, m_mem_ln_g, m_mem_ln_b, m_w_in, m_sg_ln_g, m_sg_ln_b, m_sg_w, m_sg_b, m_conv_w, m_conv_b, m_dt_bias, m_a_log, m_d_skip, m_ssm_norm_g, m_p_a, m_p_b, m_w_mix_o, m_w_xq, m_w_xkv, m_w_xo, m_w_ffn_in, m_w_ffn_out, m_ln_g, m_ln_b, v_mem_ln_g, v_mem_ln_b, v_w_in, v_sg_ln_g, v_sg_ln_b, v_sg_w, v_sg_b, v_conv_w, v_conv_b, v_dt_bias, v_a_log, v_d_skip, v_ssm_norm_g, v_p_a, v_p_b, v_w_mix_o, v_w_xq, v_w_xkv, v_w_xo, v_w_ffn_in, v_w_ffn_out, v_ln_g, v_ln_b):
    args = dict(locals())
    w = {n: args[n] for n in WEIGHTS}
    m = {n: args["m_" + n] for n in WEIGHTS}
    v = {n: args["v_" + n] for n in WEIGHTS}
    me = 4 * lax.axis_index("x") + 2 * lax.axis_index("y") + lax.axis_index("c")

    shards = {n: w[n].astype(BF16) for n in BIG}
    sh_shapes = [w[n].shape for n in SMALL_SH]
    first = _Comm()
    first.gathers.append((shards["w_in"], 0))
    first.gathers.append((_pack_flat([w[n] for n in SMALL_SH], _small_rows(sum(math.prod(s) for s in sh_shapes))), None))
    w_in0, small_sh = _comm_only("gather_first", first)
    small = {n: w[n] for n in SMALL_REP}
    for n, sh in zip(SMALL_SH, _unpack_flat(small_sh, sh_shapes)):
        small[n] = sh.transpose(1, 2, 0, 3).reshape(sh.shape[1], sh.shape[2], N_DEV * sh.shape[3])

    ex = _MeshExchange(shards, {("w_in", 0): _from_shards("w_in", w_in0)})
    loss, grad_x = _run_step(x, mem, loss_target, small, ex)
    loss = lax.psum(loss[0, 0], ("x", "y", "c"))

    out = {}
    for n in BIG:
        out[n] = _adamw_sharded("adamw_" + n, [ex.received[(n, l)] for l in range(DEPTH)], w[n], m[n], v[n])
    names = SMALL_REP + SMALL_SH
    g_small = dict(zip(names, _unpack_flat(_sum_parts("sum_small_grads", ex.small_gathered),
                                           [ex.small[n].shape for n in names])))
    for n in names:
        g = g_small[n]
        if n in SMALL_SH:
            width = w[n].shape[-1]
            g = lax.dynamic_slice_in_dim(g, me * width, width, axis=-1)
        two_d = (-1, w[n].shape[-1])
        res = _adamw_small("adamw_" + n, g.reshape(two_d), w[n].reshape(two_d), m[n].reshape(two_d), v[n].reshape(two_d))
        out[n] = [g] + [r.reshape(w[n].shape) for r in res]

    results = []
    for k in range(4):
        results.extend(out[n][k] for n in WEIGHTS)
    return (loss, grad_x, *results)
```

```python
import functools
import math

import jax
import jax.numpy as jnp
from jax import lax
from jax.experimental import pallas as pl
from jax.experimental.pallas import tpu as pltpu

F32 = jnp.float32
BF16 = jnp.bfloat16
HIGHEST = lax.Precision.HIGHEST

N_DEV = 8
D_MODEL = 1024
DEPTH = 2
MEM_LEN = 256
CHUNK = 128
SG_GROUPS = 8
SSM_INNER = 2048
SSM_HEADDIM = 64
SSM_HEADS = 32
SSM_STATE = 128
SSM_GROUPS = 4
SSM_RPG = 8
SSM_CONV = 4
SSM_CONV_DIM = 3072
X_HEADS = 4
X_HEADDIM = 256
FFN_HIDDEN = 2816
ALPHA = float((2 * DEPTH) ** 0.25)
LN_EPS = 1e-5
RMS_EPS = 1e-5
XBC_COL = 4096
DT_COL = 7168
GA_COL = 7200
IN_COLS = 9248
P_GATE = 4096
P_XBC = 6144
P_COLS = 9216
DT_LANES = 128

ADAM_LR = 0.001
ADAM_B1 = 0.9
ADAM_B2 = 0.999
ADAM_EPS = 1e-08
ADAM_WD = 0.01
ADAM_STEP = 10

VMEM_LIMIT = 48 * 1024 * 1024
SMALL_ROW_TILE = 256

BIG = ("w_in", "p_a", "p_b", "w_mix_o", "w_xq", "w_xkv", "w_xo", "w_ffn_in", "w_ffn_out")
BIG_COL_SHARDED = ("w_in", "w_xkv", "w_ffn_in")
SMALL_REP = ("mem_ln_g", "mem_ln_b", "sg_ln_g", "sg_ln_b", "sg_w", "sg_b", "conv_b", "dt_bias", "a_log", "d_skip",
             "ssm_norm_g")
SMALL_SH = ("conv_w", "ln_g", "ln_b")
WEIGHTS = ("mem_ln_g", "mem_ln_b", "w_in", "sg_ln_g", "sg_ln_b", "sg_w", "sg_b", "conv_w", "conv_b", "dt_bias", "a_log",
           "d_skip", "ssm_norm_g", "p_a", "p_b", "w_mix_o", "w_xq", "w_xkv", "w_xo", "w_ffn_in", "w_ffn_out", "ln_g", "ln_b")

REST = ("p_a", "p_b", "w_mix_o", "w_xq", "w_xkv", "w_xo", "w_ffn_in", "w_ffn_out")
GATHER_PLAN = {
    ("mm_in", 0): [(n, 0) for n in REST],
    ("ssd_fwd", 0): [("w_in", 1)],
    ("mm_ffn_in", 0): [(n, 1) for n in ("p_a", "p_b", "w_mix_o", "w_xq", "w_xkv", "w_xo")],
    ("mm_in", 1): [("w_ffn_in", 1), ("w_ffn_out", 1)],
}
SCATTER_PLAN = {}
for _l in range(DEPTH):
    SCATTER_PLAN[("swiglu_bwd", _l)] = [("w_ffn_out", _l)]
    SCATTER_PLAN[("sgu_bwd", _l)] = [("w_mix_o", _l), ("p_a", _l)]
    SCATTER_PLAN[("ssd_bwd", _l)] = [("w_ffn_in", _l), ("w_xkv", _l), ("w_xq", _l), ("w_xo", _l)]
    SCATTER_PLAN[("conv_bwd", _l)] = [("p_b", _l)]
    SCATTER_PLAN[("mm_in_dx", _l)] = [("w_in", _l)]


def _layer_norm(x, g, b):
    mu = jnp.mean(x, axis=-1, keepdims=True)
    xc = x - mu
    var = jnp.mean(xc * xc, axis=-1, keepdims=True)
    return xc * lax.rsqrt(var + LN_EPS) * g + b


def _gelu(x):
    return 0.5 * x * (1.0 + lax.erf(x * (1.0 / math.sqrt(2.0))))


def _silu(x):
    return x * jax.nn.sigmoid(x)


def _softplus(x):
    return jnp.maximum(x, 0.0) + jnp.log1p(jnp.exp(-jnp.abs(x)))


def _causal_mask():
    r = lax.broadcasted_iota(jnp.int32, (CHUNK, CHUNK), 0)
    c = lax.broadcasted_iota(jnp.int32, (CHUNK, CHUNK), 1)
    return r >= c


def _sgu_block(uv, ln_g, ln_b, w, sb):
    gu = _gelu(uv[:, :D_MODEL])
    vn = _layer_norm(_gelu(uv[:, D_MODEL:]), ln_g, ln_b)
    causal = _causal_mask()
    width = D_MODEL // SG_GROUPS
    outs = []
    for g in range(SG_GROUPS):
        wg = jnp.where(causal, w[g], 0.0).astype(BF16)
        mixed = jnp.dot(wg, vn[:, g * width:(g + 1) * width].astype(BF16), preferred_element_type=F32)
        outs.append(mixed + sb[:, g:g + 1])
    return (gu * jnp.concatenate(outs, axis=1),)


GROUP_W = SSM_RPG * SSM_HEADDIM
NT_DIMS = (((1,), (1,)), ((), ()))
TN_DIMS = (((0,), (0,)), ((), ()))


def _mxu(a, b, dims=(((1,), (0,)), ((), ()))):
    return lax.dot_general(a.astype(BF16), b.astype(BF16), dims, preferred_element_type=F32)


def _head_expander():
    return (jnp.arange(SSM_INNER)[None, :] // SSM_HEADDIM == jnp.arange(128)[:, None]).astype(BF16)


def _bf16_terms(x, n):
    terms = []
    for _ in range(n):
        t = x.astype(BF16)
        terms.append(t)
        x = x - t.astype(F32)
    return terms


def _expand_heads(q, e):
    return sum(jnp.dot(t, e, preferred_element_type=F32) for t in _bf16_terms(q, 3))


def _reduce_heads(v, e):
    return sum(lax.dot_general(t, e, NT_DIMS, preferred_element_type=F32) for t in _bf16_terms(v, 2))


def _ssd_common(xc, dtraw, dt_bias, a_log, e):
    xs = xc[:, :SSM_INNER]
    pre = dtraw + dt_bias
    dt = _softplus(pre)
    a = -jnp.exp(a_log)
    r_i = lax.broadcasted_iota(jnp.int32, (CHUNK, CHUNK), 0)
    c_i = lax.broadcasted_iota(jnp.int32, (CHUNK, CHUNK), 1)
    tril = jnp.where(r_i >= c_i, 1.0, 0.0).astype(F32)
    cs = jnp.dot(tril, dt * a, precision=HIGHEST, preferred_element_type=F32)
    cs_last = cs[CHUNK - 1:CHUNK, :]
    decay_in = jnp.exp(cs)
    decay_st = jnp.exp(cs_last - cs)
    dt_x = _expand_heads(dt, e)
    w_st_x = _expand_heads(dt * decay_st, e)
    decay_in_x = _expand_heads(decay_in, e)
    return dict(xs=xs, pre=pre, dt=dt, a=a, lower=r_i >= c_i, upper=c_i >= r_i, cs=cs, cs_t=cs.T, decay_in=decay_in,
                decay_st=decay_st, chunk_decay=jnp.exp(cs_last), dt_x=dt_x, w_st_x=w_st_x, decay_in_x=decay_in_x,
                chunk_decay_x=decay_in_x[CHUNK - 1:CHUNK, :], xdt=xs * dt_x, x_st=(xs * w_st_x).astype(BF16),
                low=lax.broadcasted_iota(jnp.int32, (CHUNK, 128), 1) < SSM_HEADDIM)


def _pair_decay(c, h):
    return jnp.exp(jnp.where(c["lower"], c["cs"][:, h:h + 1] - c["cs_t"][h:h + 1, :], -1e30))


def _pair_decay_t(c, h):
    return jnp.exp(jnp.where(c["upper"], c["cs_t"][h:h + 1, :] - c["cs"][:, h:h + 1], -1e30))


def _ssd_forward(xc, dtraw, z, prev, dt_bias, a_log, d_skip_x, norm_g, e):
    c = _ssd_common(xc, dtraw, dt_bias, a_log, e)
    y_groups, new_states = [], []
    for g in range(SSM_GROUPS):
        lanes = slice(g * GROUP_W, (g + 1) * GROUP_W)
        bg = xc[:, SSM_INNER + g * SSM_STATE:SSM_INNER + (g + 1) * SSM_STATE]
        cg = xc[:, SSM_INNER + (SSM_GROUPS + g) * SSM_STATE:SSM_INNER + (SSM_GROUPS + g + 1) * SSM_STATE].astype(BF16)
        pg = prev[g * SSM_STATE:(g + 1) * SSM_STATE, :]
        cb = _mxu(cg, bg, NT_DIMS)
        y_in = _mxu(cg, pg) * c["decay_in_x"][:, lanes]
        new_states.append(pg * c["chunk_decay_x"][:, lanes] + _mxu(bg.T, c["x_st"][:, lanes]))
        pairs = []
        for j in range(SSM_RPG // 2):
            h0 = g * SSM_RPG + 2 * j
            xp = c["xdt"][:, 128 * (h0 // 2):128 * (h0 // 2 + 1)]
            pairs.append(_mxu(cb * _pair_decay(c, h0), jnp.where(c["low"], xp, 0.0))
                         + _mxu(cb * _pair_decay(c, h0 + 1), jnp.where(c["low"], 0.0, xp)))
        y_groups.append(jnp.concatenate(pairs, axis=1) + y_in)
    y_pre = jnp.concatenate(y_groups, axis=1) + c["xs"] * d_skip_x
    gated = y_pre * _silu(z)
    normed = [gated[:, g * GROUP_W:(g + 1) * GROUP_W] for g in range(SSM_GROUPS)]
    normed = [yg * lax.rsqrt(jnp.mean(yg * yg, axis=-1, keepdims=True) + RMS_EPS) for yg in normed]
    return jnp.concatenate(normed, axis=1) * norm_g, y_pre, jnp.concatenate(new_states, axis=0)


def _ssd_backward(xc, dtraw, z, prev, y_pre, dt_bias, a_log, d_skip_x, norm_g, e, dout, dnew):
    c = _ssd_common(xc, dtraw, dt_bias, a_log, e)
    xs = c["xs"]
    sig = jax.nn.sigmoid(z)
    silu_z = z * sig
    gated = y_pre * silu_z
    d_gated, normed = [], []
    for g in range(SSM_GROUPS):
        lanes = slice(g * GROUP_W, (g + 1) * GROUP_W)
        yg = gated[:, lanes]
        r = lax.rsqrt(jnp.mean(yg * yg, axis=-1, keepdims=True) + RMS_EPS)
        n = yg * r
        gh = dout[:, lanes] * norm_g[:, lanes]
        d_gated.append(r * (gh - n * jnp.mean(gh * n, axis=-1, keepdims=True)))
        normed.append(n)
    d_gated = jnp.concatenate(d_gated, axis=1)
    dnorm_g = jnp.sum(dout * jnp.concatenate(normed, axis=1), axis=0, keepdims=True)
    dy = d_gated * silu_z
    dz = d_gated * y_pre * (sig * (1.0 + z * (1.0 - sig)))
    dxs = dy * d_skip_x
    dd_skip = jnp.sum(_reduce_heads(dy * xs, e), axis=0, keepdims=True)

    lane = lax.broadcasted_iota(jnp.int32, (CHUNK, 128), 1)
    sub = lax.broadcasted_iota(jnp.int32, (8, 128), 0)
    dcs_neg = jnp.zeros((CHUNK, 128), F32)
    row_slabs = []
    dxdt, dx_st, d_decay_in_x, dprev, d_chunk_decay_x, db_all, dc_all = [], [], [], [], [], [], []
    for g in range(SSM_GROUPS):
        lanes = slice(g * GROUP_W, (g + 1) * GROUP_W)
        bg = xc[:, SSM_INNER + g * SSM_STATE:SSM_INNER + (g + 1) * SSM_STATE].astype(BF16)
        cg_f = xc[:, SSM_INNER + (SSM_GROUPS + g) * SSM_STATE:SSM_INNER + (SSM_GROUPS + g + 1) * SSM_STATE]
        cg = cg_f.astype(BF16)
        pg = prev[g * SSM_STATE:(g + 1) * SSM_STATE, :]
        dng = dnew[g * SSM_STATE:(g + 1) * SSM_STATE, :]
        dy_g = dy[:, lanes]
        cb_t = _mxu(bg, cg, NT_DIMS)
        t1 = (dy_g * c["decay_in_x"][:, lanes]).astype(BF16)
        d_decay_in_x.append(dy_g * _mxu(cg, pg))
        dc = _mxu(t1, pg, NT_DIMS)
        dprev.append(_mxu(cg_f.T, t1) + dng * c["chunk_decay_x"][:, lanes])
        d_chunk_decay_x.append(dng * pg)
        db = _mxu(c["x_st"][:, lanes], dng, NT_DIMS)
        dx_st.append(_mxu(bg, dng))
        dcb_t = jnp.zeros((CHUNK, CHUNK), F32)
        rows = []
        for j in range(SSM_RPG // 2):
            h0 = g * SSM_RPG + 2 * j
            blk = slice(128 * (h0 // 2), 128 * (h0 // 2 + 1))
            xp = c["xdt"][:, blk]
            dyp = dy[:, blk].astype(BF16)
            pair_dx = []
            for k, xk in enumerate((jnp.where(c["low"], xp, 0.0), jnp.where(c["low"], 0.0, xp))):
                dec_t = _pair_decay_t(c, h0 + k)
                pair_dx.append(_mxu(cb_t * dec_t, dyp))
                dml_t = _mxu(xk, dyp, NT_DIMS) * dec_t
                dcb_t = dcb_t + dml_t
                dseg_t = dml_t * cb_t
                dcs_neg = dcs_neg + jnp.where(lane == h0 + k, jnp.sum(dseg_t, axis=-1, keepdims=True), 0.0)
                rows.append(jnp.sum(dseg_t, axis=0, keepdims=True))
            dxdt.append(jnp.where(c["low"], pair_dx[0], pair_dx[1]))
        slab = jnp.zeros((8, 128), F32)
        for r in range(SSM_RPG):
            slab = slab + jnp.where(sub == r, rows[r], 0.0)
        row_slabs.append(slab)
        dc_all.append(dc + _mxu(dcb_t.T, bg))
        db_all.append(db + _mxu(dcb_t, cg))
    dxdt = jnp.concatenate(dxdt, axis=1)
    dx_st = jnp.concatenate(dx_st, axis=1)
    by_head = jnp.concatenate(row_slabs + [jnp.zeros((CHUNK - SSM_HEADS, 128), F32)], axis=0)
    dcs = by_head.T - dcs_neg
    dxs = dxs + dxdt * c["dt_x"] + dx_st * c["w_st_x"]
    ddt = _reduce_heads(dxdt * xs, e)
    dw_st = _reduce_heads(dx_st * xs, e)
    dcs = dcs + _reduce_heads(jnp.concatenate(d_decay_in_x, axis=1), e) * c["decay_in"]
    ddt = ddt + dw_st * c["decay_st"]
    d_log_st = dw_st * c["dt"] * c["decay_st"]
    dcs = dcs - d_log_st
    d_chunk_decay = jnp.sum(_reduce_heads(jnp.concatenate(d_chunk_decay_x, axis=1), e), axis=0, keepdims=True)
    dcs_last = jnp.sum(d_log_st, axis=0, keepdims=True) + d_chunk_decay * c["chunk_decay"]
    row = lax.broadcasted_iota(jnp.int32, (CHUNK, 128), 0)
    dcs = dcs + jnp.where(row == CHUNK - 1, dcs_last, 0.0)
    triu = jnp.where(c["upper"], 1.0, 0.0).astype(F32)
    dda = jnp.dot(triu, dcs, precision=HIGHEST, preferred_element_type=F32)
    ddt = ddt + dda * c["a"]
    da_log = jnp.sum(dda * c["dt"], axis=0, keepdims=True) * c["a"]
    dpre = ddt * jax.nn.sigmoid(c["pre"])
    dxc = jnp.concatenate([dxs] + db_all + dc_all, axis=1)
    return (dxc, dpre, dz, jnp.concatenate(dprev, axis=0), jnp.sum(dpre, axis=0, keepdims=True), da_log, dd_skip,
            dnorm_g)


def _conv_block(x, w, b):
    rows = lax.broadcasted_iota(jnp.int32, x.shape, 0)
    acc = x * w[SSM_CONV - 1:SSM_CONV, :] + b
    for k in range(SSM_CONV - 1):
        shift = SSM_CONV - 1 - k
        acc = acc + _shift_rows(x, rows, shift) * w[k:k + 1, :]
    return (_silu(acc),)


@functools.partial(jax.custom_vjp, nondiff_argnums=(2,))
def _shift_rows(x, rows, shift):
    return jnp.where(rows >= shift, pltpu.roll(x, shift, 0), 0.0)


def _shift_rows_fwd(x, rows, shift):
    return _shift_rows(x, rows, shift), rows


def _shift_rows_bwd(shift, rows, g):
    n = g.shape[0]
    return jnp.where(rows < n - shift, pltpu.roll(g, n - shift, 0), 0.0), None


_shift_rows.defvjp(_shift_rows_fwd, _shift_rows_bwd)


def _merge_block(gates, br_a, br_b):
    return (jax.nn.sigmoid(gates[:, :D_MODEL]) * br_a + jax.nn.sigmoid(gates[:, D_MODEL:]) * br_b,)


def _lnres_block(x, y, g, b):
    return (_layer_norm(ALPHA * x + y, g, b),)


def _lnres_block_twice(x, y, g, b):
    out = _layer_norm(ALPHA * x + y, g, b)
    return out, out


def _memln_block(x, g, b):
    return (_layer_norm(x, g, b),)


def _attn_block(q, kv):
    outs = []
    for h in range(X_HEADS):
        qh = q[:, h * X_HEADDIM:(h + 1) * X_HEADDIM].astype(BF16)
        kh = kv[:, h * X_HEADDIM:(h + 1) * X_HEADDIM].astype(BF16)
        vh = kv[:, D_MODEL + h * X_HEADDIM:D_MODEL + (h + 1) * X_HEADDIM].astype(BF16)
        s = lax.dot_general(qh, kh, (((1,), (1,)), ((), ())), preferred_element_type=F32) * (X_HEADDIM ** -0.5)
        s = s - lax.stop_gradient(jnp.max(s, axis=-1, keepdims=True))
        e = jnp.exp(s)
        p = e / jnp.sum(e, axis=-1, keepdims=True)
        outs.append(jnp.dot(p.astype(BF16), vh, preferred_element_type=F32))
    return (jnp.concatenate(outs, axis=1),)


def _swiglu_block(gu):
    return (_silu(gu[:, :FFN_HIDDEN]) * gu[:, FFN_HIDDEN:],)


class _Comm:
    def __init__(self):
        self.gathers = []
        self.scatters = []

    def operands(self):
        ins = [a for a, _ in self.gathers] + list(self.scatters)
        shapes = [jax.ShapeDtypeStruct((N_DEV,) + (a.shape if idx is None else a.shape[1:]), a.dtype)
                  for a, idx in self.gathers]
        shapes += [jax.ShapeDtypeStruct(a.shape, a.dtype) for a in self.scatters]
        scratch = []
        for n in (len(self.gathers), len(self.scatters)):
            if n:
                scratch += [pltpu.SemaphoreType.DMA((7 * n,)), pltpu.SemaphoreType.DMA((7 * n,)),
                            pltpu.SemaphoreType.DMA((n,))]
        return ins, shapes, scratch

    def _split(self, in_refs, out_refs, sems):
        ng = len(self.gathers)
        g_sems = sems[:3] if ng else None
        s_sems = sems[3:] if ng else sems
        return in_refs[:ng], in_refs[ng:], out_refs[:ng], out_refs[ng:], g_sems, s_sems

    def _gather_copies(self, i, src_ref, out_ref, sems):
        send_sems, recv_sems, local_sems = sems
        x, y, c = lax.axis_index("x"), lax.axis_index("y"), lax.axis_index("c")
        me, sibling = (x, y, c), (x, y, 1 - c)
        chips = [(1 - x, y), (x, 1 - y), (1 - x, 1 - y)]
        idx = self.gathers[i][1]
        src = src_ref if idx is None else src_ref.at[idx]

        def slot(px, py, pc):
            return out_ref.at[4 * px + 2 * py + pc]

        def copy(k, blk, to, from_src=False):
            return pltpu.make_async_remote_copy(
                src_ref=src if from_src else slot(*blk), dst_ref=slot(*blk), send_sem=send_sems.at[7 * i + k],
                recv_sem=recv_sems.at[7 * i + k], device_id=to, device_id_type=pl.DeviceIdType.MESH)

        mine = pltpu.make_async_copy(src, slot(*me), local_sems.at[i])
        first = [copy(0, me, sibling, True)] + [copy(1 + j, me, (*chip, c), True) for j, chip in enumerate(chips)]
        passed = [copy(4 + j, (*chip, c), sibling) for j, chip in enumerate(chips)]
        arrivals = [copy(1 + j, (*chip, c), me) for j, chip in enumerate(chips)]
        from_sibling = [copy(0, sibling, me)] + [copy(4 + j, (*chip, 1 - c), me) for j, chip in enumerate(chips)]
        return mine, first, passed, arrivals, from_sibling

    def _scatter_copies(self, i, src_ref, out_ref, sems):
        send_sems, recv_sems, local_sems = sems
        x, y, c = lax.axis_index("x"), lax.axis_index("y"), lax.axis_index("c")
        me = 4 * x + 2 * y + c
        mine = pltpu.make_async_copy(src_ref.at[me], out_ref.at[me], local_sems.at[i])
        copies = []
        for k in range(1, N_DEV):
            px = 1 - x if k & 4 else x
            py = 1 - y if k & 2 else y
            pc = 1 - c if k & 1 else c
            copies.append(pltpu.make_async_remote_copy(
                src_ref=src_ref.at[4 * px + 2 * py + pc], dst_ref=out_ref.at[me], send_sem=send_sems.at[7 * i + k - 1],
                recv_sem=recv_sems.at[7 * i + k - 1], device_id=(px, py, pc), device_id_type=pl.DeviceIdType.MESH))
        return mine, copies

    def start(self, in_refs, out_refs, sems):
        g_in, s_in, g_out, s_out, g_sems, s_sems = self._split(in_refs, out_refs, sems)
        for i in range(len(self.gathers)):
            mine, first, _, _, _ = self._gather_copies(i, g_in[i], g_out[i], g_sems)
            mine.start()
            for cp in first:
                cp.start()
        for i in range(len(self.scatters)):
            mine, copies = self._scatter_copies(i, s_in[i], s_out[i], s_sems)
            mine.start()
            for cp in copies:
                cp.start()

    def finish(self, in_refs, out_refs, sems):
        g_in, s_in, g_out, s_out, g_sems, s_sems = self._split(in_refs, out_refs, sems)
        parts = [self._gather_copies(i, g_in[i], g_out[i], g_sems) for i in range(len(self.gathers))]
        for j in range(3):
            for _, _, passed, arrivals, _ in parts:
                arrivals[j].wait_recv()
                passed[j].start()
        for mine, first, passed, _, from_sibling in parts:
            for cp in from_sibling:
                cp.wait_recv()
            for cp in first + passed:
                cp.wait_send()
            mine.wait()
        for i in range(len(self.scatters)):
            mine, copies = self._scatter_copies(i, s_in[i], s_out[i], s_sems)
            for cp in copies:
                cp.wait_recv()
            for cp in copies:
                cp.wait_send()
            mine.wait()


def _params(grid):
    return pltpu.CompilerParams(dimension_semantics=("arbitrary",) * len(grid), vmem_limit_bytes=VMEM_LIMIT)


def _call(name, body, *, grid, ins, in_specs, out_shape, out_specs, scratch=(), comm=None, aliases=None):
    n_in, n_out, n_scr = len(ins), len(out_shape), len(scratch)
    aliases = aliases or {}
    if comm is None:
        outs = pl.pallas_call(body, grid=grid, in_specs=list(in_specs), out_specs=list(out_specs),
                              out_shape=list(out_shape), scratch_shapes=list(scratch), name=name,
                              input_output_aliases=aliases, compiler_params=_params(grid))(*ins)
        return list(outs), []
    c_ins, c_shapes, c_scratch = comm.operands()
    nci, nco = len(c_ins), len(c_shapes)
    anywhere = pl.BlockSpec(memory_space=pl.ANY)

    def carrier(*refs):
        main_in, comm_in = refs[:n_in], refs[n_in:n_in + nci]
        o0 = n_in + nci
        main_out, comm_out = refs[o0:o0 + n_out], refs[o0 + n_out:o0 + n_out + nco]
        s0 = o0 + n_out + nco
        main_scr, comm_scr = refs[s0:s0 + n_scr], refs[s0 + n_scr:]
        first = pl.program_id(0) == 0
        last = pl.program_id(0) == grid[0] - 1
        for ax in range(1, len(grid)):
            first = first & (pl.program_id(ax) == 0)
            last = last & (pl.program_id(ax) == grid[ax] - 1)

        @pl.when(first)
        def _():
            comm.start(comm_in, comm_out, comm_scr)

        body(*main_in, *main_out, *main_scr)

        @pl.when(last)
        def _():
            comm.finish(comm_in, comm_out, comm_scr)

    outs = pl.pallas_call(carrier, grid=grid, in_specs=list(in_specs) + [anywhere] * nci,
                          out_specs=list(out_specs) + [anywhere] * nco, out_shape=list(out_shape) + c_shapes,
                          scratch_shapes=list(scratch) + c_scratch, name=name, input_output_aliases=aliases,
                          compiler_params=_params(grid))(*ins, *c_ins)
    return list(outs[:n_out]), list(outs[n_out:])


def _comm_only(name, comm):
    c_ins, c_shapes, c_scratch = comm.operands()
    nci, nco = len(c_ins), len(c_shapes)
    anywhere = pl.BlockSpec(memory_space=pl.ANY)

    def body(*refs):
        comm.start(refs[:nci], refs[nci:nci + nco], refs[nci + nco:])
        comm.finish(refs[:nci], refs[nci:nci + nco], refs[nci + nco:])

    return list(pl.pallas_call(body, in_specs=[anywhere] * nci, out_specs=[anywhere] * nco, out_shape=c_shapes,
                               scratch_shapes=c_scratch, name=name)(*c_ins))


def _stage_fwd(name, f, grid, ins, in_specs, out_shapes, out_specs):
    n_in = len(ins)

    def body(*refs):
        res = f(*[r[...].astype(F32) for r in refs[:n_in]])
        for o_ref, val in zip(refs[n_in:], res):
            o_ref[...] = val.astype(o_ref.dtype)

    return _call(name, body, grid=grid, ins=ins, in_specs=in_specs, out_shape=out_shapes, out_specs=out_specs)[0]


def _stage_bwd(name, f, grid, ins, in_specs, cts, ct_specs, grads, comm=None):
    n_in = len(ins)
    flat_cts = [c for group in cts for c in group]
    flat_ct_specs = [s for group, spec in zip(cts, ct_specs) for s in (spec,) * len(group)]
    n_ct = len(flat_cts)
    diff = [g[0] for g in grads]
    buffers = [(k, g[4]) for k, g in enumerate(grads) if len(g) > 4]
    n_buf = len(buffers)

    def body(*refs):
        vals = [r[...].astype(F32) for r in refs[:n_in]]
        ct_refs = refs[n_in:n_in + n_ct]
        g_refs = refs[n_in + n_ct + n_buf:]
        ct_vals, pos = [], 0
        for group in cts:
            acc = ct_refs[pos][...].astype(F32)
            for j in range(1, len(group)):
                acc = acc + ct_refs[pos + j][...].astype(F32)
            ct_vals.append(acc)
            pos += len(group)

        def g_fn(*dvals):
            full = list(vals)
            for i, dv in zip(diff, dvals):
                full[i] = dv
            return f(*full)

        _, vjp = jax.vjp(g_fn, *[vals[i] for i in diff])
        gvals = vjp(tuple(ct_vals))
        for gspec, g_ref, gval in zip(grads, g_refs, gvals):
            acc_axes = gspec[1]
            if not acc_axes:
                g_ref[...] = gval.astype(g_ref.dtype)
            else:
                first = pl.program_id(acc_axes[0]) == 0
                for ax in acc_axes[1:]:
                    first = first & (pl.program_id(ax) == 0)

                @pl.when(first)
                def _():
                    g_ref[...] = jnp.zeros_like(g_ref)

                g_ref[...] += gval.astype(g_ref.dtype)

    out_shapes, out_specs = [], []
    for gspec in grads:
        shape, spec = gspec[3] if len(gspec) > 3 else (ins[gspec[0]].shape, in_specs[gspec[0]])
        out_shapes.append(jax.ShapeDtypeStruct(shape, gspec[2]))
        out_specs.append(spec)
    anywhere = pl.BlockSpec(memory_space=pl.ANY)
    return _call(name, body, grid=grid, ins=list(ins) + flat_cts + [b for _, b in buffers],
                 in_specs=list(in_specs) + flat_ct_specs + [anywhere] * n_buf, out_shape=out_shapes, out_specs=out_specs,
                 comm=comm, aliases={n_in + n_ct + j: k for j, (k, _) in enumerate(buffers)})


def _pick_tile(n, candidates):
    for c in candidates:
        if n % c == 0:
            return c
    return n


def _matmul(name, a, b, *, ta=False, tb=False, add=None, out_dtype=F32, comm=None):
    if ta:
        k_dim, m = a.shape
    else:
        m, k_dim = a.shape
    n = b.shape[0] if tb else b.shape[1]
    assert (b.shape[1] if tb else b.shape[0]) == k_dim and not (ta and tb)
    tm = _pick_tile(m, (1024, 1408, 512, 256, 128))
    tn = _pick_tile(n, (1024, 1408, 512, 256, 128))
    if ta:
        tk = _pick_tile(k_dim, (1024, 512, 256, 128))
    elif k_dim <= 2816:
        tk = k_dim
    else:
        tk = _pick_tile(k_dim, (1408, 1024, 512, 256, 128))
    nk = k_dim // tk
    grid = (m // tm, n // tn, nk)
    a_spec = pl.BlockSpec((tk, tm), lambda i, j, k: (k, i)) if ta else pl.BlockSpec((tm, tk), lambda i, j, k: (i, k))
    b_spec = pl.BlockSpec((tn, tk), lambda i, j, k: (j, k)) if tb else pl.BlockSpec((tk, tn), lambda i, j, k: (k, j))
    o_spec = pl.BlockSpec((tm, tn), lambda i, j, k: (i, j))
    dims = (((0 if ta else 1,), (1 if tb else 0,)), ((), ()))
    has_add = add is not None

    def body(*refs):
        a_ref, b_ref = refs[0], refs[1]
        add_ref = refs[2] if has_add else None
        o_ref, acc_ref = refs[-2], refs[-1]
        k = pl.program_id(2)
        part = lax.dot_general(a_ref[...].astype(BF16), b_ref[...].astype(BF16), dims, preferred_element_type=F32)

        def finish(res):
            if has_add:
                res = res + add_ref[...].astype(F32)
            o_ref[...] = res.astype(o_ref.dtype)

        if nk == 1:
            finish(part)
        else:
            @pl.when(k == 0)
            def _():
                acc_ref[...] = part

            @pl.when((k > 0) & (k < nk - 1))
            def _():
                acc_ref[...] += part

            @pl.when(k == nk - 1)
            def _():
                finish(acc_ref[...] + part)

    ins = [a, b] + ([add] if has_add else [])
    in_specs = [a_spec, b_spec] + ([o_spec] if has_add else [])
    acc_shape = (tm, tn) if nk > 1 else (8, 128)
    outs, comm_outs = _call(name, body, grid=grid, ins=ins, in_specs=in_specs,
                            out_shape=[jax.ShapeDtypeStruct((m, n), out_dtype)], out_specs=[o_spec],
                            scratch=[pltpu.VMEM(acc_shape, F32)], comm=comm)
    return outs[0], comm_outs


SSD_STATE = (SSM_GROUPS * SSM_STATE, SSM_RPG * SSM_HEADDIM)


def _ssd_fwd(xc, dt_raw, proj, dt_bias, a_log, d_skip, norm_g, nb, nc, comm=None):
    t = xc.shape[0]
    row = lambda b, c: (b * nc + c, 0)
    par = lambda shape: pl.BlockSpec(shape, lambda b, c: (0, 0))

    def body(xc_ref, dt_ref, z_ref, dtb_ref, al_ref, ds_ref, ng_ref, e_ref, y_ref, ypre_ref, prev_ref, st_ref):
        @pl.when(pl.program_id(1) == 0)
        def _():
            st_ref[...] = jnp.zeros_like(st_ref)

        prev = st_ref[...]
        prev_ref[0, 0] = prev
        y, y_pre, new_state = _ssd_forward(xc_ref[...], dt_ref[...], z_ref[...].astype(F32), prev, dtb_ref[...],
                                           al_ref[...], ds_ref[...], ng_ref[...], e_ref[...])
        y_ref[...] = y.astype(y_ref.dtype)
        ypre_ref[...] = y_pre
        st_ref[...] = new_state

    return _call(
        "ssd_fwd", body, grid=(nb, nc), ins=[xc, dt_raw, proj, dt_bias, a_log, d_skip, norm_g, _head_expander()],
        in_specs=[pl.BlockSpec((CHUNK, SSM_CONV_DIM), row), pl.BlockSpec((CHUNK, 128), row),
                  pl.BlockSpec((CHUNK, SSM_INNER), lambda b, c: (b * nc + c, 1)),
                  par((1, 128)), par((1, 128)), par((1, SSM_INNER)), par((1, SSM_INNER)), par((128, SSM_INNER))],
        out_specs=[pl.BlockSpec((CHUNK, SSM_INNER), row), pl.BlockSpec((CHUNK, SSM_INNER), row),
                   pl.BlockSpec((1, 1) + SSD_STATE, lambda b, c: (b, c, 0, 0))],
        out_shape=[jax.ShapeDtypeStruct((t, SSM_INNER), BF16), jax.ShapeDtypeStruct((t, SSM_INNER), F32),
                   jax.ShapeDtypeStruct((nb, nc) + SSD_STATE, F32)],
        scratch=[pltpu.VMEM(SSD_STATE, F32)], comm=comm)


def _ssd_bwd(xc, dt_raw, proj, prevs, y_pre, dt_bias, a_log, d_skip, norm_g, dy, dproj, nb, nc, comm=None):
    t = xc.shape[0]
    row = lambda b, c: (b * nc + (nc - 1 - c), 0)
    par = lambda shape: pl.BlockSpec(shape, lambda b, c: (0, 0))
    z_spec = pl.BlockSpec((CHUNK, SSM_INNER), lambda b, c: (b * nc + (nc - 1 - c), 1))

    def body(xc_ref, dt_ref, z_ref, prev_ref, ypre_ref, dtb_ref, al_ref, ds_ref, ng_ref, e_ref, dy_ref, _,
             dxc_ref, ddt_ref, dz_ref, ddtb_ref, dal_ref, dds_ref, dng_ref, dst_ref):
        @pl.when(pl.program_id(1) == 0)
        def _():
            dst_ref[...] = jnp.zeros_like(dst_ref)

        @pl.when((pl.program_id(0) == 0) & (pl.program_id(1) == 0))
        def _():
            ddtb_ref[...] = jnp.zeros_like(ddtb_ref)
            dal_ref[...] = jnp.zeros_like(dal_ref)
            dds_ref[...] = jnp.zeros_like(dds_ref)
            dng_ref[...] = jnp.zeros_like(dng_ref)

        dxc, ddt, dz, dprev, ddtb, dal, dds, dng = _ssd_backward(
            xc_ref[...], dt_ref[...], z_ref[...].astype(F32), prev_ref[0, 0], ypre_ref[...], dtb_ref[...], al_ref[...],
            ds_ref[...], ng_ref[...], e_ref[...], dy_ref[...].astype(F32), dst_ref[...])
        dxc_ref[...] = dxc
        ddt_ref[...] = ddt.astype(ddt_ref.dtype)
        dz_ref[...] = dz.astype(dz_ref.dtype)
        dst_ref[...] = dprev
        ddtb_ref[...] += ddtb
        dal_ref[...] += dal
        dds_ref[...] += dds
        dng_ref[...] += dng

    return _call(
        "ssd_bwd", body, grid=(nb, nc),
        ins=[xc, dt_raw, proj, prevs, y_pre, dt_bias, a_log, d_skip, norm_g, _head_expander(), dy, dproj],
        in_specs=[pl.BlockSpec((CHUNK, SSM_CONV_DIM), row), pl.BlockSpec((CHUNK, DT_LANES), row), z_spec,
                  pl.BlockSpec((1, 1) + SSD_STATE, lambda b, c: (b, nc - 1 - c, 0, 0)),
                  pl.BlockSpec((CHUNK, SSM_INNER), row),
                  par((1, 128)), par((1, 128)), par((1, SSM_INNER)), par((1, SSM_INNER)), par((128, SSM_INNER)),
                  pl.BlockSpec((CHUNK, SSM_INNER), row), pl.BlockSpec(memory_space=pl.ANY)],
        out_specs=[pl.BlockSpec((CHUNK, SSM_CONV_DIM), row), pl.BlockSpec((CHUNK, DT_LANES), row), z_spec,
                   par((1, 128)), par((1, 128)), par((1, 128)), par((1, SSM_INNER))],
        out_shape=[jax.ShapeDtypeStruct((t, SSM_CONV_DIM), F32), jax.ShapeDtypeStruct((t, DT_LANES), BF16),
                   jax.ShapeDtypeStruct(dproj.shape, dproj.dtype), jax.ShapeDtypeStruct((1, 128), F32),
                   jax.ShapeDtypeStruct((1, 128), F32), jax.ShapeDtypeStruct((1, 128), F32),
                   jax.ShapeDtypeStruct((1, SSM_INNER), F32)],
        scratch=[pltpu.VMEM(SSD_STATE, F32)], comm=comm, aliases={11: 2})


def _loss_head(y, target):
    t, d = y.shape
    tm = _pick_tile(t, (256,))
    blk = pl.BlockSpec((tm, d), lambda i: (i, 0))

    def body(y_ref, t_ref, loss_ref, dy_ref):
        err = y_ref[...] - t_ref[...]
        dy_ref[...] = err * (1.0 / d)

        @pl.when(pl.program_id(0) == 0)
        def _():
            loss_ref[...] = jnp.zeros_like(loss_ref)

        loss_ref[...] += 0.5 * jnp.sum(jnp.mean(err * err, axis=-1, keepdims=True), axis=0, keepdims=True)

    return _call("loss_head", body, grid=(t // tm,), ins=[y, target], in_specs=[blk, blk],
                 out_specs=[pl.BlockSpec((1, 1), lambda i: (0, 0)), blk],
                 out_shape=[jax.ShapeDtypeStruct((1, 1), F32), jax.ShapeDtypeStruct((t, d), F32)])[0]


def _adamw_math(g, w, m, v):
    m_new = ADAM_B1 * m + (1.0 - ADAM_B1) * g
    v_new = ADAM_B2 * v + (1.0 - ADAM_B2) * jnp.square(g)
    m_hat = m_new / (1.0 - ADAM_B1 ** ADAM_STEP)
    v_hat = v_new / (1.0 - ADAM_B2 ** ADAM_STEP)
    delta = -ADAM_LR * (m_hat / (jnp.sqrt(v_hat) + ADAM_EPS) + ADAM_WD * w)
    return delta, m_new, v_new


def _adamw_sharded(name, parts, w, m, v):
    _, a, b = w.shape
    tr = _pick_tile(a, (128,))
    nt = a // tr
    part_specs = [pl.BlockSpec((N_DEV, tr, b),
                               (lambda l, i, _k=k: (0, jnp.where(l == _k, i, jnp.where(l > _k, nt - 1, 0)), 0)))
                  for k in range(DEPTH)]
    blk = pl.BlockSpec((1, tr, b), lambda l, i: (l, i, 0))

    def body(*refs):
        p_refs = refs[:DEPTH]
        w_ref, m_ref, v_ref, g_out, d_out, m_out, v_out = refs[DEPTH:]
        for k in range(DEPTH):
            @pl.when(pl.program_id(0) == k)
            def _(p_ref=p_refs[k]):
                g = p_ref[0].astype(F32)
                for p in range(1, N_DEV):
                    g = g + p_ref[p].astype(F32)
                delta, m_new, v_new = _adamw_math(g, w_ref[0], m_ref[0], v_ref[0])
                g_out[0] = g
                d_out[0] = delta
                m_out[0] = m_new
                v_out[0] = v_new

    return _call(name, body, grid=(DEPTH, nt), ins=list(parts) + [w, m, v], in_specs=part_specs + [blk, blk, blk],
                 out_specs=[blk] * 4, out_shape=[jax.ShapeDtypeStruct(w.shape, F32)] * 4)[0]


def _adamw_small(name, g, w, m, v):
    full = pl.BlockSpec(w.shape, lambda i: (0, 0))

    def body(g_ref, w_ref, m_ref, v_ref, d_out, m_out, v_out):
        delta, m_new, v_new = _adamw_math(g_ref[...], w_ref[...], m_ref[...], v_ref[...])
        d_out[...] = delta
        m_out[...] = m_new
        v_out[...] = v_new

    return _call(name, body, grid=(1,), ins=[g, w, m, v], in_specs=[full] * 4, out_specs=[full] * 3,
                 out_shape=[jax.ShapeDtypeStruct(w.shape, F32)] * 3)[0]


def _sum_parts(name, parts):
    n_parts, rows, cols = parts.shape
    tr = _pick_tile(rows, (512, 256, 128, 64, 32, 16, 8))

    def body(p_ref, o_ref):
        acc = p_ref[0]
        for p in range(1, n_parts):
            acc = acc + p_ref[p]
        o_ref[...] = acc

    return _call(name, body, grid=(rows // tr,), ins=[parts],
                 in_specs=[pl.BlockSpec((n_parts, tr, cols), lambda i: (0, i, 0))],
                 out_specs=[pl.BlockSpec((tr, cols), lambda i: (i, 0))],
                 out_shape=[jax.ShapeDtypeStruct((rows, cols), parts.dtype)])[0][0]


def _pack_w_in(w):
    main = jnp.concatenate([w[:, :XBC_COL], w[:, GA_COL:], w[:, XBC_COL:DT_COL]], axis=1)
    return main, jnp.pad(w[:, DT_COL:GA_COL], ((0, 0), (0, DT_LANES - SSM_HEADS)))


def _unpack_w_in(main, dt):
    return jnp.concatenate([main[:, :P_GATE], main[:, P_XBC:], dt[:, :SSM_HEADS], main[:, P_GATE:P_XBC]], axis=1)


def _pad_heads(v):
    return jnp.pad(v, (0, 128 - SSM_HEADS)).reshape(1, 128)


def _run_step(x, mem, target, small, ex):
    nb, s, d = x.shape
    t = nb * s
    nc = s // CHUNK
    rows = _pick_tile(t, (256,))
    rows_wide = _pick_tile(t, (512, 256))
    tq = _pick_tile(s, (512, 256))
    vec = lambda a: a.reshape(1, -1)
    full1 = lambda shape: pl.BlockSpec(shape, lambda i: (0,) * len(shape))
    row1 = lambda tm, w: pl.BlockSpec((tm, w), lambda i: (i, 0))
    sds = jax.ShapeDtypeStruct

    def mm(call, l, a, b, **kw):
        comm = ex.before(call, l)
        out, comm_outs = _matmul(call, a, b, comm=comm, **kw)
        if comm is not None:
            ex.after(call, l, comm_outs)
        return out

    def stage_bwd(call, l, *args):
        comm = ex.before(call, l)
        outs, comm_outs = _stage_bwd(call, *args, comm=comm)
        if comm is not None:
            ex.after(call, l, comm_outs)
        return outs

    mem_specs = [row1(256, d), full1((1, d)), full1((1, d))]
    mem_ins = [mem.reshape(nb * MEM_LEN, d), vec(small["mem_ln_g"]), vec(small["mem_ln_b"])]
    (mem_n,) = _stage_fwd("memln_fwd", _memln_block, (nb * MEM_LEN // 256,), mem_ins, mem_specs,
                          [sds((nb * MEM_LEN, d), BF16)], [row1(256, d)])

    h = x.reshape(t, d)
    h_bf = h.astype(BF16)
    ln_specs = [row1(rows, d), row1(rows, d), full1((1, d)), full1((1, d))]
    ln_outs = [sds((t, d), F32), sds((t, d), BF16)]
    ln_out_specs = [row1(rows, d), row1(rows, d)]
    saved = []
    for l in range(DEPTH):
        sv = {"h_bf": h_bf}
        w_p, w_dt = ex.weight("w_in", l)
        proj = mm("mm_in", l, h_bf, w_p, out_dtype=BF16)
        dt_raw = mm("mm_dt", l, h_bf, w_dt)
        sv["proj"] = proj
        sgu_ins = [proj, vec(small["sg_ln_g"][l]), vec(small["sg_ln_b"][l]), small["sg_w"][l], small["sg_b"][l].T]
        sgu_specs = [pl.BlockSpec((CHUNK, 2 * d), lambda i: (i, 0)), full1((1, d)), full1((1, d)),
                     full1((SG_GROUPS, CHUNK, CHUNK)), full1((CHUNK, SG_GROUPS))]
        (a_out,) = _stage_fwd("sgu_fwd", _sgu_block, (t // CHUNK,), sgu_ins, sgu_specs, [sds((t, d), BF16)],
                              [row1(CHUNK, d)])
        sv["sgu"] = (sgu_ins, sgu_specs)
        sv["a_out"] = a_out
        cw = 256
        conv_ins = [proj, small["conv_w"][l], vec(small["conv_b"][l])]
        conv_specs = [pl.BlockSpec((s, cw), lambda j, b: (b, P_XBC // cw + j)),
                      pl.BlockSpec((SSM_CONV, cw), lambda j, b: (0, j)), pl.BlockSpec((1, cw), lambda j, b: (0, j))]
        conv_out_spec = pl.BlockSpec((s, cw), lambda j, b: (b, j))
        (xc,) = _stage_fwd("conv_fwd", _conv_block, (SSM_CONV_DIM // cw, nb), conv_ins, conv_specs,
                           [sds((t, SSM_CONV_DIM), F32)], [conv_out_spec])
        sv["conv"] = (conv_ins, conv_specs, conv_out_spec)
        ssd_par = [_pad_heads(small["dt_bias"][l]), _pad_heads(small["a_log"][l]),
                   vec(jnp.repeat(small["d_skip"][l], SSM_HEADDIM)), vec(small["ssm_norm_g"][l])]
        comm = ex.before("ssd_fwd", l)
        (y_ssd, y_pre, prevs), comm_outs = _ssd_fwd(xc, dt_raw, proj, *ssd_par, nb, nc, comm=comm)
        if comm is not None:
            ex.after("ssd_fwd", l, comm_outs)
        sv["ssd"] = (xc, dt_raw, prevs, y_pre, ssd_par)
        sv["y_ssd"] = y_ssd
        br_a = mm("mm_sq", l, a_out, ex.weight("p_a", l), out_dtype=BF16)
        br_b = mm("mm_pb", l, y_ssd, ex.weight("p_b", l), out_dtype=BF16)
        merge_ins = [proj, br_a, br_b]
        merge_out_spec = row1(rows_wide, d)
        merge_specs = [pl.BlockSpec((rows_wide, 2 * d), lambda i: (i, P_GATE // (2 * d))), merge_out_spec, merge_out_spec]
        (merged,) = _stage_fwd("merge_fwd", _merge_block, (t // rows_wide,), merge_ins, merge_specs,
                               [sds((t, d), BF16)], [merge_out_spec])
        sv["merge"] = (merge_ins, merge_specs, merge_out_spec)
        sv["merged"] = merged
        y1 = mm("mm_sq", l, merged, ex.weight("w_mix_o", l), out_dtype=BF16)
        ln1_ins = [h, y1, vec(small["ln_g"][l, 0]), vec(small["ln_b"][l, 0])]
        h1, h1_bf = _stage_fwd("lnres_fwd", _lnres_block_twice, (t // rows,), ln1_ins, ln_specs, ln_outs, ln_out_specs)
        sv["ln1"] = ln1_ins
        q = mm("mm_sq", l, h1_bf, ex.weight("w_xq", l), out_dtype=BF16)
        kv = mm("mm_kv", l, mem_n, ex.weight("w_xkv", l), out_dtype=BF16)
        attn_ins = [q, kv]
        attn_out_spec = pl.BlockSpec((tq, d), lambda b, i: (b * (s // tq) + i, 0))
        attn_specs = [attn_out_spec, pl.BlockSpec((MEM_LEN, 2 * d), lambda b, i: (b, 0))]
        (o,) = _stage_fwd("attn_fwd", _attn_block, (nb, s // tq), attn_ins, attn_specs, [sds((t, d), BF16)],
                          [attn_out_spec])
        sv["attn"] = (attn_ins, attn_specs, attn_out_spec)
        sv["o"] = o
        sv["h1_bf"] = h1_bf
        y2 = mm("mm_sq", l, o, ex.weight("w_xo", l), out_dtype=BF16)
        ln2_ins = [h1, y2, vec(small["ln_g"][l, 1]), vec(small["ln_b"][l, 1])]
        h2, h2_bf = _stage_fwd("lnres_fwd", _lnres_block_twice, (t // rows,), ln2_ins, ln_specs, ln_outs, ln_out_specs)
        sv["ln2"] = ln2_ins
        sv["h2_bf"] = h2_bf
        gu = mm("mm_ffn_in", l, h2_bf, ex.weight("w_ffn_in", l), out_dtype=BF16)
        (act,) = _stage_fwd("swiglu_fwd", _swiglu_block, (t // 128,), [gu], [row1(128, 2 * FFN_HIDDEN)],
                            [sds((t, FFN_HIDDEN), BF16)], [row1(128, FFN_HIDDEN)])
        sv["gu"] = gu
        sv["act"] = act
        y3 = mm("mm_ffn_out", l, act, ex.weight("w_ffn_out", l), out_dtype=BF16)
        ln3_ins = [h2, y3, vec(small["ln_g"][l, 2]), vec(small["ln_b"][l, 2])]
        h, h_bf = _stage_fwd("lnres_fwd", _lnres_block_twice, (t // rows,), ln3_ins, ln_specs, ln_outs, ln_out_specs)
        sv["ln3"] = ln3_ins
        saved.append(sv)

    loss, dh = _loss_head(h, target.reshape(t, d))

    g_small = {n: [None] * DEPTH for n in SMALL_REP + SMALL_SH if n not in ("mem_ln_g", "mem_ln_b")}
    dmem_n = []
    ln_grads = [(0, (), F32), (1, (), BF16), (2, (0,), F32), (3, (0,), F32)]
    for l in reversed(range(DEPTH)):
        sv = saved[l]
        dln_g, dln_b = [None] * 3, [None] * 3
        dres, dy3, dln_g[2], dln_b[2] = stage_bwd("lnres_bwd", l, _lnres_block, (t // rows,), sv["ln3"], ln_specs,
                                                  [(dh,)], [row1(rows, d)], ln_grads)
        ex.grad("w_ffn_out", l, mm("mm_ffn_out_dw", l, sv["act"], dy3, ta=True, out_dtype=BF16))
        dact = mm("mm_ffn_out_dx", l, dy3, ex.weight("w_ffn_out", l), tb=True, out_dtype=BF16)
        (dgu,) = stage_bwd("swiglu_bwd", l, _swiglu_block, (t // 128,), [sv["gu"]], [row1(128, 2 * FFN_HIDDEN)],
                           [(dact,)], [row1(128, FFN_HIDDEN)], [(0, (), BF16)])
        ex.grad("w_ffn_in", l, mm("mm_ffn_in_dw", l, sv["h2_bf"], dgu, ta=True, out_dtype=BF16))
        dh2 = mm("mm_ffn_in_dx", l, dgu, ex.weight("w_ffn_in", l), tb=True, add=dres)
        dres, dy2, dln_g[1], dln_b[1] = stage_bwd("lnres_bwd", l, _lnres_block, (t // rows,), sv["ln2"], ln_specs,
                                                  [(dh2,)], [row1(rows, d)], ln_grads)
        ex.grad("w_xo", l, mm("mm_sq_dw", l, sv["o"], dy2, ta=True, out_dtype=BF16))
        do = mm("mm_sq_dx", l, dy2, ex.weight("w_xo", l), tb=True, out_dtype=BF16)
        attn_ins, attn_specs, attn_out_spec = sv["attn"]
        dq, dkv = stage_bwd("attn_bwd", l, _attn_block, (nb, s // tq), attn_ins, attn_specs, [(do,)], [attn_out_spec],
                            [(0, (), BF16), (1, (1,), F32)])
        ex.grad("w_xq", l, mm("mm_sq_dw", l, sv["h1_bf"], dq, ta=True, out_dtype=BF16))
        dh1 = mm("mm_sq_dx", l, dq, ex.weight("w_xq", l), tb=True, add=dres)
        ex.grad("w_xkv", l, mm("mm_kv_dw", l, mem_n, dkv, ta=True, out_dtype=BF16))
        dmem_n.append(mm("mm_kv_dx", l, dkv, ex.weight("w_xkv", l), tb=True))
        dres, dy1, dln_g[0], dln_b[0] = stage_bwd("lnres_bwd", l, _lnres_block, (t // rows,), sv["ln1"], ln_specs,
                                                  [(dh1,)], [row1(rows, d)], ln_grads)
        g_small["ln_g"][l] = jnp.concatenate(dln_g, axis=0)
        g_small["ln_b"][l] = jnp.concatenate(dln_b, axis=0)
        ex.grad("w_mix_o", l, mm("mm_sq_dw", l, sv["merged"], dy1, ta=True, out_dtype=BF16))
        dmerged = mm("mm_sq_dx", l, dy1, ex.weight("w_mix_o", l), tb=True, out_dtype=BF16)
        merge_ins, merge_specs, merge_out_spec = sv["merge"]
        dproj, dbr_a, dbr_b = stage_bwd("merge_bwd", l, _merge_block, (t // rows_wide,), merge_ins, merge_specs,
                                        [(dmerged,)], [merge_out_spec],
                                        [(0, (), BF16, ((t, P_COLS), merge_specs[0])), (1, (), BF16), (2, (), BF16)])
        ex.grad("p_a", l, mm("mm_sq_dw", l, sv["a_out"], dbr_a, ta=True, out_dtype=BF16))
        da_out = mm("mm_sq_dx", l, dbr_a, ex.weight("p_a", l), tb=True, out_dtype=BF16)
        ex.grad("p_b", l, mm("mm_pb_dw", l, sv["y_ssd"], dbr_b, ta=True, out_dtype=BF16))
        dy_ssd = mm("mm_pb_dx", l, dbr_b, ex.weight("p_b", l), tb=True, out_dtype=BF16)
        sgu_ins, sgu_specs = sv["sgu"]
        dproj, dsg_ln_g, dsg_ln_b, dsg_w, dsg_b = stage_bwd(
            "sgu_bwd", l, _sgu_block, (t // CHUNK,), sgu_ins, sgu_specs, [(da_out,)], [row1(CHUNK, d)],
            [(0, (), BF16, ((t, P_COLS), sgu_specs[0]), dproj), (1, (0,), F32), (2, (0,), F32), (3, (0,), F32),
             (4, (0,), F32)])
        g_small["sg_ln_g"][l], g_small["sg_ln_b"][l], g_small["sg_w"][l], g_small["sg_b"][l] = (
            dsg_ln_g[0], dsg_ln_b[0], dsg_w, dsg_b.T)
        xc, dt_raw, prevs, y_pre, ssd_par = sv["ssd"]
        comm = ex.before("ssd_bwd", l)
        (dxc, ddt, dproj, ddtb, dal, dds, dng), comm_outs = _ssd_bwd(xc, dt_raw, sv["proj"], prevs, y_pre, *ssd_par,
                                                                     dy_ssd, dproj, nb, nc, comm=comm)
        if comm is not None:
            ex.after("ssd_bwd", l, comm_outs)
        g_small["dt_bias"][l], g_small["a_log"][l], g_small["d_skip"][l] = (
            ddtb[0, :SSM_HEADS], dal[0, :SSM_HEADS], dds[0, :SSM_HEADS])
        g_small["ssm_norm_g"][l] = dng[0]
        conv_ins, conv_specs, conv_out_spec = sv["conv"]
        dproj, dconv_w, dconv_b = stage_bwd("conv_bwd", l, _conv_block, (SSM_CONV_DIM // 256, nb), conv_ins, conv_specs,
                                            [(dxc,)], [conv_out_spec],
                                            [(0, (), BF16, ((t, P_COLS), conv_specs[0]), dproj), (1, (1,), F32),
                                             (2, (1,), F32)])
        g_small["conv_w"][l], g_small["conv_b"][l] = dconv_w, dconv_b[0]
        w_p, w_dt = ex.weight("w_in", l)
        ex.grad("w_in", l, _unpack_w_in(mm("mm_in_dw", l, sv["h_bf"], dproj, ta=True, out_dtype=BF16),
                                        mm("mm_dt_dw", l, sv["h_bf"], ddt, ta=True, out_dtype=BF16)))
        if l == 0:
            dmg, dmb = stage_bwd("memln_bwd", l, _memln_block, (nb * MEM_LEN // 256,), mem_ins, mem_specs,
                                 [tuple(dmem_n)], [row1(256, d)], [(1, (0,), F32), (2, (0,), F32)])
            done = {n: jnp.stack(g, axis=0) for n, g in g_small.items()}
            done["mem_ln_g"], done["mem_ln_b"] = dmg[0], dmb[0]
            ex.small_grads(done)
        dh = mm("mm_in_dx", l, dproj, w_p, tb=True, add=mm("mm_dt_dx", l, ddt, w_dt, tb=True, add=dres))

    return loss, dh.reshape(nb, s, d)


def _pack_flat(arrays, rows):
    flat = jnp.concatenate([a.reshape(-1) for a in arrays])
    return jnp.pad(flat, (0, rows * 128 - flat.shape[0])).reshape(rows, 128)


def _unpack_flat(packed, shapes):
    lead = packed.shape[:-2]
    flat = packed.reshape(lead + (-1,))
    out, pos = [], 0
    for shape in shapes:
        n = math.prod(shape)
        out.append(flat[..., pos:pos + n].reshape(lead + tuple(shape)))
        pos += n
    return out


def _small_rows(n_elems):
    return -(-n_elems // (128 * SMALL_ROW_TILE)) * SMALL_ROW_TILE


def _from_shards(name, gathered):
    _, a, b = gathered.shape
    if name in BIG_COL_SHARDED:
        w = gathered.transpose(1, 0, 2).reshape(a, N_DEV * b)
        return _pack_w_in(w) if name == "w_in" else w
    return gathered.reshape(N_DEV * a, b)


def _to_shards(name, g):
    if name in BIG_COL_SHARDED:
        a, nb = g.shape
        return g.reshape(a, N_DEV, nb // N_DEV).transpose(1, 0, 2)
    a, b = g.shape
    return g.reshape(N_DEV, a // N_DEV, b)


class _MeshExchange:
    def __init__(self, shards_bf16, first):
        self.shards = shards_bf16
        self.full = dict(first)
        self.grads = {}
        self.received = {}
        self.small = None
        self.small_gathered = None

    def weight(self, name, l):
        return self.full[(name, l)]

    def grad(self, name, l, g):
        self.grads[(name, l)] = g

    def small_grads(self, done):
        self.small = done

    def before(self, call, l):
        comm = _Comm()
        for name, layer in GATHER_PLAN.get((call, l), ()):
            comm.gathers.append((self.shards[name], layer))
        for name, layer in SCATTER_PLAN.get((call, l), ()):
            comm.scatters.append(_to_shards(name, self.grads[(name, layer)]))
        if (call, l) == ("mm_in_dx", 0):
            names = SMALL_REP + SMALL_SH
            rows = _small_rows(sum(math.prod(self.small[n].shape) for n in names))
            comm.gathers.append((_pack_flat([self.small[n] for n in names], rows), None))
        return comm if comm.gathers or comm.scatters else None

    def after(self, call, l, outs):
        gathers = list(GATHER_PLAN.get((call, l), ()))
        for (name, layer), out in zip(gathers, outs):
            self.full[(name, layer)] = _from_shards(name, out)
        outs = outs[len(gathers):]
        if (call, l) == ("mm_in_dx", 0):
            self.small_gathered = outs[0]
            outs = outs[1:]
        for (name, layer), out in zip(SCATTER_PLAN.get((call, l), ()), outs):
            self.received[(name, layer)] = out


def kernel(x, mem, mem_ln_g, mem_ln_b, w_in, sg_ln_g, sg_ln_b, sg_w, sg_b, conv_w, conv_b, dt_bias, a_log, d_skip, ssm_norm_g, p_a, p_b, w_mix_o, w_xq, w_xkv, w_xo, w_ffn_in, w_ffn_out, ln_g, ln_b, loss_target, m_mem_ln_g, m_mem_ln_b, m_w_in, m_sg_ln_g, m_sg_ln_b, m_sg_w, m_sg_b, m_conv_w, m_conv_b, m_dt_bias, m_a_log, m_d_skip, m_ssm_norm_g, m_p_a, m_p_b, m_w_mix_o, m_w_xq, m_w_xkv, m_w_xo, m_w_ffn_in, m_w_ffn_out, m_ln_g, m_ln_b, v_mem_ln_g, v_mem_ln_b, v_w_in, v_sg_ln_g, v_sg_ln_b, v_sg_w, v_sg_b, v_conv_w, v_conv_b, v_dt_bias, v_a_log, v_d_skip, v_ssm_norm_g, v_p_a, v_p_b, v_w_mix_o, v_w_xq, v_w_xkv, v_w_xo, v_w_ffn_in, v_w_ffn_out, v_ln_g, v_ln_b):
    args = dict(locals())
    w = {n: args[n] for n in WEIGHTS}
    m = {n: args["m_" + n] for n in WEIGHTS}
    v = {n: args["v_" + n] for n in WEIGHTS}
    me = 4 * lax.axis_index("x") + 2 * lax.axis_index("y") + lax.axis_index("c")

    shards = {n: w[n].astype(BF16) for n in BIG}
    sh_shapes = [w[n].shape for n in SMALL_SH]
    first = _Comm()
    first.gathers.append((shards["w_in"], 0))
    first.gathers.append((_pack_flat([w[n] for n in SMALL_SH], _small_rows(sum(math.prod(s) for s in sh_shapes))), None))
    w_in0, small_sh = _comm_only("gather_first", first)
    small = {n: w[n] for n in SMALL_REP}
    for n, sh in zip(SMALL_SH, _unpack_flat(small_sh, sh_shapes)):
        small[n] = sh.transpose(1, 2, 0, 3).reshape(sh.shape[1], sh.shape[2], N_DEV * sh.shape[3])

    ex = _MeshExchange(shards, {("w_in", 0): _from_shards("w_in", w_in0)})
    loss, grad_x = _run_step(x, mem, loss_target, small, ex)
    loss = lax.psum(loss[0, 0], ("x", "y", "c"))

    out = {}
    for n in BIG:
        out[n] = _adamw_sharded("adamw_" + n, [ex.received[(n, l)] for l in range(DEPTH)], w[n], m[n], v[n])
    names = SMALL_REP + SMALL_SH
    g_small = dict(zip(names, _unpack_flat(_sum_parts("sum_small_grads", ex.small_gathered),
                                           [ex.small[n].shape for n in names])))
    for n in names:
        g = g_small[n]
        if n in SMALL_SH:
            width = w[n].shape[-1]
            g = lax.dynamic_slice_in_dim(g, me * width, width, axis=-1)
        two_d = (-1, w[n].shape[-1])
        res = _adamw_small("adamw_" + n, g.reshape(two_d), w[n].reshape(two_d), m[n].reshape(two_d), v[n].reshape(two_d))
        out[n] = [g] + [r.reshape(w[n].shape) for r in res]

    results = []
    for k in range(4):
        results.extend(out[n][k] for n in WEIGHTS)
    return (loss, grad_x, *results)
```

```python
import functools
import math

import jax
import jax.numpy as jnp
from jax import lax
from jax.experimental import pallas as pl
from jax.experimental.pallas import tpu as pltpu

F32 = jnp.float32
BF16 = jnp.bfloat16
HIGHEST = lax.Precision.HIGHEST

N_DEV = 8
D_MODEL = 1024
DEPTH = 2
MEM_LEN = 256
CHUNK = 128
SG_GROUPS = 8
SSM_INNER = 2048
SSM_HEADDIM = 64
SSM_HEADS = 32
SSM_STATE = 128
SSM_GROUPS = 4
SSM_RPG = 8
SSM_CONV = 4
SSM_CONV_DIM = 3072
X_HEADS = 4
X_HEADDIM = 256
FFN_HIDDEN = 2816
ALPHA = float((2 * DEPTH) ** 0.25)
LN_EPS = 1e-5
RMS_EPS = 1e-5
XBC_COL = 4096
DT_COL = 7168
GA_COL = 7200
IN_COLS = 9248
P_GATE = 4096
P_XBC = 6144
P_COLS = 9216
DT_LANES = 128

ADAM_LR = 0.001
ADAM_B1 = 0.9
ADAM_B2 = 0.999
ADAM_EPS = 1e-08
ADAM_WD = 0.01
ADAM_STEP = 10

VMEM_LIMIT = 48 * 1024 * 1024
SMALL_ROW_TILE = 256

BIG = ("w_in", "p_a", "p_b", "w_mix_o", "w_xq", "w_xkv", "w_xo", "w_ffn_in", "w_ffn_out")
BIG_COL_SHARDED = ("w_in", "w_xkv", "w_ffn_in")
SMALL_REP = ("mem_ln_g", "mem_ln_b", "sg_ln_g", "sg_ln_b", "sg_w", "sg_b", "conv_b", "dt_bias", "a_log", "d_skip",
             "ssm_norm_g")
SMALL_SH = ("conv_w", "ln_g", "ln_b")
WEIGHTS = ("mem_ln_g", "mem_ln_b", "w_in", "sg_ln_g", "sg_ln_b", "sg_w", "sg_b", "conv_w", "conv_b", "dt_bias", "a_log",
           "d_skip", "ssm_norm_g", "p_a", "p_b", "w_mix_o", "w_xq", "w_xkv", "w_xo", "w_ffn_in", "w_ffn_out", "ln_g", "ln_b")

W_IN_PIECES = 4
GATHER_PLAN = {("sgu_fwd", 0): [("w_in", 1, 0)], ("conv_fwd", 0): [("w_in", 1, 1)],
               ("mm_ffn_in", 0): [("w_in", 1, 2)], ("swiglu_fwd", 0): [("w_in", 1, 3)]}
SCATTER_PLAN = {("mm_ffn_in_dw", 0): [("w_in", 1, 3)]}
for _l in range(DEPTH):
    GATHER_PLAN[("mm_in", _l)] = [(n, _l, None) for n in ("p_a", "p_b", "w_mix_o", "w_xq", "w_xkv", "w_xo")]
    GATHER_PLAN[("ssd_fwd", _l)] = [("w_ffn_in", _l, None), ("w_ffn_out", _l, None)]
    SCATTER_PLAN[("swiglu_bwd", _l)] = [("w_ffn_out", _l, None)]
    SCATTER_PLAN[("sgu_bwd", _l)] = [("w_mix_o", _l, None), ("p_a", _l, None), ("w_xo", _l, None)]
    SCATTER_PLAN[("ssd_bwd", _l)] = [("w_ffn_in", _l, None), ("w_xkv", _l, None), ("w_xq", _l, None)]
    SCATTER_PLAN[("conv_bwd", _l)] = [("p_b", _l, None)]
    SCATTER_PLAN[("mm_in_dw_b", _l)] = [("w_in", _l, 0)]
    SCATTER_PLAN[("mm_in_dx", _l)] = [("w_in", _l, 1), ("w_in", _l, 2)] + ([("w_in", _l, 3)] if _l == 0 else [])
SMALL_GATHER_CALL = ("mm_in_dw_a", 0)


def _layer_norm(x, g, b):
    mu = jnp.mean(x, axis=-1, keepdims=True)
    xc = x - mu
    var = jnp.mean(xc * xc, axis=-1, keepdims=True)
    return xc * lax.rsqrt(var + LN_EPS) * g + b


def _gelu(x):
    return 0.5 * x * (1.0 + lax.erf(x * (1.0 / math.sqrt(2.0))))


def _silu(x):
    return x * jax.nn.sigmoid(x)


def _softplus(x):
    return jnp.maximum(x, 0.0) + jnp.log1p(jnp.exp(-jnp.abs(x)))


def _causal_mask():
    r = lax.broadcasted_iota(jnp.int32, (CHUNK, CHUNK), 0)
    c = lax.broadcasted_iota(jnp.int32, (CHUNK, CHUNK), 1)
    return r >= c


def _sgu_block(uv, ln_g, ln_b, w, sb):
    gu = _gelu(uv[:, :D_MODEL])
    vn = _layer_norm(_gelu(uv[:, D_MODEL:]), ln_g, ln_b)
    causal = _causal_mask()
    width = D_MODEL // SG_GROUPS
    outs = []
    for g in range(SG_GROUPS):
        wg = jnp.where(causal, w[g], 0.0).astype(BF16)
        mixed = jnp.dot(wg, vn[:, g * width:(g + 1) * width].astype(BF16), preferred_element_type=F32)
        outs.append(mixed + sb[:, g:g + 1])
    return (gu * jnp.concatenate(outs, axis=1),)


GROUP_W = SSM_RPG * SSM_HEADDIM
NT_DIMS = (((1,), (1,)), ((), ()))
TN_DIMS = (((0,), (0,)), ((), ()))


def _mxu(a, b, dims=(((1,), (0,)), ((), ()))):
    return lax.dot_general(a.astype(BF16), b.astype(BF16), dims, preferred_element_type=F32)


def _head_expander():
    return (jnp.arange(SSM_INNER)[None, :] // SSM_HEADDIM == jnp.arange(128)[:, None]).astype(BF16)


def _bf16_terms(x, n):
    terms = []
    for _ in range(n):
        t = x.astype(BF16)
        terms.append(t)
        x = x - t.astype(F32)
    return terms


def _expand_heads(q, e):
    return sum(jnp.dot(t, e, preferred_element_type=F32) for t in _bf16_terms(q, 3))


def _reduce_heads(v, e):
    return sum(lax.dot_general(t, e, NT_DIMS, preferred_element_type=F32) for t in _bf16_terms(v, 2))


def _ssd_common(xc, dtraw, dt_bias, a_log, e):
    xs = xc[:, :SSM_INNER]
    pre = dtraw + dt_bias
    dt = _softplus(pre)
    a = -jnp.exp(a_log)
    r_i = lax.broadcasted_iota(jnp.int32, (CHUNK, CHUNK), 0)
    c_i = lax.broadcasted_iota(jnp.int32, (CHUNK, CHUNK), 1)
    tril = jnp.where(r_i >= c_i, 1.0, 0.0).astype(F32)
    cs = jnp.dot(tril, dt * a, precision=HIGHEST, preferred_element_type=F32)
    cs_last = cs[CHUNK - 1:CHUNK, :]
    decay_in = jnp.exp(cs)
    decay_st = jnp.exp(cs_last - cs)
    dt_x = _expand_heads(dt, e)
    w_st_x = _expand_heads(dt * decay_st, e)
    decay_in_x = _expand_heads(decay_in, e)
    return dict(xs=xs, pre=pre, dt=dt, a=a, lower=r_i >= c_i, upper=c_i >= r_i, cs=cs, cs_t=cs.T, decay_in=decay_in,
                decay_st=decay_st, chunk_decay=jnp.exp(cs_last), dt_x=dt_x, w_st_x=w_st_x, decay_in_x=decay_in_x,
                chunk_decay_x=decay_in_x[CHUNK - 1:CHUNK, :], xdt=xs * dt_x, x_st=(xs * w_st_x).astype(BF16),
                low=lax.broadcasted_iota(jnp.int32, (CHUNK, 128), 1) < SSM_HEADDIM)


def _pair_decay(c, h):
    return jnp.exp(jnp.where(c["lower"], c["cs"][:, h:h + 1] - c["cs_t"][h:h + 1, :], -1e30))


def _pair_decay_t(c, h):
    return jnp.exp(jnp.where(c["upper"], c["cs_t"][h:h + 1, :] - c["cs"][:, h:h + 1], -1e30))


def _ssd_forward(xc, dtraw, z, prev, dt_bias, a_log, d_skip_x, norm_g, e):
    c = _ssd_common(xc, dtraw, dt_bias, a_log, e)
    y_groups, new_states = [], []
    for g in range(SSM_GROUPS):
        lanes = slice(g * GROUP_W, (g + 1) * GROUP_W)
        bg = xc[:, SSM_INNER + g * SSM_STATE:SSM_INNER + (g + 1) * SSM_STATE]
        cg = xc[:, SSM_INNER + (SSM_GROUPS + g) * SSM_STATE:SSM_INNER + (SSM_GROUPS + g + 1) * SSM_STATE].astype(BF16)
        pg = prev[g * SSM_STATE:(g + 1) * SSM_STATE, :]
        cb = _mxu(cg, bg, NT_DIMS)
        y_in = _mxu(cg, pg) * c["decay_in_x"][:, lanes]
        new_states.append(pg * c["chunk_decay_x"][:, lanes] + _mxu(bg.T, c["x_st"][:, lanes]))
        pairs = []
        for j in range(SSM_RPG // 2):
            h0 = g * SSM_RPG + 2 * j
            xp = c["xdt"][:, 128 * (h0 // 2):128 * (h0 // 2 + 1)]
            pairs.append(_mxu(cb * _pair_decay(c, h0), jnp.where(c["low"], xp, 0.0))
                         + _mxu(cb * _pair_decay(c, h0 + 1), jnp.where(c["low"], 0.0, xp)))
        y_groups.append(jnp.concatenate(pairs, axis=1) + y_in)
    y_pre = jnp.concatenate(y_groups, axis=1) + c["xs"] * d_skip_x
    gated = y_pre * _silu(z)
    normed = [gated[:, g * GROUP_W:(g + 1) * GROUP_W] for g in range(SSM_GROUPS)]
    normed = [yg * lax.rsqrt(jnp.mean(yg * yg, axis=-1, keepdims=True) + RMS_EPS) for yg in normed]
    return jnp.concatenate(normed, axis=1) * norm_g, y_pre, jnp.concatenate(new_states, axis=0)


def _ssd_backward(xc, dtraw, z, prev, y_pre, dt_bias, a_log, d_skip_x, norm_g, e, dout, dnew):
    c = _ssd_common(xc, dtraw, dt_bias, a_log, e)
    xs = c["xs"]
    sig = jax.nn.sigmoid(z)
    silu_z = z * sig
    gated = y_pre * silu_z
    d_gated, normed = [], []
    for g in range(SSM_GROUPS):
        lanes = slice(g * GROUP_W, (g + 1) * GROUP_W)
        yg = gated[:, lanes]
        r = lax.rsqrt(jnp.mean(yg * yg, axis=-1, keepdims=True) + RMS_EPS)
        n = yg * r
        gh = dout[:, lanes] * norm_g[:, lanes]
        d_gated.append(r * (gh - n * jnp.mean(gh * n, axis=-1, keepdims=True)))
        normed.append(n)
    d_gated = jnp.concatenate(d_gated, axis=1)
    dnorm_g = jnp.sum(dout * jnp.concatenate(normed, axis=1), axis=0, keepdims=True)
    dy = d_gated * silu_z
    dz = d_gated * y_pre * (sig * (1.0 + z * (1.0 - sig)))
    dxs = dy * d_skip_x
    dd_skip = jnp.sum(_reduce_heads(dy * xs, e), axis=0, keepdims=True)

    lane = lax.broadcasted_iota(jnp.int32, (CHUNK, 128), 1)
    sub = lax.broadcasted_iota(jnp.int32, (8, 128), 0)
    dcs_neg = jnp.zeros((CHUNK, 128), F32)
    row_slabs = []
    dxdt, dx_st, d_decay_in_x, dprev, d_chunk_decay_x, db_all, dc_all = [], [], [], [], [], [], []
    for g in range(SSM_GROUPS):
        lanes = slice(g * GROUP_W, (g + 1) * GROUP_W)
        bg = xc[:, SSM_INNER + g * SSM_STATE:SSM_INNER + (g + 1) * SSM_STATE].astype(BF16)
        cg_f = xc[:, SSM_INNER + (SSM_GROUPS + g) * SSM_STATE:SSM_INNER + (SSM_GROUPS + g + 1) * SSM_STATE]
        cg = cg_f.astype(BF16)
        pg = prev[g * SSM_STATE:(g + 1) * SSM_STATE, :]
        dng = dnew[g * SSM_STATE:(g + 1) * SSM_STATE, :]
        dy_g = dy[:, lanes]
        cb_t = _mxu(bg, cg, NT_DIMS)
        t1 = (dy_g * c["decay_in_x"][:, lanes]).astype(BF16)
        d_decay_in_x.append(dy_g * _mxu(cg, pg))
        dc = _mxu(t1, pg, NT_DIMS)
        dprev.append(_mxu(cg_f.T, t1) + dng * c["chunk_decay_x"][:, lanes])
        d_chunk_decay_x.append(dng * pg)
        db = _mxu(c["x_st"][:, lanes], dng, NT_DIMS)
        dx_st.append(_mxu(bg, dng))
        dcb_t = jnp.zeros((CHUNK, CHUNK), F32)
        rows = []
        for j in range(SSM_RPG // 2):
            h0 = g * SSM_RPG + 2 * j
            blk = slice(128 * (h0 // 2), 128 * (h0 // 2 + 1))
            xp = c["xdt"][:, blk]
            dyp = dy[:, blk].astype(BF16)
            pair_dx = []
            for k, xk in enumerate((jnp.where(c["low"], xp, 0.0), jnp.where(c["low"], 0.0, xp))):
                dec_t = _pair_decay_t(c, h0 + k)
                pair_dx.append(_mxu(cb_t * dec_t, dyp))
                dml_t = _mxu(xk, dyp, NT_DIMS) * dec_t
                dcb_t = dcb_t + dml_t
                dseg_t = dml_t * cb_t
                dcs_neg = dcs_neg + jnp.where(lane == h0 + k, jnp.sum(dseg_t, axis=-1, keepdims=True), 0.0)
                rows.append(jnp.sum(dseg_t, axis=0, keepdims=True))
            dxdt.append(jnp.where(c["low"], pair_dx[0], pair_dx[1]))
        slab = jnp.zeros((8, 128), F32)
        for r in range(SSM_RPG):
            slab = slab + jnp.where(sub == r, rows[r], 0.0)
        row_slabs.append(slab)
        dc_all.append(dc + _mxu(dcb_t.T, bg))
        db_all.append(db + _mxu(dcb_t, cg))
    dxdt = jnp.concatenate(dxdt, axis=1)
    dx_st = jnp.concatenate(dx_st, axis=1)
    by_head = jnp.concatenate(row_slabs + [jnp.zeros((CHUNK - SSM_HEADS, 128), F32)], axis=0)
    dcs = by_head.T - dcs_neg
    dxs = dxs + dxdt * c["dt_x"] + dx_st * c["w_st_x"]
    ddt = _reduce_heads(dxdt * xs, e)
    dw_st = _reduce_heads(dx_st * xs, e)
    dcs = dcs + _reduce_heads(jnp.concatenate(d_decay_in_x, axis=1), e) * c["decay_in"]
    ddt = ddt + dw_st * c["decay_st"]
    d_log_st = dw_st * c["dt"] * c["decay_st"]
    dcs = dcs - d_log_st
    d_chunk_decay = jnp.sum(_reduce_heads(jnp.concatenate(d_chunk_decay_x, axis=1), e), axis=0, keepdims=True)
    dcs_last = jnp.sum(d_log_st, axis=0, keepdims=True) + d_chunk_decay * c["chunk_decay"]
    row = lax.broadcasted_iota(jnp.int32, (CHUNK, 128), 0)
    dcs = dcs + jnp.where(row == CHUNK - 1, dcs_last, 0.0)
    triu = jnp.where(c["upper"], 1.0, 0.0).astype(F32)
    dda = jnp.dot(triu, dcs, precision=HIGHEST, preferred_element_type=F32)
    ddt = ddt + dda * c["a"]
    da_log = jnp.sum(dda * c["dt"], axis=0, keepdims=True) * c["a"]
    dpre = ddt * jax.nn.sigmoid(c["pre"])
    dxc = jnp.concatenate([dxs] + db_all + dc_all, axis=1)
    return (dxc, dpre, dz, jnp.concatenate(dprev, axis=0), jnp.sum(dpre, axis=0, keepdims=True), da_log, dd_skip,
            dnorm_g)


def _conv_block(x, w, b):
    rows = lax.broadcasted_iota(jnp.int32, x.shape, 0)
    acc = x * w[SSM_CONV - 1:SSM_CONV, :] + b
    for k in range(SSM_CONV - 1):
        shift = SSM_CONV - 1 - k
        acc = acc + _shift_rows(x, rows, shift) * w[k:k + 1, :]
    return (_silu(acc),)


@functools.partial(jax.custom_vjp, nondiff_argnums=(2,))
def _shift_rows(x, rows, shift):
    return jnp.where(rows >= shift, pltpu.roll(x, shift, 0), 0.0)


def _shift_rows_fwd(x, rows, shift):
    return _shift_rows(x, rows, shift), rows


def _shift_rows_bwd(shift, rows, g):
    n = g.shape[0]
    return jnp.where(rows < n - shift, pltpu.roll(g, n - shift, 0), 0.0), None


_shift_rows.defvjp(_shift_rows_fwd, _shift_rows_bwd)


def _merge_block(gates, br_a, br_b):
    return (jax.nn.sigmoid(gates[:, :D_MODEL]) * br_a + jax.nn.sigmoid(gates[:, D_MODEL:]) * br_b,)


def _lnres_block(x, y, g, b):
    return (_layer_norm(ALPHA * x + y, g, b),)


def _lnres_block_twice(x, y, g, b):
    out = _layer_norm(ALPHA * x + y, g, b)
    return out, out


def _memln_block(x, g, b):
    return (_layer_norm(x, g, b),)


def _attn_block(q, kv):
    outs = []
    for h in range(X_HEADS):
        qh = q[:, h * X_HEADDIM:(h + 1) * X_HEADDIM].astype(BF16)
        kh = kv[:, h * X_HEADDIM:(h + 1) * X_HEADDIM].astype(BF16)
        vh = kv[:, D_MODEL + h * X_HEADDIM:D_MODEL + (h + 1) * X_HEADDIM].astype(BF16)
        s = lax.dot_general(qh, kh, (((1,), (1,)), ((), ())), preferred_element_type=F32) * (X_HEADDIM ** -0.5)
        s = s - lax.stop_gradient(jnp.max(s, axis=-1, keepdims=True))
        e = jnp.exp(s)
        p = e / jnp.sum(e, axis=-1, keepdims=True)
        outs.append(jnp.dot(p.astype(BF16), vh, preferred_element_type=F32))
    return (jnp.concatenate(outs, axis=1),)


def _swiglu_block(gu):
    return (_silu(gu[:, :FFN_HIDDEN]) * gu[:, FFN_HIDDEN:],)


class _Comm:
    def __init__(self):
        self.gathers = []
        self.scatters = []

    @staticmethod
    def _rows(ref, rows):
        return ref if rows is None else ref.at[pl.ds(rows[0], rows[1])]

    def operands(self):
        ins = [a for a, _, _ in self.gathers] + [a for a, _ in self.scatters]
        shapes = []
        for a, idx, rows in self.gathers:
            blk = a.shape if idx is None else a.shape[1:]
            shapes.append(jax.ShapeDtypeStruct((N_DEV, blk[0] if rows is None else rows[1]) + tuple(blk[1:]), a.dtype))
        for a, rows in self.scatters:
            shapes.append(jax.ShapeDtypeStruct((N_DEV, a.shape[1] if rows is None else rows[1]) + tuple(a.shape[2:]),
                                               a.dtype))
        scratch = []
        for n in (len(self.gathers), len(self.scatters)):
            if n:
                scratch += [pltpu.SemaphoreType.DMA((7 * n,)), pltpu.SemaphoreType.DMA((7 * n,)),
                            pltpu.SemaphoreType.DMA((n,))]
        return ins, shapes, scratch

    def _split(self, in_refs, out_refs, sems):
        ng = len(self.gathers)
        g_sems = sems[:3] if ng else None
        s_sems = sems[3:] if ng else sems
        return in_refs[:ng], in_refs[ng:], out_refs[:ng], out_refs[ng:], g_sems, s_sems

    def _gather_copies(self, i, src_ref, out_ref, sems):
        send_sems, recv_sems, local_sems = sems
        x, y, c = lax.axis_index("x"), lax.axis_index("y"), lax.axis_index("c")
        me, sibling = (x, y, c), (x, y, 1 - c)
        chips = [(1 - x, y), (x, 1 - y), (1 - x, 1 - y)]
        _, idx, rows = self.gathers[i]
        src = self._rows(src_ref if idx is None else src_ref.at[idx], rows)

        def slot(px, py, pc):
            return out_ref.at[4 * px + 2 * py + pc]

        def copy(k, blk, to, from_src=False):
            return pltpu.make_async_remote_copy(
                src_ref=src if from_src else slot(*blk), dst_ref=slot(*blk), send_sem=send_sems.at[7 * i + k],
                recv_sem=recv_sems.at[7 * i + k], device_id=to, device_id_type=pl.DeviceIdType.MESH)

        mine = pltpu.make_async_copy(src, slot(*me), local_sems.at[i])
        first = [copy(0, me, sibling, True)] + [copy(1 + j, me, (*chip, c), True) for j, chip in enumerate(chips)]
        passed = [copy(4 + j, (*chip, c), sibling) for j, chip in enumerate(chips)]
        arrivals = [copy(1 + j, (*chip, c), me) for j, chip in enumerate(chips)]
        from_sibling = [copy(0, sibling, me)] + [copy(4 + j, (*chip, 1 - c), me) for j, chip in enumerate(chips)]
        return mine, first, passed, arrivals, from_sibling

    def _scatter_copies(self, i, src_ref, out_ref, sems):
        send_sems, recv_sems, local_sems = sems
        x, y, c = lax.axis_index("x"), lax.axis_index("y"), lax.axis_index("c")
        me = 4 * x + 2 * y + c
        rows = self.scatters[i][1]
        mine = pltpu.make_async_copy(self._rows(src_ref.at[me], rows), out_ref.at[me], local_sems.at[i])
        copies = []
        for k in range(1, N_DEV):
            px = 1 - x if k & 4 else x
            py = 1 - y if k & 2 else y
            pc = 1 - c if k & 1 else c
            copies.append(pltpu.make_async_remote_copy(
                src_ref=self._rows(src_ref.at[4 * px + 2 * py + pc], rows), dst_ref=out_ref.at[me],
                send_sem=send_sems.at[7 * i + k - 1], recv_sem=recv_sems.at[7 * i + k - 1], device_id=(px, py, pc),
                device_id_type=pl.DeviceIdType.MESH))
        return mine, copies

    def start(self, in_refs, out_refs, sems):
        g_in, s_in, g_out, s_out, g_sems, s_sems = self._split(in_refs, out_refs, sems)
        for i in range(len(self.gathers)):
            mine, first, _, _, _ = self._gather_copies(i, g_in[i], g_out[i], g_sems)
            mine.start()
            for cp in first:
                cp.start()
        for i in range(len(self.scatters)):
            mine, copies = self._scatter_copies(i, s_in[i], s_out[i], s_sems)
            mine.start()
            for cp in copies:
                cp.start()

    def finish(self, in_refs, out_refs, sems):
        g_in, s_in, g_out, s_out, g_sems, s_sems = self._split(in_refs, out_refs, sems)
        parts = [self._gather_copies(i, g_in[i], g_out[i], g_sems) for i in range(len(self.gathers))]
        for j in range(3):
            for _, _, passed, arrivals, _ in parts:
                arrivals[j].wait_recv()
                passed[j].start()
        for mine, first, passed, _, from_sibling in parts:
            for cp in from_sibling:
                cp.wait_recv()
            for cp in first + passed:
                cp.wait_send()
            mine.wait()
        for i in range(len(self.scatters)):
            mine, copies = self._scatter_copies(i, s_in[i], s_out[i], s_sems)
            for cp in copies:
                cp.wait_recv()
            for cp in copies:
                cp.wait_send()
            mine.wait()


def _params(grid):
    return pltpu.CompilerParams(dimension_semantics=("arbitrary",) * len(grid), vmem_limit_bytes=VMEM_LIMIT)


def _call(name, body, *, grid, ins, in_specs, out_shape, out_specs, scratch=(), comm=None, aliases=None):
    n_in, n_out, n_scr = len(ins), len(out_shape), len(scratch)
    aliases = aliases or {}
    if comm is None:
        outs = pl.pallas_call(body, grid=grid, in_specs=list(in_specs), out_specs=list(out_specs),
                              out_shape=list(out_shape), scratch_shapes=list(scratch), name=name,
                              input_output_aliases=aliases, compiler_params=_params(grid))(*ins)
        return list(outs), []
    c_ins, c_shapes, c_scratch = comm.operands()
    nci, nco = len(c_ins), len(c_shapes)
    anywhere = pl.BlockSpec(memory_space=pl.ANY)

    def carrier(*refs):
        main_in, comm_in = refs[:n_in], refs[n_in:n_in + nci]
        o0 = n_in + nci
        main_out, comm_out = refs[o0:o0 + n_out], refs[o0 + n_out:o0 + n_out + nco]
        s0 = o0 + n_out + nco
        main_scr, comm_scr = refs[s0:s0 + n_scr], refs[s0 + n_scr:]
        first = pl.program_id(0) == 0
        last = pl.program_id(0) == grid[0] - 1
        for ax in range(1, len(grid)):
            first = first & (pl.program_id(ax) == 0)
            last = last & (pl.program_id(ax) == grid[ax] - 1)

        @pl.when(first)
        def _():
            comm.start(comm_in, comm_out, comm_scr)

        body(*main_in, *main_out, *main_scr)

        @pl.when(last)
        def _():
            comm.finish(comm_in, comm_out, comm_scr)

    outs = pl.pallas_call(carrier, grid=grid, in_specs=list(in_specs) + [anywhere] * nci,
                          out_specs=list(out_specs) + [anywhere] * nco, out_shape=list(out_shape) + c_shapes,
                          scratch_shapes=list(scratch) + c_scratch, name=name, input_output_aliases=aliases,
                          compiler_params=_params(grid))(*ins, *c_ins)
    return list(outs[:n_out]), list(outs[n_out:])


def _comm_only(name, comm):
    c_ins, c_shapes, c_scratch = comm.operands()
    nci, nco = len(c_ins), len(c_shapes)
    anywhere = pl.BlockSpec(memory_space=pl.ANY)

    def body(*refs):
        comm.start(refs[:nci], refs[nci:nci + nco], refs[nci + nco:])
        comm.finish(refs[:nci], refs[nci:nci + nco], refs[nci + nco:])

    return list(pl.pallas_call(body, in_specs=[anywhere] * nci, out_specs=[anywhere] * nco, out_shape=c_shapes,
                               scratch_shapes=c_scratch, name=name)(*c_ins))


def _stage_fwd(name, f, grid, ins, in_specs, out_shapes, out_specs, comm=None):
    n_in = len(ins)

    def body(*refs):
        res = f(*[r[...].astype(F32) for r in refs[:n_in]])
        for o_ref, val in zip(refs[n_in:], res):
            o_ref[...] = val.astype(o_ref.dtype)

    return _call(name, body, grid=grid, ins=ins, in_specs=in_specs, out_shape=out_shapes, out_specs=out_specs, comm=comm)


def _stage_bwd(name, f, grid, ins, in_specs, cts, ct_specs, grads, comm=None):
    n_in = len(ins)
    flat_cts = [c for group in cts for c in group]
    flat_ct_specs = [s for group, spec in zip(cts, ct_specs) for s in (spec,) * len(group)]
    n_ct = len(flat_cts)
    diff = [g[0] for g in grads]
    buffers = [(k, g[4]) for k, g in enumerate(grads) if len(g) > 4]
    n_buf = len(buffers)

    def body(*refs):
        vals = [r[...].astype(F32) for r in refs[:n_in]]
        ct_refs = refs[n_in:n_in + n_ct]
        g_refs = refs[n_in + n_ct + n_buf:]
        ct_vals, pos = [], 0
        for group in cts:
            acc = ct_refs[pos][...].astype(F32)
            for j in range(1, len(group)):
                acc = acc + ct_refs[pos + j][...].astype(F32)
            ct_vals.append(acc)
            pos += len(group)

        def g_fn(*dvals):
            full = list(vals)
            for i, dv in zip(diff, dvals):
                full[i] = dv
            return f(*full)

        _, vjp = jax.vjp(g_fn, *[vals[i] for i in diff])
        gvals = vjp(tuple(ct_vals))
        for gspec, g_ref, gval in zip(grads, g_refs, gvals):
            acc_axes = gspec[1]
            if not acc_axes:
                g_ref[...] = gval.astype(g_ref.dtype)
            else:
                first = pl.program_id(acc_axes[0]) == 0
                for ax in acc_axes[1:]:
                    first = first & (pl.program_id(ax) == 0)

                @pl.when(first)
                def _():
                    g_ref[...] = jnp.zeros_like(g_ref)

                g_ref[...] += gval.astype(g_ref.dtype)

    out_shapes, out_specs = [], []
    for gspec in grads:
        shape, spec = gspec[3] if len(gspec) > 3 else (ins[gspec[0]].shape, in_specs[gspec[0]])
        out_shapes.append(jax.ShapeDtypeStruct(shape, gspec[2]))
        out_specs.append(spec)
    anywhere = pl.BlockSpec(memory_space=pl.ANY)
    return _call(name, body, grid=grid, ins=list(ins) + flat_cts + [b for _, b in buffers],
                 in_specs=list(in_specs) + flat_ct_specs + [anywhere] * n_buf, out_shape=out_shapes, out_specs=out_specs,
                 comm=comm, aliases={n_in + n_ct + j: k for j, (k, _) in enumerate(buffers)})


def _pick_tile(n, candidates):
    for c in candidates:
        if n % c == 0:
            return c
    return n


def _matmul(name, a, b, *, ta=False, tb=False, add=None, out_dtype=F32, comm=None):
    if ta:
        k_dim, m = a.shape
    else:
        m, k_dim = a.shape
    n = b.shape[0] if tb else b.shape[1]
    assert (b.shape[1] if tb else b.shape[0]) == k_dim and not (ta and tb)
    tm = _pick_tile(m, (1024, 1408, 512, 256, 128))
    tn = _pick_tile(n, (1024, 1408, 512, 256, 128))
    if ta:
        tk = _pick_tile(k_dim, (1024, 512, 256, 128))
    elif k_dim <= 2816:
        tk = k_dim
    else:
        tk = _pick_tile(k_dim, (1408, 1024, 512, 256, 128))
    nk = k_dim // tk
    grid = (m // tm, n // tn, nk)
    a_spec = pl.BlockSpec((tk, tm), lambda i, j, k: (k, i)) if ta else pl.BlockSpec((tm, tk), lambda i, j, k: (i, k))
    b_spec = pl.BlockSpec((tn, tk), lambda i, j, k: (j, k)) if tb else pl.BlockSpec((tk, tn), lambda i, j, k: (k, j))
    o_spec = pl.BlockSpec((tm, tn), lambda i, j, k: (i, j))
    dims = (((0 if ta else 1,), (1 if tb else 0,)), ((), ()))
    has_add = add is not None

    def body(*refs):
        a_ref, b_ref = refs[0], refs[1]
        add_ref = refs[2] if has_add else None
        o_ref, acc_ref = refs[-2], refs[-1]
        k = pl.program_id(2)
        part = lax.dot_general(a_ref[...].astype(BF16), b_ref[...].astype(BF16), dims, preferred_element_type=F32)

        def finish(res):
            if has_add:
                res = res + add_ref[...].astype(F32)
            o_ref[...] = res.astype(o_ref.dtype)

        if nk == 1:
            finish(part)
        else:
            @pl.when(k == 0)
            def _():
                acc_ref[...] = part

            @pl.when((k > 0) & (k < nk - 1))
            def _():
                acc_ref[...] += part

            @pl.when(k == nk - 1)
            def _():
                finish(acc_ref[...] + part)

    ins = [a, b] + ([add] if has_add else [])
    in_specs = [a_spec, b_spec] + ([o_spec] if has_add else [])
    acc_shape = (tm, tn) if nk > 1 else (8, 128)
    outs, comm_outs = _call(name, body, grid=grid, ins=ins, in_specs=in_specs,
                            out_shape=[jax.ShapeDtypeStruct((m, n), out_dtype)], out_specs=[o_spec],
                            scratch=[pltpu.VMEM(acc_shape, F32)], comm=comm)
    return outs[0], comm_outs


SSD_STATE = (SSM_GROUPS * SSM_STATE, SSM_RPG * SSM_HEADDIM)


def _ssd_fwd(xc, dt_raw, proj, dt_bias, a_log, d_skip, norm_g, nb, nc, comm=None):
    t = xc.shape[0]
    row = lambda b, c: (b * nc + c, 0)
    par = lambda shape: pl.BlockSpec(shape, lambda b, c: (0, 0))

    def body(xc_ref, dt_ref, z_ref, dtb_ref, al_ref, ds_ref, ng_ref, e_ref, y_ref, ypre_ref, prev_ref, st_ref):
        @pl.when(pl.program_id(1) == 0)
        def _():
            st_ref[...] = jnp.zeros_like(st_ref)

        prev = st_ref[...]
        prev_ref[0, 0] = prev
        y, y_pre, new_state = _ssd_forward(xc_ref[...], dt_ref[...], z_ref[...].astype(F32), prev, dtb_ref[...],
                                           al_ref[...], ds_ref[...], ng_ref[...], e_ref[...])
        y_ref[...] = y.astype(y_ref.dtype)
        ypre_ref[...] = y_pre
        st_ref[...] = new_state

    return _call(
        "ssd_fwd", body, grid=(nb, nc), ins=[xc, dt_raw, proj, dt_bias, a_log, d_skip, norm_g, _head_expander()],
        in_specs=[pl.BlockSpec((CHUNK, SSM_CONV_DIM), row), pl.BlockSpec((CHUNK, 128), row),
                  pl.BlockSpec((CHUNK, SSM_INNER), lambda b, c: (b * nc + c, 1)),
                  par((1, 128)), par((1, 128)), par((1, SSM_INNER)), par((1, SSM_INNER)), par((128, SSM_INNER))],
        out_specs=[pl.BlockSpec((CHUNK, SSM_INNER), row), pl.BlockSpec((CHUNK, SSM_INNER), row),
                   pl.BlockSpec((1, 1) + SSD_STATE, lambda b, c: (b, c, 0, 0))],
        out_shape=[jax.ShapeDtypeStruct((t, SSM_INNER), BF16), jax.ShapeDtypeStruct((t, SSM_INNER), F32),
                   jax.ShapeDtypeStruct((nb, nc) + SSD_STATE, F32)],
        scratch=[pltpu.VMEM(SSD_STATE, F32)], comm=comm)


def _ssd_bwd(xc, dt_raw, proj, prevs, y_pre, dt_bias, a_log, d_skip, norm_g, dy, dproj, nb, nc, comm=None):
    t = xc.shape[0]
    row = lambda b, c: (b * nc + (nc - 1 - c), 0)
    par = lambda shape: pl.BlockSpec(shape, lambda b, c: (0, 0))
    z_spec = pl.BlockSpec((CHUNK, SSM_INNER), lambda b, c: (b * nc + (nc - 1 - c), 1))

    def body(xc_ref, dt_ref, z_ref, prev_ref, ypre_ref, dtb_ref, al_ref, ds_ref, ng_ref, e_ref, dy_ref, _,
             dxc_ref, ddt_ref, dz_ref, ddtb_ref, dal_ref, dds_ref, dng_ref, dst_ref):
        @pl.when(pl.program_id(1) == 0)
        def _():
            dst_ref[...] = jnp.zeros_like(dst_ref)

        @pl.when((pl.program_id(0) == 0) & (pl.program_id(1) == 0))
        def _():
            ddtb_ref[...] = jnp.zeros_like(ddtb_ref)
            dal_ref[...] = jnp.zeros_like(dal_ref)
            dds_ref[...] = jnp.zeros_like(dds_ref)
            dng_ref[...] = jnp.zeros_like(dng_ref)

        dxc, ddt, dz, dprev, ddtb, dal, dds, dng = _ssd_backward(
            xc_ref[...], dt_ref[...], z_ref[...].astype(F32), prev_ref[0, 0], ypre_ref[...], dtb_ref[...], al_ref[...],
            ds_ref[...], ng_ref[...], e_ref[...], dy_ref[...].astype(F32), dst_ref[...])
        dxc_ref[...] = dxc
        ddt_ref[...] = ddt.astype(ddt_ref.dtype)
        dz_ref[...] = dz.astype(dz_ref.dtype)
        dst_ref[...] = dprev
        ddtb_ref[...] += ddtb
        dal_ref[...] += dal
        dds_ref[...] += dds
        dng_ref[...] += dng

    return _call(
        "ssd_bwd", body, grid=(nb, nc),
        ins=[xc, dt_raw, proj, prevs, y_pre, dt_bias, a_log, d_skip, norm_g, _head_expander(), dy, dproj],
        in_specs=[pl.BlockSpec((CHUNK, SSM_CONV_DIM), row), pl.BlockSpec((CHUNK, DT_LANES), row), z_spec,
                  pl.BlockSpec((1, 1) + SSD_STATE, lambda b, c: (b, nc - 1 - c, 0, 0)),
                  pl.BlockSpec((CHUNK, SSM_INNER), row),
                  par((1, 128)), par((1, 128)), par((1, SSM_INNER)), par((1, SSM_INNER)), par((128, SSM_INNER)),
                  pl.BlockSpec((CHUNK, SSM_INNER), row), pl.BlockSpec(memory_space=pl.ANY)],
        out_specs=[pl.BlockSpec((CHUNK, SSM_CONV_DIM), row), pl.BlockSpec((CHUNK, DT_LANES), row), z_spec,
                   par((1, 128)), par((1, 128)), par((1, 128)), par((1, SSM_INNER))],
        out_shape=[jax.ShapeDtypeStruct((t, SSM_CONV_DIM), F32), jax.ShapeDtypeStruct((t, DT_LANES), BF16),
                   jax.ShapeDtypeStruct(dproj.shape, dproj.dtype), jax.ShapeDtypeStruct((1, 128), F32),
                   jax.ShapeDtypeStruct((1, 128), F32), jax.ShapeDtypeStruct((1, 128), F32),
                   jax.ShapeDtypeStruct((1, SSM_INNER), F32)],
        scratch=[pltpu.VMEM(SSD_STATE, F32)], comm=comm, aliases={11: 2})


def _loss_head(y, target):
    t, d = y.shape
    tm = _pick_tile(t, (256,))
    blk = pl.BlockSpec((tm, d), lambda i: (i, 0))

    def body(y_ref, t_ref, loss_ref, dy_ref):
        err = y_ref[...] - t_ref[...]
        dy_ref[...] = err * (1.0 / d)

        @pl.when(pl.program_id(0) == 0)
        def _():
            loss_ref[...] = jnp.zeros_like(loss_ref)

        loss_ref[...] += 0.5 * jnp.sum(jnp.mean(err * err, axis=-1, keepdims=True), axis=0, keepdims=True)

    return _call("loss_head", body, grid=(t // tm,), ins=[y, target], in_specs=[blk, blk],
                 out_specs=[pl.BlockSpec((1, 1), lambda i: (0, 0)), blk],
                 out_shape=[jax.ShapeDtypeStruct((1, 1), F32), jax.ShapeDtypeStruct((t, d), F32)])[0]


def _adamw_math(g, w, m, v):
    m_new = ADAM_B1 * m + (1.0 - ADAM_B1) * g
    v_new = ADAM_B2 * v + (1.0 - ADAM_B2) * jnp.square(g)
    m_hat = m_new / (1.0 - ADAM_B1 ** ADAM_STEP)
    v_hat = v_new / (1.0 - ADAM_B2 ** ADAM_STEP)
    delta = -ADAM_LR * (m_hat / (jnp.sqrt(v_hat) + ADAM_EPS) + ADAM_WD * w)
    return delta, m_new, v_new


def _adamw_sharded(name, parts, w, m, v):
    _, a, b = w.shape
    tr = _pick_tile(a, (128,))
    nt = a // tr
    part_specs = [pl.BlockSpec((N_DEV, tr, b),
                               (lambda l, i, _k=k: (0, jnp.where(l == _k, i, jnp.where(l > _k, nt - 1, 0)), 0)))
                  for k in range(DEPTH)]
    blk = pl.BlockSpec((1, tr, b), lambda l, i: (l, i, 0))

    def body(*refs):
        p_refs = refs[:DEPTH]
        w_ref, m_ref, v_ref, g_out, d_out, m_out, v_out = refs[DEPTH:]
        for k in range(DEPTH):
            @pl.when(pl.program_id(0) == k)
            def _(p_ref=p_refs[k]):
                g = p_ref[0].astype(F32)
                for p in range(1, N_DEV):
                    g = g + p_ref[p].astype(F32)
                delta, m_new, v_new = _adamw_math(g, w_ref[0], m_ref[0], v_ref[0])
                g_out[0] = g
                d_out[0] = delta
                m_out[0] = m_new
                v_out[0] = v_new

    return _call(name, body, grid=(DEPTH, nt), ins=list(parts) + [w, m, v], in_specs=part_specs + [blk, blk, blk],
                 out_specs=[blk] * 4, out_shape=[jax.ShapeDtypeStruct(w.shape, F32)] * 4)[0]


def _adamw_small(name, g, w, m, v):
    full = pl.BlockSpec(w.shape, lambda i: (0, 0))

    def body(g_ref, w_ref, m_ref, v_ref, d_out, m_out, v_out):
        delta, m_new, v_new = _adamw_math(g_ref[...], w_ref[...], m_ref[...], v_ref[...])
        d_out[...] = delta
        m_out[...] = m_new
        v_out[...] = v_new

    return _call(name, body, grid=(1,), ins=[g, w, m, v], in_specs=[full] * 4, out_specs=[full] * 3,
                 out_shape=[jax.ShapeDtypeStruct(w.shape, F32)] * 3)[0]


def _sum_parts(name, parts):
    n_parts, rows, cols = parts.shape
    tr = _pick_tile(rows, (512, 256, 128, 64, 32, 16, 8))

    def body(p_ref, o_ref):
        acc = p_ref[0]
        for p in range(1, n_parts):
            acc = acc + p_ref[p]
        o_ref[...] = acc

    return _call(name, body, grid=(rows // tr,), ins=[parts],
                 in_specs=[pl.BlockSpec((n_parts, tr, cols), lambda i: (0, i, 0))],
                 out_specs=[pl.BlockSpec((tr, cols), lambda i: (i, 0))],
                 out_shape=[jax.ShapeDtypeStruct((rows, cols), parts.dtype)])[0][0]


def _pack_w_in(w):
    main = jnp.concatenate([w[:, :XBC_COL], w[:, GA_COL:], w[:, XBC_COL:DT_COL]], axis=1)
    return main, jnp.pad(w[:, DT_COL:GA_COL], ((0, 0), (0, DT_LANES - SSM_HEADS)))


def _unpack_w_in(main, dt):
    return jnp.concatenate([main[:, :P_GATE], main[:, P_XBC:], dt[:, :SSM_HEADS], main[:, P_GATE:P_XBC]], axis=1)


def _pad_heads(v):
    return jnp.pad(v, (0, 128 - SSM_HEADS)).reshape(1, 128)


def _run_step(x, mem, target, small, ex):
    nb, s, d = x.shape
    t = nb * s
    nc = s // CHUNK
    rows = _pick_tile(t, (256,))
    rows_wide = _pick_tile(t, (512, 256))
    tq = _pick_tile(s, (512, 256))
    vec = lambda a: a.reshape(1, -1)
    full1 = lambda shape: pl.BlockSpec(shape, lambda i: (0,) * len(shape))
    row1 = lambda tm, w: pl.BlockSpec((tm, w), lambda i: (i, 0))
    sds = jax.ShapeDtypeStruct

    def mm(call, l, a, b, **kw):
        comm = ex.before(call, l)
        out, comm_outs = _matmul(call, a, b, comm=comm, **kw)
        if comm is not None:
            ex.after(call, l, comm_outs)
        return out

    def stage_bwd(call, l, *args):
        comm = ex.before(call, l)
        outs, comm_outs = _stage_bwd(call, *args, comm=comm)
        if comm is not None:
            ex.after(call, l, comm_outs)
        return outs

    def stage_fwd(call, l, *args):
        comm = ex.before(call, l)
        outs, comm_outs = _stage_fwd(call, *args, comm=comm)
        if comm is not None:
            ex.after(call, l, comm_outs)
        return outs

    mem_specs = [row1(256, d), full1((1, d)), full1((1, d))]
    mem_ins = [mem.reshape(nb * MEM_LEN, d), vec(small["mem_ln_g"]), vec(small["mem_ln_b"])]
    (mem_n,) = stage_fwd("memln_fwd", 0, _memln_block, (nb * MEM_LEN // 256,), mem_ins, mem_specs,
                          [sds((nb * MEM_LEN, d), BF16)], [row1(256, d)])

    h = x.reshape(t, d)
    h_bf = h.astype(BF16)
    ln_specs = [row1(rows, d), row1(rows, d), full1((1, d)), full1((1, d))]
    ln_outs = [sds((t, d), F32), sds((t, d), BF16)]
    ln_out_specs = [row1(rows, d), row1(rows, d)]
    saved = []
    for l in range(DEPTH):
        sv = {"h_bf": h_bf}
        w_p, w_dt = ex.weight("w_in", l)
        proj = mm("mm_in", l, h_bf, w_p, out_dtype=BF16)
        dt_raw = mm("mm_dt", l, h_bf, w_dt)
        sv["proj"] = proj
        sgu_ins = [proj, vec(small["sg_ln_g"][l]), vec(small["sg_ln_b"][l]), small["sg_w"][l], small["sg_b"][l].T]
        sgu_specs = [pl.BlockSpec((CHUNK, 2 * d), lambda i: (i, 0)), full1((1, d)), full1((1, d)),
                     full1((SG_GROUPS, CHUNK, CHUNK)), full1((CHUNK, SG_GROUPS))]
        (a_out,) = stage_fwd("sgu_fwd", l, _sgu_block, (t // CHUNK,), sgu_ins, sgu_specs, [sds((t, d), BF16)],
                              [row1(CHUNK, d)])
        sv["sgu"] = (sgu_ins, sgu_specs)
        sv["a_out"] = a_out
        cw = 256
        conv_ins = [proj, small["conv_w"][l], vec(small["conv_b"][l])]
        conv_specs = [pl.BlockSpec((s, cw), lambda j, b: (b, P_XBC // cw + j)),
                      pl.BlockSpec((SSM_CONV, cw), lambda j, b: (0, j)), pl.BlockSpec((1, cw), lambda j, b: (0, j))]
        conv_out_spec = pl.BlockSpec((s, cw), lambda j, b: (b, j))
        (xc,) = stage_fwd("conv_fwd", l, _conv_block, (SSM_CONV_DIM // cw, nb), conv_ins, conv_specs,
                           [sds((t, SSM_CONV_DIM), F32)], [conv_out_spec])
        sv["conv"] = (conv_ins, conv_specs, conv_out_spec)
        ssd_par = [_pad_heads(small["dt_bias"][l]), _pad_heads(small["a_log"][l]),
                   vec(jnp.repeat(small["d_skip"][l], SSM_HEADDIM)), vec(small["ssm_norm_g"][l])]
        comm = ex.before("ssd_fwd", l)
        (y_ssd, y_pre, prevs), comm_outs = _ssd_fwd(xc, dt_raw, proj, *ssd_par, nb, nc, comm=comm)
        if comm is not None:
            ex.after("ssd_fwd", l, comm_outs)
        sv["ssd"] = (xc, dt_raw, prevs, y_pre, ssd_par)
        sv["y_ssd"] = y_ssd
        br_a = mm("mm_sq", l, a_out, ex.weight("p_a", l), out_dtype=BF16)
        br_b = mm("mm_pb", l, y_ssd, ex.weight("p_b", l), out_dtype=BF16)
        merge_ins = [proj, br_a, br_b]
        merge_out_spec = row1(rows_wide, d)
        merge_specs = [pl.BlockSpec((rows_wide, 2 * d), lambda i: (i, P_GATE // (2 * d))), merge_out_spec, merge_out_spec]
        (merged,) = stage_fwd("merge_fwd", l, _merge_block, (t // rows_wide,), merge_ins, merge_specs,
                               [sds((t, d), BF16)], [merge_out_spec])
        sv["merge"] = (merge_ins, merge_specs, merge_out_spec)
        sv["merged"] = merged
        y1 = mm("mm_sq", l, merged, ex.weight("w_mix_o", l), out_dtype=BF16)
        ln1_ins = [h, y1, vec(small["ln_g"][l, 0]), vec(small["ln_b"][l, 0])]
        h1, h1_bf = stage_fwd("lnres_fwd", l, _lnres_block_twice, (t // rows,), ln1_ins, ln_specs, ln_outs, ln_out_specs)
        sv["ln1"] = ln1_ins
        q = mm("mm_sq", l, h1_bf, ex.weight("w_xq", l), out_dtype=BF16)
        kv = mm("mm_kv", l, mem_n, ex.weight("w_xkv", l), out_dtype=BF16)
        attn_ins = [q, kv]
        attn_out_spec = pl.BlockSpec((tq, d), lambda b, i: (b * (s // tq) + i, 0))
        attn_specs = [attn_out_spec, pl.BlockSpec((MEM_LEN, 2 * d), lambda b, i: (b, 0))]
        (o,) = stage_fwd("attn_fwd", l, _attn_block, (nb, s // tq), attn_ins, attn_specs, [sds((t, d), BF16)],
                          [attn_out_spec])
        sv["attn"] = (attn_ins, attn_specs, attn_out_spec)
        sv["o"] = o
        sv["h1_bf"] = h1_bf
        y2 = mm("mm_sq", l, o, ex.weight("w_xo", l), out_dtype=BF16)
        ln2_ins = [h1, y2, vec(small["ln_g"][l, 1]), vec(small["ln_b"][l, 1])]
        h2, h2_bf = stage_fwd("lnres_fwd", l, _lnres_block_twice, (t // rows,), ln2_ins, ln_specs, ln_outs, ln_out_specs)
        sv["ln2"] = ln2_ins
        sv["h2_bf"] = h2_bf
        gu = mm("mm_ffn_in", l, h2_bf, ex.weight("w_ffn_in", l), out_dtype=BF16)
        (act,) = stage_fwd("swiglu_fwd", l, _swiglu_block, (t // 128,), [gu], [row1(128, 2 * FFN_HIDDEN)],
                            [sds((t, FFN_HIDDEN), BF16)], [row1(128, FFN_HIDDEN)])
        sv["gu"] = gu
        sv["act"] = act
        y3 = mm("mm_ffn_out", l, act, ex.weight("w_ffn_out", l), out_dtype=BF16)
        ln3_ins = [h2, y3, vec(small["ln_g"][l, 2]), vec(small["ln_b"][l, 2])]
        h, h_bf = stage_fwd("lnres_fwd", l, _lnres_block_twice, (t // rows,), ln3_ins, ln_specs, ln_outs, ln_out_specs)
        sv["ln3"] = ln3_ins
        saved.append(sv)

    loss, dh = _loss_head(h, target.reshape(t, d))

    g_small = {n: [None] * DEPTH for n in SMALL_REP + SMALL_SH if n not in ("mem_ln_g", "mem_ln_b")}
    dmem_n = []
    ln_grads = [(0, (), F32), (1, (), BF16), (2, (0,), F32), (3, (0,), F32)]
    for l in reversed(range(DEPTH)):
        sv = saved[l]
        dln_g, dln_b = [None] * 3, [None] * 3
        dres, dy3, dln_g[2], dln_b[2] = stage_bwd("lnres_bwd", l, _lnres_block, (t // rows,), sv["ln3"], ln_specs,
                                                  [(dh,)], [row1(rows, d)], ln_grads)
        ex.grad("w_ffn_out", l, mm("mm_ffn_out_dw", l, sv["act"], dy3, ta=True, out_dtype=BF16))
        dact = mm("mm_ffn_out_dx", l, dy3, ex.weight("w_ffn_out", l), tb=True, out_dtype=BF16)
        (dgu,) = stage_bwd("swiglu_bwd", l, _swiglu_block, (t // 128,), [sv["gu"]], [row1(128, 2 * FFN_HIDDEN)],
                           [(dact,)], [row1(128, FFN_HIDDEN)], [(0, (), BF16)])
        ex.grad("w_ffn_in", l, mm("mm_ffn_in_dw", l, sv["h2_bf"], dgu, ta=True, out_dtype=BF16))
        dh2 = mm("mm_ffn_in_dx", l, dgu, ex.weight("w_ffn_in", l), tb=True, add=dres)
        dres, dy2, dln_g[1], dln_b[1] = stage_bwd("lnres_bwd", l, _lnres_block, (t // rows,), sv["ln2"], ln_specs,
                                                  [(dh2,)], [row1(rows, d)], ln_grads)
        ex.grad("w_xo", l, mm("mm_sq_dw", l, sv["o"], dy2, ta=True, out_dtype=BF16))
        do = mm("mm_sq_dx", l, dy2, ex.weight("w_xo", l), tb=True, out_dtype=BF16)
        attn_ins, attn_specs, attn_out_spec = sv["attn"]
        dq, dkv = stage_bwd("attn_bwd", l, _attn_block, (nb, s // tq), attn_ins, attn_specs, [(do,)], [attn_out_spec],
                            [(0, (), BF16), (1, (1,), F32)])
        ex.grad("w_xq", l, mm("mm_sq_dw", l, sv["h1_bf"], dq, ta=True, out_dtype=BF16))
        dh1 = mm("mm_sq_dx", l, dq, ex.weight("w_xq", l), tb=True, add=dres)
        ex.grad("w_xkv", l, mm("mm_kv_dw", l, mem_n, dkv, ta=True, out_dtype=BF16))
        dmem_n.append(mm("mm_kv_dx", l, dkv, ex.weight("w_xkv", l), tb=True))
        dres, dy1, dln_g[0], dln_b[0] = stage_bwd("lnres_bwd", l, _lnres_block, (t // rows,), sv["ln1"], ln_specs,
                                                  [(dh1,)], [row1(rows, d)], ln_grads)
        g_small["ln_g"][l] = jnp.concatenate(dln_g, axis=0)
        g_small["ln_b"][l] = jnp.concatenate(dln_b, axis=0)
        ex.grad("w_mix_o", l, mm("mm_sq_dw", l, sv["merged"], dy1, ta=True, out_dtype=BF16))
        dmerged = mm("mm_sq_dx", l, dy1, ex.weight("w_mix_o", l), tb=True, out_dtype=BF16)
        merge_ins, merge_specs, merge_out_spec = sv["merge"]
        dproj, dbr_a, dbr_b = stage_bwd("merge_bwd", l, _merge_block, (t // rows_wide,), merge_ins, merge_specs,
                                        [(dmerged,)], [merge_out_spec],
                                        [(0, (), BF16, ((t, P_COLS), merge_specs[0])), (1, (), BF16), (2, (), BF16)])
        ex.grad("p_a", l, mm("mm_sq_dw", l, sv["a_out"], dbr_a, ta=True, out_dtype=BF16))
        da_out = mm("mm_sq_dx", l, dbr_a, ex.weight("p_a", l), tb=True, out_dtype=BF16)
        ex.grad("p_b", l, mm("mm_pb_dw", l, sv["y_ssd"], dbr_b, ta=True, out_dtype=BF16))
        dy_ssd = mm("mm_pb_dx", l, dbr_b, ex.weight("p_b", l), tb=True, out_dtype=BF16)
        sgu_ins, sgu_specs = sv["sgu"]
        dproj, dsg_ln_g, dsg_ln_b, dsg_w, dsg_b = stage_bwd(
            "sgu_bwd", l, _sgu_block, (t // CHUNK,), sgu_ins, sgu_specs, [(da_out,)], [row1(CHUNK, d)],
            [(0, (), BF16, ((t, P_COLS), sgu_specs[0]), dproj), (1, (0,), F32), (2, (0,), F32), (3, (0,), F32),
             (4, (0,), F32)])
        g_small["sg_ln_g"][l], g_small["sg_ln_b"][l], g_small["sg_w"][l], g_small["sg_b"][l] = (
            dsg_ln_g[0], dsg_ln_b[0], dsg_w, dsg_b.T)
        xc, dt_raw, prevs, y_pre, ssd_par = sv["ssd"]
        comm = ex.before("ssd_bwd", l)
        (dxc, ddt, dproj, ddtb, dal, dds, dng), comm_outs = _ssd_bwd(xc, dt_raw, sv["proj"], prevs, y_pre, *ssd_par,
                                                                     dy_ssd, dproj, nb, nc, comm=comm)
        if comm is not None:
            ex.after("ssd_bwd", l, comm_outs)
        g_small["dt_bias"][l], g_small["a_log"][l], g_small["d_skip"][l] = (
            ddtb[0, :SSM_HEADS], dal[0, :SSM_HEADS], dds[0, :SSM_HEADS])
        g_small["ssm_norm_g"][l] = dng[0]
        conv_ins, conv_specs, conv_out_spec = sv["conv"]
        dproj, dconv_w, dconv_b = stage_bwd("conv_bwd", l, _conv_block, (SSM_CONV_DIM // 256, nb), conv_ins, conv_specs,
                                            [(dxc,)], [conv_out_spec],
                                            [(0, (), BF16, ((t, P_COLS), conv_specs[0]), dproj), (1, (1,), F32),
                                             (2, (1,), F32)])
        g_small["conv_w"][l], g_small["conv_b"][l] = dconv_w, dconv_b[0]
        if l == 0:
            dmg, dmb = stage_bwd("memln_bwd", l, _memln_block, (nb * MEM_LEN // 256,), mem_ins, mem_specs,
                                 [tuple(dmem_n)], [row1(256, d)], [(1, (0,), F32), (2, (0,), F32)])
            done = {n: jnp.stack(g, axis=0) for n, g in g_small.items()}
            done["mem_ln_g"], done["mem_ln_b"] = dmg[0], dmb[0]
            ex.small_grads(done)
        w_p, w_dt = ex.weight("w_in", l)
        g_dt = mm("mm_dt_dw", l, sv["h_bf"], ddt, ta=True, out_dtype=BF16)
        for k, call in enumerate(("mm_in_dw_a", "mm_in_dw_b")):
            rows_k = slice(k * d // 2, (k + 1) * d // 2)
            g_main = mm(call, l, sv["h_bf"][:, rows_k], dproj, ta=True, out_dtype=BF16)
            ex.grad(("w_in", k), l, _unpack_w_in(g_main, g_dt[rows_k]))
        dh = mm("mm_in_dx", l, dproj, w_p, tb=True, add=mm("mm_dt_dx", l, ddt, w_dt, tb=True, add=dres))

    return loss, dh.reshape(nb, s, d)


def _pack_flat(arrays, rows):
    flat = jnp.concatenate([a.reshape(-1) for a in arrays])
    return jnp.pad(flat, (0, rows * 128 - flat.shape[0])).reshape(rows, 128)


def _unpack_flat(packed, shapes):
    lead = packed.shape[:-2]
    flat = packed.reshape(lead + (-1,))
    out, pos = [], 0
    for shape in shapes:
        n = math.prod(shape)
        out.append(flat[..., pos:pos + n].reshape(lead + tuple(shape)))
        pos += n
    return out


def _small_rows(n_elems):
    return -(-n_elems // (128 * SMALL_ROW_TILE)) * SMALL_ROW_TILE


def _from_shards(name, gathered):
    _, a, b = gathered.shape
    if name in BIG_COL_SHARDED:
        w = gathered.transpose(1, 0, 2).reshape(a, N_DEV * b)
        return _pack_w_in(w) if name == "w_in" else w
    return gathered.reshape(N_DEV * a, b)


def _to_shards(name, g):
    if name in BIG_COL_SHARDED:
        a, nb = g.shape
        return g.reshape(a, N_DEV, nb // N_DEV).transpose(1, 0, 2)
    a, b = g.shape
    return g.reshape(N_DEV, a // N_DEV, b)


class _MeshExchange:
    def __init__(self, shards_bf16, first):
        self.shards = shards_bf16
        self.full = dict(first)
        self.pieces = {}
        self.grads = {}
        self.to_send = {}
        self.received = {}
        self.small = None
        self.small_gathered = None

    def weight(self, name, l):
        if (name, l) not in self.full:
            got = jnp.concatenate([self.pieces[(name, l, q)] for q in range(W_IN_PIECES)], axis=1)
            self.full[(name, l)] = _from_shards(name, got)
        return self.full[(name, l)]

    def grad(self, name, l, g):
        self.grads[(name, l)] = g

    def small_grads(self, done):
        self.small = done

    def partial_sums(self, name, l):
        if (name, l, None) in self.received:
            return self.received[(name, l, None)]
        return jnp.concatenate([self.received[(name, l, q)] for q in range(W_IN_PIECES)], axis=1)

    def _slices(self, name, l, piece):
        if piece is None:
            key, rows = (name, l), None
        else:
            per_half = W_IN_PIECES // 2
            n_rows = D_MODEL // W_IN_PIECES
            key, rows = ((name, piece // per_half), l), ((piece % per_half) * n_rows, n_rows)
        if key not in self.to_send:
            self.to_send[key] = _to_shards(name, self.grads[key])
        return self.to_send[key], rows

    def before(self, call, l):
        comm = _Comm()
        for name, layer, piece in GATHER_PLAN.get((call, l), ()):
            n_rows = D_MODEL // W_IN_PIECES
            comm.gathers.append((self.shards[name], layer, None if piece is None else (piece * n_rows, n_rows)))
        if (call, l) == SMALL_GATHER_CALL:
            names = SMALL_REP + SMALL_SH
            rows = _small_rows(sum(math.prod(self.small[n].shape) for n in names))
            comm.gathers.append((_pack_flat([self.small[n] for n in names], rows), None, None))
        for name, layer, piece in SCATTER_PLAN.get((call, l), ()):
            comm.scatters.append(self._slices(name, layer, piece))
        return comm if comm.gathers or comm.scatters else None

    def after(self, call, l, outs):
        gathers = list(GATHER_PLAN.get((call, l), ()))
        for (name, layer, piece), out in zip(gathers, outs):
            if piece is None:
                self.full[(name, layer)] = _from_shards(name, out)
            else:
                self.pieces[(name, layer, piece)] = out
        outs = outs[len(gathers):]
        if (call, l) == SMALL_GATHER_CALL:
            self.small_gathered = outs[0]
            outs = outs[1:]
        for item, out in zip(SCATTER_PLAN.get((call, l), ()), outs):
            self.received[item] = out


def kernel(x, mem, mem_ln_g, mem_ln_b, w_in, sg_ln_g, sg_ln_b, sg_w, sg_b, conv_w, conv_b, dt_bias, a_log, d_skip, ssm_norm_g, p_a, p_b, w_mix_o, w_xq, w_xkv, w_xo, w_ffn_in, w_ffn_out, ln_g, ln_b, loss_target, m_mem_ln_g, m_mem_ln_b, m_w_in, m_sg_ln_g, m_sg_ln_b, m_sg_w, m_sg_b, m_conv_w, m_conv_b, m_dt_bias, m_a_log, m_d_skip, m_ssm_norm_g, m_p_a, m_p_b, m_w_mix_o, m_w_xq, m_w_xkv, m_w_xo, m_w_ffn_in, m_w_ffn_out, m_ln_g, m_ln_b, v_mem_ln_g, v_mem_ln_b, v_w_in, v_sg_ln_g, v_sg_ln_b, v_sg_w, v_sg_b, v_conv_w, v_conv_b, v_dt_bias, v_a_log, v_d_skip, v_ssm_norm_g, v_p_a, v_p_b, v_w_mix_o, v_w_xq, v_w_xkv, v_w_xo, v_w_ffn_in, v_w_ffn_out, v_ln_g, v_ln_b):
    args = dict(locals())
    w = {n: args[n] for n in WEIGHTS}
    m = {n: args["m_" + n] for n in WEIGHTS}
    v = {n: args["v_" + n] for n in WEIGHTS}
    me = 4 * lax.axis_index("x") + 2 * lax.axis_index("y") + lax.axis_index("c")

    shards = {n: w[n].astype(BF16) for n in BIG}
    sh_shapes = [w[n].shape for n in SMALL_SH]
    first = _Comm()
    first.gathers.append((shards["w_in"], 0, None))
    first.gathers.append((_pack_flat([w[n] for n in SMALL_SH], _small_rows(sum(math.prod(s) for s in sh_shapes))), None,
                          None))
    w_in0, small_sh = _comm_only("gather_first", first)
    small = {n: w[n] for n in SMALL_REP}
    for n, sh in zip(SMALL_SH, _unpack_flat(small_sh, sh_shapes)):
        small[n] = sh.transpose(1, 2, 0, 3).reshape(sh.shape[1], sh.shape[2], N_DEV * sh.shape[3])

    ex = _MeshExchange(shards, {("w_in", 0): _from_shards("w_in", w_in0)})
    loss, grad_x = _run_step(x, mem, loss_target, small, ex)
    loss = lax.psum(loss[0, 0], ("x", "y", "c"))

    out = {}
    for n in BIG:
        out[n] = _adamw_sharded("adamw_" + n, [ex.partial_sums(n, l) for l in range(DEPTH)], w[n], m[n], v[n])
    names = SMALL_REP + SMALL_SH
    g_small = dict(zip(names, _unpack_flat(_sum_parts("sum_small_grads", ex.small_gathered),
                                           [ex.small[n].shape for n in names])))
    for n in names:
        g = g_small[n]
        if n in SMALL_SH:
            width = w[n].shape[-1]
            g = lax.dynamic_slice_in_dim(g, me * width, width, axis=-1)
        two_d = (-1, w[n].shape[-1])
        res = _adamw_small("adamw_" + n, g.reshape(two_d), w[n].reshape(two_d), m[n].reshape(two_d), v[n].reshape(two_d))
        out[n] = [g] + [r.reshape(w[n].shape) for r in res]

    results = []
    for k in range(4):
        results.extend(out[n][k] for n in WEIGHTS)
    return (loss, grad_x, *results)
```

```python
import functools
import math

import jax
import jax.numpy as jnp
from jax import lax
from jax.experimental import pallas as pl
from jax.experimental.pallas import tpu as pltpu

F32 = jnp.float32
BF16 = jnp.bfloat16
HIGHEST = lax.Precision.HIGHEST

N_DEV = 8
D_MODEL = 1024
DEPTH = 2
MEM_LEN = 256
CHUNK = 128
SG_GROUPS = 8
SSM_INNER = 2048
SSM_HEADDIM = 64
SSM_HEADS = 32
SSM_STATE = 128
SSM_GROUPS = 4
SSM_RPG = 8
SSM_CONV = 4
SSM_CONV_DIM = 3072
X_HEADS = 4
X_HEADDIM = 256
FFN_HIDDEN = 2816
ALPHA = float((2 * DEPTH) ** 0.25)
LN_EPS = 1e-5
RMS_EPS = 1e-5
XBC_COL = 4096
DT_COL = 7168
GA_COL = 7200
IN_COLS = 9248
P_GATE = 4096
P_XBC = 6144
P_COLS = 9216
DT_LANES = 128

ADAM_LR = 0.001
ADAM_B1 = 0.9
ADAM_B2 = 0.999
ADAM_EPS = 1e-08
ADAM_WD = 0.01
ADAM_STEP = 10

VMEM_LIMIT = 48 * 1024 * 1024
SMALL_ROW_TILE = 256

BIG = ("w_in", "p_a", "p_b", "w_mix_o", "w_xq", "w_xkv", "w_xo", "w_ffn_in", "w_ffn_out")
BIG_COL_SHARDED = ("w_in", "w_xkv", "w_ffn_in")
SMALL_REP = ("mem_ln_g", "mem_ln_b", "sg_ln_g", "sg_ln_b", "sg_w", "sg_b", "conv_b", "dt_bias", "a_log", "d_skip",
             "ssm_norm_g")
SMALL_SH = ("conv_w", "ln_g", "ln_b")
WEIGHTS = ("mem_ln_g", "mem_ln_b", "w_in", "sg_ln_g", "sg_ln_b", "sg_w", "sg_b", "conv_w", "conv_b", "dt_bias", "a_log",
           "d_skip", "ssm_norm_g", "p_a", "p_b", "w_mix_o", "w_xq", "w_xkv", "w_xo", "w_ffn_in", "w_ffn_out", "ln_g", "ln_b")

W_IN_PIECES = 4
GATHER_PLAN = {("sgu_fwd", 0): [("w_in", 1, 0)], ("conv_fwd", 0): [("w_in", 1, 1)],
               ("mm_ffn_in", 0): [("w_in", 1, 2)], ("swiglu_fwd", 0): [("w_in", 1, 3)]}
SCATTER_PLAN = {("mm_ffn_in_dw", 0): [("w_in", 1, 3)]}
for _l in range(DEPTH):
    GATHER_PLAN[("mm_in", _l)] = [(n, _l, None) for n in ("p_a", "p_b", "w_mix_o", "w_xq", "w_xkv", "w_xo")]
    GATHER_PLAN[("ssd_fwd", _l)] = [("w_ffn_in", _l, None), ("w_ffn_out", _l, None)]
    SCATTER_PLAN[("swiglu_bwd", _l)] = [("w_ffn_out", _l, None)]
    SCATTER_PLAN[("sgu_bwd", _l)] = [("w_mix_o", _l, None), ("p_a", _l, None), ("w_xo", _l, None)]
    SCATTER_PLAN[("ssd_bwd", _l)] = [("w_ffn_in", _l, None), ("w_xkv", _l, None), ("w_xq", _l, None)]
    SCATTER_PLAN[("conv_bwd", _l)] = [("p_b", _l, None)]
    SCATTER_PLAN[("mm_in_dw_b", _l)] = [("w_in", _l, 0)]
    SCATTER_PLAN[("mm_in_dx", _l)] = [("w_in", _l, 1), ("w_in", _l, 2)] + ([("w_in", _l, 3)] if _l == 0 else [])
SMALL_GATHER_CALL = ("mm_in_dw_a", 0)


def _layer_norm(x, g, b):
    mu = jnp.mean(x, axis=-1, keepdims=True)
    xc = x - mu
    var = jnp.mean(xc * xc, axis=-1, keepdims=True)
    return xc * lax.rsqrt(var + LN_EPS) * g + b


def _gelu(x):
    return 0.5 * x * (1.0 + lax.erf(x * (1.0 / math.sqrt(2.0))))


def _silu(x):
    return x * jax.nn.sigmoid(x)


def _softplus(x):
    return jnp.maximum(x, 0.0) + jnp.log1p(jnp.exp(-jnp.abs(x)))


def _causal_mask():
    r = lax.broadcasted_iota(jnp.int32, (CHUNK, CHUNK), 0)
    c = lax.broadcasted_iota(jnp.int32, (CHUNK, CHUNK), 1)
    return r >= c


def _sgu_block(uv, ln_g, ln_b, w, sb):
    gu = _gelu(uv[:, :D_MODEL])
    vn = _layer_norm(_gelu(uv[:, D_MODEL:]), ln_g, ln_b)
    causal = _causal_mask()
    width = D_MODEL // SG_GROUPS
    outs = []
    for g in range(SG_GROUPS):
        wg = jnp.where(causal, w[g], 0.0).astype(BF16)
        mixed = jnp.dot(wg, vn[:, g * width:(g + 1) * width].astype(BF16), preferred_element_type=F32)
        outs.append(mixed + sb[:, g:g + 1])
    return (gu * jnp.concatenate(outs, axis=1),)


GROUP_W = SSM_RPG * SSM_HEADDIM
NT_DIMS = (((1,), (1,)), ((), ()))
TN_DIMS = (((0,), (0,)), ((), ()))


def _mxu(a, b, dims=(((1,), (0,)), ((), ()))):
    return lax.dot_general(a.astype(BF16), b.astype(BF16), dims, preferred_element_type=F32)


def _head_expander():
    return (jnp.arange(SSM_INNER)[None, :] // SSM_HEADDIM == jnp.arange(128)[:, None]).astype(BF16)


def _bf16_terms(x, n):
    terms = []
    for _ in range(n):
        t = x.astype(BF16)
        terms.append(t)
        x = x - t.astype(F32)
    return terms


def _expand_heads(q, e):
    return sum(jnp.dot(t, e, preferred_element_type=F32) for t in _bf16_terms(q, 2))


def _reduce_heads(v, e):
    return sum(lax.dot_general(t, e, NT_DIMS, preferred_element_type=F32) for t in _bf16_terms(v, 2))


def _reduce_heads_of_column_sums(v, e):
    sums = jnp.broadcast_to(jnp.sum(v, axis=0, keepdims=True), (8, v.shape[1]))
    return _reduce_heads(sums, e)[0:1, :]


def _ssd_common(xc, dtraw, dt_bias, a_log, e):
    xs = xc[:, :SSM_INNER]
    pre = dtraw + dt_bias
    dt = _softplus(pre)
    a = -jnp.exp(a_log)
    r_i = lax.broadcasted_iota(jnp.int32, (CHUNK, CHUNK), 0)
    c_i = lax.broadcasted_iota(jnp.int32, (CHUNK, CHUNK), 1)
    tril = jnp.where(r_i >= c_i, 1.0, 0.0).astype(F32)
    cs = jnp.dot(tril, dt * a, precision=HIGHEST, preferred_element_type=F32)
    cs_last = cs[CHUNK - 1:CHUNK, :]
    decay_in = jnp.exp(cs)
    decay_st = jnp.exp(cs_last - cs)
    dt_x = _expand_heads(dt, e)
    w_st_x = _expand_heads(dt * decay_st, e)
    decay_in_x = _expand_heads(decay_in, e)
    return dict(xs=xs, pre=pre, dt=dt, a=a, lower=r_i >= c_i, upper=c_i >= r_i, cs=cs, cs_t=cs.T, decay_in=decay_in,
                decay_st=decay_st, chunk_decay=jnp.exp(cs_last), dt_x=dt_x, w_st_x=w_st_x, decay_in_x=decay_in_x,
                chunk_decay_x=decay_in_x[CHUNK - 1:CHUNK, :], xdt=xs * dt_x, x_st=(xs * w_st_x).astype(BF16),
                low=lax.broadcasted_iota(jnp.int32, (CHUNK, 128), 1) < SSM_HEADDIM)


def _pair_decay(c, h):
    return jnp.exp(jnp.where(c["lower"], c["cs"][:, h:h + 1] - c["cs_t"][h:h + 1, :], -1e30))


def _pair_decay_t(c, h):
    return jnp.exp(jnp.where(c["upper"], c["cs_t"][h:h + 1, :] - c["cs"][:, h:h + 1], -1e30))


def _ssd_forward(xc, dtraw, z, prev, dt_bias, a_log, d_skip_x, norm_g, e):
    c = _ssd_common(xc, dtraw, dt_bias, a_log, e)
    y_groups, new_states = [], []
    for g in range(SSM_GROUPS):
        lanes = slice(g * GROUP_W, (g + 1) * GROUP_W)
        bg = xc[:, SSM_INNER + g * SSM_STATE:SSM_INNER + (g + 1) * SSM_STATE]
        cg = xc[:, SSM_INNER + (SSM_GROUPS + g) * SSM_STATE:SSM_INNER + (SSM_GROUPS + g + 1) * SSM_STATE].astype(BF16)
        pg = prev[g * SSM_STATE:(g + 1) * SSM_STATE, :]
        cb = _mxu(cg, bg, NT_DIMS)
        y_in = _mxu(cg, pg) * c["decay_in_x"][:, lanes]
        new_states.append(pg * c["chunk_decay_x"][:, lanes] + _mxu(bg.T, c["x_st"][:, lanes]))
        pairs = []
        for j in range(SSM_RPG // 2):
            h0 = g * SSM_RPG + 2 * j
            xp = c["xdt"][:, 128 * (h0 // 2):128 * (h0 // 2 + 1)]
            pairs.append(_mxu(cb * _pair_decay(c, h0), jnp.where(c["low"], xp, 0.0))
                         + _mxu(cb * _pair_decay(c, h0 + 1), jnp.where(c["low"], 0.0, xp)))
        y_groups.append(jnp.concatenate(pairs, axis=1) + y_in)
    y_pre = jnp.concatenate(y_groups, axis=1) + c["xs"] * d_skip_x
    gated = y_pre * _silu(z)
    normed = [gated[:, g * GROUP_W:(g + 1) * GROUP_W] for g in range(SSM_GROUPS)]
    normed = [yg * lax.rsqrt(jnp.mean(yg * yg, axis=-1, keepdims=True) + RMS_EPS) for yg in normed]
    return jnp.concatenate(normed, axis=1) * norm_g, y_pre, jnp.concatenate(new_states, axis=0)


def _ssd_backward(xc, dtraw, z, prev, y_pre, dt_bias, a_log, d_skip_x, norm_g, e, dout, dnew):
    c = _ssd_common(xc, dtraw, dt_bias, a_log, e)
    xs = c["xs"]
    sig = jax.nn.sigmoid(z)
    silu_z = z * sig
    gated = y_pre * silu_z
    d_gated, normed = [], []
    for g in range(SSM_GROUPS):
        lanes = slice(g * GROUP_W, (g + 1) * GROUP_W)
        yg = gated[:, lanes]
        r = lax.rsqrt(jnp.mean(yg * yg, axis=-1, keepdims=True) + RMS_EPS)
        n = yg * r
        gh = dout[:, lanes] * norm_g[:, lanes]
        d_gated.append(r * (gh - n * jnp.mean(gh * n, axis=-1, keepdims=True)))
        normed.append(n)
    d_gated = jnp.concatenate(d_gated, axis=1)
    dnorm_g = jnp.sum(dout * jnp.concatenate(normed, axis=1), axis=0, keepdims=True)
    dy = d_gated * silu_z
    dz = d_gated * y_pre * (sig * (1.0 + z * (1.0 - sig)))
    dxs = dy * d_skip_x
    dd_skip = _reduce_heads_of_column_sums(dy * xs, e)

    lane = lax.broadcasted_iota(jnp.int32, (CHUNK, 128), 1)
    sub = lax.broadcasted_iota(jnp.int32, (8, 128), 0)
    dcs_neg = jnp.zeros((CHUNK, 128), F32)
    row_slabs = []
    dxdt, dx_st, d_decay_in_x, dprev, d_chunk_decay_x, db_all, dc_all = [], [], [], [], [], [], []
    for g in range(SSM_GROUPS):
        lanes = slice(g * GROUP_W, (g + 1) * GROUP_W)
        bg = xc[:, SSM_INNER + g * SSM_STATE:SSM_INNER + (g + 1) * SSM_STATE].astype(BF16)
        cg_f = xc[:, SSM_INNER + (SSM_GROUPS + g) * SSM_STATE:SSM_INNER + (SSM_GROUPS + g + 1) * SSM_STATE]
        cg = cg_f.astype(BF16)
        pg = prev[g * SSM_STATE:(g + 1) * SSM_STATE, :]
        dng = dnew[g * SSM_STATE:(g + 1) * SSM_STATE, :]
        dy_g = dy[:, lanes]
        cb_t = _mxu(bg, cg, NT_DIMS)
        t1 = (dy_g * c["decay_in_x"][:, lanes]).astype(BF16)
        d_decay_in_x.append(dy_g * _mxu(cg, pg))
        dc = _mxu(t1, pg, NT_DIMS)
        dprev.append(_mxu(cg_f.T, t1) + dng * c["chunk_decay_x"][:, lanes])
        d_chunk_decay_x.append(dng * pg)
        db = _mxu(c["x_st"][:, lanes], dng, NT_DIMS)
        dx_st.append(_mxu(bg, dng))
        dcb_t = jnp.zeros((CHUNK, CHUNK), F32)
        rows = []
        for j in range(SSM_RPG // 2):
            h0 = g * SSM_RPG + 2 * j
            blk = slice(128 * (h0 // 2), 128 * (h0 // 2 + 1))
            xp = c["xdt"][:, blk]
            dyp = dy[:, blk].astype(BF16)
            pair_dx = []
            for k, xk in enumerate((jnp.where(c["low"], xp, 0.0), jnp.where(c["low"], 0.0, xp))):
                dec_t = _pair_decay_t(c, h0 + k)
                pair_dx.append(_mxu(cb_t * dec_t, dyp))
                dml_t = _mxu(xk, dyp, NT_DIMS) * dec_t
                dcb_t = dcb_t + dml_t
                dseg_t = dml_t * cb_t
                dcs_neg = dcs_neg + jnp.where(lane == h0 + k, jnp.sum(dseg_t, axis=-1, keepdims=True), 0.0)
                rows.append(jnp.sum(dseg_t, axis=0, keepdims=True))
            dxdt.append(jnp.where(c["low"], pair_dx[0], pair_dx[1]))
        slab = jnp.zeros((8, 128), F32)
        for r in range(SSM_RPG):
            slab = slab + jnp.where(sub == r, rows[r], 0.0)
        row_slabs.append(slab)
        dc_all.append(dc + _mxu(dcb_t.T, bg))
        db_all.append(db + _mxu(dcb_t, cg))
    dxdt = jnp.concatenate(dxdt, axis=1)
    dx_st = jnp.concatenate(dx_st, axis=1)
    by_head = jnp.concatenate(row_slabs + [jnp.zeros((CHUNK - SSM_HEADS, 128), F32)], axis=0)
    dcs = by_head.T - dcs_neg
    dxs = dxs + dxdt * c["dt_x"] + dx_st * c["w_st_x"]
    ddt = _reduce_heads(dxdt * xs, e)
    dw_st = _reduce_heads(dx_st * xs, e)
    dcs = dcs + _reduce_heads(jnp.concatenate(d_decay_in_x, axis=1), e) * c["decay_in"]
    ddt = ddt + dw_st * c["decay_st"]
    d_log_st = dw_st * c["dt"] * c["decay_st"]
    dcs = dcs - d_log_st
    d_chunk_decay = _reduce_heads_of_column_sums(jnp.concatenate(d_chunk_decay_x, axis=1), e)
    dcs_last = jnp.sum(d_log_st, axis=0, keepdims=True) + d_chunk_decay * c["chunk_decay"]
    row = lax.broadcasted_iota(jnp.int32, (CHUNK, 128), 0)
    dcs = dcs + jnp.where(row == CHUNK - 1, dcs_last, 0.0)
    triu = jnp.where(c["upper"], 1.0, 0.0).astype(F32)
    dda = jnp.dot(triu, dcs, precision=HIGHEST, preferred_element_type=F32)
    ddt = ddt + dda * c["a"]
    da_log = jnp.sum(dda * c["dt"], axis=0, keepdims=True) * c["a"]
    dpre = ddt * jax.nn.sigmoid(c["pre"])
    dxc = jnp.concatenate([dxs] + db_all + dc_all, axis=1)
    return (dxc, dpre, dz, jnp.concatenate(dprev, axis=0), jnp.sum(dpre, axis=0, keepdims=True), da_log, dd_skip,
            dnorm_g)


def _conv_block(x, w, b):
    rows = lax.broadcasted_iota(jnp.int32, x.shape, 0)
    acc = x * w[SSM_CONV - 1:SSM_CONV, :] + b
    for k in range(SSM_CONV - 1):
        shift = SSM_CONV - 1 - k
        acc = acc + _shift_rows(x, rows, shift) * w[k:k + 1, :]
    return (_silu(acc),)


@functools.partial(jax.custom_vjp, nondiff_argnums=(2,))
def _shift_rows(x, rows, shift):
    return jnp.where(rows >= shift, pltpu.roll(x, shift, 0), 0.0)


def _shift_rows_fwd(x, rows, shift):
    return _shift_rows(x, rows, shift), rows


def _shift_rows_bwd(shift, rows, g):
    n = g.shape[0]
    return jnp.where(rows < n - shift, pltpu.roll(g, n - shift, 0), 0.0), None


_shift_rows.defvjp(_shift_rows_fwd, _shift_rows_bwd)


def _merge_block(gates, br_a, br_b):
    return (jax.nn.sigmoid(gates[:, :D_MODEL]) * br_a + jax.nn.sigmoid(gates[:, D_MODEL:]) * br_b,)


def _lnres_block(x, y, g, b):
    return (_layer_norm(ALPHA * x + y, g, b),)


def _memln_block(x, g, b):
    return (_layer_norm(x, g, b),)


def _attn_block(q, kv):
    outs = []
    for h in range(X_HEADS):
        qh = q[:, h * X_HEADDIM:(h + 1) * X_HEADDIM].astype(BF16)
        kh = kv[:, h * X_HEADDIM:(h + 1) * X_HEADDIM].astype(BF16)
        vh = kv[:, D_MODEL + h * X_HEADDIM:D_MODEL + (h + 1) * X_HEADDIM].astype(BF16)
        s = lax.dot_general(qh, kh, (((1,), (1,)), ((), ())), preferred_element_type=F32) * (X_HEADDIM ** -0.5)
        s = s - lax.stop_gradient(jnp.max(s, axis=-1, keepdims=True))
        e = jnp.exp(s)
        p = e / jnp.sum(e, axis=-1, keepdims=True)
        outs.append(jnp.dot(p.astype(BF16), vh, preferred_element_type=F32))
    return (jnp.concatenate(outs, axis=1),)


def _swiglu_block(gu):
    return (_silu(gu[:, :FFN_HIDDEN]) * gu[:, FFN_HIDDEN:],)


class _Comm:
    def __init__(self):
        self.gathers = []
        self.scatters = []

    @staticmethod
    def _rows(ref, rows):
        return ref if rows is None else ref.at[pl.ds(rows[0], rows[1])]

    def operands(self):
        ins = [a for a, _, _ in self.gathers] + [a for a, _ in self.scatters]
        shapes = []
        for a, idx, rows in self.gathers:
            blk = a.shape if idx is None else a.shape[1:]
            shapes.append(jax.ShapeDtypeStruct((N_DEV, blk[0] if rows is None else rows[1]) + tuple(blk[1:]), a.dtype))
        for a, rows in self.scatters:
            shapes.append(jax.ShapeDtypeStruct((N_DEV, a.shape[1] if rows is None else rows[1]) + tuple(a.shape[2:]),
                                               a.dtype))
        scratch = []
        for n in (len(self.gathers), len(self.scatters)):
            if n:
                scratch += [pltpu.SemaphoreType.DMA((7 * n,)), pltpu.SemaphoreType.DMA((7 * n,)),
                            pltpu.SemaphoreType.DMA((n,))]
        return ins, shapes, scratch

    def _split(self, in_refs, out_refs, sems):
        ng = len(self.gathers)
        g_sems = sems[:3] if ng else None
        s_sems = sems[3:] if ng else sems
        return in_refs[:ng], in_refs[ng:], out_refs[:ng], out_refs[ng:], g_sems, s_sems

    def _gather_copies(self, i, src_ref, out_ref, sems):
        send_sems, recv_sems, local_sems = sems
        x, y, c = lax.axis_index("x"), lax.axis_index("y"), lax.axis_index("c")
        me, sibling = (x, y, c), (x, y, 1 - c)
        chips = [(1 - x, y), (x, 1 - y), (1 - x, 1 - y)]
        _, idx, rows = self.gathers[i]
        src = self._rows(src_ref if idx is None else src_ref.at[idx], rows)

        def slot(px, py, pc):
            return out_ref.at[4 * px + 2 * py + pc]

        def copy(k, blk, to, from_src=False):
            return pltpu.make_async_remote_copy(
                src_ref=src if from_src else slot(*blk), dst_ref=slot(*blk), send_sem=send_sems.at[7 * i + k],
                recv_sem=recv_sems.at[7 * i + k], device_id=to, device_id_type=pl.DeviceIdType.MESH)

        mine = pltpu.make_async_copy(src, slot(*me), local_sems.at[i])
        first = [copy(0, me, sibling, True)] + [copy(1 + j, me, (*chip, c), True) for j, chip in enumerate(chips)]
        passed = [copy(4 + j, (*chip, c), sibling) for j, chip in enumerate(chips)]
        arrivals = [copy(1 + j, (*chip, c), me) for j, chip in enumerate(chips)]
        from_sibling = [copy(0, sibling, me)] + [copy(4 + j, (*chip, 1 - c), me) for j, chip in enumerate(chips)]
        return mine, first, passed, arrivals, from_sibling

    def _scatter_copies(self, i, src_ref, out_ref, sems):
        send_sems, recv_sems, local_sems = sems
        x, y, c = lax.axis_index("x"), lax.axis_index("y"), lax.axis_index("c")
        me = 4 * x + 2 * y + c
        rows = self.scatters[i][1]
        mine = pltpu.make_async_copy(self._rows(src_ref.at[me], rows), out_ref.at[me], local_sems.at[i])
        copies = []
        for k in range(1, N_DEV):
            px = 1 - x if k & 4 else x
            py = 1 - y if k & 2 else y
            pc = 1 - c if k & 1 else c
            copies.append(pltpu.make_async_remote_copy(
                src_ref=self._rows(src_ref.at[4 * px + 2 * py + pc], rows), dst_ref=out_ref.at[me],
                send_sem=send_sems.at[7 * i + k - 1], recv_sem=recv_sems.at[7 * i + k - 1], device_id=(px, py, pc),
                device_id_type=pl.DeviceIdType.MESH))
        return mine, copies

    def start(self, in_refs, out_refs, sems):
        g_in, s_in, g_out, s_out, g_sems, s_sems = self._split(in_refs, out_refs, sems)
        for i in range(len(self.gathers)):
            mine, first, _, _, _ = self._gather_copies(i, g_in[i], g_out[i], g_sems)
            mine.start()
            for cp in first:
                cp.start()
        for i in range(len(self.scatters)):
            mine, copies = self._scatter_copies(i, s_in[i], s_out[i], s_sems)
            mine.start()
            for cp in copies:
                cp.start()

    def finish(self, in_refs, out_refs, sems):
        g_in, s_in, g_out, s_out, g_sems, s_sems = self._split(in_refs, out_refs, sems)
        parts = [self._gather_copies(i, g_in[i], g_out[i], g_sems) for i in range(len(self.gathers))]
        for j in range(3):
            for _, _, passed, arrivals, _ in parts:
                arrivals[j].wait_recv()
                passed[j].start()
        for mine, first, passed, _, from_sibling in parts:
            for cp in from_sibling:
                cp.wait_recv()
            for cp in first + passed:
                cp.wait_send()
            mine.wait()
        for i in range(len(self.scatters)):
            mine, copies = self._scatter_copies(i, s_in[i], s_out[i], s_sems)
            for cp in copies:
                cp.wait_recv()
            for cp in copies:
                cp.wait_send()
            mine.wait()


def _params(grid):
    return pltpu.CompilerParams(dimension_semantics=("arbitrary",) * len(grid), vmem_limit_bytes=VMEM_LIMIT)


def _call(name, body, *, grid, ins, in_specs, out_shape, out_specs, scratch=(), comm=None, aliases=None):
    n_in, n_out, n_scr = len(ins), len(out_shape), len(scratch)
    aliases = aliases or {}
    if comm is None:
        outs = pl.pallas_call(body, grid=grid, in_specs=list(in_specs), out_specs=list(out_specs),
                              out_shape=list(out_shape), scratch_shapes=list(scratch), name=name,
                              input_output_aliases=aliases, compiler_params=_params(grid))(*ins)
        return list(outs), []
    c_ins, c_shapes, c_scratch = comm.operands()
    nci, nco = len(c_ins), len(c_shapes)
    anywhere = pl.BlockSpec(memory_space=pl.ANY)

    def carrier(*refs):
        main_in, comm_in = refs[:n_in], refs[n_in:n_in + nci]
        o0 = n_in + nci
        main_out, comm_out = refs[o0:o0 + n_out], refs[o0 + n_out:o0 + n_out + nco]
        s0 = o0 + n_out + nco
        main_scr, comm_scr = refs[s0:s0 + n_scr], refs[s0 + n_scr:]
        first = pl.program_id(0) == 0
        last = pl.program_id(0) == grid[0] - 1
        for ax in range(1, len(grid)):
            first = first & (pl.program_id(ax) == 0)
            last = last & (pl.program_id(ax) == grid[ax] - 1)

        @pl.when(first)
        def _():
            comm.start(comm_in, comm_out, comm_scr)

        body(*main_in, *main_out, *main_scr)

        @pl.when(last)
        def _():
            comm.finish(comm_in, comm_out, comm_scr)

    outs = pl.pallas_call(carrier, grid=grid, in_specs=list(in_specs) + [anywhere] * nci,
                          out_specs=list(out_specs) + [anywhere] * nco, out_shape=list(out_shape) + c_shapes,
                          scratch_shapes=list(scratch) + c_scratch, name=name, input_output_aliases=aliases,
                          compiler_params=_params(grid))(*ins, *c_ins)
    return list(outs[:n_out]), list(outs[n_out:])


def _comm_only(name, comm):
    c_ins, c_shapes, c_scratch = comm.operands()
    nci, nco = len(c_ins), len(c_shapes)
    anywhere = pl.BlockSpec(memory_space=pl.ANY)

    def body(*refs):
        comm.start(refs[:nci], refs[nci:nci + nco], refs[nci + nco:])
        comm.finish(refs[:nci], refs[nci:nci + nco], refs[nci + nco:])

    return list(pl.pallas_call(body, in_specs=[anywhere] * nci, out_specs=[anywhere] * nco, out_shape=c_shapes,
                               scratch_shapes=c_scratch, name=name)(*c_ins))


def _stage_fwd(name, f, grid, ins, in_specs, out_shapes, out_specs, comm=None):
    n_in = len(ins)

    def body(*refs):
        res = f(*[r[...].astype(F32) for r in refs[:n_in]])
        for o_ref, val in zip(refs[n_in:], res):
            o_ref[...] = val.astype(o_ref.dtype)

    return _call(name, body, grid=grid, ins=ins, in_specs=in_specs, out_shape=out_shapes, out_specs=out_specs, comm=comm)


def _stage_bwd(name, f, grid, ins, in_specs, cts, ct_specs, grads, comm=None):
    n_in = len(ins)
    flat_cts = [c for group in cts for c in group]
    flat_ct_specs = [s for group, spec in zip(cts, ct_specs) for s in (spec,) * len(group)]
    n_ct = len(flat_cts)
    diff = [g[0] for g in grads]
    buffers = [(k, g[4]) for k, g in enumerate(grads) if len(g) > 4]
    n_buf = len(buffers)

    def body(*refs):
        vals = [r[...].astype(F32) for r in refs[:n_in]]
        ct_refs = refs[n_in:n_in + n_ct]
        g_refs = refs[n_in + n_ct + n_buf:]
        ct_vals, pos = [], 0
        for group in cts:
            acc = ct_refs[pos][...].astype(F32)
            for j in range(1, len(group)):
                acc = acc + ct_refs[pos + j][...].astype(F32)
            ct_vals.append(acc)
            pos += len(group)

        def g_fn(*dvals):
            full = list(vals)
            for i, dv in zip(diff, dvals):
                full[i] = dv
            return f(*full)

        _, vjp = jax.vjp(g_fn, *[vals[i] for i in diff])
        gvals = vjp(tuple(ct_vals))
        for gspec, g_ref, gval in zip(grads, g_refs, gvals):
            acc_axes = gspec[1]
            if not acc_axes:
                g_ref[...] = gval.astype(g_ref.dtype)
            else:
                first = pl.program_id(acc_axes[0]) == 0
                for ax in acc_axes[1:]:
                    first = first & (pl.program_id(ax) == 0)

                @pl.when(first)
                def _():
                    g_ref[...] = jnp.zeros_like(g_ref)

                g_ref[...] += gval.astype(g_ref.dtype)

    out_shapes, out_specs = [], []
    for gspec in grads:
        shape, spec = gspec[3] if len(gspec) > 3 else (ins[gspec[0]].shape, in_specs[gspec[0]])
        out_shapes.append(jax.ShapeDtypeStruct(shape, gspec[2]))
        out_specs.append(spec)
    anywhere = pl.BlockSpec(memory_space=pl.ANY)
    return _call(name, body, grid=grid, ins=list(ins) + flat_cts + [b for _, b in buffers],
                 in_specs=list(in_specs) + flat_ct_specs + [anywhere] * n_buf, out_shape=out_shapes, out_specs=out_specs,
                 comm=comm, aliases={n_in + n_ct + j: k for j, (k, _) in enumerate(buffers)})


def _pick_tile(n, candidates):
    for c in candidates:
        if n % c == 0:
            return c
    return n


def _matmul(name, a, b, *, ta=False, tb=False, add=None, out_dtype=F32, comm=None):
    if ta:
        k_dim, m = a.shape
    else:
        m, k_dim = a.shape
    n = b.shape[0] if tb else b.shape[1]
    assert (b.shape[1] if tb else b.shape[0]) == k_dim and not (ta and tb)
    tm = _pick_tile(m, (1024, 1408, 512, 256, 128))
    tn = _pick_tile(n, (1024, 1408, 512, 256, 128))
    if ta:
        tk = _pick_tile(k_dim, (1024, 512, 256, 128))
    elif k_dim <= 2816:
        tk = k_dim
    else:
        tk = _pick_tile(k_dim, (1408, 1024, 512, 256, 128))
    nk = k_dim // tk
    grid = (m // tm, n // tn, nk)
    a_spec = pl.BlockSpec((tk, tm), lambda i, j, k: (k, i)) if ta else pl.BlockSpec((tm, tk), lambda i, j, k: (i, k))
    b_spec = pl.BlockSpec((tn, tk), lambda i, j, k: (j, k)) if tb else pl.BlockSpec((tk, tn), lambda i, j, k: (k, j))
    o_spec = pl.BlockSpec((tm, tn), lambda i, j, k: (i, j))
    dims = (((0 if ta else 1,), (1 if tb else 0,)), ((), ()))
    has_add = add is not None

    def body(*refs):
        a_ref, b_ref = refs[0], refs[1]
        add_ref = refs[2] if has_add else None
        o_ref, acc_ref = refs[-2], refs[-1]
        k = pl.program_id(2)
        part = lax.dot_general(a_ref[...].astype(BF16), b_ref[...].astype(BF16), dims, preferred_element_type=F32)

        def finish(res):
            if has_add:
                res = res + add_ref[...].astype(F32)
            o_ref[...] = res.astype(o_ref.dtype)

        if nk == 1:
            finish(part)
        else:
            @pl.when(k == 0)
            def _():
                acc_ref[...] = part

            @pl.when((k > 0) & (k < nk - 1))
            def _():
                acc_ref[...] += part

            @pl.when(k == nk - 1)
            def _():
                finish(acc_ref[...] + part)

    ins = [a, b] + ([add] if has_add else [])
    in_specs = [a_spec, b_spec] + ([o_spec] if has_add else [])
    acc_shape = (tm, tn) if nk > 1 else (8, 128)
    outs, comm_outs = _call(name, body, grid=grid, ins=ins, in_specs=in_specs,
                            out_shape=[jax.ShapeDtypeStruct((m, n), out_dtype)], out_specs=[o_spec],
                            scratch=[pltpu.VMEM(acc_shape, F32)], comm=comm)
    return outs[0], comm_outs


def _matmul_lnres(name, a, b, x, g, beta):
    m, k_dim = a.shape
    n = b.shape[1]
    tm = _pick_tile(m, (512, 256, 128))
    row = lambda w: pl.BlockSpec((tm, w), lambda i: (i, 0))
    whole = lambda shape: pl.BlockSpec(shape, lambda i: (0, 0))

    def body(a_ref, b_ref, x_ref, g_ref, beta_ref, y_ref, h_ref, hb_ref):
        y = jnp.dot(a_ref[...].astype(BF16), b_ref[...].astype(BF16), preferred_element_type=F32).astype(y_ref.dtype)
        y_ref[...] = y
        (h,) = _lnres_block(x_ref[...], y.astype(F32), g_ref[...], beta_ref[...])
        h_ref[...] = h
        hb_ref[...] = h.astype(hb_ref.dtype)

    sds = jax.ShapeDtypeStruct
    return _call(name, body, grid=(m // tm,), ins=[a, b, x, g, beta],
                 in_specs=[row(k_dim), whole((k_dim, n)), row(n), whole((1, n)), whole((1, n))],
                 out_shape=[sds((m, n), BF16), sds((m, n), F32), sds((m, n), BF16)], out_specs=[row(n)] * 3)[0]


SSD_STATE = (SSM_GROUPS * SSM_STATE, SSM_RPG * SSM_HEADDIM)


def _ssd_fwd(xc, dt_raw, proj, dt_bias, a_log, d_skip, norm_g, nb, nc, comm=None):
    t = xc.shape[0]
    row = lambda b, c: (b * nc + c, 0)
    par = lambda shape: pl.BlockSpec(shape, lambda b, c: (0, 0))

    def body(xc_ref, dt_ref, z_ref, dtb_ref, al_ref, ds_ref, ng_ref, e_ref, y_ref, ypre_ref, prev_ref, st_ref):
        @pl.when(pl.program_id(1) == 0)
        def _():
            st_ref[...] = jnp.zeros_like(st_ref)

        prev = st_ref[...]
        prev_ref[0, 0] = prev
        y, y_pre, new_state = _ssd_forward(xc_ref[...], dt_ref[...], z_ref[...].astype(F32), prev, dtb_ref[...],
                                           al_ref[...], ds_ref[...], ng_ref[...], e_ref[...])
        y_ref[...] = y.astype(y_ref.dtype)
        ypre_ref[...] = y_pre
        st_ref[...] = new_state

    return _call(
        "ssd_fwd", body, grid=(nb, nc), ins=[xc, dt_raw, proj, dt_bias, a_log, d_skip, norm_g, _head_expander()],
        in_specs=[pl.BlockSpec((CHUNK, SSM_CONV_DIM), row), pl.BlockSpec((CHUNK, 128), row),
                  pl.BlockSpec((CHUNK, SSM_INNER), lambda b, c: (b * nc + c, 1)),
                  par((1, 128)), par((1, 128)), par((1, SSM_INNER)), par((1, SSM_INNER)), par((128, SSM_INNER))],
        out_specs=[pl.BlockSpec((CHUNK, SSM_INNER), row), pl.BlockSpec((CHUNK, SSM_INNER), row),
                   pl.BlockSpec((1, 1) + SSD_STATE, lambda b, c: (b, c, 0, 0))],
        out_shape=[jax.ShapeDtypeStruct((t, SSM_INNER), BF16), jax.ShapeDtypeStruct((t, SSM_INNER), F32),
                   jax.ShapeDtypeStruct((nb, nc) + SSD_STATE, F32)],
        scratch=[pltpu.VMEM(SSD_STATE, F32)], comm=comm)


def _ssd_bwd(xc, dt_raw, proj, prevs, y_pre, dt_bias, a_log, d_skip, norm_g, dy, dproj, nb, nc, comm=None):
    t = xc.shape[0]
    row = lambda b, c: (b * nc + (nc - 1 - c), 0)
    par = lambda shape: pl.BlockSpec(shape, lambda b, c: (0, 0))
    z_spec = pl.BlockSpec((CHUNK, SSM_INNER), lambda b, c: (b * nc + (nc - 1 - c), 1))

    def body(xc_ref, dt_ref, z_ref, prev_ref, ypre_ref, dtb_ref, al_ref, ds_ref, ng_ref, e_ref, dy_ref, _,
             dxc_ref, ddt_ref, dz_ref, ddtb_ref, dal_ref, dds_ref, dng_ref, dst_ref):
        @pl.when(pl.program_id(1) == 0)
        def _():
            dst_ref[...] = jnp.zeros_like(dst_ref)

        @pl.when((pl.program_id(0) == 0) & (pl.program_id(1) == 0))
        def _():
            ddtb_ref[...] = jnp.zeros_like(ddtb_ref)
            dal_ref[...] = jnp.zeros_like(dal_ref)
            dds_ref[...] = jnp.zeros_like(dds_ref)
            dng_ref[...] = jnp.zeros_like(dng_ref)

        dxc, ddt, dz, dprev, ddtb, dal, dds, dng = _ssd_backward(
            xc_ref[...], dt_ref[...], z_ref[...].astype(F32), prev_ref[0, 0], ypre_ref[...], dtb_ref[...], al_ref[...],
            ds_ref[...], ng_ref[...], e_ref[...], dy_ref[...].astype(F32), dst_ref[...])
        dxc_ref[...] = dxc
        ddt_ref[...] = ddt.astype(ddt_ref.dtype)
        dz_ref[...] = dz.astype(dz_ref.dtype)
        dst_ref[...] = dprev
        ddtb_ref[...] += ddtb
        dal_ref[...] += dal
        dds_ref[...] += dds
        dng_ref[...] += dng

    return _call(
        "ssd_bwd", body, grid=(nb, nc),
        ins=[xc, dt_raw, proj, prevs, y_pre, dt_bias, a_log, d_skip, norm_g, _head_expander(), dy, dproj],
        in_specs=[pl.BlockSpec((CHUNK, SSM_CONV_DIM), row), pl.BlockSpec((CHUNK, DT_LANES), row), z_spec,
                  pl.BlockSpec((1, 1) + SSD_STATE, lambda b, c: (b, nc - 1 - c, 0, 0)),
                  pl.BlockSpec((CHUNK, SSM_INNER), row),
                  par((1, 128)), par((1, 128)), par((1, SSM_INNER)), par((1, SSM_INNER)), par((128, SSM_INNER)),
                  pl.BlockSpec((CHUNK, SSM_INNER), row), pl.BlockSpec(memory_space=pl.ANY)],
        out_specs=[pl.BlockSpec((CHUNK, SSM_CONV_DIM), row), pl.BlockSpec((CHUNK, DT_LANES), row), z_spec,
                   par((1, 128)), par((1, 128)), par((1, 128)), par((1, SSM_INNER))],
        out_shape=[jax.ShapeDtypeStruct((t, SSM_CONV_DIM), F32), jax.ShapeDtypeStruct((t, DT_LANES), BF16),
                   jax.ShapeDtypeStruct(dproj.shape, dproj.dtype), jax.ShapeDtypeStruct((1, 128), F32),
                   jax.ShapeDtypeStruct((1, 128), F32), jax.ShapeDtypeStruct((1, 128), F32),
                   jax.ShapeDtypeStruct((1, SSM_INNER), F32)],
        scratch=[pltpu.VMEM(SSD_STATE, F32)], comm=comm, aliases={11: 2})


def _loss_head(y, target):
    t, d = y.shape
    tm = _pick_tile(t, (256,))
    blk = pl.BlockSpec((tm, d), lambda i: (i, 0))

    def body(y_ref, t_ref, loss_ref, dy_ref):
        err = y_ref[...] - t_ref[...]
        dy_ref[...] = err * (1.0 / d)

        @pl.when(pl.program_id(0) == 0)
        def _():
            loss_ref[...] = jnp.zeros_like(loss_ref)

        loss_ref[...] += 0.5 * jnp.sum(jnp.mean(err * err, axis=-1, keepdims=True), axis=0, keepdims=True)

    return _call("loss_head", body, grid=(t // tm,), ins=[y, target], in_specs=[blk, blk],
                 out_specs=[pl.BlockSpec((1, 1), lambda i: (0, 0)), blk],
                 out_shape=[jax.ShapeDtypeStruct((1, 1), F32), jax.ShapeDtypeStruct((t, d), F32)])[0]


def _adamw_math(g, w, m, v):
    m_new = ADAM_B1 * m + (1.0 - ADAM_B1) * g
    v_new = ADAM_B2 * v + (1.0 - ADAM_B2) * jnp.square(g)
    m_hat = m_new / (1.0 - ADAM_B1 ** ADAM_STEP)
    v_hat = v_new / (1.0 - ADAM_B2 ** ADAM_STEP)
    delta = -ADAM_LR * (m_hat / (jnp.sqrt(v_hat) + ADAM_EPS) + ADAM_WD * w)
    return delta, m_new, v_new


def _adamw_sharded(name, parts, w, m, v):
    _, a, b = w.shape
    tr = _pick_tile(a, (128,))
    nt = a // tr
    part_specs = [pl.BlockSpec((N_DEV, tr, b),
                               (lambda l, i, _k=k: (0, jnp.where(l == _k, i, jnp.where(l > _k, nt - 1, 0)), 0)))
                  for k in range(DEPTH)]
    blk = pl.BlockSpec((1, tr, b), lambda l, i: (l, i, 0))

    def body(*refs):
        p_refs = refs[:DEPTH]
        w_ref, m_ref, v_ref, g_out, d_out, m_out, v_out = refs[DEPTH:]
        for k in range(DEPTH):
            @pl.when(pl.program_id(0) == k)
            def _(p_ref=p_refs[k]):
                g = p_ref[0].astype(F32)
                for p in range(1, N_DEV):
                    g = g + p_ref[p].astype(F32)
                delta, m_new, v_new = _adamw_math(g, w_ref[0], m_ref[0], v_ref[0])
                g_out[0] = g
                d_out[0] = delta
                m_out[0] = m_new
                v_out[0] = v_new

    return _call(name, body, grid=(DEPTH, nt), ins=list(parts) + [w, m, v], in_specs=part_specs + [blk, blk, blk],
                 out_specs=[blk] * 4, out_shape=[jax.ShapeDtypeStruct(w.shape, F32)] * 4)[0]


def _adamw_small(name, g, w, m, v):
    full = pl.BlockSpec(w.shape, lambda i: (0, 0))

    def body(g_ref, w_ref, m_ref, v_ref, d_out, m_out, v_out):
        delta, m_new, v_new = _adamw_math(g_ref[...], w_ref[...], m_ref[...], v_ref[...])
        d_out[...] = delta
        m_out[...] = m_new
        v_out[...] = v_new

    return _call(name, body, grid=(1,), ins=[g, w, m, v], in_specs=[full] * 4, out_specs=[full] * 3,
                 out_shape=[jax.ShapeDtypeStruct(w.shape, F32)] * 3)[0]


def _sum_parts(name, parts):
    n_parts, rows, cols = parts.shape
    tr = _pick_tile(rows, (512, 256, 128, 64, 32, 16, 8))

    def body(p_ref, o_ref):
        acc = p_ref[0]
        for p in range(1, n_parts):
            acc = acc + p_ref[p]
        o_ref[...] = acc

    return _call(name, body, grid=(rows // tr,), ins=[parts],
                 in_specs=[pl.BlockSpec((n_parts, tr, cols), lambda i: (0, i, 0))],
                 out_specs=[pl.BlockSpec((tr, cols), lambda i: (i, 0))],
                 out_shape=[jax.ShapeDtypeStruct((rows, cols), parts.dtype)])[0][0]


def _pack_w_in(w):
    main = jnp.concatenate([w[:, :XBC_COL], w[:, GA_COL:], w[:, XBC_COL:DT_COL]], axis=1)
    return main, jnp.pad(w[:, DT_COL:GA_COL], ((0, 0), (0, DT_LANES - SSM_HEADS)))


def _unpack_w_in(main, dt):
    return jnp.concatenate([main[:, :P_GATE], main[:, P_XBC:], dt[:, :SSM_HEADS], main[:, P_GATE:P_XBC]], axis=1)


def _pad_heads(v):
    return jnp.pad(v, (0, 128 - SSM_HEADS)).reshape(1, 128)


def _run_step(x, mem, target, small, ex):
    nb, s, d = x.shape
    t = nb * s
    nc = s // CHUNK
    rows = _pick_tile(t, (256,))
    rows_wide = _pick_tile(t, (512, 256))
    tq = _pick_tile(s, (512, 256))
    vec = lambda a: a.reshape(1, -1)
    full1 = lambda shape: pl.BlockSpec(shape, lambda i: (0,) * len(shape))
    row1 = lambda tm, w: pl.BlockSpec((tm, w), lambda i: (i, 0))
    sds = jax.ShapeDtypeStruct

    def mm(call, l, a, b, **kw):
        comm = ex.before(call, l)
        out, comm_outs = _matmul(call, a, b, comm=comm, **kw)
        if comm is not None:
            ex.after(call, l, comm_outs)
        return out

    def stage_bwd(call, l, *args):
        comm = ex.before(call, l)
        outs, comm_outs = _stage_bwd(call, *args, comm=comm)
        if comm is not None:
            ex.after(call, l, comm_outs)
        return outs

    def stage_fwd(call, l, *args):
        comm = ex.before(call, l)
        outs, comm_outs = _stage_fwd(call, *args, comm=comm)
        if comm is not None:
            ex.after(call, l, comm_outs)
        return outs

    mem_specs = [row1(256, d), full1((1, d)), full1((1, d))]
    mem_ins = [mem.reshape(nb * MEM_LEN, d), vec(small["mem_ln_g"]), vec(small["mem_ln_b"])]
    (mem_n,) = stage_fwd("memln_fwd", 0, _memln_block, (nb * MEM_LEN // 256,), mem_ins, mem_specs,
                          [sds((nb * MEM_LEN, d), BF16)], [row1(256, d)])

    h = x.reshape(t, d)
    h_bf = h.astype(BF16)
    ln_specs = [row1(rows, d), row1(rows, d), full1((1, d)), full1((1, d))]
    saved = []
    for l in range(DEPTH):
        sv = {"h_bf": h_bf}
        w_p, w_dt = ex.weight("w_in", l)
        proj = mm("mm_in", l, h_bf, w_p, out_dtype=BF16)
        dt_raw = mm("mm_dt", l, h_bf, w_dt)
        sv["proj"] = proj
        sgu_ins = [proj, vec(small["sg_ln_g"][l]), vec(small["sg_ln_b"][l]), small["sg_w"][l], small["sg_b"][l].T]
        sgu_specs = [pl.BlockSpec((CHUNK, 2 * d), lambda i: (i, 0)), full1((1, d)), full1((1, d)),
                     full1((SG_GROUPS, CHUNK, CHUNK)), full1((CHUNK, SG_GROUPS))]
        (a_out,) = stage_fwd("sgu_fwd", l, _sgu_block, (t // CHUNK,), sgu_ins, sgu_specs, [sds((t, d), BF16)],
                              [row1(CHUNK, d)])
        sv["sgu"] = (sgu_ins, sgu_specs)
        sv["a_out"] = a_out
        cw = 256
        conv_ins = [proj, small["conv_w"][l], vec(small["conv_b"][l])]
        conv_specs = [pl.BlockSpec((s, cw), lambda j, b: (b, P_XBC // cw + j)),
                      pl.BlockSpec((SSM_CONV, cw), lambda j, b: (0, j)), pl.BlockSpec((1, cw), lambda j, b: (0, j))]
        conv_out_spec = pl.BlockSpec((s, cw), lambda j, b: (b, j))
        (xc,) = stage_fwd("conv_fwd", l, _conv_block, (SSM_CONV_DIM // cw, nb), conv_ins, conv_specs,
                           [sds((t, SSM_CONV_DIM), F32)], [conv_out_spec])
        sv["conv"] = (conv_ins, conv_specs, conv_out_spec)
        ssd_par = [_pad_heads(small["dt_bias"][l]), _pad_heads(small["a_log"][l]),
                   vec(jnp.repeat(small["d_skip"][l], SSM_HEADDIM)), vec(small["ssm_norm_g"][l])]
        comm = ex.before("ssd_fwd", l)
        (y_ssd, y_pre, prevs), comm_outs = _ssd_fwd(xc, dt_raw, proj, *ssd_par, nb, nc, comm=comm)
        if comm is not None:
            ex.after("ssd_fwd", l, comm_outs)
        sv["ssd"] = (xc, dt_raw, prevs, y_pre, ssd_par)
        sv["y_ssd"] = y_ssd
        br_a = mm("mm_sq", l, a_out, ex.weight("p_a", l), out_dtype=BF16)
        br_b = mm("mm_pb", l, y_ssd, ex.weight("p_b", l), out_dtype=BF16)
        merge_ins = [proj, br_a, br_b]
        merge_out_spec = row1(rows_wide, d)
        merge_specs = [pl.BlockSpec((rows_wide, 2 * d), lambda i: (i, P_GATE // (2 * d))), merge_out_spec, merge_out_spec]
        (merged,) = stage_fwd("merge_fwd", l, _merge_block, (t // rows_wide,), merge_ins, merge_specs,
                               [sds((t, d), BF16)], [merge_out_spec])
        sv["merge"] = (merge_ins, merge_specs, merge_out_spec)
        sv["merged"] = merged
        ln_par = [(vec(small["ln_g"][l, k]), vec(small["ln_b"][l, k])) for k in range(3)]
        y1, h1, h1_bf = _matmul_lnres("mm_sq_ln", merged, ex.weight("w_mix_o", l), h, *ln_par[0])
        sv["ln1"] = [h, y1, *ln_par[0]]
        q = mm("mm_sq", l, h1_bf, ex.weight("w_xq", l), out_dtype=BF16)
        kv = mm("mm_kv", l, mem_n, ex.weight("w_xkv", l), out_dtype=BF16)
        attn_ins = [q, kv]
        attn_out_spec = pl.BlockSpec((tq, d), lambda b, i: (b * (s // tq) + i, 0))
        attn_specs = [attn_out_spec, pl.BlockSpec((MEM_LEN, 2 * d), lambda b, i: (b, 0))]
        (o,) = stage_fwd("attn_fwd", l, _attn_block, (nb, s // tq), attn_ins, attn_specs, [sds((t, d), BF16)],
                          [attn_out_spec])
        sv["attn"] = (attn_ins, attn_specs, attn_out_spec)
        sv["o"] = o
        sv["h1_bf"] = h1_bf
        y2, h2, h2_bf = _matmul_lnres("mm_sq_ln", o, ex.weight("w_xo", l), h1, *ln_par[1])
        sv["ln2"] = [h1, y2, *ln_par[1]]
        sv["h2_bf"] = h2_bf
        gu = mm("mm_ffn_in", l, h2_bf, ex.weight("w_ffn_in", l), out_dtype=BF16)
        (act,) = stage_fwd("swiglu_fwd", l, _swiglu_block, (t // 128,), [gu], [row1(128, 2 * FFN_HIDDEN)],
                            [sds((t, FFN_HIDDEN), BF16)], [row1(128, FFN_HIDDEN)])
        sv["gu"] = gu
        sv["act"] = act
        y3, h3, h3_bf = _matmul_lnres("mm_ffn_out_ln", act, ex.weight("w_ffn_out", l), h2, *ln_par[2])
        sv["ln3"] = [h2, y3, *ln_par[2]]
        h, h_bf = h3, h3_bf
        saved.append(sv)

    loss, dh = _loss_head(h, target.reshape(t, d))

    g_small = {n: [None] * DEPTH for n in SMALL_REP + SMALL_SH if n not in ("mem_ln_g", "mem_ln_b")}
    dmem_n = []
    ln_grads = [(0, (), F32), (1, (), BF16), (2, (0,), F32), (3, (0,), F32)]
    for l in reversed(range(DEPTH)):
        sv = saved[l]
        dln_g, dln_b = [None] * 3, [None] * 3
        dres, dy3, dln_g[2], dln_b[2] = stage_bwd("lnres_bwd", l, _lnres_block, (t // rows,), sv["ln3"], ln_specs,
                                                  [(dh,)], [row1(rows, d)], ln_grads)
        ex.grad("w_ffn_out", l, mm("mm_ffn_out_dw", l, sv["act"], dy3, ta=True, out_dtype=BF16))
        dact = mm("mm_ffn_out_dx", l, dy3, ex.weight("w_ffn_out", l), tb=True, out_dtype=BF16)
        (dgu,) = stage_bwd("swiglu_bwd", l, _swiglu_block, (t // 128,), [sv["gu"]], [row1(128, 2 * FFN_HIDDEN)],
                           [(dact,)], [row1(128, FFN_HIDDEN)], [(0, (), BF16)])
        ex.grad("w_ffn_in", l, mm("mm_ffn_in_dw", l, sv["h2_bf"], dgu, ta=True, out_dtype=BF16))
        dh2 = mm("mm_ffn_in_dx", l, dgu, ex.weight("w_ffn_in", l), tb=True, add=dres)
        dres, dy2, dln_g[1], dln_b[1] = stage_bwd("lnres_bwd", l, _lnres_block, (t // rows,), sv["ln2"], ln_specs,
                                                  [(dh2,)], [row1(rows, d)], ln_grads)
        ex.grad("w_xo", l, mm("mm_sq_dw", l, sv["o"], dy2, ta=True, out_dtype=BF16))
        do = mm("mm_sq_dx", l, dy2, ex.weight("w_xo", l), tb=True, out_dtype=BF16)
        attn_ins, attn_specs, attn_out_spec = sv["attn"]
        dq, dkv = stage_bwd("attn_bwd", l, _attn_block, (nb, s // tq), attn_ins, attn_specs, [(do,)], [attn_out_spec],
                            [(0, (), BF16), (1, (1,), F32)])
        ex.grad("w_xq", l, mm("mm_sq_dw", l, sv["h1_bf"], dq, ta=True, out_dtype=BF16))
        dh1 = mm("mm_sq_dx", l, dq, ex.weight("w_xq", l), tb=True, add=dres)
        ex.grad("w_xkv", l, mm("mm_kv_dw", l, mem_n, dkv, ta=True, out_dtype=BF16))
        dmem_n.append(mm("mm_kv_dx", l, dkv, ex.weight("w_xkv", l), tb=True))
        dres, dy1, dln_g[0], dln_b[0] = stage_bwd("lnres_bwd", l, _lnres_block, (t // rows,), sv["ln1"], ln_specs,
                                                  [(dh1,)], [row1(rows, d)], ln_grads)
        g_small["ln_g"][l] = jnp.concatenate(dln_g, axis=0)
        g_small["ln_b"][l] = jnp.concatenate(dln_b, axis=0)
        ex.grad("w_mix_o", l, mm("mm_sq_dw", l, sv["merged"], dy1, ta=True, out_dtype=BF16))
        dmerged = mm("mm_sq_dx", l, dy1, ex.weight("w_mix_o", l), tb=True, out_dtype=BF16)
        merge_ins, merge_specs, merge_out_spec = sv["merge"]
        dproj, dbr_a, dbr_b = stage_bwd("merge_bwd", l, _merge_block, (t // rows_wide,), merge_ins, merge_specs,
                                        [(dmerged,)], [merge_out_spec],
                                        [(0, (), BF16, ((t, P_COLS), merge_specs[0])), (1, (), BF16), (2, (), BF16)])
        ex.grad("p_a", l, mm("mm_sq_dw", l, sv["a_out"], dbr_a, ta=True, out_dtype=BF16))
        da_out = mm("mm_sq_dx", l, dbr_a, ex.weight("p_a", l), tb=True, out_dtype=BF16)
        ex.grad("p_b", l, mm("mm_pb_dw", l, sv["y_ssd"], dbr_b, ta=True, out_dtype=BF16))
        dy_ssd = mm("mm_pb_dx", l, dbr_b, ex.weight("p_b", l), tb=True, out_dtype=BF16)
        sgu_ins, sgu_specs = sv["sgu"]
        dproj, dsg_ln_g, dsg_ln_b, dsg_w, dsg_b = stage_bwd(
            "sgu_bwd", l, _sgu_block, (t // CHUNK,), sgu_ins, sgu_specs, [(da_out,)], [row1(CHUNK, d)],
            [(0, (), BF16, ((t, P_COLS), sgu_specs[0]), dproj), (1, (0,), F32), (2, (0,), F32), (3, (0,), F32),
             (4, (0,), F32)])
        g_small["sg_ln_g"][l], g_small["sg_ln_b"][l], g_small["sg_w"][l], g_small["sg_b"][l] = (
            dsg_ln_g[0], dsg_ln_b[0], dsg_w, dsg_b.T)
        xc, dt_raw, prevs, y_pre, ssd_par = sv["ssd"]
        comm = ex.before("ssd_bwd", l)
        (dxc, ddt, dproj, ddtb, dal, dds, dng), comm_outs = _ssd_bwd(xc, dt_raw, sv["proj"], prevs, y_pre, *ssd_par,
                                                                     dy_ssd, dproj, nb, nc, comm=comm)
        if comm is not None:
            ex.after("ssd_bwd", l, comm_outs)
        g_small["dt_bias"][l], g_small["a_log"][l], g_small["d_skip"][l] = (
            ddtb[0, :SSM_HEADS], dal[0, :SSM_HEADS], dds[0, :SSM_HEADS])
        g_small["ssm_norm_g"][l] = dng[0]
        conv_ins, conv_specs, conv_out_spec = sv["conv"]
        dproj, dconv_w, dconv_b = stage_bwd("conv_bwd", l, _conv_block, (SSM_CONV_DIM // 256, nb), conv_ins, conv_specs,
                                            [(dxc,)], [conv_out_spec],
                                            [(0, (), BF16, ((t, P_COLS), conv_specs[0]), dproj), (1, (1,), F32),
                                             (2, (1,), F32)])
        g_small["conv_w"][l], g_small["conv_b"][l] = dconv_w, dconv_b[0]
        if l == 0:
            dmg, dmb = stage_bwd("memln_bwd", l, _memln_block, (nb * MEM_LEN // 256,), mem_ins, mem_specs,
                                 [tuple(dmem_n)], [row1(256, d)], [(1, (0,), F32), (2, (0,), F32)])
            done = {n: jnp.stack(g, axis=0) for n, g in g_small.items()}
            done["mem_ln_g"], done["mem_ln_b"] = dmg[0], dmb[0]
            ex.small_grads(done)
        w_p, w_dt = ex.weight("w_in", l)
        g_dt = mm("mm_dt_dw", l, sv["h_bf"], ddt, ta=True, out_dtype=BF16)
        for k, call in enumerate(("mm_in_dw_a", "mm_in_dw_b")):
            rows_k = slice(k * d // 2, (k + 1) * d // 2)
            g_main = mm(call, l, sv["h_bf"][:, rows_k], dproj, ta=True, out_dtype=BF16)
            ex.grad(("w_in", k), l, _unpack_w_in(g_main, g_dt[rows_k]))
        dh = mm("mm_in_dx", l, dproj, w_p, tb=True, add=mm("mm_dt_dx", l, ddt, w_dt, tb=True, add=dres))

    return loss, dh.reshape(nb, s, d)


def _pack_flat(arrays, rows):
    flat = jnp.concatenate([a.reshape(-1) for a in arrays])
    return jnp.pad(flat, (0, rows * 128 - flat.shape[0])).reshape(rows, 128)


def _unpack_flat(packed, shapes):
    lead = packed.shape[:-2]
    flat = packed.reshape(lead + (-1,))
    out, pos = [], 0
    for shape in shapes:
        n = math.prod(shape)
        out.append(flat[..., pos:pos + n].reshape(lead + tuple(shape)))
        pos += n
    return out


def _small_rows(n_elems):
    return -(-n_elems // (128 * SMALL_ROW_TILE)) * SMALL_ROW_TILE


def _from_shards(name, gathered):
    _, a, b = gathered.shape
    if name in BIG_COL_SHARDED:
        w = gathered.transpose(1, 0, 2).reshape(a, N_DEV * b)
        return _pack_w_in(w) if name == "w_in" else w
    return gathered.reshape(N_DEV * a, b)


def _to_shards(name, g):
    if name in BIG_COL_SHARDED:
        a, nb = g.shape
        return g.reshape(a, N_DEV, nb // N_DEV).transpose(1, 0, 2)
    a, b = g.shape
    return g.reshape(N_DEV, a // N_DEV, b)


class _MeshExchange:
    def __init__(self, shards_bf16, first):
        self.shards = shards_bf16
        self.full = dict(first)
        self.pieces = {}
        self.grads = {}
        self.to_send = {}
        self.received = {}
        self.small = None
        self.small_gathered = None

    def weight(self, name, l):
        if (name, l) not in self.full:
            got = jnp.concatenate([self.pieces[(name, l, q)] for q in range(W_IN_PIECES)], axis=1)
            self.full[(name, l)] = _from_shards(name, got)
        return self.full[(name, l)]

    def grad(self, name, l, g):
        self.grads[(name, l)] = g

    def small_grads(self, done):
        self.small = done

    def partial_sums(self, name, l):
        if (name, l, None) in self.received:
            return self.received[(name, l, None)]
        return jnp.concatenate([self.received[(name, l, q)] for q in range(W_IN_PIECES)], axis=1)

    def _slices(self, name, l, piece):
        if piece is None:
            key, rows = (name, l), None
        else:
            per_half = W_IN_PIECES // 2
            n_rows = D_MODEL // W_IN_PIECES
            key, rows = ((name, piece // per_half), l), ((piece % per_half) * n_rows, n_rows)
        if key not in self.to_send:
            self.to_send[key] = _to_shards(name, self.grads[key])
        return self.to_send[key], rows

    def before(self, call, l):
        comm = _Comm()
        for name, layer, piece in GATHER_PLAN.get((call, l), ()):
            n_rows = D_MODEL // W_IN_PIECES
            comm.gathers.append((self.shards[name], layer, None if piece is None else (piece * n_rows, n_rows)))
        if (call, l) == SMALL_GATHER_CALL:
            names = SMALL_REP + SMALL_SH
            rows = _small_rows(sum(math.prod(self.small[n].shape) for n in names))
            comm.gathers.append((_pack_flat([self.small[n] for n in names], rows), None, None))
        for name, layer, piece in SCATTER_PLAN.get((call, l), ()):
            comm.scatters.append(self._slices(name, layer, piece))
        return comm if comm.gathers or comm.scatters else None

    def after(self, call, l, outs):
        gathers = list(GATHER_PLAN.get((call, l), ()))
        for (name, layer, piece), out in zip(gathers, outs):
            if piece is None:
                self.full[(name, layer)] = _from_shards(name, out)
            else:
                self.pieces[(name, layer, piece)] = out
        outs = outs[len(gathers):]
        if (call, l) == SMALL_GATHER_CALL:
            self.small_gathered = outs[0]
            outs = outs[1:]
        for item, out in zip(SCATTER_PLAN.get((call, l), ()), outs):
            self.received[item] = out


def kernel(x, mem, mem_ln_g, mem_ln_b, w_in, sg_ln_g, sg_ln_b, sg_w, sg_b, conv_w, conv_b, dt_bias, a_log, d_skip, ssm_norm_g, p_a, p_b, w_mix_o, w_xq, w_xkv, w_xo, w_ffn_in, w_ffn_out, ln_g, ln_b, loss_target, m_mem_ln_g, m_mem_ln_b, m_w_in, m_sg_ln_g, m_sg_ln_b, m_sg_w, m_sg_b, m_conv_w, m_conv_b, m_dt_bias, m_a_log, m_d_skip, m_ssm_norm_g, m_p_a, m_p_b, m_w_mix_o, m_w_xq, m_w_xkv, m_w_xo, m_w_ffn_in, m_w_ffn_out, m_ln_g, m_ln_b, v_mem_ln_g, v_mem_ln_b, v_w_in, v_sg_ln_g, v_sg_ln_b, v_sg_w, v_sg_b, v_conv_w, v_conv_b, v_dt_bias, v_a_log, v_d_skip, v_ssm_norm_g, v_p_a, v_p_b, v_w_mix_o, v_w_xq, v_w_xkv, v_w_xo, v_w_ffn_in, v_w_ffn_out, v_ln_g, v_ln_b):
    args = dict(locals())
    w = {n: args[n] for n in WEIGHTS}
    m = {n: args["m_" + n] for n in WEIGHTS}
    v = {n: args["v_" + n] for n in WEIGHTS}
    me = 4 * lax.axis_index("x") + 2 * lax.axis_index("y") + lax.axis_index("c")

    shards = {n: w[n].astype(BF16) for n in BIG}
    sh_shapes = [w[n].shape for n in SMALL_SH]
    first = _Comm()
    first.gathers.append((shards["w_in"], 0, None))
    first.gathers.append((_pack_flat([w[n] for n in SMALL_SH], _small_rows(sum(math.prod(s) for s in sh_shapes))), None,
                          None))
    w_in0, small_sh = _comm_only("gather_first", first)
    small = {n: w[n] for n in SMALL_REP}
    for n, sh in zip(SMALL_SH, _unpack_flat(small_sh, sh_shapes)):
        small[n] = sh.transpose(1, 2, 0, 3).reshape(sh.shape[1], sh.shape[2], N_DEV * sh.shape[3])

    ex = _MeshExchange(shards, {("w_in", 0): _from_shards("w_in", w_in0)})
    loss, grad_x = _run_step(x, mem, loss_target, small, ex)
    loss = lax.psum(loss[0, 0], ("x", "y", "c"))

    out = {}
    for n in BIG:
        out[n] = _adamw_sharded("adamw_" + n, [ex.partial_sums(n, l) for l in range(DEPTH)], w[n], m[n], v[n])
    names = SMALL_REP + SMALL_SH
    g_small = dict(zip(names, _unpack_flat(_sum_parts("sum_small_grads", ex.small_gathered),
                                           [ex.small[n].shape for n in names])))
    for n in names:
        g = g_small[n]
        if n in SMALL_SH:
            width = w[n].shape[-1]
            g = lax.dynamic_slice_in_dim(g, me * width, width, axis=-1)
        two_d = (-1, w[n].shape[-1])
        res = _adamw_small("adamw_" + n, g.reshape(two_d), w[n].reshape(two_d), m[n].reshape(two_d), v[n].reshape(two_d))
        out[n] = [g] + [r.reshape(w[n].shape) for r in res]

    results = []
    for k in range(4):
        results.extend(out[n][k] for n in WEIGHTS)
    return (loss, grad_x, *results)
```

```python
import functools
import math

import jax
import jax.numpy as jnp
from jax import lax
from jax.experimental import pallas as pl
from jax.experimental.pallas import tpu as pltpu

F32 = jnp.float32
BF16 = jnp.bfloat16
HIGHEST = lax.Precision.HIGHEST

N_DEV = 8
D_MODEL = 1024
DEPTH = 2
MEM_LEN = 256
CHUNK = 128
SG_GROUPS = 8
SSM_INNER = 2048
SSM_HEADDIM = 64
SSM_HEADS = 32
SSM_STATE = 128
SSM_GROUPS = 4
SSM_RPG = 8
SSM_CONV = 4
SSM_CONV_DIM = 3072
X_HEADS = 4
X_HEADDIM = 256
FFN_HIDDEN = 2816
ALPHA = float((2 * DEPTH) ** 0.25)
LN_EPS = 1e-5
RMS_EPS = 1e-5
XBC_COL = 4096
DT_COL = 7168
GA_COL = 7200
IN_COLS = 9248
P_GATE = 4096
P_XBC = 6144
P_COLS = 9216
DT_LANES = 128

ADAM_LR = 0.001
ADAM_B1 = 0.9
ADAM_B2 = 0.999
ADAM_EPS = 1e-08
ADAM_WD = 0.01
ADAM_STEP = 10

VMEM_LIMIT = 48 * 1024 * 1024
SMALL_ROW_TILE = 256

BIG = ("w_in", "p_a", "p_b", "w_mix_o", "w_xq", "w_xkv", "w_xo", "w_ffn_in", "w_ffn_out")
BIG_COL_SHARDED = ("w_in", "w_xkv", "w_ffn_in")
SMALL_REP = ("mem_ln_g", "mem_ln_b", "sg_ln_g", "sg_ln_b", "sg_w", "sg_b", "conv_b", "dt_bias", "a_log", "d_skip",
             "ssm_norm_g")
SMALL_SH = ("conv_w", "ln_g", "ln_b")
WEIGHTS = ("mem_ln_g", "mem_ln_b", "w_in", "sg_ln_g", "sg_ln_b", "sg_w", "sg_b", "conv_w", "conv_b", "dt_bias", "a_log",
           "d_skip", "ssm_norm_g", "p_a", "p_b", "w_mix_o", "w_xq", "w_xkv", "w_xo", "w_ffn_in", "w_ffn_out", "ln_g", "ln_b")

W_IN_PIECES = 4
GATHER_PLAN = {("sgu_fwd", 0): [("w_in", 1, 0)], ("conv_fwd", 0): [("w_in", 1, 1)],
               ("mm_ffn_in", 0): [("w_in", 1, 2)], ("swiglu_fwd", 0): [("w_in", 1, 3)]}
SCATTER_PLAN = {("mm_ffn_in_dw", 0): [("w_in", 1, 3)]}
for _l in range(DEPTH):
    GATHER_PLAN[("mm_in", _l)] = [(n, _l, None) for n in ("p_a", "p_b", "w_mix_o", "w_xq", "w_xkv", "w_xo")]
    GATHER_PLAN[("ssd_fwd", _l)] = [("w_ffn_in", _l, None), ("w_ffn_out", _l, None)]
    SCATTER_PLAN[("swiglu_bwd", _l)] = [("w_ffn_out", _l, None)]
    SCATTER_PLAN[("sgu_bwd", _l)] = [("w_mix_o", _l, None), ("p_a", _l, None), ("w_xo", _l, None)]
    SCATTER_PLAN[("ssd_bwd", _l)] = [("w_ffn_in", _l, None), ("w_xkv", _l, None), ("w_xq", _l, None)]
    SCATTER_PLAN[("conv_bwd", _l)] = [("p_b", _l, None)]
    SCATTER_PLAN[("mm_in_dw_b", _l)] = [("w_in", _l, 0)]
    SCATTER_PLAN[("mm_in_dx", _l)] = [("w_in", _l, 1), ("w_in", _l, 2)] + ([("w_in", _l, 3)] if _l == 0 else [])
SMALL_GATHER_CALL = ("mm_in_dw_a", 0)


def _layer_norm(x, g, b):
    mu = jnp.mean(x, axis=-1, keepdims=True)
    xc = x - mu
    var = jnp.mean(xc * xc, axis=-1, keepdims=True)
    return xc * lax.rsqrt(var + LN_EPS) * g + b


def _gelu(x):
    return 0.5 * x * (1.0 + lax.erf(x * (1.0 / math.sqrt(2.0))))


def _sigmoid(x):
    return 0.5 * jnp.tanh(0.5 * x) + 0.5


def _silu(x):
    return x * _sigmoid(x)


def _softplus(x):
    return jnp.maximum(x, 0.0) + jnp.log1p(jnp.exp(-jnp.abs(x)))


def _causal_mask():
    r = lax.broadcasted_iota(jnp.int32, (CHUNK, CHUNK), 0)
    c = lax.broadcasted_iota(jnp.int32, (CHUNK, CHUNK), 1)
    return r >= c


def _sgu_block(uv, ln_g, ln_b, w, sb):
    gu = _gelu(uv[:, :D_MODEL])
    vn = _layer_norm(_gelu(uv[:, D_MODEL:]), ln_g, ln_b)
    causal = _causal_mask()
    width = D_MODEL // SG_GROUPS
    outs = []
    for g in range(SG_GROUPS):
        wg = jnp.where(causal, w[g], 0.0).astype(BF16)
        mixed = jnp.dot(wg, vn[:, g * width:(g + 1) * width].astype(BF16), preferred_element_type=F32)
        outs.append(mixed + sb[:, g:g + 1])
    return (gu * jnp.concatenate(outs, axis=1),)


GROUP_W = SSM_RPG * SSM_HEADDIM
NT_DIMS = (((1,), (1,)), ((), ()))
TN_DIMS = (((0,), (0,)), ((), ()))


def _mxu(a, b, dims=(((1,), (0,)), ((), ()))):
    return lax.dot_general(a.astype(BF16), b.astype(BF16), dims, preferred_element_type=F32)


def _head_expander():
    return (jnp.arange(SSM_INNER)[None, :] // SSM_HEADDIM == jnp.arange(128)[:, None]).astype(BF16)


def _bf16_terms(x, n):
    terms = []
    for _ in range(n):
        t = x.astype(BF16)
        terms.append(t)
        x = x - t.astype(F32)
    return terms


def _expand_heads(q, e):
    return sum(jnp.dot(t, e, preferred_element_type=F32) for t in _bf16_terms(q, 2))


def _reduce_heads(v, e):
    return sum(lax.dot_general(t, e, NT_DIMS, preferred_element_type=F32) for t in _bf16_terms(v, 2))


def _reduce_heads_of_column_sums(v, e):
    sums = jnp.broadcast_to(jnp.sum(v, axis=0, keepdims=True), (8, v.shape[1]))
    return _reduce_heads(sums, e)[0:1, :]


def _ssd_common(xc, dtraw, dt_bias, a_log, e):
    xs = xc[:, :SSM_INNER]
    pre = dtraw + dt_bias
    dt = _softplus(pre)
    a = -jnp.exp(a_log)
    r_i = lax.broadcasted_iota(jnp.int32, (CHUNK, CHUNK), 0)
    c_i = lax.broadcasted_iota(jnp.int32, (CHUNK, CHUNK), 1)
    tril = jnp.where(r_i >= c_i, 1.0, 0.0).astype(F32)
    cs = jnp.dot(tril, dt * a, precision=HIGHEST, preferred_element_type=F32)
    cs_last = cs[CHUNK - 1:CHUNK, :]
    decay_in = jnp.exp(cs)
    decay_st = jnp.exp(cs_last - cs)
    dt_x = _expand_heads(dt, e)
    w_st_x = _expand_heads(dt * decay_st, e)
    decay_in_x = _expand_heads(decay_in, e)
    return dict(xs=xs, pre=pre, dt=dt, a=a, lower=r_i >= c_i, upper=c_i >= r_i, cs=cs, cs_t=cs.T, decay_in=decay_in,
                decay_st=decay_st, chunk_decay=jnp.exp(cs_last), dt_x=dt_x, w_st_x=w_st_x, decay_in_x=decay_in_x,
                chunk_decay_x=decay_in_x[CHUNK - 1:CHUNK, :], xdt=xs * dt_x, x_st=(xs * w_st_x).astype(BF16),
                low=lax.broadcasted_iota(jnp.int32, (CHUNK, 128), 1) < SSM_HEADDIM)


def _pair_decay(c, h):
    return jnp.exp(jnp.where(c["lower"], c["cs"][:, h:h + 1] - c["cs_t"][h:h + 1, :], -1e30))


def _pair_decay_t(c, h):
    return jnp.exp(jnp.where(c["upper"], c["cs_t"][h:h + 1, :] - c["cs"][:, h:h + 1], -1e30))


def _ssd_forward(xc, dtraw, z, prev, dt_bias, a_log, d_skip_x, norm_g, e):
    c = _ssd_common(xc, dtraw, dt_bias, a_log, e)
    y_groups, new_states = [], []
    for g in range(SSM_GROUPS):
        lanes = slice(g * GROUP_W, (g + 1) * GROUP_W)
        bg = xc[:, SSM_INNER + g * SSM_STATE:SSM_INNER + (g + 1) * SSM_STATE]
        cg = xc[:, SSM_INNER + (SSM_GROUPS + g) * SSM_STATE:SSM_INNER + (SSM_GROUPS + g + 1) * SSM_STATE].astype(BF16)
        pg = prev[g * SSM_STATE:(g + 1) * SSM_STATE, :]
        cb = _mxu(cg, bg, NT_DIMS)
        y_in = _mxu(cg, pg) * c["decay_in_x"][:, lanes]
        new_states.append(pg * c["chunk_decay_x"][:, lanes] + _mxu(bg.T, c["x_st"][:, lanes]))
        pairs = []
        for j in range(SSM_RPG // 2):
            h0 = g * SSM_RPG + 2 * j
            xp = c["xdt"][:, 128 * (h0 // 2):128 * (h0 // 2 + 1)]
            pairs.append(_mxu(cb * _pair_decay(c, h0), jnp.where(c["low"], xp, 0.0))
                         + _mxu(cb * _pair_decay(c, h0 + 1), jnp.where(c["low"], 0.0, xp)))
        y_groups.append(jnp.concatenate(pairs, axis=1) + y_in)
    y_pre = jnp.concatenate(y_groups, axis=1) + c["xs"] * d_skip_x
    gated = y_pre * _silu(z)
    normed = [gated[:, g * GROUP_W:(g + 1) * GROUP_W] for g in range(SSM_GROUPS)]
    normed = [yg * lax.rsqrt(jnp.mean(yg * yg, axis=-1, keepdims=True) + RMS_EPS) for yg in normed]
    return jnp.concatenate(normed, axis=1) * norm_g, y_pre, jnp.concatenate(new_states, axis=0)


def _ssd_backward(xc, dtraw, z, prev, y_pre, dt_bias, a_log, d_skip_x, norm_g, e, dout, dnew):
    c = _ssd_common(xc, dtraw, dt_bias, a_log, e)
    xs = c["xs"]
    sig = _sigmoid(z)
    silu_z = z * sig
    gated = y_pre * silu_z
    d_gated, normed = [], []
    for g in range(SSM_GROUPS):
        lanes = slice(g * GROUP_W, (g + 1) * GROUP_W)
        yg = gated[:, lanes]
        r = lax.rsqrt(jnp.mean(yg * yg, axis=-1, keepdims=True) + RMS_EPS)
        n = yg * r
        gh = dout[:, lanes] * norm_g[:, lanes]
        d_gated.append(r * (gh - n * jnp.mean(gh * n, axis=-1, keepdims=True)))
        normed.append(n)
    d_gated = jnp.concatenate(d_gated, axis=1)
    dnorm_g = jnp.sum(dout * jnp.concatenate(normed, axis=1), axis=0, keepdims=True)
    dy = d_gated * silu_z
    dz = d_gated * y_pre * (sig * (1.0 + z * (1.0 - sig)))
    dxs = dy * d_skip_x
    dd_skip = _reduce_heads_of_column_sums(dy * xs, e)

    lane = lax.broadcasted_iota(jnp.int32, (CHUNK, 128), 1)
    sub = lax.broadcasted_iota(jnp.int32, (8, 128), 0)
    dcs_neg = jnp.zeros((CHUNK, 128), F32)
    row_slabs = []
    dxdt, dx_st, d_decay_in_x, dprev, d_chunk_decay_x, db_all, dc_all = [], [], [], [], [], [], []
    for g in range(SSM_GROUPS):
        lanes = slice(g * GROUP_W, (g + 1) * GROUP_W)
        bg = xc[:, SSM_INNER + g * SSM_STATE:SSM_INNER + (g + 1) * SSM_STATE].astype(BF16)
        cg_f = xc[:, SSM_INNER + (SSM_GROUPS + g) * SSM_STATE:SSM_INNER + (SSM_GROUPS + g + 1) * SSM_STATE]
        cg = cg_f.astype(BF16)
        pg = prev[g * SSM_STATE:(g + 1) * SSM_STATE, :]
        dng = dnew[g * SSM_STATE:(g + 1) * SSM_STATE, :]
        dy_g = dy[:, lanes]
        cb_t = _mxu(bg, cg, NT_DIMS)
        t1 = (dy_g * c["decay_in_x"][:, lanes]).astype(BF16)
        d_decay_in_x.append(dy_g * _mxu(cg, pg))
        dc = _mxu(t1, pg, NT_DIMS)
        dprev.append(_mxu(cg_f.T, t1) + dng * c["chunk_decay_x"][:, lanes])
        d_chunk_decay_x.append(dng * pg)
        db = _mxu(c["x_st"][:, lanes], dng, NT_DIMS)
        dx_st.append(_mxu(bg, dng))
        dcb_t = jnp.zeros((CHUNK, CHUNK), F32)
        rows = []
        for j in range(SSM_RPG // 2):
            h0 = g * SSM_RPG + 2 * j
            blk = slice(128 * (h0 // 2), 128 * (h0 // 2 + 1))
            xp = c["xdt"][:, blk]
            dyp = dy[:, blk].astype(BF16)
            pair_dx = []
            for k, xk in enumerate((jnp.where(c["low"], xp, 0.0), jnp.where(c["low"], 0.0, xp))):
                dec_t = _pair_decay_t(c, h0 + k)
                pair_dx.append(_mxu(cb_t * dec_t, dyp))
                dml_t = _mxu(xk, dyp, NT_DIMS) * dec_t
                dcb_t = dcb_t + dml_t
                dseg_t = dml_t * cb_t
                dcs_neg = dcs_neg + jnp.where(lane == h0 + k, jnp.sum(dseg_t, axis=-1, keepdims=True), 0.0)
                rows.append(jnp.sum(dseg_t, axis=0, keepdims=True))
            dxdt.append(jnp.where(c["low"], pair_dx[0], pair_dx[1]))
        slab = jnp.zeros((8, 128), F32)
        for r in range(SSM_RPG):
            slab = slab + jnp.where(sub == r, rows[r], 0.0)
        row_slabs.append(slab)
        dc_all.append(dc + _mxu(dcb_t.T, bg))
        db_all.append(db + _mxu(dcb_t, cg))
    dxdt = jnp.concatenate(dxdt, axis=1)
    dx_st = jnp.concatenate(dx_st, axis=1)
    by_head = jnp.concatenate(row_slabs + [jnp.zeros((CHUNK - SSM_HEADS, 128), F32)], axis=0)
    dcs = by_head.T - dcs_neg
    dxs = dxs + dxdt * c["dt_x"] + dx_st * c["w_st_x"]
    ddt = _reduce_heads(dxdt * xs, e)
    dw_st = _reduce_heads(dx_st * xs, e)
    dcs = dcs + _reduce_heads(jnp.concatenate(d_decay_in_x, axis=1), e) * c["decay_in"]
    ddt = ddt + dw_st * c["decay_st"]
    d_log_st = dw_st * c["dt"] * c["decay_st"]
    dcs = dcs - d_log_st
    d_chunk_decay = _reduce_heads_of_column_sums(jnp.concatenate(d_chunk_decay_x, axis=1), e)
    dcs_last = jnp.sum(d_log_st, axis=0, keepdims=True) + d_chunk_decay * c["chunk_decay"]
    row = lax.broadcasted_iota(jnp.int32, (CHUNK, 128), 0)
    dcs = dcs + jnp.where(row == CHUNK - 1, dcs_last, 0.0)
    triu = jnp.where(c["upper"], 1.0, 0.0).astype(F32)
    dda = jnp.dot(triu, dcs, precision=HIGHEST, preferred_element_type=F32)
    ddt = ddt + dda * c["a"]
    da_log = jnp.sum(dda * c["dt"], axis=0, keepdims=True) * c["a"]
    dpre = ddt * _sigmoid(c["pre"])
    dxc = jnp.concatenate([dxs] + db_all + dc_all, axis=1)
    return (dxc, dpre, dz, jnp.concatenate(dprev, axis=0), jnp.sum(dpre, axis=0, keepdims=True), da_log, dd_skip,
            dnorm_g)


def _conv_block(x, w, b):
    rows = lax.broadcasted_iota(jnp.int32, x.shape, 0)
    acc = x * w[SSM_CONV - 1:SSM_CONV, :] + b
    for k in range(SSM_CONV - 1):
        shift = SSM_CONV - 1 - k
        acc = acc + _shift_rows(x, rows, shift) * w[k:k + 1, :]
    return (_silu(acc),)


@functools.partial(jax.custom_vjp, nondiff_argnums=(2,))
def _shift_rows(x, rows, shift):
    return jnp.where(rows >= shift, pltpu.roll(x, shift, 0), 0.0)


def _shift_rows_fwd(x, rows, shift):
    return _shift_rows(x, rows, shift), rows


def _shift_rows_bwd(shift, rows, g):
    n = g.shape[0]
    return jnp.where(rows < n - shift, pltpu.roll(g, n - shift, 0), 0.0), None


_shift_rows.defvjp(_shift_rows_fwd, _shift_rows_bwd)


def _merge_block(gates, br_a, br_b):
    return (_sigmoid(gates[:, :D_MODEL]) * br_a + _sigmoid(gates[:, D_MODEL:]) * br_b,)


def _lnres_block(x, y, g, b):
    return (_layer_norm(ALPHA * x + y, g, b),)


def _memln_block(x, g, b):
    return (_layer_norm(x, g, b),)


def _attn_block(q, kv):
    outs = []
    for h in range(X_HEADS):
        qh = q[:, h * X_HEADDIM:(h + 1) * X_HEADDIM].astype(BF16)
        kh = kv[:, h * X_HEADDIM:(h + 1) * X_HEADDIM].astype(BF16)
        vh = kv[:, D_MODEL + h * X_HEADDIM:D_MODEL + (h + 1) * X_HEADDIM].astype(BF16)
        s = lax.dot_general(qh, kh, (((1,), (1,)), ((), ())), preferred_element_type=F32) * (X_HEADDIM ** -0.5)
        s = s - lax.stop_gradient(jnp.max(s, axis=-1, keepdims=True))
        e = jnp.exp(s)
        p = e / jnp.sum(e, axis=-1, keepdims=True)
        outs.append(jnp.dot(p.astype(BF16), vh, preferred_element_type=F32))
    return (jnp.concatenate(outs, axis=1),)


def _swiglu_block(gu):
    return (_silu(gu[:, :FFN_HIDDEN]) * gu[:, FFN_HIDDEN:],)


class _Comm:
    def __init__(self):
        self.gathers = []
        self.scatters = []

    @staticmethod
    def _rows(ref, rows):
        return ref if rows is None else ref.at[pl.ds(rows[0], rows[1])]

    def operands(self):
        ins = [a for a, _, _ in self.gathers] + [a for a, _ in self.scatters]
        shapes = []
        for a, idx, rows in self.gathers:
            blk = a.shape if idx is None else a.shape[1:]
            shapes.append(jax.ShapeDtypeStruct((N_DEV, blk[0] if rows is None else rows[1]) + tuple(blk[1:]), a.dtype))
        for a, rows in self.scatters:
            shapes.append(jax.ShapeDtypeStruct((N_DEV, a.shape[1] if rows is None else rows[1]) + tuple(a.shape[2:]),
                                               a.dtype))
        scratch = []
        for n in (len(self.gathers), len(self.scatters)):
            if n:
                scratch += [pltpu.SemaphoreType.DMA((7 * n,)), pltpu.SemaphoreType.DMA((7 * n,)),
                            pltpu.SemaphoreType.DMA((n,))]
        return ins, shapes, scratch

    def _split(self, in_refs, out_refs, sems):
        ng = len(self.gathers)
        g_sems = sems[:3] if ng else None
        s_sems = sems[3:] if ng else sems
        return in_refs[:ng], in_refs[ng:], out_refs[:ng], out_refs[ng:], g_sems, s_sems

    def _gather_copies(self, i, src_ref, out_ref, sems):
        send_sems, recv_sems, local_sems = sems
        x, y, c = lax.axis_index("x"), lax.axis_index("y"), lax.axis_index("c")
        me, sibling = (x, y, c), (x, y, 1 - c)
        chips = [(1 - x, y), (x, 1 - y), (1 - x, 1 - y)]
        _, idx, rows = self.gathers[i]
        src = self._rows(src_ref if idx is None else src_ref.at[idx], rows)

        def slot(px, py, pc):
            return out_ref.at[4 * px + 2 * py + pc]

        def copy(k, blk, to, from_src=False):
            return pltpu.make_async_remote_copy(
                src_ref=src if from_src else slot(*blk), dst_ref=slot(*blk), send_sem=send_sems.at[7 * i + k],
                recv_sem=recv_sems.at[7 * i + k], device_id=to, device_id_type=pl.DeviceIdType.MESH)

        mine = pltpu.make_async_copy(src, slot(*me), local_sems.at[i])
        first = [copy(0, me, sibling, True)] + [copy(1 + j, me, (*chip, c), True) for j, chip in enumerate(chips)]
        passed = [copy(4 + j, (*chip, c), sibling) for j, chip in enumerate(chips)]
        arrivals = [copy(1 + j, (*chip, c), me) for j, chip in enumerate(chips)]
        from_sibling = [copy(0, sibling, me)] + [copy(4 + j, (*chip, 1 - c), me) for j, chip in enumerate(chips)]
        return mine, first, passed, arrivals, from_sibling

    def _scatter_copies(self, i, src_ref, out_ref, sems):
        send_sems, recv_sems, local_sems = sems
        x, y, c = lax.axis_index("x"), lax.axis_index("y"), lax.axis_index("c")
        me = 4 * x + 2 * y + c
        rows = self.scatters[i][1]
        mine = pltpu.make_async_copy(self._rows(src_ref.at[me], rows), out_ref.at[me], local_sems.at[i])
        copies = []
        for k in range(1, N_DEV):
            px = 1 - x if k & 4 else x
            py = 1 - y if k & 2 else y
            pc = 1 - c if k & 1 else c
            copies.append(pltpu.make_async_remote_copy(
                src_ref=self._rows(src_ref.at[4 * px + 2 * py + pc], rows), dst_ref=out_ref.at[me],
                send_sem=send_sems.at[7 * i + k - 1], recv_sem=recv_sems.at[7 * i + k - 1], device_id=(px, py, pc),
                device_id_type=pl.DeviceIdType.MESH))
        return mine, copies

    def start(self, in_refs, out_refs, sems):
        g_in, s_in, g_out, s_out, g_sems, s_sems = self._split(in_refs, out_refs, sems)
        for i in range(len(self.gathers)):
            mine, first, _, _, _ = self._gather_copies(i, g_in[i], g_out[i], g_sems)
            mine.start()
            for cp in first:
                cp.start()
        for i in range(len(self.scatters)):
            mine, copies = self._scatter_copies(i, s_in[i], s_out[i], s_sems)
            mine.start()
            for cp in copies:
                cp.start()

    def finish(self, in_refs, out_refs, sems):
        g_in, s_in, g_out, s_out, g_sems, s_sems = self._split(in_refs, out_refs, sems)
        parts = [self._gather_copies(i, g_in[i], g_out[i], g_sems) for i in range(len(self.gathers))]
        for j in range(3):
            for _, _, passed, arrivals, _ in parts:
                arrivals[j].wait_recv()
                passed[j].start()
        for mine, first, passed, _, from_sibling in parts:
            for cp in from_sibling:
                cp.wait_recv()
            for cp in first + passed:
                cp.wait_send()
            mine.wait()
        for i in range(len(self.scatters)):
            mine, copies = self._scatter_copies(i, s_in[i], s_out[i], s_sems)
            for cp in copies:
                cp.wait_recv()
            for cp in copies:
                cp.wait_send()
            mine.wait()


def _params(grid):
    return pltpu.CompilerParams(dimension_semantics=("arbitrary",) * len(grid), vmem_limit_bytes=VMEM_LIMIT)


def _call(name, body, *, grid, ins, in_specs, out_shape, out_specs, scratch=(), comm=None, aliases=None):
    n_in, n_out, n_scr = len(ins), len(out_shape), len(scratch)
    aliases = aliases or {}
    if comm is None:
        outs = pl.pallas_call(body, grid=grid, in_specs=list(in_specs), out_specs=list(out_specs),
                              out_shape=list(out_shape), scratch_shapes=list(scratch), name=name,
                              input_output_aliases=aliases, compiler_params=_params(grid))(*ins)
        return list(outs), []
    c_ins, c_shapes, c_scratch = comm.operands()
    nci, nco = len(c_ins), len(c_shapes)
    anywhere = pl.BlockSpec(memory_space=pl.ANY)

    def carrier(*refs):
        main_in, comm_in = refs[:n_in], refs[n_in:n_in + nci]
        o0 = n_in + nci
        main_out, comm_out = refs[o0:o0 + n_out], refs[o0 + n_out:o0 + n_out + nco]
        s0 = o0 + n_out + nco
        main_scr, comm_scr = refs[s0:s0 + n_scr], refs[s0 + n_scr:]
        first = pl.program_id(0) == 0
        last = pl.program_id(0) == grid[0] - 1
        for ax in range(1, len(grid)):
            first = first & (pl.program_id(ax) == 0)
            last = last & (pl.program_id(ax) == grid[ax] - 1)

        @pl.when(first)
        def _():
            comm.start(comm_in, comm_out, comm_scr)

        body(*main_in, *main_out, *main_scr)

        @pl.when(last)
        def _():
            comm.finish(comm_in, comm_out, comm_scr)

    outs = pl.pallas_call(carrier, grid=grid, in_specs=list(in_specs) + [anywhere] * nci,
                          out_specs=list(out_specs) + [anywhere] * nco, out_shape=list(out_shape) + c_shapes,
                          scratch_shapes=list(scratch) + c_scratch, name=name, input_output_aliases=aliases,
                          compiler_params=_params(grid))(*ins, *c_ins)
    return list(outs[:n_out]), list(outs[n_out:])


def _comm_only(name, comm):
    c_ins, c_shapes, c_scratch = comm.operands()
    nci, nco = len(c_ins), len(c_shapes)
    anywhere = pl.BlockSpec(memory_space=pl.ANY)

    def body(*refs):
        comm.start(refs[:nci], refs[nci:nci + nco], refs[nci + nco:])
        comm.finish(refs[:nci], refs[nci:nci + nco], refs[nci + nco:])

    return list(pl.pallas_call(body, in_specs=[anywhere] * nci, out_specs=[anywhere] * nco, out_shape=c_shapes,
                               scratch_shapes=c_scratch, name=name)(*c_ins))


def _stage_fwd(name, f, grid, ins, in_specs, out_shapes, out_specs, comm=None):
    n_in = len(ins)

    def body(*refs):
        res = f(*[r[...].astype(F32) for r in refs[:n_in]])
        for o_ref, val in zip(refs[n_in:], res):
            o_ref[...] = val.astype(o_ref.dtype)

    return _call(name, body, grid=grid, ins=ins, in_specs=in_specs, out_shape=out_shapes, out_specs=out_specs, comm=comm)


def _stage_bwd(name, f, grid, ins, in_specs, cts, ct_specs, grads, comm=None):
    n_in = len(ins)
    flat_cts = [c for group in cts for c in group]
    flat_ct_specs = [s for group, spec in zip(cts, ct_specs) for s in (spec,) * len(group)]
    n_ct = len(flat_cts)
    diff = [g[0] for g in grads]
    buffers = [(k, g[4]) for k, g in enumerate(grads) if len(g) > 4]
    n_buf = len(buffers)

    def body(*refs):
        vals = [r[...].astype(F32) for r in refs[:n_in]]
        ct_refs = refs[n_in:n_in + n_ct]
        g_refs = refs[n_in + n_ct + n_buf:]
        ct_vals, pos = [], 0
        for group in cts:
            acc = ct_refs[pos][...].astype(F32)
            for j in range(1, len(group)):
                acc = acc + ct_refs[pos + j][...].astype(F32)
            ct_vals.append(acc)
            pos += len(group)

        def g_fn(*dvals):
            full = list(vals)
            for i, dv in zip(diff, dvals):
                full[i] = dv
            return f(*full)

        _, vjp = jax.vjp(g_fn, *[vals[i] for i in diff])
        gvals = vjp(tuple(ct_vals))
        for gspec, g_ref, gval in zip(grads, g_refs, gvals):
            acc_axes = gspec[1]
            if not acc_axes:
                g_ref[...] = gval.astype(g_ref.dtype)
            else:
                first = pl.program_id(acc_axes[0]) == 0
                for ax in acc_axes[1:]:
                    first = first & (pl.program_id(ax) == 0)

                @pl.when(first)
                def _():
                    g_ref[...] = jnp.zeros_like(g_ref)

                g_ref[...] += gval.astype(g_ref.dtype)

    out_shapes, out_specs = [], []
    for gspec in grads:
        shape, spec = gspec[3] if len(gspec) > 3 else (ins[gspec[0]].shape, in_specs[gspec[0]])
        out_shapes.append(jax.ShapeDtypeStruct(shape, gspec[2]))
        out_specs.append(spec)
    anywhere = pl.BlockSpec(memory_space=pl.ANY)
    return _call(name, body, grid=grid, ins=list(ins) + flat_cts + [b for _, b in buffers],
                 in_specs=list(in_specs) + flat_ct_specs + [anywhere] * n_buf, out_shape=out_shapes, out_specs=out_specs,
                 comm=comm, aliases={n_in + n_ct + j: k for j, (k, _) in enumerate(buffers)})


def _pick_tile(n, candidates):
    for c in candidates:
        if n % c == 0:
            return c
    return n


def _matmul(name, a, b, *, ta=False, tb=False, add=None, out_dtype=F32, comm=None):
    if ta:
        k_dim, m = a.shape
    else:
        m, k_dim = a.shape
    n = b.shape[0] if tb else b.shape[1]
    assert (b.shape[1] if tb else b.shape[0]) == k_dim and not (ta and tb)
    tm = _pick_tile(m, (1024, 1408, 512, 256, 128))
    tn = _pick_tile(n, (1024, 1408, 512, 256, 128))
    if ta:
        tk = _pick_tile(k_dim, (1024, 512, 256, 128))
    elif k_dim <= 2816:
        tk = k_dim
    else:
        tk = _pick_tile(k_dim, (1408, 1024, 512, 256, 128))
    nk = k_dim // tk
    grid = (m // tm, n // tn, nk)
    a_spec = pl.BlockSpec((tk, tm), lambda i, j, k: (k, i)) if ta else pl.BlockSpec((tm, tk), lambda i, j, k: (i, k))
    b_spec = pl.BlockSpec((tn, tk), lambda i, j, k: (j, k)) if tb else pl.BlockSpec((tk, tn), lambda i, j, k: (k, j))
    o_spec = pl.BlockSpec((tm, tn), lambda i, j, k: (i, j))
    dims = (((0 if ta else 1,), (1 if tb else 0,)), ((), ()))
    has_add = add is not None

    def body(*refs):
        a_ref, b_ref = refs[0], refs[1]
        add_ref = refs[2] if has_add else None
        o_ref, acc_ref = refs[-2], refs[-1]
        k = pl.program_id(2)
        part = lax.dot_general(a_ref[...].astype(BF16), b_ref[...].astype(BF16), dims, preferred_element_type=F32)

        def finish(res):
            if has_add:
                res = res + add_ref[...].astype(F32)
            o_ref[...] = res.astype(o_ref.dtype)

        if nk == 1:
            finish(part)
        else:
            @pl.when(k == 0)
            def _():
                acc_ref[...] = part

            @pl.when((k > 0) & (k < nk - 1))
            def _():
                acc_ref[...] += part

            @pl.when(k == nk - 1)
            def _():
                finish(acc_ref[...] + part)

    ins = [a, b] + ([add] if has_add else [])
    in_specs = [a_spec, b_spec] + ([o_spec] if has_add else [])
    acc_shape = (tm, tn) if nk > 1 else (8, 128)
    outs, comm_outs = _call(name, body, grid=grid, ins=ins, in_specs=in_specs,
                            out_shape=[jax.ShapeDtypeStruct((m, n), out_dtype)], out_specs=[o_spec],
                            scratch=[pltpu.VMEM(acc_shape, F32)], comm=comm)
    return outs[0], comm_outs


def _matmul_lnres(name, a, b, x, g, beta):
    m, k_dim = a.shape
    n = b.shape[1]
    tm = _pick_tile(m, (512, 256, 128))
    row = lambda w: pl.BlockSpec((tm, w), lambda i: (i, 0))
    whole = lambda shape: pl.BlockSpec(shape, lambda i: (0, 0))

    def body(a_ref, b_ref, x_ref, g_ref, beta_ref, y_ref, h_ref, hb_ref):
        y = jnp.dot(a_ref[...].astype(BF16), b_ref[...].astype(BF16), preferred_element_type=F32).astype(y_ref.dtype)
        y_ref[...] = y
        (h,) = _lnres_block(x_ref[...], y.astype(F32), g_ref[...], beta_ref[...])
        h_ref[...] = h
        hb_ref[...] = h.astype(hb_ref.dtype)

    sds = jax.ShapeDtypeStruct
    return _call(name, body, grid=(m // tm,), ins=[a, b, x, g, beta],
                 in_specs=[row(k_dim), whole((k_dim, n)), row(n), whole((1, n)), whole((1, n))],
                 out_shape=[sds((m, n), BF16), sds((m, n), F32), sds((m, n), BF16)], out_specs=[row(n)] * 3)[0]


SSD_STATE = (SSM_GROUPS * SSM_STATE, SSM_RPG * SSM_HEADDIM)


def _ssd_fwd(xc, dt_raw, proj, dt_bias, a_log, d_skip, norm_g, nb, nc, comm=None):
    t = xc.shape[0]
    row = lambda b, c: (b * nc + c, 0)
    par = lambda shape: pl.BlockSpec(shape, lambda b, c: (0, 0))

    def body(xc_ref, dt_ref, z_ref, dtb_ref, al_ref, ds_ref, ng_ref, e_ref, y_ref, ypre_ref, prev_ref, st_ref):
        @pl.when(pl.program_id(1) == 0)
        def _():
            st_ref[...] = jnp.zeros_like(st_ref)

        prev = st_ref[...]
        prev_ref[0, 0] = prev
        y, y_pre, new_state = _ssd_forward(xc_ref[...], dt_ref[...], z_ref[...].astype(F32), prev, dtb_ref[...],
                                           al_ref[...], ds_ref[...], ng_ref[...], e_ref[...])
        y_ref[...] = y.astype(y_ref.dtype)
        ypre_ref[...] = y_pre
        st_ref[...] = new_state

    return _call(
        "ssd_fwd", body, grid=(nb, nc), ins=[xc, dt_raw, proj, dt_bias, a_log, d_skip, norm_g, _head_expander()],
        in_specs=[pl.BlockSpec((CHUNK, SSM_CONV_DIM), row), pl.BlockSpec((CHUNK, 128), row),
                  pl.BlockSpec((CHUNK, SSM_INNER), lambda b, c: (b * nc + c, 1)),
                  par((1, 128)), par((1, 128)), par((1, SSM_INNER)), par((1, SSM_INNER)), par((128, SSM_INNER))],
        out_specs=[pl.BlockSpec((CHUNK, SSM_INNER), row), pl.BlockSpec((CHUNK, SSM_INNER), row),
                   pl.BlockSpec((1, 1) + SSD_STATE, lambda b, c: (b, c, 0, 0))],
        out_shape=[jax.ShapeDtypeStruct((t, SSM_INNER), BF16), jax.ShapeDtypeStruct((t, SSM_INNER), F32),
                   jax.ShapeDtypeStruct((nb, nc) + SSD_STATE, F32)],
        scratch=[pltpu.VMEM(SSD_STATE, F32)], comm=comm)


def _ssd_bwd(xc, dt_raw, proj, prevs, y_pre, dt_bias, a_log, d_skip, norm_g, dy, dproj, nb, nc, comm=None):
    t = xc.shape[0]
    row = lambda b, c: (b * nc + (nc - 1 - c), 0)
    par = lambda shape: pl.BlockSpec(shape, lambda b, c: (0, 0))
    z_spec = pl.BlockSpec((CHUNK, SSM_INNER), lambda b, c: (b * nc + (nc - 1 - c), 1))

    def body(xc_ref, dt_ref, z_ref, prev_ref, ypre_ref, dtb_ref, al_ref, ds_ref, ng_ref, e_ref, dy_ref, _,
             dxc_ref, ddt_ref, dz_ref, ddtb_ref, dal_ref, dds_ref, dng_ref, dst_ref):
        @pl.when(pl.program_id(1) == 0)
        def _():
            dst_ref[...] = jnp.zeros_like(dst_ref)

        @pl.when((pl.program_id(0) == 0) & (pl.program_id(1) == 0))
        def _():
            ddtb_ref[...] = jnp.zeros_like(ddtb_ref)
            dal_ref[...] = jnp.zeros_like(dal_ref)
            dds_ref[...] = jnp.zeros_like(dds_ref)
            dng_ref[...] = jnp.zeros_like(dng_ref)

        dxc, ddt, dz, dprev, ddtb, dal, dds, dng = _ssd_backward(
            xc_ref[...], dt_ref[...], z_ref[...].astype(F32), prev_ref[0, 0], ypre_ref[...], dtb_ref[...], al_ref[...],
            ds_ref[...], ng_ref[...], e_ref[...], dy_ref[...].astype(F32), dst_ref[...])
        dxc_ref[...] = dxc
        ddt_ref[...] = ddt.astype(ddt_ref.dtype)
        dz_ref[...] = dz.astype(dz_ref.dtype)
        dst_ref[...] = dprev
        ddtb_ref[...] += ddtb
        dal_ref[...] += dal
        dds_ref[...] += dds
        dng_ref[...] += dng

    return _call(
        "ssd_bwd", body, grid=(nb, nc),
        ins=[xc, dt_raw, proj, prevs, y_pre, dt_bias, a_log, d_skip, norm_g, _head_expander(), dy, dproj],
        in_specs=[pl.BlockSpec((CHUNK, SSM_CONV_DIM), row), pl.BlockSpec((CHUNK, DT_LANES), row), z_spec,
                  pl.BlockSpec((1, 1) + SSD_STATE, lambda b, c: (b, nc - 1 - c, 0, 0)),
                  pl.BlockSpec((CHUNK, SSM_INNER), row),
                  par((1, 128)), par((1, 128)), par((1, SSM_INNER)), par((1, SSM_INNER)), par((128, SSM_INNER)),
                  pl.BlockSpec((CHUNK, SSM_INNER), row), pl.BlockSpec(memory_space=pl.ANY)],
        out_specs=[pl.BlockSpec((CHUNK, SSM_CONV_DIM), row), pl.BlockSpec((CHUNK, DT_LANES), row), z_spec,
                   par((1, 128)), par((1, 128)), par((1, 128)), par((1, SSM_INNER))],
        out_shape=[jax.ShapeDtypeStruct((t, SSM_CONV_DIM), F32), jax.ShapeDtypeStruct((t, DT_LANES), BF16),
                   jax.ShapeDtypeStruct(dproj.shape, dproj.dtype), jax.ShapeDtypeStruct((1, 128), F32),
                   jax.ShapeDtypeStruct((1, 128), F32), jax.ShapeDtypeStruct((1, 128), F32),
                   jax.ShapeDtypeStruct((1, SSM_INNER), F32)],
        scratch=[pltpu.VMEM(SSD_STATE, F32)], comm=comm, aliases={11: 2})


def _loss_head(y, target):
    t, d = y.shape
    tm = _pick_tile(t, (256,))
    blk = pl.BlockSpec((tm, d), lambda i: (i, 0))

    def body(y_ref, t_ref, loss_ref, dy_ref):
        err = y_ref[...] - t_ref[...]
        dy_ref[...] = err * (1.0 / d)

        @pl.when(pl.program_id(0) == 0)
        def _():
            loss_ref[...] = jnp.zeros_like(loss_ref)

        loss_ref[...] += 0.5 * jnp.sum(jnp.mean(err * err, axis=-1, keepdims=True), axis=0, keepdims=True)

    return _call("loss_head", body, grid=(t // tm,), ins=[y, target], in_specs=[blk, blk],
                 out_specs=[pl.BlockSpec((1, 1), lambda i: (0, 0)), blk],
                 out_shape=[jax.ShapeDtypeStruct((1, 1), F32), jax.ShapeDtypeStruct((t, d), F32)])[0]


def _adamw_math(g, w, m, v):
    m_new = ADAM_B1 * m + (1.0 - ADAM_B1) * g
    v_new = ADAM_B2 * v + (1.0 - ADAM_B2) * jnp.square(g)
    m_hat = m_new / (1.0 - ADAM_B1 ** ADAM_STEP)
    v_hat = v_new / (1.0 - ADAM_B2 ** ADAM_STEP)
    delta = -ADAM_LR * (m_hat / (jnp.sqrt(v_hat) + ADAM_EPS) + ADAM_WD * w)
    return delta, m_new, v_new


def _adamw_sharded(name, parts, w, m, v):
    _, a, b = w.shape
    tr = _pick_tile(a, (128,))
    nt = a // tr
    part_specs = [pl.BlockSpec((N_DEV, tr, b),
                               (lambda l, i, _k=k: (0, jnp.where(l == _k, i, jnp.where(l > _k, nt - 1, 0)), 0)))
                  for k in range(DEPTH)]
    blk = pl.BlockSpec((1, tr, b), lambda l, i: (l, i, 0))

    def body(*refs):
        p_refs = refs[:DEPTH]
        w_ref, m_ref, v_ref, g_out, d_out, m_out, v_out = refs[DEPTH:]
        for k in range(DEPTH):
            @pl.when(pl.program_id(0) == k)
            def _(p_ref=p_refs[k]):
                g = p_ref[0].astype(F32)
                for p in range(1, N_DEV):
                    g = g + p_ref[p].astype(F32)
                delta, m_new, v_new = _adamw_math(g, w_ref[0], m_ref[0], v_ref[0])
                g_out[0] = g
                d_out[0] = delta
                m_out[0] = m_new
                v_out[0] = v_new

    return _call(name, body, grid=(DEPTH, nt), ins=list(parts) + [w, m, v], in_specs=part_specs + [blk, blk, blk],
                 out_specs=[blk] * 4, out_shape=[jax.ShapeDtypeStruct(w.shape, F32)] * 4)[0]


def _adamw_small(name, g, w, m, v):
    full = pl.BlockSpec(w.shape, lambda i: (0, 0))

    def body(g_ref, w_ref, m_ref, v_ref, d_out, m_out, v_out):
        delta, m_new, v_new = _adamw_math(g_ref[...], w_ref[...], m_ref[...], v_ref[...])
        d_out[...] = delta
        m_out[...] = m_new
        v_out[...] = v_new

    return _call(name, body, grid=(1,), ins=[g, w, m, v], in_specs=[full] * 4, out_specs=[full] * 3,
                 out_shape=[jax.ShapeDtypeStruct(w.shape, F32)] * 3)[0]


def _sum_parts(name, parts):
    n_parts, rows, cols = parts.shape
    tr = _pick_tile(rows, (512, 256, 128, 64, 32, 16, 8))

    def body(p_ref, o_ref):
        acc = p_ref[0]
        for p in range(1, n_parts):
            acc = acc + p_ref[p]
        o_ref[...] = acc

    return _call(name, body, grid=(rows // tr,), ins=[parts],
                 in_specs=[pl.BlockSpec((n_parts, tr, cols), lambda i: (0, i, 0))],
                 out_specs=[pl.BlockSpec((tr, cols), lambda i: (i, 0))],
                 out_shape=[jax.ShapeDtypeStruct((rows, cols), parts.dtype)])[0][0]


W_IN_SHARD = IN_COLS // N_DEV


def _pack_w_in(gathered):
    r = gathered.shape[1]
    tr = _pick_tile(r, (128,))

    def body(g_ref, main_ref, dt_ref):
        w = jnp.concatenate([g_ref[j].astype(F32) for j in range(N_DEV)], axis=1)
        main_ref[...] = jnp.concatenate([w[:, :XBC_COL], w[:, GA_COL:], w[:, XBC_COL:DT_COL]],
                                        axis=1).astype(main_ref.dtype)
        dt_ref[...] = jnp.concatenate([w[:, DT_COL:GA_COL], jnp.zeros((tr, DT_LANES - SSM_HEADS), F32)],
                                      axis=1).astype(dt_ref.dtype)

    return _call("pack_w_in", body, grid=(r // tr,), ins=[gathered],
                 in_specs=[pl.BlockSpec((N_DEV, tr, W_IN_SHARD), lambda i: (0, i, 0))],
                 out_specs=[pl.BlockSpec((tr, P_COLS), lambda i: (i, 0)), pl.BlockSpec((tr, DT_LANES), lambda i: (i, 0))],
                 out_shape=[jax.ShapeDtypeStruct((r, P_COLS), gathered.dtype),
                            jax.ShapeDtypeStruct((r, DT_LANES), gathered.dtype)])[0]


def _unpack_w_in(main, dt):
    r = main.shape[0]
    tr = _pick_tile(r, (128,))

    def body(main_ref, dt_ref, o_ref):
        main = main_ref[...].astype(F32)
        w = jnp.concatenate([main[:, :P_GATE], main[:, P_XBC:], dt_ref[...].astype(F32)[:, :SSM_HEADS],
                             main[:, P_GATE:P_XBC]], axis=1)
        for j in range(N_DEV):
            o_ref[j] = w[:, j * W_IN_SHARD:(j + 1) * W_IN_SHARD].astype(o_ref.dtype)

    return _call("unpack_w_in", body, grid=(r // tr,), ins=[main, dt],
                 in_specs=[pl.BlockSpec((tr, P_COLS), lambda i: (i, 0)), pl.BlockSpec((tr, DT_LANES), lambda i: (i, 0))],
                 out_specs=[pl.BlockSpec((N_DEV, tr, W_IN_SHARD), lambda i: (0, i, 0))],
                 out_shape=[jax.ShapeDtypeStruct((N_DEV, r, W_IN_SHARD), main.dtype)])[0][0]


def _pad_heads(v):
    return jnp.pad(v, (0, 128 - SSM_HEADS)).reshape(1, 128)


def _run_step(x, mem, target, small, ex):
    nb, s, d = x.shape
    t = nb * s
    nc = s // CHUNK
    rows = _pick_tile(t, (256,))
    rows_wide = _pick_tile(t, (512, 256))
    tq = _pick_tile(s, (512, 256))
    vec = lambda a: a.reshape(1, -1)
    full1 = lambda shape: pl.BlockSpec(shape, lambda i: (0,) * len(shape))
    row1 = lambda tm, w: pl.BlockSpec((tm, w), lambda i: (i, 0))
    sds = jax.ShapeDtypeStruct

    def mm(call, l, a, b, **kw):
        comm = ex.before(call, l)
        out, comm_outs = _matmul(call, a, b, comm=comm, **kw)
        if comm is not None:
            ex.after(call, l, comm_outs)
        return out

    def stage_bwd(call, l, *args):
        comm = ex.before(call, l)
        outs, comm_outs = _stage_bwd(call, *args, comm=comm)
        if comm is not None:
            ex.after(call, l, comm_outs)
        return outs

    def stage_fwd(call, l, *args):
        comm = ex.before(call, l)
        outs, comm_outs = _stage_fwd(call, *args, comm=comm)
        if comm is not None:
            ex.after(call, l, comm_outs)
        return outs

    mem_specs = [row1(256, d), full1((1, d)), full1((1, d))]
    mem_ins = [mem.reshape(nb * MEM_LEN, d), vec(small["mem_ln_g"]), vec(small["mem_ln_b"])]
    (mem_n,) = stage_fwd("memln_fwd", 0, _memln_block, (nb * MEM_LEN // 256,), mem_ins, mem_specs,
                          [sds((nb * MEM_LEN, d), BF16)], [row1(256, d)])

    h = x.reshape(t, d)
    h_bf = h.astype(BF16)
    ln_specs = [row1(rows, d), row1(rows, d), full1((1, d)), full1((1, d))]
    saved = []
    for l in range(DEPTH):
        sv = {"h_bf": h_bf}
        w_p, w_dt = ex.weight("w_in", l)
        proj = mm("mm_in", l, h_bf, w_p, out_dtype=BF16)
        dt_raw = mm("mm_dt", l, h_bf, w_dt)
        sv["proj"] = proj
        sgu_ins = [proj, vec(small["sg_ln_g"][l]), vec(small["sg_ln_b"][l]), small["sg_w"][l], small["sg_b"][l].T]
        sgu_specs = [pl.BlockSpec((CHUNK, 2 * d), lambda i: (i, 0)), full1((1, d)), full1((1, d)),
                     full1((SG_GROUPS, CHUNK, CHUNK)), full1((CHUNK, SG_GROUPS))]
        (a_out,) = stage_fwd("sgu_fwd", l, _sgu_block, (t // CHUNK,), sgu_ins, sgu_specs, [sds((t, d), BF16)],
                              [row1(CHUNK, d)])
        sv["sgu"] = (sgu_ins, sgu_specs)
        sv["a_out"] = a_out
        cw = 256
        conv_ins = [proj, small["conv_w"][l], vec(small["conv_b"][l])]
        conv_specs = [pl.BlockSpec((s, cw), lambda j, b: (b, P_XBC // cw + j)),
                      pl.BlockSpec((SSM_CONV, cw), lambda j, b: (0, j)), pl.BlockSpec((1, cw), lambda j, b: (0, j))]
        conv_out_spec = pl.BlockSpec((s, cw), lambda j, b: (b, j))
        (xc,) = stage_fwd("conv_fwd", l, _conv_block, (SSM_CONV_DIM // cw, nb), conv_ins, conv_specs,
                           [sds((t, SSM_CONV_DIM), F32)], [conv_out_spec])
        sv["conv"] = (conv_ins, conv_specs, conv_out_spec)
        ssd_par = [_pad_heads(small["dt_bias"][l]), _pad_heads(small["a_log"][l]),
                   vec(jnp.repeat(small["d_skip"][l], SSM_HEADDIM)), vec(small["ssm_norm_g"][l])]
        comm = ex.before("ssd_fwd", l)
        (y_ssd, y_pre, prevs), comm_outs = _ssd_fwd(xc, dt_raw, proj, *ssd_par, nb, nc, comm=comm)
        if comm is not None:
            ex.after("ssd_fwd", l, comm_outs)
        sv["ssd"] = (xc, dt_raw, prevs, y_pre, ssd_par)
        sv["y_ssd"] = y_ssd
        br_a = mm("mm_sq", l, a_out, ex.weight("p_a", l), out_dtype=BF16)
        br_b = mm("mm_pb", l, y_ssd, ex.weight("p_b", l), out_dtype=BF16)
        merge_ins = [proj, br_a, br_b]
        merge_out_spec = row1(rows_wide, d)
        merge_specs = [pl.BlockSpec((rows_wide, 2 * d), lambda i: (i, P_GATE // (2 * d))), merge_out_spec, merge_out_spec]
        (merged,) = stage_fwd("merge_fwd", l, _merge_block, (t // rows_wide,), merge_ins, merge_specs,
                               [sds((t, d), BF16)], [merge_out_spec])
        sv["merge"] = (merge_ins, merge_specs, merge_out_spec)
        sv["merged"] = merged
        ln_par = [(vec(small["ln_g"][l, k]), vec(small["ln_b"][l, k])) for k in range(3)]
        y1, h1, h1_bf = _matmul_lnres("mm_sq_ln", merged, ex.weight("w_mix_o", l), h, *ln_par[0])
        sv["ln1"] = [h, y1, *ln_par[0]]
        q = mm("mm_sq", l, h1_bf, ex.weight("w_xq", l), out_dtype=BF16)
        kv = mm("mm_kv", l, mem_n, ex.weight("w_xkv", l), out_dtype=BF16)
        attn_ins = [q, kv]
        attn_out_spec = pl.BlockSpec((tq, d), lambda b, i: (b * (s // tq) + i, 0))
        attn_specs = [attn_out_spec, pl.BlockSpec((MEM_LEN, 2 * d), lambda b, i: (b, 0))]
        (o,) = stage_fwd("attn_fwd", l, _attn_block, (nb, s // tq), attn_ins, attn_specs, [sds((t, d), BF16)],
                          [attn_out_spec])
        sv["attn"] = (attn_ins, attn_specs, attn_out_spec)
        sv["o"] = o
        sv["h1_bf"] = h1_bf
        y2, h2, h2_bf = _matmul_lnres("mm_sq_ln", o, ex.weight("w_xo", l), h1, *ln_par[1])
        sv["ln2"] = [h1, y2, *ln_par[1]]
        sv["h2_bf"] = h2_bf
        gu = mm("mm_ffn_in", l, h2_bf, ex.weight("w_ffn_in", l), out_dtype=BF16)
        (act,) = stage_fwd("swiglu_fwd", l, _swiglu_block, (t // 128,), [gu], [row1(128, 2 * FFN_HIDDEN)],
                            [sds((t, FFN_HIDDEN), BF16)], [row1(128, FFN_HIDDEN)])
        sv["gu"] = gu
        sv["act"] = act
        y3, h3, h3_bf = _matmul_lnres("mm_ffn_out_ln", act, ex.weight("w_ffn_out", l), h2, *ln_par[2])
        sv["ln3"] = [h2, y3, *ln_par[2]]
        h, h_bf = h3, h3_bf
        saved.append(sv)

    loss, dh = _loss_head(h, target.reshape(t, d))

    g_small = {n: [None] * DEPTH for n in SMALL_REP + SMALL_SH if n not in ("mem_ln_g", "mem_ln_b")}
    dmem_n = []
    ln_grads = [(0, (), F32), (1, (), BF16), (2, (0,), F32), (3, (0,), F32)]
    for l in reversed(range(DEPTH)):
        sv = saved[l]
        dln_g, dln_b = [None] * 3, [None] * 3
        dres, dy3, dln_g[2], dln_b[2] = stage_bwd("lnres_bwd", l, _lnres_block, (t // rows,), sv["ln3"], ln_specs,
                                                  [(dh,)], [row1(rows, d)], ln_grads)
        ex.grad("w_ffn_out", l, mm("mm_ffn_out_dw", l, sv["act"], dy3, ta=True, out_dtype=BF16))
        dact = mm("mm_ffn_out_dx", l, dy3, ex.weight("w_ffn_out", l), tb=True, out_dtype=BF16)
        (dgu,) = stage_bwd("swiglu_bwd", l, _swiglu_block, (t // 128,), [sv["gu"]], [row1(128, 2 * FFN_HIDDEN)],
                           [(dact,)], [row1(128, FFN_HIDDEN)], [(0, (), BF16)])
        ex.grad("w_ffn_in", l, mm("mm_ffn_in_dw", l, sv["h2_bf"], dgu, ta=True, out_dtype=BF16))
        dh2 = mm("mm_ffn_in_dx", l, dgu, ex.weight("w_ffn_in", l), tb=True, add=dres)
        dres, dy2, dln_g[1], dln_b[1] = stage_bwd("lnres_bwd", l, _lnres_block, (t // rows,), sv["ln2"], ln_specs,
                                                  [(dh2,)], [row1(rows, d)], ln_grads)
        ex.grad("w_xo", l, mm("mm_sq_dw", l, sv["o"], dy2, ta=True, out_dtype=BF16))
        do = mm("mm_sq_dx", l, dy2, ex.weight("w_xo", l), tb=True, out_dtype=BF16)
        attn_ins, attn_specs, attn_out_spec = sv["attn"]
        dq, dkv = stage_bwd("attn_bwd", l, _attn_block, (nb, s // tq), attn_ins, attn_specs, [(do,)], [attn_out_spec],
                            [(0, (), BF16), (1, (1,), F32)])
        ex.grad("w_xq", l, mm("mm_sq_dw", l, sv["h1_bf"], dq, ta=True, out_dtype=BF16))
        dh1 = mm("mm_sq_dx", l, dq, ex.weight("w_xq", l), tb=True, add=dres)
        ex.grad("w_xkv", l, mm("mm_kv_dw", l, mem_n, dkv, ta=True, out_dtype=BF16))
        dmem_n.append(mm("mm_kv_dx", l, dkv, ex.weight("w_xkv", l), tb=True))
        dres, dy1, dln_g[0], dln_b[0] = stage_bwd("lnres_bwd", l, _lnres_block, (t // rows,), sv["ln1"], ln_specs,
                                                  [(dh1,)], [row1(rows, d)], ln_grads)
        g_small["ln_g"][l] = jnp.concatenate(dln_g, axis=0)
        g_small["ln_b"][l] = jnp.concatenate(dln_b, axis=0)
        ex.grad("w_mix_o", l, mm("mm_sq_dw", l, sv["merged"], dy1, ta=True, out_dtype=BF16))
        dmerged = mm("mm_sq_dx", l, dy1, ex.weight("w_mix_o", l), tb=True, out_dtype=BF16)
        merge_ins, merge_specs, merge_out_spec = sv["merge"]
        dproj, dbr_a, dbr_b = stage_bwd("merge_bwd", l, _merge_block, (t // rows_wide,), merge_ins, merge_specs,
                                        [(dmerged,)], [merge_out_spec],
                                        [(0, (), BF16, ((t, P_COLS), merge_specs[0])), (1, (), BF16), (2, (), BF16)])
        ex.grad("p_a", l, mm("mm_sq_dw", l, sv["a_out"], dbr_a, ta=True, out_dtype=BF16))
        da_out = mm("mm_sq_dx", l, dbr_a, ex.weight("p_a", l), tb=True, out_dtype=BF16)
        ex.grad("p_b", l, mm("mm_pb_dw", l, sv["y_ssd"], dbr_b, ta=True, out_dtype=BF16))
        dy_ssd = mm("mm_pb_dx", l, dbr_b, ex.weight("p_b", l), tb=True, out_dtype=BF16)
        sgu_ins, sgu_specs = sv["sgu"]
        dproj, dsg_ln_g, dsg_ln_b, dsg_w, dsg_b = stage_bwd(
            "sgu_bwd", l, _sgu_block, (t // CHUNK,), sgu_ins, sgu_specs, [(da_out,)], [row1(CHUNK, d)],
            [(0, (), BF16, ((t, P_COLS), sgu_specs[0]), dproj), (1, (0,), F32), (2, (0,), F32), (3, (0,), F32),
             (4, (0,), F32)])
        g_small["sg_ln_g"][l], g_small["sg_ln_b"][l], g_small["sg_w"][l], g_small["sg_b"][l] = (
            dsg_ln_g[0], dsg_ln_b[0], dsg_w, dsg_b.T)
        xc, dt_raw, prevs, y_pre, ssd_par = sv["ssd"]
        comm = ex.before("ssd_bwd", l)
        (dxc, ddt, dproj, ddtb, dal, dds, dng), comm_outs = _ssd_bwd(xc, dt_raw, sv["proj"], prevs, y_pre, *ssd_par,
                                                                     dy_ssd, dproj, nb, nc, comm=comm)
        if comm is not None:
            ex.after("ssd_bwd", l, comm_outs)
        g_small["dt_bias"][l], g_small["a_log"][l], g_small["d_skip"][l] = (
            ddtb[0, :SSM_HEADS], dal[0, :SSM_HEADS], dds[0, :SSM_HEADS])
        g_small["ssm_norm_g"][l] = dng[0]
        conv_ins, conv_specs, conv_out_spec = sv["conv"]
        dproj, dconv_w, dconv_b = stage_bwd("conv_bwd", l, _conv_block, (SSM_CONV_DIM // 256, nb), conv_ins, conv_specs,
                                            [(dxc,)], [conv_out_spec],
                                            [(0, (), BF16, ((t, P_COLS), conv_specs[0]), dproj), (1, (1,), F32),
                                             (2, (1,), F32)])
        g_small["conv_w"][l], g_small["conv_b"][l] = dconv_w, dconv_b[0]
        if l == 0:
            dmg, dmb = stage_bwd("memln_bwd", l, _memln_block, (nb * MEM_LEN // 256,), mem_ins, mem_specs,
                                 [tuple(dmem_n)], [row1(256, d)], [(1, (0,), F32), (2, (0,), F32)])
            done = {n: jnp.stack(g, axis=0) for n, g in g_small.items()}
            done["mem_ln_g"], done["mem_ln_b"] = dmg[0], dmb[0]
            ex.small_grads(done)
        w_p, w_dt = ex.weight("w_in", l)
        g_dt = mm("mm_dt_dw", l, sv["h_bf"], ddt, ta=True, out_dtype=BF16)
        for k, call in enumerate(("mm_in_dw_a", "mm_in_dw_b")):
            rows_k = slice(k * d // 2, (k + 1) * d // 2)
            g_main = mm(call, l, sv["h_bf"][:, rows_k], dproj, ta=True, out_dtype=BF16)
            ex.grad(("w_in", k), l, _unpack_w_in(g_main, g_dt[rows_k]))
        dh = mm("mm_in_dx", l, dproj, w_p, tb=True, add=mm("mm_dt_dx", l, ddt, w_dt, tb=True, add=dres))

    return loss, dh.reshape(nb, s, d)


def _pack_flat(arrays, rows):
    flat = jnp.concatenate([a.reshape(-1) for a in arrays])
    return jnp.pad(flat, (0, rows * 128 - flat.shape[0])).reshape(rows, 128)


def _unpack_flat(packed, shapes):
    lead = packed.shape[:-2]
    flat = packed.reshape(lead + (-1,))
    out, pos = [], 0
    for shape in shapes:
        n = math.prod(shape)
        out.append(flat[..., pos:pos + n].reshape(lead + tuple(shape)))
        pos += n
    return out


def _small_rows(n_elems):
    return -(-n_elems // (128 * SMALL_ROW_TILE)) * SMALL_ROW_TILE


def _from_shards(name, gathered):
    _, a, b = gathered.shape
    if name == "w_in":
        return tuple(_pack_w_in(gathered))
    if name in BIG_COL_SHARDED:
        return gathered.transpose(1, 0, 2).reshape(a, N_DEV * b)
    return gathered.reshape(N_DEV * a, b)


def _to_shards(name, g):
    if name in BIG_COL_SHARDED:
        a, nb = g.shape
        return g.reshape(a, N_DEV, nb // N_DEV).transpose(1, 0, 2)
    a, b = g.shape
    return g.reshape(N_DEV, a // N_DEV, b)


class _MeshExchange:
    def __init__(self, shards_bf16, first):
        self.shards = shards_bf16
        self.full = dict(first)
        self.pieces = {}
        self.grads = {}
        self.to_send = {}
        self.received = {}
        self.small = None
        self.small_gathered = None

    def weight(self, name, l):
        if (name, l) not in self.full:
            got = jnp.concatenate([self.pieces[(name, l, q)] for q in range(W_IN_PIECES)], axis=1)
            self.full[(name, l)] = _from_shards(name, got)
        return self.full[(name, l)]

    def grad(self, name, l, g):
        self.grads[(name, l)] = g

    def small_grads(self, done):
        self.small = done

    def partial_sums(self, name, l):
        if (name, l, None) in self.received:
            return self.received[(name, l, None)]
        return jnp.concatenate([self.received[(name, l, q)] for q in range(W_IN_PIECES)], axis=1)

    def _slices(self, name, l, piece):
        if piece is None:
            key, rows = (name, l), None
        else:
            per_half = W_IN_PIECES // 2
            n_rows = D_MODEL // W_IN_PIECES
            key, rows = ((name, piece // per_half), l), ((piece % per_half) * n_rows, n_rows)
        if key not in self.to_send:
            self.to_send[key] = self.grads[key] if piece is not None else _to_shards(name, self.grads[key])
        return self.to_send[key], rows

    def before(self, call, l):
        comm = _Comm()
        for name, layer, piece in GATHER_PLAN.get((call, l), ()):
            n_rows = D_MODEL // W_IN_PIECES
            comm.gathers.append((self.shards[name], layer, None if piece is None else (piece * n_rows, n_rows)))
        if (call, l) == SMALL_GATHER_CALL:
            names = SMALL_REP + SMALL_SH
            rows = _small_rows(sum(math.prod(self.small[n].shape) for n in names))
            comm.gathers.append((_pack_flat([self.small[n] for n in names], rows), None, None))
        for name, layer, piece in SCATTER_PLAN.get((call, l), ()):
            comm.scatters.append(self._slices(name, layer, piece))
        return comm if comm.gathers or comm.scatters else None

    def after(self, call, l, outs):
        gathers = list(GATHER_PLAN.get((call, l), ()))
        for (name, layer, piece), out in zip(gathers, outs):
            if piece is None:
                self.full[(name, layer)] = _from_shards(name, out)
            else:
                self.pieces[(name, layer, piece)] = out
        outs = outs[len(gathers):]
        if (call, l) == SMALL_GATHER_CALL:
            self.small_gathered = outs[0]
            outs = outs[1:]
        for item, out in zip(SCATTER_PLAN.get((call, l), ()), outs):
            self.received[item] = out


def kernel(x, mem, mem_ln_g, mem_ln_b, w_in, sg_ln_g, sg_ln_b, sg_w, sg_b, conv_w, conv_b, dt_bias, a_log, d_skip, ssm_norm_g, p_a, p_b, w_mix_o, w_xq, w_xkv, w_xo, w_ffn_in, w_ffn_out, ln_g, ln_b, loss_target, m_mem_ln_g, m_mem_ln_b, m_w_in, m_sg_ln_g, m_sg_ln_b, m_sg_w, m_sg_b, m_conv_w, m_conv_b, m_dt_bias, m_a_log, m_d_skip, m_ssm_norm_g, m_p_a, m_p_b, m_w_mix_o, m_w_xq, m_w_xkv, m_w_xo, m_w_ffn_in, m_w_ffn_out, m_ln_g, m_ln_b, v_mem_ln_g, v_mem_ln_b, v_w_in, v_sg_ln_g, v_sg_ln_b, v_sg_w, v_sg_b, v_conv_w, v_conv_b, v_dt_bias, v_a_log, v_d_skip, v_ssm_norm_g, v_p_a, v_p_b, v_w_mix_o, v_w_xq, v_w_xkv, v_w_xo, v_w_ffn_in, v_w_ffn_out, v_ln_g, v_ln_b):
    args = dict(locals())
    w = {n: args[n] for n in WEIGHTS}
    m = {n: args["m_" + n] for n in WEIGHTS}
    v = {n: args["v_" + n] for n in WEIGHTS}
    me = 4 * lax.axis_index("x") + 2 * lax.axis_index("y") + lax.axis_index("c")

    shards = {n: w[n].astype(BF16) for n in BIG}
    sh_shapes = [w[n].shape for n in SMALL_SH]
    first = _Comm()
    first.gathers.append((shards["w_in"], 0, None))
    first.gathers.append((_pack_flat([w[n] for n in SMALL_SH], _small_rows(sum(math.prod(s) for s in sh_shapes))), None,
                          None))
    w_in0, small_sh = _comm_only("gather_first", first)
    small = {n: w[n] for n in SMALL_REP}
    for n, sh in zip(SMALL_SH, _unpack_flat(small_sh, sh_shapes)):
        small[n] = sh.transpose(1, 2, 0, 3).reshape(sh.shape[1], sh.shape[2], N_DEV * sh.shape[3])

    ex = _MeshExchange(shards, {("w_in", 0): _from_shards("w_in", w_in0)})
    loss, grad_x = _run_step(x, mem, loss_target, small, ex)
    loss = lax.psum(loss[0, 0], ("x", "y", "c"))

    out = {}
    for n in BIG:
        out[n] = _adamw_sharded("adamw_" + n, [ex.partial_sums(n, l) for l in range(DEPTH)], w[n], m[n], v[n])
    names = SMALL_REP + SMALL_SH
    g_small = dict(zip(names, _unpack_flat(_sum_parts("sum_small_grads", ex.small_gathered),
                                           [ex.small[n].shape for n in names])))
    for n in names:
        g = g_small[n]
        if n in SMALL_SH:
            width = w[n].shape[-1]
            g = lax.dynamic_slice_in_dim(g, me * width, width, axis=-1)
        two_d = (-1, w[n].shape[-1])
        res = _adamw_small("adamw_" + n, g.reshape(two_d), w[n].reshape(two_d), m[n].reshape(two_d), v[n].reshape(two_d))
        out[n] = [g] + [r.reshape(w[n].shape) for r in res]

    results = []
    for k in range(4):
        results.extend(out[n][k] for n in WEIGHTS)
    return (loss, grad_x, *results)
```

```python
import functools
import math

import jax
import jax.numpy as jnp
from jax import lax
from jax.experimental import pallas as pl
from jax.experimental.pallas import tpu as pltpu

F32 = jnp.float32
BF16 = jnp.bfloat16
HIGHEST = lax.Precision.HIGHEST

N_DEV = 8
D_MODEL = 1024
DEPTH = 2
MEM_LEN = 256
CHUNK = 128
SG_GROUPS = 8
SSM_INNER = 2048
SSM_HEADDIM = 64
SSM_HEADS = 32
SSM_STATE = 128
SSM_GROUPS = 4
SSM_RPG = 8
SSM_CONV = 4
SSM_CONV_DIM = 3072
X_HEADS = 4
X_HEADDIM = 256
FFN_HIDDEN = 2816
ALPHA = float((2 * DEPTH) ** 0.25)
LN_EPS = 1e-5
RMS_EPS = 1e-5
XBC_COL = 4096
DT_COL = 7168
GA_COL = 7200
IN_COLS = 9248
P_GATE = 4096
P_XBC = 6144
P_COLS = 9216
DT_LANES = 128

ADAM_LR = 0.001
ADAM_B1 = 0.9
ADAM_B2 = 0.999
ADAM_EPS = 1e-08
ADAM_WD = 0.01
ADAM_STEP = 10

VMEM_LIMIT = 48 * 1024 * 1024
SMALL_ROW_TILE = 256

BIG = ("w_in", "p_a", "p_b", "w_mix_o", "w_xq", "w_xkv", "w_xo", "w_ffn_in", "w_ffn_out")
BIG_COL_SHARDED = ("w_in", "w_xkv", "w_ffn_in")
SMALL_REP = ("mem_ln_g", "mem_ln_b", "sg_ln_g", "sg_ln_b", "sg_w", "sg_b", "conv_b", "dt_bias", "a_log", "d_skip",
             "ssm_norm_g")
SMALL_SH = ("conv_w", "ln_g", "ln_b")
WEIGHTS = ("mem_ln_g", "mem_ln_b", "w_in", "sg_ln_g", "sg_ln_b", "sg_w", "sg_b", "conv_w", "conv_b", "dt_bias", "a_log",
           "d_skip", "ssm_norm_g", "p_a", "p_b", "w_mix_o", "w_xq", "w_xkv", "w_xo", "w_ffn_in", "w_ffn_out", "ln_g", "ln_b")

W_IN_PIECES = 4
GATHER_PLAN = {("sgu_fwd", 0): [("w_in", 1, 0)], ("conv_fwd", 0): [("w_in", 1, 1)],
               ("mm_ffn_in", 0): [("w_in", 1, 2)], ("swiglu_fwd", 0): [("w_in", 1, 3)]}
SCATTER_PLAN = {("mm_ffn_in_dw", 0): [("w_in", 1, 2)], ("mm_ffn_in_dx", 0): [("w_in", 1, 3)],
                ("adamw_w_ffn_in", 0): [("w_in", 0, 3)]}
for _l in range(DEPTH):
    GATHER_PLAN[("mm_in", _l)] = [(n, _l, None) for n in ("p_a", "p_b", "w_mix_o", "w_xq", "w_xkv", "w_xo")]
    GATHER_PLAN[("ssd_fwd", _l)] = [("w_ffn_in", _l, None), ("w_ffn_out", _l, None)]
    SCATTER_PLAN[("swiglu_bwd", _l)] = [("w_ffn_out", _l, None)]
    SCATTER_PLAN[("sgu_bwd", _l)] = [("w_mix_o", _l, None), ("p_a", _l, None), ("w_xo", _l, None)]
    SCATTER_PLAN[("ssd_bwd", _l)] = [("w_ffn_in", _l, None), ("w_xkv", _l, None), ("w_xq", _l, None)]
    SCATTER_PLAN[("conv_bwd", _l)] = [("p_b", _l, None)]
    SCATTER_PLAN[("mm_in_dx", _l)] = [("w_in", _l, 0), ("w_in", _l, 1)] + ([("w_in", _l, 2)] if _l == 0 else [])
SMALL_GATHER_CALL = ("mm_in_dw", 0)


def _layer_norm(x, g, b):
    mu = jnp.mean(x, axis=-1, keepdims=True)
    xc = x - mu
    var = jnp.mean(xc * xc, axis=-1, keepdims=True)
    return xc * lax.rsqrt(var + LN_EPS) * g + b


def _gelu(x):
    return 0.5 * x * (1.0 + lax.erf(x * (1.0 / math.sqrt(2.0))))


def _sigmoid(x):
    return 0.5 * jnp.tanh(0.5 * x) + 0.5


def _silu(x):
    return x * _sigmoid(x)


def _softplus(x):
    return jnp.maximum(x, 0.0) + jnp.log1p(jnp.exp(-jnp.abs(x)))


def _causal_mask():
    r = lax.broadcasted_iota(jnp.int32, (CHUNK, CHUNK), 0)
    c = lax.broadcasted_iota(jnp.int32, (CHUNK, CHUNK), 1)
    return r >= c


def _sgu_block(uv, ln_g, ln_b, w, sb):
    gu = _gelu(uv[:, :D_MODEL])
    vn = _layer_norm(_gelu(uv[:, D_MODEL:]), ln_g, ln_b)
    causal = _causal_mask()
    width = D_MODEL // SG_GROUPS
    outs = []
    for g in range(SG_GROUPS):
        wg = jnp.where(causal, w[g], 0.0).astype(BF16)
        mixed = jnp.dot(wg, vn[:, g * width:(g + 1) * width].astype(BF16), preferred_element_type=F32)
        outs.append(mixed + sb[:, g:g + 1])
    return (gu * jnp.concatenate(outs, axis=1),)


GROUP_W = SSM_RPG * SSM_HEADDIM
NT_DIMS = (((1,), (1,)), ((), ()))
TN_DIMS = (((0,), (0,)), ((), ()))


def _mxu(a, b, dims=(((1,), (0,)), ((), ()))):
    return lax.dot_general(a.astype(BF16), b.astype(BF16), dims, preferred_element_type=F32)


def _head_expander():
    return (jnp.arange(SSM_INNER)[None, :] // SSM_HEADDIM == jnp.arange(128)[:, None]).astype(BF16)


def _bf16_terms(x, n):
    terms = []
    for _ in range(n):
        t = x.astype(BF16)
        terms.append(t)
        x = x - t.astype(F32)
    return terms


def _expand_heads(q, e):
    return sum(jnp.dot(t, e, preferred_element_type=F32) for t in _bf16_terms(q, 2))


def _reduce_heads(v, e):
    return sum(lax.dot_general(t, e, NT_DIMS, preferred_element_type=F32) for t in _bf16_terms(v, 2))


def _reduce_heads_of_column_sums(v, e):
    sums = jnp.broadcast_to(jnp.sum(v, axis=0, keepdims=True), (8, v.shape[1]))
    return _reduce_heads(sums, e)[0:1, :]


def _ssd_common(xc, dtraw, dt_bias, a_log, e):
    xs = xc[:, :SSM_INNER]
    pre = dtraw + dt_bias
    dt = _softplus(pre)
    a = -jnp.exp(a_log)
    r_i = lax.broadcasted_iota(jnp.int32, (CHUNK, CHUNK), 0)
    c_i = lax.broadcasted_iota(jnp.int32, (CHUNK, CHUNK), 1)
    tril = jnp.where(r_i >= c_i, 1.0, 0.0).astype(F32)
    cs = jnp.dot(tril, dt * a, precision=HIGHEST, preferred_element_type=F32)
    cs_last = cs[CHUNK - 1:CHUNK, :]
    decay_in = jnp.exp(cs)
    decay_st = jnp.exp(cs_last - cs)
    dt_x = _expand_heads(dt, e)
    w_st_x = _expand_heads(dt * decay_st, e)
    decay_in_x = _expand_heads(decay_in, e)
    return dict(xs=xs, pre=pre, dt=dt, a=a, lower=r_i >= c_i, upper=c_i >= r_i, cs=cs, cs_t=cs.T, decay_in=decay_in,
                decay_st=decay_st, chunk_decay=jnp.exp(cs_last), dt_x=dt_x, w_st_x=w_st_x, decay_in_x=decay_in_x,
                chunk_decay_x=decay_in_x[CHUNK - 1:CHUNK, :], xdt=xs * dt_x, x_st=(xs * w_st_x).astype(BF16),
                low=lax.broadcasted_iota(jnp.int32, (CHUNK, 128), 1) < SSM_HEADDIM)


def _pair_decay(c, h):
    return jnp.exp(jnp.where(c["lower"], c["cs"][:, h:h + 1] - c["cs_t"][h:h + 1, :], -1e30))


def _pair_decay_t(c, h):
    return jnp.exp(jnp.where(c["upper"], c["cs_t"][h:h + 1, :] - c["cs"][:, h:h + 1], -1e30))


def _ssd_forward(xc, dtraw, z, prev, dt_bias, a_log, d_skip_x, norm_g, e):
    c = _ssd_common(xc, dtraw, dt_bias, a_log, e)
    y_groups, new_states = [], []
    for g in range(SSM_GROUPS):
        lanes = slice(g * GROUP_W, (g + 1) * GROUP_W)
        bg = xc[:, SSM_INNER + g * SSM_STATE:SSM_INNER + (g + 1) * SSM_STATE]
        cg = xc[:, SSM_INNER + (SSM_GROUPS + g) * SSM_STATE:SSM_INNER + (SSM_GROUPS + g + 1) * SSM_STATE].astype(BF16)
        pg = prev[g * SSM_STATE:(g + 1) * SSM_STATE, :]
        cb = _mxu(cg, bg, NT_DIMS)
        y_in = _mxu(cg, pg) * c["decay_in_x"][:, lanes]
        new_states.append(pg * c["chunk_decay_x"][:, lanes] + _mxu(bg.T, c["x_st"][:, lanes]))
        pairs = []
        for j in range(SSM_RPG // 2):
            h0 = g * SSM_RPG + 2 * j
            xp = c["xdt"][:, 128 * (h0 // 2):128 * (h0 // 2 + 1)]
            pairs.append(_mxu(cb * _pair_decay(c, h0), jnp.where(c["low"], xp, 0.0))
                         + _mxu(cb * _pair_decay(c, h0 + 1), jnp.where(c["low"], 0.0, xp)))
        y_groups.append(jnp.concatenate(pairs, axis=1) + y_in)
    y_pre = jnp.concatenate(y_groups, axis=1) + c["xs"] * d_skip_x
    gated = y_pre * _silu(z)
    normed = [gated[:, g * GROUP_W:(g + 1) * GROUP_W] for g in range(SSM_GROUPS)]
    normed = [yg * lax.rsqrt(jnp.mean(yg * yg, axis=-1, keepdims=True) + RMS_EPS) for yg in normed]
    return jnp.concatenate(normed, axis=1) * norm_g, y_pre, jnp.concatenate(new_states, axis=0)


def _ssd_backward(xc, dtraw, z, prev, y_pre, dt_bias, a_log, d_skip_x, norm_g, e, dout, dnew):
    c = _ssd_common(xc, dtraw, dt_bias, a_log, e)
    xs = c["xs"]
    sig = _sigmoid(z)
    silu_z = z * sig
    gated = y_pre * silu_z
    d_gated, normed = [], []
    for g in range(SSM_GROUPS):
        lanes = slice(g * GROUP_W, (g + 1) * GROUP_W)
        yg = gated[:, lanes]
        r = lax.rsqrt(jnp.mean(yg * yg, axis=-1, keepdims=True) + RMS_EPS)
        n = yg * r
        gh = dout[:, lanes] * norm_g[:, lanes]
        d_gated.append(r * (gh - n * jnp.mean(gh * n, axis=-1, keepdims=True)))
        normed.append(n)
    d_gated = jnp.concatenate(d_gated, axis=1)
    dnorm_g = jnp.sum(dout * jnp.concatenate(normed, axis=1), axis=0, keepdims=True)
    dy = d_gated * silu_z
    dz = d_gated * y_pre * (sig * (1.0 + z * (1.0 - sig)))
    dxs = dy * d_skip_x
    dd_skip = _reduce_heads_of_column_sums(dy * xs, e)

    lane = lax.broadcasted_iota(jnp.int32, (CHUNK, 128), 1)
    sub = lax.broadcasted_iota(jnp.int32, (8, 128), 0)
    dcs_neg = jnp.zeros((CHUNK, 128), F32)
    row_slabs = []
    dxdt, dx_st, d_decay_in_x, dprev, d_chunk_decay_x, db_all, dc_all = [], [], [], [], [], [], []
    for g in range(SSM_GROUPS):
        lanes = slice(g * GROUP_W, (g + 1) * GROUP_W)
        bg = xc[:, SSM_INNER + g * SSM_STATE:SSM_INNER + (g + 1) * SSM_STATE].astype(BF16)
        cg_f = xc[:, SSM_INNER + (SSM_GROUPS + g) * SSM_STATE:SSM_INNER + (SSM_GROUPS + g + 1) * SSM_STATE]
        cg = cg_f.astype(BF16)
        pg = prev[g * SSM_STATE:(g + 1) * SSM_STATE, :]
        dng = dnew[g * SSM_STATE:(g + 1) * SSM_STATE, :]
        dy_g = dy[:, lanes]
        cb_t = _mxu(bg, cg, NT_DIMS)
        t1 = (dy_g * c["decay_in_x"][:, lanes]).astype(BF16)
        d_decay_in_x.append(dy_g * _mxu(cg, pg))
        dc = _mxu(t1, pg, NT_DIMS)
        dprev.append(_mxu(cg_f.T, t1) + dng * c["chunk_decay_x"][:, lanes])
        d_chunk_decay_x.append(dng * pg)
        db = _mxu(c["x_st"][:, lanes], dng, NT_DIMS)
        dx_st.append(_mxu(bg, dng))
        dcb_t = jnp.zeros((CHUNK, CHUNK), F32)
        rows = []
        for j in range(SSM_RPG // 2):
            h0 = g * SSM_RPG + 2 * j
            blk = slice(128 * (h0 // 2), 128 * (h0 // 2 + 1))
            xp = c["xdt"][:, blk]
            dyp = dy[:, blk].astype(BF16)
            pair_dx = []
            for k, xk in enumerate((jnp.where(c["low"], xp, 0.0), jnp.where(c["low"], 0.0, xp))):
                dec_t = _pair_decay_t(c, h0 + k)
                pair_dx.append(_mxu(cb_t * dec_t, dyp))
                dml_t = _mxu(xk, dyp, NT_DIMS) * dec_t
                dcb_t = dcb_t + dml_t
                dseg_t = dml_t * cb_t
                dcs_neg = dcs_neg + jnp.where(lane == h0 + k, jnp.sum(dseg_t, axis=-1, keepdims=True), 0.0)
                rows.append(jnp.sum(dseg_t, axis=0, keepdims=True))
            dxdt.append(jnp.where(c["low"], pair_dx[0], pair_dx[1]))
        slab = jnp.zeros((8, 128), F32)
        for r in range(SSM_RPG):
            slab = slab + jnp.where(sub == r, rows[r], 0.0)
        row_slabs.append(slab)
        dc_all.append(dc + _mxu(dcb_t.T, bg))
        db_all.append(db + _mxu(dcb_t, cg))
    dxdt = jnp.concatenate(dxdt, axis=1)
    dx_st = jnp.concatenate(dx_st, axis=1)
    by_head = jnp.concatenate(row_slabs + [jnp.zeros((CHUNK - SSM_HEADS, 128), F32)], axis=0)
    dcs = by_head.T - dcs_neg
    dxs = dxs + dxdt * c["dt_x"] + dx_st * c["w_st_x"]
    ddt = _reduce_heads(dxdt * xs, e)
    dw_st = _reduce_heads(dx_st * xs, e)
    dcs = dcs + _reduce_heads(jnp.concatenate(d_decay_in_x, axis=1), e) * c["decay_in"]
    ddt = ddt + dw_st * c["decay_st"]
    d_log_st = dw_st * c["dt"] * c["decay_st"]
    dcs = dcs - d_log_st
    d_chunk_decay = _reduce_heads_of_column_sums(jnp.concatenate(d_chunk_decay_x, axis=1), e)
    dcs_last = jnp.sum(d_log_st, axis=0, keepdims=True) + d_chunk_decay * c["chunk_decay"]
    row = lax.broadcasted_iota(jnp.int32, (CHUNK, 128), 0)
    dcs = dcs + jnp.where(row == CHUNK - 1, dcs_last, 0.0)
    triu = jnp.where(c["upper"], 1.0, 0.0).astype(F32)
    dda = jnp.dot(triu, dcs, precision=HIGHEST, preferred_element_type=F32)
    ddt = ddt + dda * c["a"]
    da_log = jnp.sum(dda * c["dt"], axis=0, keepdims=True) * c["a"]
    dpre = ddt * _sigmoid(c["pre"])
    dxc = jnp.concatenate([dxs] + db_all + dc_all, axis=1)
    return (dxc, dpre, dz, jnp.concatenate(dprev, axis=0), jnp.sum(dpre, axis=0, keepdims=True), da_log, dd_skip,
            dnorm_g)


def _conv_block(x, w, b):
    rows = lax.broadcasted_iota(jnp.int32, x.shape, 0)
    acc = x * w[SSM_CONV - 1:SSM_CONV, :] + b
    for k in range(SSM_CONV - 1):
        shift = SSM_CONV - 1 - k
        acc = acc + _shift_rows(x, rows, shift) * w[k:k + 1, :]
    return (_silu(acc),)


@functools.partial(jax.custom_vjp, nondiff_argnums=(2,))
def _shift_rows(x, rows, shift):
    return jnp.where(rows >= shift, pltpu.roll(x, shift, 0), 0.0)


def _shift_rows_fwd(x, rows, shift):
    return _shift_rows(x, rows, shift), rows


def _shift_rows_bwd(shift, rows, g):
    n = g.shape[0]
    return jnp.where(rows < n - shift, pltpu.roll(g, n - shift, 0), 0.0), None


_shift_rows.defvjp(_shift_rows_fwd, _shift_rows_bwd)


def _merge_block(gates, br_a, br_b):
    return (_sigmoid(gates[:, :D_MODEL]) * br_a + _sigmoid(gates[:, D_MODEL:]) * br_b,)


def _lnres_block(x, y, g, b):
    return (_layer_norm(ALPHA * x + y, g, b),)


def _memln_block(x, g, b):
    return (_layer_norm(x, g, b),)


def _attn_block(q, kv):
    outs = []
    for h in range(X_HEADS):
        qh = q[:, h * X_HEADDIM:(h + 1) * X_HEADDIM].astype(BF16)
        kh = kv[:, h * X_HEADDIM:(h + 1) * X_HEADDIM].astype(BF16)
        vh = kv[:, D_MODEL + h * X_HEADDIM:D_MODEL + (h + 1) * X_HEADDIM].astype(BF16)
        s = lax.dot_general(qh, kh, (((1,), (1,)), ((), ())), preferred_element_type=F32) * (X_HEADDIM ** -0.5)
        s = s - lax.stop_gradient(jnp.max(s, axis=-1, keepdims=True))
        e = jnp.exp(s)
        p = e / jnp.sum(e, axis=-1, keepdims=True)
        outs.append(jnp.dot(p.astype(BF16), vh, preferred_element_type=F32))
    return (jnp.concatenate(outs, axis=1),)


def _swiglu_block(gu):
    return (_silu(gu[:, :FFN_HIDDEN]) * gu[:, FFN_HIDDEN:],)


class _Comm:
    def __init__(self):
        self.gathers = []
        self.scatters = []

    @staticmethod
    def _rows(ref, rows):
        return ref if rows is None else ref.at[pl.ds(rows[0], rows[1])]

    def operands(self):
        ins = [a for a, _, _ in self.gathers] + [a for a, _ in self.scatters]
        shapes = []
        for a, idx, rows in self.gathers:
            blk = a.shape if idx is None else a.shape[1:]
            shapes.append(jax.ShapeDtypeStruct((N_DEV, blk[0] if rows is None else rows[1]) + tuple(blk[1:]), a.dtype))
        for a, rows in self.scatters:
            shapes.append(jax.ShapeDtypeStruct((N_DEV, a.shape[1] if rows is None else rows[1]) + tuple(a.shape[2:]),
                                               a.dtype))
        scratch = []
        for n in (len(self.gathers), len(self.scatters)):
            if n:
                scratch += [pltpu.SemaphoreType.DMA((7 * n,)), pltpu.SemaphoreType.DMA((7 * n,)),
                            pltpu.SemaphoreType.DMA((n,))]
        return ins, shapes, scratch

    def _split(self, in_refs, out_refs, sems):
        ng = len(self.gathers)
        g_sems = sems[:3] if ng else None
        s_sems = sems[3:] if ng else sems
        return in_refs[:ng], in_refs[ng:], out_refs[:ng], out_refs[ng:], g_sems, s_sems

    def _gather_copies(self, i, src_ref, out_ref, sems):
        send_sems, recv_sems, local_sems = sems
        x, y, c = lax.axis_index("x"), lax.axis_index("y"), lax.axis_index("c")
        me, sibling = (x, y, c), (x, y, 1 - c)
        chips = [(1 - x, y), (x, 1 - y), (1 - x, 1 - y)]
        _, idx, rows = self.gathers[i]
        src = self._rows(src_ref if idx is None else src_ref.at[idx], rows)

        def slot(px, py, pc):
            return out_ref.at[4 * px + 2 * py + pc]

        def copy(k, blk, to, from_src=False):
            return pltpu.make_async_remote_copy(
                src_ref=src if from_src else slot(*blk), dst_ref=slot(*blk), send_sem=send_sems.at[7 * i + k],
                recv_sem=recv_sems.at[7 * i + k], device_id=to, device_id_type=pl.DeviceIdType.MESH)

        mine = pltpu.make_async_copy(src, slot(*me), local_sems.at[i])
        first = [copy(0, me, sibling, True)] + [copy(1 + j, me, (*chip, c), True) for j, chip in enumerate(chips)]
        passed = [copy(4 + j, (*chip, c), sibling) for j, chip in enumerate(chips)]
        arrivals = [copy(1 + j, (*chip, c), me) for j, chip in enumerate(chips)]
        from_sibling = [copy(0, sibling, me)] + [copy(4 + j, (*chip, 1 - c), me) for j, chip in enumerate(chips)]
        return mine, first, passed, arrivals, from_sibling

    def _scatter_copies(self, i, src_ref, out_ref, sems):
        send_sems, recv_sems, local_sems = sems
        x, y, c = lax.axis_index("x"), lax.axis_index("y"), lax.axis_index("c")
        me = 4 * x + 2 * y + c
        rows = self.scatters[i][1]
        mine = pltpu.make_async_copy(self._rows(src_ref.at[me], rows), out_ref.at[me], local_sems.at[i])
        copies = []
        for k in range(1, N_DEV):
            px = 1 - x if k & 4 else x
            py = 1 - y if k & 2 else y
            pc = 1 - c if k & 1 else c
            copies.append(pltpu.make_async_remote_copy(
                src_ref=self._rows(src_ref.at[4 * px + 2 * py + pc], rows), dst_ref=out_ref.at[me],
                send_sem=send_sems.at[7 * i + k - 1], recv_sem=recv_sems.at[7 * i + k - 1], device_id=(px, py, pc),
                device_id_type=pl.DeviceIdType.MESH))
        return mine, copies

    def start(self, in_refs, out_refs, sems):
        g_in, s_in, g_out, s_out, g_sems, s_sems = self._split(in_refs, out_refs, sems)
        for i in range(len(self.gathers)):
            mine, first, _, _, _ = self._gather_copies(i, g_in[i], g_out[i], g_sems)
            mine.start()
            for cp in first:
                cp.start()
        for i in range(len(self.scatters)):
            mine, copies = self._scatter_copies(i, s_in[i], s_out[i], s_sems)
            mine.start()
            for cp in copies:
                cp.start()

    def finish(self, in_refs, out_refs, sems):
        g_in, s_in, g_out, s_out, g_sems, s_sems = self._split(in_refs, out_refs, sems)
        parts = [self._gather_copies(i, g_in[i], g_out[i], g_sems) for i in range(len(self.gathers))]
        for j in range(3):
            for _, _, passed, arrivals, _ in parts:
                arrivals[j].wait_recv()
                passed[j].start()
        for mine, first, passed, _, from_sibling in parts:
            for cp in from_sibling:
                cp.wait_recv()
            for cp in first + passed:
                cp.wait_send()
            mine.wait()
        for i in range(len(self.scatters)):
            mine, copies = self._scatter_copies(i, s_in[i], s_out[i], s_sems)
            for cp in copies:
                cp.wait_recv()
            for cp in copies:
                cp.wait_send()
            mine.wait()


def _params(grid):
    return pltpu.CompilerParams(dimension_semantics=("arbitrary",) * len(grid), vmem_limit_bytes=VMEM_LIMIT)


def _call(name, body, *, grid, ins, in_specs, out_shape, out_specs, scratch=(), comm=None, aliases=None):
    n_in, n_out, n_scr = len(ins), len(out_shape), len(scratch)
    aliases = aliases or {}
    if comm is None:
        outs = pl.pallas_call(body, grid=grid, in_specs=list(in_specs), out_specs=list(out_specs),
                              out_shape=list(out_shape), scratch_shapes=list(scratch), name=name,
                              input_output_aliases=aliases, compiler_params=_params(grid))(*ins)
        return list(outs), []
    c_ins, c_shapes, c_scratch = comm.operands()
    nci, nco = len(c_ins), len(c_shapes)
    anywhere = pl.BlockSpec(memory_space=pl.ANY)

    def carrier(*refs):
        main_in, comm_in = refs[:n_in], refs[n_in:n_in + nci]
        o0 = n_in + nci
        main_out, comm_out = refs[o0:o0 + n_out], refs[o0 + n_out:o0 + n_out + nco]
        s0 = o0 + n_out + nco
        main_scr, comm_scr = refs[s0:s0 + n_scr], refs[s0 + n_scr:]
        first = pl.program_id(0) == 0
        last = pl.program_id(0) == grid[0] - 1
        for ax in range(1, len(grid)):
            first = first & (pl.program_id(ax) == 0)
            last = last & (pl.program_id(ax) == grid[ax] - 1)

        @pl.when(first)
        def _():
            comm.start(comm_in, comm_out, comm_scr)

        body(*main_in, *main_out, *main_scr)

        @pl.when(last)
        def _():
            comm.finish(comm_in, comm_out, comm_scr)

    outs = pl.pallas_call(carrier, grid=grid, in_specs=list(in_specs) + [anywhere] * nci,
                          out_specs=list(out_specs) + [anywhere] * nco, out_shape=list(out_shape) + c_shapes,
                          scratch_shapes=list(scratch) + c_scratch, name=name, input_output_aliases=aliases,
                          compiler_params=_params(grid))(*ins, *c_ins)
    return list(outs[:n_out]), list(outs[n_out:])


def _comm_only(name, comm):
    c_ins, c_shapes, c_scratch = comm.operands()
    nci, nco = len(c_ins), len(c_shapes)
    anywhere = pl.BlockSpec(memory_space=pl.ANY)

    def body(*refs):
        comm.start(refs[:nci], refs[nci:nci + nco], refs[nci + nco:])
        comm.finish(refs[:nci], refs[nci:nci + nco], refs[nci + nco:])

    return list(pl.pallas_call(body, in_specs=[anywhere] * nci, out_specs=[anywhere] * nco, out_shape=c_shapes,
                               scratch_shapes=c_scratch, name=name)(*c_ins))


def _stage_fwd(name, f, grid, ins, in_specs, out_shapes, out_specs, comm=None):
    n_in = len(ins)

    def body(*refs):
        res = f(*[r[...].astype(F32) for r in refs[:n_in]])
        for o_ref, val in zip(refs[n_in:], res):
            o_ref[...] = val.astype(o_ref.dtype)

    return _call(name, body, grid=grid, ins=ins, in_specs=in_specs, out_shape=out_shapes, out_specs=out_specs, comm=comm)


def _stage_bwd(name, f, grid, ins, in_specs, cts, ct_specs, grads, comm=None):
    n_in = len(ins)
    flat_cts = [c for group in cts for c in group]
    flat_ct_specs = [s for group, spec in zip(cts, ct_specs) for s in (spec,) * len(group)]
    n_ct = len(flat_cts)
    diff = [g[0] for g in grads]
    buffers = [(k, g[4]) for k, g in enumerate(grads) if len(g) > 4]
    n_buf = len(buffers)

    def body(*refs):
        vals = [r[...].astype(F32) for r in refs[:n_in]]
        ct_refs = refs[n_in:n_in + n_ct]
        g_refs = refs[n_in + n_ct + n_buf:]
        ct_vals, pos = [], 0
        for group in cts:
            acc = ct_refs[pos][...].astype(F32)
            for j in range(1, len(group)):
                acc = acc + ct_refs[pos + j][...].astype(F32)
            ct_vals.append(acc)
            pos += len(group)

        def g_fn(*dvals):
            full = list(vals)
            for i, dv in zip(diff, dvals):
                full[i] = dv
            return f(*full)

        _, vjp = jax.vjp(g_fn, *[vals[i] for i in diff])
        gvals = vjp(tuple(ct_vals))
        for gspec, g_ref, gval in zip(grads, g_refs, gvals):
            acc_axes = gspec[1]
            if not acc_axes:
                g_ref[...] = gval.astype(g_ref.dtype)
            else:
                first = pl.program_id(acc_axes[0]) == 0
                for ax in acc_axes[1:]:
                    first = first & (pl.program_id(ax) == 0)

                @pl.when(first)
                def _():
                    g_ref[...] = jnp.zeros_like(g_ref)

                g_ref[...] += gval.astype(g_ref.dtype)

    out_shapes, out_specs = [], []
    for gspec in grads:
        shape, spec = gspec[3] if len(gspec) > 3 else (ins[gspec[0]].shape, in_specs[gspec[0]])
        out_shapes.append(jax.ShapeDtypeStruct(shape, gspec[2]))
        out_specs.append(spec)
    anywhere = pl.BlockSpec(memory_space=pl.ANY)
    return _call(name, body, grid=grid, ins=list(ins) + flat_cts + [b for _, b in buffers],
                 in_specs=list(in_specs) + flat_ct_specs + [anywhere] * n_buf, out_shape=out_shapes, out_specs=out_specs,
                 comm=comm, aliases={n_in + n_ct + j: k for j, (k, _) in enumerate(buffers)})


def _pick_tile(n, candidates):
    for c in candidates:
        if n % c == 0:
            return c
    return n


def _matmul(name, a, b, *, ta=False, tb=False, add=None, out_dtype=F32, comm=None):
    if ta:
        k_dim, m = a.shape
    else:
        m, k_dim = a.shape
    n = b.shape[0] if tb else b.shape[1]
    assert (b.shape[1] if tb else b.shape[0]) == k_dim and not (ta and tb)
    tm = _pick_tile(m, (1024, 1408, 512, 256, 128))
    tn = _pick_tile(n, (1024, 1408, 512, 256, 128))
    if ta:
        tk = _pick_tile(k_dim, (1024, 512, 256, 128))
    elif k_dim <= 2816:
        tk = k_dim
    else:
        tk = _pick_tile(k_dim, (1408, 1024, 512, 256, 128))
    nk = k_dim // tk
    grid = (m // tm, n // tn, nk)
    a_spec = pl.BlockSpec((tk, tm), lambda i, j, k: (k, i)) if ta else pl.BlockSpec((tm, tk), lambda i, j, k: (i, k))
    b_spec = pl.BlockSpec((tn, tk), lambda i, j, k: (j, k)) if tb else pl.BlockSpec((tk, tn), lambda i, j, k: (k, j))
    o_spec = pl.BlockSpec((tm, tn), lambda i, j, k: (i, j))
    dims = (((0 if ta else 1,), (1 if tb else 0,)), ((), ()))
    has_add = add is not None

    def body(*refs):
        a_ref, b_ref = refs[0], refs[1]
        add_ref = refs[2] if has_add else None
        o_ref, acc_ref = refs[-2], refs[-1]
        k = pl.program_id(2)
        part = lax.dot_general(a_ref[...].astype(BF16), b_ref[...].astype(BF16), dims, preferred_element_type=F32)

        def finish(res):
            if has_add:
                res = res + add_ref[...].astype(F32)
            o_ref[...] = res.astype(o_ref.dtype)

        if nk == 1:
            finish(part)
        else:
            @pl.when(k == 0)
            def _():
                acc_ref[...] = part

            @pl.when((k > 0) & (k < nk - 1))
            def _():
                acc_ref[...] += part

            @pl.when(k == nk - 1)
            def _():
                finish(acc_ref[...] + part)

    ins = [a, b] + ([add] if has_add else [])
    in_specs = [a_spec, b_spec] + ([o_spec] if has_add else [])
    acc_shape = (tm, tn) if nk > 1 else (8, 128)
    outs, comm_outs = _call(name, body, grid=grid, ins=ins, in_specs=in_specs,
                            out_shape=[jax.ShapeDtypeStruct((m, n), out_dtype)], out_specs=[o_spec],
                            scratch=[pltpu.VMEM(acc_shape, F32)], comm=comm)
    return outs[0], comm_outs


def _matmul_lnres(name, a, b, x, g, beta):
    m, k_dim = a.shape
    n = b.shape[1]
    tm = _pick_tile(m, (512, 256, 128))
    row = lambda w: pl.BlockSpec((tm, w), lambda i: (i, 0))
    whole = lambda shape: pl.BlockSpec(shape, lambda i: (0, 0))

    def body(a_ref, b_ref, x_ref, g_ref, beta_ref, y_ref, h_ref, hb_ref):
        y = jnp.dot(a_ref[...].astype(BF16), b_ref[...].astype(BF16), preferred_element_type=F32).astype(y_ref.dtype)
        y_ref[...] = y
        (h,) = _lnres_block(x_ref[...], y.astype(F32), g_ref[...], beta_ref[...])
        h_ref[...] = h
        hb_ref[...] = h.astype(hb_ref.dtype)

    sds = jax.ShapeDtypeStruct
    return _call(name, body, grid=(m // tm,), ins=[a, b, x, g, beta],
                 in_specs=[row(k_dim), whole((k_dim, n)), row(n), whole((1, n)), whole((1, n))],
                 out_shape=[sds((m, n), BF16), sds((m, n), F32), sds((m, n), BF16)], out_specs=[row(n)] * 3)[0]


SSD_STATE = (SSM_GROUPS * SSM_STATE, SSM_RPG * SSM_HEADDIM)


def _ssd_fwd(xc, dt_raw, proj, dt_bias, a_log, d_skip, norm_g, nb, nc, comm=None):
    t = xc.shape[0]
    row = lambda b, c: (b * nc + c, 0)
    par = lambda shape: pl.BlockSpec(shape, lambda b, c: (0, 0))

    def body(xc_ref, dt_ref, z_ref, dtb_ref, al_ref, ds_ref, ng_ref, e_ref, y_ref, ypre_ref, prev_ref, st_ref):
        @pl.when(pl.program_id(1) == 0)
        def _():
            st_ref[...] = jnp.zeros_like(st_ref)

        prev = st_ref[...]
        prev_ref[0, 0] = prev
        y, y_pre, new_state = _ssd_forward(xc_ref[...], dt_ref[...], z_ref[...].astype(F32), prev, dtb_ref[...],
                                           al_ref[...], ds_ref[...], ng_ref[...], e_ref[...])
        y_ref[...] = y.astype(y_ref.dtype)
        ypre_ref[...] = y_pre
        st_ref[...] = new_state

    return _call(
        "ssd_fwd", body, grid=(nb, nc), ins=[xc, dt_raw, proj, dt_bias, a_log, d_skip, norm_g, _head_expander()],
        in_specs=[pl.BlockSpec((CHUNK, SSM_CONV_DIM), row), pl.BlockSpec((CHUNK, 128), row),
                  pl.BlockSpec((CHUNK, SSM_INNER), lambda b, c: (b * nc + c, 1)),
                  par((1, 128)), par((1, 128)), par((1, SSM_INNER)), par((1, SSM_INNER)), par((128, SSM_INNER))],
        out_specs=[pl.BlockSpec((CHUNK, SSM_INNER), row), pl.BlockSpec((CHUNK, SSM_INNER), row),
                   pl.BlockSpec((1, 1) + SSD_STATE, lambda b, c: (b, c, 0, 0))],
        out_shape=[jax.ShapeDtypeStruct((t, SSM_INNER), BF16), jax.ShapeDtypeStruct((t, SSM_INNER), F32),
                   jax.ShapeDtypeStruct((nb, nc) + SSD_STATE, F32)],
        scratch=[pltpu.VMEM(SSD_STATE, F32)], comm=comm)


def _ssd_bwd(xc, dt_raw, proj, prevs, y_pre, dt_bias, a_log, d_skip, norm_g, dy, dproj, nb, nc, comm=None):
    t = xc.shape[0]
    row = lambda b, c: (b * nc + (nc - 1 - c), 0)
    par = lambda shape: pl.BlockSpec(shape, lambda b, c: (0, 0))
    z_spec = pl.BlockSpec((CHUNK, SSM_INNER), lambda b, c: (b * nc + (nc - 1 - c), 1))

    def body(xc_ref, dt_ref, z_ref, prev_ref, ypre_ref, dtb_ref, al_ref, ds_ref, ng_ref, e_ref, dy_ref, _,
             dxc_ref, ddt_ref, dz_ref, ddtb_ref, dal_ref, dds_ref, dng_ref, dst_ref):
        @pl.when(pl.program_id(1) == 0)
        def _():
            dst_ref[...] = jnp.zeros_like(dst_ref)

        @pl.when((pl.program_id(0) == 0) & (pl.program_id(1) == 0))
        def _():
            ddtb_ref[...] = jnp.zeros_like(ddtb_ref)
            dal_ref[...] = jnp.zeros_like(dal_ref)
            dds_ref[...] = jnp.zeros_like(dds_ref)
            dng_ref[...] = jnp.zeros_like(dng_ref)

        dxc, ddt, dz, dprev, ddtb, dal, dds, dng = _ssd_backward(
            xc_ref[...], dt_ref[...], z_ref[...].astype(F32), prev_ref[0, 0], ypre_ref[...], dtb_ref[...], al_ref[...],
            ds_ref[...], ng_ref[...], e_ref[...], dy_ref[...].astype(F32), dst_ref[...])
        dxc_ref[...] = dxc
        ddt_ref[...] = ddt.astype(ddt_ref.dtype)
        dz_ref[...] = dz.astype(dz_ref.dtype)
        dst_ref[...] = dprev
        ddtb_ref[...] += ddtb
        dal_ref[...] += dal
        dds_ref[...] += dds
        dng_ref[...] += dng

    return _call(
        "ssd_bwd", body, grid=(nb, nc),
        ins=[xc, dt_raw, proj, prevs, y_pre, dt_bias, a_log, d_skip, norm_g, _head_expander(), dy, dproj],
        in_specs=[pl.BlockSpec((CHUNK, SSM_CONV_DIM), row), pl.BlockSpec((CHUNK, DT_LANES), row), z_spec,
                  pl.BlockSpec((1, 1) + SSD_STATE, lambda b, c: (b, nc - 1 - c, 0, 0)),
                  pl.BlockSpec((CHUNK, SSM_INNER), row),
                  par((1, 128)), par((1, 128)), par((1, SSM_INNER)), par((1, SSM_INNER)), par((128, SSM_INNER)),
                  pl.BlockSpec((CHUNK, SSM_INNER), row), pl.BlockSpec(memory_space=pl.ANY)],
        out_specs=[pl.BlockSpec((CHUNK, SSM_CONV_DIM), row), pl.BlockSpec((CHUNK, DT_LANES), row), z_spec,
                   par((1, 128)), par((1, 128)), par((1, 128)), par((1, SSM_INNER))],
        out_shape=[jax.ShapeDtypeStruct((t, SSM_CONV_DIM), F32), jax.ShapeDtypeStruct((t, DT_LANES), BF16),
                   jax.ShapeDtypeStruct(dproj.shape, dproj.dtype), jax.ShapeDtypeStruct((1, 128), F32),
                   jax.ShapeDtypeStruct((1, 128), F32), jax.ShapeDtypeStruct((1, 128), F32),
                   jax.ShapeDtypeStruct((1, SSM_INNER), F32)],
        scratch=[pltpu.VMEM(SSD_STATE, F32)], comm=comm, aliases={11: 2})


def _loss_head(y, target):
    t, d = y.shape
    tm = _pick_tile(t, (256,))
    blk = pl.BlockSpec((tm, d), lambda i: (i, 0))

    def body(y_ref, t_ref, loss_ref, dy_ref):
        err = y_ref[...] - t_ref[...]
        dy_ref[...] = err * (1.0 / d)

        @pl.when(pl.program_id(0) == 0)
        def _():
            loss_ref[...] = jnp.zeros_like(loss_ref)

        loss_ref[...] += 0.5 * jnp.sum(jnp.mean(err * err, axis=-1, keepdims=True), axis=0, keepdims=True)

    return _call("loss_head", body, grid=(t // tm,), ins=[y, target], in_specs=[blk, blk],
                 out_specs=[pl.BlockSpec((1, 1), lambda i: (0, 0)), blk],
                 out_shape=[jax.ShapeDtypeStruct((1, 1), F32), jax.ShapeDtypeStruct((t, d), F32)])[0]


def _adamw_math(g, w, m, v):
    m_new = ADAM_B1 * m + (1.0 - ADAM_B1) * g
    v_new = ADAM_B2 * v + (1.0 - ADAM_B2) * jnp.square(g)
    m_hat = m_new / (1.0 - ADAM_B1 ** ADAM_STEP)
    v_hat = v_new / (1.0 - ADAM_B2 ** ADAM_STEP)
    delta = -ADAM_LR * (m_hat / (jnp.sqrt(v_hat) + ADAM_EPS) + ADAM_WD * w)
    return delta, m_new, v_new


def _adamw_sharded(name, parts, w, m, v, comm=None):
    _, a, b = w.shape
    tr = _pick_tile(a, (128,))
    nt = a // tr
    part_specs = [pl.BlockSpec((N_DEV, tr, b),
                               (lambda l, i, _k=k: (0, jnp.where(l == _k, i, jnp.where(l > _k, nt - 1, 0)), 0)))
                  for k in range(DEPTH)]
    blk = pl.BlockSpec((1, tr, b), lambda l, i: (l, i, 0))

    def body(*refs):
        p_refs = refs[:DEPTH]
        w_ref, m_ref, v_ref, g_out, d_out, m_out, v_out = refs[DEPTH:]
        for k in range(DEPTH):
            @pl.when(pl.program_id(0) == k)
            def _(p_ref=p_refs[k]):
                g = p_ref[0].astype(F32)
                for p in range(1, N_DEV):
                    g = g + p_ref[p].astype(F32)
                delta, m_new, v_new = _adamw_math(g, w_ref[0], m_ref[0], v_ref[0])
                g_out[0] = g
                d_out[0] = delta
                m_out[0] = m_new
                v_out[0] = v_new

    return _call(name, body, grid=(DEPTH, nt), ins=list(parts) + [w, m, v], in_specs=part_specs + [blk, blk, blk],
                 out_specs=[blk] * 4, out_shape=[jax.ShapeDtypeStruct(w.shape, F32)] * 4, comm=comm)


def _adamw_small(name, g, w, m, v):
    full = pl.BlockSpec(w.shape, lambda i: (0, 0))

    def body(g_ref, w_ref, m_ref, v_ref, d_out, m_out, v_out):
        delta, m_new, v_new = _adamw_math(g_ref[...], w_ref[...], m_ref[...], v_ref[...])
        d_out[...] = delta
        m_out[...] = m_new
        v_out[...] = v_new

    return _call(name, body, grid=(1,), ins=[g, w, m, v], in_specs=[full] * 4, out_specs=[full] * 3,
                 out_shape=[jax.ShapeDtypeStruct(w.shape, F32)] * 3)[0]


def _sum_parts(name, parts):
    n_parts, rows, cols = parts.shape
    tr = _pick_tile(rows, (512, 256, 128, 64, 32, 16, 8))

    def body(p_ref, o_ref):
        acc = p_ref[0]
        for p in range(1, n_parts):
            acc = acc + p_ref[p]
        o_ref[...] = acc

    return _call(name, body, grid=(rows // tr,), ins=[parts],
                 in_specs=[pl.BlockSpec((n_parts, tr, cols), lambda i: (0, i, 0))],
                 out_specs=[pl.BlockSpec((tr, cols), lambda i: (i, 0))],
                 out_shape=[jax.ShapeDtypeStruct((rows, cols), parts.dtype)])[0][0]


W_IN_SHARD = IN_COLS // N_DEV


def _pack_w_in(gathered):
    r = gathered.shape[1]
    tr = _pick_tile(r, (128,))

    def body(g_ref, main_ref, dt_ref):
        w = jnp.concatenate([g_ref[j].astype(F32) for j in range(N_DEV)], axis=1)
        main_ref[...] = jnp.concatenate([w[:, :XBC_COL], w[:, GA_COL:], w[:, XBC_COL:DT_COL]],
                                        axis=1).astype(main_ref.dtype)
        dt_ref[...] = jnp.concatenate([w[:, DT_COL:GA_COL], jnp.zeros((tr, DT_LANES - SSM_HEADS), F32)],
                                      axis=1).astype(dt_ref.dtype)

    return _call("pack_w_in", body, grid=(r // tr,), ins=[gathered],
                 in_specs=[pl.BlockSpec((N_DEV, tr, W_IN_SHARD), lambda i: (0, i, 0))],
                 out_specs=[pl.BlockSpec((tr, P_COLS), lambda i: (i, 0)), pl.BlockSpec((tr, DT_LANES), lambda i: (i, 0))],
                 out_shape=[jax.ShapeDtypeStruct((r, P_COLS), gathered.dtype),
                            jax.ShapeDtypeStruct((r, DT_LANES), gathered.dtype)])[0]


def _unpack_w_in(main, dt):
    r = main.shape[0]
    tr = _pick_tile(r, (128,))

    def body(main_ref, dt_ref, o_ref):
        main = main_ref[...].astype(F32)
        w = jnp.concatenate([main[:, :P_GATE], main[:, P_XBC:], dt_ref[...].astype(F32)[:, :SSM_HEADS],
                             main[:, P_GATE:P_XBC]], axis=1)
        for j in range(N_DEV):
            o_ref[j] = w[:, j * W_IN_SHARD:(j + 1) * W_IN_SHARD].astype(o_ref.dtype)

    return _call("unpack_w_in", body, grid=(r // tr,), ins=[main, dt],
                 in_specs=[pl.BlockSpec((tr, P_COLS), lambda i: (i, 0)), pl.BlockSpec((tr, DT_LANES), lambda i: (i, 0))],
                 out_specs=[pl.BlockSpec((N_DEV, tr, W_IN_SHARD), lambda i: (0, i, 0))],
                 out_shape=[jax.ShapeDtypeStruct((N_DEV, r, W_IN_SHARD), main.dtype)])[0][0]


def _pad_heads(v):
    return jnp.pad(v, (0, 128 - SSM_HEADS)).reshape(1, 128)


def _run_step(x, mem, target, small, ex):
    nb, s, d = x.shape
    t = nb * s
    nc = s // CHUNK
    rows = _pick_tile(t, (256,))
    rows_wide = _pick_tile(t, (512, 256))
    tq = _pick_tile(s, (512, 256))
    vec = lambda a: a.reshape(1, -1)
    full1 = lambda shape: pl.BlockSpec(shape, lambda i: (0,) * len(shape))
    row1 = lambda tm, w: pl.BlockSpec((tm, w), lambda i: (i, 0))
    sds = jax.ShapeDtypeStruct

    def mm(call, l, a, b, **kw):
        comm = ex.before(call, l)
        out, comm_outs = _matmul(call, a, b, comm=comm, **kw)
        if comm is not None:
            ex.after(call, l, comm_outs)
        return out

    def stage_bwd(call, l, *args):
        comm = ex.before(call, l)
        outs, comm_outs = _stage_bwd(call, *args, comm=comm)
        if comm is not None:
            ex.after(call, l, comm_outs)
        return outs

    def stage_fwd(call, l, *args):
        comm = ex.before(call, l)
        outs, comm_outs = _stage_fwd(call, *args, comm=comm)
        if comm is not None:
            ex.after(call, l, comm_outs)
        return outs

    mem_specs = [row1(256, d), full1((1, d)), full1((1, d))]
    mem_ins = [mem.reshape(nb * MEM_LEN, d), vec(small["mem_ln_g"]), vec(small["mem_ln_b"])]
    (mem_n,) = stage_fwd("memln_fwd", 0, _memln_block, (nb * MEM_LEN // 256,), mem_ins, mem_specs,
                          [sds((nb * MEM_LEN, d), BF16)], [row1(256, d)])

    h = x.reshape(t, d)
    h_bf = h.astype(BF16)
    ln_specs = [row1(rows, d), row1(rows, d), full1((1, d)), full1((1, d))]
    saved = []
    for l in range(DEPTH):
        sv = {"h_bf": h_bf}
        w_p, w_dt = ex.weight("w_in", l)
        proj = mm("mm_in", l, h_bf, w_p, out_dtype=BF16)
        dt_raw = mm("mm_dt", l, h_bf, w_dt)
        sv["proj"] = proj
        sgu_ins = [proj, vec(small["sg_ln_g"][l]), vec(small["sg_ln_b"][l]), small["sg_w"][l], small["sg_b"][l].T]
        sgu_specs = [pl.BlockSpec((CHUNK, 2 * d), lambda i: (i, 0)), full1((1, d)), full1((1, d)),
                     full1((SG_GROUPS, CHUNK, CHUNK)), full1((CHUNK, SG_GROUPS))]
        (a_out,) = stage_fwd("sgu_fwd", l, _sgu_block, (t // CHUNK,), sgu_ins, sgu_specs, [sds((t, d), BF16)],
                              [row1(CHUNK, d)])
        sv["sgu"] = (sgu_ins, sgu_specs)
        sv["a_out"] = a_out
        cw = 256
        conv_ins = [proj, small["conv_w"][l], vec(small["conv_b"][l])]
        conv_specs = [pl.BlockSpec((s, cw), lambda j, b: (b, P_XBC // cw + j)),
                      pl.BlockSpec((SSM_CONV, cw), lambda j, b: (0, j)), pl.BlockSpec((1, cw), lambda j, b: (0, j))]
        conv_out_spec = pl.BlockSpec((s, cw), lambda j, b: (b, j))
        (xc,) = stage_fwd("conv_fwd", l, _conv_block, (SSM_CONV_DIM // cw, nb), conv_ins, conv_specs,
                           [sds((t, SSM_CONV_DIM), F32)], [conv_out_spec])
        sv["conv"] = (conv_ins, conv_specs, conv_out_spec)
        ssd_par = [_pad_heads(small["dt_bias"][l]), _pad_heads(small["a_log"][l]),
                   vec(jnp.repeat(small["d_skip"][l], SSM_HEADDIM)), vec(small["ssm_norm_g"][l])]
        comm = ex.before("ssd_fwd", l)
        (y_ssd, y_pre, prevs), comm_outs = _ssd_fwd(xc, dt_raw, proj, *ssd_par, nb, nc, comm=comm)
        if comm is not None:
            ex.after("ssd_fwd", l, comm_outs)
        sv["ssd"] = (xc, dt_raw, prevs, y_pre, ssd_par)
        sv["y_ssd"] = y_ssd
        br_a = mm("mm_sq", l, a_out, ex.weight("p_a", l), out_dtype=BF16)
        br_b = mm("mm_pb", l, y_ssd, ex.weight("p_b", l), out_dtype=BF16)
        merge_ins = [proj, br_a, br_b]
        merge_out_spec = row1(rows_wide, d)
        merge_specs = [pl.BlockSpec((rows_wide, 2 * d), lambda i: (i, P_GATE // (2 * d))), merge_out_spec, merge_out_spec]
        (merged,) = stage_fwd("merge_fwd", l, _merge_block, (t // rows_wide,), merge_ins, merge_specs,
                               [sds((t, d), BF16)], [merge_out_spec])
        sv["merge"] = (merge_ins, merge_specs, merge_out_spec)
        sv["merged"] = merged
        ln_par = [(vec(small["ln_g"][l, k]), vec(small["ln_b"][l, k])) for k in range(3)]
        y1, h1, h1_bf = _matmul_lnres("mm_sq_ln", merged, ex.weight("w_mix_o", l), h, *ln_par[0])
        sv["ln1"] = [h, y1, *ln_par[0]]
        q = mm("mm_sq", l, h1_bf, ex.weight("w_xq", l), out_dtype=BF16)
        kv = mm("mm_kv", l, mem_n, ex.weight("w_xkv", l), out_dtype=BF16)
        attn_ins = [q, kv]
        attn_out_spec = pl.BlockSpec((tq, d), lambda b, i: (b * (s // tq) + i, 0))
        attn_specs = [attn_out_spec, pl.BlockSpec((MEM_LEN, 2 * d), lambda b, i: (b, 0))]
        (o,) = stage_fwd("attn_fwd", l, _attn_block, (nb, s // tq), attn_ins, attn_specs, [sds((t, d), BF16)],
                          [attn_out_spec])
        sv["attn"] = (attn_ins, attn_specs, attn_out_spec)
        sv["o"] = o
        sv["h1_bf"] = h1_bf
        y2, h2, h2_bf = _matmul_lnres("mm_sq_ln", o, ex.weight("w_xo", l), h1, *ln_par[1])
        sv["ln2"] = [h1, y2, *ln_par[1]]
        sv["h2_bf"] = h2_bf
        gu = mm("mm_ffn_in", l, h2_bf, ex.weight("w_ffn_in", l), out_dtype=BF16)
        (act,) = stage_fwd("swiglu_fwd", l, _swiglu_block, (t // 128,), [gu], [row1(128, 2 * FFN_HIDDEN)],
                            [sds((t, FFN_HIDDEN), BF16)], [row1(128, FFN_HIDDEN)])
        sv["gu"] = gu
        sv["act"] = act
        y3, h3, h3_bf = _matmul_lnres("mm_ffn_out_ln", act, ex.weight("w_ffn_out", l), h2, *ln_par[2])
        sv["ln3"] = [h2, y3, *ln_par[2]]
        h, h_bf = h3, h3_bf
        saved.append(sv)

    loss, dh = _loss_head(h, target.reshape(t, d))

    g_small = {n: [None] * DEPTH for n in SMALL_REP + SMALL_SH if n not in ("mem_ln_g", "mem_ln_b")}
    dmem_n = []
    ln_grads = [(0, (), F32), (1, (), BF16), (2, (0,), F32), (3, (0,), F32)]
    for l in reversed(range(DEPTH)):
        sv = saved[l]
        dln_g, dln_b = [None] * 3, [None] * 3
        dres, dy3, dln_g[2], dln_b[2] = stage_bwd("lnres_bwd", l, _lnres_block, (t // rows,), sv["ln3"], ln_specs,
                                                  [(dh,)], [row1(rows, d)], ln_grads)
        ex.grad("w_ffn_out", l, mm("mm_ffn_out_dw", l, sv["act"], dy3, ta=True, out_dtype=BF16))
        dact = mm("mm_ffn_out_dx", l, dy3, ex.weight("w_ffn_out", l), tb=True, out_dtype=BF16)
        (dgu,) = stage_bwd("swiglu_bwd", l, _swiglu_block, (t // 128,), [sv["gu"]], [row1(128, 2 * FFN_HIDDEN)],
                           [(dact,)], [row1(128, FFN_HIDDEN)], [(0, (), BF16)])
        ex.grad("w_ffn_in", l, mm("mm_ffn_in_dw", l, sv["h2_bf"], dgu, ta=True, out_dtype=BF16))
        dh2 = mm("mm_ffn_in_dx", l, dgu, ex.weight("w_ffn_in", l), tb=True, add=dres)
        dres, dy2, dln_g[1], dln_b[1] = stage_bwd("lnres_bwd", l, _lnres_block, (t // rows,), sv["ln2"], ln_specs,
                                                  [(dh2,)], [row1(rows, d)], ln_grads)
        ex.grad("w_xo", l, mm("mm_sq_dw", l, sv["o"], dy2, ta=True, out_dtype=BF16))
        do = mm("mm_sq_dx", l, dy2, ex.weight("w_xo", l), tb=True, out_dtype=BF16)
        attn_ins, attn_specs, attn_out_spec = sv["attn"]
        dq, dkv = stage_bwd("attn_bwd", l, _attn_block, (nb, s // tq), attn_ins, attn_specs, [(do,)], [attn_out_spec],
                            [(0, (), BF16), (1, (1,), F32)])
        ex.grad("w_xq", l, mm("mm_sq_dw", l, sv["h1_bf"], dq, ta=True, out_dtype=BF16))
        dh1 = mm("mm_sq_dx", l, dq, ex.weight("w_xq", l), tb=True, add=dres)
        ex.grad("w_xkv", l, mm("mm_kv_dw", l, mem_n, dkv, ta=True, out_dtype=BF16))
        dmem_n.append(mm("mm_kv_dx", l, dkv, ex.weight("w_xkv", l), tb=True))
        dres, dy1, dln_g[0], dln_b[0] = stage_bwd("lnres_bwd", l, _lnres_block, (t // rows,), sv["ln1"], ln_specs,
                                                  [(dh1,)], [row1(rows, d)], ln_grads)
        g_small["ln_g"][l] = jnp.concatenate(dln_g, axis=0)
        g_small["ln_b"][l] = jnp.concatenate(dln_b, axis=0)
        ex.grad("w_mix_o", l, mm("mm_sq_dw", l, sv["merged"], dy1, ta=True, out_dtype=BF16))
        dmerged = mm("mm_sq_dx", l, dy1, ex.weight("w_mix_o", l), tb=True, out_dtype=BF16)
        merge_ins, merge_specs, merge_out_spec = sv["merge"]
        dproj, dbr_a, dbr_b = stage_bwd("merge_bwd", l, _merge_block, (t // rows_wide,), merge_ins, merge_specs,
                                        [(dmerged,)], [merge_out_spec],
                                        [(0, (), BF16, ((t, P_COLS), merge_specs[0])), (1, (), BF16), (2, (), BF16)])
        ex.grad("p_a", l, mm("mm_sq_dw", l, sv["a_out"], dbr_a, ta=True, out_dtype=BF16))
        da_out = mm("mm_sq_dx", l, dbr_a, ex.weight("p_a", l), tb=True, out_dtype=BF16)
        ex.grad("p_b", l, mm("mm_pb_dw", l, sv["y_ssd"], dbr_b, ta=True, out_dtype=BF16))
        dy_ssd = mm("mm_pb_dx", l, dbr_b, ex.weight("p_b", l), tb=True, out_dtype=BF16)
        sgu_ins, sgu_specs = sv["sgu"]
        dproj, dsg_ln_g, dsg_ln_b, dsg_w, dsg_b = stage_bwd(
            "sgu_bwd", l, _sgu_block, (t // CHUNK,), sgu_ins, sgu_specs, [(da_out,)], [row1(CHUNK, d)],
            [(0, (), BF16, ((t, P_COLS), sgu_specs[0]), dproj), (1, (0,), F32), (2, (0,), F32), (3, (0,), F32),
             (4, (0,), F32)])
        g_small["sg_ln_g"][l], g_small["sg_ln_b"][l], g_small["sg_w"][l], g_small["sg_b"][l] = (
            dsg_ln_g[0], dsg_ln_b[0], dsg_w, dsg_b.T)
        xc, dt_raw, prevs, y_pre, ssd_par = sv["ssd"]
        comm = ex.before("ssd_bwd", l)
        (dxc, ddt, dproj, ddtb, dal, dds, dng), comm_outs = _ssd_bwd(xc, dt_raw, sv["proj"], prevs, y_pre, *ssd_par,
                                                                     dy_ssd, dproj, nb, nc, comm=comm)
        if comm is not None:
            ex.after("ssd_bwd", l, comm_outs)
        g_small["dt_bias"][l], g_small["a_log"][l], g_small["d_skip"][l] = (
            ddtb[0, :SSM_HEADS], dal[0, :SSM_HEADS], dds[0, :SSM_HEADS])
        g_small["ssm_norm_g"][l] = dng[0]
        conv_ins, conv_specs, conv_out_spec = sv["conv"]
        dproj, dconv_w, dconv_b = stage_bwd("conv_bwd", l, _conv_block, (SSM_CONV_DIM // 256, nb), conv_ins, conv_specs,
                                            [(dxc,)], [conv_out_spec],
                                            [(0, (), BF16, ((t, P_COLS), conv_specs[0]), dproj), (1, (1,), F32),
                                             (2, (1,), F32)])
        g_small["conv_w"][l], g_small["conv_b"][l] = dconv_w, dconv_b[0]
        if l == 0:
            dmg, dmb = stage_bwd("memln_bwd", l, _memln_block, (nb * MEM_LEN // 256,), mem_ins, mem_specs,
                                 [tuple(dmem_n)], [row1(256, d)], [(1, (0,), F32), (2, (0,), F32)])
            done = {n: jnp.stack(g, axis=0) for n, g in g_small.items()}
            done["mem_ln_g"], done["mem_ln_b"] = dmg[0], dmb[0]
            ex.small_grads(done)
        w_p, w_dt = ex.weight("w_in", l)
        g_dt = mm("mm_dt_dw", l, sv["h_bf"], ddt, ta=True, out_dtype=BF16)
        ex.grad("w_in", l, _unpack_w_in(mm("mm_in_dw", l, sv["h_bf"], dproj, ta=True, out_dtype=BF16), g_dt))
        dh = mm("mm_in_dx", l, dproj, w_p, tb=True, add=mm("mm_dt_dx", l, ddt, w_dt, tb=True, add=dres))

    return loss, dh.reshape(nb, s, d)


def _pack_flat(arrays, rows):
    flat = jnp.concatenate([a.reshape(-1) for a in arrays])
    return jnp.pad(flat, (0, rows * 128 - flat.shape[0])).reshape(rows, 128)


def _unpack_flat(packed, shapes):
    lead = packed.shape[:-2]
    flat = packed.reshape(lead + (-1,))
    out, pos = [], 0
    for shape in shapes:
        n = math.prod(shape)
        out.append(flat[..., pos:pos + n].reshape(lead + tuple(shape)))
        pos += n
    return out


def _small_rows(n_elems):
    return -(-n_elems // (128 * SMALL_ROW_TILE)) * SMALL_ROW_TILE


def _from_shards(name, gathered):
    _, a, b = gathered.shape
    if name == "w_in":
        return tuple(_pack_w_in(gathered))
    if name in BIG_COL_SHARDED:
        return _join_columns(gathered)
    return gathered.reshape(N_DEV * a, b)


def _to_shards(name, g):
    if name == "w_in":
        return g
    if name in BIG_COL_SHARDED:
        return _split_columns(g)
    a, b = g.shape
    return g.reshape(N_DEV, a // N_DEV, b)


def _join_columns(gathered):
    _, r, b = gathered.shape
    tr = _pick_tile(r, (128,))

    def body(g_ref, o_ref):
        o_ref[...] = jnp.concatenate([g_ref[j].astype(F32) for j in range(N_DEV)], axis=1).astype(o_ref.dtype)

    return _call("join_columns", body, grid=(r // tr,), ins=[gathered],
                 in_specs=[pl.BlockSpec((N_DEV, tr, b), lambda i: (0, i, 0))],
                 out_specs=[pl.BlockSpec((tr, N_DEV * b), lambda i: (i, 0))],
                 out_shape=[jax.ShapeDtypeStruct((r, N_DEV * b), gathered.dtype)])[0][0]


def _split_columns(full):
    r, nb = full.shape
    b = nb // N_DEV
    tr = _pick_tile(r, (128,))

    def body(f_ref, o_ref):
        w = f_ref[...].astype(F32)
        for j in range(N_DEV):
            o_ref[j] = w[:, j * b:(j + 1) * b].astype(o_ref.dtype)

    return _call("split_columns", body, grid=(r // tr,), ins=[full],
                 in_specs=[pl.BlockSpec((tr, nb), lambda i: (i, 0))],
                 out_specs=[pl.BlockSpec((N_DEV, tr, b), lambda i: (0, i, 0))],
                 out_shape=[jax.ShapeDtypeStruct((N_DEV, r, b), full.dtype)])[0][0]


class _MeshExchange:
    def __init__(self, shards_bf16, first):
        self.shards = shards_bf16
        self.full = dict(first)
        self.pieces = {}
        self.grads = {}
        self.to_send = {}
        self.received = {}
        self.small = None
        self.small_gathered = None

    def weight(self, name, l):
        if (name, l) not in self.full:
            got = jnp.concatenate([self.pieces[(name, l, q)] for q in range(W_IN_PIECES)], axis=1)
            self.full[(name, l)] = _from_shards(name, got)
        return self.full[(name, l)]

    def grad(self, name, l, g):
        self.grads[(name, l)] = g

    def small_grads(self, done):
        self.small = done

    def partial_sums(self, name, l):
        if (name, l, None) in self.received:
            return self.received[(name, l, None)]
        return jnp.concatenate([self.received[(name, l, q)] for q in range(W_IN_PIECES)], axis=1)

    def _slices(self, name, l, piece):
        n_rows = D_MODEL // W_IN_PIECES
        key, rows = (name, l), None if piece is None else (piece * n_rows, n_rows)
        if key not in self.to_send:
            self.to_send[key] = _to_shards(name, self.grads[key])
        return self.to_send[key], rows

    def before(self, call, l):
        comm = _Comm()
        for name, layer, piece in GATHER_PLAN.get((call, l), ()):
            n_rows = D_MODEL // W_IN_PIECES
            comm.gathers.append((self.shards[name], layer, None if piece is None else (piece * n_rows, n_rows)))
        if (call, l) == SMALL_GATHER_CALL:
            names = SMALL_REP + SMALL_SH
            rows = _small_rows(sum(math.prod(self.small[n].shape) for n in names))
            comm.gathers.append((_pack_flat([self.small[n] for n in names], rows), None, None))
        for name, layer, piece in SCATTER_PLAN.get((call, l), ()):
            comm.scatters.append(self._slices(name, layer, piece))
        return comm if comm.gathers or comm.scatters else None

    def after(self, call, l, outs):
        gathers = list(GATHER_PLAN.get((call, l), ()))
        for (name, layer, piece), out in zip(gathers, outs):
            if piece is None:
                self.full[(name, layer)] = _from_shards(name, out)
            else:
                self.pieces[(name, layer, piece)] = out
        outs = outs[len(gathers):]
        if (call, l) == SMALL_GATHER_CALL:
            self.small_gathered = outs[0]
            outs = outs[1:]
        for item, out in zip(SCATTER_PLAN.get((call, l), ()), outs):
            self.received[item] = out


def kernel(x, mem, mem_ln_g, mem_ln_b, w_in, sg_ln_g, sg_ln_b, sg_w, sg_b, conv_w, conv_b, dt_bias, a_log, d_skip, ssm_norm_g, p_a, p_b, w_mix_o, w_xq, w_xkv, w_xo, w_ffn_in, w_ffn_out, ln_g, ln_b, loss_target, m_mem_ln_g, m_mem_ln_b, m_w_in, m_sg_ln_g, m_sg_ln_b, m_sg_w, m_sg_b, m_conv_w, m_conv_b, m_dt_bias, m_a_log, m_d_skip, m_ssm_norm_g, m_p_a, m_p_b, m_w_mix_o, m_w_xq, m_w_xkv, m_w_xo, m_w_ffn_in, m_w_ffn_out, m_ln_g, m_ln_b, v_mem_ln_g, v_mem_ln_b, v_w_in, v_sg_ln_g, v_sg_ln_b, v_sg_w, v_sg_b, v_conv_w, v_conv_b, v_dt_bias, v_a_log, v_d_skip, v_ssm_norm_g, v_p_a, v_p_b, v_w_mix_o, v_w_xq, v_w_xkv, v_w_xo, v_w_ffn_in, v_w_ffn_out, v_ln_g, v_ln_b):
    args = dict(locals())
    w = {n: args[n] for n in WEIGHTS}
    m = {n: args["m_" + n] for n in WEIGHTS}
    v = {n: args["v_" + n] for n in WEIGHTS}
    me = 4 * lax.axis_index("x") + 2 * lax.axis_index("y") + lax.axis_index("c")

    shards = {n: w[n].astype(BF16) for n in BIG}
    sh_shapes = [w[n].shape for n in SMALL_SH]
    first = _Comm()
    first.gathers.append((shards["w_in"], 0, None))
    first.gathers.append((_pack_flat([w[n] for n in SMALL_SH], _small_rows(sum(math.prod(s) for s in sh_shapes))), None,
                          None))
    w_in0, small_sh = _comm_only("gather_first", first)
    small = {n: w[n] for n in SMALL_REP}
    for n, sh in zip(SMALL_SH, _unpack_flat(small_sh, sh_shapes)):
        small[n] = sh.transpose(1, 2, 0, 3).reshape(sh.shape[1], sh.shape[2], N_DEV * sh.shape[3])

    ex = _MeshExchange(shards, {("w_in", 0): _from_shards("w_in", w_in0)})
    loss, grad_x = _run_step(x, mem, loss_target, small, ex)
    loss = lax.psum(loss[0, 0], ("x", "y", "c"))

    out = {}
    for n in BIG[1:] + BIG[:1]:
        comm = ex.before("adamw_" + n, 0)
        out[n], comm_outs = _adamw_sharded("adamw_" + n, [ex.partial_sums(n, l) for l in range(DEPTH)], w[n], m[n], v[n],
                                           comm=comm)
        if comm is not None:
            ex.after("adamw_" + n, 0, comm_outs)
    names = SMALL_REP + SMALL_SH
    g_small = dict(zip(names, _unpack_flat(_sum_parts("sum_small_grads", ex.small_gathered),
                                           [ex.small[n].shape for n in names])))
    for n in names:
        g = g_small[n]
        if n in SMALL_SH:
            width = w[n].shape[-1]
            g = lax.dynamic_slice_in_dim(g, me * width, width, axis=-1)
        two_d = (-1, w[n].shape[-1])
        res = _adamw_small("adamw_" + n, g.reshape(two_d), w[n].reshape(two_d), m[n].reshape(two_d), v[n].reshape(two_d))
        out[n] = [g] + [r.reshape(w[n].shape) for r in res]

    results = []
    for k in range(4):
        results.extend(out[n][k] for n in WEIGHTS)
    return (loss, grad_x, *results)
```

```python
import functools
import math

import jax
import jax.numpy as jnp
from jax import lax
from jax.experimental import pallas as pl
from jax.experimental.pallas import tpu as pltpu

F32 = jnp.float32
BF16 = jnp.bfloat16
HIGHEST = lax.Precision.HIGHEST

N_DEV = 8
D_MODEL = 1024
DEPTH = 2
MEM_LEN = 256
CHUNK = 128
SG_GROUPS = 8
SSM_INNER = 2048
SSM_HEADDIM = 64
SSM_HEADS = 32
SSM_STATE = 128
SSM_GROUPS = 4
SSM_RPG = 8
SSM_CONV = 4
SSM_CONV_DIM = 3072
X_HEADS = 4
X_HEADDIM = 256
FFN_HIDDEN = 2816
ALPHA = float((2 * DEPTH) ** 0.25)
LN_EPS = 1e-5
RMS_EPS = 1e-5
XBC_COL = 4096
DT_COL = 7168
GA_COL = 7200
IN_COLS = 9248
P_GATE = 4096
P_XBC = 6144
P_COLS = 9216
DT_LANES = 128

ADAM_LR = 0.001
ADAM_B1 = 0.9
ADAM_B2 = 0.999
ADAM_EPS = 1e-08
ADAM_WD = 0.01
ADAM_STEP = 10

VMEM_LIMIT = 48 * 1024 * 1024
SMALL_ROW_TILE = 256

BIG = ("w_in", "p_a", "p_b", "w_mix_o", "w_xq", "w_xkv", "w_xo", "w_ffn_in", "w_ffn_out")
BIG_COL_SHARDED = ("w_in", "w_xkv", "w_ffn_in")
SMALL_REP = ("mem_ln_g", "mem_ln_b", "sg_ln_g", "sg_ln_b", "sg_w", "sg_b", "conv_b", "dt_bias", "a_log", "d_skip",
             "ssm_norm_g")
SMALL_SH = ("conv_w", "ln_g", "ln_b")
WEIGHTS = ("mem_ln_g", "mem_ln_b", "w_in", "sg_ln_g", "sg_ln_b", "sg_w", "sg_b", "conv_w", "conv_b", "dt_bias", "a_log",
           "d_skip", "ssm_norm_g", "p_a", "p_b", "w_mix_o", "w_xq", "w_xkv", "w_xo", "w_ffn_in", "w_ffn_out", "ln_g", "ln_b")

W_IN_PIECES = 4
GATHER_PLAN = {("sgu_fwd", 0): [("w_in", 1, 0)], ("conv_fwd", 0): [("w_in", 1, 1)],
               ("mm_ffn_in", 0): [("w_in", 1, 2)], ("mm_ffn_out_ln", 0): [("w_in", 1, 3)]}
SCATTER_PLAN = {("mm_ffn_in_dw", 0): [("w_in", 1, 2)], ("mm_ffn_in_dx", 0): [("w_in", 1, 3)],
                ("adamw_w_ffn_in", 0): [("w_in", 0, 3)]}
for _l in range(DEPTH):
    GATHER_PLAN[("mm_in", _l)] = [(n, _l, None) for n in ("p_a", "p_b", "w_mix_o", "w_xq", "w_xkv", "w_xo")]
    GATHER_PLAN[("ssd_fwd", _l)] = [("w_ffn_in", _l, None), ("w_ffn_out", _l, None)]
    SCATTER_PLAN[("swiglu_bwd", _l)] = [("w_ffn_out", _l, None)]
    SCATTER_PLAN[("sgu_bwd", _l)] = [("w_mix_o", _l, None), ("p_a", _l, None), ("w_xo", _l, None)]
    SCATTER_PLAN[("ssd_bwd", _l)] = [("w_ffn_in", _l, None), ("w_xkv", _l, None), ("w_xq", _l, None)]
    SCATTER_PLAN[("conv_bwd", _l)] = [("p_b", _l, None)]
    SCATTER_PLAN[("mm_in_dx", _l)] = [("w_in", _l, 0), ("w_in", _l, 1)] + ([("w_in", _l, 2)] if _l == 0 else [])
SMALL_GATHER_CALL = ("mm_in_dw", 0)


def _layer_norm(x, g, b):
    mu = jnp.mean(x, axis=-1, keepdims=True)
    xc = x - mu
    var = jnp.mean(xc * xc, axis=-1, keepdims=True)
    return xc * lax.rsqrt(var + LN_EPS) * g + b


def _gelu(x):
    return 0.5 * x * (1.0 + lax.erf(x * (1.0 / math.sqrt(2.0))))


def _sigmoid(x):
    return 0.5 * jnp.tanh(0.5 * x) + 0.5


def _silu(x):
    return x * _sigmoid(x)


def _softplus(x):
    return jnp.maximum(x, 0.0) + jnp.log1p(jnp.exp(-jnp.abs(x)))


def _causal_mask():
    r = lax.broadcasted_iota(jnp.int32, (CHUNK, CHUNK), 0)
    c = lax.broadcasted_iota(jnp.int32, (CHUNK, CHUNK), 1)
    return r >= c


def _sgu_block(uv, ln_g, ln_b, w, sb):
    gu = _gelu(uv[:, :D_MODEL])
    vn = _layer_norm(_gelu(uv[:, D_MODEL:]), ln_g, ln_b)
    causal = _causal_mask()
    width = D_MODEL // SG_GROUPS
    outs = []
    for g in range(SG_GROUPS):
        wg = jnp.where(causal, w[g], 0.0).astype(BF16)
        mixed = jnp.dot(wg, vn[:, g * width:(g + 1) * width].astype(BF16), preferred_element_type=F32)
        outs.append(mixed + sb[:, g:g + 1])
    return (gu * jnp.concatenate(outs, axis=1),)


GROUP_W = SSM_RPG * SSM_HEADDIM
NT_DIMS = (((1,), (1,)), ((), ()))
TN_DIMS = (((0,), (0,)), ((), ()))


def _mxu(a, b, dims=(((1,), (0,)), ((), ()))):
    return lax.dot_general(a.astype(BF16), b.astype(BF16), dims, preferred_element_type=F32)


def _head_expander():
    return (jnp.arange(SSM_INNER)[None, :] // SSM_HEADDIM == jnp.arange(128)[:, None]).astype(BF16)


def _bf16_terms(x, n):
    terms = []
    for _ in range(n):
        t = x.astype(BF16)
        terms.append(t)
        x = x - t.astype(F32)
    return terms


def _expand_heads(q, e):
    return sum(jnp.dot(t, e, preferred_element_type=F32) for t in _bf16_terms(q, 2))


def _reduce_heads(v, e):
    return sum(lax.dot_general(t, e, NT_DIMS, preferred_element_type=F32) for t in _bf16_terms(v, 2))


def _reduce_heads_of_column_sums(v, e):
    sums = jnp.broadcast_to(jnp.sum(v, axis=0, keepdims=True), (8, v.shape[1]))
    return _reduce_heads(sums, e)[0:1, :]


def _ssd_common(xc, dtraw, dt_bias, a_log, e):
    xs = xc[:, :SSM_INNER]
    pre = dtraw + dt_bias
    dt = _softplus(pre)
    a = -jnp.exp(a_log)
    r_i = lax.broadcasted_iota(jnp.int32, (CHUNK, CHUNK), 0)
    c_i = lax.broadcasted_iota(jnp.int32, (CHUNK, CHUNK), 1)
    tril = jnp.where(r_i >= c_i, 1.0, 0.0).astype(F32)
    cs = jnp.dot(tril, dt * a, precision=HIGHEST, preferred_element_type=F32)
    cs_last = cs[CHUNK - 1:CHUNK, :]
    decay_in = jnp.exp(cs)
    decay_st = jnp.exp(cs_last - cs)
    dt_x = _expand_heads(dt, e)
    w_st_x = _expand_heads(dt * decay_st, e)
    decay_in_x = _expand_heads(decay_in, e)
    return dict(xs=xs, pre=pre, dt=dt, a=a, lower=r_i >= c_i, upper=c_i >= r_i, cs=cs, cs_t=cs.T, decay_in=decay_in,
                decay_st=decay_st, chunk_decay=jnp.exp(cs_last), dt_x=dt_x, w_st_x=w_st_x, decay_in_x=decay_in_x,
                chunk_decay_x=decay_in_x[CHUNK - 1:CHUNK, :], xdt=xs * dt_x, x_st=(xs * w_st_x).astype(BF16),
                low=lax.broadcasted_iota(jnp.int32, (CHUNK, 128), 1) < SSM_HEADDIM)


def _pair_decay(c, h):
    return jnp.exp(jnp.where(c["lower"], c["cs"][:, h:h + 1] - c["cs_t"][h:h + 1, :], -1e30))


def _pair_decay_t(c, h):
    return jnp.exp(jnp.where(c["upper"], c["cs_t"][h:h + 1, :] - c["cs"][:, h:h + 1], -1e30))


def _ssd_forward(xc, dtraw, z, prev, dt_bias, a_log, d_skip_x, norm_g, e):
    c = _ssd_common(xc, dtraw, dt_bias, a_log, e)
    y_groups, new_states = [], []
    for g in range(SSM_GROUPS):
        lanes = slice(g * GROUP_W, (g + 1) * GROUP_W)
        bg = xc[:, SSM_INNER + g * SSM_STATE:SSM_INNER + (g + 1) * SSM_STATE]
        cg = xc[:, SSM_INNER + (SSM_GROUPS + g) * SSM_STATE:SSM_INNER + (SSM_GROUPS + g + 1) * SSM_STATE].astype(BF16)
        pg = prev[g * SSM_STATE:(g + 1) * SSM_STATE, :]
        cb = _mxu(cg, bg, NT_DIMS)
        y_in = _mxu(cg, pg) * c["decay_in_x"][:, lanes]
        new_states.append(pg * c["chunk_decay_x"][:, lanes] + _mxu(bg.T, c["x_st"][:, lanes]))
        pairs = []
        for j in range(SSM_RPG // 2):
            h0 = g * SSM_RPG + 2 * j
            xp = c["xdt"][:, 128 * (h0 // 2):128 * (h0 // 2 + 1)]
            pairs.append(_mxu(cb * _pair_decay(c, h0), jnp.where(c["low"], xp, 0.0))
                         + _mxu(cb * _pair_decay(c, h0 + 1), jnp.where(c["low"], 0.0, xp)))
        y_groups.append(jnp.concatenate(pairs, axis=1) + y_in)
    y_pre = jnp.concatenate(y_groups, axis=1) + c["xs"] * d_skip_x
    gated = y_pre * _silu(z)
    normed = [gated[:, g * GROUP_W:(g + 1) * GROUP_W] for g in range(SSM_GROUPS)]
    normed = [yg * lax.rsqrt(jnp.mean(yg * yg, axis=-1, keepdims=True) + RMS_EPS) for yg in normed]
    return jnp.concatenate(normed, axis=1) * norm_g, y_pre, jnp.concatenate(new_states, axis=0)


def _ssd_backward(xc, dtraw, z, prev, y_pre, dt_bias, a_log, d_skip_x, norm_g, e, dout, dnew):
    c = _ssd_common(xc, dtraw, dt_bias, a_log, e)
    xs = c["xs"]
    sig = _sigmoid(z)
    silu_z = z * sig
    gated = y_pre * silu_z
    d_gated, normed = [], []
    for g in range(SSM_GROUPS):
        lanes = slice(g * GROUP_W, (g + 1) * GROUP_W)
        yg = gated[:, lanes]
        r = lax.rsqrt(jnp.mean(yg * yg, axis=-1, keepdims=True) + RMS_EPS)
        n = yg * r
        gh = dout[:, lanes] * norm_g[:, lanes]
        d_gated.append(r * (gh - n * jnp.mean(gh * n, axis=-1, keepdims=True)))
        normed.append(n)
    d_gated = jnp.concatenate(d_gated, axis=1)
    dnorm_g = jnp.sum(dout * jnp.concatenate(normed, axis=1), axis=0, keepdims=True)
    dy = d_gated * silu_z
    dz = d_gated * y_pre * (sig * (1.0 + z * (1.0 - sig)))
    dxs = dy * d_skip_x
    dd_skip = _reduce_heads_of_column_sums(dy * xs, e)

    lane = lax.broadcasted_iota(jnp.int32, (CHUNK, 128), 1)
    sub = lax.broadcasted_iota(jnp.int32, (8, 128), 0)
    dcs_neg = jnp.zeros((CHUNK, 128), F32)
    row_slabs = []
    dxdt, dx_st, d_decay_in_x, dprev, d_chunk_decay_x, db_all, dc_all = [], [], [], [], [], [], []
    for g in range(SSM_GROUPS):
        lanes = slice(g * GROUP_W, (g + 1) * GROUP_W)
        bg = xc[:, SSM_INNER + g * SSM_STATE:SSM_INNER + (g + 1) * SSM_STATE].astype(BF16)
        cg_f = xc[:, SSM_INNER + (SSM_GROUPS + g) * SSM_STATE:SSM_INNER + (SSM_GROUPS + g + 1) * SSM_STATE]
        cg = cg_f.astype(BF16)
        pg = prev[g * SSM_STATE:(g + 1) * SSM_STATE, :]
        dng = dnew[g * SSM_STATE:(g + 1) * SSM_STATE, :]
        dy_g = dy[:, lanes]
        cb_t = _mxu(bg, cg, NT_DIMS)
        t1 = (dy_g * c["decay_in_x"][:, lanes]).astype(BF16)
        d_decay_in_x.append(dy_g * _mxu(cg, pg))
        dc = _mxu(t1, pg, NT_DIMS)
        dprev.append(_mxu(cg_f.T, t1) + dng * c["chunk_decay_x"][:, lanes])
        d_chunk_decay_x.append(dng * pg)
        db = _mxu(c["x_st"][:, lanes], dng, NT_DIMS)
        dx_st.append(_mxu(bg, dng))
        dcb_t = jnp.zeros((CHUNK, CHUNK), F32)
        rows = []
        for j in range(SSM_RPG // 2):
            h0 = g * SSM_RPG + 2 * j
            blk = slice(128 * (h0 // 2), 128 * (h0 // 2 + 1))
            xp = c["xdt"][:, blk]
            dyp = dy[:, blk].astype(BF16)
            pair_dx = []
            for k, xk in enumerate((jnp.where(c["low"], xp, 0.0), jnp.where(c["low"], 0.0, xp))):
                dec_t = _pair_decay_t(c, h0 + k)
                pair_dx.append(_mxu(cb_t * dec_t, dyp))
                dml_t = _mxu(xk, dyp, NT_DIMS) * dec_t
                dcb_t = dcb_t + dml_t
                dseg_t = dml_t * cb_t
                dcs_neg = dcs_neg + jnp.where(lane == h0 + k, jnp.sum(dseg_t, axis=-1, keepdims=True), 0.0)
                rows.append(jnp.sum(dseg_t, axis=0, keepdims=True))
            dxdt.append(jnp.where(c["low"], pair_dx[0], pair_dx[1]))
        slab = jnp.zeros((8, 128), F32)
        for r in range(SSM_RPG):
            slab = slab + jnp.where(sub == r, rows[r], 0.0)
        row_slabs.append(slab)
        dc_all.append(dc + _mxu(dcb_t.T, bg))
        db_all.append(db + _mxu(dcb_t, cg))
    dxdt = jnp.concatenate(dxdt, axis=1)
    dx_st = jnp.concatenate(dx_st, axis=1)
    by_head = jnp.concatenate(row_slabs + [jnp.zeros((CHUNK - SSM_HEADS, 128), F32)], axis=0)
    dcs = by_head.T - dcs_neg
    dxs = dxs + dxdt * c["dt_x"] + dx_st * c["w_st_x"]
    ddt = _reduce_heads(dxdt * xs, e)
    dw_st = _reduce_heads(dx_st * xs, e)
    dcs = dcs + _reduce_heads(jnp.concatenate(d_decay_in_x, axis=1), e) * c["decay_in"]
    ddt = ddt + dw_st * c["decay_st"]
    d_log_st = dw_st * c["dt"] * c["decay_st"]
    dcs = dcs - d_log_st
    d_chunk_decay = _reduce_heads_of_column_sums(jnp.concatenate(d_chunk_decay_x, axis=1), e)
    dcs_last = jnp.sum(d_log_st, axis=0, keepdims=True) + d_chunk_decay * c["chunk_decay"]
    row = lax.broadcasted_iota(jnp.int32, (CHUNK, 128), 0)
    dcs = dcs + jnp.where(row == CHUNK - 1, dcs_last, 0.0)
    triu = jnp.where(c["upper"], 1.0, 0.0).astype(F32)
    dda = jnp.dot(triu, dcs, precision=HIGHEST, preferred_element_type=F32)
    ddt = ddt + dda * c["a"]
    da_log = jnp.sum(dda * c["dt"], axis=0, keepdims=True) * c["a"]
    dpre = ddt * _sigmoid(c["pre"])
    dxc = jnp.concatenate([dxs] + db_all + dc_all, axis=1)
    return (dxc, dpre, dz, jnp.concatenate(dprev, axis=0), jnp.sum(dpre, axis=0, keepdims=True), da_log, dd_skip,
            dnorm_g)


def _conv_block(x, w, b):
    rows = lax.broadcasted_iota(jnp.int32, x.shape, 0)
    acc = x * w[SSM_CONV - 1:SSM_CONV, :] + b
    for k in range(SSM_CONV - 1):
        shift = SSM_CONV - 1 - k
        acc = acc + _shift_rows(x, rows, shift) * w[k:k + 1, :]
    return (_silu(acc),)


@functools.partial(jax.custom_vjp, nondiff_argnums=(2,))
def _shift_rows(x, rows, shift):
    return jnp.where(rows >= shift, pltpu.roll(x, shift, 0), 0.0)


def _shift_rows_fwd(x, rows, shift):
    return _shift_rows(x, rows, shift), rows


def _shift_rows_bwd(shift, rows, g):
    n = g.shape[0]
    return jnp.where(rows < n - shift, pltpu.roll(g, n - shift, 0), 0.0), None


_shift_rows.defvjp(_shift_rows_fwd, _shift_rows_bwd)


def _merge_block(gates, br_a, br_b):
    return (_sigmoid(gates[:, :D_MODEL]) * br_a + _sigmoid(gates[:, D_MODEL:]) * br_b,)


def _lnres_block(x, y, g, b):
    return (_layer_norm(ALPHA * x + y, g, b),)


def _memln_block(x, g, b):
    return (_layer_norm(x, g, b),)


def _attn_block(q, kv):
    outs = []
    for h in range(X_HEADS):
        qh = q[:, h * X_HEADDIM:(h + 1) * X_HEADDIM].astype(BF16)
        kh = kv[:, h * X_HEADDIM:(h + 1) * X_HEADDIM].astype(BF16)
        vh = kv[:, D_MODEL + h * X_HEADDIM:D_MODEL + (h + 1) * X_HEADDIM].astype(BF16)
        s = lax.dot_general(qh, kh, (((1,), (1,)), ((), ())), preferred_element_type=F32) * (X_HEADDIM ** -0.5)
        s = s - lax.stop_gradient(jnp.max(s, axis=-1, keepdims=True))
        e = jnp.exp(s)
        p = e / jnp.sum(e, axis=-1, keepdims=True)
        outs.append(jnp.dot(p.astype(BF16), vh, preferred_element_type=F32))
    return (jnp.concatenate(outs, axis=1),)


def _swiglu_block(gu):
    return (_silu(gu[:, :FFN_HIDDEN]) * gu[:, FFN_HIDDEN:],)


class _Comm:
    def __init__(self):
        self.gathers = []
        self.scatters = []

    @staticmethod
    def _rows(ref, rows):
        return ref if rows is None else ref.at[pl.ds(rows[0], rows[1])]

    def operands(self):
        ins = [a for a, _, _ in self.gathers] + [a for a, _ in self.scatters]
        shapes = []
        for a, idx, rows in self.gathers:
            blk = a.shape if idx is None else a.shape[1:]
            shapes.append(jax.ShapeDtypeStruct((N_DEV, blk[0] if rows is None else rows[1]) + tuple(blk[1:]), a.dtype))
        for a, rows in self.scatters:
            shapes.append(jax.ShapeDtypeStruct((N_DEV, a.shape[1] if rows is None else rows[1]) + tuple(a.shape[2:]),
                                               a.dtype))
        scratch = []
        for n in (len(self.gathers), len(self.scatters)):
            if n:
                scratch += [pltpu.SemaphoreType.DMA((7 * n,)), pltpu.SemaphoreType.DMA((7 * n,)),
                            pltpu.SemaphoreType.DMA((n,))]
        return ins, shapes, scratch

    def _split(self, in_refs, out_refs, sems):
        ng = len(self.gathers)
        g_sems = sems[:3] if ng else None
        s_sems = sems[3:] if ng else sems
        return in_refs[:ng], in_refs[ng:], out_refs[:ng], out_refs[ng:], g_sems, s_sems

    def _gather_copies(self, i, src_ref, out_ref, sems):
        send_sems, recv_sems, local_sems = sems
        x, y, c = lax.axis_index("x"), lax.axis_index("y"), lax.axis_index("c")
        me, sibling = (x, y, c), (x, y, 1 - c)
        chips = [(1 - x, y), (x, 1 - y), (1 - x, 1 - y)]
        _, idx, rows = self.gathers[i]
        src = self._rows(src_ref if idx is None else src_ref.at[idx], rows)

        def slot(px, py, pc):
            return out_ref.at[4 * px + 2 * py + pc]

        def copy(k, blk, to, from_src=False):
            return pltpu.make_async_remote_copy(
                src_ref=src if from_src else slot(*blk), dst_ref=slot(*blk), send_sem=send_sems.at[7 * i + k],
                recv_sem=recv_sems.at[7 * i + k], device_id=to, device_id_type=pl.DeviceIdType.MESH)

        mine = pltpu.make_async_copy(src, slot(*me), local_sems.at[i])
        first = [copy(0, me, sibling, True)] + [copy(1 + j, me, (*chip, c), True) for j, chip in enumerate(chips)]
        passed = [copy(4 + j, (*chip, c), sibling) for j, chip in enumerate(chips)]
        arrivals = [copy(1 + j, (*chip, c), me) for j, chip in enumerate(chips)]
        from_sibling = [copy(0, sibling, me)] + [copy(4 + j, (*chip, 1 - c), me) for j, chip in enumerate(chips)]
        return mine, first, passed, arrivals, from_sibling

    def _scatter_copies(self, i, src_ref, out_ref, sems):
        send_sems, recv_sems, local_sems = sems
        x, y, c = lax.axis_index("x"), lax.axis_index("y"), lax.axis_index("c")
        me = 4 * x + 2 * y + c
        rows = self.scatters[i][1]
        mine = pltpu.make_async_copy(self._rows(src_ref.at[me], rows), out_ref.at[me], local_sems.at[i])
        copies = []
        for k in range(1, N_DEV):
            px = 1 - x if k & 4 else x
            py = 1 - y if k & 2 else y
            pc = 1 - c if k & 1 else c
            copies.append(pltpu.make_async_remote_copy(
                src_ref=self._rows(src_ref.at[4 * px + 2 * py + pc], rows), dst_ref=out_ref.at[me],
                send_sem=send_sems.at[7 * i + k - 1], recv_sem=recv_sems.at[7 * i + k - 1], device_id=(px, py, pc),
                device_id_type=pl.DeviceIdType.MESH))
        return mine, copies

    def start(self, in_refs, out_refs, sems):
        g_in, s_in, g_out, s_out, g_sems, s_sems = self._split(in_refs, out_refs, sems)
        for i in range(len(self.gathers)):
            mine, first, _, _, _ = self._gather_copies(i, g_in[i], g_out[i], g_sems)
            mine.start()
            for cp in first:
                cp.start()
        for i in range(len(self.scatters)):
            mine, copies = self._scatter_copies(i, s_in[i], s_out[i], s_sems)
            mine.start()
            for cp in copies:
                cp.start()

    def finish(self, in_refs, out_refs, sems):
        g_in, s_in, g_out, s_out, g_sems, s_sems = self._split(in_refs, out_refs, sems)
        parts = [self._gather_copies(i, g_in[i], g_out[i], g_sems) for i in range(len(self.gathers))]
        for j in range(3):
            for _, _, passed, arrivals, _ in parts:
                arrivals[j].wait_recv()
                passed[j].start()
        for mine, first, passed, _, from_sibling in parts:
            for cp in from_sibling:
                cp.wait_recv()
            for cp in first + passed:
                cp.wait_send()
            mine.wait()
        for i in range(len(self.scatters)):
            mine, copies = self._scatter_copies(i, s_in[i], s_out[i], s_sems)
            for cp in copies:
                cp.wait_recv()
            for cp in copies:
                cp.wait_send()
            mine.wait()


def _params(grid):
    return pltpu.CompilerParams(dimension_semantics=("arbitrary",) * len(grid), vmem_limit_bytes=VMEM_LIMIT)


def _call(name, body, *, grid, ins, in_specs, out_shape, out_specs, scratch=(), comm=None, aliases=None):
    n_in, n_out, n_scr = len(ins), len(out_shape), len(scratch)
    aliases = aliases or {}
    if comm is None:
        outs = pl.pallas_call(body, grid=grid, in_specs=list(in_specs), out_specs=list(out_specs),
                              out_shape=list(out_shape), scratch_shapes=list(scratch), name=name,
                              input_output_aliases=aliases, compiler_params=_params(grid))(*ins)
        return list(outs), []
    c_ins, c_shapes, c_scratch = comm.operands()
    nci, nco = len(c_ins), len(c_shapes)
    anywhere = pl.BlockSpec(memory_space=pl.ANY)

    def carrier(*refs):
        main_in, comm_in = refs[:n_in], refs[n_in:n_in + nci]
        o0 = n_in + nci
        main_out, comm_out = refs[o0:o0 + n_out], refs[o0 + n_out:o0 + n_out + nco]
        s0 = o0 + n_out + nco
        main_scr, comm_scr = refs[s0:s0 + n_scr], refs[s0 + n_scr:]
        first = pl.program_id(0) == 0
        last = pl.program_id(0) == grid[0] - 1
        for ax in range(1, len(grid)):
            first = first & (pl.program_id(ax) == 0)
            last = last & (pl.program_id(ax) == grid[ax] - 1)

        @pl.when(first)
        def _():
            comm.start(comm_in, comm_out, comm_scr)

        body(*main_in, *main_out, *main_scr)

        @pl.when(last)
        def _():
            comm.finish(comm_in, comm_out, comm_scr)

    outs = pl.pallas_call(carrier, grid=grid, in_specs=list(in_specs) + [anywhere] * nci,
                          out_specs=list(out_specs) + [anywhere] * nco, out_shape=list(out_shape) + c_shapes,
                          scratch_shapes=list(scratch) + c_scratch, name=name, input_output_aliases=aliases,
                          compiler_params=_params(grid))(*ins, *c_ins)
    return list(outs[:n_out]), list(outs[n_out:])


def _comm_only(name, comm):
    c_ins, c_shapes, c_scratch = comm.operands()
    nci, nco = len(c_ins), len(c_shapes)
    anywhere = pl.BlockSpec(memory_space=pl.ANY)

    def body(*refs):
        comm.start(refs[:nci], refs[nci:nci + nco], refs[nci + nco:])
        comm.finish(refs[:nci], refs[nci:nci + nco], refs[nci + nco:])

    return list(pl.pallas_call(body, in_specs=[anywhere] * nci, out_specs=[anywhere] * nco, out_shape=c_shapes,
                               scratch_shapes=c_scratch, name=name)(*c_ins))


def _stage_fwd(name, f, grid, ins, in_specs, out_shapes, out_specs, comm=None):
    n_in = len(ins)

    def body(*refs):
        res = f(*[r[...].astype(F32) for r in refs[:n_in]])
        for o_ref, val in zip(refs[n_in:], res):
            o_ref[...] = val.astype(o_ref.dtype)

    return _call(name, body, grid=grid, ins=ins, in_specs=in_specs, out_shape=out_shapes, out_specs=out_specs, comm=comm)


def _stage_bwd(name, f, grid, ins, in_specs, cts, ct_specs, grads, comm=None):
    n_in = len(ins)
    flat_cts = [c for group in cts for c in group]
    flat_ct_specs = [s for group, spec in zip(cts, ct_specs) for s in (spec,) * len(group)]
    n_ct = len(flat_cts)
    diff = [g[0] for g in grads]
    buffers = [(k, g[4]) for k, g in enumerate(grads) if len(g) > 4]
    n_buf = len(buffers)

    def body(*refs):
        vals = [r[...].astype(F32) for r in refs[:n_in]]
        ct_refs = refs[n_in:n_in + n_ct]
        g_refs = refs[n_in + n_ct + n_buf:]
        ct_vals, pos = [], 0
        for group in cts:
            acc = ct_refs[pos][...].astype(F32)
            for j in range(1, len(group)):
                acc = acc + ct_refs[pos + j][...].astype(F32)
            ct_vals.append(acc)
            pos += len(group)

        def g_fn(*dvals):
            full = list(vals)
            for i, dv in zip(diff, dvals):
                full[i] = dv
            return f(*full)

        _, vjp = jax.vjp(g_fn, *[vals[i] for i in diff])
        gvals = vjp(tuple(ct_vals))
        for gspec, g_ref, gval in zip(grads, g_refs, gvals):
            acc_axes = gspec[1]
            if not acc_axes:
                g_ref[...] = gval.astype(g_ref.dtype)
            else:
                first = pl.program_id(acc_axes[0]) == 0
                for ax in acc_axes[1:]:
                    first = first & (pl.program_id(ax) == 0)

                @pl.when(first)
                def _():
                    g_ref[...] = jnp.zeros_like(g_ref)

                g_ref[...] += gval.astype(g_ref.dtype)

    out_shapes, out_specs = [], []
    for gspec in grads:
        shape, spec = gspec[3] if len(gspec) > 3 else (ins[gspec[0]].shape, in_specs[gspec[0]])
        out_shapes.append(jax.ShapeDtypeStruct(shape, gspec[2]))
        out_specs.append(spec)
    anywhere = pl.BlockSpec(memory_space=pl.ANY)
    return _call(name, body, grid=grid, ins=list(ins) + flat_cts + [b for _, b in buffers],
                 in_specs=list(in_specs) + flat_ct_specs + [anywhere] * n_buf, out_shape=out_shapes, out_specs=out_specs,
                 comm=comm, aliases={n_in + n_ct + j: k for j, (k, _) in enumerate(buffers)})


def _pick_tile(n, candidates):
    for c in candidates:
        if n % c == 0:
            return c
    return n


def _matmul(name, a, b, *, ta=False, tb=False, add=None, out_dtype=F32, comm=None):
    if ta:
        k_dim, m = a.shape
    else:
        m, k_dim = a.shape
    n = b.shape[0] if tb else b.shape[1]
    assert (b.shape[1] if tb else b.shape[0]) == k_dim and not (ta and tb)
    tm = _pick_tile(m, (1024, 1408, 512, 256, 128))
    tn = _pick_tile(n, (1024, 1408, 512, 256, 128))
    if ta:
        tk = _pick_tile(k_dim, (1024, 512, 256, 128))
    elif k_dim <= 2816:
        tk = k_dim
    else:
        tk = _pick_tile(k_dim, (1408, 1024, 512, 256, 128))
    nk = k_dim // tk
    grid = (m // tm, n // tn, nk)
    a_spec = pl.BlockSpec((tk, tm), lambda i, j, k: (k, i)) if ta else pl.BlockSpec((tm, tk), lambda i, j, k: (i, k))
    b_spec = pl.BlockSpec((tn, tk), lambda i, j, k: (j, k)) if tb else pl.BlockSpec((tk, tn), lambda i, j, k: (k, j))
    o_spec = pl.BlockSpec((tm, tn), lambda i, j, k: (i, j))
    dims = (((0 if ta else 1,), (1 if tb else 0,)), ((), ()))
    has_add = add is not None

    def body(*refs):
        a_ref, b_ref = refs[0], refs[1]
        add_ref = refs[2] if has_add else None
        o_ref, acc_ref = refs[-2], refs[-1]
        k = pl.program_id(2)
        part = lax.dot_general(a_ref[...].astype(BF16), b_ref[...].astype(BF16), dims, preferred_element_type=F32)

        def finish(res):
            if has_add:
                res = res + add_ref[...].astype(F32)
            o_ref[...] = res.astype(o_ref.dtype)

        if nk == 1:
            finish(part)
        else:
            @pl.when(k == 0)
            def _():
                acc_ref[...] = part

            @pl.when((k > 0) & (k < nk - 1))
            def _():
                acc_ref[...] += part

            @pl.when(k == nk - 1)
            def _():
                finish(acc_ref[...] + part)

    ins = [a, b] + ([add] if has_add else [])
    in_specs = [a_spec, b_spec] + ([o_spec] if has_add else [])
    acc_shape = (tm, tn) if nk > 1 else (8, 128)
    outs, comm_outs = _call(name, body, grid=grid, ins=ins, in_specs=in_specs,
                            out_shape=[jax.ShapeDtypeStruct((m, n), out_dtype)], out_specs=[o_spec],
                            scratch=[pltpu.VMEM(acc_shape, F32)], comm=comm)
    return outs[0], comm_outs


def _matmul_lnres(name, a_fn, a_ins, a_specs, tm, b, x, g, beta, comm=None):
    m = x.shape[0]
    k_dim, n = b.shape
    n_a = len(a_ins)
    row = lambda w: pl.BlockSpec((tm, w), lambda i: (i, 0))
    whole = lambda shape: pl.BlockSpec(shape, lambda i: (0, 0))

    def body(*refs):
        b_ref, x_ref, g_ref, beta_ref = refs[n_a:n_a + 4]
        y_ref, h_ref, hb_ref = refs[-3:]
        if a_fn is None:
            a = refs[0][...].astype(BF16)
        else:
            (a,) = a_fn(*[r[...].astype(F32) for r in refs[:n_a]])
            a = a.astype(BF16)
            refs[n_a + 4][...] = a
        y = jnp.dot(a, b_ref[...].astype(BF16), preferred_element_type=F32).astype(y_ref.dtype)
        y_ref[...] = y
        (h,) = _lnres_block(x_ref[...], y.astype(F32), g_ref[...], beta_ref[...])
        h_ref[...] = h
        hb_ref[...] = h.astype(hb_ref.dtype)

    sds = jax.ShapeDtypeStruct
    a_out = ([sds((m, k_dim), BF16)], [row(k_dim)]) if a_fn is not None else ([], [])
    return _call(name, body, grid=(m // tm,), ins=list(a_ins) + [b, x, g, beta],
                 in_specs=list(a_specs) + [whole((k_dim, n)), row(n), whole((1, n)), whole((1, n))],
                 out_shape=a_out[0] + [sds((m, n), BF16), sds((m, n), F32), sds((m, n), BF16)],
                 out_specs=a_out[1] + [row(n), row(n), row(n)], comm=comm)


SSD_STATE = (SSM_GROUPS * SSM_STATE, SSM_RPG * SSM_HEADDIM)


def _ssd_fwd(xc, dt_raw, proj, dt_bias, a_log, d_skip, norm_g, nb, nc, comm=None):
    t = xc.shape[0]
    row = lambda b, c: (b * nc + c, 0)
    par = lambda shape: pl.BlockSpec(shape, lambda b, c: (0, 0))

    def body(xc_ref, dt_ref, z_ref, dtb_ref, al_ref, ds_ref, ng_ref, e_ref, y_ref, ypre_ref, prev_ref, st_ref):
        @pl.when(pl.program_id(1) == 0)
        def _():
            st_ref[...] = jnp.zeros_like(st_ref)

        prev = st_ref[...]
        prev_ref[0, 0] = prev
        y, y_pre, new_state = _ssd_forward(xc_ref[...], dt_ref[...], z_ref[...].astype(F32), prev, dtb_ref[...],
                                           al_ref[...], ds_ref[...], ng_ref[...], e_ref[...])
        y_ref[...] = y.astype(y_ref.dtype)
        ypre_ref[...] = y_pre
        st_ref[...] = new_state

    return _call(
        "ssd_fwd", body, grid=(nb, nc), ins=[xc, dt_raw, proj, dt_bias, a_log, d_skip, norm_g, _head_expander()],
        in_specs=[pl.BlockSpec((CHUNK, SSM_CONV_DIM), row), pl.BlockSpec((CHUNK, 128), row),
                  pl.BlockSpec((CHUNK, SSM_INNER), lambda b, c: (b * nc + c, 1)),
                  par((1, 128)), par((1, 128)), par((1, SSM_INNER)), par((1, SSM_INNER)), par((128, SSM_INNER))],
        out_specs=[pl.BlockSpec((CHUNK, SSM_INNER), row), pl.BlockSpec((CHUNK, SSM_INNER), row),
                   pl.BlockSpec((1, 1) + SSD_STATE, lambda b, c: (b, c, 0, 0))],
        out_shape=[jax.ShapeDtypeStruct((t, SSM_INNER), BF16), jax.ShapeDtypeStruct((t, SSM_INNER), F32),
                   jax.ShapeDtypeStruct((nb, nc) + SSD_STATE, F32)],
        scratch=[pltpu.VMEM(SSD_STATE, F32)], comm=comm)


def _ssd_bwd(xc, dt_raw, proj, prevs, y_pre, dt_bias, a_log, d_skip, norm_g, dy, dproj, nb, nc, comm=None):
    t = xc.shape[0]
    row = lambda b, c: (b * nc + (nc - 1 - c), 0)
    par = lambda shape: pl.BlockSpec(shape, lambda b, c: (0, 0))
    z_spec = pl.BlockSpec((CHUNK, SSM_INNER), lambda b, c: (b * nc + (nc - 1 - c), 1))

    def body(xc_ref, dt_ref, z_ref, prev_ref, ypre_ref, dtb_ref, al_ref, ds_ref, ng_ref, e_ref, dy_ref, _,
             dxc_ref, ddt_ref, dz_ref, ddtb_ref, dal_ref, dds_ref, dng_ref, dst_ref):
        @pl.when(pl.program_id(1) == 0)
        def _():
            dst_ref[...] = jnp.zeros_like(dst_ref)

        @pl.when((pl.program_id(0) == 0) & (pl.program_id(1) == 0))
        def _():
            ddtb_ref[...] = jnp.zeros_like(ddtb_ref)
            dal_ref[...] = jnp.zeros_like(dal_ref)
            dds_ref[...] = jnp.zeros_like(dds_ref)
            dng_ref[...] = jnp.zeros_like(dng_ref)

        dxc, ddt, dz, dprev, ddtb, dal, dds, dng = _ssd_backward(
            xc_ref[...], dt_ref[...], z_ref[...].astype(F32), prev_ref[0, 0], ypre_ref[...], dtb_ref[...], al_ref[...],
            ds_ref[...], ng_ref[...], e_ref[...], dy_ref[...].astype(F32), dst_ref[...])
        dxc_ref[...] = dxc
        ddt_ref[...] = ddt.astype(ddt_ref.dtype)
        dz_ref[...] = dz.astype(dz_ref.dtype)
        dst_ref[...] = dprev
        ddtb_ref[...] += ddtb
        dal_ref[...] += dal
        dds_ref[...] += dds
        dng_ref[...] += dng

    return _call(
        "ssd_bwd", body, grid=(nb, nc),
        ins=[xc, dt_raw, proj, prevs, y_pre, dt_bias, a_log, d_skip, norm_g, _head_expander(), dy, dproj],
        in_specs=[pl.BlockSpec((CHUNK, SSM_CONV_DIM), row), pl.BlockSpec((CHUNK, DT_LANES), row), z_spec,
                  pl.BlockSpec((1, 1) + SSD_STATE, lambda b, c: (b, nc - 1 - c, 0, 0)),
                  pl.BlockSpec((CHUNK, SSM_INNER), row),
                  par((1, 128)), par((1, 128)), par((1, SSM_INNER)), par((1, SSM_INNER)), par((128, SSM_INNER)),
                  pl.BlockSpec((CHUNK, SSM_INNER), row), pl.BlockSpec(memory_space=pl.ANY)],
        out_specs=[pl.BlockSpec((CHUNK, SSM_CONV_DIM), row), pl.BlockSpec((CHUNK, DT_LANES), row), z_spec,
                   par((1, 128)), par((1, 128)), par((1, 128)), par((1, SSM_INNER))],
        out_shape=[jax.ShapeDtypeStruct((t, SSM_CONV_DIM), F32), jax.ShapeDtypeStruct((t, DT_LANES), BF16),
                   jax.ShapeDtypeStruct(dproj.shape, dproj.dtype), jax.ShapeDtypeStruct((1, 128), F32),
                   jax.ShapeDtypeStruct((1, 128), F32), jax.ShapeDtypeStruct((1, 128), F32),
                   jax.ShapeDtypeStruct((1, SSM_INNER), F32)],
        scratch=[pltpu.VMEM(SSD_STATE, F32)], comm=comm, aliases={11: 2})


def _loss_head(y, target):
    t, d = y.shape
    tm = _pick_tile(t, (256,))
    blk = pl.BlockSpec((tm, d), lambda i: (i, 0))

    def body(y_ref, t_ref, loss_ref, dy_ref):
        err = y_ref[...] - t_ref[...]
        dy_ref[...] = err * (1.0 / d)

        @pl.when(pl.program_id(0) == 0)
        def _():
            loss_ref[...] = jnp.zeros_like(loss_ref)

        loss_ref[...] += 0.5 * jnp.sum(jnp.mean(err * err, axis=-1, keepdims=True), axis=0, keepdims=True)

    return _call("loss_head", body, grid=(t // tm,), ins=[y, target], in_specs=[blk, blk],
                 out_specs=[pl.BlockSpec((1, 1), lambda i: (0, 0)), blk],
                 out_shape=[jax.ShapeDtypeStruct((1, 1), F32), jax.ShapeDtypeStruct((t, d), F32)])[0]


def _adamw_math(g, w, m, v):
    m_new = ADAM_B1 * m + (1.0 - ADAM_B1) * g
    v_new = ADAM_B2 * v + (1.0 - ADAM_B2) * jnp.square(g)
    m_hat = m_new / (1.0 - ADAM_B1 ** ADAM_STEP)
    v_hat = v_new / (1.0 - ADAM_B2 ** ADAM_STEP)
    delta = -ADAM_LR * (m_hat / (jnp.sqrt(v_hat) + ADAM_EPS) + ADAM_WD * w)
    return delta, m_new, v_new


def _adamw_sharded(name, parts, w, m, v, comm=None):
    _, a, b = w.shape
    tr = _pick_tile(a, (128,))
    nt = a // tr
    part_specs = [pl.BlockSpec((N_DEV, tr, b),
                               (lambda l, i, _k=k: (0, jnp.where(l == _k, i, jnp.where(l > _k, nt - 1, 0)), 0)))
                  for k in range(DEPTH)]
    blk = pl.BlockSpec((1, tr, b), lambda l, i: (l, i, 0))

    def body(*refs):
        p_refs = refs[:DEPTH]
        w_ref, m_ref, v_ref, g_out, d_out, m_out, v_out = refs[DEPTH:]
        for k in range(DEPTH):
            @pl.when(pl.program_id(0) == k)
            def _(p_ref=p_refs[k]):
                g = p_ref[0].astype(F32)
                for p in range(1, N_DEV):
                    g = g + p_ref[p].astype(F32)
                delta, m_new, v_new = _adamw_math(g, w_ref[0], m_ref[0], v_ref[0])
                g_out[0] = g
                d_out[0] = delta
                m_out[0] = m_new
                v_out[0] = v_new

    return _call(name, body, grid=(DEPTH, nt), ins=list(parts) + [w, m, v], in_specs=part_specs + [blk, blk, blk],
                 out_specs=[blk] * 4, out_shape=[jax.ShapeDtypeStruct(w.shape, F32)] * 4, comm=comm)


def _adamw_small(name, g, w, m, v):
    full = pl.BlockSpec(w.shape, lambda i: (0, 0))

    def body(g_ref, w_ref, m_ref, v_ref, d_out, m_out, v_out):
        delta, m_new, v_new = _adamw_math(g_ref[...], w_ref[...], m_ref[...], v_ref[...])
        d_out[...] = delta
        m_out[...] = m_new
        v_out[...] = v_new

    return _call(name, body, grid=(1,), ins=[g, w, m, v], in_specs=[full] * 4, out_specs=[full] * 3,
                 out_shape=[jax.ShapeDtypeStruct(w.shape, F32)] * 3)[0]


def _sum_parts(name, parts):
    n_parts, rows, cols = parts.shape
    tr = _pick_tile(rows, (512, 256, 128, 64, 32, 16, 8))

    def body(p_ref, o_ref):
        acc = p_ref[0]
        for p in range(1, n_parts):
            acc = acc + p_ref[p]
        o_ref[...] = acc

    return _call(name, body, grid=(rows // tr,), ins=[parts],
                 in_specs=[pl.BlockSpec((n_parts, tr, cols), lambda i: (0, i, 0))],
                 out_specs=[pl.BlockSpec((tr, cols), lambda i: (i, 0))],
                 out_shape=[jax.ShapeDtypeStruct((rows, cols), parts.dtype)])[0][0]


W_IN_SHARD = IN_COLS // N_DEV


def _pack_w_in(gathered):
    r = gathered.shape[1]
    tr = _pick_tile(r, (128,))

    def body(g_ref, main_ref, dt_ref):
        w = jnp.concatenate([g_ref[j].astype(F32) for j in range(N_DEV)], axis=1)
        main_ref[...] = jnp.concatenate([w[:, :XBC_COL], w[:, GA_COL:], w[:, XBC_COL:DT_COL]],
                                        axis=1).astype(main_ref.dtype)
        dt_ref[...] = jnp.concatenate([w[:, DT_COL:GA_COL], jnp.zeros((tr, DT_LANES - SSM_HEADS), F32)],
                                      axis=1).astype(dt_ref.dtype)

    return _call("pack_w_in", body, grid=(r // tr,), ins=[gathered],
                 in_specs=[pl.BlockSpec((N_DEV, tr, W_IN_SHARD), lambda i: (0, i, 0))],
                 out_specs=[pl.BlockSpec((tr, P_COLS), lambda i: (i, 0)), pl.BlockSpec((tr, DT_LANES), lambda i: (i, 0))],
                 out_shape=[jax.ShapeDtypeStruct((r, P_COLS), gathered.dtype),
                            jax.ShapeDtypeStruct((r, DT_LANES), gathered.dtype)])[0]


def _unpack_w_in(main, dt):
    r = main.shape[0]
    tr = _pick_tile(r, (128,))

    def body(main_ref, dt_ref, o_ref):
        main = main_ref[...].astype(F32)
        w = jnp.concatenate([main[:, :P_GATE], main[:, P_XBC:], dt_ref[...].astype(F32)[:, :SSM_HEADS],
                             main[:, P_GATE:P_XBC]], axis=1)
        for j in range(N_DEV):
            o_ref[j] = w[:, j * W_IN_SHARD:(j + 1) * W_IN_SHARD].astype(o_ref.dtype)

    return _call("unpack_w_in", body, grid=(r // tr,), ins=[main, dt],
                 in_specs=[pl.BlockSpec((tr, P_COLS), lambda i: (i, 0)), pl.BlockSpec((tr, DT_LANES), lambda i: (i, 0))],
                 out_specs=[pl.BlockSpec((N_DEV, tr, W_IN_SHARD), lambda i: (0, i, 0))],
                 out_shape=[jax.ShapeDtypeStruct((N_DEV, r, W_IN_SHARD), main.dtype)])[0][0]


def _pad_heads(v):
    return jnp.pad(v, (0, 128 - SSM_HEADS)).reshape(1, 128)


def _run_step(x, mem, target, small, ex):
    nb, s, d = x.shape
    t = nb * s
    nc = s // CHUNK
    rows = _pick_tile(t, (256,))
    rows_wide = _pick_tile(t, (512, 256))
    tq = _pick_tile(s, (512, 256))
    vec = lambda a: a.reshape(1, -1)
    full1 = lambda shape: pl.BlockSpec(shape, lambda i: (0,) * len(shape))
    row1 = lambda tm, w: pl.BlockSpec((tm, w), lambda i: (i, 0))
    sds = jax.ShapeDtypeStruct

    def mm(call, l, a, b, **kw):
        comm = ex.before(call, l)
        out, comm_outs = _matmul(call, a, b, comm=comm, **kw)
        if comm is not None:
            ex.after(call, l, comm_outs)
        return out

    def stage_bwd(call, l, *args):
        comm = ex.before(call, l)
        outs, comm_outs = _stage_bwd(call, *args, comm=comm)
        if comm is not None:
            ex.after(call, l, comm_outs)
        return outs

    def stage_fwd(call, l, *args):
        comm = ex.before(call, l)
        outs, comm_outs = _stage_fwd(call, *args, comm=comm)
        if comm is not None:
            ex.after(call, l, comm_outs)
        return outs

    mem_specs = [row1(256, d), full1((1, d)), full1((1, d))]
    mem_ins = [mem.reshape(nb * MEM_LEN, d), vec(small["mem_ln_g"]), vec(small["mem_ln_b"])]
    (mem_n,) = stage_fwd("memln_fwd", 0, _memln_block, (nb * MEM_LEN // 256,), mem_ins, mem_specs,
                          [sds((nb * MEM_LEN, d), BF16)], [row1(256, d)])

    h = x.reshape(t, d)
    h_bf = h.astype(BF16)
    ln_specs = [row1(rows, d), row1(rows, d), full1((1, d)), full1((1, d))]
    saved = []
    for l in range(DEPTH):
        sv = {"h_bf": h_bf}
        w_p, w_dt = ex.weight("w_in", l)
        proj = mm("mm_in", l, h_bf, w_p, out_dtype=BF16)
        dt_raw = mm("mm_dt", l, h_bf, w_dt)
        sv["proj"] = proj
        sgu_ins = [proj, vec(small["sg_ln_g"][l]), vec(small["sg_ln_b"][l]), small["sg_w"][l], small["sg_b"][l].T]
        sgu_specs = [pl.BlockSpec((CHUNK, 2 * d), lambda i: (i, 0)), full1((1, d)), full1((1, d)),
                     full1((SG_GROUPS, CHUNK, CHUNK)), full1((CHUNK, SG_GROUPS))]
        (a_out,) = stage_fwd("sgu_fwd", l, _sgu_block, (t // CHUNK,), sgu_ins, sgu_specs, [sds((t, d), BF16)],
                              [row1(CHUNK, d)])
        sv["sgu"] = (sgu_ins, sgu_specs)
        sv["a_out"] = a_out
        cw = 256
        conv_ins = [proj, small["conv_w"][l], vec(small["conv_b"][l])]
        conv_specs = [pl.BlockSpec((s, cw), lambda j, b: (b, P_XBC // cw + j)),
                      pl.BlockSpec((SSM_CONV, cw), lambda j, b: (0, j)), pl.BlockSpec((1, cw), lambda j, b: (0, j))]
        conv_out_spec = pl.BlockSpec((s, cw), lambda j, b: (b, j))
        (xc,) = stage_fwd("conv_fwd", l, _conv_block, (SSM_CONV_DIM // cw, nb), conv_ins, conv_specs,
                           [sds((t, SSM_CONV_DIM), F32)], [conv_out_spec])
        sv["conv"] = (conv_ins, conv_specs, conv_out_spec)
        ssd_par = [_pad_heads(small["dt_bias"][l]), _pad_heads(small["a_log"][l]),
                   vec(jnp.repeat(small["d_skip"][l], SSM_HEADDIM)), vec(small["ssm_norm_g"][l])]
        comm = ex.before("ssd_fwd", l)
        (y_ssd, y_pre, prevs), comm_outs = _ssd_fwd(xc, dt_raw, proj, *ssd_par, nb, nc, comm=comm)
        if comm is not None:
            ex.after("ssd_fwd", l, comm_outs)
        sv["ssd"] = (xc, dt_raw, prevs, y_pre, ssd_par)
        sv["y_ssd"] = y_ssd
        br_a = mm("mm_sq", l, a_out, ex.weight("p_a", l), out_dtype=BF16)
        br_b = mm("mm_pb", l, y_ssd, ex.weight("p_b", l), out_dtype=BF16)
        merge_ins = [proj, br_a, br_b]
        merge_out_spec = row1(rows_wide, d)
        merge_specs = [pl.BlockSpec((rows_wide, 2 * d), lambda i: (i, P_GATE // (2 * d))), merge_out_spec, merge_out_spec]
        sv["merge"] = (merge_ins, merge_specs, merge_out_spec)
        ln_par = [(vec(small["ln_g"][l, k]), vec(small["ln_b"][l, k])) for k in range(3)]

        def fused(call, a_fn, a_ins, a_specs, tm, w, x_in, par):
            comm = ex.before(call, l)
            outs, comm_outs = _matmul_lnres(call, a_fn, a_ins, a_specs, tm, w, x_in, *par, comm=comm)
            if comm is not None:
                ex.after(call, l, comm_outs)
            return outs

        merged, y1, h1, h1_bf = fused("mm_mix_ln", _merge_block, merge_ins, merge_specs, rows_wide,
                                      ex.weight("w_mix_o", l), h, ln_par[0])
        sv["merged"] = merged
        sv["ln1"] = [h, y1, *ln_par[0]]
        q = mm("mm_sq", l, h1_bf, ex.weight("w_xq", l), out_dtype=BF16)
        kv = mm("mm_kv", l, mem_n, ex.weight("w_xkv", l), out_dtype=BF16)
        attn_ins = [q, kv]
        attn_out_spec = pl.BlockSpec((tq, d), lambda b, i: (b * (s // tq) + i, 0))
        attn_specs = [attn_out_spec, pl.BlockSpec((MEM_LEN, 2 * d), lambda b, i: (b, 0))]
        (o,) = stage_fwd("attn_fwd", l, _attn_block, (nb, s // tq), attn_ins, attn_specs, [sds((t, d), BF16)],
                          [attn_out_spec])
        sv["attn"] = (attn_ins, attn_specs, attn_out_spec)
        sv["o"] = o
        sv["h1_bf"] = h1_bf
        y2, h2, h2_bf = fused("mm_xo_ln", None, [o], [row1(rows_wide, d)], rows_wide, ex.weight("w_xo", l), h1, ln_par[1])
        sv["ln2"] = [h1, y2, *ln_par[1]]
        sv["h2_bf"] = h2_bf
        gu = mm("mm_ffn_in", l, h2_bf, ex.weight("w_ffn_in", l), out_dtype=BF16)
        act, y3, h3, h3_bf = fused("mm_ffn_out_ln", _swiglu_block, [gu], [row1(rows, 2 * FFN_HIDDEN)], rows,
                                   ex.weight("w_ffn_out", l), h2, ln_par[2])
        sv["gu"] = gu
        sv["act"] = act
        sv["ln3"] = [h2, y3, *ln_par[2]]
        h, h_bf = h3, h3_bf
        saved.append(sv)

    loss, dh = _loss_head(h, target.reshape(t, d))

    g_small = {n: [None] * DEPTH for n in SMALL_REP + SMALL_SH if n not in ("mem_ln_g", "mem_ln_b")}
    dmem_n = []
    ln_grads = [(0, (), F32), (1, (), BF16), (2, (0,), F32), (3, (0,), F32)]
    for l in reversed(range(DEPTH)):
        sv = saved[l]
        dln_g, dln_b = [None] * 3, [None] * 3
        dres, dy3, dln_g[2], dln_b[2] = stage_bwd("lnres_bwd", l, _lnres_block, (t // rows,), sv["ln3"], ln_specs,
                                                  [(dh,)], [row1(rows, d)], ln_grads)
        ex.grad("w_ffn_out", l, mm("mm_ffn_out_dw", l, sv["act"], dy3, ta=True, out_dtype=BF16))
        dact = mm("mm_ffn_out_dx", l, dy3, ex.weight("w_ffn_out", l), tb=True, out_dtype=BF16)
        (dgu,) = stage_bwd("swiglu_bwd", l, _swiglu_block, (t // 128,), [sv["gu"]], [row1(128, 2 * FFN_HIDDEN)],
                           [(dact,)], [row1(128, FFN_HIDDEN)], [(0, (), BF16)])
        ex.grad("w_ffn_in", l, mm("mm_ffn_in_dw", l, sv["h2_bf"], dgu, ta=True, out_dtype=BF16))
        dh2 = mm("mm_ffn_in_dx", l, dgu, ex.weight("w_ffn_in", l), tb=True, add=dres)
        dres, dy2, dln_g[1], dln_b[1] = stage_bwd("lnres_bwd", l, _lnres_block, (t // rows,), sv["ln2"], ln_specs,
                                                  [(dh2,)], [row1(rows, d)], ln_grads)
        ex.grad("w_xo", l, mm("mm_sq_dw", l, sv["o"], dy2, ta=True, out_dtype=BF16))
        do = mm("mm_sq_dx", l, dy2, ex.weight("w_xo", l), tb=True, out_dtype=BF16)
        attn_ins, attn_specs, attn_out_spec = sv["attn"]
        dq, dkv = stage_bwd("attn_bwd", l, _attn_block, (nb, s // tq), attn_ins, attn_specs, [(do,)], [attn_out_spec],
                            [(0, (), BF16), (1, (1,), F32)])
        ex.grad("w_xq", l, mm("mm_sq_dw", l, sv["h1_bf"], dq, ta=True, out_dtype=BF16))
        dh1 = mm("mm_sq_dx", l, dq, ex.weight("w_xq", l), tb=True, add=dres)
        ex.grad("w_xkv", l, mm("mm_kv_dw", l, mem_n, dkv, ta=True, out_dtype=BF16))
        dmem_n.append(mm("mm_kv_dx", l, dkv, ex.weight("w_xkv", l), tb=True))
        dres, dy1, dln_g[0], dln_b[0] = stage_bwd("lnres_bwd", l, _lnres_block, (t // rows,), sv["ln1"], ln_specs,
                                                  [(dh1,)], [row1(rows, d)], ln_grads)
        g_small["ln_g"][l] = jnp.concatenate(dln_g, axis=0)
        g_small["ln_b"][l] = jnp.concatenate(dln_b, axis=0)
        ex.grad("w_mix_o", l, mm("mm_sq_dw", l, sv["merged"], dy1, ta=True, out_dtype=BF16))
        dmerged = mm("mm_sq_dx", l, dy1, ex.weight("w_mix_o", l), tb=True, out_dtype=BF16)
        merge_ins, merge_specs, merge_out_spec = sv["merge"]
        dproj, dbr_a, dbr_b = stage_bwd("merge_bwd", l, _merge_block, (t // rows_wide,), merge_ins, merge_specs,
                                        [(dmerged,)], [merge_out_spec],
                                        [(0, (), BF16, ((t, P_COLS), merge_specs[0])), (1, (), BF16), (2, (), BF16)])
        ex.grad("p_a", l, mm("mm_sq_dw", l, sv["a_out"], dbr_a, ta=True, out_dtype=BF16))
        da_out = mm("mm_sq_dx", l, dbr_a, ex.weight("p_a", l), tb=True, out_dtype=BF16)
        ex.grad("p_b", l, mm("mm_pb_dw", l, sv["y_ssd"], dbr_b, ta=True, out_dtype=BF16))
        dy_ssd = mm("mm_pb_dx", l, dbr_b, ex.weight("p_b", l), tb=True, out_dtype=BF16)
        sgu_ins, sgu_specs = sv["sgu"]
        dproj, dsg_ln_g, dsg_ln_b, dsg_w, dsg_b = stage_bwd(
            "sgu_bwd", l, _sgu_block, (t // CHUNK,), sgu_ins, sgu_specs, [(da_out,)], [row1(CHUNK, d)],
            [(0, (), BF16, ((t, P_COLS), sgu_specs[0]), dproj), (1, (0,), F32), (2, (0,), F32), (3, (0,), F32),
             (4, (0,), F32)])
        g_small["sg_ln_g"][l], g_small["sg_ln_b"][l], g_small["sg_w"][l], g_small["sg_b"][l] = (
            dsg_ln_g[0], dsg_ln_b[0], dsg_w, dsg_b.T)
        xc, dt_raw, prevs, y_pre, ssd_par = sv["ssd"]
        comm = ex.before("ssd_bwd", l)
        (dxc, ddt, dproj, ddtb, dal, dds, dng), comm_outs = _ssd_bwd(xc, dt_raw, sv["proj"], prevs, y_pre, *ssd_par,
                                                                     dy_ssd, dproj, nb, nc, comm=comm)
        if comm is not None:
            ex.after("ssd_bwd", l, comm_outs)
        g_small["dt_bias"][l], g_small["a_log"][l], g_small["d_skip"][l] = (
            ddtb[0, :SSM_HEADS], dal[0, :SSM_HEADS], dds[0, :SSM_HEADS])
        g_small["ssm_norm_g"][l] = dng[0]
        conv_ins, conv_specs, conv_out_spec = sv["conv"]
        dproj, dconv_w, dconv_b = stage_bwd("conv_bwd", l, _conv_block, (SSM_CONV_DIM // 256, nb), conv_ins, conv_specs,
                                            [(dxc,)], [conv_out_spec],
                                            [(0, (), BF16, ((t, P_COLS), conv_specs[0]), dproj), (1, (1,), F32),
                                             (2, (1,), F32)])
        g_small["conv_w"][l], g_small["conv_b"][l] = dconv_w, dconv_b[0]
        if l == 0:
            dmg, dmb = stage_bwd("memln_bwd", l, _memln_block, (nb * MEM_LEN // 256,), mem_ins, mem_specs,
                                 [tuple(dmem_n)], [row1(256, d)], [(1, (0,), F32), (2, (0,), F32)])
            done = {n: jnp.stack(g, axis=0) for n, g in g_small.items()}
            done["mem_ln_g"], done["mem_ln_b"] = dmg[0], dmb[0]
            ex.small_grads(done)
        w_p, w_dt = ex.weight("w_in", l)
        g_dt = mm("mm_dt_dw", l, sv["h_bf"], ddt, ta=True, out_dtype=BF16)
        ex.grad("w_in", l, _unpack_w_in(mm("mm_in_dw", l, sv["h_bf"], dproj, ta=True, out_dtype=BF16), g_dt))
        dh = mm("mm_in_dx", l, dproj, w_p, tb=True, add=mm("mm_dt_dx", l, ddt, w_dt, tb=True, add=dres))

    return loss, dh.reshape(nb, s, d)


def _pack_flat(arrays, rows):
    flat = jnp.concatenate([a.reshape(-1) for a in arrays])
    return jnp.pad(flat, (0, rows * 128 - flat.shape[0])).reshape(rows, 128)


def _unpack_flat(packed, shapes):
    lead = packed.shape[:-2]
    flat = packed.reshape(lead + (-1,))
    out, pos = [], 0
    for shape in shapes:
        n = math.prod(shape)
        out.append(flat[..., pos:pos + n].reshape(lead + tuple(shape)))
        pos += n
    return out


def _small_rows(n_elems):
    return -(-n_elems // (128 * SMALL_ROW_TILE)) * SMALL_ROW_TILE


def _from_shards(name, gathered):
    _, a, b = gathered.shape
    if name == "w_in":
        return tuple(_pack_w_in(gathered))
    if name in BIG_COL_SHARDED:
        return _join_columns(gathered)
    return gathered.reshape(N_DEV * a, b)


def _to_shards(name, g):
    if name == "w_in":
        return g
    if name in BIG_COL_SHARDED:
        return _split_columns(g)
    a, b = g.shape
    return g.reshape(N_DEV, a // N_DEV, b)


def _join_columns(gathered):
    _, r, b = gathered.shape
    tr = _pick_tile(r, (128,))

    def body(g_ref, o_ref):
        o_ref[...] = jnp.concatenate([g_ref[j].astype(F32) for j in range(N_DEV)], axis=1).astype(o_ref.dtype)

    return _call("join_columns", body, grid=(r // tr,), ins=[gathered],
                 in_specs=[pl.BlockSpec((N_DEV, tr, b), lambda i: (0, i, 0))],
                 out_specs=[pl.BlockSpec((tr, N_DEV * b), lambda i: (i, 0))],
                 out_shape=[jax.ShapeDtypeStruct((r, N_DEV * b), gathered.dtype)])[0][0]


def _split_columns(full):
    r, nb = full.shape
    b = nb // N_DEV
    tr = _pick_tile(r, (128,))

    def body(f_ref, o_ref):
        w = f_ref[...].astype(F32)
        for j in range(N_DEV):
            o_ref[j] = w[:, j * b:(j + 1) * b].astype(o_ref.dtype)

    return _call("split_columns", body, grid=(r // tr,), ins=[full],
                 in_specs=[pl.BlockSpec((tr, nb), lambda i: (i, 0))],
                 out_specs=[pl.BlockSpec((N_DEV, tr, b), lambda i: (0, i, 0))],
                 out_shape=[jax.ShapeDtypeStruct((N_DEV, r, b), full.dtype)])[0][0]


class _MeshExchange:
    def __init__(self, shards_bf16, first):
        self.shards = shards_bf16
        self.full = dict(first)
        self.pieces = {}
        self.grads = {}
        self.to_send = {}
        self.received = {}
        self.small = None
        self.small_gathered = None

    def weight(self, name, l):
        if (name, l) not in self.full:
            got = jnp.concatenate([self.pieces[(name, l, q)] for q in range(W_IN_PIECES)], axis=1)
            self.full[(name, l)] = _from_shards(name, got)
        return self.full[(name, l)]

    def grad(self, name, l, g):
        self.grads[(name, l)] = g

    def small_grads(self, done):
        self.small = done

    def partial_sums(self, name, l):
        if (name, l, None) in self.received:
            return self.received[(name, l, None)]
        return jnp.concatenate([self.received[(name, l, q)] for q in range(W_IN_PIECES)], axis=1)

    def _slices(self, name, l, piece):
        n_rows = D_MODEL // W_IN_PIECES
        key, rows = (name, l), None if piece is None else (piece * n_rows, n_rows)
        if key not in self.to_send:
            self.to_send[key] = _to_shards(name, self.grads[key])
        return self.to_send[key], rows

    def before(self, call, l):
        comm = _Comm()
        for name, layer, piece in GATHER_PLAN.get((call, l), ()):
            n_rows = D_MODEL // W_IN_PIECES
            comm.gathers.append((self.shards[name], layer, None if piece is None else (piece * n_rows, n_rows)))
        if (call, l) == SMALL_GATHER_CALL:
            names = SMALL_REP + SMALL_SH
            rows = _small_rows(sum(math.prod(self.small[n].shape) for n in names))
            comm.gathers.append((_pack_flat([self.small[n] for n in names], rows), None, None))
        for name, layer, piece in SCATTER_PLAN.get((call, l), ()):
            comm.scatters.append(self._slices(name, layer, piece))
        return comm if comm.gathers or comm.scatters else None

    def after(self, call, l, outs):
        gathers = list(GATHER_PLAN.get((call, l), ()))
        for (name, layer, piece), out in zip(gathers, outs):
            if piece is None:
                self.full[(name, layer)] = _from_shards(name, out)
            else:
                self.pieces[(name, layer, piece)] = out
        outs = outs[len(gathers):]
        if (call, l) == SMALL_GATHER_CALL:
            self.small_gathered = outs[0]
            outs = outs[1:]
        for item, out in zip(SCATTER_PLAN.get((call, l), ()), outs):
            self.received[item] = out


def kernel(x, mem, mem_ln_g, mem_ln_b, w_in, sg_ln_g, sg_ln_b, sg_w, sg_b, conv_w, conv_b, dt_bias, a_log, d_skip, ssm_norm_g, p_a, p_b, w_mix_o, w_xq, w_xkv, w_xo, w_ffn_in, w_ffn_out, ln_g, ln_b, loss_target, m_mem_ln_g, m_mem_ln_b, m_w_in, m_sg_ln_g, m_sg_ln_b, m_sg_w, m_sg_b, m_conv_w, m_conv_b, m_dt_bias, m_a_log, m_d_skip, m_ssm_norm_g, m_p_a, m_p_b, m_w_mix_o, m_w_xq, m_w_xkv, m_w_xo, m_w_ffn_in, m_w_ffn_out, m_ln_g, m_ln_b, v_mem_ln_g, v_mem_ln_b, v_w_in, v_sg_ln_g, v_sg_ln_b, v_sg_w, v_sg_b, v_conv_w, v_conv_b, v_dt_bias, v_a_log, v_d_skip, v_ssm_norm_g, v_p_a, v_p_b, v_w_mix_o, v_w_xq, v_w_xkv, v_w_xo, v_w_ffn_in, v_w_ffn_out, v_ln_g, v_ln_b):
    args = dict(locals())
    w = {n: args[n] for n in WEIGHTS}
    m = {n: args["m_" + n] for n in WEIGHTS}
    v = {n: args["v_" + n] for n in WEIGHTS}
    me = 4 * lax.axis_index("x") + 2 * lax.axis_index("y") + lax.axis_index("c")

    shards = {n: w[n].astype(BF16) for n in BIG}
    sh_shapes = [w[n].shape for n in SMALL_SH]
    first = _Comm()
    first.gathers.append((shards["w_in"], 0, None))
    first.gathers.append((_pack_flat([w[n] for n in SMALL_SH], _small_rows(sum(math.prod(s) for s in sh_shapes))), None,
                          None))
    w_in0, small_sh = _comm_only("gather_first", first)
    small = {n: w[n] for n in SMALL_REP}
    for n, sh in zip(SMALL_SH, _unpack_flat(small_sh, sh_shapes)):
        small[n] = sh.transpose(1, 2, 0, 3).reshape(sh.shape[1], sh.shape[2], N_DEV * sh.shape[3])

    ex = _MeshExchange(shards, {("w_in", 0): _from_shards("w_in", w_in0)})
    loss, grad_x = _run_step(x, mem, loss_target, small, ex)
    loss = lax.psum(loss[0, 0], ("x", "y", "c"))

    out = {}
    for n in BIG[1:] + BIG[:1]:
        comm = ex.before("adamw_" + n, 0)
        out[n], comm_outs = _adamw_sharded("adamw_" + n, [ex.partial_sums(n, l) for l in range(DEPTH)], w[n], m[n], v[n],
                                           comm=comm)
        if comm is not None:
            ex.after("adamw_" + n, 0, comm_outs)
    names = SMALL_REP + SMALL_SH
    g_small = dict(zip(names, _unpack_flat(_sum_parts("sum_small_grads", ex.small_gathered),
                                           [ex.small[n].shape for n in names])))
    for n in names:
        g = g_small[n]
        if n in SMALL_SH:
            width = w[n].shape[-1]
            g = lax.dynamic_slice_in_dim(g, me * width, width, axis=-1)
        two_d = (-1, w[n].shape[-1])
        res = _adamw_small("adamw_" + n, g.reshape(two_d), w[n].reshape(two_d), m[n].reshape(two_d), v[n].reshape(two_d))
        out[n] = [g] + [r.reshape(w[n].shape) for r in res]

    results = []
    for k in range(4):
        results.extend(out[n][k] for n in WEIGHTS)
    return (loss, grad_x, *results)
```

```python
import functools
import math

import jax
import jax.numpy as jnp
from jax import lax
from jax.experimental import pallas as pl
from jax.experimental.pallas import tpu as pltpu

F32 = jnp.float32
BF16 = jnp.bfloat16
HIGHEST = lax.Precision.HIGHEST

N_DEV = 8
D_MODEL = 1024
DEPTH = 2
MEM_LEN = 256
CHUNK = 128
SG_GROUPS = 8
SSM_INNER = 2048
SSM_HEADDIM = 64
SSM_HEADS = 32
SSM_STATE = 128
SSM_GROUPS = 4
SSM_RPG = 8
SSM_CONV = 4
SSM_CONV_DIM = 3072
X_HEADS = 4
X_HEADDIM = 256
FFN_HIDDEN = 2816
ALPHA = float((2 * DEPTH) ** 0.25)
LN_EPS = 1e-5
RMS_EPS = 1e-5
XBC_COL = 4096
DT_COL = 7168
GA_COL = 7200
IN_COLS = 9248
P_GATE = 4096
P_XBC = 6144
P_COLS = 9216
DT_LANES = 128

ADAM_LR = 0.001
ADAM_B1 = 0.9
ADAM_B2 = 0.999
ADAM_EPS = 1e-08
ADAM_WD = 0.01
ADAM_STEP = 10

VMEM_LIMIT = 48 * 1024 * 1024
SMALL_ROW_TILE = 256

BIG = ("w_in", "p_a", "p_b", "w_mix_o", "w_xq", "w_xkv", "w_xo", "w_ffn_in", "w_ffn_out")
BIG_COL_SHARDED = ("w_in", "w_xkv", "w_ffn_in")
SMALL_REP = ("mem_ln_g", "mem_ln_b", "sg_ln_g", "sg_ln_b", "sg_w", "sg_b", "conv_b", "dt_bias", "a_log", "d_skip",
             "ssm_norm_g")
SMALL_SH = ("conv_w", "ln_g", "ln_b")
WEIGHTS = ("mem_ln_g", "mem_ln_b", "w_in", "sg_ln_g", "sg_ln_b", "sg_w", "sg_b", "conv_w", "conv_b", "dt_bias", "a_log",
           "d_skip", "ssm_norm_g", "p_a", "p_b", "w_mix_o", "w_xq", "w_xkv", "w_xo", "w_ffn_in", "w_ffn_out", "ln_g", "ln_b")

W_IN_PIECES = 4
GATHER_PLAN = {("sgu_fwd", 0): [("w_in", 1, 0)], ("conv_fwd", 0): [("w_in", 1, 1)],
               ("mm_ffn_in", 0): [("w_in", 1, 2)], ("mm_ffn_out_ln", 0): [("w_in", 1, 3)]}
SCATTER_PLAN = {("mm_ffn_in_dw", 0): [("w_in", 1, 2)], ("mm_ffn_in_dx", 0): [("w_in", 1, 3)],
                ("adamw_w_ffn_in", 0): [("w_in", 0, 3)]}
for _l in range(DEPTH):
    GATHER_PLAN[("mm_in", _l)] = [(n, _l, None) for n in ("p_a", "p_b", "w_mix_o", "w_xq", "w_xkv", "w_xo")]
    GATHER_PLAN[("ssd_fwd", _l)] = [("w_ffn_in", _l, None), ("w_ffn_out", _l, None)]
    SCATTER_PLAN[("swiglu_bwd", _l)] = [("w_ffn_out", _l, None)]
    SCATTER_PLAN[("sgu_bwd", _l)] = [("w_mix_o", _l, None), ("p_a", _l, None), ("w_xo", _l, None)]
    SCATTER_PLAN[("ssd_bwd", _l)] = [("w_ffn_in", _l, None), ("w_xkv", _l, None), ("w_xq", _l, None)]
    SCATTER_PLAN[("conv_bwd", _l)] = [("p_b", _l, None)]
    SCATTER_PLAN[("mm_in_dx", _l)] = [("w_in", _l, 0), ("w_in", _l, 1)] + ([("w_in", _l, 2)] if _l == 0 else [])
SMALL_GATHER_CALL = ("mm_in_dw", 0)


def _layer_norm(x, g, b):
    mu = jnp.mean(x, axis=-1, keepdims=True)
    xc = x - mu
    var = jnp.mean(xc * xc, axis=-1, keepdims=True)
    return xc * lax.rsqrt(var + LN_EPS) * g + b


def _gelu(x):
    return 0.5 * x * (1.0 + lax.erf(x * (1.0 / math.sqrt(2.0))))


def _sigmoid(x):
    return 0.5 * jnp.tanh(0.5 * x) + 0.5


def _silu(x):
    return x * _sigmoid(x)


def _softplus(x):
    return jnp.maximum(x, 0.0) + jnp.log1p(jnp.exp(-jnp.abs(x)))


def _causal_mask():
    r = lax.broadcasted_iota(jnp.int32, (CHUNK, CHUNK), 0)
    c = lax.broadcasted_iota(jnp.int32, (CHUNK, CHUNK), 1)
    return r >= c


def _sgu_block(uv, ln_g, ln_b, w, sb):
    gu = _gelu(uv[:, :D_MODEL])
    vn = _layer_norm(_gelu(uv[:, D_MODEL:]), ln_g, ln_b)
    causal = _causal_mask()
    width = D_MODEL // SG_GROUPS
    outs = []
    for g in range(SG_GROUPS):
        wg = jnp.where(causal, w[g], 0.0).astype(BF16)
        mixed = jnp.dot(wg, vn[:, g * width:(g + 1) * width].astype(BF16), preferred_element_type=F32)
        outs.append(mixed + sb[:, g:g + 1])
    return (gu * jnp.concatenate(outs, axis=1),)


GROUP_W = SSM_RPG * SSM_HEADDIM
NT_DIMS = (((1,), (1,)), ((), ()))
TN_DIMS = (((0,), (0,)), ((), ()))


def _mxu(a, b, dims=(((1,), (0,)), ((), ()))):
    return lax.dot_general(a.astype(BF16), b.astype(BF16), dims, preferred_element_type=F32)


def _head_expander():
    return (jnp.arange(SSM_INNER)[None, :] // SSM_HEADDIM == jnp.arange(128)[:, None]).astype(BF16)


def _bf16_terms(x, n):
    terms = []
    for _ in range(n):
        t = x.astype(BF16)
        terms.append(t)
        x = x - t.astype(F32)
    return terms


def _expand_heads(q, e):
    return sum(jnp.dot(t, e, preferred_element_type=F32) for t in _bf16_terms(q, 2))


def _reduce_heads(v, e):
    return sum(lax.dot_general(t, e, NT_DIMS, preferred_element_type=F32) for t in _bf16_terms(v, 2))


def _reduce_heads_of_column_sums(v, e):
    sums = jnp.broadcast_to(jnp.sum(v, axis=0, keepdims=True), (8, v.shape[1]))
    return _reduce_heads(sums, e)[0:1, :]


def _ssd_common(xc, dtraw, dt_bias, a_log, e):
    xs = xc[:, :SSM_INNER]
    pre = dtraw + dt_bias
    dt = _softplus(pre)
    a = -jnp.exp(a_log)
    r_i = lax.broadcasted_iota(jnp.int32, (CHUNK, CHUNK), 0)
    c_i = lax.broadcasted_iota(jnp.int32, (CHUNK, CHUNK), 1)
    tril = jnp.where(r_i >= c_i, 1.0, 0.0).astype(F32)
    cs = jnp.dot(tril, dt * a, precision=HIGHEST, preferred_element_type=F32)
    cs_last = cs[CHUNK - 1:CHUNK, :]
    decay_in = jnp.exp(cs)
    decay_st = jnp.exp(cs_last - cs)
    dt_x = _expand_heads(dt, e)
    w_st_x = _expand_heads(dt * decay_st, e)
    decay_in_x = _expand_heads(decay_in, e)
    return dict(xs=xs, pre=pre, dt=dt, a=a, lower=r_i >= c_i, upper=c_i >= r_i, cs=cs, cs_t=cs.T, decay_in=decay_in,
                decay_st=decay_st, chunk_decay=jnp.exp(cs_last), dt_x=dt_x, w_st_x=w_st_x, decay_in_x=decay_in_x,
                chunk_decay_x=decay_in_x[CHUNK - 1:CHUNK, :], xdt=xs * dt_x, x_st=(xs * w_st_x).astype(BF16),
                low=lax.broadcasted_iota(jnp.int32, (CHUNK, 128), 1) < SSM_HEADDIM)


def _pair_decay(c, h):
    return jnp.exp(jnp.where(c["lower"], c["cs"][:, h:h + 1] - c["cs_t"][h:h + 1, :], -1e30))


def _pair_decay_t(c, h):
    return jnp.exp(jnp.where(c["upper"], c["cs_t"][h:h + 1, :] - c["cs"][:, h:h + 1], -1e30))


def _ssd_forward(xc, dtraw, z, prev, dt_bias, a_log, d_skip_x, norm_g, e):
    c = _ssd_common(xc, dtraw, dt_bias, a_log, e)
    y_groups, new_states = [], []
    for g in range(SSM_GROUPS):
        lanes = slice(g * GROUP_W, (g + 1) * GROUP_W)
        bg = xc[:, SSM_INNER + g * SSM_STATE:SSM_INNER + (g + 1) * SSM_STATE]
        cg = xc[:, SSM_INNER + (SSM_GROUPS + g) * SSM_STATE:SSM_INNER + (SSM_GROUPS + g + 1) * SSM_STATE].astype(BF16)
        pg = prev[g * SSM_STATE:(g + 1) * SSM_STATE, :]
        cb = _mxu(cg, bg, NT_DIMS)
        y_in = _mxu(cg, pg) * c["decay_in_x"][:, lanes]
        new_states.append(pg * c["chunk_decay_x"][:, lanes] + _mxu(bg.T, c["x_st"][:, lanes]))
        pairs = []
        for j in range(SSM_RPG // 2):
            h0 = g * SSM_RPG + 2 * j
            xp = c["xdt"][:, 128 * (h0 // 2):128 * (h0 // 2 + 1)]
            pairs.append(_mxu(cb * _pair_decay(c, h0), jnp.where(c["low"], xp, 0.0))
                         + _mxu(cb * _pair_decay(c, h0 + 1), jnp.where(c["low"], 0.0, xp)))
        y_groups.append(jnp.concatenate(pairs, axis=1) + y_in)
    y_pre = jnp.concatenate(y_groups, axis=1) + c["xs"] * d_skip_x
    gated = y_pre * _silu(z)
    normed = [gated[:, g * GROUP_W:(g + 1) * GROUP_W] for g in range(SSM_GROUPS)]
    normed = [yg * lax.rsqrt(jnp.mean(yg * yg, axis=-1, keepdims=True) + RMS_EPS) for yg in normed]
    return jnp.concatenate(normed, axis=1) * norm_g, y_pre, jnp.concatenate(new_states, axis=0)


def _ssd_backward(xc, dtraw, z, prev, y_pre, dt_bias, a_log, d_skip_x, norm_g, e, dout, dnew):
    c = _ssd_common(xc, dtraw, dt_bias, a_log, e)
    xs = c["xs"]
    sig = _sigmoid(z)
    silu_z = z * sig
    gated = y_pre * silu_z
    d_gated, normed = [], []
    for g in range(SSM_GROUPS):
        lanes = slice(g * GROUP_W, (g + 1) * GROUP_W)
        yg = gated[:, lanes]
        r = lax.rsqrt(jnp.mean(yg * yg, axis=-1, keepdims=True) + RMS_EPS)
        n = yg * r
        gh = dout[:, lanes] * norm_g[:, lanes]
        d_gated.append(r * (gh - n * jnp.mean(gh * n, axis=-1, keepdims=True)))
        normed.append(n)
    d_gated = jnp.concatenate(d_gated, axis=1)
    dnorm_g = jnp.sum(dout * jnp.concatenate(normed, axis=1), axis=0, keepdims=True)
    dy = d_gated * silu_z
    dz = d_gated * y_pre * (sig * (1.0 + z * (1.0 - sig)))
    dxs = dy * d_skip_x
    dd_skip = _reduce_heads_of_column_sums(dy * xs, e)

    lane = lax.broadcasted_iota(jnp.int32, (CHUNK, 128), 1)
    sub = lax.broadcasted_iota(jnp.int32, (8, 128), 0)
    dcs_neg = jnp.zeros((CHUNK, 128), F32)
    row_slabs = []
    dxdt, dx_st, d_decay_in_x, dprev, d_chunk_decay_x, db_all, dc_all = [], [], [], [], [], [], []
    for g in range(SSM_GROUPS):
        lanes = slice(g * GROUP_W, (g + 1) * GROUP_W)
        bg = xc[:, SSM_INNER + g * SSM_STATE:SSM_INNER + (g + 1) * SSM_STATE].astype(BF16)
        cg_f = xc[:, SSM_INNER + (SSM_GROUPS + g) * SSM_STATE:SSM_INNER + (SSM_GROUPS + g + 1) * SSM_STATE]
        cg = cg_f.astype(BF16)
        pg = prev[g * SSM_STATE:(g + 1) * SSM_STATE, :]
        dng = dnew[g * SSM_STATE:(g + 1) * SSM_STATE, :]
        dy_g = dy[:, lanes]
        cb_t = _mxu(bg, cg, NT_DIMS)
        t1 = (dy_g * c["decay_in_x"][:, lanes]).astype(BF16)
        d_decay_in_x.append(dy_g * _mxu(cg, pg))
        dc = _mxu(t1, pg, NT_DIMS)
        dprev.append(_mxu(cg_f.T, t1) + dng * c["chunk_decay_x"][:, lanes])
        d_chunk_decay_x.append(dng * pg)
        db = _mxu(c["x_st"][:, lanes], dng, NT_DIMS)
        dx_st.append(_mxu(bg, dng))
        dcb_t = jnp.zeros((CHUNK, CHUNK), F32)
        rows = []
        for j in range(SSM_RPG // 2):
            h0 = g * SSM_RPG + 2 * j
            blk = slice(128 * (h0 // 2), 128 * (h0 // 2 + 1))
            xp = c["xdt"][:, blk]
            dyp = dy[:, blk].astype(BF16)
            pair_dx = []
            for k, xk in enumerate((jnp.where(c["low"], xp, 0.0), jnp.where(c["low"], 0.0, xp))):
                dec_t = _pair_decay_t(c, h0 + k)
                pair_dx.append(_mxu(cb_t * dec_t, dyp))
                dml_t = _mxu(xk, dyp, NT_DIMS) * dec_t
                dcb_t = dcb_t + dml_t
                dseg_t = dml_t * cb_t
                dcs_neg = dcs_neg + jnp.where(lane == h0 + k, jnp.sum(dseg_t, axis=-1, keepdims=True), 0.0)
                rows.append(jnp.sum(dseg_t, axis=0, keepdims=True))
            dxdt.append(jnp.where(c["low"], pair_dx[0], pair_dx[1]))
        slab = jnp.zeros((8, 128), F32)
        for r in range(SSM_RPG):
            slab = slab + jnp.where(sub == r, rows[r], 0.0)
        row_slabs.append(slab)
        dc_all.append(dc + _mxu(dcb_t.T, bg))
        db_all.append(db + _mxu(dcb_t, cg))
    dxdt = jnp.concatenate(dxdt, axis=1)
    dx_st = jnp.concatenate(dx_st, axis=1)
    by_head = jnp.concatenate(row_slabs + [jnp.zeros((CHUNK - SSM_HEADS, 128), F32)], axis=0)
    dcs = by_head.T - dcs_neg
    dxs = dxs + dxdt * c["dt_x"] + dx_st * c["w_st_x"]
    ddt = _reduce_heads(dxdt * xs, e)
    dw_st = _reduce_heads(dx_st * xs, e)
    dcs = dcs + _reduce_heads(jnp.concatenate(d_decay_in_x, axis=1), e) * c["decay_in"]
    ddt = ddt + dw_st * c["decay_st"]
    d_log_st = dw_st * c["dt"] * c["decay_st"]
    dcs = dcs - d_log_st
    d_chunk_decay = _reduce_heads_of_column_sums(jnp.concatenate(d_chunk_decay_x, axis=1), e)
    dcs_last = jnp.sum(d_log_st, axis=0, keepdims=True) + d_chunk_decay * c["chunk_decay"]
    row = lax.broadcasted_iota(jnp.int32, (CHUNK, 128), 0)
    dcs = dcs + jnp.where(row == CHUNK - 1, dcs_last, 0.0)
    triu = jnp.where(c["upper"], 1.0, 0.0).astype(F32)
    dda = jnp.dot(triu, dcs, precision=HIGHEST, preferred_element_type=F32)
    ddt = ddt + dda * c["a"]
    da_log = jnp.sum(dda * c["dt"], axis=0, keepdims=True) * c["a"]
    dpre = ddt * _sigmoid(c["pre"])
    dxc = jnp.concatenate([dxs] + db_all + dc_all, axis=1)
    return (dxc, dpre, dz, jnp.concatenate(dprev, axis=0), jnp.sum(dpre, axis=0, keepdims=True), da_log, dd_skip,
            dnorm_g)


def _conv_block(x, w, b):
    rows = lax.broadcasted_iota(jnp.int32, x.shape, 0)
    acc = x * w[SSM_CONV - 1:SSM_CONV, :] + b
    for k in range(SSM_CONV - 1):
        shift = SSM_CONV - 1 - k
        acc = acc + _shift_rows(x, rows, shift) * w[k:k + 1, :]
    return (_silu(acc),)


@functools.partial(jax.custom_vjp, nondiff_argnums=(2,))
def _shift_rows(x, rows, shift):
    return jnp.where(rows >= shift, pltpu.roll(x, shift, 0), 0.0)


def _shift_rows_fwd(x, rows, shift):
    return _shift_rows(x, rows, shift), rows


def _shift_rows_bwd(shift, rows, g):
    n = g.shape[0]
    return jnp.where(rows < n - shift, pltpu.roll(g, n - shift, 0), 0.0), None


_shift_rows.defvjp(_shift_rows_fwd, _shift_rows_bwd)


def _merge_block(gates, br_a, br_b):
    return (_sigmoid(gates[:, :D_MODEL]) * br_a + _sigmoid(gates[:, D_MODEL:]) * br_b,)


def _lnres_block(x, y, g, b):
    return (_layer_norm(ALPHA * x + y, g, b),)


def _memln_block(x, g, b):
    return (_layer_norm(x, g, b),)


def _attn_block(q, kv):
    outs = []
    for h in range(X_HEADS):
        qh = q[:, h * X_HEADDIM:(h + 1) * X_HEADDIM].astype(BF16)
        kh = kv[:, h * X_HEADDIM:(h + 1) * X_HEADDIM].astype(BF16)
        vh = kv[:, D_MODEL + h * X_HEADDIM:D_MODEL + (h + 1) * X_HEADDIM].astype(BF16)
        s = lax.dot_general(qh, kh, (((1,), (1,)), ((), ())), preferred_element_type=F32) * (X_HEADDIM ** -0.5)
        s = s - lax.stop_gradient(jnp.max(s, axis=-1, keepdims=True))
        e = jnp.exp(s)
        p = e / jnp.sum(e, axis=-1, keepdims=True)
        outs.append(jnp.dot(p.astype(BF16), vh, preferred_element_type=F32))
    return (jnp.concatenate(outs, axis=1),)


def _swiglu_block(gu):
    return (_silu(gu[:, :FFN_HIDDEN]) * gu[:, FFN_HIDDEN:],)


class _Comm:
    def __init__(self):
        self.gathers = []
        self.scatters = []

    @staticmethod
    def _rows(ref, rows):
        return ref if rows is None else ref.at[pl.ds(rows[0], rows[1])]

    def operands(self):
        ins = [a for a, _, _ in self.gathers] + [a for a, _ in self.scatters]
        shapes = []
        for a, idx, rows in self.gathers:
            blk = a.shape if idx is None else a.shape[1:]
            shapes.append(jax.ShapeDtypeStruct((N_DEV, blk[0] if rows is None else rows[1]) + tuple(blk[1:]), a.dtype))
        for a, rows in self.scatters:
            shapes.append(jax.ShapeDtypeStruct((N_DEV, a.shape[1] if rows is None else rows[1]) + tuple(a.shape[2:]),
                                               a.dtype))
        scratch = []
        for n in (len(self.gathers), len(self.scatters)):
            if n:
                scratch += [pltpu.SemaphoreType.DMA((7 * n,)), pltpu.SemaphoreType.DMA((7 * n,)),
                            pltpu.SemaphoreType.DMA((n,))]
        return ins, shapes, scratch

    def _split(self, in_refs, out_refs, sems):
        ng = len(self.gathers)
        g_sems = sems[:3] if ng else None
        s_sems = sems[3:] if ng else sems
        return in_refs[:ng], in_refs[ng:], out_refs[:ng], out_refs[ng:], g_sems, s_sems

    def _gather_copies(self, i, src_ref, out_ref, sems):
        send_sems, recv_sems, local_sems = sems
        x, y, c = lax.axis_index("x"), lax.axis_index("y"), lax.axis_index("c")
        me, sibling = (x, y, c), (x, y, 1 - c)
        chips = [(1 - x, y), (x, 1 - y), (1 - x, 1 - y)]
        _, idx, rows = self.gathers[i]
        src = self._rows(src_ref if idx is None else src_ref.at[idx], rows)

        def slot(px, py, pc):
            return out_ref.at[4 * px + 2 * py + pc]

        def copy(k, blk, to, from_src=False):
            return pltpu.make_async_remote_copy(
                src_ref=src if from_src else slot(*blk), dst_ref=slot(*blk), send_sem=send_sems.at[7 * i + k],
                recv_sem=recv_sems.at[7 * i + k], device_id=to, device_id_type=pl.DeviceIdType.MESH)

        mine = pltpu.make_async_copy(src, slot(*me), local_sems.at[i])
        first = [copy(0, me, sibling, True)] + [copy(1 + j, me, (*chip, c), True) for j, chip in enumerate(chips)]
        passed = [copy(4 + j, (*chip, c), sibling) for j, chip in enumerate(chips)]
        arrivals = [copy(1 + j, (*chip, c), me) for j, chip in enumerate(chips)]
        from_sibling = [copy(0, sibling, me)] + [copy(4 + j, (*chip, 1 - c), me) for j, chip in enumerate(chips)]
        return mine, first, passed, arrivals, from_sibling

    def _scatter_copies(self, i, src_ref, out_ref, sems):
        send_sems, recv_sems, local_sems = sems
        x, y, c = lax.axis_index("x"), lax.axis_index("y"), lax.axis_index("c")
        me = 4 * x + 2 * y + c
        rows = self.scatters[i][1]
        mine = pltpu.make_async_copy(self._rows(src_ref.at[me], rows), out_ref.at[me], local_sems.at[i])
        copies = []
        for k in range(1, N_DEV):
            px = 1 - x if k & 4 else x
            py = 1 - y if k & 2 else y
            pc = 1 - c if k & 1 else c
            copies.append(pltpu.make_async_remote_copy(
                src_ref=self._rows(src_ref.at[4 * px + 2 * py + pc], rows), dst_ref=out_ref.at[me],
                send_sem=send_sems.at[7 * i + k - 1], recv_sem=recv_sems.at[7 * i + k - 1], device_id=(px, py, pc),
                device_id_type=pl.DeviceIdType.MESH))
        return mine, copies

    def start(self, in_refs, out_refs, sems):
        g_in, s_in, g_out, s_out, g_sems, s_sems = self._split(in_refs, out_refs, sems)
        for i in range(len(self.gathers)):
            mine, first, _, _, _ = self._gather_copies(i, g_in[i], g_out[i], g_sems)
            mine.start()
            for cp in first:
                cp.start()
        for i in range(len(self.scatters)):
            mine, copies = self._scatter_copies(i, s_in[i], s_out[i], s_sems)
            mine.start()
            for cp in copies:
                cp.start()

    def finish(self, in_refs, out_refs, sems):
        g_in, s_in, g_out, s_out, g_sems, s_sems = self._split(in_refs, out_refs, sems)
        parts = [self._gather_copies(i, g_in[i], g_out[i], g_sems) for i in range(len(self.gathers))]
        for j in range(3):
            for _, _, passed, arrivals, _ in parts:
                arrivals[j].wait_recv()
                passed[j].start()
        for mine, first, passed, _, from_sibling in parts:
            for cp in from_sibling:
                cp.wait_recv()
            for cp in first + passed:
                cp.wait_send()
            mine.wait()
        for i in range(len(self.scatters)):
            mine, copies = self._scatter_copies(i, s_in[i], s_out[i], s_sems)
            for cp in copies:
                cp.wait_recv()
            for cp in copies:
                cp.wait_send()
            mine.wait()


def _params(grid):
    return pltpu.CompilerParams(dimension_semantics=("arbitrary",) * len(grid), vmem_limit_bytes=VMEM_LIMIT)


def _call(name, body, *, grid, ins, in_specs, out_shape, out_specs, scratch=(), comm=None, aliases=None):
    n_in, n_out, n_scr = len(ins), len(out_shape), len(scratch)
    aliases = aliases or {}
    if comm is None:
        outs = pl.pallas_call(body, grid=grid, in_specs=list(in_specs), out_specs=list(out_specs),
                              out_shape=list(out_shape), scratch_shapes=list(scratch), name=name,
                              input_output_aliases=aliases, compiler_params=_params(grid))(*ins)
        return list(outs), []
    c_ins, c_shapes, c_scratch = comm.operands()
    nci, nco = len(c_ins), len(c_shapes)
    anywhere = pl.BlockSpec(memory_space=pl.ANY)

    def carrier(*refs):
        main_in, comm_in = refs[:n_in], refs[n_in:n_in + nci]
        o0 = n_in + nci
        main_out, comm_out = refs[o0:o0 + n_out], refs[o0 + n_out:o0 + n_out + nco]
        s0 = o0 + n_out + nco
        main_scr, comm_scr = refs[s0:s0 + n_scr], refs[s0 + n_scr:]
        first = pl.program_id(0) == 0
        last = pl.program_id(0) == grid[0] - 1
        for ax in range(1, len(grid)):
            first = first & (pl.program_id(ax) == 0)
            last = last & (pl.program_id(ax) == grid[ax] - 1)

        @pl.when(first)
        def _():
            comm.start(comm_in, comm_out, comm_scr)

        body(*main_in, *main_out, *main_scr)

        @pl.when(last)
        def _():
            comm.finish(comm_in, comm_out, comm_scr)

    outs = pl.pallas_call(carrier, grid=grid, in_specs=list(in_specs) + [anywhere] * nci,
                          out_specs=list(out_specs) + [anywhere] * nco, out_shape=list(out_shape) + c_shapes,
                          scratch_shapes=list(scratch) + c_scratch, name=name, input_output_aliases=aliases,
                          compiler_params=_params(grid))(*ins, *c_ins)
    return list(outs[:n_out]), list(outs[n_out:])


def _comm_only(name, comm):
    c_ins, c_shapes, c_scratch = comm.operands()
    nci, nco = len(c_ins), len(c_shapes)
    anywhere = pl.BlockSpec(memory_space=pl.ANY)

    def body(*refs):
        comm.start(refs[:nci], refs[nci:nci + nco], refs[nci + nco:])
        comm.finish(refs[:nci], refs[nci:nci + nco], refs[nci + nco:])

    return list(pl.pallas_call(body, in_specs=[anywhere] * nci, out_specs=[anywhere] * nco, out_shape=c_shapes,
                               scratch_shapes=c_scratch, name=name)(*c_ins))


def _stage_fwd(name, f, grid, ins, in_specs, out_shapes, out_specs, comm=None):
    n_in = len(ins)

    def body(*refs):
        res = f(*[r[...].astype(F32) for r in refs[:n_in]])
        for o_ref, val in zip(refs[n_in:], res):
            o_ref[...] = val.astype(o_ref.dtype)

    return _call(name, body, grid=grid, ins=ins, in_specs=in_specs, out_shape=out_shapes, out_specs=out_specs, comm=comm)


def _stage_bwd(name, f, grid, ins, in_specs, cts, ct_specs, grads, comm=None, ct_product=None):
    n_in = len(ins)
    flat_cts = [c for group in cts for c in group]
    flat_ct_specs = [s for group, spec in zip(cts, ct_specs) for s in (spec,) * len(group)]
    if ct_product is not None:
        assert not cts
        flat_cts = [ct_product[0], ct_product[2]]
        flat_ct_specs = [ct_product[1], pl.BlockSpec(ct_product[2].shape, lambda *_: (0, 0))]
    n_ct = len(flat_cts)
    diff = [g[0] for g in grads]
    buffers = [(k, g[4]) for k, g in enumerate(grads) if len(g) > 4]
    n_buf = len(buffers)

    def body(*refs):
        vals = [r[...].astype(F32) for r in refs[:n_in]]
        ct_refs = refs[n_in:n_in + n_ct]
        g_refs = refs[n_in + n_ct + n_buf:]
        ct_vals, pos = [], 0
        if ct_product is not None:
            ct_vals.append(lax.dot_general(ct_refs[0][...].astype(BF16), ct_refs[1][...].astype(BF16), NT_DIMS,
                                           preferred_element_type=F32))
        for group in cts:
            acc = ct_refs[pos][...].astype(F32)
            for j in range(1, len(group)):
                acc = acc + ct_refs[pos + j][...].astype(F32)
            ct_vals.append(acc)
            pos += len(group)

        def g_fn(*dvals):
            full = list(vals)
            for i, dv in zip(diff, dvals):
                full[i] = dv
            return f(*full)

        _, vjp = jax.vjp(g_fn, *[vals[i] for i in diff])
        gvals = vjp(tuple(ct_vals))
        for gspec, g_ref, gval in zip(grads, g_refs, gvals):
            acc_axes = gspec[1]
            if not acc_axes:
                g_ref[...] = gval.astype(g_ref.dtype)
            else:
                first = pl.program_id(acc_axes[0]) == 0
                for ax in acc_axes[1:]:
                    first = first & (pl.program_id(ax) == 0)

                @pl.when(first)
                def _():
                    g_ref[...] = jnp.zeros_like(g_ref)

                g_ref[...] += gval.astype(g_ref.dtype)

    out_shapes, out_specs = [], []
    for gspec in grads:
        shape, spec = gspec[3] if len(gspec) > 3 else (ins[gspec[0]].shape, in_specs[gspec[0]])
        out_shapes.append(jax.ShapeDtypeStruct(shape, gspec[2]))
        out_specs.append(spec)
    anywhere = pl.BlockSpec(memory_space=pl.ANY)
    return _call(name, body, grid=grid, ins=list(ins) + flat_cts + [b for _, b in buffers],
                 in_specs=list(in_specs) + flat_ct_specs + [anywhere] * n_buf, out_shape=out_shapes, out_specs=out_specs,
                 comm=comm, aliases={n_in + n_ct + j: k for j, (k, _) in enumerate(buffers)})


def _pick_tile(n, candidates):
    for c in candidates:
        if n % c == 0:
            return c
    return n


def _matmul(name, a, b, *, ta=False, tb=False, add=None, out_dtype=F32, comm=None):
    if ta:
        k_dim, m = a.shape
    else:
        m, k_dim = a.shape
    n = b.shape[0] if tb else b.shape[1]
    assert (b.shape[1] if tb else b.shape[0]) == k_dim and not (ta and tb)
    tm = _pick_tile(m, (1024, 1408, 512, 256, 128))
    tn = _pick_tile(n, (1024, 1408, 512, 256, 128))
    if ta:
        tk = _pick_tile(k_dim, (1024, 512, 256, 128))
    elif k_dim <= 2816:
        tk = k_dim
    else:
        tk = _pick_tile(k_dim, (1408, 1024, 512, 256, 128))
    nk = k_dim // tk
    grid = (m // tm, n // tn, nk)
    a_spec = pl.BlockSpec((tk, tm), lambda i, j, k: (k, i)) if ta else pl.BlockSpec((tm, tk), lambda i, j, k: (i, k))
    b_spec = pl.BlockSpec((tn, tk), lambda i, j, k: (j, k)) if tb else pl.BlockSpec((tk, tn), lambda i, j, k: (k, j))
    o_spec = pl.BlockSpec((tm, tn), lambda i, j, k: (i, j))
    dims = (((0 if ta else 1,), (1 if tb else 0,)), ((), ()))
    has_add = add is not None

    def body(*refs):
        a_ref, b_ref = refs[0], refs[1]
        add_ref = refs[2] if has_add else None
        o_ref, acc_ref = refs[-2], refs[-1]
        k = pl.program_id(2)
        part = lax.dot_general(a_ref[...].astype(BF16), b_ref[...].astype(BF16), dims, preferred_element_type=F32)

        def finish(res):
            if has_add:
                res = res + add_ref[...].astype(F32)
            o_ref[...] = res.astype(o_ref.dtype)

        if nk == 1:
            finish(part)
        else:
            @pl.when(k == 0)
            def _():
                acc_ref[...] = part

            @pl.when((k > 0) & (k < nk - 1))
            def _():
                acc_ref[...] += part

            @pl.when(k == nk - 1)
            def _():
                finish(acc_ref[...] + part)

    ins = [a, b] + ([add] if has_add else [])
    in_specs = [a_spec, b_spec] + ([o_spec] if has_add else [])
    acc_shape = (tm, tn) if nk > 1 else (8, 128)
    outs, comm_outs = _call(name, body, grid=grid, ins=ins, in_specs=in_specs,
                            out_shape=[jax.ShapeDtypeStruct((m, n), out_dtype)], out_specs=[o_spec],
                            scratch=[pltpu.VMEM(acc_shape, F32)], comm=comm)
    return outs[0], comm_outs


def _matmul_lnres(name, a_fn, a_ins, a_specs, tm, b, x, g, beta, comm=None):
    m = x.shape[0]
    k_dim, n = b.shape
    n_a = len(a_ins)
    row = lambda w: pl.BlockSpec((tm, w), lambda i: (i, 0))
    whole = lambda shape: pl.BlockSpec(shape, lambda i: (0, 0))

    def body(*refs):
        b_ref, x_ref, g_ref, beta_ref = refs[n_a:n_a + 4]
        y_ref, h_ref, hb_ref = refs[-3:]
        if a_fn is None:
            a = refs[0][...].astype(BF16)
        else:
            (a,) = a_fn(*[r[...].astype(F32) for r in refs[:n_a]])
            a = a.astype(BF16)
            refs[n_a + 4][...] = a
        y = jnp.dot(a, b_ref[...].astype(BF16), preferred_element_type=F32).astype(y_ref.dtype)
        y_ref[...] = y
        (h,) = _lnres_block(x_ref[...], y.astype(F32), g_ref[...], beta_ref[...])
        h_ref[...] = h
        hb_ref[...] = h.astype(hb_ref.dtype)

    sds = jax.ShapeDtypeStruct
    a_out = ([sds((m, k_dim), BF16)], [row(k_dim)]) if a_fn is not None else ([], [])
    return _call(name, body, grid=(m // tm,), ins=list(a_ins) + [b, x, g, beta],
                 in_specs=list(a_specs) + [whole((k_dim, n)), row(n), whole((1, n)), whole((1, n))],
                 out_shape=a_out[0] + [sds((m, n), BF16), sds((m, n), F32), sds((m, n), BF16)],
                 out_specs=a_out[1] + [row(n), row(n), row(n)], comm=comm)


SSD_STATE = (SSM_GROUPS * SSM_STATE, SSM_RPG * SSM_HEADDIM)


def _ssd_fwd(xc, dt_raw, proj, dt_bias, a_log, d_skip, norm_g, nb, nc, comm=None):
    t = xc.shape[0]
    row = lambda b, c: (b * nc + c, 0)
    par = lambda shape: pl.BlockSpec(shape, lambda b, c: (0, 0))

    def body(xc_ref, dt_ref, z_ref, dtb_ref, al_ref, ds_ref, ng_ref, e_ref, y_ref, ypre_ref, prev_ref, st_ref):
        @pl.when(pl.program_id(1) == 0)
        def _():
            st_ref[...] = jnp.zeros_like(st_ref)

        prev = st_ref[...]
        prev_ref[0, 0] = prev
        y, y_pre, new_state = _ssd_forward(xc_ref[...], dt_ref[...], z_ref[...].astype(F32), prev, dtb_ref[...],
                                           al_ref[...], ds_ref[...], ng_ref[...], e_ref[...])
        y_ref[...] = y.astype(y_ref.dtype)
        ypre_ref[...] = y_pre
        st_ref[...] = new_state

    return _call(
        "ssd_fwd", body, grid=(nb, nc), ins=[xc, dt_raw, proj, dt_bias, a_log, d_skip, norm_g, _head_expander()],
        in_specs=[pl.BlockSpec((CHUNK, SSM_CONV_DIM), row), pl.BlockSpec((CHUNK, 128), row),
                  pl.BlockSpec((CHUNK, SSM_INNER), lambda b, c: (b * nc + c, 1)),
                  par((1, 128)), par((1, 128)), par((1, SSM_INNER)), par((1, SSM_INNER)), par((128, SSM_INNER))],
        out_specs=[pl.BlockSpec((CHUNK, SSM_INNER), row), pl.BlockSpec((CHUNK, SSM_INNER), row),
                   pl.BlockSpec((1, 1) + SSD_STATE, lambda b, c: (b, c, 0, 0))],
        out_shape=[jax.ShapeDtypeStruct((t, SSM_INNER), BF16), jax.ShapeDtypeStruct((t, SSM_INNER), F32),
                   jax.ShapeDtypeStruct((nb, nc) + SSD_STATE, F32)],
        scratch=[pltpu.VMEM(SSD_STATE, F32)], comm=comm)


def _ssd_bwd(xc, dt_raw, proj, prevs, y_pre, dt_bias, a_log, d_skip, norm_g, dy, dproj, nb, nc, comm=None):
    t = xc.shape[0]
    row = lambda b, c: (b * nc + (nc - 1 - c), 0)
    par = lambda shape: pl.BlockSpec(shape, lambda b, c: (0, 0))
    z_spec = pl.BlockSpec((CHUNK, SSM_INNER), lambda b, c: (b * nc + (nc - 1 - c), 1))

    def body(xc_ref, dt_ref, z_ref, prev_ref, ypre_ref, dtb_ref, al_ref, ds_ref, ng_ref, e_ref, dy_ref, _,
             dxc_ref, ddt_ref, dz_ref, ddtb_ref, dal_ref, dds_ref, dng_ref, dst_ref):
        @pl.when(pl.program_id(1) == 0)
        def _():
            dst_ref[...] = jnp.zeros_like(dst_ref)

        @pl.when((pl.program_id(0) == 0) & (pl.program_id(1) == 0))
        def _():
            ddtb_ref[...] = jnp.zeros_like(ddtb_ref)
            dal_ref[...] = jnp.zeros_like(dal_ref)
            dds_ref[...] = jnp.zeros_like(dds_ref)
            dng_ref[...] = jnp.zeros_like(dng_ref)

        dxc, ddt, dz, dprev, ddtb, dal, dds, dng = _ssd_backward(
            xc_ref[...], dt_ref[...], z_ref[...].astype(F32), prev_ref[0, 0], ypre_ref[...], dtb_ref[...], al_ref[...],
            ds_ref[...], ng_ref[...], e_ref[...], dy_ref[...].astype(F32), dst_ref[...])
        dxc_ref[...] = dxc
        ddt_ref[...] = ddt.astype(ddt_ref.dtype)
        dz_ref[...] = dz.astype(dz_ref.dtype)
        dst_ref[...] = dprev
        ddtb_ref[...] += ddtb
        dal_ref[...] += dal
        dds_ref[...] += dds
        dng_ref[...] += dng

    return _call(
        "ssd_bwd", body, grid=(nb, nc),
        ins=[xc, dt_raw, proj, prevs, y_pre, dt_bias, a_log, d_skip, norm_g, _head_expander(), dy, dproj],
        in_specs=[pl.BlockSpec((CHUNK, SSM_CONV_DIM), row), pl.BlockSpec((CHUNK, DT_LANES), row), z_spec,
                  pl.BlockSpec((1, 1) + SSD_STATE, lambda b, c: (b, nc - 1 - c, 0, 0)),
                  pl.BlockSpec((CHUNK, SSM_INNER), row),
                  par((1, 128)), par((1, 128)), par((1, SSM_INNER)), par((1, SSM_INNER)), par((128, SSM_INNER)),
                  pl.BlockSpec((CHUNK, SSM_INNER), row), pl.BlockSpec(memory_space=pl.ANY)],
        out_specs=[pl.BlockSpec((CHUNK, SSM_CONV_DIM), row), pl.BlockSpec((CHUNK, DT_LANES), row), z_spec,
                   par((1, 128)), par((1, 128)), par((1, 128)), par((1, SSM_INNER))],
        out_shape=[jax.ShapeDtypeStruct((t, SSM_CONV_DIM), F32), jax.ShapeDtypeStruct((t, DT_LANES), BF16),
                   jax.ShapeDtypeStruct(dproj.shape, dproj.dtype), jax.ShapeDtypeStruct((1, 128), F32),
                   jax.ShapeDtypeStruct((1, 128), F32), jax.ShapeDtypeStruct((1, 128), F32),
                   jax.ShapeDtypeStruct((1, SSM_INNER), F32)],
        scratch=[pltpu.VMEM(SSD_STATE, F32)], comm=comm, aliases={11: 2})


def _loss_head(y, target):
    t, d = y.shape
    tm = _pick_tile(t, (256,))
    blk = pl.BlockSpec((tm, d), lambda i: (i, 0))

    def body(y_ref, t_ref, loss_ref, dy_ref):
        err = y_ref[...] - t_ref[...]
        dy_ref[...] = err * (1.0 / d)

        @pl.when(pl.program_id(0) == 0)
        def _():
            loss_ref[...] = jnp.zeros_like(loss_ref)

        loss_ref[...] += 0.5 * jnp.sum(jnp.mean(err * err, axis=-1, keepdims=True), axis=0, keepdims=True)

    return _call("loss_head", body, grid=(t // tm,), ins=[y, target], in_specs=[blk, blk],
                 out_specs=[pl.BlockSpec((1, 1), lambda i: (0, 0)), blk],
                 out_shape=[jax.ShapeDtypeStruct((1, 1), F32), jax.ShapeDtypeStruct((t, d), F32)])[0]


def _adamw_math(g, w, m, v):
    m_new = ADAM_B1 * m + (1.0 - ADAM_B1) * g
    v_new = ADAM_B2 * v + (1.0 - ADAM_B2) * jnp.square(g)
    m_hat = m_new / (1.0 - ADAM_B1 ** ADAM_STEP)
    v_hat = v_new / (1.0 - ADAM_B2 ** ADAM_STEP)
    delta = -ADAM_LR * (m_hat / (jnp.sqrt(v_hat) + ADAM_EPS) + ADAM_WD * w)
    return delta, m_new, v_new


def _adamw_sharded(name, parts, w, m, v, comm=None):
    _, a, b = w.shape
    tr = _pick_tile(a, (128,))
    nt = a // tr
    part_specs = [pl.BlockSpec((N_DEV, tr, b),
                               (lambda l, i, _k=k: (0, jnp.where(l == _k, i, jnp.where(l > _k, nt - 1, 0)), 0)))
                  for k in range(DEPTH)]
    blk = pl.BlockSpec((1, tr, b), lambda l, i: (l, i, 0))

    def body(*refs):
        p_refs = refs[:DEPTH]
        w_ref, m_ref, v_ref, g_out, d_out, m_out, v_out = refs[DEPTH:]
        for k in range(DEPTH):
            @pl.when(pl.program_id(0) == k)
            def _(p_ref=p_refs[k]):
                g = p_ref[0].astype(F32)
                for p in range(1, N_DEV):
                    g = g + p_ref[p].astype(F32)
                delta, m_new, v_new = _adamw_math(g, w_ref[0], m_ref[0], v_ref[0])
                g_out[0] = g
                d_out[0] = delta
                m_out[0] = m_new
                v_out[0] = v_new

    return _call(name, body, grid=(DEPTH, nt), ins=list(parts) + [w, m, v], in_specs=part_specs + [blk, blk, blk],
                 out_specs=[blk] * 4, out_shape=[jax.ShapeDtypeStruct(w.shape, F32)] * 4, comm=comm)


def _adamw_small(name, g, w, m, v):
    full = pl.BlockSpec(w.shape, lambda i: (0, 0))

    def body(g_ref, w_ref, m_ref, v_ref, d_out, m_out, v_out):
        delta, m_new, v_new = _adamw_math(g_ref[...], w_ref[...], m_ref[...], v_ref[...])
        d_out[...] = delta
        m_out[...] = m_new
        v_out[...] = v_new

    return _call(name, body, grid=(1,), ins=[g, w, m, v], in_specs=[full] * 4, out_specs=[full] * 3,
                 out_shape=[jax.ShapeDtypeStruct(w.shape, F32)] * 3)[0]


def _sum_parts(name, parts):
    n_parts, rows, cols = parts.shape
    tr = _pick_tile(rows, (512, 256, 128, 64, 32, 16, 8))

    def body(p_ref, o_ref):
        acc = p_ref[0]
        for p in range(1, n_parts):
            acc = acc + p_ref[p]
        o_ref[...] = acc

    return _call(name, body, grid=(rows // tr,), ins=[parts],
                 in_specs=[pl.BlockSpec((n_parts, tr, cols), lambda i: (0, i, 0))],
                 out_specs=[pl.BlockSpec((tr, cols), lambda i: (i, 0))],
                 out_shape=[jax.ShapeDtypeStruct((rows, cols), parts.dtype)])[0][0]


W_IN_SHARD = IN_COLS // N_DEV


def _pack_w_in(gathered):
    r = gathered.shape[1]
    tr = _pick_tile(r, (128,))

    def body(g_ref, main_ref, dt_ref):
        w = jnp.concatenate([g_ref[j].astype(F32) for j in range(N_DEV)], axis=1)
        main_ref[...] = jnp.concatenate([w[:, :XBC_COL], w[:, GA_COL:], w[:, XBC_COL:DT_COL]],
                                        axis=1).astype(main_ref.dtype)
        dt_ref[...] = jnp.concatenate([w[:, DT_COL:GA_COL], jnp.zeros((tr, DT_LANES - SSM_HEADS), F32)],
                                      axis=1).astype(dt_ref.dtype)

    return _call("pack_w_in", body, grid=(r // tr,), ins=[gathered],
                 in_specs=[pl.BlockSpec((N_DEV, tr, W_IN_SHARD), lambda i: (0, i, 0))],
                 out_specs=[pl.BlockSpec((tr, P_COLS), lambda i: (i, 0)), pl.BlockSpec((tr, DT_LANES), lambda i: (i, 0))],
                 out_shape=[jax.ShapeDtypeStruct((r, P_COLS), gathered.dtype),
                            jax.ShapeDtypeStruct((r, DT_LANES), gathered.dtype)])[0]


def _unpack_w_in(main, dt):
    r = main.shape[0]
    tr = _pick_tile(r, (128,))

    def body(main_ref, dt_ref, o_ref):
        main = main_ref[...].astype(F32)
        w = jnp.concatenate([main[:, :P_GATE], main[:, P_XBC:], dt_ref[...].astype(F32)[:, :SSM_HEADS],
                             main[:, P_GATE:P_XBC]], axis=1)
        for j in range(N_DEV):
            o_ref[j] = w[:, j * W_IN_SHARD:(j + 1) * W_IN_SHARD].astype(o_ref.dtype)

    return _call("unpack_w_in", body, grid=(r // tr,), ins=[main, dt],
                 in_specs=[pl.BlockSpec((tr, P_COLS), lambda i: (i, 0)), pl.BlockSpec((tr, DT_LANES), lambda i: (i, 0))],
                 out_specs=[pl.BlockSpec((N_DEV, tr, W_IN_SHARD), lambda i: (0, i, 0))],
                 out_shape=[jax.ShapeDtypeStruct((N_DEV, r, W_IN_SHARD), main.dtype)])[0][0]


def _pad_heads(v):
    return jnp.pad(v, (0, 128 - SSM_HEADS)).reshape(1, 128)


def _run_step(x, mem, target, small, ex):
    nb, s, d = x.shape
    t = nb * s
    nc = s // CHUNK
    rows = _pick_tile(t, (256,))
    rows_wide = _pick_tile(t, (512, 256))
    tq = _pick_tile(s, (512, 256))
    vec = lambda a: a.reshape(1, -1)
    full1 = lambda shape: pl.BlockSpec(shape, lambda i: (0,) * len(shape))
    row1 = lambda tm, w: pl.BlockSpec((tm, w), lambda i: (i, 0))
    sds = jax.ShapeDtypeStruct

    def mm(call, l, a, b, **kw):
        comm = ex.before(call, l)
        out, comm_outs = _matmul(call, a, b, comm=comm, **kw)
        if comm is not None:
            ex.after(call, l, comm_outs)
        return out

    def stage_bwd(call, l, *args, **kw):
        comm = ex.before(call, l)
        outs, comm_outs = _stage_bwd(call, *args, comm=comm, **kw)
        if comm is not None:
            ex.after(call, l, comm_outs)
        return outs

    def stage_fwd(call, l, *args):
        comm = ex.before(call, l)
        outs, comm_outs = _stage_fwd(call, *args, comm=comm)
        if comm is not None:
            ex.after(call, l, comm_outs)
        return outs

    mem_specs = [row1(256, d), full1((1, d)), full1((1, d))]
    mem_ins = [mem.reshape(nb * MEM_LEN, d), vec(small["mem_ln_g"]), vec(small["mem_ln_b"])]
    (mem_n,) = stage_fwd("memln_fwd", 0, _memln_block, (nb * MEM_LEN // 256,), mem_ins, mem_specs,
                          [sds((nb * MEM_LEN, d), BF16)], [row1(256, d)])

    h = x.reshape(t, d)
    h_bf = h.astype(BF16)
    ln_specs = [row1(rows, d), row1(rows, d), full1((1, d)), full1((1, d))]
    saved = []
    for l in range(DEPTH):
        sv = {"h_bf": h_bf}
        w_p, w_dt = ex.weight("w_in", l)
        proj = mm("mm_in", l, h_bf, w_p, out_dtype=BF16)
        dt_raw = mm("mm_dt", l, h_bf, w_dt)
        sv["proj"] = proj
        sgu_ins = [proj, vec(small["sg_ln_g"][l]), vec(small["sg_ln_b"][l]), small["sg_w"][l], small["sg_b"][l].T]
        sgu_specs = [pl.BlockSpec((CHUNK, 2 * d), lambda i: (i, 0)), full1((1, d)), full1((1, d)),
                     full1((SG_GROUPS, CHUNK, CHUNK)), full1((CHUNK, SG_GROUPS))]
        (a_out,) = stage_fwd("sgu_fwd", l, _sgu_block, (t // CHUNK,), sgu_ins, sgu_specs, [sds((t, d), BF16)],
                              [row1(CHUNK, d)])
        sv["sgu"] = (sgu_ins, sgu_specs)
        sv["a_out"] = a_out
        cw = 256
        conv_ins = [proj, small["conv_w"][l], vec(small["conv_b"][l])]
        conv_specs = [pl.BlockSpec((s, cw), lambda j, b: (b, P_XBC // cw + j)),
                      pl.BlockSpec((SSM_CONV, cw), lambda j, b: (0, j)), pl.BlockSpec((1, cw), lambda j, b: (0, j))]
        conv_out_spec = pl.BlockSpec((s, cw), lambda j, b: (b, j))
        (xc,) = stage_fwd("conv_fwd", l, _conv_block, (SSM_CONV_DIM // cw, nb), conv_ins, conv_specs,
                           [sds((t, SSM_CONV_DIM), F32)], [conv_out_spec])
        sv["conv"] = (conv_ins, conv_specs, conv_out_spec)
        ssd_par = [_pad_heads(small["dt_bias"][l]), _pad_heads(small["a_log"][l]),
                   vec(jnp.repeat(small["d_skip"][l], SSM_HEADDIM)), vec(small["ssm_norm_g"][l])]
        comm = ex.before("ssd_fwd", l)
        (y_ssd, y_pre, prevs), comm_outs = _ssd_fwd(xc, dt_raw, proj, *ssd_par, nb, nc, comm=comm)
        if comm is not None:
            ex.after("ssd_fwd", l, comm_outs)
        sv["ssd"] = (xc, dt_raw, prevs, y_pre, ssd_par)
        sv["y_ssd"] = y_ssd
        br_a = mm("mm_sq", l, a_out, ex.weight("p_a", l), out_dtype=BF16)
        br_b = mm("mm_pb", l, y_ssd, ex.weight("p_b", l), out_dtype=BF16)
        merge_ins = [proj, br_a, br_b]
        merge_out_spec = row1(rows_wide, d)
        merge_specs = [pl.BlockSpec((rows_wide, 2 * d), lambda i: (i, P_GATE // (2 * d))), merge_out_spec, merge_out_spec]
        sv["merge"] = (merge_ins, merge_specs, merge_out_spec)
        ln_par = [(vec(small["ln_g"][l, k]), vec(small["ln_b"][l, k])) for k in range(3)]

        def fused(call, a_fn, a_ins, a_specs, tm, w, x_in, par):
            comm = ex.before(call, l)
            outs, comm_outs = _matmul_lnres(call, a_fn, a_ins, a_specs, tm, w, x_in, *par, comm=comm)
            if comm is not None:
                ex.after(call, l, comm_outs)
            return outs

        merged, y1, h1, h1_bf = fused("mm_mix_ln", _merge_block, merge_ins, merge_specs, rows_wide,
                                      ex.weight("w_mix_o", l), h, ln_par[0])
        sv["merged"] = merged
        sv["ln1"] = [h, y1, *ln_par[0]]
        q = mm("mm_sq", l, h1_bf, ex.weight("w_xq", l), out_dtype=BF16)
        kv = mm("mm_kv", l, mem_n, ex.weight("w_xkv", l), out_dtype=BF16)
        attn_ins = [q, kv]
        attn_out_spec = pl.BlockSpec((tq, d), lambda b, i: (b * (s // tq) + i, 0))
        attn_specs = [attn_out_spec, pl.BlockSpec((MEM_LEN, 2 * d), lambda b, i: (b, 0))]
        (o,) = stage_fwd("attn_fwd", l, _attn_block, (nb, s // tq), attn_ins, attn_specs, [sds((t, d), BF16)],
                          [attn_out_spec])
        sv["attn"] = (attn_ins, attn_specs, attn_out_spec)
        sv["o"] = o
        sv["h1_bf"] = h1_bf
        y2, h2, h2_bf = fused("mm_xo_ln", None, [o], [row1(rows_wide, d)], rows_wide, ex.weight("w_xo", l), h1, ln_par[1])
        sv["ln2"] = [h1, y2, *ln_par[1]]
        sv["h2_bf"] = h2_bf
        gu = mm("mm_ffn_in", l, h2_bf, ex.weight("w_ffn_in", l), out_dtype=BF16)
        act, y3, h3, h3_bf = fused("mm_ffn_out_ln", _swiglu_block, [gu], [row1(rows, 2 * FFN_HIDDEN)], rows,
                                   ex.weight("w_ffn_out", l), h2, ln_par[2])
        sv["gu"] = gu
        sv["act"] = act
        sv["ln3"] = [h2, y3, *ln_par[2]]
        h, h_bf = h3, h3_bf
        saved.append(sv)

    loss, dh = _loss_head(h, target.reshape(t, d))

    g_small = {n: [None] * DEPTH for n in SMALL_REP + SMALL_SH if n not in ("mem_ln_g", "mem_ln_b")}
    dmem_n = []
    ln_grads = [(0, (), F32), (1, (), BF16), (2, (0,), F32), (3, (0,), F32)]
    for l in reversed(range(DEPTH)):
        sv = saved[l]
        dln_g, dln_b = [None] * 3, [None] * 3
        dres, dy3, dln_g[2], dln_b[2] = stage_bwd("lnres_bwd", l, _lnres_block, (t // rows,), sv["ln3"], ln_specs,
                                                  [(dh,)], [row1(rows, d)], ln_grads)
        ex.grad("w_ffn_out", l, mm("mm_ffn_out_dw", l, sv["act"], dy3, ta=True, out_dtype=BF16))
        (dgu,) = stage_bwd("swiglu_bwd", l, _swiglu_block, (t // rows,), [sv["gu"]], [row1(rows, 2 * FFN_HIDDEN)],
                           [], [], [(0, (), BF16)], ct_product=(dy3, row1(rows, d), ex.weight("w_ffn_out", l)))
        ex.grad("w_ffn_in", l, mm("mm_ffn_in_dw", l, sv["h2_bf"], dgu, ta=True, out_dtype=BF16))
        dh2 = mm("mm_ffn_in_dx", l, dgu, ex.weight("w_ffn_in", l), tb=True, add=dres)
        dres, dy2, dln_g[1], dln_b[1] = stage_bwd("lnres_bwd", l, _lnres_block, (t // rows,), sv["ln2"], ln_specs,
                                                  [(dh2,)], [row1(rows, d)], ln_grads)
        ex.grad("w_xo", l, mm("mm_sq_dw", l, sv["o"], dy2, ta=True, out_dtype=BF16))
        do = mm("mm_sq_dx", l, dy2, ex.weight("w_xo", l), tb=True, out_dtype=BF16)
        attn_ins, attn_specs, attn_out_spec = sv["attn"]
        dq, dkv = stage_bwd("attn_bwd", l, _attn_block, (nb, s // tq), attn_ins, attn_specs, [(do,)], [attn_out_spec],
                            [(0, (), BF16), (1, (1,), F32)])
        ex.grad("w_xq", l, mm("mm_sq_dw", l, sv["h1_bf"], dq, ta=True, out_dtype=BF16))
        dh1 = mm("mm_sq_dx", l, dq, ex.weight("w_xq", l), tb=True, add=dres)
        ex.grad("w_xkv", l, mm("mm_kv_dw", l, mem_n, dkv, ta=True, out_dtype=BF16))
        dmem_n.append(mm("mm_kv_dx", l, dkv, ex.weight("w_xkv", l), tb=True))
        dres, dy1, dln_g[0], dln_b[0] = stage_bwd("lnres_bwd", l, _lnres_block, (t // rows,), sv["ln1"], ln_specs,
                                                  [(dh1,)], [row1(rows, d)], ln_grads)
        g_small["ln_g"][l] = jnp.concatenate(dln_g, axis=0)
        g_small["ln_b"][l] = jnp.concatenate(dln_b, axis=0)
        ex.grad("w_mix_o", l, mm("mm_sq_dw", l, sv["merged"], dy1, ta=True, out_dtype=BF16))
        merge_ins, merge_specs, merge_out_spec = sv["merge"]
        dproj, dbr_a, dbr_b = stage_bwd("merge_bwd", l, _merge_block, (t // rows_wide,), merge_ins, merge_specs, [], [],
                                        [(0, (), BF16, ((t, P_COLS), merge_specs[0])), (1, (), BF16), (2, (), BF16)],
                                        ct_product=(dy1, merge_out_spec, ex.weight("w_mix_o", l)))
        ex.grad("p_a", l, mm("mm_sq_dw", l, sv["a_out"], dbr_a, ta=True, out_dtype=BF16))
        da_out = mm("mm_sq_dx", l, dbr_a, ex.weight("p_a", l), tb=True, out_dtype=BF16)
        ex.grad("p_b", l, mm("mm_pb_dw", l, sv["y_ssd"], dbr_b, ta=True, out_dtype=BF16))
        dy_ssd = mm("mm_pb_dx", l, dbr_b, ex.weight("p_b", l), tb=True, out_dtype=BF16)
        sgu_ins, sgu_specs = sv["sgu"]
        dproj, dsg_ln_g, dsg_ln_b, dsg_w, dsg_b = stage_bwd(
            "sgu_bwd", l, _sgu_block, (t // CHUNK,), sgu_ins, sgu_specs, [(da_out,)], [row1(CHUNK, d)],
            [(0, (), BF16, ((t, P_COLS), sgu_specs[0]), dproj), (1, (0,), F32), (2, (0,), F32), (3, (0,), F32),
             (4, (0,), F32)])
        g_small["sg_ln_g"][l], g_small["sg_ln_b"][l], g_small["sg_w"][l], g_small["sg_b"][l] = (
            dsg_ln_g[0], dsg_ln_b[0], dsg_w, dsg_b.T)
        xc, dt_raw, prevs, y_pre, ssd_par = sv["ssd"]
        comm = ex.before("ssd_bwd", l)
        (dxc, ddt, dproj, ddtb, dal, dds, dng), comm_outs = _ssd_bwd(xc, dt_raw, sv["proj"], prevs, y_pre, *ssd_par,
                                                                     dy_ssd, dproj, nb, nc, comm=comm)
        if comm is not None:
            ex.after("ssd_bwd", l, comm_outs)
        g_small["dt_bias"][l], g_small["a_log"][l], g_small["d_skip"][l] = (
            ddtb[0, :SSM_HEADS], dal[0, :SSM_HEADS], dds[0, :SSM_HEADS])
        g_small["ssm_norm_g"][l] = dng[0]
        conv_ins, conv_specs, conv_out_spec = sv["conv"]
        dproj, dconv_w, dconv_b = stage_bwd("conv_bwd", l, _conv_block, (SSM_CONV_DIM // 256, nb), conv_ins, conv_specs,
                                            [(dxc,)], [conv_out_spec],
                                            [(0, (), BF16, ((t, P_COLS), conv_specs[0]), dproj), (1, (1,), F32),
                                             (2, (1,), F32)])
        g_small["conv_w"][l], g_small["conv_b"][l] = dconv_w, dconv_b[0]
        if l == 0:
            dmg, dmb = stage_bwd("memln_bwd", l, _memln_block, (nb * MEM_LEN // 256,), mem_ins, mem_specs,
                                 [tuple(dmem_n)], [row1(256, d)], [(1, (0,), F32), (2, (0,), F32)])
            done = {n: jnp.stack(g, axis=0) for n, g in g_small.items()}
            done["mem_ln_g"], done["mem_ln_b"] = dmg[0], dmb[0]
            ex.small_grads(done)
        w_p, w_dt = ex.weight("w_in", l)
        g_dt = mm("mm_dt_dw", l, sv["h_bf"], ddt, ta=True, out_dtype=BF16)
        ex.grad("w_in", l, _unpack_w_in(mm("mm_in_dw", l, sv["h_bf"], dproj, ta=True, out_dtype=BF16), g_dt))
        dh = mm("mm_in_dx", l, dproj, w_p, tb=True, add=mm("mm_dt_dx", l, ddt, w_dt, tb=True, add=dres))

    return loss, dh.reshape(nb, s, d)


def _pack_flat(arrays, rows):
    flat = jnp.concatenate([a.reshape(-1) for a in arrays])
    return jnp.pad(flat, (0, rows * 128 - flat.shape[0])).reshape(rows, 128)


def _unpack_flat(packed, shapes):
    lead = packed.shape[:-2]
    flat = packed.reshape(lead + (-1,))
    out, pos = [], 0
    for shape in shapes:
        n = math.prod(shape)
        out.append(flat[..., pos:pos + n].reshape(lead + tuple(shape)))
        pos += n
    return out


def _small_rows(n_elems):
    return -(-n_elems // (128 * SMALL_ROW_TILE)) * SMALL_ROW_TILE


def _from_shards(name, gathered):
    _, a, b = gathered.shape
    if name == "w_in":
        return tuple(_pack_w_in(gathered))
    if name in BIG_COL_SHARDED:
        return _join_columns(gathered)
    return gathered.reshape(N_DEV * a, b)


def _to_shards(name, g):
    if name == "w_in":
        return g
    if name in BIG_COL_SHARDED:
        return _split_columns(g)
    a, b = g.shape
    return g.reshape(N_DEV, a // N_DEV, b)


def _join_columns(gathered):
    _, r, b = gathered.shape
    tr = _pick_tile(r, (128,))

    def body(g_ref, o_ref):
        o_ref[...] = jnp.concatenate([g_ref[j].astype(F32) for j in range(N_DEV)], axis=1).astype(o_ref.dtype)

    return _call("join_columns", body, grid=(r // tr,), ins=[gathered],
                 in_specs=[pl.BlockSpec((N_DEV, tr, b), lambda i: (0, i, 0))],
                 out_specs=[pl.BlockSpec((tr, N_DEV * b), lambda i: (i, 0))],
                 out_shape=[jax.ShapeDtypeStruct((r, N_DEV * b), gathered.dtype)])[0][0]


def _split_columns(full):
    r, nb = full.shape
    b = nb // N_DEV
    tr = _pick_tile(r, (128,))

    def body(f_ref, o_ref):
        w = f_ref[...].astype(F32)
        for j in range(N_DEV):
            o_ref[j] = w[:, j * b:(j + 1) * b].astype(o_ref.dtype)

    return _call("split_columns", body, grid=(r // tr,), ins=[full],
                 in_specs=[pl.BlockSpec((tr, nb), lambda i: (i, 0))],
                 out_specs=[pl.BlockSpec((N_DEV, tr, b), lambda i: (0, i, 0))],
                 out_shape=[jax.ShapeDtypeStruct((N_DEV, r, b), full.dtype)])[0][0]


class _MeshExchange:
    def __init__(self, shards_bf16, first):
        self.shards = shards_bf16
        self.full = dict(first)
        self.pieces = {}
        self.grads = {}
        self.to_send = {}
        self.received = {}
        self.small = None
        self.small_gathered = None

    def weight(self, name, l):
        if (name, l) not in self.full:
            got = jnp.concatenate([self.pieces[(name, l, q)] for q in range(W_IN_PIECES)], axis=1)
            self.full[(name, l)] = _from_shards(name, got)
        return self.full[(name, l)]

    def grad(self, name, l, g):
        self.grads[(name, l)] = g

    def small_grads(self, done):
        self.small = done

    def partial_sums(self, name, l):
        if (name, l, None) in self.received:
            return self.received[(name, l, None)]
        return jnp.concatenate([self.received[(name, l, q)] for q in range(W_IN_PIECES)], axis=1)

    def _slices(self, name, l, piece):
        n_rows = D_MODEL // W_IN_PIECES
        key, rows = (name, l), None if piece is None else (piece * n_rows, n_rows)
        if key not in self.to_send:
            self.to_send[key] = _to_shards(name, self.grads[key])
        return self.to_send[key], rows

    def before(self, call, l):
        comm = _Comm()
        for name, layer, piece in GATHER_PLAN.get((call, l), ()):
            n_rows = D_MODEL // W_IN_PIECES
            comm.gathers.append((self.shards[name], layer, None if piece is None else (piece * n_rows, n_rows)))
        if (call, l) == SMALL_GATHER_CALL:
            names = SMALL_REP + SMALL_SH
            rows = _small_rows(sum(math.prod(self.small[n].shape) for n in names))
            comm.gathers.append((_pack_flat([self.small[n] for n in names], rows), None, None))
        for name, layer, piece in SCATTER_PLAN.get((call, l), ()):
            comm.scatters.append(self._slices(name, layer, piece))
        return comm if comm.gathers or comm.scatters else None

    def after(self, call, l, outs):
        gathers = list(GATHER_PLAN.get((call, l), ()))
        for (name, layer, piece), out in zip(gathers, outs):
            if piece is None:
                self.full[(name, layer)] = _from_shards(name, out)
            else:
                self.pieces[(name, layer, piece)] = out
        outs = outs[len(gathers):]
        if (call, l) == SMALL_GATHER_CALL:
            self.small_gathered = outs[0]
            outs = outs[1:]
        for item, out in zip(SCATTER_PLAN.get((call, l), ()), outs):
            self.received[item] = out


def kernel(x, mem, mem_ln_g, mem_ln_b, w_in, sg_ln_g, sg_ln_b, sg_w, sg_b, conv_w, conv_b, dt_bias, a_log, d_skip, ssm_norm_g, p_a, p_b, w_mix_o, w_xq, w_xkv, w_xo, w_ffn_in, w_ffn_out, ln_g, ln_b, loss_target, m_mem_ln_g, m_mem_ln_b, m_w_in, m_sg_ln_g, m_sg_ln_b, m_sg_w, m_sg_b, m_conv_w, m_conv_b, m_dt_bias, m_a_log, m_d_skip, m_ssm_norm_g, m_p_a, m_p_b, m_w_mix_o, m_w_xq, m_w_xkv, m_w_xo, m_w_ffn_in, m_w_ffn_out, m_ln_g, m_ln_b, v_mem_ln_g, v_mem_ln_b, v_w_in, v_sg_ln_g, v_sg_ln_b, v_sg_w, v_sg_b, v_conv_w, v_conv_b, v_dt_bias, v_a_log, v_d_skip, v_ssm_norm_g, v_p_a, v_p_b, v_w_mix_o, v_w_xq, v_w_xkv, v_w_xo, v_w_ffn_in, v_w_ffn_out, v_ln_g, v_ln_b):
    args = dict(locals())
    w = {n: args[n] for n in WEIGHTS}
    m = {n: args["m_" + n] for n in WEIGHTS}
    v = {n: args["v_" + n] for n in WEIGHTS}
    me = 4 * lax.axis_index("x") + 2 * lax.axis_index("y") + lax.axis_index("c")

    shards = {n: w[n].astype(BF16) for n in BIG}
    sh_shapes = [w[n].shape for n in SMALL_SH]
    first = _Comm()
    first.gathers.append((shards["w_in"], 0, None))
    first.gathers.append((_pack_flat([w[n] for n in SMALL_SH], _small_rows(sum(math.prod(s) for s in sh_shapes))), None,
                          None))
    w_in0, small_sh = _comm_only("gather_first", first)
    small = {n: w[n] for n in SMALL_REP}
    for n, sh in zip(SMALL_SH, _unpack_flat(small_sh, sh_shapes)):
        small[n] = sh.transpose(1, 2, 0, 3).reshape(sh.shape[1], sh.shape[2], N_DEV * sh.shape[3])

    ex = _MeshExchange(shards, {("w_in", 0): _from_shards("w_in", w_in0)})
    loss, grad_x = _run_step(x, mem, loss_target, small, ex)
    loss = lax.psum(loss[0, 0], ("x", "y", "c"))

    out = {}
    for n in BIG[1:] + BIG[:1]:
        comm = ex.before("adamw_" + n, 0)
        out[n], comm_outs = _adamw_sharded("adamw_" + n, [ex.partial_sums(n, l) for l in range(DEPTH)], w[n], m[n], v[n],
                                           comm=comm)
        if comm is not None:
            ex.after("adamw_" + n, 0, comm_outs)
    names = SMALL_REP + SMALL_SH
    g_small = dict(zip(names, _unpack_flat(_sum_parts("sum_small_grads", ex.small_gathered),
                                           [ex.small[n].shape for n in names])))
    for n in names:
        g = g_small[n]
        if n in SMALL_SH:
            width = w[n].shape[-1]
            g = lax.dynamic_slice_in_dim(g, me * width, width, axis=-1)
        two_d = (-1, w[n].shape[-1])
        res = _adamw_small("adamw_" + n, g.reshape(two_d), w[n].reshape(two_d), m[n].reshape(two_d), v[n].reshape(two_d))
        out[n] = [g] + [r.reshape(w[n].shape) for r in res]

    results = []
    for k in range(4):
        results.extend(out[n][k] for n in WEIGHTS)
    return (loss, grad_x, *results)
```

```python
import functools
import math

import jax
import jax.numpy as jnp
from jax import lax
from jax.experimental import pallas as pl
from jax.experimental.pallas import tpu as pltpu

F32 = jnp.float32
BF16 = jnp.bfloat16
HIGHEST = lax.Precision.HIGHEST

N_DEV = 8
D_MODEL = 1024
DEPTH = 2
MEM_LEN = 256
CHUNK = 128
SG_GROUPS = 8
SSM_INNER = 2048
SSM_HEADDIM = 64
SSM_HEADS = 32
SSM_STATE = 128
SSM_GROUPS = 4
SSM_RPG = 8
SSM_CONV = 4
SSM_CONV_DIM = 3072
X_HEADS = 4
X_HEADDIM = 256
FFN_HIDDEN = 2816
ALPHA = float((2 * DEPTH) ** 0.25)
LN_EPS = 1e-5
RMS_EPS = 1e-5
XBC_COL = 4096
DT_COL = 7168
GA_COL = 7200
IN_COLS = 9248
P_GATE = 4096
P_XBC = 6144
P_COLS = 9216
DT_LANES = 128

ADAM_LR = 0.001
ADAM_B1 = 0.9
ADAM_B2 = 0.999
ADAM_EPS = 1e-08
ADAM_WD = 0.01
ADAM_STEP = 10

VMEM_LIMIT = 48 * 1024 * 1024
SMALL_ROW_TILE = 256

BIG = ("w_in", "p_a", "p_b", "w_mix_o", "w_xq", "w_xkv", "w_xo", "w_ffn_in", "w_ffn_out")
BIG_COL_SHARDED = ("w_in", "w_xkv", "w_ffn_in")
SMALL_REP = ("mem_ln_g", "mem_ln_b", "sg_ln_g", "sg_ln_b", "sg_w", "sg_b", "conv_b", "dt_bias", "a_log", "d_skip",
             "ssm_norm_g")
SMALL_SH = ("conv_w", "ln_g", "ln_b")
WEIGHTS = ("mem_ln_g", "mem_ln_b", "w_in", "sg_ln_g", "sg_ln_b", "sg_w", "sg_b", "conv_w", "conv_b", "dt_bias", "a_log",
           "d_skip", "ssm_norm_g", "p_a", "p_b", "w_mix_o", "w_xq", "w_xkv", "w_xo", "w_ffn_in", "w_ffn_out", "ln_g", "ln_b")

W_IN_PIECES = 4
GATHER_PLAN = {("sgu_fwd", 0): [("w_in", 1, 0)], ("conv_fwd", 0): [("w_in", 1, 1)],
               ("mm_ffn_in", 0): [("w_in", 1, 2), ("w_ffn_out", 0, None)], ("mm_ffn_out_ln", 0): [("w_in", 1, 3)],
               ("conv_fwd", 1): [("w_ffn_out", 1, None)]}
SCATTER_PLAN = {("mm_ffn_in_dw", 0): [("w_in", 1, 2)], ("mm_ffn_in_dx", 0): [("w_in", 1, 3)],
                ("adamw_w_ffn_in", 0): [("w_in", 0, 2)], ("adamw_w_xkv", 0): [("w_in", 0, 3)]}
for _l in range(DEPTH):
    GATHER_PLAN[("mm_in", _l)] = [(n, _l, None) for n in ("p_a", "p_b", "w_mix_o", "w_xq", "w_xkv", "w_xo")]
    GATHER_PLAN[("ssd_fwd", _l)] = [("w_ffn_in", _l, None)]
    SCATTER_PLAN[("swiglu_bwd", _l)] = [("w_ffn_out", _l, None)]
    SCATTER_PLAN[("sgu_bwd", _l)] = [("w_mix_o", _l, None), ("p_a", _l, None), ("w_xo", _l, None)]
    SCATTER_PLAN[("ssd_bwd", _l)] = [("w_ffn_in", _l, None), ("w_xkv", _l, None), ("w_xq", _l, None)]
    SCATTER_PLAN[("conv_bwd", _l)] = [("p_b", _l, None)]
    SCATTER_PLAN[("mm_in_dx", _l)] = [("w_in", _l, 0), ("w_in", _l, 1)]
SMALL_GATHER_CALL = ("mm_in_dw", 0)


def _layer_norm(x, g, b):
    mu = jnp.mean(x, axis=-1, keepdims=True)
    xc = x - mu
    var = jnp.mean(xc * xc, axis=-1, keepdims=True)
    return xc * lax.rsqrt(var + LN_EPS) * g + b


def _gelu(x):
    return 0.5 * x * (1.0 + lax.erf(x * (1.0 / math.sqrt(2.0))))


def _sigmoid(x):
    return 0.5 * jnp.tanh(0.5 * x) + 0.5


def _silu(x):
    return x * _sigmoid(x)


def _softplus(x):
    return jnp.maximum(x, 0.0) + jnp.log1p(jnp.exp(-jnp.abs(x)))


def _causal_mask():
    r = lax.broadcasted_iota(jnp.int32, (CHUNK, CHUNK), 0)
    c = lax.broadcasted_iota(jnp.int32, (CHUNK, CHUNK), 1)
    return r >= c


def _sgu_block(uv, ln_g, ln_b, w, sb):
    gu = _gelu(uv[:, :D_MODEL])
    vn = _layer_norm(_gelu(uv[:, D_MODEL:]), ln_g, ln_b)
    causal = _causal_mask()
    width = D_MODEL // SG_GROUPS
    outs = []
    for g in range(SG_GROUPS):
        wg = jnp.where(causal, w[g], 0.0).astype(BF16)
        mixed = jnp.dot(wg, vn[:, g * width:(g + 1) * width].astype(BF16), preferred_element_type=F32)
        outs.append(mixed + sb[:, g:g + 1])
    return (gu * jnp.concatenate(outs, axis=1),)


GROUP_W = SSM_RPG * SSM_HEADDIM
NT_DIMS = (((1,), (1,)), ((), ()))
TN_DIMS = (((0,), (0,)), ((), ()))


def _mxu(a, b, dims=(((1,), (0,)), ((), ()))):
    return lax.dot_general(a.astype(BF16), b.astype(BF16), dims, preferred_element_type=F32)


def _head_expander():
    return (jnp.arange(SSM_INNER)[None, :] // SSM_HEADDIM == jnp.arange(128)[:, None]).astype(BF16)


def _bf16_terms(x, n):
    terms = []
    for _ in range(n):
        t = x.astype(BF16)
        terms.append(t)
        x = x - t.astype(F32)
    return terms


def _expand_heads(q, e):
    return sum(jnp.dot(t, e, preferred_element_type=F32) for t in _bf16_terms(q, 2))


def _reduce_heads(v, e):
    return sum(lax.dot_general(t, e, NT_DIMS, preferred_element_type=F32) for t in _bf16_terms(v, 2))


def _reduce_heads_of_column_sums(v, e):
    sums = jnp.broadcast_to(jnp.sum(v, axis=0, keepdims=True), (8, v.shape[1]))
    return _reduce_heads(sums, e)[0:1, :]


def _ssd_common(xc, dtraw, dt_bias, a_log, e):
    xs = xc[:, :SSM_INNER]
    pre = dtraw + dt_bias
    dt = _softplus(pre)
    a = -jnp.exp(a_log)
    r_i = lax.broadcasted_iota(jnp.int32, (CHUNK, CHUNK), 0)
    c_i = lax.broadcasted_iota(jnp.int32, (CHUNK, CHUNK), 1)
    tril = jnp.where(r_i >= c_i, 1.0, 0.0).astype(F32)
    cs = jnp.dot(tril, dt * a, precision=HIGHEST, preferred_element_type=F32)
    cs_last = cs[CHUNK - 1:CHUNK, :]
    decay_in = jnp.exp(cs)
    decay_st = jnp.exp(cs_last - cs)
    dt_x = _expand_heads(dt, e)
    w_st_x = _expand_heads(dt * decay_st, e)
    decay_in_x = _expand_heads(decay_in, e)
    return dict(xs=xs, pre=pre, dt=dt, a=a, lower=r_i >= c_i, upper=c_i >= r_i, cs=cs, cs_t=cs.T, decay_in=decay_in,
                decay_st=decay_st, chunk_decay=jnp.exp(cs_last), dt_x=dt_x, w_st_x=w_st_x, decay_in_x=decay_in_x,
                chunk_decay_x=decay_in_x[CHUNK - 1:CHUNK, :], xdt=xs * dt_x, x_st=(xs * w_st_x).astype(BF16),
                low=lax.broadcasted_iota(jnp.int32, (CHUNK, 128), 1) < SSM_HEADDIM)


def _pair_decay(c, h):
    return jnp.exp(jnp.where(c["lower"], c["cs"][:, h:h + 1] - c["cs_t"][h:h + 1, :], -1e30))


def _pair_decay_t(c, h):
    return jnp.exp(jnp.where(c["upper"], c["cs_t"][h:h + 1, :] - c["cs"][:, h:h + 1], -1e30))


def _ssd_forward(xc, dtraw, z, prev, dt_bias, a_log, d_skip_x, norm_g, e):
    c = _ssd_common(xc, dtraw, dt_bias, a_log, e)
    y_groups, new_states = [], []
    for g in range(SSM_GROUPS):
        lanes = slice(g * GROUP_W, (g + 1) * GROUP_W)
        bg = xc[:, SSM_INNER + g * SSM_STATE:SSM_INNER + (g + 1) * SSM_STATE]
        cg = xc[:, SSM_INNER + (SSM_GROUPS + g) * SSM_STATE:SSM_INNER + (SSM_GROUPS + g + 1) * SSM_STATE].astype(BF16)
        pg = prev[g * SSM_STATE:(g + 1) * SSM_STATE, :]
        cb = _mxu(cg, bg, NT_DIMS)
        y_in = _mxu(cg, pg) * c["decay_in_x"][:, lanes]
        new_states.append(pg * c["chunk_decay_x"][:, lanes] + _mxu(bg.T, c["x_st"][:, lanes]))
        pairs = []
        for j in range(SSM_RPG // 2):
            h0 = g * SSM_RPG + 2 * j
            xp = c["xdt"][:, 128 * (h0 // 2):128 * (h0 // 2 + 1)]
            pairs.append(_mxu(cb * _pair_decay(c, h0), jnp.where(c["low"], xp, 0.0))
                         + _mxu(cb * _pair_decay(c, h0 + 1), jnp.where(c["low"], 0.0, xp)))
        y_groups.append(jnp.concatenate(pairs, axis=1) + y_in)
    y_pre = jnp.concatenate(y_groups, axis=1) + c["xs"] * d_skip_x
    gated = y_pre * _silu(z)
    normed = [gated[:, g * GROUP_W:(g + 1) * GROUP_W] for g in range(SSM_GROUPS)]
    normed = [yg * lax.rsqrt(jnp.mean(yg * yg, axis=-1, keepdims=True) + RMS_EPS) for yg in normed]
    return jnp.concatenate(normed, axis=1) * norm_g, y_pre, jnp.concatenate(new_states, axis=0)


def _ssd_backward(xc, dtraw, z, prev, y_pre, dt_bias, a_log, d_skip_x, norm_g, e, dout, dnew):
    c = _ssd_common(xc, dtraw, dt_bias, a_log, e)
    xs = c["xs"]
    sig = _sigmoid(z)
    silu_z = z * sig
    gated = y_pre * silu_z
    d_gated, normed = [], []
    for g in range(SSM_GROUPS):
        lanes = slice(g * GROUP_W, (g + 1) * GROUP_W)
        yg = gated[:, lanes]
        r = lax.rsqrt(jnp.mean(yg * yg, axis=-1, keepdims=True) + RMS_EPS)
        n = yg * r
        gh = dout[:, lanes] * norm_g[:, lanes]
        d_gated.append(r * (gh - n * jnp.mean(gh * n, axis=-1, keepdims=True)))
        normed.append(n)
    d_gated = jnp.concatenate(d_gated, axis=1)
    dnorm_g = jnp.sum(dout * jnp.concatenate(normed, axis=1), axis=0, keepdims=True)
    dy = d_gated * silu_z
    dz = d_gated * y_pre * (sig * (1.0 + z * (1.0 - sig)))
    dxs = dy * d_skip_x
    dd_skip = _reduce_heads_of_column_sums(dy * xs, e)

    lane = lax.broadcasted_iota(jnp.int32, (CHUNK, 128), 1)
    sub = lax.broadcasted_iota(jnp.int32, (8, 128), 0)
    dcs_neg = jnp.zeros((CHUNK, 128), F32)
    row_slabs = []
    dxdt, dx_st, d_decay_in_x, dprev, d_chunk_decay_x, db_all, dc_all = [], [], [], [], [], [], []
    for g in range(SSM_GROUPS):
        lanes = slice(g * GROUP_W, (g + 1) * GROUP_W)
        bg = xc[:, SSM_INNER + g * SSM_STATE:SSM_INNER + (g + 1) * SSM_STATE].astype(BF16)
        cg_f = xc[:, SSM_INNER + (SSM_GROUPS + g) * SSM_STATE:SSM_INNER + (SSM_GROUPS + g + 1) * SSM_STATE]
        cg = cg_f.astype(BF16)
        pg = prev[g * SSM_STATE:(g + 1) * SSM_STATE, :]
        dng = dnew[g * SSM_STATE:(g + 1) * SSM_STATE, :]
        dy_g = dy[:, lanes]
        cb_t = _mxu(bg, cg, NT_DIMS)
        t1 = (dy_g * c["decay_in_x"][:, lanes]).astype(BF16)
        d_decay_in_x.append(dy_g * _mxu(cg, pg))
        dc = _mxu(t1, pg, NT_DIMS)
        dprev.append(_mxu(cg_f.T, t1) + dng * c["chunk_decay_x"][:, lanes])
        d_chunk_decay_x.append(dng * pg)
        db = _mxu(c["x_st"][:, lanes], dng, NT_DIMS)
        dx_st.append(_mxu(bg, dng))
        dcb_t = jnp.zeros((CHUNK, CHUNK), F32)
        rows = []
        for j in range(SSM_RPG // 2):
            h0 = g * SSM_RPG + 2 * j
            blk = slice(128 * (h0 // 2), 128 * (h0 // 2 + 1))
            xp = c["xdt"][:, blk]
            dyp = dy[:, blk].astype(BF16)
            pair_dx = []
            for k, xk in enumerate((jnp.where(c["low"], xp, 0.0), jnp.where(c["low"], 0.0, xp))):
                dec_t = _pair_decay_t(c, h0 + k)
                pair_dx.append(_mxu(cb_t * dec_t, dyp))
                dml_t = _mxu(xk, dyp, NT_DIMS) * dec_t
                dcb_t = dcb_t + dml_t
                dseg_t = dml_t * cb_t
                dcs_neg = dcs_neg + jnp.where(lane == h0 + k, jnp.sum(dseg_t, axis=-1, keepdims=True), 0.0)
                rows.append(jnp.sum(dseg_t, axis=0, keepdims=True))
            dxdt.append(jnp.where(c["low"], pair_dx[0], pair_dx[1]))
        slab = jnp.zeros((8, 128), F32)
        for r in range(SSM_RPG):
            slab = slab + jnp.where(sub == r, rows[r], 0.0)
        row_slabs.append(slab)
        dc_all.append(dc + _mxu(dcb_t.T, bg))
        db_all.append(db + _mxu(dcb_t, cg))
    dxdt = jnp.concatenate(dxdt, axis=1)
    dx_st = jnp.concatenate(dx_st, axis=1)
    by_head = jnp.concatenate(row_slabs + [jnp.zeros((CHUNK - SSM_HEADS, 128), F32)], axis=0)
    dcs = by_head.T - dcs_neg
    dxs = dxs + dxdt * c["dt_x"] + dx_st * c["w_st_x"]
    ddt = _reduce_heads(dxdt * xs, e)
    dw_st = _reduce_heads(dx_st * xs, e)
    dcs = dcs + _reduce_heads(jnp.concatenate(d_decay_in_x, axis=1), e) * c["decay_in"]
    ddt = ddt + dw_st * c["decay_st"]
    d_log_st = dw_st * c["dt"] * c["decay_st"]
    dcs = dcs - d_log_st
    d_chunk_decay = _reduce_heads_of_column_sums(jnp.concatenate(d_chunk_decay_x, axis=1), e)
    dcs_last = jnp.sum(d_log_st, axis=0, keepdims=True) + d_chunk_decay * c["chunk_decay"]
    row = lax.broadcasted_iota(jnp.int32, (CHUNK, 128), 0)
    dcs = dcs + jnp.where(row == CHUNK - 1, dcs_last, 0.0)
    triu = jnp.where(c["upper"], 1.0, 0.0).astype(F32)
    dda = jnp.dot(triu, dcs, precision=HIGHEST, preferred_element_type=F32)
    ddt = ddt + dda * c["a"]
    da_log = jnp.sum(dda * c["dt"], axis=0, keepdims=True) * c["a"]
    dpre = ddt * _sigmoid(c["pre"])
    dxc = jnp.concatenate([dxs] + db_all + dc_all, axis=1)
    return (dxc, dpre, dz, jnp.concatenate(dprev, axis=0), jnp.sum(dpre, axis=0, keepdims=True), da_log, dd_skip,
            dnorm_g)


def _conv_block(x, w, b):
    rows = lax.broadcasted_iota(jnp.int32, x.shape, 0)
    acc = x * w[SSM_CONV - 1:SSM_CONV, :] + b
    for k in range(SSM_CONV - 1):
        shift = SSM_CONV - 1 - k
        acc = acc + _shift_rows(x, rows, shift) * w[k:k + 1, :]
    return (_silu(acc),)


@functools.partial(jax.custom_vjp, nondiff_argnums=(2,))
def _shift_rows(x, rows, shift):
    return jnp.where(rows >= shift, pltpu.roll(x, shift, 0), 0.0)


def _shift_rows_fwd(x, rows, shift):
    return _shift_rows(x, rows, shift), rows


def _shift_rows_bwd(shift, rows, g):
    n = g.shape[0]
    return jnp.where(rows < n - shift, pltpu.roll(g, n - shift, 0), 0.0), None


_shift_rows.defvjp(_shift_rows_fwd, _shift_rows_bwd)


def _merge_block(gates, br_a, br_b):
    return (_sigmoid(gates[:, :D_MODEL]) * br_a + _sigmoid(gates[:, D_MODEL:]) * br_b,)


def _lnres_block(x, y, g, b):
    return (_layer_norm(ALPHA * x + y, g, b),)


def _memln_block(x, g, b):
    return (_layer_norm(x, g, b),)


def _attn_block(q, kv):
    outs = []
    for h in range(X_HEADS):
        qh = q[:, h * X_HEADDIM:(h + 1) * X_HEADDIM].astype(BF16)
        kh = kv[:, h * X_HEADDIM:(h + 1) * X_HEADDIM].astype(BF16)
        vh = kv[:, D_MODEL + h * X_HEADDIM:D_MODEL + (h + 1) * X_HEADDIM].astype(BF16)
        s = lax.dot_general(qh, kh, (((1,), (1,)), ((), ())), preferred_element_type=F32) * (X_HEADDIM ** -0.5)
        s = s - lax.stop_gradient(jnp.max(s, axis=-1, keepdims=True))
        e = jnp.exp(s)
        p = e / jnp.sum(e, axis=-1, keepdims=True)
        outs.append(jnp.dot(p.astype(BF16), vh, preferred_element_type=F32))
    return (jnp.concatenate(outs, axis=1),)


def _swiglu_block(gu):
    return (_silu(gu[:, :FFN_HIDDEN]) * gu[:, FFN_HIDDEN:],)


class _Comm:
    def __init__(self):
        self.gathers = []
        self.scatters = []

    @staticmethod
    def _rows(ref, rows):
        return ref if rows is None else ref.at[pl.ds(rows[0], rows[1])]

    def operands(self):
        ins = [a for a, _, _ in self.gathers] + [a for a, _ in self.scatters]
        shapes = []
        for a, idx, rows in self.gathers:
            blk = a.shape if idx is None else a.shape[1:]
            shapes.append(jax.ShapeDtypeStruct((N_DEV, blk[0] if rows is None else rows[1]) + tuple(blk[1:]), a.dtype))
        for a, rows in self.scatters:
            shapes.append(jax.ShapeDtypeStruct((N_DEV, a.shape[1] if rows is None else rows[1]) + tuple(a.shape[2:]),
                                               a.dtype))
        scratch = []
        for n in (len(self.gathers), len(self.scatters)):
            if n:
                scratch += [pltpu.SemaphoreType.DMA((7 * n,)), pltpu.SemaphoreType.DMA((7 * n,)),
                            pltpu.SemaphoreType.DMA((n,))]
        return ins, shapes, scratch

    def _split(self, in_refs, out_refs, sems):
        ng = len(self.gathers)
        g_sems = sems[:3] if ng else None
        s_sems = sems[3:] if ng else sems
        return in_refs[:ng], in_refs[ng:], out_refs[:ng], out_refs[ng:], g_sems, s_sems

    def _gather_copies(self, i, src_ref, out_ref, sems):
        send_sems, recv_sems, local_sems = sems
        x, y, c = lax.axis_index("x"), lax.axis_index("y"), lax.axis_index("c")
        me, sibling = (x, y, c), (x, y, 1 - c)
        chips = [(1 - x, y), (x, 1 - y), (1 - x, 1 - y)]
        _, idx, rows = self.gathers[i]
        src = self._rows(src_ref if idx is None else src_ref.at[idx], rows)

        def slot(px, py, pc):
            return out_ref.at[4 * px + 2 * py + pc]

        def copy(k, blk, to, from_src=False):
            return pltpu.make_async_remote_copy(
                src_ref=src if from_src else slot(*blk), dst_ref=slot(*blk), send_sem=send_sems.at[7 * i + k],
                recv_sem=recv_sems.at[7 * i + k], device_id=to, device_id_type=pl.DeviceIdType.MESH)

        mine = pltpu.make_async_copy(src, slot(*me), local_sems.at[i])
        first = [copy(0, me, sibling, True)] + [copy(1 + j, me, (*chip, c), True) for j, chip in enumerate(chips)]
        passed = [copy(4 + j, (*chip, c), sibling) for j, chip in enumerate(chips)]
        arrivals = [copy(1 + j, (*chip, c), me) for j, chip in enumerate(chips)]
        from_sibling = [copy(0, sibling, me)] + [copy(4 + j, (*chip, 1 - c), me) for j, chip in enumerate(chips)]
        return mine, first, passed, arrivals, from_sibling

    def _scatter_copies(self, i, src_ref, out_ref, sems):
        send_sems, recv_sems, local_sems = sems
        x, y, c = lax.axis_index("x"), lax.axis_index("y"), lax.axis_index("c")
        me = 4 * x + 2 * y + c
        rows = self.scatters[i][1]
        mine = pltpu.make_async_copy(self._rows(src_ref.at[me], rows), out_ref.at[me], local_sems.at[i])
        copies = []
        for k in range(1, N_DEV):
            px = 1 - x if k & 4 else x
            py = 1 - y if k & 2 else y
            pc = 1 - c if k & 1 else c
            copies.append(pltpu.make_async_remote_copy(
                src_ref=self._rows(src_ref.at[4 * px + 2 * py + pc], rows), dst_ref=out_ref.at[me],
                send_sem=send_sems.at[7 * i + k - 1], recv_sem=recv_sems.at[7 * i + k - 1], device_id=(px, py, pc),
                device_id_type=pl.DeviceIdType.MESH))
        return mine, copies

    def start(self, in_refs, out_refs, sems):
        g_in, s_in, g_out, s_out, g_sems, s_sems = self._split(in_refs, out_refs, sems)
        for i in range(len(self.gathers)):
            mine, first, _, _, _ = self._gather_copies(i, g_in[i], g_out[i], g_sems)
            mine.start()
            for cp in first:
                cp.start()
        for i in range(len(self.scatters)):
            mine, copies = self._scatter_copies(i, s_in[i], s_out[i], s_sems)
            mine.start()
            for cp in copies:
                cp.start()

    def finish(self, in_refs, out_refs, sems):
        g_in, s_in, g_out, s_out, g_sems, s_sems = self._split(in_refs, out_refs, sems)
        parts = [self._gather_copies(i, g_in[i], g_out[i], g_sems) for i in range(len(self.gathers))]
        for j in range(3):
            for _, _, passed, arrivals, _ in parts:
                arrivals[j].wait_recv()
                passed[j].start()
        for mine, first, passed, _, from_sibling in parts:
            for cp in from_sibling:
                cp.wait_recv()
            for cp in first + passed:
                cp.wait_send()
            mine.wait()
        for i in range(len(self.scatters)):
            mine, copies = self._scatter_copies(i, s_in[i], s_out[i], s_sems)
            for cp in copies:
                cp.wait_recv()
            for cp in copies:
                cp.wait_send()
            mine.wait()


def _params(grid):
    return pltpu.CompilerParams(dimension_semantics=("arbitrary",) * len(grid), vmem_limit_bytes=VMEM_LIMIT)


def _call(name, body, *, grid, ins, in_specs, out_shape, out_specs, scratch=(), comm=None, aliases=None):
    n_in, n_out, n_scr = len(ins), len(out_shape), len(scratch)
    aliases = aliases or {}
    if comm is None:
        outs = pl.pallas_call(body, grid=grid, in_specs=list(in_specs), out_specs=list(out_specs),
                              out_shape=list(out_shape), scratch_shapes=list(scratch), name=name,
                              input_output_aliases=aliases, compiler_params=_params(grid))(*ins)
        return list(outs), []
    c_ins, c_shapes, c_scratch = comm.operands()
    nci, nco = len(c_ins), len(c_shapes)
    anywhere = pl.BlockSpec(memory_space=pl.ANY)

    def carrier(*refs):
        main_in, comm_in = refs[:n_in], refs[n_in:n_in + nci]
        o0 = n_in + nci
        main_out, comm_out = refs[o0:o0 + n_out], refs[o0 + n_out:o0 + n_out + nco]
        s0 = o0 + n_out + nco
        main_scr, comm_scr = refs[s0:s0 + n_scr], refs[s0 + n_scr:]
        first = pl.program_id(0) == 0
        last = pl.program_id(0) == grid[0] - 1
        for ax in range(1, len(grid)):
            first = first & (pl.program_id(ax) == 0)
            last = last & (pl.program_id(ax) == grid[ax] - 1)

        @pl.when(first)
        def _():
            comm.start(comm_in, comm_out, comm_scr)

        body(*main_in, *main_out, *main_scr)

        @pl.when(last)
        def _():
            comm.finish(comm_in, comm_out, comm_scr)

    outs = pl.pallas_call(carrier, grid=grid, in_specs=list(in_specs) + [anywhere] * nci,
                          out_specs=list(out_specs) + [anywhere] * nco, out_shape=list(out_shape) + c_shapes,
                          scratch_shapes=list(scratch) + c_scratch, name=name, input_output_aliases=aliases,
                          compiler_params=_params(grid))(*ins, *c_ins)
    return list(outs[:n_out]), list(outs[n_out:])


def _comm_only(name, comm):
    c_ins, c_shapes, c_scratch = comm.operands()
    nci, nco = len(c_ins), len(c_shapes)
    anywhere = pl.BlockSpec(memory_space=pl.ANY)

    def body(*refs):
        comm.start(refs[:nci], refs[nci:nci + nco], refs[nci + nco:])
        comm.finish(refs[:nci], refs[nci:nci + nco], refs[nci + nco:])

    return list(pl.pallas_call(body, in_specs=[anywhere] * nci, out_specs=[anywhere] * nco, out_shape=c_shapes,
                               scratch_shapes=c_scratch, name=name)(*c_ins))


def _stage_fwd(name, f, grid, ins, in_specs, out_shapes, out_specs, comm=None):
    n_in = len(ins)

    def body(*refs):
        res = f(*[r[...].astype(F32) for r in refs[:n_in]])
        for o_ref, val in zip(refs[n_in:], res):
            o_ref[...] = val.astype(o_ref.dtype)

    return _call(name, body, grid=grid, ins=ins, in_specs=in_specs, out_shape=out_shapes, out_specs=out_specs, comm=comm)


def _stage_bwd(name, f, grid, ins, in_specs, cts, ct_specs, grads, comm=None, ct_product=None):
    n_in = len(ins)
    flat_cts = [c for group in cts for c in group]
    flat_ct_specs = [s for group, spec in zip(cts, ct_specs) for s in (spec,) * len(group)]
    if ct_product is not None:
        assert not cts
        flat_cts = [ct_product[0], ct_product[2]]
        flat_ct_specs = [ct_product[1], pl.BlockSpec(ct_product[2].shape, lambda *_: (0, 0))]
    n_ct = len(flat_cts)
    diff = [g[0] for g in grads]
    buffers = [(k, g[4]) for k, g in enumerate(grads) if len(g) > 4]
    n_buf = len(buffers)

    def body(*refs):
        vals = [r[...].astype(F32) for r in refs[:n_in]]
        ct_refs = refs[n_in:n_in + n_ct]
        g_refs = refs[n_in + n_ct + n_buf:]
        ct_vals, pos = [], 0
        if ct_product is not None:
            ct_vals.append(lax.dot_general(ct_refs[0][...].astype(BF16), ct_refs[1][...].astype(BF16), NT_DIMS,
                                           preferred_element_type=F32))
        for group in cts:
            acc = ct_refs[pos][...].astype(F32)
            for j in range(1, len(group)):
                acc = acc + ct_refs[pos + j][...].astype(F32)
            ct_vals.append(acc)
            pos += len(group)

        def g_fn(*dvals):
            full = list(vals)
            for i, dv in zip(diff, dvals):
                full[i] = dv
            return f(*full)

        _, vjp = jax.vjp(g_fn, *[vals[i] for i in diff])
        gvals = vjp(tuple(ct_vals))
        for gspec, g_ref, gval in zip(grads, g_refs, gvals):
            acc_axes = gspec[1]
            if not acc_axes:
                g_ref[...] = gval.astype(g_ref.dtype)
            else:
                first = pl.program_id(acc_axes[0]) == 0
                for ax in acc_axes[1:]:
                    first = first & (pl.program_id(ax) == 0)

                @pl.when(first)
                def _():
                    g_ref[...] = jnp.zeros_like(g_ref)

                g_ref[...] += gval.astype(g_ref.dtype)

    out_shapes, out_specs = [], []
    for gspec in grads:
        shape, spec = gspec[3] if len(gspec) > 3 else (ins[gspec[0]].shape, in_specs[gspec[0]])
        out_shapes.append(jax.ShapeDtypeStruct(shape, gspec[2]))
        out_specs.append(spec)
    anywhere = pl.BlockSpec(memory_space=pl.ANY)
    return _call(name, body, grid=grid, ins=list(ins) + flat_cts + [b for _, b in buffers],
                 in_specs=list(in_specs) + flat_ct_specs + [anywhere] * n_buf, out_shape=out_shapes, out_specs=out_specs,
                 comm=comm, aliases={n_in + n_ct + j: k for j, (k, _) in enumerate(buffers)})


def _pick_tile(n, candidates):
    for c in candidates:
        if n % c == 0:
            return c
    return n


def _matmul(name, a, b, *, ta=False, tb=False, add=None, extra=None, out_dtype=F32, comm=None):
    if ta:
        k_dim, m = a.shape
    else:
        m, k_dim = a.shape
    n = b.shape[0] if tb else b.shape[1]
    assert (b.shape[1] if tb else b.shape[0]) == k_dim and not (ta and tb)
    tm = _pick_tile(m, (1024, 1408, 512, 256, 128))
    tn = _pick_tile(n, (1024, 1408, 512, 256, 128))
    if ta:
        tk = _pick_tile(k_dim, (1024, 512, 256, 128))
    elif k_dim <= 2816:
        tk = k_dim
    else:
        tk = _pick_tile(k_dim, (1408, 1024, 512, 256, 128))
    nk = k_dim // tk
    grid = (m // tm, n // tn, nk)
    a_spec = pl.BlockSpec((tk, tm), lambda i, j, k: (k, i)) if ta else pl.BlockSpec((tm, tk), lambda i, j, k: (i, k))
    b_spec = pl.BlockSpec((tn, tk), lambda i, j, k: (j, k)) if tb else pl.BlockSpec((tk, tn), lambda i, j, k: (k, j))
    o_spec = pl.BlockSpec((tm, tn), lambda i, j, k: (i, j))
    dims = (((0 if ta else 1,), (1 if tb else 0,)), ((), ()))
    has_add = add is not None
    has_extra = extra is not None

    def body(*refs):
        a_ref, b_ref = refs[0], refs[1]
        add_ref = refs[2] if has_add else None
        o_ref, acc_ref = refs[-2], refs[-1]
        k = pl.program_id(2)
        part = lax.dot_general(a_ref[...].astype(BF16), b_ref[...].astype(BF16), dims, preferred_element_type=F32)

        def finish(res):
            if has_add:
                res = res + add_ref[...].astype(F32)
            if has_extra:
                a2_ref, b2_ref = refs[2 + has_add], refs[3 + has_add]
                res = res + lax.dot_general(a2_ref[...].astype(BF16), b2_ref[...].astype(BF16), NT_DIMS,
                                            preferred_element_type=F32)
            o_ref[...] = res.astype(o_ref.dtype)

        if nk == 1:
            finish(part)
        else:
            @pl.when(k == 0)
            def _():
                acc_ref[...] = part

            @pl.when((k > 0) & (k < nk - 1))
            def _():
                acc_ref[...] += part

            @pl.when(k == nk - 1)
            def _():
                finish(acc_ref[...] + part)

    ins = [a, b] + ([add] if has_add else [])
    in_specs = [a_spec, b_spec] + ([o_spec] if has_add else [])
    if has_extra:
        k2 = extra[0].shape[1]
        ins += list(extra)
        in_specs += [pl.BlockSpec((tm, k2), lambda i, j, k: (i, 0)), pl.BlockSpec((tn, k2), lambda i, j, k: (j, 0))]
    acc_shape = (tm, tn) if nk > 1 else (8, 128)
    outs, comm_outs = _call(name, body, grid=grid, ins=ins, in_specs=in_specs,
                            out_shape=[jax.ShapeDtypeStruct((m, n), out_dtype)], out_specs=[o_spec],
                            scratch=[pltpu.VMEM(acc_shape, F32)], comm=comm)
    return outs[0], comm_outs


def _matmul_lnres(name, a_fn, a_ins, a_specs, tm, b, x, g, beta, comm=None):
    m = x.shape[0]
    k_dim, n = b.shape
    n_a = len(a_ins)
    row = lambda w: pl.BlockSpec((tm, w), lambda i: (i, 0))
    whole = lambda shape: pl.BlockSpec(shape, lambda i: (0, 0))

    def body(*refs):
        b_ref, x_ref, g_ref, beta_ref = refs[n_a:n_a + 4]
        y_ref, h_ref, hb_ref = refs[-3:]
        if a_fn is None:
            a = refs[0][...].astype(BF16)
        else:
            (a,) = a_fn(*[r[...].astype(F32) for r in refs[:n_a]])
            a = a.astype(BF16)
            refs[n_a + 4][...] = a
        y = jnp.dot(a, b_ref[...].astype(BF16), preferred_element_type=F32).astype(y_ref.dtype)
        y_ref[...] = y
        (h,) = _lnres_block(x_ref[...], y.astype(F32), g_ref[...], beta_ref[...])
        h_ref[...] = h
        hb_ref[...] = h.astype(hb_ref.dtype)

    sds = jax.ShapeDtypeStruct
    a_out = ([sds((m, k_dim), BF16)], [row(k_dim)]) if a_fn is not None else ([], [])
    return _call(name, body, grid=(m // tm,), ins=list(a_ins) + [b, x, g, beta],
                 in_specs=list(a_specs) + [whole((k_dim, n)), row(n), whole((1, n)), whole((1, n))],
                 out_shape=a_out[0] + [sds((m, n), BF16), sds((m, n), F32), sds((m, n), BF16)],
                 out_specs=a_out[1] + [row(n), row(n), row(n)], comm=comm)


SSD_STATE = (SSM_GROUPS * SSM_STATE, SSM_RPG * SSM_HEADDIM)


def _ssd_fwd(xc, dt_raw, proj, dt_bias, a_log, d_skip, norm_g, nb, nc, comm=None):
    t = xc.shape[0]
    row = lambda b, c: (b * nc + c, 0)
    par = lambda shape: pl.BlockSpec(shape, lambda b, c: (0, 0))

    def body(xc_ref, dt_ref, z_ref, dtb_ref, al_ref, ds_ref, ng_ref, e_ref, y_ref, ypre_ref, prev_ref, st_ref):
        @pl.when(pl.program_id(1) == 0)
        def _():
            st_ref[...] = jnp.zeros_like(st_ref)

        prev = st_ref[...]
        prev_ref[0, 0] = prev
        y, y_pre, new_state = _ssd_forward(xc_ref[...], dt_ref[...], z_ref[...].astype(F32), prev, dtb_ref[...],
                                           al_ref[...], ds_ref[...], ng_ref[...], e_ref[...])
        y_ref[...] = y.astype(y_ref.dtype)
        ypre_ref[...] = y_pre
        st_ref[...] = new_state

    return _call(
        "ssd_fwd", body, grid=(nb, nc), ins=[xc, dt_raw, proj, dt_bias, a_log, d_skip, norm_g, _head_expander()],
        in_specs=[pl.BlockSpec((CHUNK, SSM_CONV_DIM), row), pl.BlockSpec((CHUNK, 128), row),
                  pl.BlockSpec((CHUNK, SSM_INNER), lambda b, c: (b * nc + c, 1)),
                  par((1, 128)), par((1, 128)), par((1, SSM_INNER)), par((1, SSM_INNER)), par((128, SSM_INNER))],
        out_specs=[pl.BlockSpec((CHUNK, SSM_INNER), row), pl.BlockSpec((CHUNK, SSM_INNER), row),
                   pl.BlockSpec((1, 1) + SSD_STATE, lambda b, c: (b, c, 0, 0))],
        out_shape=[jax.ShapeDtypeStruct((t, SSM_INNER), BF16), jax.ShapeDtypeStruct((t, SSM_INNER), F32),
                   jax.ShapeDtypeStruct((nb, nc) + SSD_STATE, F32)],
        scratch=[pltpu.VMEM(SSD_STATE, F32)], comm=comm)


def _ssd_bwd(xc, dt_raw, proj, prevs, y_pre, dt_bias, a_log, d_skip, norm_g, dy, dproj, nb, nc, comm=None):
    t = xc.shape[0]
    row = lambda b, c: (b * nc + (nc - 1 - c), 0)
    par = lambda shape: pl.BlockSpec(shape, lambda b, c: (0, 0))
    z_spec = pl.BlockSpec((CHUNK, SSM_INNER), lambda b, c: (b * nc + (nc - 1 - c), 1))

    def body(xc_ref, dt_ref, z_ref, prev_ref, ypre_ref, dtb_ref, al_ref, ds_ref, ng_ref, e_ref, dy_ref, _,
             dxc_ref, ddt_ref, dz_ref, ddtb_ref, dal_ref, dds_ref, dng_ref, dst_ref):
        @pl.when(pl.program_id(1) == 0)
        def _():
            dst_ref[...] = jnp.zeros_like(dst_ref)

        @pl.when((pl.program_id(0) == 0) & (pl.program_id(1) == 0))
        def _():
            ddtb_ref[...] = jnp.zeros_like(ddtb_ref)
            dal_ref[...] = jnp.zeros_like(dal_ref)
            dds_ref[...] = jnp.zeros_like(dds_ref)
            dng_ref[...] = jnp.zeros_like(dng_ref)

        dxc, ddt, dz, dprev, ddtb, dal, dds, dng = _ssd_backward(
            xc_ref[...], dt_ref[...], z_ref[...].astype(F32), prev_ref[0, 0], ypre_ref[...], dtb_ref[...], al_ref[...],
            ds_ref[...], ng_ref[...], e_ref[...], dy_ref[...].astype(F32), dst_ref[...])
        dxc_ref[...] = dxc
        ddt_ref[...] = ddt.astype(ddt_ref.dtype)
        dz_ref[...] = dz.astype(dz_ref.dtype)
        dst_ref[...] = dprev
        ddtb_ref[...] += ddtb
        dal_ref[...] += dal
        dds_ref[...] += dds
        dng_ref[...] += dng

    return _call(
        "ssd_bwd", body, grid=(nb, nc),
        ins=[xc, dt_raw, proj, prevs, y_pre, dt_bias, a_log, d_skip, norm_g, _head_expander(), dy, dproj],
        in_specs=[pl.BlockSpec((CHUNK, SSM_CONV_DIM), row), pl.BlockSpec((CHUNK, DT_LANES), row), z_spec,
                  pl.BlockSpec((1, 1) + SSD_STATE, lambda b, c: (b, nc - 1 - c, 0, 0)),
                  pl.BlockSpec((CHUNK, SSM_INNER), row),
                  par((1, 128)), par((1, 128)), par((1, SSM_INNER)), par((1, SSM_INNER)), par((128, SSM_INNER)),
                  pl.BlockSpec((CHUNK, SSM_INNER), row), pl.BlockSpec(memory_space=pl.ANY)],
        out_specs=[pl.BlockSpec((CHUNK, SSM_CONV_DIM), row), pl.BlockSpec((CHUNK, DT_LANES), row), z_spec,
                   par((1, 128)), par((1, 128)), par((1, 128)), par((1, SSM_INNER))],
        out_shape=[jax.ShapeDtypeStruct((t, SSM_CONV_DIM), F32), jax.ShapeDtypeStruct((t, DT_LANES), BF16),
                   jax.ShapeDtypeStruct(dproj.shape, dproj.dtype), jax.ShapeDtypeStruct((1, 128), F32),
                   jax.ShapeDtypeStruct((1, 128), F32), jax.ShapeDtypeStruct((1, 128), F32),
                   jax.ShapeDtypeStruct((1, SSM_INNER), F32)],
        scratch=[pltpu.VMEM(SSD_STATE, F32)], comm=comm, aliases={11: 2})


def _loss_head(y, target):
    t, d = y.shape
    tm = _pick_tile(t, (256,))
    blk = pl.BlockSpec((tm, d), lambda i: (i, 0))

    def body(y_ref, t_ref, loss_ref, dy_ref):
        err = y_ref[...] - t_ref[...]
        dy_ref[...] = err * (1.0 / d)

        @pl.when(pl.program_id(0) == 0)
        def _():
            loss_ref[...] = jnp.zeros_like(loss_ref)

        loss_ref[...] += 0.5 * jnp.sum(jnp.mean(err * err, axis=-1, keepdims=True), axis=0, keepdims=True)

    return _call("loss_head", body, grid=(t // tm,), ins=[y, target], in_specs=[blk, blk],
                 out_specs=[pl.BlockSpec((1, 1), lambda i: (0, 0)), blk],
                 out_shape=[jax.ShapeDtypeStruct((1, 1), F32), jax.ShapeDtypeStruct((t, d), F32)])[0]


def _adamw_math(g, w, m, v):
    m_new = ADAM_B1 * m + (1.0 - ADAM_B1) * g
    v_new = ADAM_B2 * v + (1.0 - ADAM_B2) * jnp.square(g)
    m_hat = m_new / (1.0 - ADAM_B1 ** ADAM_STEP)
    v_hat = v_new / (1.0 - ADAM_B2 ** ADAM_STEP)
    delta = -ADAM_LR * (m_hat / (jnp.sqrt(v_hat) + ADAM_EPS) + ADAM_WD * w)
    return delta, m_new, v_new


def _adamw_sharded(name, parts, w, m, v, comm=None):
    _, a, b = w.shape
    tr = _pick_tile(a, (128,))
    nt = a // tr
    part_specs = [pl.BlockSpec((N_DEV, tr, b),
                               (lambda l, i, _k=k: (0, jnp.where(l == _k, i, jnp.where(l > _k, nt - 1, 0)), 0)))
                  for k in range(DEPTH)]
    blk = pl.BlockSpec((1, tr, b), lambda l, i: (l, i, 0))

    def body(*refs):
        p_refs = refs[:DEPTH]
        w_ref, m_ref, v_ref, g_out, d_out, m_out, v_out = refs[DEPTH:]
        for k in range(DEPTH):
            @pl.when(pl.program_id(0) == k)
            def _(p_ref=p_refs[k]):
                g = p_ref[0].astype(F32)
                for p in range(1, N_DEV):
                    g = g + p_ref[p].astype(F32)
                delta, m_new, v_new = _adamw_math(g, w_ref[0], m_ref[0], v_ref[0])
                g_out[0] = g
                d_out[0] = delta
                m_out[0] = m_new
                v_out[0] = v_new

    return _call(name, body, grid=(DEPTH, nt), ins=list(parts) + [w, m, v], in_specs=part_specs + [blk, blk, blk],
                 out_specs=[blk] * 4, out_shape=[jax.ShapeDtypeStruct(w.shape, F32)] * 4, comm=comm)


def _adamw_small(name, g, w, m, v):
    full = pl.BlockSpec(w.shape, lambda i: (0, 0))

    def body(g_ref, w_ref, m_ref, v_ref, d_out, m_out, v_out):
        delta, m_new, v_new = _adamw_math(g_ref[...], w_ref[...], m_ref[...], v_ref[...])
        d_out[...] = delta
        m_out[...] = m_new
        v_out[...] = v_new

    return _call(name, body, grid=(1,), ins=[g, w, m, v], in_specs=[full] * 4, out_specs=[full] * 3,
                 out_shape=[jax.ShapeDtypeStruct(w.shape, F32)] * 3)[0]


def _sum_parts(name, parts):
    n_parts, rows, cols = parts.shape
    tr = _pick_tile(rows, (512, 256, 128, 64, 32, 16, 8))

    def body(p_ref, o_ref):
        acc = p_ref[0]
        for p in range(1, n_parts):
            acc = acc + p_ref[p]
        o_ref[...] = acc

    return _call(name, body, grid=(rows // tr,), ins=[parts],
                 in_specs=[pl.BlockSpec((n_parts, tr, cols), lambda i: (0, i, 0))],
                 out_specs=[pl.BlockSpec((tr, cols), lambda i: (i, 0))],
                 out_shape=[jax.ShapeDtypeStruct((rows, cols), parts.dtype)])[0][0]


W_IN_SHARD = IN_COLS // N_DEV


def _pack_w_in(gathered):
    r = gathered.shape[1]
    tr = _pick_tile(r, (128,))

    def body(g_ref, main_ref, dt_ref):
        w = jnp.concatenate([g_ref[j].astype(F32) for j in range(N_DEV)], axis=1)
        main_ref[...] = jnp.concatenate([w[:, :XBC_COL], w[:, GA_COL:], w[:, XBC_COL:DT_COL]],
                                        axis=1).astype(main_ref.dtype)
        dt_ref[...] = jnp.concatenate([w[:, DT_COL:GA_COL], jnp.zeros((tr, DT_LANES - SSM_HEADS), F32)],
                                      axis=1).astype(dt_ref.dtype)

    return _call("pack_w_in", body, grid=(r // tr,), ins=[gathered],
                 in_specs=[pl.BlockSpec((N_DEV, tr, W_IN_SHARD), lambda i: (0, i, 0))],
                 out_specs=[pl.BlockSpec((tr, P_COLS), lambda i: (i, 0)), pl.BlockSpec((tr, DT_LANES), lambda i: (i, 0))],
                 out_shape=[jax.ShapeDtypeStruct((r, P_COLS), gathered.dtype),
                            jax.ShapeDtypeStruct((r, DT_LANES), gathered.dtype)])[0]


def _unpack_w_in(main, dt):
    r = main.shape[0]
    tr = _pick_tile(r, (128,))

    def body(main_ref, dt_ref, o_ref):
        main = main_ref[...].astype(F32)
        w = jnp.concatenate([main[:, :P_GATE], main[:, P_XBC:], dt_ref[...].astype(F32)[:, :SSM_HEADS],
                             main[:, P_GATE:P_XBC]], axis=1)
        for j in range(N_DEV):
            o_ref[j] = w[:, j * W_IN_SHARD:(j + 1) * W_IN_SHARD].astype(o_ref.dtype)

    return _call("unpack_w_in", body, grid=(r // tr,), ins=[main, dt],
                 in_specs=[pl.BlockSpec((tr, P_COLS), lambda i: (i, 0)), pl.BlockSpec((tr, DT_LANES), lambda i: (i, 0))],
                 out_specs=[pl.BlockSpec((N_DEV, tr, W_IN_SHARD), lambda i: (0, i, 0))],
                 out_shape=[jax.ShapeDtypeStruct((N_DEV, r, W_IN_SHARD), main.dtype)])[0][0]


def _pad_heads(v):
    return jnp.pad(v, (0, 128 - SSM_HEADS)).reshape(1, 128)


def _run_step(x, mem, target, small, ex):
    nb, s, d = x.shape
    t = nb * s
    nc = s // CHUNK
    rows = _pick_tile(t, (256,))
    rows_wide = _pick_tile(t, (512, 256))
    tq = _pick_tile(s, (512, 256))
    vec = lambda a: a.reshape(1, -1)
    full1 = lambda shape: pl.BlockSpec(shape, lambda i: (0,) * len(shape))
    row1 = lambda tm, w: pl.BlockSpec((tm, w), lambda i: (i, 0))
    sds = jax.ShapeDtypeStruct

    def mm(call, l, a, b, **kw):
        comm = ex.before(call, l)
        out, comm_outs = _matmul(call, a, b, comm=comm, **kw)
        if comm is not None:
            ex.after(call, l, comm_outs)
        return out

    def stage_bwd(call, l, *args, **kw):
        comm = ex.before(call, l)
        outs, comm_outs = _stage_bwd(call, *args, comm=comm, **kw)
        if comm is not None:
            ex.after(call, l, comm_outs)
        return outs

    def stage_fwd(call, l, *args):
        comm = ex.before(call, l)
        outs, comm_outs = _stage_fwd(call, *args, comm=comm)
        if comm is not None:
            ex.after(call, l, comm_outs)
        return outs

    mem_specs = [row1(256, d), full1((1, d)), full1((1, d))]
    mem_ins = [mem.reshape(nb * MEM_LEN, d), vec(small["mem_ln_g"]), vec(small["mem_ln_b"])]
    (mem_n,) = stage_fwd("memln_fwd", 0, _memln_block, (nb * MEM_LEN // 256,), mem_ins, mem_specs,
                          [sds((nb * MEM_LEN, d), BF16)], [row1(256, d)])

    h = x.reshape(t, d)
    h_bf = h.astype(BF16)
    ln_specs = [row1(rows, d), row1(rows, d), full1((1, d)), full1((1, d))]
    saved = []
    for l in range(DEPTH):
        sv = {"h_bf": h_bf}
        w_p, w_dt = ex.weight("w_in", l)
        proj = mm("mm_in", l, h_bf, w_p, out_dtype=BF16)
        dt_raw = mm("mm_dt", l, h_bf, w_dt)
        sv["proj"] = proj
        sgu_ins = [proj, vec(small["sg_ln_g"][l]), vec(small["sg_ln_b"][l]), small["sg_w"][l], small["sg_b"][l].T]
        sgu_specs = [pl.BlockSpec((CHUNK, 2 * d), lambda i: (i, 0)), full1((1, d)), full1((1, d)),
                     full1((SG_GROUPS, CHUNK, CHUNK)), full1((CHUNK, SG_GROUPS))]
        (a_out,) = stage_fwd("sgu_fwd", l, _sgu_block, (t // CHUNK,), sgu_ins, sgu_specs, [sds((t, d), BF16)],
                              [row1(CHUNK, d)])
        sv["sgu"] = (sgu_ins, sgu_specs)
        sv["a_out"] = a_out
        cw = 256
        conv_ins = [proj, small["conv_w"][l], vec(small["conv_b"][l])]
        conv_specs = [pl.BlockSpec((s, cw), lambda j, b: (b, P_XBC // cw + j)),
                      pl.BlockSpec((SSM_CONV, cw), lambda j, b: (0, j)), pl.BlockSpec((1, cw), lambda j, b: (0, j))]
        conv_out_spec = pl.BlockSpec((s, cw), lambda j, b: (b, j))
        (xc,) = stage_fwd("conv_fwd", l, _conv_block, (SSM_CONV_DIM // cw, nb), conv_ins, conv_specs,
                           [sds((t, SSM_CONV_DIM), F32)], [conv_out_spec])
        sv["conv"] = (conv_ins, conv_specs, conv_out_spec)
        ssd_par = [_pad_heads(small["dt_bias"][l]), _pad_heads(small["a_log"][l]),
                   vec(jnp.repeat(small["d_skip"][l], SSM_HEADDIM)), vec(small["ssm_norm_g"][l])]
        comm = ex.before("ssd_fwd", l)
        (y_ssd, y_pre, prevs), comm_outs = _ssd_fwd(xc, dt_raw, proj, *ssd_par, nb, nc, comm=comm)
        if comm is not None:
            ex.after("ssd_fwd", l, comm_outs)
        sv["ssd"] = (xc, dt_raw, prevs, y_pre, ssd_par)
        sv["y_ssd"] = y_ssd
        br_a = mm("mm_sq", l, a_out, ex.weight("p_a", l), out_dtype=BF16)
        br_b = mm("mm_pb", l, y_ssd, ex.weight("p_b", l), out_dtype=BF16)
        merge_ins = [proj, br_a, br_b]
        merge_out_spec = row1(rows_wide, d)
        merge_specs = [pl.BlockSpec((rows_wide, 2 * d), lambda i: (i, P_GATE // (2 * d))), merge_out_spec, merge_out_spec]
        sv["merge"] = (merge_ins, merge_specs, merge_out_spec)
        ln_par = [(vec(small["ln_g"][l, k]), vec(small["ln_b"][l, k])) for k in range(3)]

        def fused(call, a_fn, a_ins, a_specs, tm, w, x_in, par):
            comm = ex.before(call, l)
            outs, comm_outs = _matmul_lnres(call, a_fn, a_ins, a_specs, tm, w, x_in, *par, comm=comm)
            if comm is not None:
                ex.after(call, l, comm_outs)
            return outs

        merged, y1, h1, h1_bf = fused("mm_mix_ln", _merge_block, merge_ins, merge_specs, rows_wide,
                                      ex.weight("w_mix_o", l), h, ln_par[0])
        sv["merged"] = merged
        sv["ln1"] = [h, y1, *ln_par[0]]
        q = mm("mm_sq", l, h1_bf, ex.weight("w_xq", l), out_dtype=BF16)
        kv = mm("mm_kv", l, mem_n, ex.weight("w_xkv", l), out_dtype=BF16)
        attn_ins = [q, kv]
        attn_out_spec = pl.BlockSpec((tq, d), lambda b, i: (b * (s // tq) + i, 0))
        attn_specs = [attn_out_spec, pl.BlockSpec((MEM_LEN, 2 * d), lambda b, i: (b, 0))]
        (o,) = stage_fwd("attn_fwd", l, _attn_block, (nb, s // tq), attn_ins, attn_specs, [sds((t, d), BF16)],
                          [attn_out_spec])
        sv["attn"] = (attn_ins, attn_specs, attn_out_spec)
        sv["o"] = o
        sv["h1_bf"] = h1_bf
        y2, h2, h2_bf = fused("mm_xo_ln", None, [o], [row1(rows_wide, d)], rows_wide, ex.weight("w_xo", l), h1, ln_par[1])
        sv["ln2"] = [h1, y2, *ln_par[1]]
        sv["h2_bf"] = h2_bf
        gu = mm("mm_ffn_in", l, h2_bf, ex.weight("w_ffn_in", l), out_dtype=BF16)
        act, y3, h3, h3_bf = fused("mm_ffn_out_ln", _swiglu_block, [gu], [row1(rows, 2 * FFN_HIDDEN)], rows,
                                   ex.weight("w_ffn_out", l), h2, ln_par[2])
        sv["gu"] = gu
        sv["act"] = act
        sv["ln3"] = [h2, y3, *ln_par[2]]
        h, h_bf = h3, h3_bf
        saved.append(sv)

    loss, dh = _loss_head(h, target.reshape(t, d))

    g_small = {n: [None] * DEPTH for n in SMALL_REP + SMALL_SH if n not in ("mem_ln_g", "mem_ln_b")}
    dmem_n = []
    ln_grads = [(0, (), F32), (1, (), BF16), (2, (0,), F32), (3, (0,), F32)]
    for l in reversed(range(DEPTH)):
        sv = saved[l]
        dln_g, dln_b = [None] * 3, [None] * 3
        dres, dy3, dln_g[2], dln_b[2] = stage_bwd("lnres_bwd", l, _lnres_block, (t // rows,), sv["ln3"], ln_specs,
                                                  [(dh,)], [row1(rows, d)], ln_grads)
        ex.grad("w_ffn_out", l, mm("mm_ffn_out_dw", l, sv["act"], dy3, ta=True, out_dtype=BF16))
        (dgu,) = stage_bwd("swiglu_bwd", l, _swiglu_block, (t // rows,), [sv["gu"]], [row1(rows, 2 * FFN_HIDDEN)],
                           [], [], [(0, (), BF16)], ct_product=(dy3, row1(rows, d), ex.weight("w_ffn_out", l)))
        ex.grad("w_ffn_in", l, mm("mm_ffn_in_dw", l, sv["h2_bf"], dgu, ta=True, out_dtype=BF16))
        dh2 = mm("mm_ffn_in_dx", l, dgu, ex.weight("w_ffn_in", l), tb=True, add=dres)
        dres, dy2, dln_g[1], dln_b[1] = stage_bwd("lnres_bwd", l, _lnres_block, (t // rows,), sv["ln2"], ln_specs,
                                                  [(dh2,)], [row1(rows, d)], ln_grads)
        ex.grad("w_xo", l, mm("mm_sq_dw", l, sv["o"], dy2, ta=True, out_dtype=BF16))
        do = mm("mm_sq_dx", l, dy2, ex.weight("w_xo", l), tb=True, out_dtype=BF16)
        attn_ins, attn_specs, attn_out_spec = sv["attn"]
        dq, dkv = stage_bwd("attn_bwd", l, _attn_block, (nb, s // tq), attn_ins, attn_specs, [(do,)], [attn_out_spec],
                            [(0, (), BF16), (1, (1,), F32)])
        ex.grad("w_xq", l, mm("mm_sq_dw", l, sv["h1_bf"], dq, ta=True, out_dtype=BF16))
        dh1 = mm("mm_sq_dx", l, dq, ex.weight("w_xq", l), tb=True, add=dres)
        ex.grad("w_xkv", l, mm("mm_kv_dw", l, mem_n, dkv, ta=True, out_dtype=BF16))
        dmem_n.append(mm("mm_kv_dx", l, dkv, ex.weight("w_xkv", l), tb=True))
        dres, dy1, dln_g[0], dln_b[0] = stage_bwd("lnres_bwd", l, _lnres_block, (t // rows,), sv["ln1"], ln_specs,
                                                  [(dh1,)], [row1(rows, d)], ln_grads)
        g_small["ln_g"][l] = jnp.concatenate(dln_g, axis=0)
        g_small["ln_b"][l] = jnp.concatenate(dln_b, axis=0)
        ex.grad("w_mix_o", l, mm("mm_sq_dw", l, sv["merged"], dy1, ta=True, out_dtype=BF16))
        merge_ins, merge_specs, merge_out_spec = sv["merge"]
        dproj, dbr_a, dbr_b = stage_bwd("merge_bwd", l, _merge_block, (t // rows_wide,), merge_ins, merge_specs, [], [],
                                        [(0, (), BF16, ((t, P_COLS), merge_specs[0])), (1, (), BF16), (2, (), BF16)],
                                        ct_product=(dy1, merge_out_spec, ex.weight("w_mix_o", l)))
        ex.grad("p_a", l, mm("mm_sq_dw", l, sv["a_out"], dbr_a, ta=True, out_dtype=BF16))
        da_out = mm("mm_sq_dx", l, dbr_a, ex.weight("p_a", l), tb=True, out_dtype=BF16)
        ex.grad("p_b", l, mm("mm_pb_dw", l, sv["y_ssd"], dbr_b, ta=True, out_dtype=BF16))
        dy_ssd = mm("mm_pb_dx", l, dbr_b, ex.weight("p_b", l), tb=True, out_dtype=BF16)
        sgu_ins, sgu_specs = sv["sgu"]
        dproj, dsg_ln_g, dsg_ln_b, dsg_w, dsg_b = stage_bwd(
            "sgu_bwd", l, _sgu_block, (t // CHUNK,), sgu_ins, sgu_specs, [(da_out,)], [row1(CHUNK, d)],
            [(0, (), BF16, ((t, P_COLS), sgu_specs[0]), dproj), (1, (0,), F32), (2, (0,), F32), (3, (0,), F32),
             (4, (0,), F32)])
        g_small["sg_ln_g"][l], g_small["sg_ln_b"][l], g_small["sg_w"][l], g_small["sg_b"][l] = (
            dsg_ln_g[0], dsg_ln_b[0], dsg_w, dsg_b.T)
        xc, dt_raw, prevs, y_pre, ssd_par = sv["ssd"]
        comm = ex.before("ssd_bwd", l)
        (dxc, ddt, dproj, ddtb, dal, dds, dng), comm_outs = _ssd_bwd(xc, dt_raw, sv["proj"], prevs, y_pre, *ssd_par,
                                                                     dy_ssd, dproj, nb, nc, comm=comm)
        if comm is not None:
            ex.after("ssd_bwd", l, comm_outs)
        g_small["dt_bias"][l], g_small["a_log"][l], g_small["d_skip"][l] = (
            ddtb[0, :SSM_HEADS], dal[0, :SSM_HEADS], dds[0, :SSM_HEADS])
        g_small["ssm_norm_g"][l] = dng[0]
        conv_ins, conv_specs, conv_out_spec = sv["conv"]
        dproj, dconv_w, dconv_b = stage_bwd("conv_bwd", l, _conv_block, (SSM_CONV_DIM // 256, nb), conv_ins, conv_specs,
                                            [(dxc,)], [conv_out_spec],
                                            [(0, (), BF16, ((t, P_COLS), conv_specs[0]), dproj), (1, (1,), F32),
                                             (2, (1,), F32)])
        g_small["conv_w"][l], g_small["conv_b"][l] = dconv_w, dconv_b[0]
        if l == 0:
            dmg, dmb = stage_bwd("memln_bwd", l, _memln_block, (nb * MEM_LEN // 256,), mem_ins, mem_specs,
                                 [tuple(dmem_n)], [row1(256, d)], [(1, (0,), F32), (2, (0,), F32)])
            done = {n: jnp.stack(g, axis=0) for n, g in g_small.items()}
            done["mem_ln_g"], done["mem_ln_b"] = dmg[0], dmb[0]
            ex.small_grads(done)
        w_p, w_dt = ex.weight("w_in", l)
        g_dt = mm("mm_dt_dw", l, sv["h_bf"], ddt, ta=True, out_dtype=BF16)
        ex.grad("w_in", l, _unpack_w_in(mm("mm_in_dw", l, sv["h_bf"], dproj, ta=True, out_dtype=BF16), g_dt))
        dh = mm("mm_in_dx", l, dproj, w_p, tb=True, add=dres, extra=(ddt, w_dt))

    return loss, dh.reshape(nb, s, d)


def _pack_flat(arrays, rows):
    flat = jnp.concatenate([a.reshape(-1) for a in arrays])
    return jnp.pad(flat, (0, rows * 128 - flat.shape[0])).reshape(rows, 128)


def _unpack_flat(packed, shapes):
    lead = packed.shape[:-2]
    flat = packed.reshape(lead + (-1,))
    out, pos = [], 0
    for shape in shapes:
        n = math.prod(shape)
        out.append(flat[..., pos:pos + n].reshape(lead + tuple(shape)))
        pos += n
    return out


def _small_rows(n_elems):
    return -(-n_elems // (128 * SMALL_ROW_TILE)) * SMALL_ROW_TILE


def _from_shards(name, gathered):
    _, a, b = gathered.shape
    if name == "w_in":
        return tuple(_pack_w_in(gathered))
    if name in BIG_COL_SHARDED:
        return _join_columns(gathered)
    return gathered.reshape(N_DEV * a, b)


def _to_shards(name, g):
    if name == "w_in":
        return g
    if name in BIG_COL_SHARDED:
        return _split_columns(g)
    a, b = g.shape
    return g.reshape(N_DEV, a // N_DEV, b)


def _join_columns(gathered):
    _, r, b = gathered.shape
    tr = _pick_tile(r, (128,))

    def body(g_ref, o_ref):
        o_ref[...] = jnp.concatenate([g_ref[j].astype(F32) for j in range(N_DEV)], axis=1).astype(o_ref.dtype)

    return _call("join_columns", body, grid=(r // tr,), ins=[gathered],
                 in_specs=[pl.BlockSpec((N_DEV, tr, b), lambda i: (0, i, 0))],
                 out_specs=[pl.BlockSpec((tr, N_DEV * b), lambda i: (i, 0))],
                 out_shape=[jax.ShapeDtypeStruct((r, N_DEV * b), gathered.dtype)])[0][0]


def _split_columns(full):
    r, nb = full.shape
    b = nb // N_DEV
    tr = _pick_tile(r, (128,))

    def body(f_ref, o_ref):
        w = f_ref[...].astype(F32)
        for j in range(N_DEV):
            o_ref[j] = w[:, j * b:(j + 1) * b].astype(o_ref.dtype)

    return _call("split_columns", body, grid=(r // tr,), ins=[full],
                 in_specs=[pl.BlockSpec((tr, nb), lambda i: (i, 0))],
                 out_specs=[pl.BlockSpec((N_DEV, tr, b), lambda i: (0, i, 0))],
                 out_shape=[jax.ShapeDtypeStruct((N_DEV, r, b), full.dtype)])[0][0]


class _MeshExchange:
    def __init__(self, shards_bf16, first):
        self.shards = shards_bf16
        self.full = dict(first)
        self.pieces = {}
        self.grads = {}
        self.to_send = {}
        self.received = {}
        self.small = None
        self.small_gathered = None

    def weight(self, name, l):
        if (name, l) not in self.full:
            got = jnp.concatenate([self.pieces[(name, l, q)] for q in range(W_IN_PIECES)], axis=1)
            self.full[(name, l)] = _from_shards(name, got)
        return self.full[(name, l)]

    def grad(self, name, l, g):
        self.grads[(name, l)] = g

    def small_grads(self, done):
        self.small = done

    def partial_sums(self, name, l):
        if (name, l, None) in self.received:
            return self.received[(name, l, None)]
        return jnp.concatenate([self.received[(name, l, q)] for q in range(W_IN_PIECES)], axis=1)

    def _slices(self, name, l, piece):
        n_rows = D_MODEL // W_IN_PIECES
        key, rows = (name, l), None if piece is None else (piece * n_rows, n_rows)
        if key not in self.to_send:
            self.to_send[key] = _to_shards(name, self.grads[key])
        return self.to_send[key], rows

    def before(self, call, l):
        comm = _Comm()
        for name, layer, piece in GATHER_PLAN.get((call, l), ()):
            n_rows = D_MODEL // W_IN_PIECES
            comm.gathers.append((self.shards[name], layer, None if piece is None else (piece * n_rows, n_rows)))
        if (call, l) == SMALL_GATHER_CALL:
            names = SMALL_REP + SMALL_SH
            rows = _small_rows(sum(math.prod(self.small[n].shape) for n in names))
            comm.gathers.append((_pack_flat([self.small[n] for n in names], rows), None, None))
        for name, layer, piece in SCATTER_PLAN.get((call, l), ()):
            comm.scatters.append(self._slices(name, layer, piece))
        return comm if comm.gathers or comm.scatters else None

    def after(self, call, l, outs):
        gathers = list(GATHER_PLAN.get((call, l), ()))
        for (name, layer, piece), out in zip(gathers, outs):
            if piece is None:
                self.full[(name, layer)] = _from_shards(name, out)
            else:
                self.pieces[(name, layer, piece)] = out
        outs = outs[len(gathers):]
        if (call, l) == SMALL_GATHER_CALL:
            self.small_gathered = outs[0]
            outs = outs[1:]
        for item, out in zip(SCATTER_PLAN.get((call, l), ()), outs):
            self.received[item] = out


def kernel(x, mem, mem_ln_g, mem_ln_b, w_in, sg_ln_g, sg_ln_b, sg_w, sg_b, conv_w, conv_b, dt_bias, a_log, d_skip, ssm_norm_g, p_a, p_b, w_mix_o, w_xq, w_xkv, w_xo, w_ffn_in, w_ffn_out, ln_g, ln_b, loss_target, m_mem_ln_g, m_mem_ln_b, m_w_in, m_sg_ln_g, m_sg_ln_b, m_sg_w, m_sg_b, m_conv_w, m_conv_b, m_dt_bias, m_a_log, m_d_skip, m_ssm_norm_g, m_p_a, m_p_b, m_w_mix_o, m_w_xq, m_w_xkv, m_w_xo, m_w_ffn_in, m_w_ffn_out, m_ln_g, m_ln_b, v_mem_ln_g, v_mem_ln_b, v_w_in, v_sg_ln_g, v_sg_ln_b, v_sg_w, v_sg_b, v_conv_w, v_conv_b, v_dt_bias, v_a_log, v_d_skip, v_ssm_norm_g, v_p_a, v_p_b, v_w_mix_o, v_w_xq, v_w_xkv, v_w_xo, v_w_ffn_in, v_w_ffn_out, v_ln_g, v_ln_b):
    args = dict(locals())
    w = {n: args[n] for n in WEIGHTS}
    m = {n: args["m_" + n] for n in WEIGHTS}
    v = {n: args["v_" + n] for n in WEIGHTS}
    me = 4 * lax.axis_index("x") + 2 * lax.axis_index("y") + lax.axis_index("c")

    shards = {n: w[n].astype(BF16) for n in BIG}
    sh_shapes = [w[n].shape for n in SMALL_SH]
    first = _Comm()
    first.gathers.append((shards["w_in"], 0, None))
    first.gathers.append((_pack_flat([w[n] for n in SMALL_SH], _small_rows(sum(math.prod(s) for s in sh_shapes))), None,
                          None))
    w_in0, small_sh = _comm_only("gather_first", first)
    small = {n: w[n] for n in SMALL_REP}
    for n, sh in zip(SMALL_SH, _unpack_flat(small_sh, sh_shapes)):
        small[n] = sh.transpose(1, 2, 0, 3).reshape(sh.shape[1], sh.shape[2], N_DEV * sh.shape[3])

    ex = _MeshExchange(shards, {("w_in", 0): _from_shards("w_in", w_in0)})
    loss, grad_x = _run_step(x, mem, loss_target, small, ex)
    loss = lax.psum(loss[0, 0], ("x", "y", "c"))

    out = {}
    for n in BIG[1:] + BIG[:1]:
        comm = ex.before("adamw_" + n, 0)
        out[n], comm_outs = _adamw_sharded("adamw_" + n, [ex.partial_sums(n, l) for l in range(DEPTH)], w[n], m[n], v[n],
                                           comm=comm)
        if comm is not None:
            ex.after("adamw_" + n, 0, comm_outs)
    names = SMALL_REP + SMALL_SH
    g_small = dict(zip(names, _unpack_flat(_sum_parts("sum_small_grads", ex.small_gathered),
                                           [ex.small[n].shape for n in names])))
    for n in names:
        g = g_small[n]
        if n in SMALL_SH:
            width = w[n].shape[-1]
            g = lax.dynamic_slice_in_dim(g, me * width, width, axis=-1)
        two_d = (-1, w[n].shape[-1])
        res = _adamw_small("adamw_" + n, g.reshape(two_d), w[n].reshape(two_d), m[n].reshape(two_d), v[n].reshape(two_d))
        out[n] = [g] + [r.reshape(w[n].shape) for r in res]

    results = []
    for k in range(4):
        results.extend(out[n][k] for n in WEIGHTS)
    return (loss, grad_x, *results)
```

```python
import functools
import math

import jax
import jax.numpy as jnp
from jax import lax
from jax.experimental import pallas as pl
from jax.experimental.pallas import tpu as pltpu

F32 = jnp.float32
BF16 = jnp.bfloat16
HIGHEST = lax.Precision.HIGHEST

N_DEV = 8
D_MODEL = 1024
DEPTH = 2
MEM_LEN = 256
CHUNK = 128
SG_GROUPS = 8
SSM_INNER = 2048
SSM_HEADDIM = 64
SSM_HEADS = 32
SSM_STATE = 128
SSM_GROUPS = 4
SSM_RPG = 8
SSM_CONV = 4
SSM_CONV_DIM = 3072
X_HEADS = 4
X_HEADDIM = 256
FFN_HIDDEN = 2816
ALPHA = float((2 * DEPTH) ** 0.25)
LN_EPS = 1e-5
RMS_EPS = 1e-5
XBC_COL = 4096
DT_COL = 7168
GA_COL = 7200
IN_COLS = 9248
P_GATE = 4096
P_XBC = 6144
P_COLS = 9216
DT_LANES = 128

ADAM_LR = 0.001
ADAM_B1 = 0.9
ADAM_B2 = 0.999
ADAM_EPS = 1e-08
ADAM_WD = 0.01
ADAM_STEP = 10

VMEM_LIMIT = 48 * 1024 * 1024
SMALL_ROW_TILE = 256

BIG = ("w_in", "p_a", "p_b", "w_mix_o", "w_xq", "w_xkv", "w_xo", "w_ffn_in", "w_ffn_out")
BIG_COL_SHARDED = ("w_in", "w_xkv", "w_ffn_in")
SMALL_REP = ("mem_ln_g", "mem_ln_b", "sg_ln_g", "sg_ln_b", "sg_w", "sg_b", "conv_b", "dt_bias", "a_log", "d_skip",
             "ssm_norm_g")
SMALL_SH = ("conv_w", "ln_g", "ln_b")
WEIGHTS = ("mem_ln_g", "mem_ln_b", "w_in", "sg_ln_g", "sg_ln_b", "sg_w", "sg_b", "conv_w", "conv_b", "dt_bias", "a_log",
           "d_skip", "ssm_norm_g", "p_a", "p_b", "w_mix_o", "w_xq", "w_xkv", "w_xo", "w_ffn_in", "w_ffn_out", "ln_g", "ln_b")

W_IN_PIECES = 8
GATHER_PLAN = {("sgu_fwd", 0): [("w_in", 1, 0), ("w_in", 1, 1)],
               ("conv_fwd", 0): [("w_in", 1, 2), ("w_in", 1, 3), ("w_in", 1, 4)],
               ("mm_ffn_in", 0): [("w_in", 1, 5), ("w_ffn_out", 0, None)],
               ("mm_ffn_out_ln", 0): [("w_in", 1, 6), ("w_in", 1, 7)],
               ("conv_fwd", 1): [("w_ffn_out", 1, None)]}
SCATTER_PLAN = {("mm_ffn_in_dw", 0): [("w_in", 1, 4), ("w_in", 1, 5)], ("mm_ffn_in_dx", 0): [("w_in", 1, 6), ("w_in", 1, 7)],
                ("adamw_w_ffn_in", 0): [("w_in", 0, 4)], ("adamw_w_xkv", 0): [("w_in", 0, 5)],
                ("adamw_w_ffn_out", 0): [("w_in", 0, 6)], ("adamw_p_b", 0): [("w_in", 0, 7)]}
for _l in range(DEPTH):
    GATHER_PLAN[("mm_in", _l)] = [(n, _l, None) for n in ("p_a", "p_b", "w_mix_o", "w_xq", "w_xkv", "w_xo")]
    GATHER_PLAN[("ssd_fwd", _l)] = [("w_ffn_in", _l, None)]
    SCATTER_PLAN[("swiglu_bwd", _l)] = [("w_ffn_out", _l, None)]
    SCATTER_PLAN[("sgu_bwd", _l)] = [("w_mix_o", _l, None), ("p_a", _l, None), ("w_xo", _l, None)]
    SCATTER_PLAN[("ssd_bwd", _l)] = [("w_ffn_in", _l, None), ("w_xkv", _l, None), ("w_xq", _l, None)]
    SCATTER_PLAN[("conv_bwd", _l)] = [("p_b", _l, None)]
    SCATTER_PLAN[("mm_in_dx", _l)] = [("w_in", _l, q) for q in range(4)]
SMALL_GATHER_CALL = ("mm_in_dw", 0)


def _layer_norm(x, g, b):
    mu = jnp.mean(x, axis=-1, keepdims=True)
    xc = x - mu
    var = jnp.mean(xc * xc, axis=-1, keepdims=True)
    return xc * lax.rsqrt(var + LN_EPS) * g + b


def _gelu(x):
    return 0.5 * x * (1.0 + lax.erf(x * (1.0 / math.sqrt(2.0))))


def _sigmoid(x):
    return 0.5 * jnp.tanh(0.5 * x) + 0.5


def _silu(x):
    return x * _sigmoid(x)


def _softplus(x):
    return jnp.maximum(x, 0.0) + jnp.log1p(jnp.exp(-jnp.abs(x)))


def _causal_mask():
    r = lax.broadcasted_iota(jnp.int32, (CHUNK, CHUNK), 0)
    c = lax.broadcasted_iota(jnp.int32, (CHUNK, CHUNK), 1)
    return r >= c


def _sgu_block(uv, ln_g, ln_b, w, sb):
    gu = _gelu(uv[:, :D_MODEL])
    vn = _layer_norm(_gelu(uv[:, D_MODEL:]), ln_g, ln_b)
    causal = _causal_mask()
    width = D_MODEL // SG_GROUPS
    outs = []
    for g in range(SG_GROUPS):
        wg = jnp.where(causal, w[g], 0.0).astype(BF16)
        mixed = jnp.dot(wg, vn[:, g * width:(g + 1) * width].astype(BF16), preferred_element_type=F32)
        outs.append(mixed + sb[:, g:g + 1])
    return (gu * jnp.concatenate(outs, axis=1),)


GROUP_W = SSM_RPG * SSM_HEADDIM
NT_DIMS = (((1,), (1,)), ((), ()))
TN_DIMS = (((0,), (0,)), ((), ()))


def _mxu(a, b, dims=(((1,), (0,)), ((), ()))):
    return lax.dot_general(a.astype(BF16), b.astype(BF16), dims, preferred_element_type=F32)


def _head_expander():
    return (jnp.arange(SSM_INNER)[None, :] // SSM_HEADDIM == jnp.arange(128)[:, None]).astype(BF16)


def _bf16_terms(x, n):
    terms = []
    for _ in range(n):
        t = x.astype(BF16)
        terms.append(t)
        x = x - t.astype(F32)
    return terms


def _expand_heads(q, e):
    return sum(jnp.dot(t, e, preferred_element_type=F32) for t in _bf16_terms(q, 2))


def _reduce_heads(v, e):
    return sum(lax.dot_general(t, e, NT_DIMS, preferred_element_type=F32) for t in _bf16_terms(v, 2))


def _reduce_heads_of_column_sums(v, e):
    sums = jnp.broadcast_to(jnp.sum(v, axis=0, keepdims=True), (8, v.shape[1]))
    return _reduce_heads(sums, e)[0:1, :]


def _ssd_common(xc, dtraw, dt_bias, a_log, e):
    xs = xc[:, :SSM_INNER]
    pre = dtraw + dt_bias
    dt = _softplus(pre)
    a = -jnp.exp(a_log)
    r_i = lax.broadcasted_iota(jnp.int32, (CHUNK, CHUNK), 0)
    c_i = lax.broadcasted_iota(jnp.int32, (CHUNK, CHUNK), 1)
    tril = jnp.where(r_i >= c_i, 1.0, 0.0).astype(F32)
    cs = jnp.dot(tril, dt * a, precision=HIGHEST, preferred_element_type=F32)
    cs_last = cs[CHUNK - 1:CHUNK, :]
    decay_in = jnp.exp(cs)
    decay_st = jnp.exp(cs_last - cs)
    dt_x = _expand_heads(dt, e)
    w_st_x = _expand_heads(dt * decay_st, e)
    decay_in_x = _expand_heads(decay_in, e)
    return dict(xs=xs, pre=pre, dt=dt, a=a, lower=r_i >= c_i, upper=c_i >= r_i, cs=cs, cs_t=cs.T, decay_in=decay_in,
                decay_st=decay_st, chunk_decay=jnp.exp(cs_last), dt_x=dt_x, w_st_x=w_st_x, decay_in_x=decay_in_x,
                chunk_decay_x=decay_in_x[CHUNK - 1:CHUNK, :], xdt=xs * dt_x, x_st=(xs * w_st_x).astype(BF16),
                low=lax.broadcasted_iota(jnp.int32, (CHUNK, 128), 1) < SSM_HEADDIM)


def _pair_decay(c, h):
    return jnp.exp(jnp.where(c["lower"], c["cs"][:, h:h + 1] - c["cs_t"][h:h + 1, :], -1e30))


def _pair_decay_t(c, h):
    return jnp.exp(jnp.where(c["upper"], c["cs_t"][h:h + 1, :] - c["cs"][:, h:h + 1], -1e30))


def _ssd_forward(xc, dtraw, z, prev, dt_bias, a_log, d_skip_x, norm_g, e):
    c = _ssd_common(xc, dtraw, dt_bias, a_log, e)
    y_groups, new_states = [], []
    for g in range(SSM_GROUPS):
        lanes = slice(g * GROUP_W, (g + 1) * GROUP_W)
        bg = xc[:, SSM_INNER + g * SSM_STATE:SSM_INNER + (g + 1) * SSM_STATE]
        cg = xc[:, SSM_INNER + (SSM_GROUPS + g) * SSM_STATE:SSM_INNER + (SSM_GROUPS + g + 1) * SSM_STATE].astype(BF16)
        pg = prev[g * SSM_STATE:(g + 1) * SSM_STATE, :]
        cb = _mxu(cg, bg, NT_DIMS)
        y_in = _mxu(cg, pg) * c["decay_in_x"][:, lanes]
        new_states.append(pg * c["chunk_decay_x"][:, lanes] + _mxu(bg.T, c["x_st"][:, lanes]))
        pairs = []
        for j in range(SSM_RPG // 2):
            h0 = g * SSM_RPG + 2 * j
            xp = c["xdt"][:, 128 * (h0 // 2):128 * (h0 // 2 + 1)]
            pairs.append(_mxu(cb * _pair_decay(c, h0), jnp.where(c["low"], xp, 0.0))
                         + _mxu(cb * _pair_decay(c, h0 + 1), jnp.where(c["low"], 0.0, xp)))
        y_groups.append(jnp.concatenate(pairs, axis=1) + y_in)
    y_pre = jnp.concatenate(y_groups, axis=1) + c["xs"] * d_skip_x
    gated = y_pre * _silu(z)
    normed = [gated[:, g * GROUP_W:(g + 1) * GROUP_W] for g in range(SSM_GROUPS)]
    normed = [yg * lax.rsqrt(jnp.mean(yg * yg, axis=-1, keepdims=True) + RMS_EPS) for yg in normed]
    return jnp.concatenate(normed, axis=1) * norm_g, y_pre, jnp.concatenate(new_states, axis=0)


def _ssd_backward(xc, dtraw, z, prev, y_pre, dt_bias, a_log, d_skip_x, norm_g, e, dout, dnew):
    c = _ssd_common(xc, dtraw, dt_bias, a_log, e)
    xs = c["xs"]
    sig = _sigmoid(z)
    silu_z = z * sig
    gated = y_pre * silu_z
    d_gated, normed = [], []
    for g in range(SSM_GROUPS):
        lanes = slice(g * GROUP_W, (g + 1) * GROUP_W)
        yg = gated[:, lanes]
        r = lax.rsqrt(jnp.mean(yg * yg, axis=-1, keepdims=True) + RMS_EPS)
        n = yg * r
        gh = dout[:, lanes] * norm_g[:, lanes]
        d_gated.append(r * (gh - n * jnp.mean(gh * n, axis=-1, keepdims=True)))
        normed.append(n)
    d_gated = jnp.concatenate(d_gated, axis=1)
    dnorm_g = jnp.sum(dout * jnp.concatenate(normed, axis=1), axis=0, keepdims=True)
    dy = d_gated * silu_z
    dz = d_gated * y_pre * (sig * (1.0 + z * (1.0 - sig)))
    dxs = dy * d_skip_x
    dd_skip = _reduce_heads_of_column_sums(dy * xs, e)

    lane = lax.broadcasted_iota(jnp.int32, (CHUNK, 128), 1)
    sub = lax.broadcasted_iota(jnp.int32, (8, 128), 0)
    dcs_neg = jnp.zeros((CHUNK, 128), F32)
    row_slabs = []
    dxdt, dx_st, d_decay_in_x, dprev, d_chunk_decay_x, db_all, dc_all = [], [], [], [], [], [], []
    for g in range(SSM_GROUPS):
        lanes = slice(g * GROUP_W, (g + 1) * GROUP_W)
        bg = xc[:, SSM_INNER + g * SSM_STATE:SSM_INNER + (g + 1) * SSM_STATE].astype(BF16)
        cg_f = xc[:, SSM_INNER + (SSM_GROUPS + g) * SSM_STATE:SSM_INNER + (SSM_GROUPS + g + 1) * SSM_STATE]
        cg = cg_f.astype(BF16)
        pg = prev[g * SSM_STATE:(g + 1) * SSM_STATE, :]
        dng = dnew[g * SSM_STATE:(g + 1) * SSM_STATE, :]
        dy_g = dy[:, lanes]
        cb_t = _mxu(bg, cg, NT_DIMS)
        t1 = (dy_g * c["decay_in_x"][:, lanes]).astype(BF16)
        d_decay_in_x.append(dy_g * _mxu(cg, pg))
        dc = _mxu(t1, pg, NT_DIMS)
        dprev.append(_mxu(cg_f.T, t1) + dng * c["chunk_decay_x"][:, lanes])
        d_chunk_decay_x.append(dng * pg)
        db = _mxu(c["x_st"][:, lanes], dng, NT_DIMS)
        dx_st.append(_mxu(bg, dng))
        dcb_t = jnp.zeros((CHUNK, CHUNK), F32)
        rows = []
        for j in range(SSM_RPG // 2):
            h0 = g * SSM_RPG + 2 * j
            blk = slice(128 * (h0 // 2), 128 * (h0 // 2 + 1))
            xp = c["xdt"][:, blk]
            dyp = dy[:, blk].astype(BF16)
            pair_dx = []
            for k, xk in enumerate((jnp.where(c["low"], xp, 0.0), jnp.where(c["low"], 0.0, xp))):
                dec_t = _pair_decay_t(c, h0 + k)
                pair_dx.append(_mxu(cb_t * dec_t, dyp))
                dml_t = _mxu(xk, dyp, NT_DIMS) * dec_t
                dcb_t = dcb_t + dml_t
                dseg_t = dml_t * cb_t
                dcs_neg = dcs_neg + jnp.where(lane == h0 + k, jnp.sum(dseg_t, axis=-1, keepdims=True), 0.0)
                rows.append(jnp.sum(dseg_t, axis=0, keepdims=True))
            dxdt.append(jnp.where(c["low"], pair_dx[0], pair_dx[1]))
        slab = jnp.zeros((8, 128), F32)
        for r in range(SSM_RPG):
            slab = slab + jnp.where(sub == r, rows[r], 0.0)
        row_slabs.append(slab)
        dc_all.append(dc + _mxu(dcb_t.T, bg))
        db_all.append(db + _mxu(dcb_t, cg))
    dxdt = jnp.concatenate(dxdt, axis=1)
    dx_st = jnp.concatenate(dx_st, axis=1)
    by_head = jnp.concatenate(row_slabs + [jnp.zeros((CHUNK - SSM_HEADS, 128), F32)], axis=0)
    dcs = by_head.T - dcs_neg
    dxs = dxs + dxdt * c["dt_x"] + dx_st * c["w_st_x"]
    ddt = _reduce_heads(dxdt * xs, e)
    dw_st = _reduce_heads(dx_st * xs, e)
    dcs = dcs + _reduce_heads(jnp.concatenate(d_decay_in_x, axis=1), e) * c["decay_in"]
    ddt = ddt + dw_st * c["decay_st"]
    d_log_st = dw_st * c["dt"] * c["decay_st"]
    dcs = dcs - d_log_st
    d_chunk_decay = _reduce_heads_of_column_sums(jnp.concatenate(d_chunk_decay_x, axis=1), e)
    dcs_last = jnp.sum(d_log_st, axis=0, keepdims=True) + d_chunk_decay * c["chunk_decay"]
    row = lax.broadcasted_iota(jnp.int32, (CHUNK, 128), 0)
    dcs = dcs + jnp.where(row == CHUNK - 1, dcs_last, 0.0)
    triu = jnp.where(c["upper"], 1.0, 0.0).astype(F32)
    dda = jnp.dot(triu, dcs, precision=HIGHEST, preferred_element_type=F32)
    ddt = ddt + dda * c["a"]
    da_log = jnp.sum(dda * c["dt"], axis=0, keepdims=True) * c["a"]
    dpre = ddt * _sigmoid(c["pre"])
    dxc = jnp.concatenate([dxs] + db_all + dc_all, axis=1)
    return (dxc, dpre, dz, jnp.concatenate(dprev, axis=0), jnp.sum(dpre, axis=0, keepdims=True), da_log, dd_skip,
            dnorm_g)


def _conv_block(x, w, b):
    rows = lax.broadcasted_iota(jnp.int32, x.shape, 0)
    acc = x * w[SSM_CONV - 1:SSM_CONV, :] + b
    for k in range(SSM_CONV - 1):
        shift = SSM_CONV - 1 - k
        acc = acc + _shift_rows(x, rows, shift) * w[k:k + 1, :]
    return (_silu(acc),)


@functools.partial(jax.custom_vjp, nondiff_argnums=(2,))
def _shift_rows(x, rows, shift):
    return jnp.where(rows >= shift, pltpu.roll(x, shift, 0), 0.0)


def _shift_rows_fwd(x, rows, shift):
    return _shift_rows(x, rows, shift), rows


def _shift_rows_bwd(shift, rows, g):
    n = g.shape[0]
    return jnp.where(rows < n - shift, pltpu.roll(g, n - shift, 0), 0.0), None


_shift_rows.defvjp(_shift_rows_fwd, _shift_rows_bwd)


def _merge_block(gates, br_a, br_b):
    return (_sigmoid(gates[:, :D_MODEL]) * br_a + _sigmoid(gates[:, D_MODEL:]) * br_b,)


def _lnres_block(x, y, g, b):
    return (_layer_norm(ALPHA * x + y, g, b),)


def _memln_block(x, g, b):
    return (_layer_norm(x, g, b),)


def _attn_block(q, kv):
    outs = []
    for h in range(X_HEADS):
        qh = q[:, h * X_HEADDIM:(h + 1) * X_HEADDIM].astype(BF16)
        kh = kv[:, h * X_HEADDIM:(h + 1) * X_HEADDIM].astype(BF16)
        vh = kv[:, D_MODEL + h * X_HEADDIM:D_MODEL + (h + 1) * X_HEADDIM].astype(BF16)
        s = lax.dot_general(qh, kh, (((1,), (1,)), ((), ())), preferred_element_type=F32) * (X_HEADDIM ** -0.5)
        s = s - lax.stop_gradient(jnp.max(s, axis=-1, keepdims=True))
        e = jnp.exp(s)
        p = e / jnp.sum(e, axis=-1, keepdims=True)
        outs.append(jnp.dot(p.astype(BF16), vh, preferred_element_type=F32))
    return (jnp.concatenate(outs, axis=1),)


def _swiglu_block(gu):
    return (_silu(gu[:, :FFN_HIDDEN]) * gu[:, FFN_HIDDEN:],)


class _Comm:
    def __init__(self):
        self.gathers = []
        self.scatters = []

    @staticmethod
    def _rows(ref, rows):
        return ref if rows is None else ref.at[pl.ds(rows[0], rows[1])]

    def operands(self):
        ins = [a for a, _, _ in self.gathers] + [a for a, _ in self.scatters]
        shapes = []
        for a, idx, rows in self.gathers:
            blk = a.shape if idx is None else a.shape[1:]
            shapes.append(jax.ShapeDtypeStruct((N_DEV, blk[0] if rows is None else rows[1]) + tuple(blk[1:]), a.dtype))
        for a, rows in self.scatters:
            shapes.append(jax.ShapeDtypeStruct((N_DEV, a.shape[1] if rows is None else rows[1]) + tuple(a.shape[2:]),
                                               a.dtype))
        scratch = []
        for n in (len(self.gathers), len(self.scatters)):
            if n:
                scratch += [pltpu.SemaphoreType.DMA((7 * n,)), pltpu.SemaphoreType.DMA((7 * n,)),
                            pltpu.SemaphoreType.DMA((n,))]
        return ins, shapes, scratch

    def _split(self, in_refs, out_refs, sems):
        ng = len(self.gathers)
        g_sems = sems[:3] if ng else None
        s_sems = sems[3:] if ng else sems
        return in_refs[:ng], in_refs[ng:], out_refs[:ng], out_refs[ng:], g_sems, s_sems

    def _gather_copies(self, i, src_ref, out_ref, sems):
        send_sems, recv_sems, local_sems = sems
        x, y, c = lax.axis_index("x"), lax.axis_index("y"), lax.axis_index("c")
        me, sibling = (x, y, c), (x, y, 1 - c)
        chips = [(1 - x, y), (x, 1 - y), (1 - x, 1 - y)]
        _, idx, rows = self.gathers[i]
        src = self._rows(src_ref if idx is None else src_ref.at[idx], rows)

        def slot(px, py, pc):
            return out_ref.at[4 * px + 2 * py + pc]

        def copy(k, blk, to, from_src=False):
            return pltpu.make_async_remote_copy(
                src_ref=src if from_src else slot(*blk), dst_ref=slot(*blk), send_sem=send_sems.at[7 * i + k],
                recv_sem=recv_sems.at[7 * i + k], device_id=to, device_id_type=pl.DeviceIdType.MESH)

        mine = pltpu.make_async_copy(src, slot(*me), local_sems.at[i])
        first = [copy(0, me, sibling, True)] + [copy(1 + j, me, (*chip, c), True) for j, chip in enumerate(chips)]
        passed = [copy(4 + j, (*chip, c), sibling) for j, chip in enumerate(chips)]
        arrivals = [copy(1 + j, (*chip, c), me) for j, chip in enumerate(chips)]
        from_sibling = [copy(0, sibling, me)] + [copy(4 + j, (*chip, 1 - c), me) for j, chip in enumerate(chips)]
        return mine, first, passed, arrivals, from_sibling

    def _scatter_copies(self, i, src_ref, out_ref, sems):
        send_sems, recv_sems, local_sems = sems
        x, y, c = lax.axis_index("x"), lax.axis_index("y"), lax.axis_index("c")
        me = 4 * x + 2 * y + c
        rows = self.scatters[i][1]
        mine = pltpu.make_async_copy(self._rows(src_ref.at[me], rows), out_ref.at[me], local_sems.at[i])
        copies = []
        for k in range(1, N_DEV):
            px = 1 - x if k & 4 else x
            py = 1 - y if k & 2 else y
            pc = 1 - c if k & 1 else c
            copies.append(pltpu.make_async_remote_copy(
                src_ref=self._rows(src_ref.at[4 * px + 2 * py + pc], rows), dst_ref=out_ref.at[me],
                send_sem=send_sems.at[7 * i + k - 1], recv_sem=recv_sems.at[7 * i + k - 1], device_id=(px, py, pc),
                device_id_type=pl.DeviceIdType.MESH))
        return mine, copies

    def start(self, in_refs, out_refs, sems):
        g_in, s_in, g_out, s_out, g_sems, s_sems = self._split(in_refs, out_refs, sems)
        for i in range(len(self.gathers)):
            mine, first, _, _, _ = self._gather_copies(i, g_in[i], g_out[i], g_sems)
            mine.start()
            for cp in first:
                cp.start()
        for i in range(len(self.scatters)):
            mine, copies = self._scatter_copies(i, s_in[i], s_out[i], s_sems)
            mine.start()
            for cp in copies:
                cp.start()

    def finish(self, in_refs, out_refs, sems):
        g_in, s_in, g_out, s_out, g_sems, s_sems = self._split(in_refs, out_refs, sems)
        parts = [self._gather_copies(i, g_in[i], g_out[i], g_sems) for i in range(len(self.gathers))]
        for j in range(3):
            for _, _, passed, arrivals, _ in parts:
                arrivals[j].wait_recv()
                passed[j].start()
        for mine, first, passed, _, from_sibling in parts:
            for cp in from_sibling:
                cp.wait_recv()
            for cp in first + passed:
                cp.wait_send()
            mine.wait()
        for i in range(len(self.scatters)):
            mine, copies = self._scatter_copies(i, s_in[i], s_out[i], s_sems)
            for cp in copies:
                cp.wait_recv()
            for cp in copies:
                cp.wait_send()
            mine.wait()


def _params(grid):
    return pltpu.CompilerParams(dimension_semantics=("arbitrary",) * len(grid), vmem_limit_bytes=VMEM_LIMIT)


def _call(name, body, *, grid, ins, in_specs, out_shape, out_specs, scratch=(), comm=None, aliases=None):
    n_in, n_out, n_scr = len(ins), len(out_shape), len(scratch)
    aliases = aliases or {}
    if comm is None:
        outs = pl.pallas_call(body, grid=grid, in_specs=list(in_specs), out_specs=list(out_specs),
                              out_shape=list(out_shape), scratch_shapes=list(scratch), name=name,
                              input_output_aliases=aliases, compiler_params=_params(grid))(*ins)
        return list(outs), []
    c_ins, c_shapes, c_scratch = comm.operands()
    nci, nco = len(c_ins), len(c_shapes)
    anywhere = pl.BlockSpec(memory_space=pl.ANY)

    def carrier(*refs):
        main_in, comm_in = refs[:n_in], refs[n_in:n_in + nci]
        o0 = n_in + nci
        main_out, comm_out = refs[o0:o0 + n_out], refs[o0 + n_out:o0 + n_out + nco]
        s0 = o0 + n_out + nco
        main_scr, comm_scr = refs[s0:s0 + n_scr], refs[s0 + n_scr:]
        first = pl.program_id(0) == 0
        last = pl.program_id(0) == grid[0] - 1
        for ax in range(1, len(grid)):
            first = first & (pl.program_id(ax) == 0)
            last = last & (pl.program_id(ax) == grid[ax] - 1)

        @pl.when(first)
        def _():
            comm.start(comm_in, comm_out, comm_scr)

        body(*main_in, *main_out, *main_scr)

        @pl.when(last)
        def _():
            comm.finish(comm_in, comm_out, comm_scr)

    outs = pl.pallas_call(carrier, grid=grid, in_specs=list(in_specs) + [anywhere] * nci,
                          out_specs=list(out_specs) + [anywhere] * nco, out_shape=list(out_shape) + c_shapes,
                          scratch_shapes=list(scratch) + c_scratch, name=name, input_output_aliases=aliases,
                          compiler_params=_params(grid))(*ins, *c_ins)
    return list(outs[:n_out]), list(outs[n_out:])


def _comm_only(name, comm):
    c_ins, c_shapes, c_scratch = comm.operands()
    nci, nco = len(c_ins), len(c_shapes)
    anywhere = pl.BlockSpec(memory_space=pl.ANY)

    def body(*refs):
        comm.start(refs[:nci], refs[nci:nci + nco], refs[nci + nco:])
        comm.finish(refs[:nci], refs[nci:nci + nco], refs[nci + nco:])

    return list(pl.pallas_call(body, in_specs=[anywhere] * nci, out_specs=[anywhere] * nco, out_shape=c_shapes,
                               scratch_shapes=c_scratch, name=name)(*c_ins))


def _stage_fwd(name, f, grid, ins, in_specs, out_shapes, out_specs, comm=None):
    n_in = len(ins)

    def body(*refs):
        res = f(*[r[...].astype(F32) for r in refs[:n_in]])
        for o_ref, val in zip(refs[n_in:], res):
            o_ref[...] = val.astype(o_ref.dtype)

    return _call(name, body, grid=grid, ins=ins, in_specs=in_specs, out_shape=out_shapes, out_specs=out_specs, comm=comm)


def _stage_bwd(name, f, grid, ins, in_specs, cts, ct_specs, grads, comm=None, ct_product=None):
    n_in = len(ins)
    flat_cts = [c for group in cts for c in group]
    flat_ct_specs = [s for group, spec in zip(cts, ct_specs) for s in (spec,) * len(group)]
    if ct_product is not None:
        assert not cts
        flat_cts = [ct_product[0], ct_product[2]]
        flat_ct_specs = [ct_product[1], pl.BlockSpec(ct_product[2].shape, lambda *_: (0, 0))]
    n_ct = len(flat_cts)
    diff = [g[0] for g in grads]
    buffers = [(k, g[4]) for k, g in enumerate(grads) if len(g) > 4]
    n_buf = len(buffers)

    def body(*refs):
        vals = [r[...].astype(F32) for r in refs[:n_in]]
        ct_refs = refs[n_in:n_in + n_ct]
        g_refs = refs[n_in + n_ct + n_buf:]
        ct_vals, pos = [], 0
        if ct_product is not None:
            ct_vals.append(lax.dot_general(ct_refs[0][...].astype(BF16), ct_refs[1][...].astype(BF16), NT_DIMS,
                                           preferred_element_type=F32))
        for group in cts:
            acc = ct_refs[pos][...].astype(F32)
            for j in range(1, len(group)):
                acc = acc + ct_refs[pos + j][...].astype(F32)
            ct_vals.append(acc)
            pos += len(group)

        def g_fn(*dvals):
            full = list(vals)
            for i, dv in zip(diff, dvals):
                full[i] = dv
            return f(*full)

        _, vjp = jax.vjp(g_fn, *[vals[i] for i in diff])
        gvals = vjp(tuple(ct_vals))
        for gspec, g_ref, gval in zip(grads, g_refs, gvals):
            acc_axes = gspec[1]
            if not acc_axes:
                g_ref[...] = gval.astype(g_ref.dtype)
            else:
                first = pl.program_id(acc_axes[0]) == 0
                for ax in acc_axes[1:]:
                    first = first & (pl.program_id(ax) == 0)

                @pl.when(first)
                def _():
                    g_ref[...] = jnp.zeros_like(g_ref)

                g_ref[...] += gval.astype(g_ref.dtype)

    out_shapes, out_specs = [], []
    for gspec in grads:
        shape, spec = gspec[3] if len(gspec) > 3 else (ins[gspec[0]].shape, in_specs[gspec[0]])
        out_shapes.append(jax.ShapeDtypeStruct(shape, gspec[2]))
        out_specs.append(spec)
    anywhere = pl.BlockSpec(memory_space=pl.ANY)
    return _call(name, body, grid=grid, ins=list(ins) + flat_cts + [b for _, b in buffers],
                 in_specs=list(in_specs) + flat_ct_specs + [anywhere] * n_buf, out_shape=out_shapes, out_specs=out_specs,
                 comm=comm, aliases={n_in + n_ct + j: k for j, (k, _) in enumerate(buffers)})


def _pick_tile(n, candidates):
    for c in candidates:
        if n % c == 0:
            return c
    return n


def _matmul(name, a, b, *, ta=False, tb=False, add=None, extra=None, out_dtype=F32, comm=None):
    if ta:
        k_dim, m = a.shape
    else:
        m, k_dim = a.shape
    n = b.shape[0] if tb else b.shape[1]
    assert (b.shape[1] if tb else b.shape[0]) == k_dim and not (ta and tb)
    tm = _pick_tile(m, (1024, 1408, 512, 256, 128))
    tn = _pick_tile(n, (1024, 1408, 512, 256, 128))
    if ta:
        tk = _pick_tile(k_dim, (1024, 512, 256, 128))
    elif k_dim <= 2816:
        tk = k_dim
    else:
        tk = _pick_tile(k_dim, (1408, 1024, 512, 256, 128))
    nk = k_dim // tk
    grid = (m // tm, n // tn, nk)
    a_spec = pl.BlockSpec((tk, tm), lambda i, j, k: (k, i)) if ta else pl.BlockSpec((tm, tk), lambda i, j, k: (i, k))
    b_spec = pl.BlockSpec((tn, tk), lambda i, j, k: (j, k)) if tb else pl.BlockSpec((tk, tn), lambda i, j, k: (k, j))
    o_spec = pl.BlockSpec((tm, tn), lambda i, j, k: (i, j))
    dims = (((0 if ta else 1,), (1 if tb else 0,)), ((), ()))
    has_add = add is not None
    has_extra = extra is not None

    def body(*refs):
        a_ref, b_ref = refs[0], refs[1]
        add_ref = refs[2] if has_add else None
        o_ref, acc_ref = refs[-2], refs[-1]
        k = pl.program_id(2)
        part = lax.dot_general(a_ref[...].astype(BF16), b_ref[...].astype(BF16), dims, preferred_element_type=F32)

        def finish(res):
            if has_add:
                res = res + add_ref[...].astype(F32)
            if has_extra:
                a2_ref, b2_ref = refs[2 + has_add], refs[3 + has_add]
                res = res + lax.dot_general(a2_ref[...].astype(BF16), b2_ref[...].astype(BF16), NT_DIMS,
                                            preferred_element_type=F32)
            o_ref[...] = res.astype(o_ref.dtype)

        if nk == 1:
            finish(part)
        else:
            @pl.when(k == 0)
            def _():
                acc_ref[...] = part

            @pl.when((k > 0) & (k < nk - 1))
            def _():
                acc_ref[...] += part

            @pl.when(k == nk - 1)
            def _():
                finish(acc_ref[...] + part)

    ins = [a, b] + ([add] if has_add else [])
    in_specs = [a_spec, b_spec] + ([o_spec] if has_add else [])
    if has_extra:
        k2 = extra[0].shape[1]
        ins += list(extra)
        in_specs += [pl.BlockSpec((tm, k2), lambda i, j, k: (i, 0)), pl.BlockSpec((tn, k2), lambda i, j, k: (j, 0))]
    acc_shape = (tm, tn) if nk > 1 else (8, 128)
    outs, comm_outs = _call(name, body, grid=grid, ins=ins, in_specs=in_specs,
                            out_shape=[jax.ShapeDtypeStruct((m, n), out_dtype)], out_specs=[o_spec],
                            scratch=[pltpu.VMEM(acc_shape, F32)], comm=comm)
    return outs[0], comm_outs


def _matmul_lnres(name, a_fn, a_ins, a_specs, tm, b, x, g, beta, comm=None):
    m = x.shape[0]
    k_dim, n = b.shape
    n_a = len(a_ins)
    row = lambda w: pl.BlockSpec((tm, w), lambda i: (i, 0))
    whole = lambda shape: pl.BlockSpec(shape, lambda i: (0, 0))

    def body(*refs):
        b_ref, x_ref, g_ref, beta_ref = refs[n_a:n_a + 4]
        y_ref, h_ref, hb_ref = refs[-3:]
        if a_fn is None:
            a = refs[0][...].astype(BF16)
        else:
            (a,) = a_fn(*[r[...].astype(F32) for r in refs[:n_a]])
            a = a.astype(BF16)
            refs[n_a + 4][...] = a
        y = jnp.dot(a, b_ref[...].astype(BF16), preferred_element_type=F32).astype(y_ref.dtype)
        y_ref[...] = y
        (h,) = _lnres_block(x_ref[...], y.astype(F32), g_ref[...], beta_ref[...])
        h_ref[...] = h
        hb_ref[...] = h.astype(hb_ref.dtype)

    sds = jax.ShapeDtypeStruct
    a_out = ([sds((m, k_dim), BF16)], [row(k_dim)]) if a_fn is not None else ([], [])
    return _call(name, body, grid=(m // tm,), ins=list(a_ins) + [b, x, g, beta],
                 in_specs=list(a_specs) + [whole((k_dim, n)), row(n), whole((1, n)), whole((1, n))],
                 out_shape=a_out[0] + [sds((m, n), BF16), sds((m, n), F32), sds((m, n), BF16)],
                 out_specs=a_out[1] + [row(n), row(n), row(n)], comm=comm)


SSD_STATE = (SSM_GROUPS * SSM_STATE, SSM_RPG * SSM_HEADDIM)


def _ssd_fwd(xc, dt_raw, proj, dt_bias, a_log, d_skip, norm_g, nb, nc, comm=None):
    t = xc.shape[0]
    row = lambda b, c: (b * nc + c, 0)
    par = lambda shape: pl.BlockSpec(shape, lambda b, c: (0, 0))

    def body(xc_ref, dt_ref, z_ref, dtb_ref, al_ref, ds_ref, ng_ref, e_ref, y_ref, ypre_ref, prev_ref, st_ref):
        @pl.when(pl.program_id(1) == 0)
        def _():
            st_ref[...] = jnp.zeros_like(st_ref)

        prev = st_ref[...]
        prev_ref[0, 0] = prev
        y, y_pre, new_state = _ssd_forward(xc_ref[...], dt_ref[...], z_ref[...].astype(F32), prev, dtb_ref[...],
                                           al_ref[...], ds_ref[...], ng_ref[...], e_ref[...])
        y_ref[...] = y.astype(y_ref.dtype)
        ypre_ref[...] = y_pre
        st_ref[...] = new_state

    return _call(
        "ssd_fwd", body, grid=(nb, nc), ins=[xc, dt_raw, proj, dt_bias, a_log, d_skip, norm_g, _head_expander()],
        in_specs=[pl.BlockSpec((CHUNK, SSM_CONV_DIM), row), pl.BlockSpec((CHUNK, 128), row),
                  pl.BlockSpec((CHUNK, SSM_INNER), lambda b, c: (b * nc + c, 1)),
                  par((1, 128)), par((1, 128)), par((1, SSM_INNER)), par((1, SSM_INNER)), par((128, SSM_INNER))],
        out_specs=[pl.BlockSpec((CHUNK, SSM_INNER), row), pl.BlockSpec((CHUNK, SSM_INNER), row),
                   pl.BlockSpec((1, 1) + SSD_STATE, lambda b, c: (b, c, 0, 0))],
        out_shape=[jax.ShapeDtypeStruct((t, SSM_INNER), BF16), jax.ShapeDtypeStruct((t, SSM_INNER), F32),
                   jax.ShapeDtypeStruct((nb, nc) + SSD_STATE, F32)],
        scratch=[pltpu.VMEM(SSD_STATE, F32)], comm=comm)


def _ssd_bwd(xc, dt_raw, proj, prevs, y_pre, dt_bias, a_log, d_skip, norm_g, dy, dproj, nb, nc, comm=None):
    t = xc.shape[0]
    row = lambda b, c: (b * nc + (nc - 1 - c), 0)
    par = lambda shape: pl.BlockSpec(shape, lambda b, c: (0, 0))
    z_spec = pl.BlockSpec((CHUNK, SSM_INNER), lambda b, c: (b * nc + (nc - 1 - c), 1))

    def body(xc_ref, dt_ref, z_ref, prev_ref, ypre_ref, dtb_ref, al_ref, ds_ref, ng_ref, e_ref, dy_ref, _,
             dxc_ref, ddt_ref, dz_ref, ddtb_ref, dal_ref, dds_ref, dng_ref, dst_ref):
        @pl.when(pl.program_id(1) == 0)
        def _():
            dst_ref[...] = jnp.zeros_like(dst_ref)

        @pl.when((pl.program_id(0) == 0) & (pl.program_id(1) == 0))
        def _():
            ddtb_ref[...] = jnp.zeros_like(ddtb_ref)
            dal_ref[...] = jnp.zeros_like(dal_ref)
            dds_ref[...] = jnp.zeros_like(dds_ref)
            dng_ref[...] = jnp.zeros_like(dng_ref)

        dxc, ddt, dz, dprev, ddtb, dal, dds, dng = _ssd_backward(
            xc_ref[...], dt_ref[...], z_ref[...].astype(F32), prev_ref[0, 0], ypre_ref[...], dtb_ref[...], al_ref[...],
            ds_ref[...], ng_ref[...], e_ref[...], dy_ref[...].astype(F32), dst_ref[...])
        dxc_ref[...] = dxc
        ddt_ref[...] = ddt.astype(ddt_ref.dtype)
        dz_ref[...] = dz.astype(dz_ref.dtype)
        dst_ref[...] = dprev
        ddtb_ref[...] += ddtb
        dal_ref[...] += dal
        dds_ref[...] += dds
        dng_ref[...] += dng

    return _call(
        "ssd_bwd", body, grid=(nb, nc),
        ins=[xc, dt_raw, proj, prevs, y_pre, dt_bias, a_log, d_skip, norm_g, _head_expander(), dy, dproj],
        in_specs=[pl.BlockSpec((CHUNK, SSM_CONV_DIM), row), pl.BlockSpec((CHUNK, DT_LANES), row), z_spec,
                  pl.BlockSpec((1, 1) + SSD_STATE, lambda b, c: (b, nc - 1 - c, 0, 0)),
                  pl.BlockSpec((CHUNK, SSM_INNER), row),
                  par((1, 128)), par((1, 128)), par((1, SSM_INNER)), par((1, SSM_INNER)), par((128, SSM_INNER)),
                  pl.BlockSpec((CHUNK, SSM_INNER), row), pl.BlockSpec(memory_space=pl.ANY)],
        out_specs=[pl.BlockSpec((CHUNK, SSM_CONV_DIM), row), pl.BlockSpec((CHUNK, DT_LANES), row), z_spec,
                   par((1, 128)), par((1, 128)), par((1, 128)), par((1, SSM_INNER))],
        out_shape=[jax.ShapeDtypeStruct((t, SSM_CONV_DIM), F32), jax.ShapeDtypeStruct((t, DT_LANES), BF16),
                   jax.ShapeDtypeStruct(dproj.shape, dproj.dtype), jax.ShapeDtypeStruct((1, 128), F32),
                   jax.ShapeDtypeStruct((1, 128), F32), jax.ShapeDtypeStruct((1, 128), F32),
                   jax.ShapeDtypeStruct((1, SSM_INNER), F32)],
        scratch=[pltpu.VMEM(SSD_STATE, F32)], comm=comm, aliases={11: 2})


def _loss_head(y, target):
    t, d = y.shape
    tm = _pick_tile(t, (256,))
    blk = pl.BlockSpec((tm, d), lambda i: (i, 0))

    def body(y_ref, t_ref, loss_ref, dy_ref):
        err = y_ref[...] - t_ref[...]
        dy_ref[...] = err * (1.0 / d)

        @pl.when(pl.program_id(0) == 0)
        def _():
            loss_ref[...] = jnp.zeros_like(loss_ref)

        loss_ref[...] += 0.5 * jnp.sum(jnp.mean(err * err, axis=-1, keepdims=True), axis=0, keepdims=True)

    return _call("loss_head", body, grid=(t // tm,), ins=[y, target], in_specs=[blk, blk],
                 out_specs=[pl.BlockSpec((1, 1), lambda i: (0, 0)), blk],
                 out_shape=[jax.ShapeDtypeStruct((1, 1), F32), jax.ShapeDtypeStruct((t, d), F32)])[0]


def _adamw_math(g, w, m, v):
    m_new = ADAM_B1 * m + (1.0 - ADAM_B1) * g
    v_new = ADAM_B2 * v + (1.0 - ADAM_B2) * jnp.square(g)
    m_hat = m_new / (1.0 - ADAM_B1 ** ADAM_STEP)
    v_hat = v_new / (1.0 - ADAM_B2 ** ADAM_STEP)
    delta = -ADAM_LR * (m_hat / (jnp.sqrt(v_hat) + ADAM_EPS) + ADAM_WD * w)
    return delta, m_new, v_new


def _adamw_sharded(name, parts, w, m, v, comm=None):
    _, a, b = w.shape
    tr = _pick_tile(a, (128,))
    nt = a // tr
    part_specs = [pl.BlockSpec((N_DEV, tr, b),
                               (lambda l, i, _k=k: (0, jnp.where(l == _k, i, jnp.where(l > _k, nt - 1, 0)), 0)))
                  for k in range(DEPTH)]
    blk = pl.BlockSpec((1, tr, b), lambda l, i: (l, i, 0))

    def body(*refs):
        p_refs = refs[:DEPTH]
        w_ref, m_ref, v_ref, g_out, d_out, m_out, v_out = refs[DEPTH:]
        for k in range(DEPTH):
            @pl.when(pl.program_id(0) == k)
            def _(p_ref=p_refs[k]):
                g = p_ref[0].astype(F32)
                for p in range(1, N_DEV):
                    g = g + p_ref[p].astype(F32)
                delta, m_new, v_new = _adamw_math(g, w_ref[0], m_ref[0], v_ref[0])
                g_out[0] = g
                d_out[0] = delta
                m_out[0] = m_new
                v_out[0] = v_new

    return _call(name, body, grid=(DEPTH, nt), ins=list(parts) + [w, m, v], in_specs=part_specs + [blk, blk, blk],
                 out_specs=[blk] * 4, out_shape=[jax.ShapeDtypeStruct(w.shape, F32)] * 4, comm=comm)


def _adamw_small(name, g, w, m, v):
    full = pl.BlockSpec(w.shape, lambda i: (0, 0))

    def body(g_ref, w_ref, m_ref, v_ref, d_out, m_out, v_out):
        delta, m_new, v_new = _adamw_math(g_ref[...], w_ref[...], m_ref[...], v_ref[...])
        d_out[...] = delta
        m_out[...] = m_new
        v_out[...] = v_new

    return _call(name, body, grid=(1,), ins=[g, w, m, v], in_specs=[full] * 4, out_specs=[full] * 3,
                 out_shape=[jax.ShapeDtypeStruct(w.shape, F32)] * 3)[0]


def _sum_parts(name, parts):
    n_parts, rows, cols = parts.shape
    tr = _pick_tile(rows, (512, 256, 128, 64, 32, 16, 8))

    def body(p_ref, o_ref):
        acc = p_ref[0]
        for p in range(1, n_parts):
            acc = acc + p_ref[p]
        o_ref[...] = acc

    return _call(name, body, grid=(rows // tr,), ins=[parts],
                 in_specs=[pl.BlockSpec((n_parts, tr, cols), lambda i: (0, i, 0))],
                 out_specs=[pl.BlockSpec((tr, cols), lambda i: (i, 0))],
                 out_shape=[jax.ShapeDtypeStruct((rows, cols), parts.dtype)])[0][0]


W_IN_SHARD = IN_COLS // N_DEV


def _pack_w_in(gathered):
    r = gathered.shape[1]
    tr = _pick_tile(r, (128,))

    def body(g_ref, main_ref, dt_ref):
        w = jnp.concatenate([g_ref[j].astype(F32) for j in range(N_DEV)], axis=1)
        main_ref[...] = jnp.concatenate([w[:, :XBC_COL], w[:, GA_COL:], w[:, XBC_COL:DT_COL]],
                                        axis=1).astype(main_ref.dtype)
        dt_ref[...] = jnp.concatenate([w[:, DT_COL:GA_COL], jnp.zeros((tr, DT_LANES - SSM_HEADS), F32)],
                                      axis=1).astype(dt_ref.dtype)

    return _call("pack_w_in", body, grid=(r // tr,), ins=[gathered],
                 in_specs=[pl.BlockSpec((N_DEV, tr, W_IN_SHARD), lambda i: (0, i, 0))],
                 out_specs=[pl.BlockSpec((tr, P_COLS), lambda i: (i, 0)), pl.BlockSpec((tr, DT_LANES), lambda i: (i, 0))],
                 out_shape=[jax.ShapeDtypeStruct((r, P_COLS), gathered.dtype),
                            jax.ShapeDtypeStruct((r, DT_LANES), gathered.dtype)])[0]


def _unpack_w_in(main, dt):
    r = main.shape[0]
    tr = _pick_tile(r, (128,))

    def body(main_ref, dt_ref, o_ref):
        main = main_ref[...].astype(F32)
        w = jnp.concatenate([main[:, :P_GATE], main[:, P_XBC:], dt_ref[...].astype(F32)[:, :SSM_HEADS],
                             main[:, P_GATE:P_XBC]], axis=1)
        for j in range(N_DEV):
            o_ref[j] = w[:, j * W_IN_SHARD:(j + 1) * W_IN_SHARD].astype(o_ref.dtype)

    return _call("unpack_w_in", body, grid=(r // tr,), ins=[main, dt],
                 in_specs=[pl.BlockSpec((tr, P_COLS), lambda i: (i, 0)), pl.BlockSpec((tr, DT_LANES), lambda i: (i, 0))],
                 out_specs=[pl.BlockSpec((N_DEV, tr, W_IN_SHARD), lambda i: (0, i, 0))],
                 out_shape=[jax.ShapeDtypeStruct((N_DEV, r, W_IN_SHARD), main.dtype)])[0][0]


def _pad_heads(v):
    return jnp.pad(v, (0, 128 - SSM_HEADS)).reshape(1, 128)


def _run_step(x, mem, target, small, ex):
    nb, s, d = x.shape
    t = nb * s
    nc = s // CHUNK
    rows = _pick_tile(t, (256,))
    rows_wide = _pick_tile(t, (512, 256))
    tq = _pick_tile(s, (512, 256))
    vec = lambda a: a.reshape(1, -1)
    full1 = lambda shape: pl.BlockSpec(shape, lambda i: (0,) * len(shape))
    row1 = lambda tm, w: pl.BlockSpec((tm, w), lambda i: (i, 0))
    sds = jax.ShapeDtypeStruct

    def mm(call, l, a, b, **kw):
        comm = ex.before(call, l)
        out, comm_outs = _matmul(call, a, b, comm=comm, **kw)
        if comm is not None:
            ex.after(call, l, comm_outs)
        return out

    def stage_bwd(call, l, *args, **kw):
        comm = ex.before(call, l)
        outs, comm_outs = _stage_bwd(call, *args, comm=comm, **kw)
        if comm is not None:
            ex.after(call, l, comm_outs)
        return outs

    def stage_fwd(call, l, *args):
        comm = ex.before(call, l)
        outs, comm_outs = _stage_fwd(call, *args, comm=comm)
        if comm is not None:
            ex.after(call, l, comm_outs)
        return outs

    mem_specs = [row1(256, d), full1((1, d)), full1((1, d))]
    mem_ins = [mem.reshape(nb * MEM_LEN, d), vec(small["mem_ln_g"]), vec(small["mem_ln_b"])]
    (mem_n,) = stage_fwd("memln_fwd", 0, _memln_block, (nb * MEM_LEN // 256,), mem_ins, mem_specs,
                          [sds((nb * MEM_LEN, d), BF16)], [row1(256, d)])

    h = x.reshape(t, d)
    h_bf = h.astype(BF16)
    ln_specs = [row1(rows, d), row1(rows, d), full1((1, d)), full1((1, d))]
    saved = []
    for l in range(DEPTH):
        sv = {"h_bf": h_bf}
        w_p, w_dt = ex.weight("w_in", l)
        proj = mm("mm_in", l, h_bf, w_p, out_dtype=BF16)
        dt_raw = mm("mm_dt", l, h_bf, w_dt)
        sv["proj"] = proj
        sgu_ins = [proj, vec(small["sg_ln_g"][l]), vec(small["sg_ln_b"][l]), small["sg_w"][l], small["sg_b"][l].T]
        sgu_specs = [pl.BlockSpec((CHUNK, 2 * d), lambda i: (i, 0)), full1((1, d)), full1((1, d)),
                     full1((SG_GROUPS, CHUNK, CHUNK)), full1((CHUNK, SG_GROUPS))]
        (a_out,) = stage_fwd("sgu_fwd", l, _sgu_block, (t // CHUNK,), sgu_ins, sgu_specs, [sds((t, d), BF16)],
                              [row1(CHUNK, d)])
        sv["sgu"] = (sgu_ins, sgu_specs)
        sv["a_out"] = a_out
        cw = 256
        conv_ins = [proj, small["conv_w"][l], vec(small["conv_b"][l])]
        conv_specs = [pl.BlockSpec((s, cw), lambda j, b: (b, P_XBC // cw + j)),
                      pl.BlockSpec((SSM_CONV, cw), lambda j, b: (0, j)), pl.BlockSpec((1, cw), lambda j, b: (0, j))]
        conv_out_spec = pl.BlockSpec((s, cw), lambda j, b: (b, j))
        (xc,) = stage_fwd("conv_fwd", l, _conv_block, (SSM_CONV_DIM // cw, nb), conv_ins, conv_specs,
                           [sds((t, SSM_CONV_DIM), F32)], [conv_out_spec])
        sv["conv"] = (conv_ins, conv_specs, conv_out_spec)
        ssd_par = [_pad_heads(small["dt_bias"][l]), _pad_heads(small["a_log"][l]),
                   vec(jnp.repeat(small["d_skip"][l], SSM_HEADDIM)), vec(small["ssm_norm_g"][l])]
        comm = ex.before("ssd_fwd", l)
        (y_ssd, y_pre, prevs), comm_outs = _ssd_fwd(xc, dt_raw, proj, *ssd_par, nb, nc, comm=comm)
        if comm is not None:
            ex.after("ssd_fwd", l, comm_outs)
        sv["ssd"] = (xc, dt_raw, prevs, y_pre, ssd_par)
        sv["y_ssd"] = y_ssd
        br_a = mm("mm_sq", l, a_out, ex.weight("p_a", l), out_dtype=BF16)
        br_b = mm("mm_pb", l, y_ssd, ex.weight("p_b", l), out_dtype=BF16)
        merge_ins = [proj, br_a, br_b]
        merge_out_spec = row1(rows_wide, d)
        merge_specs = [pl.BlockSpec((rows_wide, 2 * d), lambda i: (i, P_GATE // (2 * d))), merge_out_spec, merge_out_spec]
        sv["merge"] = (merge_ins, merge_specs, merge_out_spec)
        ln_par = [(vec(small["ln_g"][l, k]), vec(small["ln_b"][l, k])) for k in range(3)]

        def fused(call, a_fn, a_ins, a_specs, tm, w, x_in, par):
            comm = ex.before(call, l)
            outs, comm_outs = _matmul_lnres(call, a_fn, a_ins, a_specs, tm, w, x_in, *par, comm=comm)
            if comm is not None:
                ex.after(call, l, comm_outs)
            return outs

        merged, y1, h1, h1_bf = fused("mm_mix_ln", _merge_block, merge_ins, merge_specs, rows_wide,
                                      ex.weight("w_mix_o", l), h, ln_par[0])
        sv["merged"] = merged
        sv["ln1"] = [h, y1, *ln_par[0]]
        q = mm("mm_sq", l, h1_bf, ex.weight("w_xq", l), out_dtype=BF16)
        kv = mm("mm_kv", l, mem_n, ex.weight("w_xkv", l), out_dtype=BF16)
        attn_ins = [q, kv]
        attn_out_spec = pl.BlockSpec((tq, d), lambda b, i: (b * (s // tq) + i, 0))
        attn_specs = [attn_out_spec, pl.BlockSpec((MEM_LEN, 2 * d), lambda b, i: (b, 0))]
        (o,) = stage_fwd("attn_fwd", l, _attn_block, (nb, s // tq), attn_ins, attn_specs, [sds((t, d), BF16)],
                          [attn_out_spec])
        sv["attn"] = (attn_ins, attn_specs, attn_out_spec)
        sv["o"] = o
        sv["h1_bf"] = h1_bf
        y2, h2, h2_bf = fused("mm_xo_ln", None, [o], [row1(rows_wide, d)], rows_wide, ex.weight("w_xo", l), h1, ln_par[1])
        sv["ln2"] = [h1, y2, *ln_par[1]]
        sv["h2_bf"] = h2_bf
        gu = mm("mm_ffn_in", l, h2_bf, ex.weight("w_ffn_in", l), out_dtype=BF16)
        act, y3, h3, h3_bf = fused("mm_ffn_out_ln", _swiglu_block, [gu], [row1(rows, 2 * FFN_HIDDEN)], rows,
                                   ex.weight("w_ffn_out", l), h2, ln_par[2])
        sv["gu"] = gu
        sv["act"] = act
        sv["ln3"] = [h2, y3, *ln_par[2]]
        h, h_bf = h3, h3_bf
        saved.append(sv)

    loss, dh = _loss_head(h, target.reshape(t, d))

    g_small = {n: [None] * DEPTH for n in SMALL_REP + SMALL_SH if n not in ("mem_ln_g", "mem_ln_b")}
    dmem_n = []
    ln_grads = [(0, (), F32), (1, (), BF16), (2, (0,), F32), (3, (0,), F32)]
    for l in reversed(range(DEPTH)):
        sv = saved[l]
        dln_g, dln_b = [None] * 3, [None] * 3
        dres, dy3, dln_g[2], dln_b[2] = stage_bwd("lnres_bwd", l, _lnres_block, (t // rows,), sv["ln3"], ln_specs,
                                                  [(dh,)], [row1(rows, d)], ln_grads)
        ex.grad("w_ffn_out", l, mm("mm_ffn_out_dw", l, sv["act"], dy3, ta=True, out_dtype=BF16))
        (dgu,) = stage_bwd("swiglu_bwd", l, _swiglu_block, (t // rows,), [sv["gu"]], [row1(rows, 2 * FFN_HIDDEN)],
                           [], [], [(0, (), BF16)], ct_product=(dy3, row1(rows, d), ex.weight("w_ffn_out", l)))
        ex.grad("w_ffn_in", l, mm("mm_ffn_in_dw", l, sv["h2_bf"], dgu, ta=True, out_dtype=BF16))
        dh2 = mm("mm_ffn_in_dx", l, dgu, ex.weight("w_ffn_in", l), tb=True, add=dres)
        dres, dy2, dln_g[1], dln_b[1] = stage_bwd("lnres_bwd", l, _lnres_block, (t // rows,), sv["ln2"], ln_specs,
                                                  [(dh2,)], [row1(rows, d)], ln_grads)
        ex.grad("w_xo", l, mm("mm_sq_dw", l, sv["o"], dy2, ta=True, out_dtype=BF16))
        do = mm("mm_sq_dx", l, dy2, ex.weight("w_xo", l), tb=True, out_dtype=BF16)
        attn_ins, attn_specs, attn_out_spec = sv["attn"]
        dq, dkv = stage_bwd("attn_bwd", l, _attn_block, (nb, s // tq), attn_ins, attn_specs, [(do,)], [attn_out_spec],
                            [(0, (), BF16), (1, (1,), F32)])
        ex.grad("w_xq", l, mm("mm_sq_dw", l, sv["h1_bf"], dq, ta=True, out_dtype=BF16))
        dh1 = mm("mm_sq_dx", l, dq, ex.weight("w_xq", l), tb=True, add=dres)
        ex.grad("w_xkv", l, mm("mm_kv_dw", l, mem_n, dkv, ta=True, out_dtype=BF16))
        dmem_n.append(mm("mm_kv_dx", l, dkv, ex.weight("w_xkv", l), tb=True))
        dres, dy1, dln_g[0], dln_b[0] = stage_bwd("lnres_bwd", l, _lnres_block, (t // rows,), sv["ln1"], ln_specs,
                                                  [(dh1,)], [row1(rows, d)], ln_grads)
        g_small["ln_g"][l] = jnp.concatenate(dln_g, axis=0)
        g_small["ln_b"][l] = jnp.concatenate(dln_b, axis=0)
        ex.grad("w_mix_o", l, mm("mm_sq_dw", l, sv["merged"], dy1, ta=True, out_dtype=BF16))
        merge_ins, merge_specs, merge_out_spec = sv["merge"]
        dproj, dbr_a, dbr_b = stage_bwd("merge_bwd", l, _merge_block, (t // rows_wide,), merge_ins, merge_specs, [], [],
                                        [(0, (), BF16, ((t, P_COLS), merge_specs[0])), (1, (), BF16), (2, (), BF16)],
                                        ct_product=(dy1, merge_out_spec, ex.weight("w_mix_o", l)))
        ex.grad("p_a", l, mm("mm_sq_dw", l, sv["a_out"], dbr_a, ta=True, out_dtype=BF16))
        da_out = mm("mm_sq_dx", l, dbr_a, ex.weight("p_a", l), tb=True, out_dtype=BF16)
        ex.grad("p_b", l, mm("mm_pb_dw", l, sv["y_ssd"], dbr_b, ta=True, out_dtype=BF16))
        dy_ssd = mm("mm_pb_dx", l, dbr_b, ex.weight("p_b", l), tb=True, out_dtype=BF16)
        sgu_ins, sgu_specs = sv["sgu"]
        dproj, dsg_ln_g, dsg_ln_b, dsg_w, dsg_b = stage_bwd(
            "sgu_bwd", l, _sgu_block, (t // CHUNK,), sgu_ins, sgu_specs, [(da_out,)], [row1(CHUNK, d)],
            [(0, (), BF16, ((t, P_COLS), sgu_specs[0]), dproj), (1, (0,), F32), (2, (0,), F32), (3, (0,), F32),
             (4, (0,), F32)])
        g_small["sg_ln_g"][l], g_small["sg_ln_b"][l], g_small["sg_w"][l], g_small["sg_b"][l] = (
            dsg_ln_g[0], dsg_ln_b[0], dsg_w, dsg_b.T)
        xc, dt_raw, prevs, y_pre, ssd_par = sv["ssd"]
        comm = ex.before("ssd_bwd", l)
        (dxc, ddt, dproj, ddtb, dal, dds, dng), comm_outs = _ssd_bwd(xc, dt_raw, sv["proj"], prevs, y_pre, *ssd_par,
                                                                     dy_ssd, dproj, nb, nc, comm=comm)
        if comm is not None:
            ex.after("ssd_bwd", l, comm_outs)
        g_small["dt_bias"][l], g_small["a_log"][l], g_small["d_skip"][l] = (
            ddtb[0, :SSM_HEADS], dal[0, :SSM_HEADS], dds[0, :SSM_HEADS])
        g_small["ssm_norm_g"][l] = dng[0]
        conv_ins, conv_specs, conv_out_spec = sv["conv"]
        dproj, dconv_w, dconv_b = stage_bwd("conv_bwd", l, _conv_block, (SSM_CONV_DIM // 256, nb), conv_ins, conv_specs,
                                            [(dxc,)], [conv_out_spec],
                                            [(0, (), BF16, ((t, P_COLS), conv_specs[0]), dproj), (1, (1,), F32),
                                             (2, (1,), F32)])
        g_small["conv_w"][l], g_small["conv_b"][l] = dconv_w, dconv_b[0]
        if l == 0:
            dmg, dmb = stage_bwd("memln_bwd", l, _memln_block, (nb * MEM_LEN // 256,), mem_ins, mem_specs,
                                 [tuple(dmem_n)], [row1(256, d)], [(1, (0,), F32), (2, (0,), F32)])
            done = {n: jnp.stack(g, axis=0) for n, g in g_small.items()}
            done["mem_ln_g"], done["mem_ln_b"] = dmg[0], dmb[0]
            ex.small_grads(done)
        w_p, w_dt = ex.weight("w_in", l)
        g_dt = mm("mm_dt_dw", l, sv["h_bf"], ddt, ta=True, out_dtype=BF16)
        ex.grad("w_in", l, _unpack_w_in(mm("mm_in_dw", l, sv["h_bf"], dproj, ta=True, out_dtype=BF16), g_dt))
        dh = mm("mm_in_dx", l, dproj, w_p, tb=True, add=dres, extra=(ddt, w_dt))

    return loss, dh.reshape(nb, s, d)


def _pack_flat(arrays, rows):
    flat = jnp.concatenate([a.reshape(-1) for a in arrays])
    return jnp.pad(flat, (0, rows * 128 - flat.shape[0])).reshape(rows, 128)


def _unpack_flat(packed, shapes):
    lead = packed.shape[:-2]
    flat = packed.reshape(lead + (-1,))
    out, pos = [], 0
    for shape in shapes:
        n = math.prod(shape)
        out.append(flat[..., pos:pos + n].reshape(lead + tuple(shape)))
        pos += n
    return out


def _small_rows(n_elems):
    return -(-n_elems // (128 * SMALL_ROW_TILE)) * SMALL_ROW_TILE


def _from_shards(name, gathered):
    _, a, b = gathered.shape
    if name == "w_in":
        return tuple(_pack_w_in(gathered))
    if name in BIG_COL_SHARDED:
        return _join_columns(gathered)
    return gathered.reshape(N_DEV * a, b)


def _to_shards(name, g):
    if name == "w_in":
        return g
    if name in BIG_COL_SHARDED:
        return _split_columns(g)
    a, b = g.shape
    return g.reshape(N_DEV, a // N_DEV, b)


def _join_columns(gathered):
    _, r, b = gathered.shape
    tr = _pick_tile(r, (128,))

    def body(g_ref, o_ref):
        o_ref[...] = jnp.concatenate([g_ref[j].astype(F32) for j in range(N_DEV)], axis=1).astype(o_ref.dtype)

    return _call("join_columns", body, grid=(r // tr,), ins=[gathered],
                 in_specs=[pl.BlockSpec((N_DEV, tr, b), lambda i: (0, i, 0))],
                 out_specs=[pl.BlockSpec((tr, N_DEV * b), lambda i: (i, 0))],
                 out_shape=[jax.ShapeDtypeStruct((r, N_DEV * b), gathered.dtype)])[0][0]


def _split_columns(full):
    r, nb = full.shape
    b = nb // N_DEV
    tr = _pick_tile(r, (128,))

    def body(f_ref, o_ref):
        w = f_ref[...].astype(F32)
        for j in range(N_DEV):
            o_ref[j] = w[:, j * b:(j + 1) * b].astype(o_ref.dtype)

    return _call("split_columns", body, grid=(r // tr,), ins=[full],
                 in_specs=[pl.BlockSpec((tr, nb), lambda i: (i, 0))],
                 out_specs=[pl.BlockSpec((N_DEV, tr, b), lambda i: (0, i, 0))],
                 out_shape=[jax.ShapeDtypeStruct((N_DEV, r, b), full.dtype)])[0][0]


class _MeshExchange:
    def __init__(self, shards_bf16, first):
        self.shards = shards_bf16
        self.full = dict(first)
        self.pieces = {}
        self.grads = {}
        self.to_send = {}
        self.received = {}
        self.small = None
        self.small_gathered = None

    def weight(self, name, l):
        if (name, l) not in self.full:
            got = jnp.concatenate([self.pieces[(name, l, q)] for q in range(W_IN_PIECES)], axis=1)
            self.full[(name, l)] = _from_shards(name, got)
        return self.full[(name, l)]

    def grad(self, name, l, g):
        self.grads[(name, l)] = g

    def small_grads(self, done):
        self.small = done

    def partial_sums(self, name, l):
        if (name, l, None) in self.received:
            return self.received[(name, l, None)]
        return jnp.concatenate([self.received[(name, l, q)] for q in range(W_IN_PIECES)], axis=1)

    def _slices(self, name, l, piece):
        n_rows = D_MODEL // W_IN_PIECES
        key, rows = (name, l), None if piece is None else (piece * n_rows, n_rows)
        if key not in self.to_send:
            self.to_send[key] = _to_shards(name, self.grads[key])
        return self.to_send[key], rows

    def before(self, call, l):
        comm = _Comm()
        for name, layer, piece in GATHER_PLAN.get((call, l), ()):
            n_rows = D_MODEL // W_IN_PIECES
            comm.gathers.append((self.shards[name], layer, None if piece is None else (piece * n_rows, n_rows)))
        if (call, l) == SMALL_GATHER_CALL:
            names = SMALL_REP + SMALL_SH
            rows = _small_rows(sum(math.prod(self.small[n].shape) for n in names))
            comm.gathers.append((_pack_flat([self.small[n] for n in names], rows), None, None))
        for name, layer, piece in SCATTER_PLAN.get((call, l), ()):
            comm.scatters.append(self._slices(name, layer, piece))
        return comm if comm.gathers or comm.scatters else None

    def after(self, call, l, outs):
        gathers = list(GATHER_PLAN.get((call, l), ()))
        for (name, layer, piece), out in zip(gathers, outs):
            if piece is None:
                self.full[(name, layer)] = _from_shards(name, out)
            else:
                self.pieces[(name, layer, piece)] = out
        outs = outs[len(gathers):]
        if (call, l) == SMALL_GATHER_CALL:
            self.small_gathered = outs[0]
            outs = outs[1:]
        for item, out in zip(SCATTER_PLAN.get((call, l), ()), outs):
            self.received[item] = out


def kernel(x, mem, mem_ln_g, mem_ln_b, w_in, sg_ln_g, sg_ln_b, sg_w, sg_b, conv_w, conv_b, dt_bias, a_log, d_skip, ssm_norm_g, p_a, p_b, w_mix_o, w_xq, w_xkv, w_xo, w_ffn_in, w_ffn_out, ln_g, ln_b, loss_target, m_mem_ln_g, m_mem_ln_b, m_w_in, m_sg_ln_g, m_sg_ln_b, m_sg_w, m_sg_b, m_conv_w, m_conv_b, m_dt_bias, m_a_log, m_d_skip, m_ssm_norm_g, m_p_a, m_p_b, m_w_mix_o, m_w_xq, m_w_xkv, m_w_xo, m_w_ffn_in, m_w_ffn_out, m_ln_g, m_ln_b, v_mem_ln_g, v_mem_ln_b, v_w_in, v_sg_ln_g, v_sg_ln_b, v_sg_w, v_sg_b, v_conv_w, v_conv_b, v_dt_bias, v_a_log, v_d_skip, v_ssm_norm_g, v_p_a, v_p_b, v_w_mix_o, v_w_xq, v_w_xkv, v_w_xo, v_w_ffn_in, v_w_ffn_out, v_ln_g, v_ln_b):
    args = dict(locals())
    w = {n: args[n] for n in WEIGHTS}
    m = {n: args["m_" + n] for n in WEIGHTS}
    v = {n: args["v_" + n] for n in WEIGHTS}
    me = 4 * lax.axis_index("x") + 2 * lax.axis_index("y") + lax.axis_index("c")

    shards = {n: w[n].astype(BF16) for n in BIG}
    sh_shapes = [w[n].shape for n in SMALL_SH]
    first = _Comm()
    first.gathers.append((shards["w_in"], 0, None))
    first.gathers.append((_pack_flat([w[n] for n in SMALL_SH], _small_rows(sum(math.prod(s) for s in sh_shapes))), None,
                          None))
    w_in0, small_sh = _comm_only("gather_first", first)
    small = {n: w[n] for n in SMALL_REP}
    for n, sh in zip(SMALL_SH, _unpack_flat(small_sh, sh_shapes)):
        small[n] = sh.transpose(1, 2, 0, 3).reshape(sh.shape[1], sh.shape[2], N_DEV * sh.shape[3])

    ex = _MeshExchange(shards, {("w_in", 0): _from_shards("w_in", w_in0)})
    loss, grad_x = _run_step(x, mem, loss_target, small, ex)
    loss = lax.psum(loss[0, 0], ("x", "y", "c"))

    out = {}
    for n in BIG[1:] + BIG[:1]:
        comm = ex.before("adamw_" + n, 0)
        out[n], comm_outs = _adamw_sharded("adamw_" + n, [ex.partial_sums(n, l) for l in range(DEPTH)], w[n], m[n], v[n],
                                           comm=comm)
        if comm is not None:
            ex.after("adamw_" + n, 0, comm_outs)
    names = SMALL_REP + SMALL_SH
    g_small = dict(zip(names, _unpack_flat(_sum_parts("sum_small_grads", ex.small_gathered),
                                           [ex.small[n].shape for n in names])))
    for n in names:
        g = g_small[n]
        if n in SMALL_SH:
            width = w[n].shape[-1]
            g = lax.dynamic_slice_in_dim(g, me * width, width, axis=-1)
        two_d = (-1, w[n].shape[-1])
        res = _adamw_small("adamw_" + n, g.reshape(two_d), w[n].reshape(two_d), m[n].reshape(two_d), v[n].reshape(two_d))
        out[n] = [g] + [r.reshape(w[n].shape) for r in res]

    results = []
    for k in range(4):
        results.extend(out[n][k] for n in WEIGHTS)
    return (loss, grad_x, *results)
```

```python
import functools
import math

import jax
import jax.numpy as jnp
from jax import lax
from jax.experimental import pallas as pl
from jax.experimental.pallas import tpu as pltpu

F32 = jnp.float32
BF16 = jnp.bfloat16
HIGHEST = lax.Precision.HIGHEST

N_DEV = 8
D_MODEL = 1024
DEPTH = 2
MEM_LEN = 256
CHUNK = 128
SG_GROUPS = 8
SSM_INNER = 2048
SSM_HEADDIM = 64
SSM_HEADS = 32
SSM_STATE = 128
SSM_GROUPS = 4
SSM_RPG = 8
SSM_CONV = 4
SSM_CONV_DIM = 3072
X_HEADS = 4
X_HEADDIM = 256
FFN_HIDDEN = 2816
ALPHA = float((2 * DEPTH) ** 0.25)
LN_EPS = 1e-5
RMS_EPS = 1e-5
XBC_COL = 4096
DT_COL = 7168
GA_COL = 7200
IN_COLS = 9248
P_GATE = 4096
P_XBC = 6144
P_COLS = 9216
DT_LANES = 128

ADAM_LR = 0.001
ADAM_B1 = 0.9
ADAM_B2 = 0.999
ADAM_EPS = 1e-08
ADAM_WD = 0.01
ADAM_STEP = 10

VMEM_LIMIT = 48 * 1024 * 1024
SMALL_ROW_TILE = 256

BIG = ("w_in", "p_a", "p_b", "w_mix_o", "w_xq", "w_xkv", "w_xo", "w_ffn_in", "w_ffn_out")
BIG_COL_SHARDED = ("w_in", "w_xkv", "w_ffn_in")
SMALL_REP = ("mem_ln_g", "mem_ln_b", "sg_ln_g", "sg_ln_b", "sg_w", "sg_b", "conv_b", "dt_bias", "a_log", "d_skip",
             "ssm_norm_g")
SMALL_SH = ("conv_w", "ln_g", "ln_b")
WEIGHTS = ("mem_ln_g", "mem_ln_b", "w_in", "sg_ln_g", "sg_ln_b", "sg_w", "sg_b", "conv_w", "conv_b", "dt_bias", "a_log",
           "d_skip", "ssm_norm_g", "p_a", "p_b", "w_mix_o", "w_xq", "w_xkv", "w_xo", "w_ffn_in", "w_ffn_out", "ln_g", "ln_b")

W_IN_PIECES = 4
GATHER_PLAN = {("sgu_fwd", 0): [("w_in", 1, 0)], ("conv_fwd", 0): [("w_in", 1, 1)],
               ("mm_ffn_in", 0): [("w_in", 1, 2), ("w_ffn_out", 0, None)], ("mm_ffn_out_ln", 0): [("w_in", 1, 3)],
               ("conv_fwd", 1): [("w_ffn_out", 1, None)]}
SCATTER_PLAN = {("mm_ffn_in_dw", 0): [("w_in", 1, 2)], ("mm_ffn_in_dx", 0): [("w_in", 1, 3)],
                ("adamw_w_ffn_in", 0): [("w_in", 0, 2)], ("adamw_w_xkv", 0): [("w_in", 0, 3)]}
for _l in range(DEPTH):
    GATHER_PLAN[("mm_in", _l)] = [(n, _l, None) for n in ("p_a", "p_b", "w_mix_o", "w_xq", "w_xkv", "w_xo")]
    GATHER_PLAN[("ssd_fwd", _l)] = [("w_ffn_in", _l, None)]
    SCATTER_PLAN[("swiglu_bwd", _l)] = [("w_ffn_out", _l, None)]
    SCATTER_PLAN[("sgu_bwd", _l)] = [("w_mix_o", _l, None), ("p_a", _l, None), ("w_xo", _l, None)]
    SCATTER_PLAN[("ssd_bwd", _l)] = [("w_ffn_in", _l, None), ("w_xkv", _l, None), ("w_xq", _l, None)]
    SCATTER_PLAN[("conv_bwd", _l)] = [("p_b", _l, None)]
    SCATTER_PLAN[("mm_in_dx", _l)] = [("w_in", _l, 0), ("w_in", _l, 1)]
SMALL_GATHER_CALL = ("mm_in_dw", 0)


def _layer_norm(x, g, b):
    mu = jnp.mean(x, axis=-1, keepdims=True)
    xc = x - mu
    var = jnp.mean(xc * xc, axis=-1, keepdims=True)
    return xc * lax.rsqrt(var + LN_EPS) * g + b


def _gelu(x):
    return 0.5 * x * (1.0 + lax.erf(x * (1.0 / math.sqrt(2.0))))


def _sigmoid(x):
    return 0.5 * jnp.tanh(0.5 * x) + 0.5


def _silu(x):
    return x * _sigmoid(x)


def _softplus(x):
    return jnp.maximum(x, 0.0) + jnp.log1p(jnp.exp(-jnp.abs(x)))


def _causal_mask():
    r = lax.broadcasted_iota(jnp.int32, (CHUNK, CHUNK), 0)
    c = lax.broadcasted_iota(jnp.int32, (CHUNK, CHUNK), 1)
    return r >= c


def _sgu_block(uv, ln_g, ln_b, w, sb):
    gu = _gelu(uv[:, :D_MODEL])
    vn = _layer_norm(_gelu(uv[:, D_MODEL:]), ln_g, ln_b).astype(BF16)
    causal = _causal_mask()
    width = D_MODEL // SG_GROUPS
    wgs = [jnp.where(causal, w[g], 0.0).astype(BF16) for g in range(SG_GROUPS)]
    chunks = []
    for c in range(uv.shape[0] // CHUNK):
        vc = vn[c * CHUNK:(c + 1) * CHUNK, :]
        chunks.append(jnp.concatenate(
            [jnp.dot(wgs[g], vc[:, g * width:(g + 1) * width], preferred_element_type=F32) + sb[:, g:g + 1]
             for g in range(SG_GROUPS)], axis=1))
    return (gu * jnp.concatenate(chunks, axis=0),)


GROUP_W = SSM_RPG * SSM_HEADDIM
NT_DIMS = (((1,), (1,)), ((), ()))
TN_DIMS = (((0,), (0,)), ((), ()))


def _mxu(a, b, dims=(((1,), (0,)), ((), ()))):
    return lax.dot_general(a.astype(BF16), b.astype(BF16), dims, preferred_element_type=F32)


def _head_expander():
    return (jnp.arange(SSM_INNER)[None, :] // SSM_HEADDIM == jnp.arange(128)[:, None]).astype(BF16)


def _bf16_terms(x, n):
    terms = []
    for _ in range(n):
        t = x.astype(BF16)
        terms.append(t)
        x = x - t.astype(F32)
    return terms


def _expand_heads(q, e):
    return sum(jnp.dot(t, e, preferred_element_type=F32) for t in _bf16_terms(q, 2))


def _reduce_heads(v, e):
    return sum(lax.dot_general(t, e, NT_DIMS, preferred_element_type=F32) for t in _bf16_terms(v, 2))


def _reduce_heads_of_column_sums(v, e):
    sums = jnp.broadcast_to(jnp.sum(v, axis=0, keepdims=True), (8, v.shape[1]))
    return _reduce_heads(sums, e)[0:1, :]


def _ssd_common(xc, dtraw, dt_bias, a_log, e):
    xs = xc[:, :SSM_INNER]
    pre = dtraw + dt_bias
    dt = _softplus(pre)
    a = -jnp.exp(a_log)
    r_i = lax.broadcasted_iota(jnp.int32, (CHUNK, CHUNK), 0)
    c_i = lax.broadcasted_iota(jnp.int32, (CHUNK, CHUNK), 1)
    tril = jnp.where(r_i >= c_i, 1.0, 0.0).astype(F32)
    cs = jnp.dot(tril, dt * a, precision=HIGHEST, preferred_element_type=F32)
    cs_last = cs[CHUNK - 1:CHUNK, :]
    decay_in = jnp.exp(cs)
    decay_st = jnp.exp(cs_last - cs)
    dt_x = _expand_heads(dt, e)
    w_st_x = _expand_heads(dt * decay_st, e)
    decay_in_x = _expand_heads(decay_in, e)
    return dict(xs=xs, pre=pre, dt=dt, a=a, lower=r_i >= c_i, upper=c_i >= r_i, cs=cs, cs_t=cs.T, decay_in=decay_in,
                decay_st=decay_st, chunk_decay=jnp.exp(cs_last), dt_x=dt_x, w_st_x=w_st_x, decay_in_x=decay_in_x,
                chunk_decay_x=decay_in_x[CHUNK - 1:CHUNK, :], xdt=xs * dt_x, x_st=(xs * w_st_x).astype(BF16),
                low=lax.broadcasted_iota(jnp.int32, (CHUNK, 128), 1) < SSM_HEADDIM)


def _pair_decay(c, h):
    return jnp.exp(jnp.where(c["lower"], c["cs"][:, h:h + 1] - c["cs_t"][h:h + 1, :], -1e30))


def _pair_decay_t(c, h):
    return jnp.exp(jnp.where(c["upper"], c["cs_t"][h:h + 1, :] - c["cs"][:, h:h + 1], -1e30))


def _ssd_forward(xc, dtraw, z, prev, dt_bias, a_log, d_skip_x, norm_g, e):
    c = _ssd_common(xc, dtraw, dt_bias, a_log, e)
    y_groups, new_states = [], []
    for g in range(SSM_GROUPS):
        lanes = slice(g * GROUP_W, (g + 1) * GROUP_W)
        bg = xc[:, SSM_INNER + g * SSM_STATE:SSM_INNER + (g + 1) * SSM_STATE]
        cg = xc[:, SSM_INNER + (SSM_GROUPS + g) * SSM_STATE:SSM_INNER + (SSM_GROUPS + g + 1) * SSM_STATE].astype(BF16)
        pg = prev[g * SSM_STATE:(g + 1) * SSM_STATE, :]
        cb = _mxu(cg, bg, NT_DIMS)
        y_in = _mxu(cg, pg) * c["decay_in_x"][:, lanes]
        new_states.append(pg * c["chunk_decay_x"][:, lanes] + _mxu(bg.T, c["x_st"][:, lanes]))
        pairs = []
        for j in range(SSM_RPG // 2):
            h0 = g * SSM_RPG + 2 * j
            xp = c["xdt"][:, 128 * (h0 // 2):128 * (h0 // 2 + 1)]
            pairs.append(_mxu(cb * _pair_decay(c, h0), jnp.where(c["low"], xp, 0.0))
                         + _mxu(cb * _pair_decay(c, h0 + 1), jnp.where(c["low"], 0.0, xp)))
        y_groups.append(jnp.concatenate(pairs, axis=1) + y_in)
    y_pre = jnp.concatenate(y_groups, axis=1) + c["xs"] * d_skip_x
    gated = y_pre * _silu(z)
    normed = [gated[:, g * GROUP_W:(g + 1) * GROUP_W] for g in range(SSM_GROUPS)]
    normed = [yg * lax.rsqrt(jnp.mean(yg * yg, axis=-1, keepdims=True) + RMS_EPS) for yg in normed]
    return jnp.concatenate(normed, axis=1) * norm_g, y_pre, jnp.concatenate(new_states, axis=0)


def _ssd_backward(xc, dtraw, z, prev, y_pre, dt_bias, a_log, d_skip_x, norm_g, e, dout, dnew):
    c = _ssd_common(xc, dtraw, dt_bias, a_log, e)
    xs = c["xs"]
    sig = _sigmoid(z)
    silu_z = z * sig
    gated = y_pre * silu_z
    d_gated, normed = [], []
    for g in range(SSM_GROUPS):
        lanes = slice(g * GROUP_W, (g + 1) * GROUP_W)
        yg = gated[:, lanes]
        r = lax.rsqrt(jnp.mean(yg * yg, axis=-1, keepdims=True) + RMS_EPS)
        n = yg * r
        gh = dout[:, lanes] * norm_g[:, lanes]
        d_gated.append(r * (gh - n * jnp.mean(gh * n, axis=-1, keepdims=True)))
        normed.append(n)
    d_gated = jnp.concatenate(d_gated, axis=1)
    dnorm_g = jnp.sum(dout * jnp.concatenate(normed, axis=1), axis=0, keepdims=True)
    dy = d_gated * silu_z
    dz = d_gated * y_pre * (sig * (1.0 + z * (1.0 - sig)))
    dxs = dy * d_skip_x
    dd_skip = _reduce_heads_of_column_sums(dy * xs, e)

    lane = lax.broadcasted_iota(jnp.int32, (CHUNK, 128), 1)
    sub = lax.broadcasted_iota(jnp.int32, (8, 128), 0)
    dcs_neg = jnp.zeros((CHUNK, 128), F32)
    row_slabs = []
    dxdt, dx_st, d_decay_in_x, dprev, d_chunk_decay_x, db_all, dc_all = [], [], [], [], [], [], []
    for g in range(SSM_GROUPS):
        lanes = slice(g * GROUP_W, (g + 1) * GROUP_W)
        bg = xc[:, SSM_INNER + g * SSM_STATE:SSM_INNER + (g + 1) * SSM_STATE].astype(BF16)
        cg_f = xc[:, SSM_INNER + (SSM_GROUPS + g) * SSM_STATE:SSM_INNER + (SSM_GROUPS + g + 1) * SSM_STATE]
        cg = cg_f.astype(BF16)
        pg = prev[g * SSM_STATE:(g + 1) * SSM_STATE, :]
        dng = dnew[g * SSM_STATE:(g + 1) * SSM_STATE, :]
        dy_g = dy[:, lanes]
        cb_t = _mxu(bg, cg, NT_DIMS)
        t1 = (dy_g * c["decay_in_x"][:, lanes]).astype(BF16)
        d_decay_in_x.append(dy_g * _mxu(cg, pg))
        dc = _mxu(t1, pg, NT_DIMS)
        dprev.append(_mxu(cg_f.T, t1) + dng * c["chunk_decay_x"][:, lanes])
        d_chunk_decay_x.append(dng * pg)
        db = _mxu(c["x_st"][:, lanes], dng, NT_DIMS)
        dx_st.append(_mxu(bg, dng))
        dcb_t = jnp.zeros((CHUNK, CHUNK), F32)
        rows = []
        for j in range(SSM_RPG // 2):
            h0 = g * SSM_RPG + 2 * j
            blk = slice(128 * (h0 // 2), 128 * (h0 // 2 + 1))
            xp = c["xdt"][:, blk]
            dyp = dy[:, blk].astype(BF16)
            pair_dx = []
            for k, xk in enumerate((jnp.where(c["low"], xp, 0.0), jnp.where(c["low"], 0.0, xp))):
                dec_t = _pair_decay_t(c, h0 + k)
                pair_dx.append(_mxu(cb_t * dec_t, dyp))
                dml_t = _mxu(xk, dyp, NT_DIMS) * dec_t
                dcb_t = dcb_t + dml_t
                dseg_t = dml_t * cb_t
                dcs_neg = dcs_neg + jnp.where(lane == h0 + k, jnp.sum(dseg_t, axis=-1, keepdims=True), 0.0)
                rows.append(jnp.sum(dseg_t, axis=0, keepdims=True))
            dxdt.append(jnp.where(c["low"], pair_dx[0], pair_dx[1]))
        slab = jnp.zeros((8, 128), F32)
        for r in range(SSM_RPG):
            slab = slab + jnp.where(sub == r, rows[r], 0.0)
        row_slabs.append(slab)
        dc_all.append(dc + _mxu(dcb_t.T, bg))
        db_all.append(db + _mxu(dcb_t, cg))
    dxdt = jnp.concatenate(dxdt, axis=1)
    dx_st = jnp.concatenate(dx_st, axis=1)
    by_head = jnp.concatenate(row_slabs + [jnp.zeros((CHUNK - SSM_HEADS, 128), F32)], axis=0)
    dcs = by_head.T - dcs_neg
    dxs = dxs + dxdt * c["dt_x"] + dx_st * c["w_st_x"]
    ddt = _reduce_heads(dxdt * xs, e)
    dw_st = _reduce_heads(dx_st * xs, e)
    dcs = dcs + _reduce_heads(jnp.concatenate(d_decay_in_x, axis=1), e) * c["decay_in"]
    ddt = ddt + dw_st * c["decay_st"]
    d_log_st = dw_st * c["dt"] * c["decay_st"]
    dcs = dcs - d_log_st
    d_chunk_decay = _reduce_heads_of_column_sums(jnp.concatenate(d_chunk_decay_x, axis=1), e)
    dcs_last = jnp.sum(d_log_st, axis=0, keepdims=True) + d_chunk_decay * c["chunk_decay"]
    row = lax.broadcasted_iota(jnp.int32, (CHUNK, 128), 0)
    dcs = dcs + jnp.where(row == CHUNK - 1, dcs_last, 0.0)
    triu = jnp.where(c["upper"], 1.0, 0.0).astype(F32)
    dda = jnp.dot(triu, dcs, precision=HIGHEST, preferred_element_type=F32)
    ddt = ddt + dda * c["a"]
    da_log = jnp.sum(dda * c["dt"], axis=0, keepdims=True) * c["a"]
    dpre = ddt * _sigmoid(c["pre"])
    dxc = jnp.concatenate([dxs] + db_all + dc_all, axis=1)
    return (dxc, dpre, dz, jnp.concatenate(dprev, axis=0), jnp.sum(dpre, axis=0, keepdims=True), da_log, dd_skip,
            dnorm_g)


def _conv_block(x, w, b):
    rows = lax.broadcasted_iota(jnp.int32, x.shape, 0)
    acc = x * w[SSM_CONV - 1:SSM_CONV, :] + b
    for k in range(SSM_CONV - 1):
        shift = SSM_CONV - 1 - k
        acc = acc + _shift_rows(x, rows, shift) * w[k:k + 1, :]
    return (_silu(acc),)


@functools.partial(jax.custom_vjp, nondiff_argnums=(2,))
def _shift_rows(x, rows, shift):
    return jnp.where(rows >= shift, pltpu.roll(x, shift, 0), 0.0)


def _shift_rows_fwd(x, rows, shift):
    return _shift_rows(x, rows, shift), rows


def _shift_rows_bwd(shift, rows, g):
    n = g.shape[0]
    return jnp.where(rows < n - shift, pltpu.roll(g, n - shift, 0), 0.0), None


_shift_rows.defvjp(_shift_rows_fwd, _shift_rows_bwd)


def _merge_block(gates, br_a, br_b):
    return (_sigmoid(gates[:, :D_MODEL]) * br_a + _sigmoid(gates[:, D_MODEL:]) * br_b,)


def _lnres_block(x, y, g, b):
    return (_layer_norm(ALPHA * x + y, g, b),)


def _memln_block(x, g, b):
    return (_layer_norm(x, g, b),)


def _attn_block(q, kv):
    outs = []
    for h in range(X_HEADS):
        qh = q[:, h * X_HEADDIM:(h + 1) * X_HEADDIM].astype(BF16)
        kh = kv[:, h * X_HEADDIM:(h + 1) * X_HEADDIM].astype(BF16)
        vh = kv[:, D_MODEL + h * X_HEADDIM:D_MODEL + (h + 1) * X_HEADDIM].astype(BF16)
        s = lax.dot_general(qh, kh, (((1,), (1,)), ((), ())), preferred_element_type=F32) * (X_HEADDIM ** -0.5)
        s = s - lax.stop_gradient(jnp.max(s, axis=-1, keepdims=True))
        e = jnp.exp(s)
        p = e / jnp.sum(e, axis=-1, keepdims=True)
        outs.append(jnp.dot(p.astype(BF16), vh, preferred_element_type=F32))
    return (jnp.concatenate(outs, axis=1),)


def _swiglu_block(gu):
    return (_silu(gu[:, :FFN_HIDDEN]) * gu[:, FFN_HIDDEN:],)


class _Comm:
    def __init__(self):
        self.gathers = []
        self.scatters = []

    @staticmethod
    def _rows(ref, rows):
        return ref if rows is None else ref.at[pl.ds(rows[0], rows[1])]

    def operands(self):
        ins = [a for a, _, _ in self.gathers] + [a for a, _ in self.scatters]
        shapes = []
        for a, idx, rows in self.gathers:
            blk = a.shape if idx is None else a.shape[1:]
            shapes.append(jax.ShapeDtypeStruct((N_DEV, blk[0] if rows is None else rows[1]) + tuple(blk[1:]), a.dtype))
        for a, rows in self.scatters:
            shapes.append(jax.ShapeDtypeStruct((N_DEV, a.shape[1] if rows is None else rows[1]) + tuple(a.shape[2:]),
                                               a.dtype))
        scratch = []
        for n in (len(self.gathers), len(self.scatters)):
            if n:
                scratch += [pltpu.SemaphoreType.DMA((7 * n,)), pltpu.SemaphoreType.DMA((7 * n,)),
                            pltpu.SemaphoreType.DMA((n,))]
        return ins, shapes, scratch

    def _split(self, in_refs, out_refs, sems):
        ng = len(self.gathers)
        g_sems = sems[:3] if ng else None
        s_sems = sems[3:] if ng else sems
        return in_refs[:ng], in_refs[ng:], out_refs[:ng], out_refs[ng:], g_sems, s_sems

    def _gather_copies(self, i, src_ref, out_ref, sems):
        send_sems, recv_sems, local_sems = sems
        x, y, c = lax.axis_index("x"), lax.axis_index("y"), lax.axis_index("c")
        me, sibling = (x, y, c), (x, y, 1 - c)
        chips = [(1 - x, y), (x, 1 - y), (1 - x, 1 - y)]
        _, idx, rows = self.gathers[i]
        src = self._rows(src_ref if idx is None else src_ref.at[idx], rows)

        def slot(px, py, pc):
            return out_ref.at[4 * px + 2 * py + pc]

        def copy(k, blk, to, from_src=False):
            return pltpu.make_async_remote_copy(
                src_ref=src if from_src else slot(*blk), dst_ref=slot(*blk), send_sem=send_sems.at[7 * i + k],
                recv_sem=recv_sems.at[7 * i + k], device_id=to, device_id_type=pl.DeviceIdType.MESH)

        mine = pltpu.make_async_copy(src, slot(*me), local_sems.at[i])
        first = [copy(0, me, sibling, True)] + [copy(1 + j, me, (*chip, c), True) for j, chip in enumerate(chips)]
        passed = [copy(4 + j, (*chip, c), sibling) for j, chip in enumerate(chips)]
        arrivals = [copy(1 + j, (*chip, c), me) for j, chip in enumerate(chips)]
        from_sibling = [copy(0, sibling, me)] + [copy(4 + j, (*chip, 1 - c), me) for j, chip in enumerate(chips)]
        return mine, first, passed, arrivals, from_sibling

    def _scatter_copies(self, i, src_ref, out_ref, sems):
        send_sems, recv_sems, local_sems = sems
        x, y, c = lax.axis_index("x"), lax.axis_index("y"), lax.axis_index("c")
        me = 4 * x + 2 * y + c
        rows = self.scatters[i][1]
        mine = pltpu.make_async_copy(self._rows(src_ref.at[me], rows), out_ref.at[me], local_sems.at[i])
        copies = []
        for k in range(1, N_DEV):
            px = 1 - x if k & 4 else x
            py = 1 - y if k & 2 else y
            pc = 1 - c if k & 1 else c
            copies.append(pltpu.make_async_remote_copy(
                src_ref=self._rows(src_ref.at[4 * px + 2 * py + pc], rows), dst_ref=out_ref.at[me],
                send_sem=send_sems.at[7 * i + k - 1], recv_sem=recv_sems.at[7 * i + k - 1], device_id=(px, py, pc),
                device_id_type=pl.DeviceIdType.MESH))
        return mine, copies

    def start(self, in_refs, out_refs, sems):
        g_in, s_in, g_out, s_out, g_sems, s_sems = self._split(in_refs, out_refs, sems)
        for i in range(len(self.gathers)):
            mine, first, _, _, _ = self._gather_copies(i, g_in[i], g_out[i], g_sems)
            mine.start()
            for cp in first:
                cp.start()
        for i in range(len(self.scatters)):
            mine, copies = self._scatter_copies(i, s_in[i], s_out[i], s_sems)
            mine.start()
            for cp in copies:
                cp.start()

    def finish(self, in_refs, out_refs, sems):
        g_in, s_in, g_out, s_out, g_sems, s_sems = self._split(in_refs, out_refs, sems)
        parts = [self._gather_copies(i, g_in[i], g_out[i], g_sems) for i in range(len(self.gathers))]
        for j in range(3):
            for _, _, passed, arrivals, _ in parts:
                arrivals[j].wait_recv()
                passed[j].start()
        for mine, first, passed, _, from_sibling in parts:
            for cp in from_sibling:
                cp.wait_recv()
            for cp in first + passed:
                cp.wait_send()
            mine.wait()
        for i in range(len(self.scatters)):
            mine, copies = self._scatter_copies(i, s_in[i], s_out[i], s_sems)
            for cp in copies:
                cp.wait_recv()
            for cp in copies:
                cp.wait_send()
            mine.wait()


def _params(grid):
    return pltpu.CompilerParams(dimension_semantics=("arbitrary",) * len(grid), vmem_limit_bytes=VMEM_LIMIT)


def _call(name, body, *, grid, ins, in_specs, out_shape, out_specs, scratch=(), comm=None, aliases=None):
    n_in, n_out, n_scr = len(ins), len(out_shape), len(scratch)
    aliases = aliases or {}
    if comm is None:
        outs = pl.pallas_call(body, grid=grid, in_specs=list(in_specs), out_specs=list(out_specs),
                              out_shape=list(out_shape), scratch_shapes=list(scratch), name=name,
                              input_output_aliases=aliases, compiler_params=_params(grid))(*ins)
        return list(outs), []
    c_ins, c_shapes, c_scratch = comm.operands()
    nci, nco = len(c_ins), len(c_shapes)
    anywhere = pl.BlockSpec(memory_space=pl.ANY)

    def carrier(*refs):
        main_in, comm_in = refs[:n_in], refs[n_in:n_in + nci]
        o0 = n_in + nci
        main_out, comm_out = refs[o0:o0 + n_out], refs[o0 + n_out:o0 + n_out + nco]
        s0 = o0 + n_out + nco
        main_scr, comm_scr = refs[s0:s0 + n_scr], refs[s0 + n_scr:]
        first = pl.program_id(0) == 0
        last = pl.program_id(0) == grid[0] - 1
        for ax in range(1, len(grid)):
            first = first & (pl.program_id(ax) == 0)
            last = last & (pl.program_id(ax) == grid[ax] - 1)

        @pl.when(first)
        def _():
            comm.start(comm_in, comm_out, comm_scr)

        body(*main_in, *main_out, *main_scr)

        @pl.when(last)
        def _():
            comm.finish(comm_in, comm_out, comm_scr)

    outs = pl.pallas_call(carrier, grid=grid, in_specs=list(in_specs) + [anywhere] * nci,
                          out_specs=list(out_specs) + [anywhere] * nco, out_shape=list(out_shape) + c_shapes,
                          scratch_shapes=list(scratch) + c_scratch, name=name, input_output_aliases=aliases,
                          compiler_params=_params(grid))(*ins, *c_ins)
    return list(outs[:n_out]), list(outs[n_out:])


def _comm_only(name, comm):
    c_ins, c_shapes, c_scratch = comm.operands()
    nci, nco = len(c_ins), len(c_shapes)
    anywhere = pl.BlockSpec(memory_space=pl.ANY)

    def body(*refs):
        comm.start(refs[:nci], refs[nci:nci + nco], refs[nci + nco:])
        comm.finish(refs[:nci], refs[nci:nci + nco], refs[nci + nco:])

    return list(pl.pallas_call(body, in_specs=[anywhere] * nci, out_specs=[anywhere] * nco, out_shape=c_shapes,
                               scratch_shapes=c_scratch, name=name)(*c_ins))


def _stage_fwd(name, f, grid, ins, in_specs, out_shapes, out_specs, comm=None):
    n_in = len(ins)

    def body(*refs):
        res = f(*[r[...].astype(F32) for r in refs[:n_in]])
        for o_ref, val in zip(refs[n_in:], res):
            o_ref[...] = val.astype(o_ref.dtype)

    return _call(name, body, grid=grid, ins=ins, in_specs=in_specs, out_shape=out_shapes, out_specs=out_specs, comm=comm)


def _stage_bwd(name, f, grid, ins, in_specs, cts, ct_specs, grads, comm=None, ct_product=None):
    n_in = len(ins)
    flat_cts = [c for group in cts for c in group]
    flat_ct_specs = [s for group, spec in zip(cts, ct_specs) for s in (spec,) * len(group)]
    if ct_product is not None:
        assert not cts
        flat_cts = [ct_product[0], ct_product[2]]
        flat_ct_specs = [ct_product[1], pl.BlockSpec(ct_product[2].shape, lambda *_: (0, 0))]
    n_ct = len(flat_cts)
    diff = [g[0] for g in grads]
    buffers = [(k, g[4]) for k, g in enumerate(grads) if len(g) > 4]
    n_buf = len(buffers)

    def body(*refs):
        vals = [r[...].astype(F32) for r in refs[:n_in]]
        ct_refs = refs[n_in:n_in + n_ct]
        g_refs = refs[n_in + n_ct + n_buf:]
        ct_vals, pos = [], 0
        if ct_product is not None:
            ct_vals.append(lax.dot_general(ct_refs[0][...].astype(BF16), ct_refs[1][...].astype(BF16), NT_DIMS,
                                           preferred_element_type=F32))
        for group in cts:
            acc = ct_refs[pos][...].astype(F32)
            for j in range(1, len(group)):
                acc = acc + ct_refs[pos + j][...].astype(F32)
            ct_vals.append(acc)
            pos += len(group)

        def g_fn(*dvals):
            full = list(vals)
            for i, dv in zip(diff, dvals):
                full[i] = dv
            return f(*full)

        _, vjp = jax.vjp(g_fn, *[vals[i] for i in diff])
        gvals = vjp(tuple(ct_vals))
        for gspec, g_ref, gval in zip(grads, g_refs, gvals):
            acc_axes = gspec[1]
            if not acc_axes:
                g_ref[...] = gval.astype(g_ref.dtype)
            else:
                first = pl.program_id(acc_axes[0]) == 0
                for ax in acc_axes[1:]:
                    first = first & (pl.program_id(ax) == 0)

                @pl.when(first)
                def _():
                    g_ref[...] = jnp.zeros_like(g_ref)

                g_ref[...] += gval.astype(g_ref.dtype)

    out_shapes, out_specs = [], []
    for gspec in grads:
        shape, spec = gspec[3] if len(gspec) > 3 else (ins[gspec[0]].shape, in_specs[gspec[0]])
        out_shapes.append(jax.ShapeDtypeStruct(shape, gspec[2]))
        out_specs.append(spec)
    anywhere = pl.BlockSpec(memory_space=pl.ANY)
    return _call(name, body, grid=grid, ins=list(ins) + flat_cts + [b for _, b in buffers],
                 in_specs=list(in_specs) + flat_ct_specs + [anywhere] * n_buf, out_shape=out_shapes, out_specs=out_specs,
                 comm=comm, aliases={n_in + n_ct + j: k for j, (k, _) in enumerate(buffers)})


def _pick_tile(n, candidates):
    for c in candidates:
        if n % c == 0:
            return c
    return n


def _matmul(name, a, b, *, ta=False, tb=False, add=None, extra=None, out_dtype=F32, comm=None):
    if ta:
        k_dim, m = a.shape
    else:
        m, k_dim = a.shape
    n = b.shape[0] if tb else b.shape[1]
    assert (b.shape[1] if tb else b.shape[0]) == k_dim and not (ta and tb)
    tm = _pick_tile(m, (1024, 1408, 512, 256, 128))
    tn = _pick_tile(n, (1024, 1408, 512, 256, 128))
    if ta:
        tk = _pick_tile(k_dim, (1024, 512, 256, 128))
    elif k_dim <= 2816:
        tk = k_dim
    else:
        tk = _pick_tile(k_dim, (1408, 1024, 512, 256, 128))
    nk = k_dim // tk
    grid = (m // tm, n // tn, nk)
    a_spec = pl.BlockSpec((tk, tm), lambda i, j, k: (k, i)) if ta else pl.BlockSpec((tm, tk), lambda i, j, k: (i, k))
    b_spec = pl.BlockSpec((tn, tk), lambda i, j, k: (j, k)) if tb else pl.BlockSpec((tk, tn), lambda i, j, k: (k, j))
    o_spec = pl.BlockSpec((tm, tn), lambda i, j, k: (i, j))
    dims = (((0 if ta else 1,), (1 if tb else 0,)), ((), ()))
    has_add = add is not None
    has_extra = extra is not None

    def body(*refs):
        a_ref, b_ref = refs[0], refs[1]
        add_ref = refs[2] if has_add else None
        o_ref, acc_ref = refs[-2], refs[-1]
        k = pl.program_id(2)
        part = lax.dot_general(a_ref[...].astype(BF16), b_ref[...].astype(BF16), dims, preferred_element_type=F32)

        def finish(res):
            if has_add:
                res = res + add_ref[...].astype(F32)
            if has_extra:
                a2_ref, b2_ref = refs[2 + has_add], refs[3 + has_add]
                res = res + lax.dot_general(a2_ref[...].astype(BF16), b2_ref[...].astype(BF16), NT_DIMS,
                                            preferred_element_type=F32)
            o_ref[...] = res.astype(o_ref.dtype)

        if nk == 1:
            finish(part)
        else:
            @pl.when(k == 0)
            def _():
                acc_ref[...] = part

            @pl.when((k > 0) & (k < nk - 1))
            def _():
                acc_ref[...] += part

            @pl.when(k == nk - 1)
            def _():
                finish(acc_ref[...] + part)

    ins = [a, b] + ([add] if has_add else [])
    in_specs = [a_spec, b_spec] + ([o_spec] if has_add else [])
    if has_extra:
        k2 = extra[0].shape[1]
        ins += list(extra)
        in_specs += [pl.BlockSpec((tm, k2), lambda i, j, k: (i, 0)), pl.BlockSpec((tn, k2), lambda i, j, k: (j, 0))]
    acc_shape = (tm, tn) if nk > 1 else (8, 128)
    outs, comm_outs = _call(name, body, grid=grid, ins=ins, in_specs=in_specs,
                            out_shape=[jax.ShapeDtypeStruct((m, n), out_dtype)], out_specs=[o_spec],
                            scratch=[pltpu.VMEM(acc_shape, F32)], comm=comm)
    return outs[0], comm_outs


def _matmul_lnres(name, a_fn, a_ins, a_specs, tm, b, x, g, beta, comm=None):
    m = x.shape[0]
    k_dim, n = b.shape
    n_a = len(a_ins)
    row = lambda w: pl.BlockSpec((tm, w), lambda i: (i, 0))
    whole = lambda shape: pl.BlockSpec(shape, lambda i: (0, 0))

    def body(*refs):
        b_ref, x_ref, g_ref, beta_ref = refs[n_a:n_a + 4]
        y_ref, h_ref, hb_ref = refs[-3:]
        if a_fn is None:
            a = refs[0][...].astype(BF16)
        else:
            (a,) = a_fn(*[r[...].astype(F32) for r in refs[:n_a]])
            a = a.astype(BF16)
            refs[n_a + 4][...] = a
        y = jnp.dot(a, b_ref[...].astype(BF16), preferred_element_type=F32).astype(y_ref.dtype)
        y_ref[...] = y
        (h,) = _lnres_block(x_ref[...], y.astype(F32), g_ref[...], beta_ref[...])
        h_ref[...] = h
        hb_ref[...] = h.astype(hb_ref.dtype)

    sds = jax.ShapeDtypeStruct
    a_out = ([sds((m, k_dim), BF16)], [row(k_dim)]) if a_fn is not None else ([], [])
    return _call(name, body, grid=(m // tm,), ins=list(a_ins) + [b, x, g, beta],
                 in_specs=list(a_specs) + [whole((k_dim, n)), row(n), whole((1, n)), whole((1, n))],
                 out_shape=a_out[0] + [sds((m, n), BF16), sds((m, n), F32), sds((m, n), BF16)],
                 out_specs=a_out[1] + [row(n), row(n), row(n)], comm=comm)


SSD_STATE = (SSM_GROUPS * SSM_STATE, SSM_RPG * SSM_HEADDIM)


def _ssd_fwd(xc, dt_raw, proj, dt_bias, a_log, d_skip, norm_g, nb, nc, comm=None):
    t = xc.shape[0]
    row = lambda b, c: (b * nc + c, 0)
    par = lambda shape: pl.BlockSpec(shape, lambda b, c: (0, 0))

    def body(xc_ref, dt_ref, z_ref, dtb_ref, al_ref, ds_ref, ng_ref, e_ref, y_ref, ypre_ref, prev_ref, st_ref):
        @pl.when(pl.program_id(1) == 0)
        def _():
            st_ref[...] = jnp.zeros_like(st_ref)

        prev = st_ref[...]
        prev_ref[0, 0] = prev
        y, y_pre, new_state = _ssd_forward(xc_ref[...], dt_ref[...], z_ref[...].astype(F32), prev, dtb_ref[...],
                                           al_ref[...], ds_ref[...], ng_ref[...], e_ref[...])
        y_ref[...] = y.astype(y_ref.dtype)
        ypre_ref[...] = y_pre
        st_ref[...] = new_state

    return _call(
        "ssd_fwd", body, grid=(nb, nc), ins=[xc, dt_raw, proj, dt_bias, a_log, d_skip, norm_g, _head_expander()],
        in_specs=[pl.BlockSpec((CHUNK, SSM_CONV_DIM), row), pl.BlockSpec((CHUNK, 128), row),
                  pl.BlockSpec((CHUNK, SSM_INNER), lambda b, c: (b * nc + c, 1)),
                  par((1, 128)), par((1, 128)), par((1, SSM_INNER)), par((1, SSM_INNER)), par((128, SSM_INNER))],
        out_specs=[pl.BlockSpec((CHUNK, SSM_INNER), row), pl.BlockSpec((CHUNK, SSM_INNER), row),
                   pl.BlockSpec((1, 1) + SSD_STATE, lambda b, c: (b, c, 0, 0))],
        out_shape=[jax.ShapeDtypeStruct((t, SSM_INNER), BF16), jax.ShapeDtypeStruct((t, SSM_INNER), F32),
                   jax.ShapeDtypeStruct((nb, nc) + SSD_STATE, F32)],
        scratch=[pltpu.VMEM(SSD_STATE, F32)], comm=comm)


def _ssd_bwd(xc, dt_raw, proj, prevs, y_pre, dt_bias, a_log, d_skip, norm_g, dy, dproj, nb, nc, comm=None):
    t = xc.shape[0]
    row = lambda b, c: (b * nc + (nc - 1 - c), 0)
    par = lambda shape: pl.BlockSpec(shape, lambda b, c: (0, 0))
    z_spec = pl.BlockSpec((CHUNK, SSM_INNER), lambda b, c: (b * nc + (nc - 1 - c), 1))

    def body(xc_ref, dt_ref, z_ref, prev_ref, ypre_ref, dtb_ref, al_ref, ds_ref, ng_ref, e_ref, dy_ref, _,
             dxc_ref, ddt_ref, dz_ref, ddtb_ref, dal_ref, dds_ref, dng_ref, dst_ref):
        @pl.when(pl.program_id(1) == 0)
        def _():
            dst_ref[...] = jnp.zeros_like(dst_ref)

        @pl.when((pl.program_id(0) == 0) & (pl.program_id(1) == 0))
        def _():
            ddtb_ref[...] = jnp.zeros_like(ddtb_ref)
            dal_ref[...] = jnp.zeros_like(dal_ref)
            dds_ref[...] = jnp.zeros_like(dds_ref)
            dng_ref[...] = jnp.zeros_like(dng_ref)

        dxc, ddt, dz, dprev, ddtb, dal, dds, dng = _ssd_backward(
            xc_ref[...], dt_ref[...], z_ref[...].astype(F32), prev_ref[0, 0], ypre_ref[...], dtb_ref[...], al_ref[...],
            ds_ref[...], ng_ref[...], e_ref[...], dy_ref[...].astype(F32), dst_ref[...])
        dxc_ref[...] = dxc
        ddt_ref[...] = ddt.astype(ddt_ref.dtype)
        dz_ref[...] = dz.astype(dz_ref.dtype)
        dst_ref[...] = dprev
        ddtb_ref[...] += ddtb
        dal_ref[...] += dal
        dds_ref[...] += dds
        dng_ref[...] += dng

    return _call(
        "ssd_bwd", body, grid=(nb, nc),
        ins=[xc, dt_raw, proj, prevs, y_pre, dt_bias, a_log, d_skip, norm_g, _head_expander(), dy, dproj],
        in_specs=[pl.BlockSpec((CHUNK, SSM_CONV_DIM), row), pl.BlockSpec((CHUNK, DT_LANES), row), z_spec,
                  pl.BlockSpec((1, 1) + SSD_STATE, lambda b, c: (b, nc - 1 - c, 0, 0)),
                  pl.BlockSpec((CHUNK, SSM_INNER), row),
                  par((1, 128)), par((1, 128)), par((1, SSM_INNER)), par((1, SSM_INNER)), par((128, SSM_INNER)),
                  pl.BlockSpec((CHUNK, SSM_INNER), row), pl.BlockSpec(memory_space=pl.ANY)],
        out_specs=[pl.BlockSpec((CHUNK, SSM_CONV_DIM), row), pl.BlockSpec((CHUNK, DT_LANES), row), z_spec,
                   par((1, 128)), par((1, 128)), par((1, 128)), par((1, SSM_INNER))],
        out_shape=[jax.ShapeDtypeStruct((t, SSM_CONV_DIM), F32), jax.ShapeDtypeStruct((t, DT_LANES), BF16),
                   jax.ShapeDtypeStruct(dproj.shape, dproj.dtype), jax.ShapeDtypeStruct((1, 128), F32),
                   jax.ShapeDtypeStruct((1, 128), F32), jax.ShapeDtypeStruct((1, 128), F32),
                   jax.ShapeDtypeStruct((1, SSM_INNER), F32)],
        scratch=[pltpu.VMEM(SSD_STATE, F32)], comm=comm, aliases={11: 2})


def _loss_head(y, target):
    t, d = y.shape
    tm = _pick_tile(t, (256,))
    blk = pl.BlockSpec((tm, d), lambda i: (i, 0))

    def body(y_ref, t_ref, loss_ref, dy_ref):
        err = y_ref[...] - t_ref[...]
        dy_ref[...] = err * (1.0 / d)

        @pl.when(pl.program_id(0) == 0)
        def _():
            loss_ref[...] = jnp.zeros_like(loss_ref)

        loss_ref[...] += 0.5 * jnp.sum(jnp.mean(err * err, axis=-1, keepdims=True), axis=0, keepdims=True)

    return _call("loss_head", body, grid=(t // tm,), ins=[y, target], in_specs=[blk, blk],
                 out_specs=[pl.BlockSpec((1, 1), lambda i: (0, 0)), blk],
                 out_shape=[jax.ShapeDtypeStruct((1, 1), F32), jax.ShapeDtypeStruct((t, d), F32)])[0]


def _adamw_math(g, w, m, v):
    m_new = ADAM_B1 * m + (1.0 - ADAM_B1) * g
    v_new = ADAM_B2 * v + (1.0 - ADAM_B2) * jnp.square(g)
    m_hat = m_new / (1.0 - ADAM_B1 ** ADAM_STEP)
    v_hat = v_new / (1.0 - ADAM_B2 ** ADAM_STEP)
    delta = -ADAM_LR * (m_hat / (jnp.sqrt(v_hat) + ADAM_EPS) + ADAM_WD * w)
    return delta, m_new, v_new


def _adamw_sharded(name, parts, w, m, v, comm=None):
    _, a, b = w.shape
    tr = _pick_tile(a, (128,))
    nt = a // tr
    part_specs = [pl.BlockSpec((N_DEV, tr, b),
                               (lambda l, i, _k=k: (0, jnp.where(l == _k, i, jnp.where(l > _k, nt - 1, 0)), 0)))
                  for k in range(DEPTH)]
    blk = pl.BlockSpec((1, tr, b), lambda l, i: (l, i, 0))

    def body(*refs):
        p_refs = refs[:DEPTH]
        w_ref, m_ref, v_ref, g_out, d_out, m_out, v_out = refs[DEPTH:]
        for k in range(DEPTH):
            @pl.when(pl.program_id(0) == k)
            def _(p_ref=p_refs[k]):
                g = p_ref[0].astype(F32)
                for p in range(1, N_DEV):
                    g = g + p_ref[p].astype(F32)
                delta, m_new, v_new = _adamw_math(g, w_ref[0], m_ref[0], v_ref[0])
                g_out[0] = g
                d_out[0] = delta
                m_out[0] = m_new
                v_out[0] = v_new

    return _call(name, body, grid=(DEPTH, nt), ins=list(parts) + [w, m, v], in_specs=part_specs + [blk, blk, blk],
                 out_specs=[blk] * 4, out_shape=[jax.ShapeDtypeStruct(w.shape, F32)] * 4, comm=comm)


def _adamw_small(name, g, w, m, v):
    full = pl.BlockSpec(w.shape, lambda i: (0, 0))

    def body(g_ref, w_ref, m_ref, v_ref, d_out, m_out, v_out):
        delta, m_new, v_new = _adamw_math(g_ref[...], w_ref[...], m_ref[...], v_ref[...])
        d_out[...] = delta
        m_out[...] = m_new
        v_out[...] = v_new

    return _call(name, body, grid=(1,), ins=[g, w, m, v], in_specs=[full] * 4, out_specs=[full] * 3,
                 out_shape=[jax.ShapeDtypeStruct(w.shape, F32)] * 3)[0]


def _sum_parts(name, parts):
    n_parts, rows, cols = parts.shape
    tr = _pick_tile(rows, (512, 256, 128, 64, 32, 16, 8))

    def body(p_ref, o_ref):
        acc = p_ref[0]
        for p in range(1, n_parts):
            acc = acc + p_ref[p]
        o_ref[...] = acc

    return _call(name, body, grid=(rows // tr,), ins=[parts],
                 in_specs=[pl.BlockSpec((n_parts, tr, cols), lambda i: (0, i, 0))],
                 out_specs=[pl.BlockSpec((tr, cols), lambda i: (i, 0))],
                 out_shape=[jax.ShapeDtypeStruct((rows, cols), parts.dtype)])[0][0]


W_IN_SHARD = IN_COLS // N_DEV


def _pack_w_in(gathered):
    r = gathered.shape[1]
    tr = _pick_tile(r, (128,))

    def body(g_ref, main_ref, dt_ref):
        w = jnp.concatenate([g_ref[j].astype(F32) for j in range(N_DEV)], axis=1)
        main_ref[...] = jnp.concatenate([w[:, :XBC_COL], w[:, GA_COL:], w[:, XBC_COL:DT_COL]],
                                        axis=1).astype(main_ref.dtype)
        dt_ref[...] = jnp.concatenate([w[:, DT_COL:GA_COL], jnp.zeros((tr, DT_LANES - SSM_HEADS), F32)],
                                      axis=1).astype(dt_ref.dtype)

    return _call("pack_w_in", body, grid=(r // tr,), ins=[gathered],
                 in_specs=[pl.BlockSpec((N_DEV, tr, W_IN_SHARD), lambda i: (0, i, 0))],
                 out_specs=[pl.BlockSpec((tr, P_COLS), lambda i: (i, 0)), pl.BlockSpec((tr, DT_LANES), lambda i: (i, 0))],
                 out_shape=[jax.ShapeDtypeStruct((r, P_COLS), gathered.dtype),
                            jax.ShapeDtypeStruct((r, DT_LANES), gathered.dtype)])[0]


def _unpack_w_in(main, dt):
    r = main.shape[0]
    tr = _pick_tile(r, (128,))

    def body(main_ref, dt_ref, o_ref):
        main = main_ref[...].astype(F32)
        w = jnp.concatenate([main[:, :P_GATE], main[:, P_XBC:], dt_ref[...].astype(F32)[:, :SSM_HEADS],
                             main[:, P_GATE:P_XBC]], axis=1)
        for j in range(N_DEV):
            o_ref[j] = w[:, j * W_IN_SHARD:(j + 1) * W_IN_SHARD].astype(o_ref.dtype)

    return _call("unpack_w_in", body, grid=(r // tr,), ins=[main, dt],
                 in_specs=[pl.BlockSpec((tr, P_COLS), lambda i: (i, 0)), pl.BlockSpec((tr, DT_LANES), lambda i: (i, 0))],
                 out_specs=[pl.BlockSpec((N_DEV, tr, W_IN_SHARD), lambda i: (0, i, 0))],
                 out_shape=[jax.ShapeDtypeStruct((N_DEV, r, W_IN_SHARD), main.dtype)])[0][0]


def _pad_heads(v):
    return jnp.pad(v, (0, 128 - SSM_HEADS)).reshape(1, 128)


def _run_step(x, mem, target, small, ex):
    nb, s, d = x.shape
    t = nb * s
    nc = s // CHUNK
    rows = _pick_tile(t, (256,))
    rows_wide = _pick_tile(t, (512, 256))
    tq = _pick_tile(s, (512, 256))
    vec = lambda a: a.reshape(1, -1)
    full1 = lambda shape: pl.BlockSpec(shape, lambda i: (0,) * len(shape))
    row1 = lambda tm, w: pl.BlockSpec((tm, w), lambda i: (i, 0))
    sds = jax.ShapeDtypeStruct

    def mm(call, l, a, b, **kw):
        comm = ex.before(call, l)
        out, comm_outs = _matmul(call, a, b, comm=comm, **kw)
        if comm is not None:
            ex.after(call, l, comm_outs)
        return out

    def stage_bwd(call, l, *args, **kw):
        comm = ex.before(call, l)
        outs, comm_outs = _stage_bwd(call, *args, comm=comm, **kw)
        if comm is not None:
            ex.after(call, l, comm_outs)
        return outs

    def stage_fwd(call, l, *args):
        comm = ex.before(call, l)
        outs, comm_outs = _stage_fwd(call, *args, comm=comm)
        if comm is not None:
            ex.after(call, l, comm_outs)
        return outs

    mem_specs = [row1(256, d), full1((1, d)), full1((1, d))]
    mem_ins = [mem.reshape(nb * MEM_LEN, d), vec(small["mem_ln_g"]), vec(small["mem_ln_b"])]
    (mem_n,) = stage_fwd("memln_fwd", 0, _memln_block, (nb * MEM_LEN // 256,), mem_ins, mem_specs,
                          [sds((nb * MEM_LEN, d), BF16)], [row1(256, d)])

    h = x.reshape(t, d)
    h_bf = h.astype(BF16)
    ln_specs = [row1(rows_wide, d), row1(rows_wide, d), full1((1, d)), full1((1, d))]
    saved = []
    for l in range(DEPTH):
        sv = {"h_bf": h_bf}
        w_p, w_dt = ex.weight("w_in", l)
        proj = mm("mm_in", l, h_bf, w_p, out_dtype=BF16)
        dt_raw = mm("mm_dt", l, h_bf, w_dt)
        sv["proj"] = proj
        sgu_ins = [proj, vec(small["sg_ln_g"][l]), vec(small["sg_ln_b"][l]), small["sg_w"][l], small["sg_b"][l].T]
        sgu_specs = [pl.BlockSpec((rows, 2 * d), lambda i: (i, 0)), full1((1, d)), full1((1, d)),
                     full1((SG_GROUPS, CHUNK, CHUNK)), full1((CHUNK, SG_GROUPS))]
        (a_out,) = stage_fwd("sgu_fwd", l, _sgu_block, (t // rows,), sgu_ins, sgu_specs, [sds((t, d), BF16)],
                              [row1(rows, d)])
        sv["sgu"] = (sgu_ins, sgu_specs)
        sv["a_out"] = a_out
        cw = 256
        conv_ins = [proj, small["conv_w"][l], vec(small["conv_b"][l])]
        conv_specs = [pl.BlockSpec((s, cw), lambda j, b: (b, P_XBC // cw + j)),
                      pl.BlockSpec((SSM_CONV, cw), lambda j, b: (0, j)), pl.BlockSpec((1, cw), lambda j, b: (0, j))]
        conv_out_spec = pl.BlockSpec((s, cw), lambda j, b: (b, j))
        (xc,) = stage_fwd("conv_fwd", l, _conv_block, (SSM_CONV_DIM // cw, nb), conv_ins, conv_specs,
                           [sds((t, SSM_CONV_DIM), F32)], [conv_out_spec])
        sv["conv"] = (conv_ins, conv_specs, conv_out_spec)
        ssd_par = [_pad_heads(small["dt_bias"][l]), _pad_heads(small["a_log"][l]),
                   vec(jnp.repeat(small["d_skip"][l], SSM_HEADDIM)), vec(small["ssm_norm_g"][l])]
        comm = ex.before("ssd_fwd", l)
        (y_ssd, y_pre, prevs), comm_outs = _ssd_fwd(xc, dt_raw, proj, *ssd_par, nb, nc, comm=comm)
        if comm is not None:
            ex.after("ssd_fwd", l, comm_outs)
        sv["ssd"] = (xc, dt_raw, prevs, y_pre, ssd_par)
        sv["y_ssd"] = y_ssd
        br_a = mm("mm_sq", l, a_out, ex.weight("p_a", l), out_dtype=BF16)
        br_b = mm("mm_pb", l, y_ssd, ex.weight("p_b", l), out_dtype=BF16)
        merge_ins = [proj, br_a, br_b]
        merge_out_spec = row1(rows_wide, d)
        merge_specs = [pl.BlockSpec((rows_wide, 2 * d), lambda i: (i, P_GATE // (2 * d))), merge_out_spec, merge_out_spec]
        sv["merge"] = (merge_ins, merge_specs, merge_out_spec)
        ln_par = [(vec(small["ln_g"][l, k]), vec(small["ln_b"][l, k])) for k in range(3)]

        def fused(call, a_fn, a_ins, a_specs, tm, w, x_in, par):
            comm = ex.before(call, l)
            outs, comm_outs = _matmul_lnres(call, a_fn, a_ins, a_specs, tm, w, x_in, *par, comm=comm)
            if comm is not None:
                ex.after(call, l, comm_outs)
            return outs

        merged, y1, h1, h1_bf = fused("mm_mix_ln", _merge_block, merge_ins, merge_specs, rows_wide,
                                      ex.weight("w_mix_o", l), h, ln_par[0])
        sv["merged"] = merged
        sv["ln1"] = [h, y1, *ln_par[0]]
        q = mm("mm_sq", l, h1_bf, ex.weight("w_xq", l), out_dtype=BF16)
        kv = mm("mm_kv", l, mem_n, ex.weight("w_xkv", l), out_dtype=BF16)
        attn_ins = [q, kv]
        attn_out_spec = pl.BlockSpec((tq, d), lambda b, i: (b * (s // tq) + i, 0))
        attn_specs = [attn_out_spec, pl.BlockSpec((MEM_LEN, 2 * d), lambda b, i: (b, 0))]
        (o,) = stage_fwd("attn_fwd", l, _attn_block, (nb, s // tq), attn_ins, attn_specs, [sds((t, d), BF16)],
                          [attn_out_spec])
        sv["attn"] = (attn_ins, attn_specs, attn_out_spec)
        sv["o"] = o
        sv["h1_bf"] = h1_bf
        y2, h2, h2_bf = fused("mm_xo_ln", None, [o], [row1(rows_wide, d)], rows_wide, ex.weight("w_xo", l), h1, ln_par[1])
        sv["ln2"] = [h1, y2, *ln_par[1]]
        sv["h2_bf"] = h2_bf
        gu = mm("mm_ffn_in", l, h2_bf, ex.weight("w_ffn_in", l), out_dtype=BF16)
        act, y3, h3, h3_bf = fused("mm_ffn_out_ln", _swiglu_block, [gu], [row1(rows, 2 * FFN_HIDDEN)], rows,
                                   ex.weight("w_ffn_out", l), h2, ln_par[2])
        sv["gu"] = gu
        sv["act"] = act
        sv["ln3"] = [h2, y3, *ln_par[2]]
        h, h_bf = h3, h3_bf
        saved.append(sv)

    loss, dh = _loss_head(h, target.reshape(t, d))

    g_small = {n: [None] * DEPTH for n in SMALL_REP + SMALL_SH if n not in ("mem_ln_g", "mem_ln_b")}
    dmem_n = []
    ln_grads = [(0, (), F32), (1, (), BF16), (2, (0,), F32), (3, (0,), F32)]
    for l in reversed(range(DEPTH)):
        sv = saved[l]
        dln_g, dln_b = [None] * 3, [None] * 3
        dres, dy3, dln_g[2], dln_b[2] = stage_bwd("lnres_bwd", l, _lnres_block, (t // rows_wide,), sv["ln3"], ln_specs,
                                                  [(dh,)], [row1(rows_wide, d)], ln_grads)
        ex.grad("w_ffn_out", l, mm("mm_ffn_out_dw", l, sv["act"], dy3, ta=True, out_dtype=BF16))
        (dgu,) = stage_bwd("swiglu_bwd", l, _swiglu_block, (t // rows,), [sv["gu"]], [row1(rows, 2 * FFN_HIDDEN)],
                           [], [], [(0, (), BF16)], ct_product=(dy3, row1(rows, d), ex.weight("w_ffn_out", l)))
        ex.grad("w_ffn_in", l, mm("mm_ffn_in_dw", l, sv["h2_bf"], dgu, ta=True, out_dtype=BF16))
        dh2 = mm("mm_ffn_in_dx", l, dgu, ex.weight("w_ffn_in", l), tb=True, add=dres)
        dres, dy2, dln_g[1], dln_b[1] = stage_bwd("lnres_bwd", l, _lnres_block, (t // rows_wide,), sv["ln2"], ln_specs,
                                                  [(dh2,)], [row1(rows_wide, d)], ln_grads)
        ex.grad("w_xo", l, mm("mm_sq_dw", l, sv["o"], dy2, ta=True, out_dtype=BF16))
        do = mm("mm_sq_dx", l, dy2, ex.weight("w_xo", l), tb=True, out_dtype=BF16)
        attn_ins, attn_specs, attn_out_spec = sv["attn"]
        dq, dkv = stage_bwd("attn_bwd", l, _attn_block, (nb, s // tq), attn_ins, attn_specs, [(do,)], [attn_out_spec],
                            [(0, (), BF16), (1, (1,), F32)])
        ex.grad("w_xq", l, mm("mm_sq_dw", l, sv["h1_bf"], dq, ta=True, out_dtype=BF16))
        dh1 = mm("mm_sq_dx", l, dq, ex.weight("w_xq", l), tb=True, add=dres)
        ex.grad("w_xkv", l, mm("mm_kv_dw", l, mem_n, dkv, ta=True, out_dtype=BF16))
        dmem_n.append(mm("mm_kv_dx", l, dkv, ex.weight("w_xkv", l), tb=True))
        dres, dy1, dln_g[0], dln_b[0] = stage_bwd("lnres_bwd", l, _lnres_block, (t // rows_wide,), sv["ln1"], ln_specs,
                                                  [(dh1,)], [row1(rows_wide, d)], ln_grads)
        g_small["ln_g"][l] = jnp.concatenate(dln_g, axis=0)
        g_small["ln_b"][l] = jnp.concatenate(dln_b, axis=0)
        ex.grad("w_mix_o", l, mm("mm_sq_dw", l, sv["merged"], dy1, ta=True, out_dtype=BF16))
        merge_ins, merge_specs, merge_out_spec = sv["merge"]
        dproj, dbr_a, dbr_b = stage_bwd("merge_bwd", l, _merge_block, (t // rows_wide,), merge_ins, merge_specs, [], [],
                                        [(0, (), BF16, ((t, P_COLS), merge_specs[0])), (1, (), BF16), (2, (), BF16)],
                                        ct_product=(dy1, merge_out_spec, ex.weight("w_mix_o", l)))
        ex.grad("p_a", l, mm("mm_sq_dw", l, sv["a_out"], dbr_a, ta=True, out_dtype=BF16))
        da_out = mm("mm_sq_dx", l, dbr_a, ex.weight("p_a", l), tb=True, out_dtype=BF16)
        ex.grad("p_b", l, mm("mm_pb_dw", l, sv["y_ssd"], dbr_b, ta=True, out_dtype=BF16))
        dy_ssd = mm("mm_pb_dx", l, dbr_b, ex.weight("p_b", l), tb=True, out_dtype=BF16)
        sgu_ins, sgu_specs = sv["sgu"]
        dproj, dsg_ln_g, dsg_ln_b, dsg_w, dsg_b = stage_bwd(
            "sgu_bwd", l, _sgu_block, (t // rows,), sgu_ins, sgu_specs, [(da_out,)], [row1(rows, d)],
            [(0, (), BF16, ((t, P_COLS), sgu_specs[0]), dproj), (1, (0,), F32), (2, (0,), F32), (3, (0,), F32),
             (4, (0,), F32)])
        g_small["sg_ln_g"][l], g_small["sg_ln_b"][l], g_small["sg_w"][l], g_small["sg_b"][l] = (
            dsg_ln_g[0], dsg_ln_b[0], dsg_w, dsg_b.T)
        xc, dt_raw, prevs, y_pre, ssd_par = sv["ssd"]
        comm = ex.before("ssd_bwd", l)
        (dxc, ddt, dproj, ddtb, dal, dds, dng), comm_outs = _ssd_bwd(xc, dt_raw, sv["proj"], prevs, y_pre, *ssd_par,
                                                                     dy_ssd, dproj, nb, nc, comm=comm)
        if comm is not None:
            ex.after("ssd_bwd", l, comm_outs)
        g_small["dt_bias"][l], g_small["a_log"][l], g_small["d_skip"][l] = (
            ddtb[0, :SSM_HEADS], dal[0, :SSM_HEADS], dds[0, :SSM_HEADS])
        g_small["ssm_norm_g"][l] = dng[0]
        conv_ins, conv_specs, conv_out_spec = sv["conv"]
        dproj, dconv_w, dconv_b = stage_bwd("conv_bwd", l, _conv_block, (SSM_CONV_DIM // 256, nb), conv_ins, conv_specs,
                                            [(dxc,)], [conv_out_spec],
                                            [(0, (), BF16, ((t, P_COLS), conv_specs[0]), dproj), (1, (1,), F32),
                                             (2, (1,), F32)])
        g_small["conv_w"][l], g_small["conv_b"][l] = dconv_w, dconv_b[0]
        if l == 0:
            dmg, dmb = stage_bwd("memln_bwd", l, _memln_block, (nb * MEM_LEN // 256,), mem_ins, mem_specs,
                                 [tuple(dmem_n)], [row1(256, d)], [(1, (0,), F32), (2, (0,), F32)])
            done = {n: jnp.stack(g, axis=0) for n, g in g_small.items()}
            done["mem_ln_g"], done["mem_ln_b"] = dmg[0], dmb[0]
            ex.small_grads(done)
        w_p, w_dt = ex.weight("w_in", l)
        g_dt = mm("mm_dt_dw", l, sv["h_bf"], ddt, ta=True, out_dtype=BF16)
        ex.grad("w_in", l, _unpack_w_in(mm("mm_in_dw", l, sv["h_bf"], dproj, ta=True, out_dtype=BF16), g_dt))
        dh = mm("mm_in_dx", l, dproj, w_p, tb=True, add=dres, extra=(ddt, w_dt))

    return loss, dh.reshape(nb, s, d)


def _pack_flat(arrays, rows):
    flat = jnp.concatenate([a.reshape(-1) for a in arrays])
    return jnp.pad(flat, (0, rows * 128 - flat.shape[0])).reshape(rows, 128)


def _unpack_flat(packed, shapes):
    lead = packed.shape[:-2]
    flat = packed.reshape(lead + (-1,))
    out, pos = [], 0
    for shape in shapes:
        n = math.prod(shape)
        out.append(flat[..., pos:pos + n].reshape(lead + tuple(shape)))
        pos += n
    return out


def _small_rows(n_elems):
    return -(-n_elems // (128 * SMALL_ROW_TILE)) * SMALL_ROW_TILE


def _from_shards(name, gathered):
    _, a, b = gathered.shape
    if name == "w_in":
        return tuple(_pack_w_in(gathered))
    if name in BIG_COL_SHARDED:
        return _join_columns(gathered)
    return gathered.reshape(N_DEV * a, b)


def _to_shards(name, g):
    if name == "w_in":
        return g
    if name in BIG_COL_SHARDED:
        return _split_columns(g)
    a, b = g.shape
    return g.reshape(N_DEV, a // N_DEV, b)


def _join_columns(gathered):
    _, r, b = gathered.shape
    tr = _pick_tile(r, (128,))

    def body(g_ref, o_ref):
        o_ref[...] = jnp.concatenate([g_ref[j].astype(F32) for j in range(N_DEV)], axis=1).astype(o_ref.dtype)

    return _call("join_columns", body, grid=(r // tr,), ins=[gathered],
                 in_specs=[pl.BlockSpec((N_DEV, tr, b), lambda i: (0, i, 0))],
                 out_specs=[pl.BlockSpec((tr, N_DEV * b), lambda i: (i, 0))],
                 out_shape=[jax.ShapeDtypeStruct((r, N_DEV * b), gathered.dtype)])[0][0]


def _split_columns(full):
    r, nb = full.shape
    b = nb // N_DEV
    tr = _pick_tile(r, (128,))

    def body(f_ref, o_ref):
        w = f_ref[...].astype(F32)
        for j in range(N_DEV):
            o_ref[j] = w[:, j * b:(j + 1) * b].astype(o_ref.dtype)

    return _call("split_columns", body, grid=(r // tr,), ins=[full],
                 in_specs=[pl.BlockSpec((tr, nb), lambda i: (i, 0))],
                 out_specs=[pl.BlockSpec((N_DEV, tr, b), lambda i: (0, i, 0))],
                 out_shape=[jax.ShapeDtypeStruct((N_DEV, r, b), full.dtype)])[0][0]


class _MeshExchange:
    def __init__(self, shards_bf16, first):
        self.shards = shards_bf16
        self.full = dict(first)
        self.pieces = {}
        self.grads = {}
        self.to_send = {}
        self.received = {}
        self.small = None
        self.small_gathered = None

    def weight(self, name, l):
        if (name, l) not in self.full:
            got = jnp.concatenate([self.pieces[(name, l, q)] for q in range(W_IN_PIECES)], axis=1)
            self.full[(name, l)] = _from_shards(name, got)
        return self.full[(name, l)]

    def grad(self, name, l, g):
        self.grads[(name, l)] = g

    def small_grads(self, done):
        self.small = done

    def partial_sums(self, name, l):
        if (name, l, None) in self.received:
            return self.received[(name, l, None)]
        return jnp.concatenate([self.received[(name, l, q)] for q in range(W_IN_PIECES)], axis=1)

    def _slices(self, name, l, piece):
        n_rows = D_MODEL // W_IN_PIECES
        key, rows = (name, l), None if piece is None else (piece * n_rows, n_rows)
        if key not in self.to_send:
            self.to_send[key] = _to_shards(name, self.grads[key])
        return self.to_send[key], rows

    def before(self, call, l):
        comm = _Comm()
        for name, layer, piece in GATHER_PLAN.get((call, l), ()):
            n_rows = D_MODEL // W_IN_PIECES
            comm.gathers.append((self.shards[name], layer, None if piece is None else (piece * n_rows, n_rows)))
        if (call, l) == SMALL_GATHER_CALL:
            names = SMALL_REP + SMALL_SH
            rows = _small_rows(sum(math.prod(self.small[n].shape) for n in names))
            comm.gathers.append((_pack_flat([self.small[n] for n in names], rows), None, None))
        for name, layer, piece in SCATTER_PLAN.get((call, l), ()):
            comm.scatters.append(self._slices(name, layer, piece))
        return comm if comm.gathers or comm.scatters else None

    def after(self, call, l, outs):
        gathers = list(GATHER_PLAN.get((call, l), ()))
        for (name, layer, piece), out in zip(gathers, outs):
            if piece is None:
                self.full[(name, layer)] = _from_shards(name, out)
            else:
                self.pieces[(name, layer, piece)] = out
        outs = outs[len(gathers):]
        if (call, l) == SMALL_GATHER_CALL:
            self.small_gathered = outs[0]
            outs = outs[1:]
        for item, out in zip(SCATTER_PLAN.get((call, l), ()), outs):
            self.received[item] = out


def kernel(x, mem, mem_ln_g, mem_ln_b, w_in, sg_ln_g, sg_ln_b, sg_w, sg_b, conv_w, conv_b, dt_bias, a_log, d_skip, ssm_norm_g, p_a, p_b, w_mix_o, w_xq, w_xkv, w_xo, w_ffn_in, w_ffn_out, ln_g, ln_b, loss_target, m_mem_ln_g, m_mem_ln_b, m_w_in, m_sg_ln_g, m_sg_ln_b, m_sg_w, m_sg_b, m_conv_w, m_conv_b, m_dt_bias, m_a_log, m_d_skip, m_ssm_norm_g, m_p_a, m_p_b, m_w_mix_o, m_w_xq, m_w_xkv, m_w_xo, m_w_ffn_in, m_w_ffn_out, m_ln_g, m_ln_b, v_mem_ln_g, v_mem_ln_b, v_w_in, v_sg_ln_g, v_sg_ln_b, v_sg_w, v_sg_b, v_conv_w, v_conv_b, v_dt_bias, v_a_log, v_d_skip, v_ssm_norm_g, v_p_a, v_p_b, v_w_mix_o, v_w_xq, v_w_xkv, v_w_xo, v_w_ffn_in, v_w_ffn_out, v_ln_g, v_ln_b):
    args = dict(locals())
    w = {n: args[n] for n in WEIGHTS}
    m = {n: args["m_" + n] for n in WEIGHTS}
    v = {n: args["v_" + n] for n in WEIGHTS}
    me = 4 * lax.axis_index("x") + 2 * lax.axis_index("y") + lax.axis_index("c")

    shards = {n: w[n].astype(BF16) for n in BIG}
    sh_shapes = [w[n].shape for n in SMALL_SH]
    first = _Comm()
    first.gathers.append((shards["w_in"], 0, None))
    first.gathers.append((_pack_flat([w[n] for n in SMALL_SH], _small_rows(sum(math.prod(s) for s in sh_shapes))), None,
                          None))
    w_in0, small_sh = _comm_only("gather_first", first)
    small = {n: w[n] for n in SMALL_REP}
    for n, sh in zip(SMALL_SH, _unpack_flat(small_sh, sh_shapes)):
        small[n] = sh.transpose(1, 2, 0, 3).reshape(sh.shape[1], sh.shape[2], N_DEV * sh.shape[3])

    ex = _MeshExchange(shards, {("w_in", 0): _from_shards("w_in", w_in0)})
    loss, grad_x = _run_step(x, mem, loss_target, small, ex)
    loss = lax.psum(loss[0, 0], ("x", "y", "c"))

    out = {}
    for n in BIG[1:] + BIG[:1]:
        comm = ex.before("adamw_" + n, 0)
        out[n], comm_outs = _adamw_sharded("adamw_" + n, [ex.partial_sums(n, l) for l in range(DEPTH)], w[n], m[n], v[n],
                                           comm=comm)
        if comm is not None:
            ex.after("adamw_" + n, 0, comm_outs)
    names = SMALL_REP + SMALL_SH
    g_small = dict(zip(names, _unpack_flat(_sum_parts("sum_small_grads", ex.small_gathered),
                                           [ex.small[n].shape for n in names])))
    for n in names:
        g = g_small[n]
        if n in SMALL_SH:
            width = w[n].shape[-1]
            g = lax.dynamic_slice_in_dim(g, me * width, width, axis=-1)
        two_d = (-1, w[n].shape[-1])
        res = _adamw_small("adamw_" + n, g.reshape(two_d), w[n].reshape(two_d), m[n].reshape(two_d), v[n].reshape(two_d))
        out[n] = [g] + [r.reshape(w[n].shape) for r in res]

    results = []
    for k in range(4):
        results.extend(out[n][k] for n in WEIGHTS)
    return (loss, grad_x, *results)
```

```python
import functools
import math

import jax
import jax.numpy as jnp
from jax import lax
from jax.experimental import pallas as pl
from jax.experimental.pallas import tpu as pltpu

F32 = jnp.float32
BF16 = jnp.bfloat16
HIGHEST = lax.Precision.HIGHEST

N_DEV = 8
D_MODEL = 1024
DEPTH = 2
MEM_LEN = 256
CHUNK = 128
SG_GROUPS = 8
SSM_INNER = 2048
SSM_HEADDIM = 64
SSM_HEADS = 32
SSM_STATE = 128
SSM_GROUPS = 4
SSM_RPG = 8
SSM_CONV = 4
SSM_CONV_DIM = 3072
X_HEADS = 4
X_HEADDIM = 256
FFN_HIDDEN = 2816
ALPHA = float((2 * DEPTH) ** 0.25)
LN_EPS = 1e-5
RMS_EPS = 1e-5
XBC_COL = 4096
DT_COL = 7168
GA_COL = 7200
IN_COLS = 9248
P_GATE = 4096
P_XBC = 6144
P_COLS = 9216
DT_LANES = 128

ADAM_LR = 0.001
ADAM_B1 = 0.9
ADAM_B2 = 0.999
ADAM_EPS = 1e-08
ADAM_WD = 0.01
ADAM_STEP = 10

VMEM_LIMIT = 48 * 1024 * 1024
SMALL_ROW_TILE = 256

BIG = ("w_in", "p_a", "p_b", "w_mix_o", "w_xq", "w_xkv", "w_xo", "w_ffn_in", "w_ffn_out")
BIG_COL_SHARDED = ("w_in", "w_xkv", "w_ffn_in")
SMALL_REP = ("mem_ln_g", "mem_ln_b", "sg_ln_g", "sg_ln_b", "sg_w", "sg_b", "conv_b", "dt_bias", "a_log", "d_skip",
             "ssm_norm_g")
SMALL_SH = ("conv_w", "ln_g", "ln_b")
WEIGHTS = ("mem_ln_g", "mem_ln_b", "w_in", "sg_ln_g", "sg_ln_b", "sg_w", "sg_b", "conv_w", "conv_b", "dt_bias", "a_log",
           "d_skip", "ssm_norm_g", "p_a", "p_b", "w_mix_o", "w_xq", "w_xkv", "w_xo", "w_ffn_in", "w_ffn_out", "ln_g", "ln_b")

W_IN_PIECES = 4
GATHER_PLAN = {("sgu_fwd", 0): [("w_in", 1, 0)], ("conv_fwd", 0): [("w_in", 1, 1)],
               ("mm_ffn_in", 0): [("w_in", 1, 2), ("w_ffn_out", 0, None)], ("mm_ffn_out_ln", 0): [("w_in", 1, 3)],
               ("conv_fwd", 1): [("w_ffn_out", 1, None)]}
SCATTER_PLAN = {("mm_ffn_in_dw", 0): [("w_in", 1, 2)], ("mm_ffn_in_dx", 0): [("w_in", 1, 3)],
                ("adamw_w_ffn_in", 0): [("w_in", 0, 2)], ("adamw_w_xkv", 0): [("w_in", 0, 3)]}
for _l in range(DEPTH):
    GATHER_PLAN[("mm_in", _l)] = [(n, _l, None) for n in ("p_a", "p_b", "w_mix_o", "w_xq", "w_xkv", "w_xo")]
    GATHER_PLAN[("ssd_fwd", _l)] = [("w_ffn_in", _l, None)]
    SCATTER_PLAN[("swiglu_bwd", _l)] = [("w_ffn_out", _l, None)]
    SCATTER_PLAN[("sgu_bwd", _l)] = [("w_mix_o", _l, None), ("p_a", _l, None), ("w_xo", _l, None)]
    SCATTER_PLAN[("ssd_bwd", _l)] = [("w_ffn_in", _l, None), ("w_xkv", _l, None), ("w_xq", _l, None)]
    SCATTER_PLAN[("conv_bwd", _l)] = [("p_b", _l, None)]
    SCATTER_PLAN[("mm_in_dx", _l)] = [("w_in", _l, 0), ("w_in", _l, 1)]
SMALL_GATHER_CALL = ("mm_in_dw", 0)


def _layer_norm(x, g, b):
    mu = jnp.mean(x, axis=-1, keepdims=True)
    xc = x - mu
    var = jnp.mean(xc * xc, axis=-1, keepdims=True)
    return xc * lax.rsqrt(var + LN_EPS) * g + b


def _gelu(x):
    return 0.5 * x * (1.0 + lax.erf(x * (1.0 / math.sqrt(2.0))))


def _sigmoid(x):
    return 0.5 * jnp.tanh(0.5 * x) + 0.5


def _silu(x):
    return x * _sigmoid(x)


def _softplus(x):
    return jnp.maximum(x, 0.0) + jnp.log1p(jnp.exp(-jnp.abs(x)))


def _causal_mask():
    r = lax.broadcasted_iota(jnp.int32, (CHUNK, CHUNK), 0)
    c = lax.broadcasted_iota(jnp.int32, (CHUNK, CHUNK), 1)
    return r >= c


def _sgu_block(uv, ln_g, ln_b, w, sb):
    gu = _gelu(uv[:, :D_MODEL])
    vn = _layer_norm(_gelu(uv[:, D_MODEL:]), ln_g, ln_b).astype(BF16)
    causal = _causal_mask()
    width = D_MODEL // SG_GROUPS
    wgs = [jnp.where(causal, w[g], 0.0).astype(BF16) for g in range(SG_GROUPS)]
    chunks = []
    for c in range(uv.shape[0] // CHUNK):
        vc = vn[c * CHUNK:(c + 1) * CHUNK, :]
        chunks.append(jnp.concatenate(
            [jnp.dot(wgs[g], vc[:, g * width:(g + 1) * width], preferred_element_type=F32) + sb[:, g:g + 1]
             for g in range(SG_GROUPS)], axis=1))
    return (gu * jnp.concatenate(chunks, axis=0),)


GROUP_W = SSM_RPG * SSM_HEADDIM
NT_DIMS = (((1,), (1,)), ((), ()))
TN_DIMS = (((0,), (0,)), ((), ()))


def _mxu(a, b, dims=(((1,), (0,)), ((), ()))):
    return lax.dot_general(a.astype(BF16), b.astype(BF16), dims, preferred_element_type=F32)


def _head_expander():
    return (jnp.arange(SSM_INNER)[None, :] // SSM_HEADDIM == jnp.arange(128)[:, None]).astype(BF16)


def _bf16_terms(x, n):
    terms = []
    for _ in range(n):
        t = x.astype(BF16)
        terms.append(t)
        x = x - t.astype(F32)
    return terms


def _expand_heads(q, e):
    return sum(jnp.dot(t, e, preferred_element_type=F32) for t in _bf16_terms(q, 2))


def _reduce_heads(v, e, terms=2):
    return sum(lax.dot_general(t, e, NT_DIMS, preferred_element_type=F32) for t in _bf16_terms(v, terms))


def _reduce_heads_of_column_sums(v, e):
    sums = jnp.broadcast_to(jnp.sum(v, axis=0, keepdims=True), (8, v.shape[1]))
    return _reduce_heads(sums, e)[0:1, :]


def _ssd_common(xc, dtraw, dt_bias, a_log, e):
    xs = xc[:, :SSM_INNER]
    pre = dtraw + dt_bias
    dt = _softplus(pre)
    a = -jnp.exp(a_log)
    r_i = lax.broadcasted_iota(jnp.int32, (CHUNK, CHUNK), 0)
    c_i = lax.broadcasted_iota(jnp.int32, (CHUNK, CHUNK), 1)
    tril = jnp.where(r_i >= c_i, 1.0, 0.0).astype(F32)
    cs = jnp.dot(tril, dt * a, precision=HIGHEST, preferred_element_type=F32)
    cs_last = cs[CHUNK - 1:CHUNK, :]
    decay_in = jnp.exp(cs)
    decay_st = jnp.exp(cs_last - cs)
    dt_x = _expand_heads(dt, e)
    w_st_x = _expand_heads(dt * decay_st, e)
    decay_in_x = _expand_heads(decay_in, e)
    return dict(xs=xs, pre=pre, dt=dt, a=a, lower=r_i >= c_i, upper=c_i >= r_i, cs=cs, cs_t=cs.T, decay_in=decay_in,
                decay_st=decay_st, chunk_decay=jnp.exp(cs_last), dt_x=dt_x, w_st_x=w_st_x, decay_in_x=decay_in_x,
                chunk_decay_x=decay_in_x[CHUNK - 1:CHUNK, :], xdt=xs * dt_x, x_st=(xs * w_st_x).astype(BF16),
                low=lax.broadcasted_iota(jnp.int32, (CHUNK, 128), 1) < SSM_HEADDIM)


def _pair_decay(c, h):
    return jnp.exp(jnp.where(c["lower"], c["cs"][:, h:h + 1] - c["cs_t"][h:h + 1, :], -1e30))


def _pair_decay_t(c, h):
    return jnp.exp(jnp.where(c["upper"], c["cs_t"][h:h + 1, :] - c["cs"][:, h:h + 1], -1e30))


def _ssd_forward(xc, dtraw, z, prev, dt_bias, a_log, d_skip_x, norm_g, e):
    c = _ssd_common(xc, dtraw, dt_bias, a_log, e)
    y_groups, new_states = [], []
    for g in range(SSM_GROUPS):
        lanes = slice(g * GROUP_W, (g + 1) * GROUP_W)
        bg = xc[:, SSM_INNER + g * SSM_STATE:SSM_INNER + (g + 1) * SSM_STATE]
        cg = xc[:, SSM_INNER + (SSM_GROUPS + g) * SSM_STATE:SSM_INNER + (SSM_GROUPS + g + 1) * SSM_STATE].astype(BF16)
        pg = prev[g * SSM_STATE:(g + 1) * SSM_STATE, :]
        cb = _mxu(cg, bg, NT_DIMS)
        y_in = _mxu(cg, pg) * c["decay_in_x"][:, lanes]
        new_states.append(pg * c["chunk_decay_x"][:, lanes] + _mxu(bg.T, c["x_st"][:, lanes]))
        pairs = []
        for j in range(SSM_RPG // 2):
            h0 = g * SSM_RPG + 2 * j
            xp = c["xdt"][:, 128 * (h0 // 2):128 * (h0 // 2 + 1)]
            pairs.append(_mxu(cb * _pair_decay(c, h0), jnp.where(c["low"], xp, 0.0))
                         + _mxu(cb * _pair_decay(c, h0 + 1), jnp.where(c["low"], 0.0, xp)))
        y_groups.append(jnp.concatenate(pairs, axis=1) + y_in)
    y_pre = jnp.concatenate(y_groups, axis=1) + c["xs"] * d_skip_x
    gated = y_pre * _silu(z)
    normed = [gated[:, g * GROUP_W:(g + 1) * GROUP_W] for g in range(SSM_GROUPS)]
    normed = [yg * lax.rsqrt(jnp.mean(yg * yg, axis=-1, keepdims=True) + RMS_EPS) for yg in normed]
    return jnp.concatenate(normed, axis=1) * norm_g, y_pre, jnp.concatenate(new_states, axis=0)


def _ssd_backward(xc, dtraw, z, prev, y_pre, dt_bias, a_log, d_skip_x, norm_g, e, dout, dnew):
    c = _ssd_common(xc, dtraw, dt_bias, a_log, e)
    xs = c["xs"]
    sig = _sigmoid(z)
    silu_z = z * sig
    gated = y_pre * silu_z
    d_gated, normed = [], []
    for g in range(SSM_GROUPS):
        lanes = slice(g * GROUP_W, (g + 1) * GROUP_W)
        yg = gated[:, lanes]
        r = lax.rsqrt(jnp.mean(yg * yg, axis=-1, keepdims=True) + RMS_EPS)
        n = yg * r
        gh = dout[:, lanes] * norm_g[:, lanes]
        d_gated.append(r * (gh - n * jnp.mean(gh * n, axis=-1, keepdims=True)))
        normed.append(n)
    d_gated = jnp.concatenate(d_gated, axis=1)
    dnorm_g = jnp.sum(dout * jnp.concatenate(normed, axis=1), axis=0, keepdims=True)
    dy = d_gated * silu_z
    dz = d_gated * y_pre * (sig * (1.0 + z * (1.0 - sig)))
    dxs = dy * d_skip_x
    dd_skip = _reduce_heads_of_column_sums(dy * xs, e)

    lane = lax.broadcasted_iota(jnp.int32, (CHUNK, 128), 1)
    sub = lax.broadcasted_iota(jnp.int32, (8, 128), 0)
    dcs_neg = jnp.zeros((CHUNK, 128), F32)
    row_slabs = []
    dxdt, dx_st, d_decay_in_x, dprev, d_chunk_decay_x, db_all, dc_all = [], [], [], [], [], [], []
    for g in range(SSM_GROUPS):
        lanes = slice(g * GROUP_W, (g + 1) * GROUP_W)
        bg = xc[:, SSM_INNER + g * SSM_STATE:SSM_INNER + (g + 1) * SSM_STATE].astype(BF16)
        cg_f = xc[:, SSM_INNER + (SSM_GROUPS + g) * SSM_STATE:SSM_INNER + (SSM_GROUPS + g + 1) * SSM_STATE]
        cg = cg_f.astype(BF16)
        pg = prev[g * SSM_STATE:(g + 1) * SSM_STATE, :]
        dng = dnew[g * SSM_STATE:(g + 1) * SSM_STATE, :]
        dy_g = dy[:, lanes]
        cb_t = _mxu(bg, cg, NT_DIMS)
        t1 = (dy_g * c["decay_in_x"][:, lanes]).astype(BF16)
        d_decay_in_x.append(dy_g * _mxu(cg, pg))
        dc = _mxu(t1, pg, NT_DIMS)
        dprev.append(_mxu(cg_f.T, t1) + dng * c["chunk_decay_x"][:, lanes])
        d_chunk_decay_x.append(dng * pg)
        db = _mxu(c["x_st"][:, lanes], dng, NT_DIMS)
        dx_st.append(_mxu(bg, dng))
        dcb_t = jnp.zeros((CHUNK, CHUNK), F32)
        rows = []
        for j in range(SSM_RPG // 2):
            h0 = g * SSM_RPG + 2 * j
            blk = slice(128 * (h0 // 2), 128 * (h0 // 2 + 1))
            xp = c["xdt"][:, blk]
            dyp = dy[:, blk].astype(BF16)
            pair_dx = []
            for k, xk in enumerate((jnp.where(c["low"], xp, 0.0), jnp.where(c["low"], 0.0, xp))):
                dec_t = _pair_decay_t(c, h0 + k)
                pair_dx.append(_mxu(cb_t * dec_t, dyp))
                dml_t = _mxu(xk, dyp, NT_DIMS) * dec_t
                dcb_t = dcb_t + dml_t
                dseg_t = dml_t * cb_t
                dcs_neg = dcs_neg + jnp.where(lane == h0 + k, jnp.sum(dseg_t, axis=-1, keepdims=True), 0.0)
                rows.append(jnp.sum(dseg_t, axis=0, keepdims=True))
            dxdt.append(jnp.where(c["low"], pair_dx[0], pair_dx[1]))
        slab = jnp.zeros((8, 128), F32)
        for r in range(SSM_RPG):
            slab = slab + jnp.where(sub == r, rows[r], 0.0)
        row_slabs.append(slab)
        dc_all.append(dc + _mxu(dcb_t.T, bg))
        db_all.append(db + _mxu(dcb_t, cg))
    dxdt = jnp.concatenate(dxdt, axis=1)
    dx_st = jnp.concatenate(dx_st, axis=1)
    by_head = jnp.concatenate(row_slabs + [jnp.zeros((CHUNK - SSM_HEADS, 128), F32)], axis=0)
    dcs = by_head.T - dcs_neg
    dxs = dxs + dxdt * c["dt_x"] + dx_st * c["w_st_x"]
    ddt = _reduce_heads(dxdt * xs, e, terms=1)
    dw_st = _reduce_heads(dx_st * xs, e, terms=1)
    dcs = dcs + _reduce_heads(jnp.concatenate(d_decay_in_x, axis=1), e, terms=1) * c["decay_in"]
    ddt = ddt + dw_st * c["decay_st"]
    d_log_st = dw_st * c["dt"] * c["decay_st"]
    dcs = dcs - d_log_st
    d_chunk_decay = _reduce_heads_of_column_sums(jnp.concatenate(d_chunk_decay_x, axis=1), e)
    dcs_last = jnp.sum(d_log_st, axis=0, keepdims=True) + d_chunk_decay * c["chunk_decay"]
    row = lax.broadcasted_iota(jnp.int32, (CHUNK, 128), 0)
    dcs = dcs + jnp.where(row == CHUNK - 1, dcs_last, 0.0)
    triu = jnp.where(c["upper"], 1.0, 0.0).astype(F32)
    dda = jnp.dot(triu, dcs, precision=HIGHEST, preferred_element_type=F32)
    ddt = ddt + dda * c["a"]
    da_log = jnp.sum(dda * c["dt"], axis=0, keepdims=True) * c["a"]
    dpre = ddt * _sigmoid(c["pre"])
    dxc = jnp.concatenate([dxs] + db_all + dc_all, axis=1)
    return (dxc, dpre, dz, jnp.concatenate(dprev, axis=0), jnp.sum(dpre, axis=0, keepdims=True), da_log, dd_skip,
            dnorm_g)


def _conv_block(x, w, b):
    rows = lax.broadcasted_iota(jnp.int32, x.shape, 0)
    acc = x * w[SSM_CONV - 1:SSM_CONV, :] + b
    for k in range(SSM_CONV - 1):
        shift = SSM_CONV - 1 - k
        acc = acc + _shift_rows(x, rows, shift) * w[k:k + 1, :]
    return (_silu(acc),)


@functools.partial(jax.custom_vjp, nondiff_argnums=(2,))
def _shift_rows(x, rows, shift):
    return jnp.where(rows >= shift, pltpu.roll(x, shift, 0), 0.0)


def _shift_rows_fwd(x, rows, shift):
    return _shift_rows(x, rows, shift), rows


def _shift_rows_bwd(shift, rows, g):
    n = g.shape[0]
    return jnp.where(rows < n - shift, pltpu.roll(g, n - shift, 0), 0.0), None


_shift_rows.defvjp(_shift_rows_fwd, _shift_rows_bwd)


def _merge_block(gates, br_a, br_b):
    return (_sigmoid(gates[:, :D_MODEL]) * br_a + _sigmoid(gates[:, D_MODEL:]) * br_b,)


def _lnres_block(x, y, g, b):
    return (_layer_norm(ALPHA * x + y, g, b),)


def _memln_block(x, g, b):
    return (_layer_norm(x, g, b),)


def _attn_block(q, kv):
    outs = []
    for h in range(X_HEADS):
        qh = q[:, h * X_HEADDIM:(h + 1) * X_HEADDIM].astype(BF16)
        kh = kv[:, h * X_HEADDIM:(h + 1) * X_HEADDIM].astype(BF16)
        vh = kv[:, D_MODEL + h * X_HEADDIM:D_MODEL + (h + 1) * X_HEADDIM].astype(BF16)
        s = lax.dot_general(qh, kh, (((1,), (1,)), ((), ())), preferred_element_type=F32) * (X_HEADDIM ** -0.5)
        s = s - lax.stop_gradient(jnp.max(s, axis=-1, keepdims=True))
        e = jnp.exp(s)
        p = e / jnp.sum(e, axis=-1, keepdims=True)
        outs.append(jnp.dot(p.astype(BF16), vh, preferred_element_type=F32))
    return (jnp.concatenate(outs, axis=1),)


def _swiglu_block(gu):
    return (_silu(gu[:, :FFN_HIDDEN]) * gu[:, FFN_HIDDEN:],)


class _Comm:
    def __init__(self):
        self.gathers = []
        self.scatters = []

    @staticmethod
    def _rows(ref, rows):
        return ref if rows is None else ref.at[pl.ds(rows[0], rows[1])]

    def operands(self):
        ins = [a for a, _, _ in self.gathers] + [a for a, _ in self.scatters]
        shapes = []
        for a, idx, rows in self.gathers:
            blk = a.shape if idx is None else a.shape[1:]
            shapes.append(jax.ShapeDtypeStruct((N_DEV, blk[0] if rows is None else rows[1]) + tuple(blk[1:]), a.dtype))
        for a, rows in self.scatters:
            shapes.append(jax.ShapeDtypeStruct((N_DEV, a.shape[1] if rows is None else rows[1]) + tuple(a.shape[2:]),
                                               a.dtype))
        scratch = []
        for n in (len(self.gathers), len(self.scatters)):
            if n:
                scratch += [pltpu.SemaphoreType.DMA((7 * n,)), pltpu.SemaphoreType.DMA((7 * n,)),
                            pltpu.SemaphoreType.DMA((n,))]
        return ins, shapes, scratch

    def _split(self, in_refs, out_refs, sems):
        ng = len(self.gathers)
        g_sems = sems[:3] if ng else None
        s_sems = sems[3:] if ng else sems
        return in_refs[:ng], in_refs[ng:], out_refs[:ng], out_refs[ng:], g_sems, s_sems

    def _gather_copies(self, i, src_ref, out_ref, sems):
        send_sems, recv_sems, local_sems = sems
        x, y, c = lax.axis_index("x"), lax.axis_index("y"), lax.axis_index("c")
        me, sibling = (x, y, c), (x, y, 1 - c)
        chips = [(1 - x, y), (x, 1 - y), (1 - x, 1 - y)]
        _, idx, rows = self.gathers[i]
        src = self._rows(src_ref if idx is None else src_ref.at[idx], rows)

        def slot(px, py, pc):
            return out_ref.at[4 * px + 2 * py + pc]

        def copy(k, blk, to, from_src=False):
            return pltpu.make_async_remote_copy(
                src_ref=src if from_src else slot(*blk), dst_ref=slot(*blk), send_sem=send_sems.at[7 * i + k],
                recv_sem=recv_sems.at[7 * i + k], device_id=to, device_id_type=pl.DeviceIdType.MESH)

        mine = pltpu.make_async_copy(src, slot(*me), local_sems.at[i])
        first = [copy(0, me, sibling, True)] + [copy(1 + j, me, (*chip, c), True) for j, chip in enumerate(chips)]
        passed = [copy(4 + j, (*chip, c), sibling) for j, chip in enumerate(chips)]
        arrivals = [copy(1 + j, (*chip, c), me) for j, chip in enumerate(chips)]
        from_sibling = [copy(0, sibling, me)] + [copy(4 + j, (*chip, 1 - c), me) for j, chip in enumerate(chips)]
        return mine, first, passed, arrivals, from_sibling

    def _scatter_copies(self, i, src_ref, out_ref, sems):
        send_sems, recv_sems, local_sems = sems
        x, y, c = lax.axis_index("x"), lax.axis_index("y"), lax.axis_index("c")
        me = 4 * x + 2 * y + c
        rows = self.scatters[i][1]
        mine = pltpu.make_async_copy(self._rows(src_ref.at[me], rows), out_ref.at[me], local_sems.at[i])
        copies = []
        for k in range(1, N_DEV):
            px = 1 - x if k & 4 else x
            py = 1 - y if k & 2 else y
            pc = 1 - c if k & 1 else c
            copies.append(pltpu.make_async_remote_copy(
                src_ref=self._rows(src_ref.at[4 * px + 2 * py + pc], rows), dst_ref=out_ref.at[me],
                send_sem=send_sems.at[7 * i + k - 1], recv_sem=recv_sems.at[7 * i + k - 1], device_id=(px, py, pc),
                device_id_type=pl.DeviceIdType.MESH))
        return mine, copies

    def start(self, in_refs, out_refs, sems):
        g_in, s_in, g_out, s_out, g_sems, s_sems = self._split(in_refs, out_refs, sems)
        for i in range(len(self.gathers)):
            mine, first, _, _, _ = self._gather_copies(i, g_in[i], g_out[i], g_sems)
            mine.start()
            for cp in first:
                cp.start()
        for i in range(len(self.scatters)):
            mine, copies = self._scatter_copies(i, s_in[i], s_out[i], s_sems)
            mine.start()
            for cp in copies:
                cp.start()

    def finish(self, in_refs, out_refs, sems):
        g_in, s_in, g_out, s_out, g_sems, s_sems = self._split(in_refs, out_refs, sems)
        parts = [self._gather_copies(i, g_in[i], g_out[i], g_sems) for i in range(len(self.gathers))]
        for j in range(3):
            for _, _, passed, arrivals, _ in parts:
                arrivals[j].wait_recv()
                passed[j].start()
        for mine, first, passed, _, from_sibling in parts:
            for cp in from_sibling:
                cp.wait_recv()
            for cp in first + passed:
                cp.wait_send()
            mine.wait()
        for i in range(len(self.scatters)):
            mine, copies = self._scatter_copies(i, s_in[i], s_out[i], s_sems)
            for cp in copies:
                cp.wait_recv()
            for cp in copies:
                cp.wait_send()
            mine.wait()


def _params(grid):
    return pltpu.CompilerParams(dimension_semantics=("arbitrary",) * len(grid), vmem_limit_bytes=VMEM_LIMIT)


def _call(name, body, *, grid, ins, in_specs, out_shape, out_specs, scratch=(), comm=None, aliases=None):
    n_in, n_out, n_scr = len(ins), len(out_shape), len(scratch)
    aliases = aliases or {}
    if comm is None:
        outs = pl.pallas_call(body, grid=grid, in_specs=list(in_specs), out_specs=list(out_specs),
                              out_shape=list(out_shape), scratch_shapes=list(scratch), name=name,
                              input_output_aliases=aliases, compiler_params=_params(grid))(*ins)
        return list(outs), []
    c_ins, c_shapes, c_scratch = comm.operands()
    nci, nco = len(c_ins), len(c_shapes)
    anywhere = pl.BlockSpec(memory_space=pl.ANY)

    def carrier(*refs):
        main_in, comm_in = refs[:n_in], refs[n_in:n_in + nci]
        o0 = n_in + nci
        main_out, comm_out = refs[o0:o0 + n_out], refs[o0 + n_out:o0 + n_out + nco]
        s0 = o0 + n_out + nco
        main_scr, comm_scr = refs[s0:s0 + n_scr], refs[s0 + n_scr:]
        first = pl.program_id(0) == 0
        last = pl.program_id(0) == grid[0] - 1
        for ax in range(1, len(grid)):
            first = first & (pl.program_id(ax) == 0)
            last = last & (pl.program_id(ax) == grid[ax] - 1)

        @pl.when(first)
        def _():
            comm.start(comm_in, comm_out, comm_scr)

        body(*main_in, *main_out, *main_scr)

        @pl.when(last)
        def _():
            comm.finish(comm_in, comm_out, comm_scr)

    outs = pl.pallas_call(carrier, grid=grid, in_specs=list(in_specs) + [anywhere] * nci,
                          out_specs=list(out_specs) + [anywhere] * nco, out_shape=list(out_shape) + c_shapes,
                          scratch_shapes=list(scratch) + c_scratch, name=name, input_output_aliases=aliases,
                          compiler_params=_params(grid))(*ins, *c_ins)
    return list(outs[:n_out]), list(outs[n_out:])


def _comm_only(name, comm):
    c_ins, c_shapes, c_scratch = comm.operands()
    nci, nco = len(c_ins), len(c_shapes)
    anywhere = pl.BlockSpec(memory_space=pl.ANY)

    def body(*refs):
        comm.start(refs[:nci], refs[nci:nci + nco], refs[nci + nco:])
        comm.finish(refs[:nci], refs[nci:nci + nco], refs[nci + nco:])

    return list(pl.pallas_call(body, in_specs=[anywhere] * nci, out_specs=[anywhere] * nco, out_shape=c_shapes,
                               scratch_shapes=c_scratch, name=name)(*c_ins))


def _stage_fwd(name, f, grid, ins, in_specs, out_shapes, out_specs, comm=None):
    n_in = len(ins)

    def body(*refs):
        res = f(*[r[...].astype(F32) for r in refs[:n_in]])
        for o_ref, val in zip(refs[n_in:], res):
            o_ref[...] = val.astype(o_ref.dtype)

    return _call(name, body, grid=grid, ins=ins, in_specs=in_specs, out_shape=out_shapes, out_specs=out_specs, comm=comm)


def _stage_bwd(name, f, grid, ins, in_specs, cts, ct_specs, grads, comm=None, ct_product=None):
    n_in = len(ins)
    flat_cts = [c for group in cts for c in group]
    flat_ct_specs = [s for group, spec in zip(cts, ct_specs) for s in (spec,) * len(group)]
    if ct_product is not None:
        assert not cts
        flat_cts = [ct_product[0], ct_product[2]]
        flat_ct_specs = [ct_product[1], pl.BlockSpec(ct_product[2].shape, lambda *_: (0, 0))]
    n_ct = len(flat_cts)
    diff = [g[0] for g in grads]
    buffers = [(k, g[4]) for k, g in enumerate(grads) if len(g) > 4]
    n_buf = len(buffers)

    def body(*refs):
        vals = [r[...].astype(F32) for r in refs[:n_in]]
        ct_refs = refs[n_in:n_in + n_ct]
        g_refs = refs[n_in + n_ct + n_buf:]
        ct_vals, pos = [], 0
        if ct_product is not None:
            ct_vals.append(lax.dot_general(ct_refs[0][...].astype(BF16), ct_refs[1][...].astype(BF16), NT_DIMS,
                                           preferred_element_type=F32))
        for group in cts:
            acc = ct_refs[pos][...].astype(F32)
            for j in range(1, len(group)):
                acc = acc + ct_refs[pos + j][...].astype(F32)
            ct_vals.append(acc)
            pos += len(group)

        def g_fn(*dvals):
            full = list(vals)
            for i, dv in zip(diff, dvals):
                full[i] = dv
            return f(*full)

        _, vjp = jax.vjp(g_fn, *[vals[i] for i in diff])
        gvals = vjp(tuple(ct_vals))
        for gspec, g_ref, gval in zip(grads, g_refs, gvals):
            acc_axes = gspec[1]
            if not acc_axes:
                g_ref[...] = gval.astype(g_ref.dtype)
            else:
                first = pl.program_id(acc_axes[0]) == 0
                for ax in acc_axes[1:]:
                    first = first & (pl.program_id(ax) == 0)

                @pl.when(first)
                def _():
                    g_ref[...] = jnp.zeros_like(g_ref)

                g_ref[...] += gval.astype(g_ref.dtype)

    out_shapes, out_specs = [], []
    for gspec in grads:
        shape, spec = gspec[3] if len(gspec) > 3 else (ins[gspec[0]].shape, in_specs[gspec[0]])
        out_shapes.append(jax.ShapeDtypeStruct(shape, gspec[2]))
        out_specs.append(spec)
    anywhere = pl.BlockSpec(memory_space=pl.ANY)
    return _call(name, body, grid=grid, ins=list(ins) + flat_cts + [b for _, b in buffers],
                 in_specs=list(in_specs) + flat_ct_specs + [anywhere] * n_buf, out_shape=out_shapes, out_specs=out_specs,
                 comm=comm, aliases={n_in + n_ct + j: k for j, (k, _) in enumerate(buffers)})


def _pick_tile(n, candidates):
    for c in candidates:
        if n % c == 0:
            return c
    return n


def _matmul(name, a, b, *, ta=False, tb=False, add=None, extra=None, out_dtype=F32, comm=None):
    if ta:
        k_dim, m = a.shape
    else:
        m, k_dim = a.shape
    n = b.shape[0] if tb else b.shape[1]
    assert (b.shape[1] if tb else b.shape[0]) == k_dim and not (ta and tb)
    tm = _pick_tile(m, (1024, 1408, 512, 256, 128))
    tn = _pick_tile(n, (1024, 1408, 512, 256, 128))
    if ta:
        tk = _pick_tile(k_dim, (1024, 512, 256, 128))
    elif k_dim <= 2816:
        tk = k_dim
    else:
        tk = _pick_tile(k_dim, (1408, 1024, 512, 256, 128))
    nk = k_dim // tk
    grid = (m // tm, n // tn, nk)
    a_spec = pl.BlockSpec((tk, tm), lambda i, j, k: (k, i)) if ta else pl.BlockSpec((tm, tk), lambda i, j, k: (i, k))
    b_spec = pl.BlockSpec((tn, tk), lambda i, j, k: (j, k)) if tb else pl.BlockSpec((tk, tn), lambda i, j, k: (k, j))
    o_spec = pl.BlockSpec((tm, tn), lambda i, j, k: (i, j))
    dims = (((0 if ta else 1,), (1 if tb else 0,)), ((), ()))
    has_add = add is not None
    has_extra = extra is not None

    def body(*refs):
        a_ref, b_ref = refs[0], refs[1]
        add_ref = refs[2] if has_add else None
        o_ref, acc_ref = refs[-2], refs[-1]
        k = pl.program_id(2)
        part = lax.dot_general(a_ref[...].astype(BF16), b_ref[...].astype(BF16), dims, preferred_element_type=F32)

        def finish(res):
            if has_add:
                res = res + add_ref[...].astype(F32)
            if has_extra:
                a2_ref, b2_ref = refs[2 + has_add], refs[3 + has_add]
                res = res + lax.dot_general(a2_ref[...].astype(BF16), b2_ref[...].astype(BF16), NT_DIMS,
                                            preferred_element_type=F32)
            o_ref[...] = res.astype(o_ref.dtype)

        if nk == 1:
            finish(part)
        else:
            @pl.when(k == 0)
            def _():
                acc_ref[...] = part

            @pl.when((k > 0) & (k < nk - 1))
            def _():
                acc_ref[...] += part

            @pl.when(k == nk - 1)
            def _():
                finish(acc_ref[...] + part)

    ins = [a, b] + ([add] if has_add else [])
    in_specs = [a_spec, b_spec] + ([o_spec] if has_add else [])
    if has_extra:
        k2 = extra[0].shape[1]
        ins += list(extra)
        in_specs += [pl.BlockSpec((tm, k2), lambda i, j, k: (i, 0)), pl.BlockSpec((tn, k2), lambda i, j, k: (j, 0))]
    acc_shape = (tm, tn) if nk > 1 else (8, 128)
    outs, comm_outs = _call(name, body, grid=grid, ins=ins, in_specs=in_specs,
                            out_shape=[jax.ShapeDtypeStruct((m, n), out_dtype)], out_specs=[o_spec],
                            scratch=[pltpu.VMEM(acc_shape, F32)], comm=comm)
    return outs[0], comm_outs


def _matmul_lnres(name, a_fn, a_ins, a_specs, tm, b, x, g, beta, comm=None):
    m = x.shape[0]
    k_dim, n = b.shape
    n_a = len(a_ins)
    row = lambda w: pl.BlockSpec((tm, w), lambda i: (i, 0))
    whole = lambda shape: pl.BlockSpec(shape, lambda i: (0, 0))

    def body(*refs):
        b_ref, x_ref, g_ref, beta_ref = refs[n_a:n_a + 4]
        y_ref, h_ref, hb_ref = refs[-3:]
        if a_fn is None:
            a = refs[0][...].astype(BF16)
        else:
            (a,) = a_fn(*[r[...].astype(F32) for r in refs[:n_a]])
            a = a.astype(BF16)
            refs[n_a + 4][...] = a
        y = jnp.dot(a, b_ref[...].astype(BF16), preferred_element_type=F32).astype(y_ref.dtype)
        y_ref[...] = y
        (h,) = _lnres_block(x_ref[...], y.astype(F32), g_ref[...], beta_ref[...])
        h_ref[...] = h
        hb_ref[...] = h.astype(hb_ref.dtype)

    sds = jax.ShapeDtypeStruct
    a_out = ([sds((m, k_dim), BF16)], [row(k_dim)]) if a_fn is not None else ([], [])
    return _call(name, body, grid=(m // tm,), ins=list(a_ins) + [b, x, g, beta],
                 in_specs=list(a_specs) + [whole((k_dim, n)), row(n), whole((1, n)), whole((1, n))],
                 out_shape=a_out[0] + [sds((m, n), BF16), sds((m, n), F32), sds((m, n), BF16)],
                 out_specs=a_out[1] + [row(n), row(n), row(n)], comm=comm)


SSD_STATE = (SSM_GROUPS * SSM_STATE, SSM_RPG * SSM_HEADDIM)


def _ssd_fwd(xc, dt_raw, proj, dt_bias, a_log, d_skip, norm_g, nb, nc, comm=None):
    t = xc.shape[0]
    row = lambda b, c: (b * nc + c, 0)
    par = lambda shape: pl.BlockSpec(shape, lambda b, c: (0, 0))

    def body(xc_ref, dt_ref, z_ref, dtb_ref, al_ref, ds_ref, ng_ref, e_ref, y_ref, ypre_ref, prev_ref, st_ref):
        @pl.when(pl.program_id(1) == 0)
        def _():
            st_ref[...] = jnp.zeros_like(st_ref)

        prev = st_ref[...]
        prev_ref[0, 0] = prev
        y, y_pre, new_state = _ssd_forward(xc_ref[...], dt_ref[...], z_ref[...].astype(F32), prev, dtb_ref[...],
                                           al_ref[...], ds_ref[...], ng_ref[...], e_ref[...])
        y_ref[...] = y.astype(y_ref.dtype)
        ypre_ref[...] = y_pre
        st_ref[...] = new_state

    return _call(
        "ssd_fwd", body, grid=(nb, nc), ins=[xc, dt_raw, proj, dt_bias, a_log, d_skip, norm_g, _head_expander()],
        in_specs=[pl.BlockSpec((CHUNK, SSM_CONV_DIM), row), pl.BlockSpec((CHUNK, 128), row),
                  pl.BlockSpec((CHUNK, SSM_INNER), lambda b, c: (b * nc + c, 1)),
                  par((1, 128)), par((1, 128)), par((1, SSM_INNER)), par((1, SSM_INNER)), par((128, SSM_INNER))],
        out_specs=[pl.BlockSpec((CHUNK, SSM_INNER), row), pl.BlockSpec((CHUNK, SSM_INNER), row),
                   pl.BlockSpec((1, 1) + SSD_STATE, lambda b, c: (b, c, 0, 0))],
        out_shape=[jax.ShapeDtypeStruct((t, SSM_INNER), BF16), jax.ShapeDtypeStruct((t, SSM_INNER), F32),
                   jax.ShapeDtypeStruct((nb, nc) + SSD_STATE, F32)],
        scratch=[pltpu.VMEM(SSD_STATE, F32)], comm=comm)


def _ssd_bwd(xc, dt_raw, proj, prevs, y_pre, dt_bias, a_log, d_skip, norm_g, dy, dproj, nb, nc, comm=None):
    t = xc.shape[0]
    row = lambda b, c: (b * nc + (nc - 1 - c), 0)
    par = lambda shape: pl.BlockSpec(shape, lambda b, c: (0, 0))
    z_spec = pl.BlockSpec((CHUNK, SSM_INNER), lambda b, c: (b * nc + (nc - 1 - c), 1))

    def body(xc_ref, dt_ref, z_ref, prev_ref, ypre_ref, dtb_ref, al_ref, ds_ref, ng_ref, e_ref, dy_ref, _,
             dxc_ref, ddt_ref, dz_ref, ddtb_ref, dal_ref, dds_ref, dng_ref, dst_ref):
        @pl.when(pl.program_id(1) == 0)
        def _():
            dst_ref[...] = jnp.zeros_like(dst_ref)

        @pl.when((pl.program_id(0) == 0) & (pl.program_id(1) == 0))
        def _():
            ddtb_ref[...] = jnp.zeros_like(ddtb_ref)
            dal_ref[...] = jnp.zeros_like(dal_ref)
            dds_ref[...] = jnp.zeros_like(dds_ref)
            dng_ref[...] = jnp.zeros_like(dng_ref)

        dxc, ddt, dz, dprev, ddtb, dal, dds, dng = _ssd_backward(
            xc_ref[...], dt_ref[...], z_ref[...].astype(F32), prev_ref[0, 0], ypre_ref[...], dtb_ref[...], al_ref[...],
            ds_ref[...], ng_ref[...], e_ref[...], dy_ref[...].astype(F32), dst_ref[...])
        dxc_ref[...] = dxc
        ddt_ref[...] = ddt.astype(ddt_ref.dtype)
        dz_ref[...] = dz.astype(dz_ref.dtype)
        dst_ref[...] = dprev
        ddtb_ref[...] += ddtb
        dal_ref[...] += dal
        dds_ref[...] += dds
        dng_ref[...] += dng

    return _call(
        "ssd_bwd", body, grid=(nb, nc),
        ins=[xc, dt_raw, proj, prevs, y_pre, dt_bias, a_log, d_skip, norm_g, _head_expander(), dy, dproj],
        in_specs=[pl.BlockSpec((CHUNK, SSM_CONV_DIM), row), pl.BlockSpec((CHUNK, DT_LANES), row), z_spec,
                  pl.BlockSpec((1, 1) + SSD_STATE, lambda b, c: (b, nc - 1 - c, 0, 0)),
                  pl.BlockSpec((CHUNK, SSM_INNER), row),
                  par((1, 128)), par((1, 128)), par((1, SSM_INNER)), par((1, SSM_INNER)), par((128, SSM_INNER)),
                  pl.BlockSpec((CHUNK, SSM_INNER), row), pl.BlockSpec(memory_space=pl.ANY)],
        out_specs=[pl.BlockSpec((CHUNK, SSM_CONV_DIM), row), pl.BlockSpec((CHUNK, DT_LANES), row), z_spec,
                   par((1, 128)), par((1, 128)), par((1, 128)), par((1, SSM_INNER))],
        out_shape=[jax.ShapeDtypeStruct((t, SSM_CONV_DIM), F32), jax.ShapeDtypeStruct((t, DT_LANES), BF16),
                   jax.ShapeDtypeStruct(dproj.shape, dproj.dtype), jax.ShapeDtypeStruct((1, 128), F32),
                   jax.ShapeDtypeStruct((1, 128), F32), jax.ShapeDtypeStruct((1, 128), F32),
                   jax.ShapeDtypeStruct((1, SSM_INNER), F32)],
        scratch=[pltpu.VMEM(SSD_STATE, F32)], comm=comm, aliases={11: 2})


def _loss_head(y, target):
    t, d = y.shape
    tm = _pick_tile(t, (256,))
    blk = pl.BlockSpec((tm, d), lambda i: (i, 0))

    def body(y_ref, t_ref, loss_ref, dy_ref):
        err = y_ref[...] - t_ref[...]
        dy_ref[...] = err * (1.0 / d)

        @pl.when(pl.program_id(0) == 0)
        def _():
            loss_ref[...] = jnp.zeros_like(loss_ref)

        loss_ref[...] += 0.5 * jnp.sum(jnp.mean(err * err, axis=-1, keepdims=True), axis=0, keepdims=True)

    return _call("loss_head", body, grid=(t // tm,), ins=[y, target], in_specs=[blk, blk],
                 out_specs=[pl.BlockSpec((1, 1), lambda i: (0, 0)), blk],
                 out_shape=[jax.ShapeDtypeStruct((1, 1), F32), jax.ShapeDtypeStruct((t, d), F32)])[0]


def _adamw_math(g, w, m, v):
    m_new = ADAM_B1 * m + (1.0 - ADAM_B1) * g
    v_new = ADAM_B2 * v + (1.0 - ADAM_B2) * jnp.square(g)
    m_hat = m_new / (1.0 - ADAM_B1 ** ADAM_STEP)
    v_hat = v_new / (1.0 - ADAM_B2 ** ADAM_STEP)
    delta = -ADAM_LR * (m_hat / (jnp.sqrt(v_hat) + ADAM_EPS) + ADAM_WD * w)
    return delta, m_new, v_new


def _adamw_sharded(name, parts, w, m, v, comm=None):
    _, a, b = w.shape
    tr = _pick_tile(a, (128,))
    nt = a // tr
    part_specs = [pl.BlockSpec((N_DEV, tr, b),
                               (lambda l, i, _k=k: (0, jnp.where(l == _k, i, jnp.where(l > _k, nt - 1, 0)), 0)))
                  for k in range(DEPTH)]
    blk = pl.BlockSpec((1, tr, b), lambda l, i: (l, i, 0))

    def body(*refs):
        p_refs = refs[:DEPTH]
        w_ref, m_ref, v_ref, g_out, d_out, m_out, v_out = refs[DEPTH:]
        for k in range(DEPTH):
            @pl.when(pl.program_id(0) == k)
            def _(p_ref=p_refs[k]):
                g = p_ref[0].astype(F32)
                for p in range(1, N_DEV):
                    g = g + p_ref[p].astype(F32)
                delta, m_new, v_new = _adamw_math(g, w_ref[0], m_ref[0], v_ref[0])
                g_out[0] = g
                d_out[0] = delta
                m_out[0] = m_new
                v_out[0] = v_new

    return _call(name, body, grid=(DEPTH, nt), ins=list(parts) + [w, m, v], in_specs=part_specs + [blk, blk, blk],
                 out_specs=[blk] * 4, out_shape=[jax.ShapeDtypeStruct(w.shape, F32)] * 4, comm=comm)


def _adamw_small(name, g, w, m, v):
    full = pl.BlockSpec(w.shape, lambda i: (0, 0))

    def body(g_ref, w_ref, m_ref, v_ref, d_out, m_out, v_out):
        delta, m_new, v_new = _adamw_math(g_ref[...], w_ref[...], m_ref[...], v_ref[...])
        d_out[...] = delta
        m_out[...] = m_new
        v_out[...] = v_new

    return _call(name, body, grid=(1,), ins=[g, w, m, v], in_specs=[full] * 4, out_specs=[full] * 3,
                 out_shape=[jax.ShapeDtypeStruct(w.shape, F32)] * 3)[0]


def _sum_parts(name, parts):
    n_parts, rows, cols = parts.shape
    tr = _pick_tile(rows, (512, 256, 128, 64, 32, 16, 8))

    def body(p_ref, o_ref):
        acc = p_ref[0]
        for p in range(1, n_parts):
            acc = acc + p_ref[p]
        o_ref[...] = acc

    return _call(name, body, grid=(rows // tr,), ins=[parts],
                 in_specs=[pl.BlockSpec((n_parts, tr, cols), lambda i: (0, i, 0))],
                 out_specs=[pl.BlockSpec((tr, cols), lambda i: (i, 0))],
                 out_shape=[jax.ShapeDtypeStruct((rows, cols), parts.dtype)])[0][0]


W_IN_SHARD = IN_COLS // N_DEV


def _pack_w_in(gathered):
    r = gathered.shape[1]
    tr = _pick_tile(r, (128,))

    def body(g_ref, main_ref, dt_ref):
        w = jnp.concatenate([g_ref[j].astype(F32) for j in range(N_DEV)], axis=1)
        main_ref[...] = jnp.concatenate([w[:, :XBC_COL], w[:, GA_COL:], w[:, XBC_COL:DT_COL]],
                                        axis=1).astype(main_ref.dtype)
        dt_ref[...] = jnp.concatenate([w[:, DT_COL:GA_COL], jnp.zeros((tr, DT_LANES - SSM_HEADS), F32)],
                                      axis=1).astype(dt_ref.dtype)

    return _call("pack_w_in", body, grid=(r // tr,), ins=[gathered],
                 in_specs=[pl.BlockSpec((N_DEV, tr, W_IN_SHARD), lambda i: (0, i, 0))],
                 out_specs=[pl.BlockSpec((tr, P_COLS), lambda i: (i, 0)), pl.BlockSpec((tr, DT_LANES), lambda i: (i, 0))],
                 out_shape=[jax.ShapeDtypeStruct((r, P_COLS), gathered.dtype),
                            jax.ShapeDtypeStruct((r, DT_LANES), gathered.dtype)])[0]


def _unpack_w_in(main, dt):
    r = main.shape[0]
    tr = _pick_tile(r, (128,))

    def body(main_ref, dt_ref, o_ref):
        main = main_ref[...].astype(F32)
        w = jnp.concatenate([main[:, :P_GATE], main[:, P_XBC:], dt_ref[...].astype(F32)[:, :SSM_HEADS],
                             main[:, P_GATE:P_XBC]], axis=1)
        for j in range(N_DEV):
            o_ref[j] = w[:, j * W_IN_SHARD:(j + 1) * W_IN_SHARD].astype(o_ref.dtype)

    return _call("unpack_w_in", body, grid=(r // tr,), ins=[main, dt],
                 in_specs=[pl.BlockSpec((tr, P_COLS), lambda i: (i, 0)), pl.BlockSpec((tr, DT_LANES), lambda i: (i, 0))],
                 out_specs=[pl.BlockSpec((N_DEV, tr, W_IN_SHARD), lambda i: (0, i, 0))],
                 out_shape=[jax.ShapeDtypeStruct((N_DEV, r, W_IN_SHARD), main.dtype)])[0][0]


def _pad_heads(v):
    return jnp.pad(v, (0, 128 - SSM_HEADS)).reshape(1, 128)


def _run_step(x, mem, target, small, ex):
    nb, s, d = x.shape
    t = nb * s
    nc = s // CHUNK
    rows = _pick_tile(t, (256,))
    rows_wide = _pick_tile(t, (512, 256))
    tq = _pick_tile(s, (512, 256))
    vec = lambda a: a.reshape(1, -1)
    full1 = lambda shape: pl.BlockSpec(shape, lambda i: (0,) * len(shape))
    row1 = lambda tm, w: pl.BlockSpec((tm, w), lambda i: (i, 0))
    sds = jax.ShapeDtypeStruct

    def mm(call, l, a, b, **kw):
        comm = ex.before(call, l)
        out, comm_outs = _matmul(call, a, b, comm=comm, **kw)
        if comm is not None:
            ex.after(call, l, comm_outs)
        return out

    def stage_bwd(call, l, *args, **kw):
        comm = ex.before(call, l)
        outs, comm_outs = _stage_bwd(call, *args, comm=comm, **kw)
        if comm is not None:
            ex.after(call, l, comm_outs)
        return outs

    def stage_fwd(call, l, *args):
        comm = ex.before(call, l)
        outs, comm_outs = _stage_fwd(call, *args, comm=comm)
        if comm is not None:
            ex.after(call, l, comm_outs)
        return outs

    mem_specs = [row1(256, d), full1((1, d)), full1((1, d))]
    mem_ins = [mem.reshape(nb * MEM_LEN, d), vec(small["mem_ln_g"]), vec(small["mem_ln_b"])]
    (mem_n,) = stage_fwd("memln_fwd", 0, _memln_block, (nb * MEM_LEN // 256,), mem_ins, mem_specs,
                          [sds((nb * MEM_LEN, d), BF16)], [row1(256, d)])

    h = x.reshape(t, d)
    h_bf = h.astype(BF16)
    ln_specs = [row1(rows_wide, d), row1(rows_wide, d), full1((1, d)), full1((1, d))]
    saved = []
    for l in range(DEPTH):
        sv = {"h_bf": h_bf}
        w_p, w_dt = ex.weight("w_in", l)
        proj = mm("mm_in", l, h_bf, w_p, out_dtype=BF16)
        dt_raw = mm("mm_dt", l, h_bf, w_dt)
        sv["proj"] = proj
        sgu_ins = [proj, vec(small["sg_ln_g"][l]), vec(small["sg_ln_b"][l]), small["sg_w"][l], small["sg_b"][l].T]
        sgu_specs = [pl.BlockSpec((rows, 2 * d), lambda i: (i, 0)), full1((1, d)), full1((1, d)),
                     full1((SG_GROUPS, CHUNK, CHUNK)), full1((CHUNK, SG_GROUPS))]
        (a_out,) = stage_fwd("sgu_fwd", l, _sgu_block, (t // rows,), sgu_ins, sgu_specs, [sds((t, d), BF16)],
                              [row1(rows, d)])
        sv["sgu"] = (sgu_ins, sgu_specs)
        sv["a_out"] = a_out
        cw = 256
        conv_ins = [proj, small["conv_w"][l], vec(small["conv_b"][l])]
        conv_specs = [pl.BlockSpec((s, cw), lambda j, b: (b, P_XBC // cw + j)),
                      pl.BlockSpec((SSM_CONV, cw), lambda j, b: (0, j)), pl.BlockSpec((1, cw), lambda j, b: (0, j))]
        conv_out_spec = pl.BlockSpec((s, cw), lambda j, b: (b, j))
        (xc,) = stage_fwd("conv_fwd", l, _conv_block, (SSM_CONV_DIM // cw, nb), conv_ins, conv_specs,
                           [sds((t, SSM_CONV_DIM), F32)], [conv_out_spec])
        sv["conv"] = (conv_ins, conv_specs, conv_out_spec)
        ssd_par = [_pad_heads(small["dt_bias"][l]), _pad_heads(small["a_log"][l]),
                   vec(jnp.repeat(small["d_skip"][l], SSM_HEADDIM)), vec(small["ssm_norm_g"][l])]
        comm = ex.before("ssd_fwd", l)
        (y_ssd, y_pre, prevs), comm_outs = _ssd_fwd(xc, dt_raw, proj, *ssd_par, nb, nc, comm=comm)
        if comm is not None:
            ex.after("ssd_fwd", l, comm_outs)
        sv["ssd"] = (xc, dt_raw, prevs, y_pre, ssd_par)
        sv["y_ssd"] = y_ssd
        br_a = mm("mm_sq", l, a_out, ex.weight("p_a", l), out_dtype=BF16)
        br_b = mm("mm_pb", l, y_ssd, ex.weight("p_b", l), out_dtype=BF16)
        merge_ins = [proj, br_a, br_b]
        merge_out_spec = row1(rows_wide, d)
        merge_specs = [pl.BlockSpec((rows_wide, 2 * d), lambda i: (i, P_GATE // (2 * d))), merge_out_spec, merge_out_spec]
        sv["merge"] = (merge_ins, merge_specs, merge_out_spec)
        ln_par = [(vec(small["ln_g"][l, k]), vec(small["ln_b"][l, k])) for k in range(3)]

        def fused(call, a_fn, a_ins, a_specs, tm, w, x_in, par):
            comm = ex.before(call, l)
            outs, comm_outs = _matmul_lnres(call, a_fn, a_ins, a_specs, tm, w, x_in, *par, comm=comm)
            if comm is not None:
                ex.after(call, l, comm_outs)
            return outs

        merged, y1, h1, h1_bf = fused("mm_mix_ln", _merge_block, merge_ins, merge_specs, rows_wide,
                                      ex.weight("w_mix_o", l), h, ln_par[0])
        sv["merged"] = merged
        sv["ln1"] = [h, y1, *ln_par[0]]
        q = mm("mm_sq", l, h1_bf, ex.weight("w_xq", l), out_dtype=BF16)
        kv = mm("mm_kv", l, mem_n, ex.weight("w_xkv", l), out_dtype=BF16)
        attn_ins = [q, kv]
        attn_out_spec = pl.BlockSpec((tq, d), lambda b, i: (b * (s // tq) + i, 0))
        attn_specs = [attn_out_spec, pl.BlockSpec((MEM_LEN, 2 * d), lambda b, i: (b, 0))]
        (o,) = stage_fwd("attn_fwd", l, _attn_block, (nb, s // tq), attn_ins, attn_specs, [sds((t, d), BF16)],
                          [attn_out_spec])
        sv["attn"] = (attn_ins, attn_specs, attn_out_spec)
        sv["o"] = o
        sv["h1_bf"] = h1_bf
        y2, h2, h2_bf = fused("mm_xo_ln", None, [o], [row1(rows_wide, d)], rows_wide, ex.weight("w_xo", l), h1, ln_par[1])
        sv["ln2"] = [h1, y2, *ln_par[1]]
        sv["h2_bf"] = h2_bf
        gu = mm("mm_ffn_in", l, h2_bf, ex.weight("w_ffn_in", l), out_dtype=BF16)
        act, y3, h3, h3_bf = fused("mm_ffn_out_ln", _swiglu_block, [gu], [row1(rows, 2 * FFN_HIDDEN)], rows,
                                   ex.weight("w_ffn_out", l), h2, ln_par[2])
        sv["gu"] = gu
        sv["act"] = act
        sv["ln3"] = [h2, y3, *ln_par[2]]
        h, h_bf = h3, h3_bf
        saved.append(sv)

    loss, dh = _loss_head(h, target.reshape(t, d))

    g_small = {n: [None] * DEPTH for n in SMALL_REP + SMALL_SH if n not in ("mem_ln_g", "mem_ln_b")}
    dmem_n = []
    ln_grads = [(0, (), F32), (1, (), BF16), (2, (0,), F32), (3, (0,), F32)]
    for l in reversed(range(DEPTH)):
        sv = saved[l]
        dln_g, dln_b = [None] * 3, [None] * 3
        dres, dy3, dln_g[2], dln_b[2] = stage_bwd("lnres_bwd", l, _lnres_block, (t // rows_wide,), sv["ln3"], ln_specs,
                                                  [(dh,)], [row1(rows_wide, d)], ln_grads)
        ex.grad("w_ffn_out", l, mm("mm_ffn_out_dw", l, sv["act"], dy3, ta=True, out_dtype=BF16))
        (dgu,) = stage_bwd("swiglu_bwd", l, _swiglu_block, (t // rows,), [sv["gu"]], [row1(rows, 2 * FFN_HIDDEN)],
                           [], [], [(0, (), BF16)], ct_product=(dy3, row1(rows, d), ex.weight("w_ffn_out", l)))
        ex.grad("w_ffn_in", l, mm("mm_ffn_in_dw", l, sv["h2_bf"], dgu, ta=True, out_dtype=BF16))
        dh2 = mm("mm_ffn_in_dx", l, dgu, ex.weight("w_ffn_in", l), tb=True, add=dres)
        dres, dy2, dln_g[1], dln_b[1] = stage_bwd("lnres_bwd", l, _lnres_block, (t // rows_wide,), sv["ln2"], ln_specs,
                                                  [(dh2,)], [row1(rows_wide, d)], ln_grads)
        ex.grad("w_xo", l, mm("mm_sq_dw", l, sv["o"], dy2, ta=True, out_dtype=BF16))
        do = mm("mm_sq_dx", l, dy2, ex.weight("w_xo", l), tb=True, out_dtype=BF16)
        attn_ins, attn_specs, attn_out_spec = sv["attn"]
        dq, dkv = stage_bwd("attn_bwd", l, _attn_block, (nb, s // tq), attn_ins, attn_specs, [(do,)], [attn_out_spec],
                            [(0, (), BF16), (1, (1,), F32)])
        ex.grad("w_xq", l, mm("mm_sq_dw", l, sv["h1_bf"], dq, ta=True, out_dtype=BF16))
        dh1 = mm("mm_sq_dx", l, dq, ex.weight("w_xq", l), tb=True, add=dres)
        ex.grad("w_xkv", l, mm("mm_kv_dw", l, mem_n, dkv, ta=True, out_dtype=BF16))
        dmem_n.append(mm("mm_kv_dx", l, dkv, ex.weight("w_xkv", l), tb=True))
        dres, dy1, dln_g[0], dln_b[0] = stage_bwd("lnres_bwd", l, _lnres_block, (t // rows_wide,), sv["ln1"], ln_specs,
                                                  [(dh1,)], [row1(rows_wide, d)], ln_grads)
        g_small["ln_g"][l] = jnp.concatenate(dln_g, axis=0)
        g_small["ln_b"][l] = jnp.concatenate(dln_b, axis=0)
        ex.grad("w_mix_o", l, mm("mm_sq_dw", l, sv["merged"], dy1, ta=True, out_dtype=BF16))
        merge_ins, merge_specs, merge_out_spec = sv["merge"]
        dproj, dbr_a, dbr_b = stage_bwd("merge_bwd", l, _merge_block, (t // rows_wide,), merge_ins, merge_specs, [], [],
                                        [(0, (), BF16, ((t, P_COLS), merge_specs[0])), (1, (), BF16), (2, (), BF16)],
                                        ct_product=(dy1, merge_out_spec, ex.weight("w_mix_o", l)))
        ex.grad("p_a", l, mm("mm_sq_dw", l, sv["a_out"], dbr_a, ta=True, out_dtype=BF16))
        da_out = mm("mm_sq_dx", l, dbr_a, ex.weight("p_a", l), tb=True, out_dtype=BF16)
        ex.grad("p_b", l, mm("mm_pb_dw", l, sv["y_ssd"], dbr_b, ta=True, out_dtype=BF16))
        dy_ssd = mm("mm_pb_dx", l, dbr_b, ex.weight("p_b", l), tb=True, out_dtype=BF16)
        sgu_ins, sgu_specs = sv["sgu"]
        dproj, dsg_ln_g, dsg_ln_b, dsg_w, dsg_b = stage_bwd(
            "sgu_bwd", l, _sgu_block, (t // CHUNK,), sgu_ins, [row1(CHUNK, 2 * d)] + sgu_specs[1:], [(da_out,)],
            [row1(CHUNK, d)],
            [(0, (), BF16, ((t, P_COLS), row1(CHUNK, 2 * d)), dproj), (1, (0,), F32), (2, (0,), F32), (3, (0,), F32),
             (4, (0,), F32)])
        g_small["sg_ln_g"][l], g_small["sg_ln_b"][l], g_small["sg_w"][l], g_small["sg_b"][l] = (
            dsg_ln_g[0], dsg_ln_b[0], dsg_w, dsg_b.T)
        xc, dt_raw, prevs, y_pre, ssd_par = sv["ssd"]
        comm = ex.before("ssd_bwd", l)
        (dxc, ddt, dproj, ddtb, dal, dds, dng), comm_outs = _ssd_bwd(xc, dt_raw, sv["proj"], prevs, y_pre, *ssd_par,
                                                                     dy_ssd, dproj, nb, nc, comm=comm)
        if comm is not None:
            ex.after("ssd_bwd", l, comm_outs)
        g_small["dt_bias"][l], g_small["a_log"][l], g_small["d_skip"][l] = (
            ddtb[0, :SSM_HEADS], dal[0, :SSM_HEADS], dds[0, :SSM_HEADS])
        g_small["ssm_norm_g"][l] = dng[0]
        conv_ins, conv_specs, conv_out_spec = sv["conv"]
        dproj, dconv_w, dconv_b = stage_bwd("conv_bwd", l, _conv_block, (SSM_CONV_DIM // 256, nb), conv_ins, conv_specs,
                                            [(dxc,)], [conv_out_spec],
                                            [(0, (), BF16, ((t, P_COLS), conv_specs[0]), dproj), (1, (1,), F32),
                                             (2, (1,), F32)])
        g_small["conv_w"][l], g_small["conv_b"][l] = dconv_w, dconv_b[0]
        if l == 0:
            dmg, dmb = stage_bwd("memln_bwd", l, _memln_block, (nb * MEM_LEN // 256,), mem_ins, mem_specs,
                                 [tuple(dmem_n)], [row1(256, d)], [(1, (0,), F32), (2, (0,), F32)])
            done = {n: jnp.stack(g, axis=0) for n, g in g_small.items()}
            done["mem_ln_g"], done["mem_ln_b"] = dmg[0], dmb[0]
            ex.small_grads(done)
        w_p, w_dt = ex.weight("w_in", l)
        g_dt = mm("mm_dt_dw", l, sv["h_bf"], ddt, ta=True, out_dtype=BF16)
        ex.grad("w_in", l, _unpack_w_in(mm("mm_in_dw", l, sv["h_bf"], dproj, ta=True, out_dtype=BF16), g_dt))
        dh = mm("mm_in_dx", l, dproj, w_p, tb=True, add=dres, extra=(ddt, w_dt))

    return loss, dh.reshape(nb, s, d)


def _pack_flat(arrays, rows):
    flat = jnp.concatenate([a.reshape(-1) for a in arrays])
    return jnp.pad(flat, (0, rows * 128 - flat.shape[0])).reshape(rows, 128)


def _unpack_flat(packed, shapes):
    lead = packed.shape[:-2]
    flat = packed.reshape(lead + (-1,))
    out, pos = [], 0
    for shape in shapes:
        n = math.prod(shape)
        out.append(flat[..., pos:pos + n].reshape(lead + tuple(shape)))
        pos += n
    return out


def _small_rows(n_elems):
    return -(-n_elems // (128 * SMALL_ROW_TILE)) * SMALL_ROW_TILE


def _from_shards(name, gathered):
    _, a, b = gathered.shape
    if name == "w_in":
        return tuple(_pack_w_in(gathered))
    if name in BIG_COL_SHARDED:
        return _join_columns(gathered)
    return gathered.reshape(N_DEV * a, b)


def _to_shards(name, g):
    if name == "w_in":
        return g
    if name in BIG_COL_SHARDED:
        return _split_columns(g)
    a, b = g.shape
    return g.reshape(N_DEV, a // N_DEV, b)


def _join_columns(gathered):
    _, r, b = gathered.shape
    tr = _pick_tile(r, (128,))

    def body(g_ref, o_ref):
        o_ref[...] = jnp.concatenate([g_ref[j].astype(F32) for j in range(N_DEV)], axis=1).astype(o_ref.dtype)

    return _call("join_columns", body, grid=(r // tr,), ins=[gathered],
                 in_specs=[pl.BlockSpec((N_DEV, tr, b), lambda i: (0, i, 0))],
                 out_specs=[pl.BlockSpec((tr, N_DEV * b), lambda i: (i, 0))],
                 out_shape=[jax.ShapeDtypeStruct((r, N_DEV * b), gathered.dtype)])[0][0]


def _split_columns(full):
    r, nb = full.shape
    b = nb // N_DEV
    tr = _pick_tile(r, (128,))

    def body(f_ref, o_ref):
        w = f_ref[...].astype(F32)
        for j in range(N_DEV):
            o_ref[j] = w[:, j * b:(j + 1) * b].astype(o_ref.dtype)

    return _call("split_columns", body, grid=(r // tr,), ins=[full],
                 in_specs=[pl.BlockSpec((tr, nb), lambda i: (i, 0))],
                 out_specs=[pl.BlockSpec((N_DEV, tr, b), lambda i: (0, i, 0))],
                 out_shape=[jax.ShapeDtypeStruct((N_DEV, r, b), full.dtype)])[0][0]


class _MeshExchange:
    def __init__(self, shards_bf16, first):
        self.shards = shards_bf16
        self.full = dict(first)
        self.pieces = {}
        self.grads = {}
        self.to_send = {}
        self.received = {}
        self.small = None
        self.small_gathered = None

    def weight(self, name, l):
        if (name, l) not in self.full:
            got = jnp.concatenate([self.pieces[(name, l, q)] for q in range(W_IN_PIECES)], axis=1)
            self.full[(name, l)] = _from_shards(name, got)
        return self.full[(name, l)]

    def grad(self, name, l, g):
        self.grads[(name, l)] = g

    def small_grads(self, done):
        self.small = done

    def partial_sums(self, name, l):
        if (name, l, None) in self.received:
            return self.received[(name, l, None)]
        return jnp.concatenate([self.received[(name, l, q)] for q in range(W_IN_PIECES)], axis=1)

    def _slices(self, name, l, piece):
        n_rows = D_MODEL // W_IN_PIECES
        key, rows = (name, l), None if piece is None else (piece * n_rows, n_rows)
        if key not in self.to_send:
            self.to_send[key] = _to_shards(name, self.grads[key])
        return self.to_send[key], rows

    def before(self, call, l):
        comm = _Comm()
        for name, layer, piece in GATHER_PLAN.get((call, l), ()):
            n_rows = D_MODEL // W_IN_PIECES
            comm.gathers.append((self.shards[name], layer, None if piece is None else (piece * n_rows, n_rows)))
        if (call, l) == SMALL_GATHER_CALL:
            names = SMALL_REP + SMALL_SH
            rows = _small_rows(sum(math.prod(self.small[n].shape) for n in names))
            comm.gathers.append((_pack_flat([self.small[n] for n in names], rows), None, None))
        for name, layer, piece in SCATTER_PLAN.get((call, l), ()):
            comm.scatters.append(self._slices(name, layer, piece))
        return comm if comm.gathers or comm.scatters else None

    def after(self, call, l, outs):
        gathers = list(GATHER_PLAN.get((call, l), ()))
        for (name, layer, piece), out in zip(gathers, outs):
            if piece is None:
                self.full[(name, layer)] = _from_shards(name, out)
            else:
                self.pieces[(name, layer, piece)] = out
        outs = outs[len(gathers):]
        if (call, l) == SMALL_GATHER_CALL:
            self.small_gathered = outs[0]
            outs = outs[1:]
        for item, out in zip(SCATTER_PLAN.get((call, l), ()), outs):
            self.received[item] = out


def kernel(x, mem, mem_ln_g, mem_ln_b, w_in, sg_ln_g, sg_ln_b, sg_w, sg_b, conv_w, conv_b, dt_bias, a_log, d_skip, ssm_norm_g, p_a, p_b, w_mix_o, w_xq, w_xkv, w_xo, w_ffn_in, w_ffn_out, ln_g, ln_b, loss_target, m_mem_ln_g, m_mem_ln_b, m_w_in, m_sg_ln_g, m_sg_ln_b, m_sg_w, m_sg_b, m_conv_w, m_conv_b, m_dt_bias, m_a_log, m_d_skip, m_ssm_norm_g, m_p_a, m_p_b, m_w_mix_o, m_w_xq, m_w_xkv, m_w_xo, m_w_ffn_in, m_w_ffn_out, m_ln_g, m_ln_b, v_mem_ln_g, v_mem_ln_b, v_w_in, v_sg_ln_g, v_sg_ln_b, v_sg_w, v_sg_b, v_conv_w, v_conv_b, v_dt_bias, v_a_log, v_d_skip, v_ssm_norm_g, v_p_a, v_p_b, v_w_mix_o, v_w_xq, v_w_xkv, v_w_xo, v_w_ffn_in, v_w_ffn_out, v_ln_g, v_ln_b):
    args = dict(locals())
    w = {n: args[n] for n in WEIGHTS}
    m = {n: args["m_" + n] for n in WEIGHTS}
    v = {n: args["v_" + n] for n in WEIGHTS}
    me = 4 * lax.axis_index("x") + 2 * lax.axis_index("y") + lax.axis_index("c")

    shards = {n: w[n].astype(BF16) for n in BIG}
    sh_shapes = [w[n].shape for n in SMALL_SH]
    first = _Comm()
    first.gathers.append((shards["w_in"], 0, None))
    first.gathers.append((_pack_flat([w[n] for n in SMALL_SH], _small_rows(sum(math.prod(s) for s in sh_shapes))), None,
                          None))
    w_in0, small_sh = _comm_only("gather_first", first)
    small = {n: w[n] for n in SMALL_REP}
    for n, sh in zip(SMALL_SH, _unpack_flat(small_sh, sh_shapes)):
        small[n] = sh.transpose(1, 2, 0, 3).reshape(sh.shape[1], sh.shape[2], N_DEV * sh.shape[3])

    ex = _MeshExchange(shards, {("w_in", 0): _from_shards("w_in", w_in0)})
    loss, grad_x = _run_step(x, mem, loss_target, small, ex)
    loss = lax.psum(loss[0, 0], ("x", "y", "c"))

    out = {}
    for n in BIG[1:] + BIG[:1]:
        comm = ex.before("adamw_" + n, 0)
        out[n], comm_outs = _adamw_sharded("adamw_" + n, [ex.partial_sums(n, l) for l in range(DEPTH)], w[n], m[n], v[n],
                                           comm=comm)
        if comm is not None:
            ex.after("adamw_" + n, 0, comm_outs)
    names = SMALL_REP + SMALL_SH
    g_small = dict(zip(names, _unpack_flat(_sum_parts("sum_small_grads", ex.small_gathered),
                                           [ex.small[n].shape for n in names])))
    for n in names:
        g = g_small[n]
        if n in SMALL_SH:
            width = w[n].shape[-1]
            g = lax.dynamic_slice_in_dim(g, me * width, width, axis=-1)
        two_d = (-1, w[n].shape[-1])
        res = _adamw_small("adamw_" + n, g.reshape(two_d), w[n].reshape(two_d), m[n].reshape(two_d), v[n].reshape(two_d))
        out[n] = [g] + [r.reshape(w[n].shape) for r in res]

    results = []
    for k in range(4):
        results.extend(out[n][k] for n in WEIGHTS)
    return (loss, grad_x, *results)
```

```python
import functools
import math

import jax
import jax.numpy as jnp
from jax import lax
from jax.experimental import pallas as pl
from jax.experimental.pallas import tpu as pltpu

F32 = jnp.float32
BF16 = jnp.bfloat16
HIGHEST = lax.Precision.HIGHEST

N_DEV = 8
D_MODEL = 1024
DEPTH = 2
MEM_LEN = 256
CHUNK = 128
SG_GROUPS = 8
SSM_INNER = 2048
SSM_HEADDIM = 64
SSM_HEADS = 32
SSM_STATE = 128
SSM_GROUPS = 4
SSM_RPG = 8
SSM_CONV = 4
SSM_CONV_DIM = 3072
X_HEADS = 4
X_HEADDIM = 256
FFN_HIDDEN = 2816
ALPHA = float((2 * DEPTH) ** 0.25)
LN_EPS = 1e-5
RMS_EPS = 1e-5
XBC_COL = 4096
DT_COL = 7168
GA_COL = 7200
IN_COLS = 9248
P_GATE = 4096
P_XBC = 6144
P_COLS = 9216
DT_LANES = 128

ADAM_LR = 0.001
ADAM_B1 = 0.9
ADAM_B2 = 0.999
ADAM_EPS = 1e-08
ADAM_WD = 0.01
ADAM_STEP = 10

VMEM_LIMIT = 48 * 1024 * 1024
SMALL_ROW_TILE = 256

BIG = ("w_in", "p_a", "p_b", "w_mix_o", "w_xq", "w_xkv", "w_xo", "w_ffn_in", "w_ffn_out")
BIG_COL_SHARDED = ("w_in", "w_xkv", "w_ffn_in")
SMALL_REP = ("mem_ln_g", "mem_ln_b", "sg_ln_g", "sg_ln_b", "sg_w", "sg_b", "conv_b", "dt_bias", "a_log", "d_skip",
             "ssm_norm_g")
SMALL_SH = ("conv_w", "ln_g", "ln_b")
WEIGHTS = ("mem_ln_g", "mem_ln_b", "w_in", "sg_ln_g", "sg_ln_b", "sg_w", "sg_b", "conv_w", "conv_b", "dt_bias", "a_log",
           "d_skip", "ssm_norm_g", "p_a", "p_b", "w_mix_o", "w_xq", "w_xkv", "w_xo", "w_ffn_in", "w_ffn_out", "ln_g", "ln_b")

W_IN_PIECES = 4
GATHER_PLAN = {("sgu_fwd", 0): [("w_in", 1, 0)], ("conv_fwd", 0): [("w_in", 1, 1)], ("mm_mix_ln", 0): [("w_in", 1, 2)],
               ("mm_xo_ln", 0): [("w_in", 1, 3)], ("mm_ffn_in", 0): [("w_ffn_out", 0, None)],
               ("conv_fwd", 1): [("w_ffn_out", 1, None)]}
SCATTER_PLAN = {("mm_ffn_in_dw", 0): [("w_in", 1, 2)], ("mm_ffn_in_dx", 0): [("w_in", 1, 3)],
                ("adamw_w_ffn_in", 0): [("w_in", 0, 2)], ("adamw_w_xkv", 0): [("w_in", 0, 3)]}
for _l in range(DEPTH):
    GATHER_PLAN[("mm_in", _l)] = [(n, _l, None) for n in ("p_a", "p_b", "w_mix_o", "w_xq", "w_xkv", "w_xo")]
    GATHER_PLAN[("ssd_fwd", _l)] = [("w_ffn_in", _l, None)]
    SCATTER_PLAN[("swiglu_bwd", _l)] = [("w_ffn_out", _l, None)]
    SCATTER_PLAN[("sgu_bwd", _l)] = [("w_mix_o", _l, None), ("p_a", _l, None), ("w_xo", _l, None)]
    SCATTER_PLAN[("ssd_bwd", _l)] = [("w_ffn_in", _l, None), ("w_xkv", _l, None), ("w_xq", _l, None)]
    SCATTER_PLAN[("conv_bwd", _l)] = [("p_b", _l, None)]
    SCATTER_PLAN[("mm_in_dx", _l)] = [("w_in", _l, 0), ("w_in", _l, 1)]
SMALL_GATHER_CALL = ("mm_in_dw", 0)


def _layer_norm(x, g, b):
    mu = jnp.mean(x, axis=-1, keepdims=True)
    xc = x - mu
    var = jnp.mean(xc * xc, axis=-1, keepdims=True)
    return xc * lax.rsqrt(var + LN_EPS) * g + b


def _gelu(x):
    return 0.5 * x * (1.0 + lax.erf(x * (1.0 / math.sqrt(2.0))))


def _sigmoid(x):
    return 0.5 * jnp.tanh(0.5 * x) + 0.5


def _silu(x):
    return x * _sigmoid(x)


def _softplus(x):
    return jnp.maximum(x, 0.0) + jnp.log1p(jnp.exp(-jnp.abs(x)))


def _causal_mask():
    r = lax.broadcasted_iota(jnp.int32, (CHUNK, CHUNK), 0)
    c = lax.broadcasted_iota(jnp.int32, (CHUNK, CHUNK), 1)
    return r >= c


def _sgu_block(uv, ln_g, ln_b, w, sb):
    gu = _gelu(uv[:, :D_MODEL])
    vn = _layer_norm(_gelu(uv[:, D_MODEL:]), ln_g, ln_b).astype(BF16)
    causal = _causal_mask()
    width = D_MODEL // SG_GROUPS
    wgs = [jnp.where(causal, w[g], 0.0).astype(BF16) for g in range(SG_GROUPS)]
    chunks = []
    for c in range(uv.shape[0] // CHUNK):
        vc = vn[c * CHUNK:(c + 1) * CHUNK, :]
        chunks.append(jnp.concatenate(
            [jnp.dot(wgs[g], vc[:, g * width:(g + 1) * width], preferred_element_type=F32) + sb[:, g:g + 1]
             for g in range(SG_GROUPS)], axis=1))
    return (gu * jnp.concatenate(chunks, axis=0),)


GROUP_W = SSM_RPG * SSM_HEADDIM
NT_DIMS = (((1,), (1,)), ((), ()))
TN_DIMS = (((0,), (0,)), ((), ()))


def _mxu(a, b, dims=(((1,), (0,)), ((), ()))):
    return lax.dot_general(a.astype(BF16), b.astype(BF16), dims, preferred_element_type=F32)


def _head_expander():
    return (jnp.arange(SSM_INNER)[None, :] // SSM_HEADDIM == jnp.arange(128)[:, None]).astype(BF16)


def _bf16_terms(x, n):
    terms = []
    for _ in range(n):
        t = x.astype(BF16)
        terms.append(t)
        x = x - t.astype(F32)
    return terms


def _expand_heads(q, e):
    return sum(jnp.dot(t, e, preferred_element_type=F32) for t in _bf16_terms(q, 2))


def _reduce_heads(v, e, terms=2):
    return sum(lax.dot_general(t, e, NT_DIMS, preferred_element_type=F32) for t in _bf16_terms(v, terms))


def _reduce_heads_of_column_sums(v, e):
    sums = jnp.broadcast_to(jnp.sum(v, axis=0, keepdims=True), (8, v.shape[1]))
    return _reduce_heads(sums, e)[0:1, :]


def _ssd_common(xc, dtraw, dt_bias, a_log, e):
    xs = xc[:, :SSM_INNER]
    pre = dtraw + dt_bias
    dt = _softplus(pre)
    a = -jnp.exp(a_log)
    r_i = lax.broadcasted_iota(jnp.int32, (CHUNK, CHUNK), 0)
    c_i = lax.broadcasted_iota(jnp.int32, (CHUNK, CHUNK), 1)
    tril = jnp.where(r_i >= c_i, 1.0, 0.0).astype(F32)
    cs = jnp.dot(tril, dt * a, precision=HIGHEST, preferred_element_type=F32)
    cs_last = cs[CHUNK - 1:CHUNK, :]
    decay_in = jnp.exp(cs)
    decay_st = jnp.exp(cs_last - cs)
    dt_x = _expand_heads(dt, e)
    w_st_x = _expand_heads(dt * decay_st, e)
    decay_in_x = _expand_heads(decay_in, e)
    return dict(xs=xs, pre=pre, dt=dt, a=a, lower=r_i >= c_i, upper=c_i >= r_i, cs=cs, cs_t=cs.T, decay_in=decay_in,
                decay_st=decay_st, chunk_decay=jnp.exp(cs_last), dt_x=dt_x, w_st_x=w_st_x, decay_in_x=decay_in_x,
                chunk_decay_x=decay_in_x[CHUNK - 1:CHUNK, :], xdt=xs * dt_x, x_st=(xs * w_st_x).astype(BF16),
                low=lax.broadcasted_iota(jnp.int32, (CHUNK, 128), 1) < SSM_HEADDIM)


def _pair_decay(c, h):
    return jnp.exp(jnp.where(c["lower"], c["cs"][:, h:h + 1] - c["cs_t"][h:h + 1, :], -1e30))


def _pair_decay_t(c, h):
    return jnp.exp(jnp.where(c["upper"], c["cs_t"][h:h + 1, :] - c["cs"][:, h:h + 1], -1e30))


def _ssd_forward(xc, dtraw, z, prev, dt_bias, a_log, d_skip_x, norm_g, e):
    c = _ssd_common(xc, dtraw, dt_bias, a_log, e)
    y_groups, new_states = [], []
    for g in range(SSM_GROUPS):
        lanes = slice(g * GROUP_W, (g + 1) * GROUP_W)
        bg = xc[:, SSM_INNER + g * SSM_STATE:SSM_INNER + (g + 1) * SSM_STATE]
        cg = xc[:, SSM_INNER + (SSM_GROUPS + g) * SSM_STATE:SSM_INNER + (SSM_GROUPS + g + 1) * SSM_STATE].astype(BF16)
        pg = prev[g * SSM_STATE:(g + 1) * SSM_STATE, :]
        cb = _mxu(cg, bg, NT_DIMS)
        y_in = _mxu(cg, pg) * c["decay_in_x"][:, lanes]
        new_states.append(pg * c["chunk_decay_x"][:, lanes] + _mxu(bg.T, c["x_st"][:, lanes]))
        pairs = []
        for j in range(SSM_RPG // 2):
            h0 = g * SSM_RPG + 2 * j
            xp = c["xdt"][:, 128 * (h0 // 2):128 * (h0 // 2 + 1)]
            pairs.append(_mxu(cb * _pair_decay(c, h0), jnp.where(c["low"], xp, 0.0))
                         + _mxu(cb * _pair_decay(c, h0 + 1), jnp.where(c["low"], 0.0, xp)))
        y_groups.append(jnp.concatenate(pairs, axis=1) + y_in)
    y_pre = jnp.concatenate(y_groups, axis=1) + c["xs"] * d_skip_x
    gated = y_pre * _silu(z)
    normed = [gated[:, g * GROUP_W:(g + 1) * GROUP_W] for g in range(SSM_GROUPS)]
    normed = [yg * lax.rsqrt(jnp.mean(yg * yg, axis=-1, keepdims=True) + RMS_EPS) for yg in normed]
    return jnp.concatenate(normed, axis=1) * norm_g, y_pre, jnp.concatenate(new_states, axis=0)


def _ssd_backward(xc, dtraw, z, prev, y_pre, dt_bias, a_log, d_skip_x, norm_g, e, dout, dnew):
    c = _ssd_common(xc, dtraw, dt_bias, a_log, e)
    xs = c["xs"]
    sig = _sigmoid(z)
    silu_z = z * sig
    gated = y_pre * silu_z
    d_gated, normed = [], []
    for g in range(SSM_GROUPS):
        lanes = slice(g * GROUP_W, (g + 1) * GROUP_W)
        yg = gated[:, lanes]
        r = lax.rsqrt(jnp.mean(yg * yg, axis=-1, keepdims=True) + RMS_EPS)
        n = yg * r
        gh = dout[:, lanes] * norm_g[:, lanes]
        d_gated.append(r * (gh - n * jnp.mean(gh * n, axis=-1, keepdims=True)))
        normed.append(n)
    d_gated = jnp.concatenate(d_gated, axis=1)
    dnorm_g = jnp.sum(dout * jnp.concatenate(normed, axis=1), axis=0, keepdims=True)
    dy = d_gated * silu_z
    dz = d_gated * y_pre * (sig * (1.0 + z * (1.0 - sig)))
    dxs = dy * d_skip_x
    dd_skip = _reduce_heads_of_column_sums(dy * xs, e)

    lane = lax.broadcasted_iota(jnp.int32, (CHUNK, 128), 1)
    sub = lax.broadcasted_iota(jnp.int32, (8, 128), 0)
    dcs_neg = jnp.zeros((CHUNK, 128), F32)
    row_slabs = []
    dxdt, dx_st, d_decay_in_x, dprev, d_chunk_decay_x, db_all, dc_all = [], [], [], [], [], [], []
    for g in range(SSM_GROUPS):
        lanes = slice(g * GROUP_W, (g + 1) * GROUP_W)
        bg = xc[:, SSM_INNER + g * SSM_STATE:SSM_INNER + (g + 1) * SSM_STATE].astype(BF16)
        cg_f = xc[:, SSM_INNER + (SSM_GROUPS + g) * SSM_STATE:SSM_INNER + (SSM_GROUPS + g + 1) * SSM_STATE]
        cg = cg_f.astype(BF16)
        pg = prev[g * SSM_STATE:(g + 1) * SSM_STATE, :]
        dng = dnew[g * SSM_STATE:(g + 1) * SSM_STATE, :]
        dy_g = dy[:, lanes]
        cb_t = _mxu(bg, cg, NT_DIMS)
        t1 = (dy_g * c["decay_in_x"][:, lanes]).astype(BF16)
        d_decay_in_x.append(dy_g * _mxu(cg, pg))
        dc = _mxu(t1, pg, NT_DIMS)
        dprev.append(_mxu(cg_f.T, t1) + dng * c["chunk_decay_x"][:, lanes])
        d_chunk_decay_x.append(dng * pg)
        db = _mxu(c["x_st"][:, lanes], dng, NT_DIMS)
        dx_st.append(_mxu(bg, dng))
        dcb_t = jnp.zeros((CHUNK, CHUNK), F32)
        rows = []
        for j in range(SSM_RPG // 2):
            h0 = g * SSM_RPG + 2 * j
            blk = slice(128 * (h0 // 2), 128 * (h0 // 2 + 1))
            xp = c["xdt"][:, blk]
            dyp = dy[:, blk].astype(BF16)
            pair_dx = []
            for k, xk in enumerate((jnp.where(c["low"], xp, 0.0), jnp.where(c["low"], 0.0, xp))):
                dec_t = _pair_decay_t(c, h0 + k)
                pair_dx.append(_mxu(cb_t * dec_t, dyp))
                dml_t = _mxu(xk, dyp, NT_DIMS) * dec_t
                dcb_t = dcb_t + dml_t
                dseg_t = dml_t * cb_t
                dcs_neg = dcs_neg + jnp.where(lane == h0 + k, jnp.sum(dseg_t, axis=-1, keepdims=True), 0.0)
                rows.append(jnp.sum(dseg_t, axis=0, keepdims=True))
            dxdt.append(jnp.where(c["low"], pair_dx[0], pair_dx[1]))
        slab = jnp.zeros((8, 128), F32)
        for r in range(SSM_RPG):
            slab = slab + jnp.where(sub == r, rows[r], 0.0)
        row_slabs.append(slab)
        dc_all.append(dc + _mxu(dcb_t.T, bg))
        db_all.append(db + _mxu(dcb_t, cg))
    dxdt = jnp.concatenate(dxdt, axis=1)
    dx_st = jnp.concatenate(dx_st, axis=1)
    by_head = jnp.concatenate(row_slabs + [jnp.zeros((CHUNK - SSM_HEADS, 128), F32)], axis=0)
    dcs = by_head.T - dcs_neg
    dxs = dxs + dxdt * c["dt_x"] + dx_st * c["w_st_x"]
    ddt = _reduce_heads(dxdt * xs, e, terms=1)
    dw_st = _reduce_heads(dx_st * xs, e, terms=1)
    dcs = dcs + _reduce_heads(jnp.concatenate(d_decay_in_x, axis=1), e, terms=1) * c["decay_in"]
    ddt = ddt + dw_st * c["decay_st"]
    d_log_st = dw_st * c["dt"] * c["decay_st"]
    dcs = dcs - d_log_st
    d_chunk_decay = _reduce_heads_of_column_sums(jnp.concatenate(d_chunk_decay_x, axis=1), e)
    dcs_last = jnp.sum(d_log_st, axis=0, keepdims=True) + d_chunk_decay * c["chunk_decay"]
    row = lax.broadcasted_iota(jnp.int32, (CHUNK, 128), 0)
    dcs = dcs + jnp.where(row == CHUNK - 1, dcs_last, 0.0)
    triu = jnp.where(c["upper"], 1.0, 0.0).astype(F32)
    dda = jnp.dot(triu, dcs, precision=HIGHEST, preferred_element_type=F32)
    ddt = ddt + dda * c["a"]
    da_log = jnp.sum(dda * c["dt"], axis=0, keepdims=True) * c["a"]
    dpre = ddt * _sigmoid(c["pre"])
    dxc = jnp.concatenate([dxs] + db_all + dc_all, axis=1)
    return (dxc, dpre, dz, jnp.concatenate(dprev, axis=0), jnp.sum(dpre, axis=0, keepdims=True), da_log, dd_skip,
            dnorm_g)


def _conv_block(x, w, b):
    rows = lax.broadcasted_iota(jnp.int32, x.shape, 0)
    acc = x * w[SSM_CONV - 1:SSM_CONV, :] + b
    for k in range(SSM_CONV - 1):
        shift = SSM_CONV - 1 - k
        acc = acc + _shift_rows(x, rows, shift) * w[k:k + 1, :]
    return (_silu(acc),)


@functools.partial(jax.custom_vjp, nondiff_argnums=(2,))
def _shift_rows(x, rows, shift):
    return jnp.where(rows >= shift, pltpu.roll(x, shift, 0), 0.0)


def _shift_rows_fwd(x, rows, shift):
    return _shift_rows(x, rows, shift), rows


def _shift_rows_bwd(shift, rows, g):
    n = g.shape[0]
    return jnp.where(rows < n - shift, pltpu.roll(g, n - shift, 0), 0.0), None


_shift_rows.defvjp(_shift_rows_fwd, _shift_rows_bwd)


def _merge_block(gates, br_a, br_b):
    return (_sigmoid(gates[:, :D_MODEL]) * br_a + _sigmoid(gates[:, D_MODEL:]) * br_b,)


def _lnres_block(x, y, g, b):
    return (_layer_norm(ALPHA * x + y, g, b),)


def _memln_block(x, g, b):
    return (_layer_norm(x, g, b),)


def _attn_block(q, kv):
    outs = []
    for h in range(X_HEADS):
        qh = q[:, h * X_HEADDIM:(h + 1) * X_HEADDIM].astype(BF16)
        kh = kv[:, h * X_HEADDIM:(h + 1) * X_HEADDIM].astype(BF16)
        vh = kv[:, D_MODEL + h * X_HEADDIM:D_MODEL + (h + 1) * X_HEADDIM].astype(BF16)
        s = lax.dot_general(qh, kh, (((1,), (1,)), ((), ())), preferred_element_type=F32) * (X_HEADDIM ** -0.5)
        s = s - lax.stop_gradient(jnp.max(s, axis=-1, keepdims=True))
        e = jnp.exp(s)
        p = e / jnp.sum(e, axis=-1, keepdims=True)
        outs.append(jnp.dot(p.astype(BF16), vh, preferred_element_type=F32))
    return (jnp.concatenate(outs, axis=1),)


def _swiglu_block(gu):
    return (_silu(gu[:, :FFN_HIDDEN]) * gu[:, FFN_HIDDEN:],)


class _Comm:
    def __init__(self):
        self.gathers = []
        self.scatters = []

    @staticmethod
    def _rows(ref, rows):
        return ref if rows is None else ref.at[pl.ds(rows[0], rows[1])]

    def operands(self):
        ins = [a for a, _, _ in self.gathers] + [a for a, _ in self.scatters]
        shapes = []
        for a, idx, rows in self.gathers:
            blk = a.shape if idx is None else a.shape[1:]
            shapes.append(jax.ShapeDtypeStruct((N_DEV, blk[0] if rows is None else rows[1]) + tuple(blk[1:]), a.dtype))
        for a, rows in self.scatters:
            shapes.append(jax.ShapeDtypeStruct((N_DEV, a.shape[1] if rows is None else rows[1]) + tuple(a.shape[2:]),
                                               a.dtype))
        scratch = []
        for n in (len(self.gathers), len(self.scatters)):
            if n:
                scratch += [pltpu.SemaphoreType.DMA((7 * n,)), pltpu.SemaphoreType.DMA((7 * n,)),
                            pltpu.SemaphoreType.DMA((n,))]
        return ins, shapes, scratch

    def _split(self, in_refs, out_refs, sems):
        ng = len(self.gathers)
        g_sems = sems[:3] if ng else None
        s_sems = sems[3:] if ng else sems
        return in_refs[:ng], in_refs[ng:], out_refs[:ng], out_refs[ng:], g_sems, s_sems

    def _gather_copies(self, i, src_ref, out_ref, sems):
        send_sems, recv_sems, local_sems = sems
        x, y, c = lax.axis_index("x"), lax.axis_index("y"), lax.axis_index("c")
        me, sibling = (x, y, c), (x, y, 1 - c)
        chips = [(1 - x, y), (x, 1 - y), (1 - x, 1 - y)]
        _, idx, rows = self.gathers[i]
        src = self._rows(src_ref if idx is None else src_ref.at[idx], rows)

        def slot(px, py, pc):
            return out_ref.at[4 * px + 2 * py + pc]

        def copy(k, blk, to, from_src=False):
            return pltpu.make_async_remote_copy(
                src_ref=src if from_src else slot(*blk), dst_ref=slot(*blk), send_sem=send_sems.at[7 * i + k],
                recv_sem=recv_sems.at[7 * i + k], device_id=to, device_id_type=pl.DeviceIdType.MESH)

        mine = pltpu.make_async_copy(src, slot(*me), local_sems.at[i])
        first = [copy(0, me, sibling, True)] + [copy(1 + j, me, (*chip, c), True) for j, chip in enumerate(chips)]
        passed = [copy(4 + j, (*chip, c), sibling) for j, chip in enumerate(chips)]
        arrivals = [copy(1 + j, (*chip, c), me) for j, chip in enumerate(chips)]
        from_sibling = [copy(0, sibling, me)] + [copy(4 + j, (*chip, 1 - c), me) for j, chip in enumerate(chips)]
        return mine, first, passed, arrivals, from_sibling

    def _scatter_copies(self, i, src_ref, out_ref, sems):
        send_sems, recv_sems, local_sems = sems
        x, y, c = lax.axis_index("x"), lax.axis_index("y"), lax.axis_index("c")
        me = 4 * x + 2 * y + c
        rows = self.scatters[i][1]
        mine = pltpu.make_async_copy(self._rows(src_ref.at[me], rows), out_ref.at[me], local_sems.at[i])
        copies = []
        for k in range(1, N_DEV):
            px = 1 - x if k & 4 else x
            py = 1 - y if k & 2 else y
            pc = 1 - c if k & 1 else c
            copies.append(pltpu.make_async_remote_copy(
                src_ref=self._rows(src_ref.at[4 * px + 2 * py + pc], rows), dst_ref=out_ref.at[me],
                send_sem=send_sems.at[7 * i + k - 1], recv_sem=recv_sems.at[7 * i + k - 1], device_id=(px, py, pc),
                device_id_type=pl.DeviceIdType.MESH))
        return mine, copies

    def start(self, in_refs, out_refs, sems):
        g_in, s_in, g_out, s_out, g_sems, s_sems = self._split(in_refs, out_refs, sems)
        for i in range(len(self.gathers)):
            mine, first, _, _, _ = self._gather_copies(i, g_in[i], g_out[i], g_sems)
            mine.start()
            for cp in first:
                cp.start()
        for i in range(len(self.scatters)):
            mine, copies = self._scatter_copies(i, s_in[i], s_out[i], s_sems)
            mine.start()
            for cp in copies:
                cp.start()

    def finish(self, in_refs, out_refs, sems):
        g_in, s_in, g_out, s_out, g_sems, s_sems = self._split(in_refs, out_refs, sems)
        parts = [self._gather_copies(i, g_in[i], g_out[i], g_sems) for i in range(len(self.gathers))]
        for j in range(3):
            for _, _, passed, arrivals, _ in parts:
                arrivals[j].wait_recv()
                passed[j].start()
        for mine, first, passed, _, from_sibling in parts:
            for cp in from_sibling:
                cp.wait_recv()
            for cp in first + passed:
                cp.wait_send()
            mine.wait()
        for i in range(len(self.scatters)):
            mine, copies = self._scatter_copies(i, s_in[i], s_out[i], s_sems)
            for cp in copies:
                cp.wait_recv()
            for cp in copies:
                cp.wait_send()
            mine.wait()


def _params(grid):
    return pltpu.CompilerParams(dimension_semantics=("arbitrary",) * len(grid), vmem_limit_bytes=VMEM_LIMIT)


def _call(name, body, *, grid, ins, in_specs, out_shape, out_specs, scratch=(), comm=None, aliases=None):
    n_in, n_out, n_scr = len(ins), len(out_shape), len(scratch)
    aliases = aliases or {}
    if comm is None:
        outs = pl.pallas_call(body, grid=grid, in_specs=list(in_specs), out_specs=list(out_specs),
                              out_shape=list(out_shape), scratch_shapes=list(scratch), name=name,
                              input_output_aliases=aliases, compiler_params=_params(grid))(*ins)
        return list(outs), []
    c_ins, c_shapes, c_scratch = comm.operands()
    nci, nco = len(c_ins), len(c_shapes)
    anywhere = pl.BlockSpec(memory_space=pl.ANY)

    def carrier(*refs):
        main_in, comm_in = refs[:n_in], refs[n_in:n_in + nci]
        o0 = n_in + nci
        main_out, comm_out = refs[o0:o0 + n_out], refs[o0 + n_out:o0 + n_out + nco]
        s0 = o0 + n_out + nco
        main_scr, comm_scr = refs[s0:s0 + n_scr], refs[s0 + n_scr:]
        first = pl.program_id(0) == 0
        last = pl.program_id(0) == grid[0] - 1
        for ax in range(1, len(grid)):
            first = first & (pl.program_id(ax) == 0)
            last = last & (pl.program_id(ax) == grid[ax] - 1)

        @pl.when(first)
        def _():
            comm.start(comm_in, comm_out, comm_scr)

        body(*main_in, *main_out, *main_scr)

        @pl.when(last)
        def _():
            comm.finish(comm_in, comm_out, comm_scr)

    outs = pl.pallas_call(carrier, grid=grid, in_specs=list(in_specs) + [anywhere] * nci,
                          out_specs=list(out_specs) + [anywhere] * nco, out_shape=list(out_shape) + c_shapes,
                          scratch_shapes=list(scratch) + c_scratch, name=name, input_output_aliases=aliases,
                          compiler_params=_params(grid))(*ins, *c_ins)
    return list(outs[:n_out]), list(outs[n_out:])


def _comm_only(name, comm):
    c_ins, c_shapes, c_scratch = comm.operands()
    nci, nco = len(c_ins), len(c_shapes)
    anywhere = pl.BlockSpec(memory_space=pl.ANY)

    def body(*refs):
        comm.start(refs[:nci], refs[nci:nci + nco], refs[nci + nco:])
        comm.finish(refs[:nci], refs[nci:nci + nco], refs[nci + nco:])

    return list(pl.pallas_call(body, in_specs=[anywhere] * nci, out_specs=[anywhere] * nco, out_shape=c_shapes,
                               scratch_shapes=c_scratch, name=name)(*c_ins))


def _stage_fwd(name, f, grid, ins, in_specs, out_shapes, out_specs, comm=None):
    n_in = len(ins)

    def body(*refs):
        res = f(*[r[...].astype(F32) for r in refs[:n_in]])
        for o_ref, val in zip(refs[n_in:], res):
            o_ref[...] = val.astype(o_ref.dtype)

    return _call(name, body, grid=grid, ins=ins, in_specs=in_specs, out_shape=out_shapes, out_specs=out_specs, comm=comm)


def _stage_bwd(name, f, grid, ins, in_specs, cts, ct_specs, grads, comm=None, ct_product=None):
    n_in = len(ins)
    flat_cts = [c for group in cts for c in group]
    flat_ct_specs = [s for group, spec in zip(cts, ct_specs) for s in (spec,) * len(group)]
    if ct_product is not None:
        assert not cts
        flat_cts = [ct_product[0], ct_product[2]]
        flat_ct_specs = [ct_product[1], pl.BlockSpec(ct_product[2].shape, lambda *_: (0, 0))]
    n_ct = len(flat_cts)
    diff = [g[0] for g in grads]
    buffers = [(k, g[4]) for k, g in enumerate(grads) if len(g) > 4]
    n_buf = len(buffers)

    def body(*refs):
        vals = [r[...].astype(F32) for r in refs[:n_in]]
        ct_refs = refs[n_in:n_in + n_ct]
        g_refs = refs[n_in + n_ct + n_buf:]
        ct_vals, pos = [], 0
        if ct_product is not None:
            ct_vals.append(lax.dot_general(ct_refs[0][...].astype(BF16), ct_refs[1][...].astype(BF16), NT_DIMS,
                                           preferred_element_type=F32))
        for group in cts:
            acc = ct_refs[pos][...].astype(F32)
            for j in range(1, len(group)):
                acc = acc + ct_refs[pos + j][...].astype(F32)
            ct_vals.append(acc)
            pos += len(group)

        def g_fn(*dvals):
            full = list(vals)
            for i, dv in zip(diff, dvals):
                full[i] = dv
            return f(*full)

        _, vjp = jax.vjp(g_fn, *[vals[i] for i in diff])
        gvals = vjp(tuple(ct_vals))
        for gspec, g_ref, gval in zip(grads, g_refs, gvals):
            acc_axes = gspec[1]
            if not acc_axes:
                g_ref[...] = gval.astype(g_ref.dtype)
            else:
                first = pl.program_id(acc_axes[0]) == 0
                for ax in acc_axes[1:]:
                    first = first & (pl.program_id(ax) == 0)

                @pl.when(first)
                def _():
                    g_ref[...] = jnp.zeros_like(g_ref)

                g_ref[...] += gval.astype(g_ref.dtype)

    out_shapes, out_specs = [], []
    for gspec in grads:
        shape, spec = gspec[3] if len(gspec) > 3 else (ins[gspec[0]].shape, in_specs[gspec[0]])
        out_shapes.append(jax.ShapeDtypeStruct(shape, gspec[2]))
        out_specs.append(spec)
    anywhere = pl.BlockSpec(memory_space=pl.ANY)
    return _call(name, body, grid=grid, ins=list(ins) + flat_cts + [b for _, b in buffers],
                 in_specs=list(in_specs) + flat_ct_specs + [anywhere] * n_buf, out_shape=out_shapes, out_specs=out_specs,
                 comm=comm, aliases={n_in + n_ct + j: k for j, (k, _) in enumerate(buffers)})


def _pick_tile(n, candidates):
    for c in candidates:
        if n % c == 0:
            return c
    return n


def _matmul(name, a, b, *, ta=False, tb=False, add=None, extra=None, out_dtype=F32, comm=None):
    if ta:
        k_dim, m = a.shape
    else:
        m, k_dim = a.shape
    n = b.shape[0] if tb else b.shape[1]
    assert (b.shape[1] if tb else b.shape[0]) == k_dim and not (ta and tb)
    tm = _pick_tile(m, (1024, 1408, 512, 256, 128))
    tn = _pick_tile(n, (1024, 1408, 512, 256, 128))
    if ta:
        tk = _pick_tile(k_dim, (1024, 512, 256, 128))
    elif k_dim <= 2816:
        tk = k_dim
    else:
        tk = _pick_tile(k_dim, (1408, 1024, 512, 256, 128))
    nk = k_dim // tk
    grid = (m // tm, n // tn, nk)
    a_spec = pl.BlockSpec((tk, tm), lambda i, j, k: (k, i)) if ta else pl.BlockSpec((tm, tk), lambda i, j, k: (i, k))
    b_spec = pl.BlockSpec((tn, tk), lambda i, j, k: (j, k)) if tb else pl.BlockSpec((tk, tn), lambda i, j, k: (k, j))
    o_spec = pl.BlockSpec((tm, tn), lambda i, j, k: (i, j))
    dims = (((0 if ta else 1,), (1 if tb else 0,)), ((), ()))
    has_add = add is not None
    has_extra = extra is not None

    def body(*refs):
        a_ref, b_ref = refs[0], refs[1]
        add_ref = refs[2] if has_add else None
        o_ref, acc_ref = refs[-2], refs[-1]
        k = pl.program_id(2)
        part = lax.dot_general(a_ref[...].astype(BF16), b_ref[...].astype(BF16), dims, preferred_element_type=F32)

        def finish(res):
            if has_add:
                res = res + add_ref[...].astype(F32)
            if has_extra:
                a2_ref, b2_ref = refs[2 + has_add], refs[3 + has_add]
                res = res + lax.dot_general(a2_ref[...].astype(BF16), b2_ref[...].astype(BF16), NT_DIMS,
                                            preferred_element_type=F32)
            o_ref[...] = res.astype(o_ref.dtype)

        if nk == 1:
            finish(part)
        else:
            @pl.when(k == 0)
            def _():
                acc_ref[...] = part

            @pl.when((k > 0) & (k < nk - 1))
            def _():
                acc_ref[...] += part

            @pl.when(k == nk - 1)
            def _():
                finish(acc_ref[...] + part)

    ins = [a, b] + ([add] if has_add else [])
    in_specs = [a_spec, b_spec] + ([o_spec] if has_add else [])
    if has_extra:
        k2 = extra[0].shape[1]
        ins += list(extra)
        in_specs += [pl.BlockSpec((tm, k2), lambda i, j, k: (i, 0)), pl.BlockSpec((tn, k2), lambda i, j, k: (j, 0))]
    acc_shape = (tm, tn) if nk > 1 else (8, 128)
    outs, comm_outs = _call(name, body, grid=grid, ins=ins, in_specs=in_specs,
                            out_shape=[jax.ShapeDtypeStruct((m, n), out_dtype)], out_specs=[o_spec],
                            scratch=[pltpu.VMEM(acc_shape, F32)], comm=comm)
    return outs[0], comm_outs


def _matmul_lnres(name, a_fn, a_ins, a_specs, tm, b, x, g, beta, comm=None):
    m = x.shape[0]
    k_dim, n = b.shape
    n_a = len(a_ins)
    row = lambda w: pl.BlockSpec((tm, w), lambda i: (i, 0))
    whole = lambda shape: pl.BlockSpec(shape, lambda i: (0, 0))

    def body(*refs):
        b_ref, x_ref, g_ref, beta_ref = refs[n_a:n_a + 4]
        y_ref, h_ref, hb_ref = refs[-3:]
        if a_fn is None:
            a = refs[0][...].astype(BF16)
        else:
            (a,) = a_fn(*[r[...].astype(F32) for r in refs[:n_a]])
            a = a.astype(BF16)
            refs[n_a + 4][...] = a
        y = jnp.dot(a, b_ref[...].astype(BF16), preferred_element_type=F32).astype(y_ref.dtype)
        y_ref[...] = y
        (h,) = _lnres_block(x_ref[...], y.astype(F32), g_ref[...], beta_ref[...])
        h_ref[...] = h
        hb_ref[...] = h.astype(hb_ref.dtype)

    sds = jax.ShapeDtypeStruct
    a_out = ([sds((m, k_dim), BF16)], [row(k_dim)]) if a_fn is not None else ([], [])
    return _call(name, body, grid=(m // tm,), ins=list(a_ins) + [b, x, g, beta],
                 in_specs=list(a_specs) + [whole((k_dim, n)), row(n), whole((1, n)), whole((1, n))],
                 out_shape=a_out[0] + [sds((m, n), BF16), sds((m, n), F32), sds((m, n), BF16)],
                 out_specs=a_out[1] + [row(n), row(n), row(n)], comm=comm)


SSD_STATE = (SSM_GROUPS * SSM_STATE, SSM_RPG * SSM_HEADDIM)


def _ssd_fwd(xc, dt_raw, proj, dt_bias, a_log, d_skip, norm_g, nb, nc, comm=None):
    t = xc.shape[0]
    row = lambda b, c: (b * nc + c, 0)
    par = lambda shape: pl.BlockSpec(shape, lambda b, c: (0, 0))

    def body(xc_ref, dt_ref, z_ref, dtb_ref, al_ref, ds_ref, ng_ref, e_ref, y_ref, ypre_ref, prev_ref, st_ref):
        @pl.when(pl.program_id(1) == 0)
        def _():
            st_ref[...] = jnp.zeros_like(st_ref)

        prev = st_ref[...]
        prev_ref[0, 0] = prev
        y, y_pre, new_state = _ssd_forward(xc_ref[...], dt_ref[...], z_ref[...].astype(F32), prev, dtb_ref[...],
                                           al_ref[...], ds_ref[...], ng_ref[...], e_ref[...])
        y_ref[...] = y.astype(y_ref.dtype)
        ypre_ref[...] = y_pre
        st_ref[...] = new_state

    return _call(
        "ssd_fwd", body, grid=(nb, nc), ins=[xc, dt_raw, proj, dt_bias, a_log, d_skip, norm_g, _head_expander()],
        in_specs=[pl.BlockSpec((CHUNK, SSM_CONV_DIM), row), pl.BlockSpec((CHUNK, 128), row),
                  pl.BlockSpec((CHUNK, SSM_INNER), lambda b, c: (b * nc + c, 1)),
                  par((1, 128)), par((1, 128)), par((1, SSM_INNER)), par((1, SSM_INNER)), par((128, SSM_INNER))],
        out_specs=[pl.BlockSpec((CHUNK, SSM_INNER), row), pl.BlockSpec((CHUNK, SSM_INNER), row),
                   pl.BlockSpec((1, 1) + SSD_STATE, lambda b, c: (b, c, 0, 0))],
        out_shape=[jax.ShapeDtypeStruct((t, SSM_INNER), BF16), jax.ShapeDtypeStruct((t, SSM_INNER), F32),
                   jax.ShapeDtypeStruct((nb, nc) + SSD_STATE, F32)],
        scratch=[pltpu.VMEM(SSD_STATE, F32)], comm=comm)


def _ssd_bwd(xc, dt_raw, proj, prevs, y_pre, dt_bias, a_log, d_skip, norm_g, dy, dproj, nb, nc, comm=None):
    t = xc.shape[0]
    row = lambda b, c: (b * nc + (nc - 1 - c), 0)
    par = lambda shape: pl.BlockSpec(shape, lambda b, c: (0, 0))
    z_spec = pl.BlockSpec((CHUNK, SSM_INNER), lambda b, c: (b * nc + (nc - 1 - c), 1))

    def body(xc_ref, dt_ref, z_ref, prev_ref, ypre_ref, dtb_ref, al_ref, ds_ref, ng_ref, e_ref, dy_ref, _,
             dxc_ref, ddt_ref, dz_ref, ddtb_ref, dal_ref, dds_ref, dng_ref, dst_ref):
        @pl.when(pl.program_id(1) == 0)
        def _():
            dst_ref[...] = jnp.zeros_like(dst_ref)

        @pl.when((pl.program_id(0) == 0) & (pl.program_id(1) == 0))
        def _():
            ddtb_ref[...] = jnp.zeros_like(ddtb_ref)
            dal_ref[...] = jnp.zeros_like(dal_ref)
            dds_ref[...] = jnp.zeros_like(dds_ref)
            dng_ref[...] = jnp.zeros_like(dng_ref)

        dxc, ddt, dz, dprev, ddtb, dal, dds, dng = _ssd_backward(
            xc_ref[...], dt_ref[...], z_ref[...].astype(F32), prev_ref[0, 0], ypre_ref[...], dtb_ref[...], al_ref[...],
            ds_ref[...], ng_ref[...], e_ref[...], dy_ref[...].astype(F32), dst_ref[...])
        dxc_ref[...] = dxc
        ddt_ref[...] = ddt.astype(ddt_ref.dtype)
        dz_ref[...] = dz.astype(dz_ref.dtype)
        dst_ref[...] = dprev
        ddtb_ref[...] += ddtb
        dal_ref[...] += dal
        dds_ref[...] += dds
        dng_ref[...] += dng

    return _call(
        "ssd_bwd", body, grid=(nb, nc),
        ins=[xc, dt_raw, proj, prevs, y_pre, dt_bias, a_log, d_skip, norm_g, _head_expander(), dy, dproj],
        in_specs=[pl.BlockSpec((CHUNK, SSM_CONV_DIM), row), pl.BlockSpec((CHUNK, DT_LANES), row), z_spec,
                  pl.BlockSpec((1, 1) + SSD_STATE, lambda b, c: (b, nc - 1 - c, 0, 0)),
                  pl.BlockSpec((CHUNK, SSM_INNER), row),
                  par((1, 128)), par((1, 128)), par((1, SSM_INNER)), par((1, SSM_INNER)), par((128, SSM_INNER)),
                  pl.BlockSpec((CHUNK, SSM_INNER), row), pl.BlockSpec(memory_space=pl.ANY)],
        out_specs=[pl.BlockSpec((CHUNK, SSM_CONV_DIM), row), pl.BlockSpec((CHUNK, DT_LANES), row), z_spec,
                   par((1, 128)), par((1, 128)), par((1, 128)), par((1, SSM_INNER))],
        out_shape=[jax.ShapeDtypeStruct((t, SSM_CONV_DIM), F32), jax.ShapeDtypeStruct((t, DT_LANES), BF16),
                   jax.ShapeDtypeStruct(dproj.shape, dproj.dtype), jax.ShapeDtypeStruct((1, 128), F32),
                   jax.ShapeDtypeStruct((1, 128), F32), jax.ShapeDtypeStruct((1, 128), F32),
                   jax.ShapeDtypeStruct((1, SSM_INNER), F32)],
        scratch=[pltpu.VMEM(SSD_STATE, F32)], comm=comm, aliases={11: 2})


def _loss_head(y, target):
    t, d = y.shape
    tm = _pick_tile(t, (256,))
    blk = pl.BlockSpec((tm, d), lambda i: (i, 0))

    def body(y_ref, t_ref, loss_ref, dy_ref):
        err = y_ref[...] - t_ref[...]
        dy_ref[...] = err * (1.0 / d)

        @pl.when(pl.program_id(0) == 0)
        def _():
            loss_ref[...] = jnp.zeros_like(loss_ref)

        loss_ref[...] += 0.5 * jnp.sum(jnp.mean(err * err, axis=-1, keepdims=True), axis=0, keepdims=True)

    return _call("loss_head", body, grid=(t // tm,), ins=[y, target], in_specs=[blk, blk],
                 out_specs=[pl.BlockSpec((1, 1), lambda i: (0, 0)), blk],
                 out_shape=[jax.ShapeDtypeStruct((1, 1), F32), jax.ShapeDtypeStruct((t, d), F32)])[0]


def _adamw_math(g, w, m, v):
    m_new = ADAM_B1 * m + (1.0 - ADAM_B1) * g
    v_new = ADAM_B2 * v + (1.0 - ADAM_B2) * jnp.square(g)
    m_hat = m_new / (1.0 - ADAM_B1 ** ADAM_STEP)
    v_hat = v_new / (1.0 - ADAM_B2 ** ADAM_STEP)
    delta = -ADAM_LR * (m_hat / (jnp.sqrt(v_hat) + ADAM_EPS) + ADAM_WD * w)
    return delta, m_new, v_new


def _adamw_sharded(name, parts, w, m, v, comm=None):
    _, a, b = w.shape
    tr = _pick_tile(a, (128,))
    nt = a // tr
    part_specs = [pl.BlockSpec((N_DEV, tr, b),
                               (lambda l, i, _k=k: (0, jnp.where(l == _k, i, jnp.where(l > _k, nt - 1, 0)), 0)))
                  for k in range(DEPTH)]
    blk = pl.BlockSpec((1, tr, b), lambda l, i: (l, i, 0))

    def body(*refs):
        p_refs = refs[:DEPTH]
        w_ref, m_ref, v_ref, g_out, d_out, m_out, v_out = refs[DEPTH:]
        for k in range(DEPTH):
            @pl.when(pl.program_id(0) == k)
            def _(p_ref=p_refs[k]):
                g = p_ref[0].astype(F32)
                for p in range(1, N_DEV):
                    g = g + p_ref[p].astype(F32)
                delta, m_new, v_new = _adamw_math(g, w_ref[0], m_ref[0], v_ref[0])
                g_out[0] = g
                d_out[0] = delta
                m_out[0] = m_new
                v_out[0] = v_new

    return _call(name, body, grid=(DEPTH, nt), ins=list(parts) + [w, m, v], in_specs=part_specs + [blk, blk, blk],
                 out_specs=[blk] * 4, out_shape=[jax.ShapeDtypeStruct(w.shape, F32)] * 4, comm=comm)


def _adamw_small(name, g, w, m, v):
    full = pl.BlockSpec(w.shape, lambda i: (0, 0))

    def body(g_ref, w_ref, m_ref, v_ref, d_out, m_out, v_out):
        delta, m_new, v_new = _adamw_math(g_ref[...], w_ref[...], m_ref[...], v_ref[...])
        d_out[...] = delta
        m_out[...] = m_new
        v_out[...] = v_new

    return _call(name, body, grid=(1,), ins=[g, w, m, v], in_specs=[full] * 4, out_specs=[full] * 3,
                 out_shape=[jax.ShapeDtypeStruct(w.shape, F32)] * 3)[0]


def _sum_parts(name, parts):
    n_parts, rows, cols = parts.shape
    tr = _pick_tile(rows, (512, 256, 128, 64, 32, 16, 8))

    def body(p_ref, o_ref):
        acc = p_ref[0]
        for p in range(1, n_parts):
            acc = acc + p_ref[p]
        o_ref[...] = acc

    return _call(name, body, grid=(rows // tr,), ins=[parts],
                 in_specs=[pl.BlockSpec((n_parts, tr, cols), lambda i: (0, i, 0))],
                 out_specs=[pl.BlockSpec((tr, cols), lambda i: (i, 0))],
                 out_shape=[jax.ShapeDtypeStruct((rows, cols), parts.dtype)])[0][0]


W_IN_SHARD = IN_COLS // N_DEV


def _pack_w_in(gathered):
    r = gathered.shape[1]
    tr = _pick_tile(r, (128,))

    def body(g_ref, main_ref, dt_ref):
        w = jnp.concatenate([g_ref[j].astype(F32) for j in range(N_DEV)], axis=1)
        main_ref[...] = jnp.concatenate([w[:, :XBC_COL], w[:, GA_COL:], w[:, XBC_COL:DT_COL]],
                                        axis=1).astype(main_ref.dtype)
        dt_ref[...] = jnp.concatenate([w[:, DT_COL:GA_COL], jnp.zeros((tr, DT_LANES - SSM_HEADS), F32)],
                                      axis=1).astype(dt_ref.dtype)

    return _call("pack_w_in", body, grid=(r // tr,), ins=[gathered],
                 in_specs=[pl.BlockSpec((N_DEV, tr, W_IN_SHARD), lambda i: (0, i, 0))],
                 out_specs=[pl.BlockSpec((tr, P_COLS), lambda i: (i, 0)), pl.BlockSpec((tr, DT_LANES), lambda i: (i, 0))],
                 out_shape=[jax.ShapeDtypeStruct((r, P_COLS), gathered.dtype),
                            jax.ShapeDtypeStruct((r, DT_LANES), gathered.dtype)])[0]


def _unpack_w_in(main, dt):
    r = main.shape[0]
    tr = _pick_tile(r, (128,))

    def body(main_ref, dt_ref, o_ref):
        main = main_ref[...].astype(F32)
        w = jnp.concatenate([main[:, :P_GATE], main[:, P_XBC:], dt_ref[...].astype(F32)[:, :SSM_HEADS],
                             main[:, P_GATE:P_XBC]], axis=1)
        for j in range(N_DEV):
            o_ref[j] = w[:, j * W_IN_SHARD:(j + 1) * W_IN_SHARD].astype(o_ref.dtype)

    return _call("unpack_w_in", body, grid=(r // tr,), ins=[main, dt],
                 in_specs=[pl.BlockSpec((tr, P_COLS), lambda i: (i, 0)), pl.BlockSpec((tr, DT_LANES), lambda i: (i, 0))],
                 out_specs=[pl.BlockSpec((N_DEV, tr, W_IN_SHARD), lambda i: (0, i, 0))],
                 out_shape=[jax.ShapeDtypeStruct((N_DEV, r, W_IN_SHARD), main.dtype)])[0][0]


def _pad_heads(v):
    return jnp.pad(v, (0, 128 - SSM_HEADS)).reshape(1, 128)


def _run_step(x, mem, target, small, ex):
    nb, s, d = x.shape
    t = nb * s
    nc = s // CHUNK
    rows = _pick_tile(t, (256,))
    rows_wide = _pick_tile(t, (512, 256))
    tq = _pick_tile(s, (512, 256))
    vec = lambda a: a.reshape(1, -1)
    full1 = lambda shape: pl.BlockSpec(shape, lambda i: (0,) * len(shape))
    row1 = lambda tm, w: pl.BlockSpec((tm, w), lambda i: (i, 0))
    sds = jax.ShapeDtypeStruct

    def mm(call, l, a, b, **kw):
        comm = ex.before(call, l)
        out, comm_outs = _matmul(call, a, b, comm=comm, **kw)
        if comm is not None:
            ex.after(call, l, comm_outs)
        return out

    def stage_bwd(call, l, *args, **kw):
        comm = ex.before(call, l)
        outs, comm_outs = _stage_bwd(call, *args, comm=comm, **kw)
        if comm is not None:
            ex.after(call, l, comm_outs)
        return outs

    def stage_fwd(call, l, *args):
        comm = ex.before(call, l)
        outs, comm_outs = _stage_fwd(call, *args, comm=comm)
        if comm is not None:
            ex.after(call, l, comm_outs)
        return outs

    mem_specs = [row1(256, d), full1((1, d)), full1((1, d))]
    mem_ins = [mem.reshape(nb * MEM_LEN, d), vec(small["mem_ln_g"]), vec(small["mem_ln_b"])]
    (mem_n,) = stage_fwd("memln_fwd", 0, _memln_block, (nb * MEM_LEN // 256,), mem_ins, mem_specs,
                          [sds((nb * MEM_LEN, d), BF16)], [row1(256, d)])

    h = x.reshape(t, d)
    h_bf = h.astype(BF16)
    ln_specs = [row1(rows_wide, d), row1(rows_wide, d), full1((1, d)), full1((1, d))]
    saved = []
    for l in range(DEPTH):
        sv = {"h_bf": h_bf}
        w_p, w_dt = ex.weight("w_in", l)
        proj = mm("mm_in", l, h_bf, w_p, out_dtype=BF16)
        dt_raw = mm("mm_dt", l, h_bf, w_dt)
        sv["proj"] = proj
        sgu_ins = [proj, vec(small["sg_ln_g"][l]), vec(small["sg_ln_b"][l]), small["sg_w"][l], small["sg_b"][l].T]
        sgu_specs = [pl.BlockSpec((rows, 2 * d), lambda i: (i, 0)), full1((1, d)), full1((1, d)),
                     full1((SG_GROUPS, CHUNK, CHUNK)), full1((CHUNK, SG_GROUPS))]
        (a_out,) = stage_fwd("sgu_fwd", l, _sgu_block, (t // rows,), sgu_ins, sgu_specs, [sds((t, d), BF16)],
                              [row1(rows, d)])
        sv["sgu"] = (sgu_ins, sgu_specs)
        sv["a_out"] = a_out
        cw = 256
        conv_ins = [proj, small["conv_w"][l], vec(small["conv_b"][l])]
        conv_specs = [pl.BlockSpec((s, cw), lambda j, b: (b, P_XBC // cw + j)),
                      pl.BlockSpec((SSM_CONV, cw), lambda j, b: (0, j)), pl.BlockSpec((1, cw), lambda j, b: (0, j))]
        conv_out_spec = pl.BlockSpec((s, cw), lambda j, b: (b, j))
        (xc,) = stage_fwd("conv_fwd", l, _conv_block, (SSM_CONV_DIM // cw, nb), conv_ins, conv_specs,
                           [sds((t, SSM_CONV_DIM), F32)], [conv_out_spec])
        sv["conv"] = (conv_ins, conv_specs, conv_out_spec)
        ssd_par = [_pad_heads(small["dt_bias"][l]), _pad_heads(small["a_log"][l]),
                   vec(jnp.repeat(small["d_skip"][l], SSM_HEADDIM)), vec(small["ssm_norm_g"][l])]
        comm = ex.before("ssd_fwd", l)
        (y_ssd, y_pre, prevs), comm_outs = _ssd_fwd(xc, dt_raw, proj, *ssd_par, nb, nc, comm=comm)
        if comm is not None:
            ex.after("ssd_fwd", l, comm_outs)
        sv["ssd"] = (xc, dt_raw, prevs, y_pre, ssd_par)
        sv["y_ssd"] = y_ssd
        br_a = mm("mm_sq", l, a_out, ex.weight("p_a", l), out_dtype=BF16)
        br_b = mm("mm_pb", l, y_ssd, ex.weight("p_b", l), out_dtype=BF16)
        merge_ins = [proj, br_a, br_b]
        merge_out_spec = row1(rows_wide, d)
        merge_specs = [pl.BlockSpec((rows_wide, 2 * d), lambda i: (i, P_GATE // (2 * d))), merge_out_spec, merge_out_spec]
        sv["merge"] = (merge_ins, merge_specs, merge_out_spec)
        ln_par = [(vec(small["ln_g"][l, k]), vec(small["ln_b"][l, k])) for k in range(3)]

        def fused(call, a_fn, a_ins, a_specs, tm, w, x_in, par):
            comm = ex.before(call, l)
            outs, comm_outs = _matmul_lnres(call, a_fn, a_ins, a_specs, tm, w, x_in, *par, comm=comm)
            if comm is not None:
                ex.after(call, l, comm_outs)
            return outs

        merged, y1, h1, h1_bf = fused("mm_mix_ln", _merge_block, merge_ins, merge_specs, rows_wide,
                                      ex.weight("w_mix_o", l), h, ln_par[0])
        sv["merged"] = merged
        sv["ln1"] = [h, y1, *ln_par[0]]
        q = mm("mm_sq", l, h1_bf, ex.weight("w_xq", l), out_dtype=BF16)
        kv = mm("mm_kv", l, mem_n, ex.weight("w_xkv", l), out_dtype=BF16)
        attn_ins = [q, kv]
        attn_out_spec = pl.BlockSpec((tq, d), lambda b, i: (b * (s // tq) + i, 0))
        attn_specs = [attn_out_spec, pl.BlockSpec((MEM_LEN, 2 * d), lambda b, i: (b, 0))]
        (o,) = stage_fwd("attn_fwd", l, _attn_block, (nb, s // tq), attn_ins, attn_specs, [sds((t, d), BF16)],
                          [attn_out_spec])
        sv["attn"] = (attn_ins, attn_specs, attn_out_spec)
        sv["o"] = o
        sv["h1_bf"] = h1_bf
        y2, h2, h2_bf = fused("mm_xo_ln", None, [o], [row1(rows_wide, d)], rows_wide, ex.weight("w_xo", l), h1, ln_par[1])
        sv["ln2"] = [h1, y2, *ln_par[1]]
        sv["h2_bf"] = h2_bf
        gu = mm("mm_ffn_in", l, h2_bf, ex.weight("w_ffn_in", l), out_dtype=BF16)
        act, y3, h3, h3_bf = fused("mm_ffn_out_ln", _swiglu_block, [gu], [row1(rows, 2 * FFN_HIDDEN)], rows,
                                   ex.weight("w_ffn_out", l), h2, ln_par[2])
        sv["gu"] = gu
        sv["act"] = act
        sv["ln3"] = [h2, y3, *ln_par[2]]
        h, h_bf = h3, h3_bf
        saved.append(sv)

    loss, dh = _loss_head(h, target.reshape(t, d))

    g_small = {n: [None] * DEPTH for n in SMALL_REP + SMALL_SH if n not in ("mem_ln_g", "mem_ln_b")}
    dmem_n = []
    ln_grads = [(0, (), F32), (1, (), BF16), (2, (0,), F32), (3, (0,), F32)]
    for l in reversed(range(DEPTH)):
        sv = saved[l]
        dln_g, dln_b = [None] * 3, [None] * 3
        dres, dy3, dln_g[2], dln_b[2] = stage_bwd("lnres_bwd", l, _lnres_block, (t // rows_wide,), sv["ln3"], ln_specs,
                                                  [(dh,)], [row1(rows_wide, d)], ln_grads)
        ex.grad("w_ffn_out", l, mm("mm_ffn_out_dw", l, sv["act"], dy3, ta=True, out_dtype=BF16))
        (dgu,) = stage_bwd("swiglu_bwd", l, _swiglu_block, (t // rows,), [sv["gu"]], [row1(rows, 2 * FFN_HIDDEN)],
                           [], [], [(0, (), BF16)], ct_product=(dy3, row1(rows, d), ex.weight("w_ffn_out", l)))
        ex.grad("w_ffn_in", l, mm("mm_ffn_in_dw", l, sv["h2_bf"], dgu, ta=True, out_dtype=BF16))
        dh2 = mm("mm_ffn_in_dx", l, dgu, ex.weight("w_ffn_in", l), tb=True, add=dres)
        dres, dy2, dln_g[1], dln_b[1] = stage_bwd("lnres_bwd", l, _lnres_block, (t // rows_wide,), sv["ln2"], ln_specs,
                                                  [(dh2,)], [row1(rows_wide, d)], ln_grads)
        ex.grad("w_xo", l, mm("mm_sq_dw", l, sv["o"], dy2, ta=True, out_dtype=BF16))
        do = mm("mm_sq_dx", l, dy2, ex.weight("w_xo", l), tb=True, out_dtype=BF16)
        attn_ins, attn_specs, attn_out_spec = sv["attn"]
        dq, dkv = stage_bwd("attn_bwd", l, _attn_block, (nb, s // tq), attn_ins, attn_specs, [(do,)], [attn_out_spec],
                            [(0, (), BF16), (1, (1,), F32)])
        ex.grad("w_xq", l, mm("mm_sq_dw", l, sv["h1_bf"], dq, ta=True, out_dtype=BF16))
        dh1 = mm("mm_sq_dx", l, dq, ex.weight("w_xq", l), tb=True, add=dres)
        ex.grad("w_xkv", l, mm("mm_kv_dw", l, mem_n, dkv, ta=True, out_dtype=BF16))
        dmem_n.append(mm("mm_kv_dx", l, dkv, ex.weight("w_xkv", l), tb=True))
        dres, dy1, dln_g[0], dln_b[0] = stage_bwd("lnres_bwd", l, _lnres_block, (t // rows_wide,), sv["ln1"], ln_specs,
                                                  [(dh1,)], [row1(rows_wide, d)], ln_grads)
        g_small["ln_g"][l] = jnp.concatenate(dln_g, axis=0)
        g_small["ln_b"][l] = jnp.concatenate(dln_b, axis=0)
        ex.grad("w_mix_o", l, mm("mm_sq_dw", l, sv["merged"], dy1, ta=True, out_dtype=BF16))
        merge_ins, merge_specs, merge_out_spec = sv["merge"]
        dproj, dbr_a, dbr_b = stage_bwd("merge_bwd", l, _merge_block, (t // rows_wide,), merge_ins, merge_specs, [], [],
                                        [(0, (), BF16, ((t, P_COLS), merge_specs[0])), (1, (), BF16), (2, (), BF16)],
                                        ct_product=(dy1, merge_out_spec, ex.weight("w_mix_o", l)))
        ex.grad("p_a", l, mm("mm_sq_dw", l, sv["a_out"], dbr_a, ta=True, out_dtype=BF16))
        da_out = mm("mm_sq_dx", l, dbr_a, ex.weight("p_a", l), tb=True, out_dtype=BF16)
        ex.grad("p_b", l, mm("mm_pb_dw", l, sv["y_ssd"], dbr_b, ta=True, out_dtype=BF16))
        dy_ssd = mm("mm_pb_dx", l, dbr_b, ex.weight("p_b", l), tb=True, out_dtype=BF16)
        sgu_ins, sgu_specs = sv["sgu"]
        dproj, dsg_ln_g, dsg_ln_b, dsg_w, dsg_b = stage_bwd(
            "sgu_bwd", l, _sgu_block, (t // CHUNK,), sgu_ins, [row1(CHUNK, 2 * d)] + sgu_specs[1:], [(da_out,)],
            [row1(CHUNK, d)],
            [(0, (), BF16, ((t, P_COLS), row1(CHUNK, 2 * d)), dproj), (1, (0,), F32), (2, (0,), F32), (3, (0,), F32),
             (4, (0,), F32)])
        g_small["sg_ln_g"][l], g_small["sg_ln_b"][l], g_small["sg_w"][l], g_small["sg_b"][l] = (
            dsg_ln_g[0], dsg_ln_b[0], dsg_w, dsg_b.T)
        xc, dt_raw, prevs, y_pre, ssd_par = sv["ssd"]
        comm = ex.before("ssd_bwd", l)
        (dxc, ddt, dproj, ddtb, dal, dds, dng), comm_outs = _ssd_bwd(xc, dt_raw, sv["proj"], prevs, y_pre, *ssd_par,
                                                                     dy_ssd, dproj, nb, nc, comm=comm)
        if comm is not None:
            ex.after("ssd_bwd", l, comm_outs)
        g_small["dt_bias"][l], g_small["a_log"][l], g_small["d_skip"][l] = (
            ddtb[0, :SSM_HEADS], dal[0, :SSM_HEADS], dds[0, :SSM_HEADS])
        g_small["ssm_norm_g"][l] = dng[0]
        conv_ins, conv_specs, conv_out_spec = sv["conv"]
        dproj, dconv_w, dconv_b = stage_bwd("conv_bwd", l, _conv_block, (SSM_CONV_DIM // 256, nb), conv_ins, conv_specs,
                                            [(dxc,)], [conv_out_spec],
                                            [(0, (), BF16, ((t, P_COLS), conv_specs[0]), dproj), (1, (1,), F32),
                                             (2, (1,), F32)])
        g_small["conv_w"][l], g_small["conv_b"][l] = dconv_w, dconv_b[0]
        if l == 0:
            dmg, dmb = stage_bwd("memln_bwd", l, _memln_block, (nb * MEM_LEN // 256,), mem_ins, mem_specs,
                                 [tuple(dmem_n)], [row1(256, d)], [(1, (0,), F32), (2, (0,), F32)])
            done = {n: jnp.stack(g, axis=0) for n, g in g_small.items()}
            done["mem_ln_g"], done["mem_ln_b"] = dmg[0], dmb[0]
            ex.small_grads(done)
        w_p, w_dt = ex.weight("w_in", l)
        g_dt = mm("mm_dt_dw", l, sv["h_bf"], ddt, ta=True, out_dtype=BF16)
        ex.grad("w_in", l, _unpack_w_in(mm("mm_in_dw", l, sv["h_bf"], dproj, ta=True, out_dtype=BF16), g_dt))
        dh = mm("mm_in_dx", l, dproj, w_p, tb=True, add=dres, extra=(ddt, w_dt))

    return loss, dh.reshape(nb, s, d)


def _pack_flat(arrays, rows):
    flat = jnp.concatenate([a.reshape(-1) for a in arrays])
    return jnp.pad(flat, (0, rows * 128 - flat.shape[0])).reshape(rows, 128)


def _unpack_flat(packed, shapes):
    lead = packed.shape[:-2]
    flat = packed.reshape(lead + (-1,))
    out, pos = [], 0
    for shape in shapes:
        n = math.prod(shape)
        out.append(flat[..., pos:pos + n].reshape(lead + tuple(shape)))
        pos += n
    return out


def _small_rows(n_elems):
    return -(-n_elems // (128 * SMALL_ROW_TILE)) * SMALL_ROW_TILE


def _from_shards(name, gathered):
    _, a, b = gathered.shape
    if name == "w_in":
        return tuple(_pack_w_in(gathered))
    if name in BIG_COL_SHARDED:
        return _join_columns(gathered)
    return gathered.reshape(N_DEV * a, b)


def _to_shards(name, g):
    if name == "w_in":
        return g
    if name in BIG_COL_SHARDED:
        return _split_columns(g)
    a, b = g.shape
    return g.reshape(N_DEV, a // N_DEV, b)


def _join_columns(gathered):
    _, r, b = gathered.shape
    tr = _pick_tile(r, (128,))

    def body(g_ref, o_ref):
        o_ref[...] = jnp.concatenate([g_ref[j].astype(F32) for j in range(N_DEV)], axis=1).astype(o_ref.dtype)

    return _call("join_columns", body, grid=(r // tr,), ins=[gathered],
                 in_specs=[pl.BlockSpec((N_DEV, tr, b), lambda i: (0, i, 0))],
                 out_specs=[pl.BlockSpec((tr, N_DEV * b), lambda i: (i, 0))],
                 out_shape=[jax.ShapeDtypeStruct((r, N_DEV * b), gathered.dtype)])[0][0]


def _split_columns(full):
    r, nb = full.shape
    b = nb // N_DEV
    tr = _pick_tile(r, (128,))

    def body(f_ref, o_ref):
        w = f_ref[...].astype(F32)
        for j in range(N_DEV):
            o_ref[j] = w[:, j * b:(j + 1) * b].astype(o_ref.dtype)

    return _call("split_columns", body, grid=(r // tr,), ins=[full],
                 in_specs=[pl.BlockSpec((tr, nb), lambda i: (i, 0))],
                 out_specs=[pl.BlockSpec((N_DEV, tr, b), lambda i: (0, i, 0))],
                 out_shape=[jax.ShapeDtypeStruct((N_DEV, r, b), full.dtype)])[0][0]


class _MeshExchange:
    def __init__(self, shards_bf16, first):
        self.shards = shards_bf16
        self.full = dict(first)
        self.pieces = {}
        self.grads = {}
        self.to_send = {}
        self.received = {}
        self.small = None
        self.small_gathered = None

    def weight(self, name, l):
        if (name, l) not in self.full:
            got = jnp.concatenate([self.pieces[(name, l, q)] for q in range(W_IN_PIECES)], axis=1)
            self.full[(name, l)] = _from_shards(name, got)
        return self.full[(name, l)]

    def grad(self, name, l, g):
        self.grads[(name, l)] = g

    def small_grads(self, done):
        self.small = done

    def partial_sums(self, name, l):
        if (name, l, None) in self.received:
            return self.received[(name, l, None)]
        return jnp.concatenate([self.received[(name, l, q)] for q in range(W_IN_PIECES)], axis=1)

    def _slices(self, name, l, piece):
        n_rows = D_MODEL // W_IN_PIECES
        key, rows = (name, l), None if piece is None else (piece * n_rows, n_rows)
        if key not in self.to_send:
            self.to_send[key] = _to_shards(name, self.grads[key])
        return self.to_send[key], rows

    def before(self, call, l):
        comm = _Comm()
        for name, layer, piece in GATHER_PLAN.get((call, l), ()):
            n_rows = D_MODEL // W_IN_PIECES
            comm.gathers.append((self.shards[name], layer, None if piece is None else (piece * n_rows, n_rows)))
        if (call, l) == SMALL_GATHER_CALL:
            names = SMALL_REP + SMALL_SH
            rows = _small_rows(sum(math.prod(self.small[n].shape) for n in names))
            comm.gathers.append((_pack_flat([self.small[n] for n in names], rows), None, None))
        for name, layer, piece in SCATTER_PLAN.get((call, l), ()):
            comm.scatters.append(self._slices(name, layer, piece))
        return comm if comm.gathers or comm.scatters else None

    def after(self, call, l, outs):
        gathers = list(GATHER_PLAN.get((call, l), ()))
        for (name, layer, piece), out in zip(gathers, outs):
            if piece is None:
                self.full[(name, layer)] = _from_shards(name, out)
            else:
                self.pieces[(name, layer, piece)] = out
        outs = outs[len(gathers):]
        if (call, l) == SMALL_GATHER_CALL:
            self.small_gathered = outs[0]
            outs = outs[1:]
        for item, out in zip(SCATTER_PLAN.get((call, l), ()), outs):
            self.received[item] = out


def kernel(x, mem, mem_ln_g, mem_ln_b, w_in, sg_ln_g, sg_ln_b, sg_w, sg_b, conv_w, conv_b, dt_bias, a_log, d_skip, ssm_norm_g, p_a, p_b, w_mix_o, w_xq, w_xkv, w_xo, w_ffn_in, w_ffn_out, ln_g, ln_b, loss_target, m_mem_ln_g, m_mem_ln_b, m_w_in, m_sg_ln_g, m_sg_ln_b, m_sg_w, m_sg_b, m_conv_w, m_conv_b, m_dt_bias, m_a_log, m_d_skip, m_ssm_norm_g, m_p_a, m_p_b, m_w_mix_o, m_w_xq, m_w_xkv, m_w_xo, m_w_ffn_in, m_w_ffn_out, m_ln_g, m_ln_b, v_mem_ln_g, v_mem_ln_b, v_w_in, v_sg_ln_g, v_sg_ln_b, v_sg_w, v_sg_b, v_conv_w, v_conv_b, v_dt_bias, v_a_log, v_d_skip, v_ssm_norm_g, v_p_a, v_p_b, v_w_mix_o, v_w_xq, v_w_xkv, v_w_xo, v_w_ffn_in, v_w_ffn_out, v_ln_g, v_ln_b):
    args = dict(locals())
    w = {n: args[n] for n in WEIGHTS}
    m = {n: args["m_" + n] for n in WEIGHTS}
    v = {n: args["v_" + n] for n in WEIGHTS}
    me = 4 * lax.axis_index("x") + 2 * lax.axis_index("y") + lax.axis_index("c")

    shards = {n: w[n].astype(BF16) for n in BIG}
    sh_shapes = [w[n].shape for n in SMALL_SH]
    first = _Comm()
    first.gathers.append((shards["w_in"], 0, None))
    first.gathers.append((_pack_flat([w[n] for n in SMALL_SH], _small_rows(sum(math.prod(s) for s in sh_shapes))), None,
                          None))
    w_in0, small_sh = _comm_only("gather_first", first)
    small = {n: w[n] for n in SMALL_REP}
    for n, sh in zip(SMALL_SH, _unpack_flat(small_sh, sh_shapes)):
        small[n] = sh.transpose(1, 2, 0, 3).reshape(sh.shape[1], sh.shape[2], N_DEV * sh.shape[3])

    ex = _MeshExchange(shards, {("w_in", 0): _from_shards("w_in", w_in0)})
    loss, grad_x = _run_step(x, mem, loss_target, small, ex)
    loss = lax.psum(loss[0, 0], ("x", "y", "c"))

    out = {}
    for n in BIG[1:] + BIG[:1]:
        comm = ex.before("adamw_" + n, 0)
        out[n], comm_outs = _adamw_sharded("adamw_" + n, [ex.partial_sums(n, l) for l in range(DEPTH)], w[n], m[n], v[n],
                                           comm=comm)
        if comm is not None:
            ex.after("adamw_" + n, 0, comm_outs)
    names = SMALL_REP + SMALL_SH
    g_small = dict(zip(names, _unpack_flat(_sum_parts("sum_small_grads", ex.small_gathered),
                                           [ex.small[n].shape for n in names])))
    for n in names:
        g = g_small[n]
        if n in SMALL_SH:
            width = w[n].shape[-1]
            g = lax.dynamic_slice_in_dim(g, me * width, width, axis=-1)
        two_d = (-1, w[n].shape[-1])
        res = _adamw_small("adamw_" + n, g.reshape(two_d), w[n].reshape(two_d), m[n].reshape(two_d), v[n].reshape(two_d))
        out[n] = [g] + [r.reshape(w[n].shape) for r in res]

    results = []
    for k in range(4):
        results.extend(out[n][k] for n in WEIGHTS)
    return (loss, grad_x, *results)
```

```python
import functools
import math

import jax
import jax.numpy as jnp
from jax import lax
from jax.experimental import pallas as pl
from jax.experimental.pallas import tpu as pltpu

F32 = jnp.float32
BF16 = jnp.bfloat16
HIGHEST = lax.Precision.HIGHEST

N_DEV = 8
D_MODEL = 1024
DEPTH = 2
MEM_LEN = 256
CHUNK = 128
SG_GROUPS = 8
SSM_INNER = 2048
SSM_HEADDIM = 64
SSM_HEADS = 32
SSM_STATE = 128
SSM_GROUPS = 4
SSM_RPG = 8
SSM_CONV = 4
SSM_CONV_DIM = 3072
X_HEADS = 4
X_HEADDIM = 256
FFN_HIDDEN = 2816
ALPHA = float((2 * DEPTH) ** 0.25)
LN_EPS = 1e-5
RMS_EPS = 1e-5
XBC_COL = 4096
DT_COL = 7168
GA_COL = 7200
IN_COLS = 9248
P_GATE = 4096
P_XBC = 6144
P_COLS = 9216
DT_LANES = 128

ADAM_LR = 0.001
ADAM_B1 = 0.9
ADAM_B2 = 0.999
ADAM_EPS = 1e-08
ADAM_WD = 0.01
ADAM_STEP = 10

VMEM_LIMIT = 48 * 1024 * 1024
SMALL_ROW_TILE = 256

BIG = ("w_in", "p_a", "p_b", "w_mix_o", "w_xq", "w_xkv", "w_xo", "w_ffn_in", "w_ffn_out")
BIG_COL_SHARDED = ("w_in", "w_xkv", "w_ffn_in")
SMALL_REP = ("mem_ln_g", "mem_ln_b", "sg_ln_g", "sg_ln_b", "sg_w", "sg_b", "conv_b", "dt_bias", "a_log", "d_skip",
             "ssm_norm_g")
SMALL_SH = ("conv_w", "ln_g", "ln_b")
WEIGHTS = ("mem_ln_g", "mem_ln_b", "w_in", "sg_ln_g", "sg_ln_b", "sg_w", "sg_b", "conv_w", "conv_b", "dt_bias", "a_log",
           "d_skip", "ssm_norm_g", "p_a", "p_b", "w_mix_o", "w_xq", "w_xkv", "w_xo", "w_ffn_in", "w_ffn_out", "ln_g", "ln_b")

W_IN_PIECES = 4
GATHER_PLAN = {("sgu_fwd", 0): [("w_in", 1, 0)], ("conv_fwd", 0): [("w_in", 1, 1)],
               ("mm_ffn_in", 0): [("w_in", 1, 2), ("w_ffn_out", 0, None)], ("mm_ffn_out_ln", 0): [("w_in", 1, 3)],
               ("conv_fwd", 1): [("w_ffn_out", 1, None)]}
SCATTER_PLAN = {("mm_ffn_in_dw", 0): [("w_in", 1, 2)], ("mm_ffn_in_dx", 0): [("w_in", 1, 3)],
                ("adamw_w_ffn_in", 0): [("w_in", 0, 2)], ("adamw_w_xkv", 0): [("w_in", 0, 3)]}
for _l in range(DEPTH):
    GATHER_PLAN[("mm_in", _l)] = [(n, _l, None) for n in ("p_a", "p_b", "w_mix_o", "w_xq", "w_xkv", "w_xo")]
    GATHER_PLAN[("ssd_fwd", _l)] = [("w_ffn_in", _l, None)]
    SCATTER_PLAN[("swiglu_bwd", _l)] = [("w_ffn_out", _l, None)]
    SCATTER_PLAN[("sgu_bwd", _l)] = [("w_mix_o", _l, None), ("p_a", _l, None), ("w_xo", _l, None)]
    SCATTER_PLAN[("ssd_bwd", _l)] = [("w_ffn_in", _l, None), ("w_xkv", _l, None), ("w_xq", _l, None)]
    SCATTER_PLAN[("conv_bwd", _l)] = [("p_b", _l, None)]
    SCATTER_PLAN[("mm_in_dx", _l)] = [("w_in", _l, 0), ("w_in", _l, 1)]
SMALL_GATHER_CALL = ("mm_in_dw", 0)


def _layer_norm(x, g, b):
    mu = jnp.mean(x, axis=-1, keepdims=True)
    xc = x - mu
    var = jnp.mean(xc * xc, axis=-1, keepdims=True)
    return xc * lax.rsqrt(var + LN_EPS) * g + b


def _gelu(x):
    return 0.5 * x * (1.0 + lax.erf(x * (1.0 / math.sqrt(2.0))))


def _sigmoid(x):
    return 0.5 * jnp.tanh(0.5 * x) + 0.5


def _silu(x):
    return x * _sigmoid(x)


def _softplus(x):
    return jnp.maximum(x, 0.0) + jnp.log1p(jnp.exp(-jnp.abs(x)))


def _causal_mask():
    r = lax.broadcasted_iota(jnp.int32, (CHUNK, CHUNK), 0)
    c = lax.broadcasted_iota(jnp.int32, (CHUNK, CHUNK), 1)
    return r >= c


def _sgu_block(uv, ln_g, ln_b, w, sb):
    gu = _gelu(uv[:, :D_MODEL])
    vn = _layer_norm(_gelu(uv[:, D_MODEL:]), ln_g, ln_b).astype(BF16)
    causal = _causal_mask()
    width = D_MODEL // SG_GROUPS
    wgs = [jnp.where(causal, w[g], 0.0).astype(BF16) for g in range(SG_GROUPS)]
    chunks = []
    for c in range(uv.shape[0] // CHUNK):
        vc = vn[c * CHUNK:(c + 1) * CHUNK, :]
        chunks.append(jnp.concatenate(
            [jnp.dot(wgs[g], vc[:, g * width:(g + 1) * width], preferred_element_type=F32) + sb[:, g:g + 1]
             for g in range(SG_GROUPS)], axis=1))
    return (gu * jnp.concatenate(chunks, axis=0),)


GROUP_W = SSM_RPG * SSM_HEADDIM
NT_DIMS = (((1,), (1,)), ((), ()))
TN_DIMS = (((0,), (0,)), ((), ()))


def _mxu(a, b, dims=(((1,), (0,)), ((), ()))):
    return lax.dot_general(a.astype(BF16), b.astype(BF16), dims, preferred_element_type=F32)


def _head_expander():
    return (jnp.arange(SSM_INNER)[None, :] // SSM_HEADDIM == jnp.arange(128)[:, None]).astype(BF16)


def _bf16_terms(x, n):
    terms = []
    for _ in range(n):
        t = x.astype(BF16)
        terms.append(t)
        x = x - t.astype(F32)
    return terms


def _expand_heads(q, e):
    return sum(jnp.dot(t, e, preferred_element_type=F32) for t in _bf16_terms(q, 2))


def _reduce_heads(v, e, terms=2):
    return sum(lax.dot_general(t, e, NT_DIMS, preferred_element_type=F32) for t in _bf16_terms(v, terms))


def _reduce_heads_of_column_sums(v, e):
    sums = jnp.broadcast_to(jnp.sum(v, axis=0, keepdims=True), (8, v.shape[1]))
    return _reduce_heads(sums, e)[0:1, :]


def _ssd_common(xc, dtraw, dt_bias, a_log, e):
    xs = xc[:, :SSM_INNER]
    pre = dtraw + dt_bias
    dt = _softplus(pre)
    a = -jnp.exp(a_log)
    r_i = lax.broadcasted_iota(jnp.int32, (CHUNK, CHUNK), 0)
    c_i = lax.broadcasted_iota(jnp.int32, (CHUNK, CHUNK), 1)
    tril = jnp.where(r_i >= c_i, 1.0, 0.0).astype(F32)
    cs = jnp.dot(tril, dt * a, precision=HIGHEST, preferred_element_type=F32)
    cs_last = cs[CHUNK - 1:CHUNK, :]
    decay_in = jnp.exp(cs)
    decay_st = jnp.exp(cs_last - cs)
    if e is None:
        return dict(pre=pre, dt=dt, a=a, lower=r_i >= c_i, upper=c_i >= r_i, cs=cs, cs_t=cs.T, decay_in=decay_in,
                    decay_st=decay_st, chunk_decay=jnp.exp(cs_last),
                    low=lax.broadcasted_iota(jnp.int32, (CHUNK, 128), 1) < SSM_HEADDIM)
    dt_x = _expand_heads(dt, e)
    w_st_x = _expand_heads(dt * decay_st, e)
    decay_in_x = _expand_heads(decay_in, e)
    return dict(xs=xs, pre=pre, dt=dt, a=a, lower=r_i >= c_i, upper=c_i >= r_i, cs=cs, cs_t=cs.T, decay_in=decay_in,
                decay_st=decay_st, chunk_decay=jnp.exp(cs_last), dt_x=dt_x, w_st_x=w_st_x, decay_in_x=decay_in_x,
                chunk_decay_x=decay_in_x[CHUNK - 1:CHUNK, :], xdt=xs * dt_x, x_st=(xs * w_st_x).astype(BF16),
                low=lax.broadcasted_iota(jnp.int32, (CHUNK, 128), 1) < SSM_HEADDIM)


def _pair_decay(c, h):
    return jnp.exp(jnp.where(c["lower"], c["cs"][:, h:h + 1] - c["cs_t"][h:h + 1, :], -1e30))


def _pair_decay_t(c, h):
    return jnp.exp(jnp.where(c["upper"], c["cs_t"][h:h + 1, :] - c["cs"][:, h:h + 1], -1e30))


def _ssd_forward(xc, dtraw, z, prev, dt_bias, a_log, d_skip_x, norm_g, e):
    c = _ssd_common(xc, dtraw, dt_bias, a_log, e)
    y_groups, new_states = [], []
    for g in range(SSM_GROUPS):
        lanes = slice(g * GROUP_W, (g + 1) * GROUP_W)
        bg = xc[:, SSM_INNER + g * SSM_STATE:SSM_INNER + (g + 1) * SSM_STATE]
        cg = xc[:, SSM_INNER + (SSM_GROUPS + g) * SSM_STATE:SSM_INNER + (SSM_GROUPS + g + 1) * SSM_STATE].astype(BF16)
        pg = prev[g * SSM_STATE:(g + 1) * SSM_STATE, :]
        cb = _mxu(cg, bg, NT_DIMS)
        y_in = _mxu(cg, pg) * c["decay_in_x"][:, lanes]
        new_states.append(pg * c["chunk_decay_x"][:, lanes] + _mxu(bg.T, c["x_st"][:, lanes]))
        pairs = []
        for j in range(SSM_RPG // 2):
            h0 = g * SSM_RPG + 2 * j
            xp = c["xdt"][:, 128 * (h0 // 2):128 * (h0 // 2 + 1)]
            pairs.append(_mxu(cb * _pair_decay(c, h0), jnp.where(c["low"], xp, 0.0))
                         + _mxu(cb * _pair_decay(c, h0 + 1), jnp.where(c["low"], 0.0, xp)))
        y_groups.append(jnp.concatenate(pairs, axis=1) + y_in)
    y_pre = jnp.concatenate(y_groups, axis=1) + c["xs"] * d_skip_x
    gated = y_pre * _silu(z)
    normed = [gated[:, g * GROUP_W:(g + 1) * GROUP_W] for g in range(SSM_GROUPS)]
    normed = [yg * lax.rsqrt(jnp.mean(yg * yg, axis=-1, keepdims=True) + RMS_EPS) for yg in normed]
    return jnp.concatenate(normed, axis=1) * norm_g, y_pre, jnp.concatenate(new_states, axis=0)


def _ssd_backward(xc_ref, dtraw, z_ref, prev_ref, ypre_ref, dt_bias, a_log, d_skip_x, norm_g, e_ref, dout_ref, dst_ref,
                  dxc_ref, dz_ref):
    c = _ssd_common(xc_ref, dtraw, dt_bias, a_log, None)
    lane = lax.broadcasted_iota(jnp.int32, (CHUNK, 128), 1)
    sub = lax.broadcasted_iota(jnp.int32, (8, 128), 0)
    dcs_neg = jnp.zeros((CHUNK, 128), F32)
    row_slabs = []
    ddt = jnp.zeros((CHUNK, 128), F32)
    dw_st = jnp.zeros((CHUNK, 128), F32)
    d_decay_in = jnp.zeros((CHUNK, 128), F32)
    dd_skip = jnp.zeros((1, 128), F32)
    d_chunk_decay = jnp.zeros((1, 128), F32)
    dnorm_g = []
    for g in range(SSM_GROUPS):
        lanes = slice(g * GROUP_W, (g + 1) * GROUP_W)
        states = slice(g * SSM_STATE, (g + 1) * SSM_STATE)
        b_cols = slice(SSM_INNER + g * SSM_STATE, SSM_INNER + (g + 1) * SSM_STATE)
        c_cols = slice(SSM_INNER + (SSM_GROUPS + g) * SSM_STATE, SSM_INNER + (SSM_GROUPS + g + 1) * SSM_STATE)
        e_g = e_ref[:, lanes]
        xs = xc_ref[:, lanes]
        z = z_ref[:, lanes].astype(F32)
        y_pre = ypre_ref[:, lanes]
        dout = dout_ref[:, lanes].astype(F32)
        dt_x = _expand_heads(c["dt"], e_g)
        w_st_x = _expand_heads(c["dt"] * c["decay_st"], e_g)
        decay_in_x = _expand_heads(c["decay_in"], e_g)
        chunk_decay_x = decay_in_x[CHUNK - 1:CHUNK, :]
        xdt = xs * dt_x
        x_st = (xs * w_st_x).astype(BF16)
        sig = _sigmoid(z)
        silu_z = z * sig
        yg = y_pre * silu_z
        r = lax.rsqrt(jnp.mean(yg * yg, axis=-1, keepdims=True) + RMS_EPS)
        n = yg * r
        gh = dout * norm_g[:, lanes]
        d_gated = r * (gh - n * jnp.mean(gh * n, axis=-1, keepdims=True))
        dnorm_g.append(jnp.sum(dout * n, axis=0, keepdims=True))
        dy_g = d_gated * silu_z
        dz_ref[:, lanes] = (d_gated * y_pre * (sig * (1.0 + z * (1.0 - sig)))).astype(dz_ref.dtype)
        dxs = dy_g * d_skip_x[:, lanes]
        dd_skip = dd_skip + _reduce_heads_of_column_sums(dy_g * xs, e_g)

        bg = xc_ref[:, b_cols].astype(BF16)
        cg_f = xc_ref[:, c_cols]
        cg = cg_f.astype(BF16)
        pg = prev_ref[states, :]
        dng = dst_ref[states, :]
        cb_t = _mxu(bg, cg, NT_DIMS)
        t1 = (dy_g * decay_in_x).astype(BF16)
        d_decay_in = d_decay_in + _reduce_heads(dy_g * _mxu(cg, pg), e_g, terms=1)
        dc = _mxu(t1, pg, NT_DIMS)
        dst_ref[states, :] = _mxu(cg_f.T, t1) + dng * chunk_decay_x
        d_chunk_decay = d_chunk_decay + _reduce_heads_of_column_sums(dng * pg, e_g)
        db = _mxu(x_st, dng, NT_DIMS)
        dx_st = _mxu(bg, dng)
        dcb_t = jnp.zeros((CHUNK, CHUNK), F32)
        rows = []
        dxdt = []
        for j in range(SSM_RPG // 2):
            h0 = g * SSM_RPG + 2 * j
            blk = slice(128 * j, 128 * (j + 1))
            xp = xdt[:, blk]
            dyp = dy_g[:, blk].astype(BF16)
            pair_dx = []
            for k, xk in enumerate((jnp.where(c["low"], xp, 0.0), jnp.where(c["low"], 0.0, xp))):
                dec_t = _pair_decay_t(c, h0 + k)
                pair_dx.append(_mxu(cb_t * dec_t, dyp))
                dml_t = _mxu(xk, dyp, NT_DIMS) * dec_t
                dcb_t = dcb_t + dml_t
                dseg_t = dml_t * cb_t
                dcs_neg = dcs_neg + jnp.where(lane == h0 + k, jnp.sum(dseg_t, axis=-1, keepdims=True), 0.0)
                rows.append(jnp.sum(dseg_t, axis=0, keepdims=True))
            dxdt.append(jnp.where(c["low"], pair_dx[0], pair_dx[1]))
        slab = jnp.zeros((8, 128), F32)
        for k in range(SSM_RPG):
            slab = slab + jnp.where(sub == k, rows[k], 0.0)
        row_slabs.append(slab)
        dxc_ref[:, c_cols] = dc + _mxu(dcb_t.T, bg)
        dxc_ref[:, b_cols] = db + _mxu(dcb_t, cg)
        dxdt = jnp.concatenate(dxdt, axis=1)
        dxc_ref[:, lanes] = dxs + dxdt * dt_x + dx_st * w_st_x
        ddt = ddt + _reduce_heads(dxdt * xs, e_g, terms=1)
        dw_st = dw_st + _reduce_heads(dx_st * xs, e_g, terms=1)
    by_head = jnp.concatenate(row_slabs + [jnp.zeros((CHUNK - SSM_HEADS, 128), F32)], axis=0)
    dcs = by_head.T - dcs_neg
    dcs = dcs + d_decay_in * c["decay_in"]
    ddt = ddt + dw_st * c["decay_st"]
    d_log_st = dw_st * c["dt"] * c["decay_st"]
    dcs = dcs - d_log_st
    dcs_last = jnp.sum(d_log_st, axis=0, keepdims=True) + d_chunk_decay * c["chunk_decay"]
    row = lax.broadcasted_iota(jnp.int32, (CHUNK, 128), 0)
    dcs = dcs + jnp.where(row == CHUNK - 1, dcs_last, 0.0)
    triu = jnp.where(c["upper"], 1.0, 0.0).astype(F32)
    dda = jnp.dot(triu, dcs, precision=HIGHEST, preferred_element_type=F32)
    ddt = ddt + dda * c["a"]
    da_log = jnp.sum(dda * c["dt"], axis=0, keepdims=True) * c["a"]
    dpre = ddt * _sigmoid(c["pre"])
    return dpre, jnp.sum(dpre, axis=0, keepdims=True), da_log, dd_skip, jnp.concatenate(dnorm_g, axis=1)


def _conv_block(x, w, b):
    rows = lax.broadcasted_iota(jnp.int32, x.shape, 0)
    acc = x * w[SSM_CONV - 1:SSM_CONV, :] + b
    for k in range(SSM_CONV - 1):
        shift = SSM_CONV - 1 - k
        acc = acc + _shift_rows(x, rows, shift) * w[k:k + 1, :]
    return (_silu(acc),)


@functools.partial(jax.custom_vjp, nondiff_argnums=(2,))
def _shift_rows(x, rows, shift):
    return jnp.where(rows >= shift, pltpu.roll(x, shift, 0), 0.0)


def _shift_rows_fwd(x, rows, shift):
    return _shift_rows(x, rows, shift), rows


def _shift_rows_bwd(shift, rows, g):
    n = g.shape[0]
    return jnp.where(rows < n - shift, pltpu.roll(g, n - shift, 0), 0.0), None


_shift_rows.defvjp(_shift_rows_fwd, _shift_rows_bwd)


def _merge_block(gates, br_a, br_b):
    return (_sigmoid(gates[:, :D_MODEL]) * br_a + _sigmoid(gates[:, D_MODEL:]) * br_b,)


def _lnres_block(x, y, g, b):
    return (_layer_norm(ALPHA * x + y, g, b),)


def _memln_block(x, g, b):
    return (_layer_norm(x, g, b),)


def _attn_block(q, kv):
    outs = []
    for h in range(X_HEADS):
        qh = q[:, h * X_HEADDIM:(h + 1) * X_HEADDIM].astype(BF16)
        kh = kv[:, h * X_HEADDIM:(h + 1) * X_HEADDIM].astype(BF16)
        vh = kv[:, D_MODEL + h * X_HEADDIM:D_MODEL + (h + 1) * X_HEADDIM].astype(BF16)
        s = lax.dot_general(qh, kh, (((1,), (1,)), ((), ())), preferred_element_type=F32) * (X_HEADDIM ** -0.5)
        s = s - lax.stop_gradient(jnp.max(s, axis=-1, keepdims=True))
        e = jnp.exp(s)
        p = e / jnp.sum(e, axis=-1, keepdims=True)
        outs.append(jnp.dot(p.astype(BF16), vh, preferred_element_type=F32))
    return (jnp.concatenate(outs, axis=1),)


def _swiglu_block(gu):
    return (_silu(gu[:, :FFN_HIDDEN]) * gu[:, FFN_HIDDEN:],)


class _Comm:
    def __init__(self):
        self.gathers = []
        self.scatters = []

    @staticmethod
    def _rows(ref, rows):
        return ref if rows is None else ref.at[pl.ds(rows[0], rows[1])]

    def operands(self):
        ins = [a for a, _, _ in self.gathers] + [a for a, _ in self.scatters]
        shapes = []
        for a, idx, rows in self.gathers:
            blk = a.shape if idx is None else a.shape[1:]
            shapes.append(jax.ShapeDtypeStruct((N_DEV, blk[0] if rows is None else rows[1]) + tuple(blk[1:]), a.dtype))
        for a, rows in self.scatters:
            shapes.append(jax.ShapeDtypeStruct((N_DEV, a.shape[1] if rows is None else rows[1]) + tuple(a.shape[2:]),
                                               a.dtype))
        scratch = []
        for n in (len(self.gathers), len(self.scatters)):
            if n:
                scratch += [pltpu.SemaphoreType.DMA((7 * n,)), pltpu.SemaphoreType.DMA((7 * n,)),
                            pltpu.SemaphoreType.DMA((n,))]
        return ins, shapes, scratch

    def _split(self, in_refs, out_refs, sems):
        ng = len(self.gathers)
        g_sems = sems[:3] if ng else None
        s_sems = sems[3:] if ng else sems
        return in_refs[:ng], in_refs[ng:], out_refs[:ng], out_refs[ng:], g_sems, s_sems

    def _gather_copies(self, i, src_ref, out_ref, sems):
        send_sems, recv_sems, local_sems = sems
        x, y, c = lax.axis_index("x"), lax.axis_index("y"), lax.axis_index("c")
        me, sibling = (x, y, c), (x, y, 1 - c)
        chips = [(1 - x, y), (x, 1 - y), (1 - x, 1 - y)]
        _, idx, rows = self.gathers[i]
        src = self._rows(src_ref if idx is None else src_ref.at[idx], rows)

        def slot(px, py, pc):
            return out_ref.at[4 * px + 2 * py + pc]

        def copy(k, blk, to, from_src=False):
            return pltpu.make_async_remote_copy(
                src_ref=src if from_src else slot(*blk), dst_ref=slot(*blk), send_sem=send_sems.at[7 * i + k],
                recv_sem=recv_sems.at[7 * i + k], device_id=to, device_id_type=pl.DeviceIdType.MESH)

        mine = pltpu.make_async_copy(src, slot(*me), local_sems.at[i])
        first = [copy(0, me, sibling, True)] + [copy(1 + j, me, (*chip, c), True) for j, chip in enumerate(chips)]
        passed = [copy(4 + j, (*chip, c), sibling) for j, chip in enumerate(chips)]
        arrivals = [copy(1 + j, (*chip, c), me) for j, chip in enumerate(chips)]
        from_sibling = [copy(0, sibling, me)] + [copy(4 + j, (*chip, 1 - c), me) for j, chip in enumerate(chips)]
        return mine, first, passed, arrivals, from_sibling

    def _scatter_copies(self, i, src_ref, out_ref, sems):
        send_sems, recv_sems, local_sems = sems
        x, y, c = lax.axis_index("x"), lax.axis_index("y"), lax.axis_index("c")
        me = 4 * x + 2 * y + c
        rows = self.scatters[i][1]
        mine = pltpu.make_async_copy(self._rows(src_ref.at[me], rows), out_ref.at[me], local_sems.at[i])
        copies = []
        for k in range(1, N_DEV):
            px = 1 - x if k & 4 else x
            py = 1 - y if k & 2 else y
            pc = 1 - c if k & 1 else c
            copies.append(pltpu.make_async_remote_copy(
                src_ref=self._rows(src_ref.at[4 * px + 2 * py + pc], rows), dst_ref=out_ref.at[me],
                send_sem=send_sems.at[7 * i + k - 1], recv_sem=recv_sems.at[7 * i + k - 1], device_id=(px, py, pc),
                device_id_type=pl.DeviceIdType.MESH))
        return mine, copies

    def start(self, in_refs, out_refs, sems):
        g_in, s_in, g_out, s_out, g_sems, s_sems = self._split(in_refs, out_refs, sems)
        for i in range(len(self.gathers)):
            mine, first, _, _, _ = self._gather_copies(i, g_in[i], g_out[i], g_sems)
            mine.start()
            for cp in first:
                cp.start()
        for i in range(len(self.scatters)):
            mine, copies = self._scatter_copies(i, s_in[i], s_out[i], s_sems)
            mine.start()
            for cp in copies:
                cp.start()

    def finish(self, in_refs, out_refs, sems):
        g_in, s_in, g_out, s_out, g_sems, s_sems = self._split(in_refs, out_refs, sems)
        parts = [self._gather_copies(i, g_in[i], g_out[i], g_sems) for i in range(len(self.gathers))]
        for j in range(3):
            for _, _, passed, arrivals, _ in parts:
                arrivals[j].wait_recv()
                passed[j].start()
        for mine, first, passed, _, from_sibling in parts:
            for cp in from_sibling:
                cp.wait_recv()
            for cp in first + passed:
                cp.wait_send()
            mine.wait()
        for i in range(len(self.scatters)):
            mine, copies = self._scatter_copies(i, s_in[i], s_out[i], s_sems)
            for cp in copies:
                cp.wait_recv()
            for cp in copies:
                cp.wait_send()
            mine.wait()


def _params(grid):
    return pltpu.CompilerParams(dimension_semantics=("arbitrary",) * len(grid), vmem_limit_bytes=VMEM_LIMIT)


def _call(name, body, *, grid, ins, in_specs, out_shape, out_specs, scratch=(), comm=None, aliases=None):
    n_in, n_out, n_scr = len(ins), len(out_shape), len(scratch)
    aliases = aliases or {}
    if comm is None:
        outs = pl.pallas_call(body, grid=grid, in_specs=list(in_specs), out_specs=list(out_specs),
                              out_shape=list(out_shape), scratch_shapes=list(scratch), name=name,
                              input_output_aliases=aliases, compiler_params=_params(grid))(*ins)
        return list(outs), []
    c_ins, c_shapes, c_scratch = comm.operands()
    nci, nco = len(c_ins), len(c_shapes)
    anywhere = pl.BlockSpec(memory_space=pl.ANY)

    def carrier(*refs):
        main_in, comm_in = refs[:n_in], refs[n_in:n_in + nci]
        o0 = n_in + nci
        main_out, comm_out = refs[o0:o0 + n_out], refs[o0 + n_out:o0 + n_out + nco]
        s0 = o0 + n_out + nco
        main_scr, comm_scr = refs[s0:s0 + n_scr], refs[s0 + n_scr:]
        first = pl.program_id(0) == 0
        last = pl.program_id(0) == grid[0] - 1
        for ax in range(1, len(grid)):
            first = first & (pl.program_id(ax) == 0)
            last = last & (pl.program_id(ax) == grid[ax] - 1)

        @pl.when(first)
        def _():
            comm.start(comm_in, comm_out, comm_scr)

        body(*main_in, *main_out, *main_scr)

        @pl.when(last)
        def _():
            comm.finish(comm_in, comm_out, comm_scr)

    outs = pl.pallas_call(carrier, grid=grid, in_specs=list(in_specs) + [anywhere] * nci,
                          out_specs=list(out_specs) + [anywhere] * nco, out_shape=list(out_shape) + c_shapes,
                          scratch_shapes=list(scratch) + c_scratch, name=name, input_output_aliases=aliases,
                          compiler_params=_params(grid))(*ins, *c_ins)
    return list(outs[:n_out]), list(outs[n_out:])


def _comm_only(name, comm):
    c_ins, c_shapes, c_scratch = comm.operands()
    nci, nco = len(c_ins), len(c_shapes)
    anywhere = pl.BlockSpec(memory_space=pl.ANY)

    def body(*refs):
        comm.start(refs[:nci], refs[nci:nci + nco], refs[nci + nco:])
        comm.finish(refs[:nci], refs[nci:nci + nco], refs[nci + nco:])

    return list(pl.pallas_call(body, in_specs=[anywhere] * nci, out_specs=[anywhere] * nco, out_shape=c_shapes,
                               scratch_shapes=c_scratch, name=name)(*c_ins))


def _stage_fwd(name, f, grid, ins, in_specs, out_shapes, out_specs, comm=None):
    n_in = len(ins)

    def body(*refs):
        res = f(*[r[...].astype(F32) for r in refs[:n_in]])
        for o_ref, val in zip(refs[n_in:], res):
            o_ref[...] = val.astype(o_ref.dtype)

    return _call(name, body, grid=grid, ins=ins, in_specs=in_specs, out_shape=out_shapes, out_specs=out_specs, comm=comm)


def _stage_bwd(name, f, grid, ins, in_specs, cts, ct_specs, grads, comm=None, ct_product=None):
    n_in = len(ins)
    flat_cts = [c for group in cts for c in group]
    flat_ct_specs = [s for group, spec in zip(cts, ct_specs) for s in (spec,) * len(group)]
    if ct_product is not None:
        assert not cts
        flat_cts = [ct_product[0], ct_product[2]]
        flat_ct_specs = [ct_product[1], pl.BlockSpec(ct_product[2].shape, lambda *_: (0, 0))]
    n_ct = len(flat_cts)
    diff = [g[0] for g in grads]
    buffers = [(k, g[4]) for k, g in enumerate(grads) if len(g) > 4]
    n_buf = len(buffers)

    def body(*refs):
        vals = [r[...].astype(F32) for r in refs[:n_in]]
        ct_refs = refs[n_in:n_in + n_ct]
        g_refs = refs[n_in + n_ct + n_buf:]
        ct_vals, pos = [], 0
        if ct_product is not None:
            ct_vals.append(lax.dot_general(ct_refs[0][...].astype(BF16), ct_refs[1][...].astype(BF16), NT_DIMS,
                                           preferred_element_type=F32))
        for group in cts:
            acc = ct_refs[pos][...].astype(F32)
            for j in range(1, len(group)):
                acc = acc + ct_refs[pos + j][...].astype(F32)
            ct_vals.append(acc)
            pos += len(group)

        def g_fn(*dvals):
            full = list(vals)
            for i, dv in zip(diff, dvals):
                full[i] = dv
            return f(*full)

        _, vjp = jax.vjp(g_fn, *[vals[i] for i in diff])
        gvals = vjp(tuple(ct_vals))
        for gspec, g_ref, gval in zip(grads, g_refs, gvals):
            acc_axes = gspec[1]
            if not acc_axes:
                g_ref[...] = gval.astype(g_ref.dtype)
            else:
                first = pl.program_id(acc_axes[0]) == 0
                for ax in acc_axes[1:]:
                    first = first & (pl.program_id(ax) == 0)

                @pl.when(first)
                def _():
                    g_ref[...] = jnp.zeros_like(g_ref)

                g_ref[...] += gval.astype(g_ref.dtype)

    out_shapes, out_specs = [], []
    for gspec in grads:
        shape, spec = gspec[3] if len(gspec) > 3 else (ins[gspec[0]].shape, in_specs[gspec[0]])
        out_shapes.append(jax.ShapeDtypeStruct(shape, gspec[2]))
        out_specs.append(spec)
    anywhere = pl.BlockSpec(memory_space=pl.ANY)
    return _call(name, body, grid=grid, ins=list(ins) + flat_cts + [b for _, b in buffers],
                 in_specs=list(in_specs) + flat_ct_specs + [anywhere] * n_buf, out_shape=out_shapes, out_specs=out_specs,
                 comm=comm, aliases={n_in + n_ct + j: k for j, (k, _) in enumerate(buffers)})


def _pick_tile(n, candidates):
    for c in candidates:
        if n % c == 0:
            return c
    return n


def _matmul(name, a, b, *, ta=False, tb=False, add=None, extra=None, out_dtype=F32, comm=None):
    if ta:
        k_dim, m = a.shape
    else:
        m, k_dim = a.shape
    n = b.shape[0] if tb else b.shape[1]
    assert (b.shape[1] if tb else b.shape[0]) == k_dim and not (ta and tb)
    tm = _pick_tile(m, (1024, 1408, 512, 256, 128))
    tn = _pick_tile(n, (1024, 1408, 512, 256, 128))
    if ta:
        tk = _pick_tile(k_dim, (1024, 512, 256, 128))
    elif k_dim <= 2816:
        tk = k_dim
    else:
        tk = _pick_tile(k_dim, (1408, 1024, 512, 256, 128))
    nk = k_dim // tk
    grid = (m // tm, n // tn, nk)
    a_spec = pl.BlockSpec((tk, tm), lambda i, j, k: (k, i)) if ta else pl.BlockSpec((tm, tk), lambda i, j, k: (i, k))
    b_spec = pl.BlockSpec((tn, tk), lambda i, j, k: (j, k)) if tb else pl.BlockSpec((tk, tn), lambda i, j, k: (k, j))
    o_spec = pl.BlockSpec((tm, tn), lambda i, j, k: (i, j))
    dims = (((0 if ta else 1,), (1 if tb else 0,)), ((), ()))
    has_add = add is not None
    has_extra = extra is not None

    def body(*refs):
        a_ref, b_ref = refs[0], refs[1]
        add_ref = refs[2] if has_add else None
        o_ref, acc_ref = refs[-2], refs[-1]
        k = pl.program_id(2)
        part = lax.dot_general(a_ref[...].astype(BF16), b_ref[...].astype(BF16), dims, preferred_element_type=F32)

        def finish(res):
            if has_add:
                res = res + add_ref[...].astype(F32)
            if has_extra:
                a2_ref, b2_ref = refs[2 + has_add], refs[3 + has_add]
                res = res + lax.dot_general(a2_ref[...].astype(BF16), b2_ref[...].astype(BF16), NT_DIMS,
                                            preferred_element_type=F32)
            o_ref[...] = res.astype(o_ref.dtype)

        if nk == 1:
            finish(part)
        else:
            @pl.when(k == 0)
            def _():
                acc_ref[...] = part

            @pl.when((k > 0) & (k < nk - 1))
            def _():
                acc_ref[...] += part

            @pl.when(k == nk - 1)
            def _():
                finish(acc_ref[...] + part)

    ins = [a, b] + ([add] if has_add else [])
    in_specs = [a_spec, b_spec] + ([o_spec] if has_add else [])
    if has_extra:
        k2 = extra[0].shape[1]
        ins += list(extra)
        in_specs += [pl.BlockSpec((tm, k2), lambda i, j, k: (i, 0)), pl.BlockSpec((tn, k2), lambda i, j, k: (j, 0))]
    acc_shape = (tm, tn) if nk > 1 else (8, 128)
    outs, comm_outs = _call(name, body, grid=grid, ins=ins, in_specs=in_specs,
                            out_shape=[jax.ShapeDtypeStruct((m, n), out_dtype)], out_specs=[o_spec],
                            scratch=[pltpu.VMEM(acc_shape, F32)], comm=comm)
    return outs[0], comm_outs


def _matmul_lnres(name, a_fn, a_ins, a_specs, tm, b, x, g, beta, comm=None):
    m = x.shape[0]
    k_dim, n = b.shape
    n_a = len(a_ins)
    row = lambda w: pl.BlockSpec((tm, w), lambda i: (i, 0))
    whole = lambda shape: pl.BlockSpec(shape, lambda i: (0, 0))

    def body(*refs):
        b_ref, x_ref, g_ref, beta_ref = refs[n_a:n_a + 4]
        y_ref, h_ref, hb_ref = refs[-3:]
        if a_fn is None:
            a = refs[0][...].astype(BF16)
        else:
            (a,) = a_fn(*[r[...].astype(F32) for r in refs[:n_a]])
            a = a.astype(BF16)
            refs[n_a + 4][...] = a
        y = jnp.dot(a, b_ref[...].astype(BF16), preferred_element_type=F32).astype(y_ref.dtype)
        y_ref[...] = y
        (h,) = _lnres_block(x_ref[...], y.astype(F32), g_ref[...], beta_ref[...])
        h_ref[...] = h
        hb_ref[...] = h.astype(hb_ref.dtype)

    sds = jax.ShapeDtypeStruct
    a_out = ([sds((m, k_dim), BF16)], [row(k_dim)]) if a_fn is not None else ([], [])
    return _call(name, body, grid=(m // tm,), ins=list(a_ins) + [b, x, g, beta],
                 in_specs=list(a_specs) + [whole((k_dim, n)), row(n), whole((1, n)), whole((1, n))],
                 out_shape=a_out[0] + [sds((m, n), BF16), sds((m, n), F32), sds((m, n), BF16)],
                 out_specs=a_out[1] + [row(n), row(n), row(n)], comm=comm)


SSD_STATE = (SSM_GROUPS * SSM_STATE, SSM_RPG * SSM_HEADDIM)


def _ssd_fwd(xc, dt_raw, proj, dt_bias, a_log, d_skip, norm_g, nb, nc, comm=None):
    t = xc.shape[0]
    row = lambda b, c: (b * nc + c, 0)
    par = lambda shape: pl.BlockSpec(shape, lambda b, c: (0, 0))

    def body(xc_ref, dt_ref, z_ref, dtb_ref, al_ref, ds_ref, ng_ref, e_ref, y_ref, ypre_ref, prev_ref, st_ref):
        @pl.when(pl.program_id(1) == 0)
        def _():
            st_ref[...] = jnp.zeros_like(st_ref)

        prev = st_ref[...]
        prev_ref[0, 0] = prev
        y, y_pre, new_state = _ssd_forward(xc_ref[...], dt_ref[...], z_ref[...].astype(F32), prev, dtb_ref[...],
                                           al_ref[...], ds_ref[...], ng_ref[...], e_ref[...])
        y_ref[...] = y.astype(y_ref.dtype)
        ypre_ref[...] = y_pre
        st_ref[...] = new_state

    return _call(
        "ssd_fwd", body, grid=(nb, nc), ins=[xc, dt_raw, proj, dt_bias, a_log, d_skip, norm_g, _head_expander()],
        in_specs=[pl.BlockSpec((CHUNK, SSM_CONV_DIM), row), pl.BlockSpec((CHUNK, 128), row),
                  pl.BlockSpec((CHUNK, SSM_INNER), lambda b, c: (b * nc + c, 1)),
                  par((1, 128)), par((1, 128)), par((1, SSM_INNER)), par((1, SSM_INNER)), par((128, SSM_INNER))],
        out_specs=[pl.BlockSpec((CHUNK, SSM_INNER), row), pl.BlockSpec((CHUNK, SSM_INNER), row),
                   pl.BlockSpec((1, 1) + SSD_STATE, lambda b, c: (b, c, 0, 0))],
        out_shape=[jax.ShapeDtypeStruct((t, SSM_INNER), BF16), jax.ShapeDtypeStruct((t, SSM_INNER), F32),
                   jax.ShapeDtypeStruct((nb, nc) + SSD_STATE, F32)],
        scratch=[pltpu.VMEM(SSD_STATE, F32)], comm=comm)


def _ssd_bwd(xc, dt_raw, proj, prevs, y_pre, dt_bias, a_log, d_skip, norm_g, dy, dproj, nb, nc, comm=None):
    t = xc.shape[0]
    row = lambda b, c: (b * nc + (nc - 1 - c), 0)
    par = lambda shape: pl.BlockSpec(shape, lambda b, c: (0, 0))
    z_spec = pl.BlockSpec((CHUNK, SSM_INNER), lambda b, c: (b * nc + (nc - 1 - c), 1))

    def body(xc_ref, dt_ref, z_ref, prev_ref, ypre_ref, dtb_ref, al_ref, ds_ref, ng_ref, e_ref, dy_ref, _,
             dxc_ref, ddt_ref, dz_ref, ddtb_ref, dal_ref, dds_ref, dng_ref, dst_ref):
        @pl.when(pl.program_id(1) == 0)
        def _():
            dst_ref[...] = jnp.zeros_like(dst_ref)

        @pl.when((pl.program_id(0) == 0) & (pl.program_id(1) == 0))
        def _():
            ddtb_ref[...] = jnp.zeros_like(ddtb_ref)
            dal_ref[...] = jnp.zeros_like(dal_ref)
            dds_ref[...] = jnp.zeros_like(dds_ref)
            dng_ref[...] = jnp.zeros_like(dng_ref)

        ddt, ddtb, dal, dds, dng = _ssd_backward(
            xc_ref, dt_ref[...], z_ref, prev_ref.at[0, 0], ypre_ref, dtb_ref[...], al_ref[...], ds_ref[...], ng_ref[...],
            e_ref, dy_ref, dst_ref, dxc_ref, dz_ref)
        ddt_ref[...] = ddt.astype(ddt_ref.dtype)
        ddtb_ref[...] += ddtb
        dal_ref[...] += dal
        dds_ref[...] += dds
        dng_ref[...] += dng

    return _call(
        "ssd_bwd", body, grid=(nb, nc),
        ins=[xc, dt_raw, proj, prevs, y_pre, dt_bias, a_log, d_skip, norm_g, _head_expander(), dy, dproj],
        in_specs=[pl.BlockSpec((CHUNK, SSM_CONV_DIM), row), pl.BlockSpec((CHUNK, DT_LANES), row), z_spec,
                  pl.BlockSpec((1, 1) + SSD_STATE, lambda b, c: (b, nc - 1 - c, 0, 0)),
                  pl.BlockSpec((CHUNK, SSM_INNER), row),
                  par((1, 128)), par((1, 128)), par((1, SSM_INNER)), par((1, SSM_INNER)), par((128, SSM_INNER)),
                  pl.BlockSpec((CHUNK, SSM_INNER), row), pl.BlockSpec(memory_space=pl.ANY)],
        out_specs=[pl.BlockSpec((CHUNK, SSM_CONV_DIM), row), pl.BlockSpec((CHUNK, DT_LANES), row), z_spec,
                   par((1, 128)), par((1, 128)), par((1, 128)), par((1, SSM_INNER))],
        out_shape=[jax.ShapeDtypeStruct((t, SSM_CONV_DIM), F32), jax.ShapeDtypeStruct((t, DT_LANES), BF16),
                   jax.ShapeDtypeStruct(dproj.shape, dproj.dtype), jax.ShapeDtypeStruct((1, 128), F32),
                   jax.ShapeDtypeStruct((1, 128), F32), jax.ShapeDtypeStruct((1, 128), F32),
                   jax.ShapeDtypeStruct((1, SSM_INNER), F32)],
        scratch=[pltpu.VMEM(SSD_STATE, F32)], comm=comm, aliases={11: 2})


def _loss_head(y, target):
    t, d = y.shape
    tm = _pick_tile(t, (256,))
    blk = pl.BlockSpec((tm, d), lambda i: (i, 0))

    def body(y_ref, t_ref, loss_ref, dy_ref):
        err = y_ref[...] - t_ref[...]
        dy_ref[...] = err * (1.0 / d)

        @pl.when(pl.program_id(0) == 0)
        def _():
            loss_ref[...] = jnp.zeros_like(loss_ref)

        loss_ref[...] += 0.5 * jnp.sum(jnp.mean(err * err, axis=-1, keepdims=True), axis=0, keepdims=True)

    return _call("loss_head", body, grid=(t // tm,), ins=[y, target], in_specs=[blk, blk],
                 out_specs=[pl.BlockSpec((1, 1), lambda i: (0, 0)), blk],
                 out_shape=[jax.ShapeDtypeStruct((1, 1), F32), jax.ShapeDtypeStruct((t, d), F32)])[0]


def _adamw_math(g, w, m, v):
    m_new = ADAM_B1 * m + (1.0 - ADAM_B1) * g
    v_new = ADAM_B2 * v + (1.0 - ADAM_B2) * jnp.square(g)
    m_hat = m_new / (1.0 - ADAM_B1 ** ADAM_STEP)
    v_hat = v_new / (1.0 - ADAM_B2 ** ADAM_STEP)
    delta = -ADAM_LR * (m_hat / (jnp.sqrt(v_hat) + ADAM_EPS) + ADAM_WD * w)
    return delta, m_new, v_new


def _adamw_sharded(name, parts, w, m, v, comm=None):
    _, a, b = w.shape
    tr = _pick_tile(a, (128,))
    nt = a // tr
    part_specs = [pl.BlockSpec((N_DEV, tr, b),
                               (lambda l, i, _k=k: (0, jnp.where(l == _k, i, jnp.where(l > _k, nt - 1, 0)), 0)))
                  for k in range(DEPTH)]
    blk = pl.BlockSpec((1, tr, b), lambda l, i: (l, i, 0))

    def body(*refs):
        p_refs = refs[:DEPTH]
        w_ref, m_ref, v_ref, g_out, d_out, m_out, v_out = refs[DEPTH:]
        for k in range(DEPTH):
            @pl.when(pl.program_id(0) == k)
            def _(p_ref=p_refs[k]):
                g = p_ref[0].astype(F32)
                for p in range(1, N_DEV):
                    g = g + p_ref[p].astype(F32)
                delta, m_new, v_new = _adamw_math(g, w_ref[0], m_ref[0], v_ref[0])
                g_out[0] = g
                d_out[0] = delta
                m_out[0] = m_new
                v_out[0] = v_new

    return _call(name, body, grid=(DEPTH, nt), ins=list(parts) + [w, m, v], in_specs=part_specs + [blk, blk, blk],
                 out_specs=[blk] * 4, out_shape=[jax.ShapeDtypeStruct(w.shape, F32)] * 4, comm=comm)


def _adamw_small(name, g, w, m, v):
    full = pl.BlockSpec(w.shape, lambda i: (0, 0))

    def body(g_ref, w_ref, m_ref, v_ref, d_out, m_out, v_out):
        delta, m_new, v_new = _adamw_math(g_ref[...], w_ref[...], m_ref[...], v_ref[...])
        d_out[...] = delta
        m_out[...] = m_new
        v_out[...] = v_new

    return _call(name, body, grid=(1,), ins=[g, w, m, v], in_specs=[full] * 4, out_specs=[full] * 3,
                 out_shape=[jax.ShapeDtypeStruct(w.shape, F32)] * 3)[0]


def _sum_parts(name, parts):
    n_parts, rows, cols = parts.shape
    tr = _pick_tile(rows, (512, 256, 128, 64, 32, 16, 8))

    def body(p_ref, o_ref):
        acc = p_ref[0]
        for p in range(1, n_parts):
            acc = acc + p_ref[p]
        o_ref[...] = acc

    return _call(name, body, grid=(rows // tr,), ins=[parts],
                 in_specs=[pl.BlockSpec((n_parts, tr, cols), lambda i: (0, i, 0))],
                 out_specs=[pl.BlockSpec((tr, cols), lambda i: (i, 0))],
                 out_shape=[jax.ShapeDtypeStruct((rows, cols), parts.dtype)])[0][0]


W_IN_SHARD = IN_COLS // N_DEV


def _pack_w_in(gathered):
    r = gathered.shape[1]
    tr = _pick_tile(r, (128,))

    def body(g_ref, main_ref, dt_ref):
        w = jnp.concatenate([g_ref[j].astype(F32) for j in range(N_DEV)], axis=1)
        main_ref[...] = jnp.concatenate([w[:, :XBC_COL], w[:, GA_COL:], w[:, XBC_COL:DT_COL]],
                                        axis=1).astype(main_ref.dtype)
        dt_ref[...] = jnp.concatenate([w[:, DT_COL:GA_COL], jnp.zeros((tr, DT_LANES - SSM_HEADS), F32)],
                                      axis=1).astype(dt_ref.dtype)

    return _call("pack_w_in", body, grid=(r // tr,), ins=[gathered],
                 in_specs=[pl.BlockSpec((N_DEV, tr, W_IN_SHARD), lambda i: (0, i, 0))],
                 out_specs=[pl.BlockSpec((tr, P_COLS), lambda i: (i, 0)), pl.BlockSpec((tr, DT_LANES), lambda i: (i, 0))],
                 out_shape=[jax.ShapeDtypeStruct((r, P_COLS), gathered.dtype),
                            jax.ShapeDtypeStruct((r, DT_LANES), gathered.dtype)])[0]


def _unpack_w_in(main, dt):
    r = main.shape[0]
    tr = _pick_tile(r, (128,))

    def body(main_ref, dt_ref, o_ref):
        main = main_ref[...].astype(F32)
        w = jnp.concatenate([main[:, :P_GATE], main[:, P_XBC:], dt_ref[...].astype(F32)[:, :SSM_HEADS],
                             main[:, P_GATE:P_XBC]], axis=1)
        for j in range(N_DEV):
            o_ref[j] = w[:, j * W_IN_SHARD:(j + 1) * W_IN_SHARD].astype(o_ref.dtype)

    return _call("unpack_w_in", body, grid=(r // tr,), ins=[main, dt],
                 in_specs=[pl.BlockSpec((tr, P_COLS), lambda i: (i, 0)), pl.BlockSpec((tr, DT_LANES), lambda i: (i, 0))],
                 out_specs=[pl.BlockSpec((N_DEV, tr, W_IN_SHARD), lambda i: (0, i, 0))],
                 out_shape=[jax.ShapeDtypeStruct((N_DEV, r, W_IN_SHARD), main.dtype)])[0][0]


def _pad_heads(v):
    return jnp.pad(v, (0, 128 - SSM_HEADS)).reshape(1, 128)


def _run_step(x, mem, target, small, ex):
    nb, s, d = x.shape
    t = nb * s
    nc = s // CHUNK
    rows = _pick_tile(t, (256,))
    rows_wide = _pick_tile(t, (512, 256))
    tq = _pick_tile(s, (512, 256))
    vec = lambda a: a.reshape(1, -1)
    full1 = lambda shape: pl.BlockSpec(shape, lambda i: (0,) * len(shape))
    row1 = lambda tm, w: pl.BlockSpec((tm, w), lambda i: (i, 0))
    sds = jax.ShapeDtypeStruct

    def mm(call, l, a, b, **kw):
        comm = ex.before(call, l)
        out, comm_outs = _matmul(call, a, b, comm=comm, **kw)
        if comm is not None:
            ex.after(call, l, comm_outs)
        return out

    def stage_bwd(call, l, *args, **kw):
        comm = ex.before(call, l)
        outs, comm_outs = _stage_bwd(call, *args, comm=comm, **kw)
        if comm is not None:
            ex.after(call, l, comm_outs)
        return outs

    def stage_fwd(call, l, *args):
        comm = ex.before(call, l)
        outs, comm_outs = _stage_fwd(call, *args, comm=comm)
        if comm is not None:
            ex.after(call, l, comm_outs)
        return outs

    mem_specs = [row1(256, d), full1((1, d)), full1((1, d))]
    mem_ins = [mem.reshape(nb * MEM_LEN, d), vec(small["mem_ln_g"]), vec(small["mem_ln_b"])]
    (mem_n,) = stage_fwd("memln_fwd", 0, _memln_block, (nb * MEM_LEN // 256,), mem_ins, mem_specs,
                          [sds((nb * MEM_LEN, d), BF16)], [row1(256, d)])

    h = x.reshape(t, d)
    h_bf = h.astype(BF16)
    ln_specs = [row1(rows_wide, d), row1(rows_wide, d), full1((1, d)), full1((1, d))]
    saved = []
    for l in range(DEPTH):
        sv = {"h_bf": h_bf}
        w_p, w_dt = ex.weight("w_in", l)
        proj = mm("mm_in", l, h_bf, w_p, out_dtype=BF16)
        dt_raw = mm("mm_dt", l, h_bf, w_dt)
        sv["proj"] = proj
        sgu_ins = [proj, vec(small["sg_ln_g"][l]), vec(small["sg_ln_b"][l]), small["sg_w"][l], small["sg_b"][l].T]
        sgu_specs = [pl.BlockSpec((rows, 2 * d), lambda i: (i, 0)), full1((1, d)), full1((1, d)),
                     full1((SG_GROUPS, CHUNK, CHUNK)), full1((CHUNK, SG_GROUPS))]
        (a_out,) = stage_fwd("sgu_fwd", l, _sgu_block, (t // rows,), sgu_ins, sgu_specs, [sds((t, d), BF16)],
                              [row1(rows, d)])
        sv["sgu"] = (sgu_ins, sgu_specs)
        sv["a_out"] = a_out
        cw = 256
        conv_ins = [proj, small["conv_w"][l], vec(small["conv_b"][l])]
        conv_specs = [pl.BlockSpec((s, cw), lambda j, b: (b, P_XBC // cw + j)),
                      pl.BlockSpec((SSM_CONV, cw), lambda j, b: (0, j)), pl.BlockSpec((1, cw), lambda j, b: (0, j))]
        conv_out_spec = pl.BlockSpec((s, cw), lambda j, b: (b, j))
        (xc,) = stage_fwd("conv_fwd", l, _conv_block, (SSM_CONV_DIM // cw, nb), conv_ins, conv_specs,
                           [sds((t, SSM_CONV_DIM), F32)], [conv_out_spec])
        sv["conv"] = (conv_ins, conv_specs, conv_out_spec)
        ssd_par = [_pad_heads(small["dt_bias"][l]), _pad_heads(small["a_log"][l]),
                   vec(jnp.repeat(small["d_skip"][l], SSM_HEADDIM)), vec(small["ssm_norm_g"][l])]
        comm = ex.before("ssd_fwd", l)
        (y_ssd, y_pre, prevs), comm_outs = _ssd_fwd(xc, dt_raw, proj, *ssd_par, nb, nc, comm=comm)
        if comm is not None:
            ex.after("ssd_fwd", l, comm_outs)
        sv["ssd"] = (xc, dt_raw, prevs, y_pre, ssd_par)
        sv["y_ssd"] = y_ssd
        br_a = mm("mm_sq", l, a_out, ex.weight("p_a", l), out_dtype=BF16)
        br_b = mm("mm_pb", l, y_ssd, ex.weight("p_b", l), out_dtype=BF16)
        merge_ins = [proj, br_a, br_b]
        merge_out_spec = row1(rows_wide, d)
        merge_specs = [pl.BlockSpec((rows_wide, 2 * d), lambda i: (i, P_GATE // (2 * d))), merge_out_spec, merge_out_spec]
        sv["merge"] = (merge_ins, merge_specs, merge_out_spec)
        ln_par = [(vec(small["ln_g"][l, k]), vec(small["ln_b"][l, k])) for k in range(3)]

        def fused(call, a_fn, a_ins, a_specs, tm, w, x_in, par):
            comm = ex.before(call, l)
            outs, comm_outs = _matmul_lnres(call, a_fn, a_ins, a_specs, tm, w, x_in, *par, comm=comm)
            if comm is not None:
                ex.after(call, l, comm_outs)
            return outs

        merged, y1, h1, h1_bf = fused("mm_mix_ln", _merge_block, merge_ins, merge_specs, rows_wide,
                                      ex.weight("w_mix_o", l), h, ln_par[0])
        sv["merged"] = merged
        sv["ln1"] = [h, y1, *ln_par[0]]
        q = mm("mm_sq", l, h1_bf, ex.weight("w_xq", l), out_dtype=BF16)
        kv = mm("mm_kv", l, mem_n, ex.weight("w_xkv", l), out_dtype=BF16)
        attn_ins = [q, kv]
        attn_out_spec = pl.BlockSpec((tq, d), lambda b, i: (b * (s // tq) + i, 0))
        attn_specs = [attn_out_spec, pl.BlockSpec((MEM_LEN, 2 * d), lambda b, i: (b, 0))]
        (o,) = stage_fwd("attn_fwd", l, _attn_block, (nb, s // tq), attn_ins, attn_specs, [sds((t, d), BF16)],
                          [attn_out_spec])
        sv["attn"] = (attn_ins, attn_specs, attn_out_spec)
        sv["o"] = o
        sv["h1_bf"] = h1_bf
        y2, h2, h2_bf = fused("mm_xo_ln", None, [o], [row1(rows_wide, d)], rows_wide, ex.weight("w_xo", l), h1, ln_par[1])
        sv["ln2"] = [h1, y2, *ln_par[1]]
        sv["h2_bf"] = h2_bf
        gu = mm("mm_ffn_in", l, h2_bf, ex.weight("w_ffn_in", l), out_dtype=BF16)
        act, y3, h3, h3_bf = fused("mm_ffn_out_ln", _swiglu_block, [gu], [row1(rows, 2 * FFN_HIDDEN)], rows,
                                   ex.weight("w_ffn_out", l), h2, ln_par[2])
        sv["gu"] = gu
        sv["act"] = act
        sv["ln3"] = [h2, y3, *ln_par[2]]
        h, h_bf = h3, h3_bf
        saved.append(sv)

    loss, dh = _loss_head(h, target.reshape(t, d))

    g_small = {n: [None] * DEPTH for n in SMALL_REP + SMALL_SH if n not in ("mem_ln_g", "mem_ln_b")}
    dmem_n = []
    ln_grads = [(0, (), F32), (1, (), BF16), (2, (0,), F32), (3, (0,), F32)]
    for l in reversed(range(DEPTH)):
        sv = saved[l]
        dln_g, dln_b = [None] * 3, [None] * 3
        dres, dy3, dln_g[2], dln_b[2] = stage_bwd("lnres_bwd", l, _lnres_block, (t // rows_wide,), sv["ln3"], ln_specs,
                                                  [(dh,)], [row1(rows_wide, d)], ln_grads)
        ex.grad("w_ffn_out", l, mm("mm_ffn_out_dw", l, sv["act"], dy3, ta=True, out_dtype=BF16))
        (dgu,) = stage_bwd("swiglu_bwd", l, _swiglu_block, (t // rows,), [sv["gu"]], [row1(rows, 2 * FFN_HIDDEN)],
                           [], [], [(0, (), BF16)], ct_product=(dy3, row1(rows, d), ex.weight("w_ffn_out", l)))
        ex.grad("w_ffn_in", l, mm("mm_ffn_in_dw", l, sv["h2_bf"], dgu, ta=True, out_dtype=BF16))
        dh2 = mm("mm_ffn_in_dx", l, dgu, ex.weight("w_ffn_in", l), tb=True, add=dres)
        dres, dy2, dln_g[1], dln_b[1] = stage_bwd("lnres_bwd", l, _lnres_block, (t // rows_wide,), sv["ln2"], ln_specs,
                                                  [(dh2,)], [row1(rows_wide, d)], ln_grads)
        ex.grad("w_xo", l, mm("mm_sq_dw", l, sv["o"], dy2, ta=True, out_dtype=BF16))
        do = mm("mm_sq_dx", l, dy2, ex.weight("w_xo", l), tb=True, out_dtype=BF16)
        attn_ins, attn_specs, attn_out_spec = sv["attn"]
        dq, dkv = stage_bwd("attn_bwd", l, _attn_block, (nb, s // tq), attn_ins, attn_specs, [(do,)], [attn_out_spec],
                            [(0, (), BF16), (1, (1,), F32)])
        ex.grad("w_xq", l, mm("mm_sq_dw", l, sv["h1_bf"], dq, ta=True, out_dtype=BF16))
        dh1 = mm("mm_sq_dx", l, dq, ex.weight("w_xq", l), tb=True, add=dres)
        ex.grad("w_xkv", l, mm("mm_kv_dw", l, mem_n, dkv, ta=True, out_dtype=BF16))
        dmem_n.append(mm("mm_kv_dx", l, dkv, ex.weight("w_xkv", l), tb=True))
        dres, dy1, dln_g[0], dln_b[0] = stage_bwd("lnres_bwd", l, _lnres_block, (t // rows_wide,), sv["ln1"], ln_specs,
                                                  [(dh1,)], [row1(rows_wide, d)], ln_grads)
        g_small["ln_g"][l] = jnp.concatenate(dln_g, axis=0)
        g_small["ln_b"][l] = jnp.concatenate(dln_b, axis=0)
        ex.grad("w_mix_o", l, mm("mm_sq_dw", l, sv["merged"], dy1, ta=True, out_dtype=BF16))
        merge_ins, merge_specs, merge_out_spec = sv["merge"]
        dproj, dbr_a, dbr_b = stage_bwd("merge_bwd", l, _merge_block, (t // rows_wide,), merge_ins, merge_specs, [], [],
                                        [(0, (), BF16, ((t, P_COLS), merge_specs[0])), (1, (), BF16), (2, (), BF16)],
                                        ct_product=(dy1, merge_out_spec, ex.weight("w_mix_o", l)))
        ex.grad("p_a", l, mm("mm_sq_dw", l, sv["a_out"], dbr_a, ta=True, out_dtype=BF16))
        da_out = mm("mm_sq_dx", l, dbr_a, ex.weight("p_a", l), tb=True, out_dtype=BF16)
        ex.grad("p_b", l, mm("mm_pb_dw", l, sv["y_ssd"], dbr_b, ta=True, out_dtype=BF16))
        dy_ssd = mm("mm_pb_dx", l, dbr_b, ex.weight("p_b", l), tb=True, out_dtype=BF16)
        sgu_ins, sgu_specs = sv["sgu"]
        dproj, dsg_ln_g, dsg_ln_b, dsg_w, dsg_b = stage_bwd(
            "sgu_bwd", l, _sgu_block, (t // CHUNK,), sgu_ins, [row1(CHUNK, 2 * d)] + sgu_specs[1:], [(da_out,)],
            [row1(CHUNK, d)],
            [(0, (), BF16, ((t, P_COLS), row1(CHUNK, 2 * d)), dproj), (1, (0,), F32), (2, (0,), F32), (3, (0,), F32),
             (4, (0,), F32)])
        g_small["sg_ln_g"][l], g_small["sg_ln_b"][l], g_small["sg_w"][l], g_small["sg_b"][l] = (
            dsg_ln_g[0], dsg_ln_b[0], dsg_w, dsg_b.T)
        xc, dt_raw, prevs, y_pre, ssd_par = sv["ssd"]
        comm = ex.before("ssd_bwd", l)
        (dxc, ddt, dproj, ddtb, dal, dds, dng), comm_outs = _ssd_bwd(xc, dt_raw, sv["proj"], prevs, y_pre, *ssd_par,
                                                                     dy_ssd, dproj, nb, nc, comm=comm)
        if comm is not None:
            ex.after("ssd_bwd", l, comm_outs)
        g_small["dt_bias"][l], g_small["a_log"][l], g_small["d_skip"][l] = (
            ddtb[0, :SSM_HEADS], dal[0, :SSM_HEADS], dds[0, :SSM_HEADS])
        g_small["ssm_norm_g"][l] = dng[0]
        conv_ins, conv_specs, conv_out_spec = sv["conv"]
        dproj, dconv_w, dconv_b = stage_bwd("conv_bwd", l, _conv_block, (SSM_CONV_DIM // 256, nb), conv_ins, conv_specs,
                                            [(dxc,)], [conv_out_spec],
                                            [(0, (), BF16, ((t, P_COLS), conv_specs[0]), dproj), (1, (1,), F32),
                                             (2, (1,), F32)])
        g_small["conv_w"][l], g_small["conv_b"][l] = dconv_w, dconv_b[0]
        if l == 0:
            dmg, dmb = stage_bwd("memln_bwd", l, _memln_block, (nb * MEM_LEN // 256,), mem_ins, mem_specs,
                                 [tuple(dmem_n)], [row1(256, d)], [(1, (0,), F32), (2, (0,), F32)])
            done = {n: jnp.stack(g, axis=0) for n, g in g_small.items()}
            done["mem_ln_g"], done["mem_ln_b"] = dmg[0], dmb[0]
            ex.small_grads(done)
        w_p, w_dt = ex.weight("w_in", l)
        g_dt = mm("mm_dt_dw", l, sv["h_bf"], ddt, ta=True, out_dtype=BF16)
        ex.grad("w_in", l, _unpack_w_in(mm("mm_in_dw", l, sv["h_bf"], dproj, ta=True, out_dtype=BF16), g_dt))
        dh = mm("mm_in_dx", l, dproj, w_p, tb=True, add=dres, extra=(ddt, w_dt))

    return loss, dh.reshape(nb, s, d)


def _pack_flat(arrays, rows):
    flat = jnp.concatenate([a.reshape(-1) for a in arrays])
    return jnp.pad(flat, (0, rows * 128 - flat.shape[0])).reshape(rows, 128)


def _unpack_flat(packed, shapes):
    lead = packed.shape[:-2]
    flat = packed.reshape(lead + (-1,))
    out, pos = [], 0
    for shape in shapes:
        n = math.prod(shape)
        out.append(flat[..., pos:pos + n].reshape(lead + tuple(shape)))
        pos += n
    return out


def _small_rows(n_elems):
    return -(-n_elems // (128 * SMALL_ROW_TILE)) * SMALL_ROW_TILE


def _from_shards(name, gathered):
    _, a, b = gathered.shape
    if name == "w_in":
        return tuple(_pack_w_in(gathered))
    if name in BIG_COL_SHARDED:
        return _join_columns(gathered)
    return gathered.reshape(N_DEV * a, b)


def _to_shards(name, g):
    if name == "w_in":
        return g
    if name in BIG_COL_SHARDED:
        return _split_columns(g)
    a, b = g.shape
    return g.reshape(N_DEV, a // N_DEV, b)


def _join_columns(gathered):
    _, r, b = gathered.shape
    tr = _pick_tile(r, (128,))

    def body(g_ref, o_ref):
        o_ref[...] = jnp.concatenate([g_ref[j].astype(F32) for j in range(N_DEV)], axis=1).astype(o_ref.dtype)

    return _call("join_columns", body, grid=(r // tr,), ins=[gathered],
                 in_specs=[pl.BlockSpec((N_DEV, tr, b), lambda i: (0, i, 0))],
                 out_specs=[pl.BlockSpec((tr, N_DEV * b), lambda i: (i, 0))],
                 out_shape=[jax.ShapeDtypeStruct((r, N_DEV * b), gathered.dtype)])[0][0]


def _split_columns(full):
    r, nb = full.shape
    b = nb // N_DEV
    tr = _pick_tile(r, (128,))

    def body(f_ref, o_ref):
        w = f_ref[...].astype(F32)
        for j in range(N_DEV):
            o_ref[j] = w[:, j * b:(j + 1) * b].astype(o_ref.dtype)

    return _call("split_columns", body, grid=(r // tr,), ins=[full],
                 in_specs=[pl.BlockSpec((tr, nb), lambda i: (i, 0))],
                 out_specs=[pl.BlockSpec((N_DEV, tr, b), lambda i: (0, i, 0))],
                 out_shape=[jax.ShapeDtypeStruct((N_DEV, r, b), full.dtype)])[0][0]


class _MeshExchange:
    def __init__(self, shards_bf16, first):
        self.shards = shards_bf16
        self.full = dict(first)
        self.pieces = {}
        self.grads = {}
        self.to_send = {}
        self.received = {}
        self.small = None
        self.small_gathered = None

    def weight(self, name, l):
        if (name, l) not in self.full:
            got = jnp.concatenate([self.pieces[(name, l, q)] for q in range(W_IN_PIECES)], axis=1)
            self.full[(name, l)] = _from_shards(name, got)
        return self.full[(name, l)]

    def grad(self, name, l, g):
        self.grads[(name, l)] = g

    def small_grads(self, done):
        self.small = done

    def partial_sums(self, name, l):
        if (name, l, None) in self.received:
            return self.received[(name, l, None)]
        return jnp.concatenate([self.received[(name, l, q)] for q in range(W_IN_PIECES)], axis=1)

    def _slices(self, name, l, piece):
        n_rows = D_MODEL // W_IN_PIECES
        key, rows = (name, l), None if piece is None else (piece * n_rows, n_rows)
        if key not in self.to_send:
            self.to_send[key] = _to_shards(name, self.grads[key])
        return self.to_send[key], rows

    def before(self, call, l):
        comm = _Comm()
        for name, layer, piece in GATHER_PLAN.get((call, l), ()):
            n_rows = D_MODEL // W_IN_PIECES
            comm.gathers.append((self.shards[name], layer, None if piece is None else (piece * n_rows, n_rows)))
        if (call, l) == SMALL_GATHER_CALL:
            names = SMALL_REP + SMALL_SH
            rows = _small_rows(sum(math.prod(self.small[n].shape) for n in names))
            comm.gathers.append((_pack_flat([self.small[n] for n in names], rows), None, None))
        for name, layer, piece in SCATTER_PLAN.get((call, l), ()):
            comm.scatters.append(self._slices(name, layer, piece))
        return comm if comm.gathers or comm.scatters else None

    def after(self, call, l, outs):
        gathers = list(GATHER_PLAN.get((call, l), ()))
        for (name, layer, piece), out in zip(gathers, outs):
            if piece is None:
                self.full[(name, layer)] = _from_shards(name, out)
            else:
                self.pieces[(name, layer, piece)] = out
        outs = outs[len(gathers):]
        if (call, l) == SMALL_GATHER_CALL:
            self.small_gathered = outs[0]
            outs = outs[1:]
        for item, out in zip(SCATTER_PLAN.get((call, l), ()), outs):
            self.received[item] = out


def kernel(x, mem, mem_ln_g, mem_ln_b, w_in, sg_ln_g, sg_ln_b, sg_w, sg_b, conv_w, conv_b, dt_bias, a_log, d_skip, ssm_norm_g, p_a, p_b, w_mix_o, w_xq, w_xkv, w_xo, w_ffn_in, w_ffn_out, ln_g, ln_b, loss_target, m_mem_ln_g, m_mem_ln_b, m_w_in, m_sg_ln_g, m_sg_ln_b, m_sg_w, m_sg_b, m_conv_w, m_conv_b, m_dt_bias, m_a_log, m_d_skip, m_ssm_norm_g, m_p_a, m_p_b, m_w_mix_o, m_w_xq, m_w_xkv, m_w_xo, m_w_ffn_in, m_w_ffn_out, m_ln_g, m_ln_b, v_mem_ln_g, v_mem_ln_b, v_w_in, v_sg_ln_g, v_sg_ln_b, v_sg_w, v_sg_b, v_conv_w, v_conv_b, v_dt_bias, v_a_log, v_d_skip, v_ssm_norm_g, v_p_a, v_p_b, v_w_mix_o, v_w_xq, v_w_xkv, v_w_xo, v_w_ffn_in, v_w_ffn_out, v_ln_g, v_ln_b):
    args = dict(locals())
    w = {n: args[n] for n in WEIGHTS}
    m = {n: args["m_" + n] for n in WEIGHTS}
    v = {n: args["v_" + n] for n in WEIGHTS}
    me = 4 * lax.axis_index("x") + 2 * lax.axis_index("y") + lax.axis_index("c")

    shards = {n: w[n].astype(BF16) for n in BIG}
    sh_shapes = [w[n].shape for n in SMALL_SH]
    first = _Comm()
    first.gathers.append((shards["w_in"], 0, None))
    first.gathers.append((_pack_flat([w[n] for n in SMALL_SH], _small_rows(sum(math.prod(s) for s in sh_shapes))), None,
                          None))
    w_in0, small_sh = _comm_only("gather_first", first)
    small = {n: w[n] for n in SMALL_REP}
    for n, sh in zip(SMALL_SH, _unpack_flat(small_sh, sh_shapes)):
        small[n] = sh.transpose(1, 2, 0, 3).reshape(sh.shape[1], sh.shape[2], N_DEV * sh.shape[3])

    ex = _MeshExchange(shards, {("w_in", 0): _from_shards("w_in", w_in0)})
    loss, grad_x = _run_step(x, mem, loss_target, small, ex)
    loss = lax.psum(loss[0, 0], ("x", "y", "c"))

    out = {}
    for n in BIG[1:] + BIG[:1]:
        comm = ex.before("adamw_" + n, 0)
        out[n], comm_outs = _adamw_sharded("adamw_" + n, [ex.partial_sums(n, l) for l in range(DEPTH)], w[n], m[n], v[n],
                                           comm=comm)
        if comm is not None:
            ex.after("adamw_" + n, 0, comm_outs)
    names = SMALL_REP + SMALL_SH
    g_small = dict(zip(names, _unpack_flat(_sum_parts("sum_small_grads", ex.small_gathered),
                                           [ex.small[n].shape for n in names])))
    for n in names:
        g = g_small[n]
        if n in SMALL_SH:
            width = w[n].shape[-1]
            g = lax.dynamic_slice_in_dim(g, me * width, width, axis=-1)
        two_d = (-1, w[n].shape[-1])
        res = _adamw_small("adamw_" + n, g.reshape(two_d), w[n].reshape(two_d), m[n].reshape(two_d), v[n].reshape(two_d))
        out[n] = [g] + [r.reshape(w[n].shape) for r in res]

    results = []
    for k in range(4):
        results.extend(out[n][k] for n in WEIGHTS)
    return (loss, grad_x, *results)
```

```python
import functools
import math

import jax
import jax.numpy as jnp
from jax import lax
from jax.experimental import pallas as pl
from jax.experimental.pallas import tpu as pltpu

F32 = jnp.float32
BF16 = jnp.bfloat16
HIGHEST = lax.Precision.HIGHEST

N_DEV = 8
D_MODEL = 1024
DEPTH = 2
MEM_LEN = 256
CHUNK = 128
SG_GROUPS = 8
SSM_INNER = 2048
SSM_HEADDIM = 64
SSM_HEADS = 32
SSM_STATE = 128
SSM_GROUPS = 4
SSM_RPG = 8
SSM_CONV = 4
SSM_CONV_DIM = 3072
X_HEADS = 4
X_HEADDIM = 256
FFN_HIDDEN = 2816
ALPHA = float((2 * DEPTH) ** 0.25)
LN_EPS = 1e-5
RMS_EPS = 1e-5
XBC_COL = 4096
DT_COL = 7168
GA_COL = 7200
IN_COLS = 9248
P_GATE = 4096
P_XBC = 6144
P_COLS = 9216
DT_LANES = 128

ADAM_LR = 0.001
ADAM_B1 = 0.9
ADAM_B2 = 0.999
ADAM_EPS = 1e-08
ADAM_WD = 0.01
ADAM_STEP = 10

VMEM_LIMIT = 48 * 1024 * 1024
SMALL_ROW_TILE = 256

BIG = ("w_in", "p_a", "p_b", "w_mix_o", "w_xq", "w_xkv", "w_xo", "w_ffn_in", "w_ffn_out")
BIG_COL_SHARDED = ("w_in", "w_xkv", "w_ffn_in")
SMALL_REP = ("mem_ln_g", "mem_ln_b", "sg_ln_g", "sg_ln_b", "sg_w", "sg_b", "conv_b", "dt_bias", "a_log", "d_skip",
             "ssm_norm_g")
SMALL_SH = ("conv_w", "ln_g", "ln_b")
WEIGHTS = ("mem_ln_g", "mem_ln_b", "w_in", "sg_ln_g", "sg_ln_b", "sg_w", "sg_b", "conv_w", "conv_b", "dt_bias", "a_log",
           "d_skip", "ssm_norm_g", "p_a", "p_b", "w_mix_o", "w_xq", "w_xkv", "w_xo", "w_ffn_in", "w_ffn_out", "ln_g", "ln_b")

W_IN_PIECES = 4
GATHER_PLAN = {("sgu_fwd", 0): [("w_in", 1, 0)], ("conv_fwd", 0): [("w_in", 1, 1)],
               ("mm_ffn_in", 0): [("w_in", 1, 2), ("w_ffn_out", 0, None)], ("mm_ffn_out_ln", 0): [("w_in", 1, 3)],
               ("conv_fwd", 1): [("w_ffn_out", 1, None)]}
SCATTER_PLAN = {("mm_ffn_in_dw", 0): [("w_in", 1, 2)], ("mm_ffn_in_dx", 0): [("w_in", 1, 3)],
                ("adamw_w_ffn_in", 0): [("w_in", 0, 2)], ("adamw_w_xkv", 0): [("w_in", 0, 3)]}
for _l in range(DEPTH):
    GATHER_PLAN[("mm_in", _l)] = [(n, _l, None) for n in ("p_a", "p_b", "w_mix_o", "w_xq", "w_xkv", "w_xo")]
    GATHER_PLAN[("ssd_fwd", _l)] = [("w_ffn_in", _l, None)]
    SCATTER_PLAN[("swiglu_bwd", _l)] = [("w_ffn_out", _l, None)]
    SCATTER_PLAN[("sgu_bwd", _l)] = [("w_mix_o", _l, None), ("p_a", _l, None), ("w_xo", _l, None)]
    SCATTER_PLAN[("ssd_bwd", _l)] = [("w_ffn_in", _l, None), ("w_xkv", _l, None)]
    SCATTER_PLAN[("conv_bwd", _l)] = [("p_b", _l, None), ("w_xq", _l, None)]
    SCATTER_PLAN[("mm_in_dx", _l)] = [("w_in", _l, 0), ("w_in", _l, 1)]
SMALL_GATHER_CALL = ("mm_in_dw", 0)


def _layer_norm(x, g, b):
    mu = jnp.mean(x, axis=-1, keepdims=True)
    xc = x - mu
    var = jnp.mean(xc * xc, axis=-1, keepdims=True)
    return xc * lax.rsqrt(var + LN_EPS) * g + b


def _gelu(x):
    return 0.5 * x * (1.0 + lax.erf(x * (1.0 / math.sqrt(2.0))))


def _sigmoid(x):
    return 0.5 * jnp.tanh(0.5 * x) + 0.5


def _silu(x):
    return x * _sigmoid(x)


def _softplus(x):
    return jnp.maximum(x, 0.0) + jnp.log1p(jnp.exp(-jnp.abs(x)))


def _causal_mask():
    r = lax.broadcasted_iota(jnp.int32, (CHUNK, CHUNK), 0)
    c = lax.broadcasted_iota(jnp.int32, (CHUNK, CHUNK), 1)
    return r >= c


def _sgu_block(uv, ln_g, ln_b, w, sb):
    gu = _gelu(uv[:, :D_MODEL])
    vn = _layer_norm(_gelu(uv[:, D_MODEL:]), ln_g, ln_b).astype(BF16)
    causal = _causal_mask()
    width = D_MODEL // SG_GROUPS
    wgs = [jnp.where(causal, w[g], 0.0).astype(BF16) for g in range(SG_GROUPS)]
    chunks = []
    for c in range(uv.shape[0] // CHUNK):
        vc = vn[c * CHUNK:(c + 1) * CHUNK, :]
        chunks.append(jnp.concatenate(
            [jnp.dot(wgs[g], vc[:, g * width:(g + 1) * width], preferred_element_type=F32) + sb[:, g:g + 1]
             for g in range(SG_GROUPS)], axis=1))
    return (gu * jnp.concatenate(chunks, axis=0),)


GROUP_W = SSM_RPG * SSM_HEADDIM
NT_DIMS = (((1,), (1,)), ((), ()))
TN_DIMS = (((0,), (0,)), ((), ()))


def _mxu(a, b, dims=(((1,), (0,)), ((), ()))):
    return lax.dot_general(a.astype(BF16), b.astype(BF16), dims, preferred_element_type=F32)


def _head_expander():
    return (jnp.arange(SSM_INNER)[None, :] // SSM_HEADDIM == jnp.arange(128)[:, None]).astype(BF16)


def _bf16_terms(x, n):
    terms = []
    for _ in range(n):
        t = x.astype(BF16)
        terms.append(t)
        x = x - t.astype(F32)
    return terms


def _expand_heads(q, e):
    return sum(jnp.dot(t, e, preferred_element_type=F32) for t in _bf16_terms(q, 2))


def _reduce_heads(v, e, terms=2):
    return sum(lax.dot_general(t, e, NT_DIMS, preferred_element_type=F32) for t in _bf16_terms(v, terms))


def _reduce_heads_of_column_sums(v, e):
    sums = jnp.broadcast_to(jnp.sum(v, axis=0, keepdims=True), (8, v.shape[1]))
    return _reduce_heads(sums, e)[0:1, :]


def _ssd_common(xc, dtraw, dt_bias, a_log, e):
    xs = xc[:, :SSM_INNER]
    pre = dtraw + dt_bias
    dt = _softplus(pre)
    a = -jnp.exp(a_log)
    r_i = lax.broadcasted_iota(jnp.int32, (CHUNK, CHUNK), 0)
    c_i = lax.broadcasted_iota(jnp.int32, (CHUNK, CHUNK), 1)
    tril = jnp.where(r_i >= c_i, 1.0, 0.0).astype(F32)
    cs = jnp.dot(tril, dt * a, precision=HIGHEST, preferred_element_type=F32)
    cs_last = cs[CHUNK - 1:CHUNK, :]
    decay_in = jnp.exp(cs)
    decay_st = jnp.exp(cs_last - cs)
    if e is None:
        return dict(pre=pre, dt=dt, a=a, lower=r_i >= c_i, upper=c_i >= r_i, cs=cs, cs_t=cs.T, decay_in=decay_in,
                    decay_st=decay_st, chunk_decay=jnp.exp(cs_last),
                    low=lax.broadcasted_iota(jnp.int32, (CHUNK, 128), 1) < SSM_HEADDIM)
    dt_x = _expand_heads(dt, e)
    w_st_x = _expand_heads(dt * decay_st, e)
    decay_in_x = _expand_heads(decay_in, e)
    return dict(xs=xs, pre=pre, dt=dt, a=a, lower=r_i >= c_i, upper=c_i >= r_i, cs=cs, cs_t=cs.T, decay_in=decay_in,
                decay_st=decay_st, chunk_decay=jnp.exp(cs_last), dt_x=dt_x, w_st_x=w_st_x, decay_in_x=decay_in_x,
                chunk_decay_x=decay_in_x[CHUNK - 1:CHUNK, :], xdt=xs * dt_x, x_st=(xs * w_st_x).astype(BF16),
                low=lax.broadcasted_iota(jnp.int32, (CHUNK, 128), 1) < SSM_HEADDIM)


def _pair_decay(c, h):
    return jnp.exp(jnp.where(c["lower"], c["cs"][:, h:h + 1] - c["cs_t"][h:h + 1, :], -1e30))


def _pair_decay_t(c, h):
    return jnp.exp(jnp.where(c["upper"], c["cs_t"][h:h + 1, :] - c["cs"][:, h:h + 1], -1e30))


def _ssd_forward(xc, dtraw, z, prev, dt_bias, a_log, d_skip_x, norm_g, e):
    c = _ssd_common(xc, dtraw, dt_bias, a_log, e)
    y_groups, new_states = [], []
    for g in range(SSM_GROUPS):
        lanes = slice(g * GROUP_W, (g + 1) * GROUP_W)
        bg = xc[:, SSM_INNER + g * SSM_STATE:SSM_INNER + (g + 1) * SSM_STATE]
        cg = xc[:, SSM_INNER + (SSM_GROUPS + g) * SSM_STATE:SSM_INNER + (SSM_GROUPS + g + 1) * SSM_STATE].astype(BF16)
        pg = prev[g * SSM_STATE:(g + 1) * SSM_STATE, :]
        cb = _mxu(cg, bg, NT_DIMS)
        y_in = _mxu(cg, pg) * c["decay_in_x"][:, lanes]
        new_states.append(pg * c["chunk_decay_x"][:, lanes] + _mxu(bg.T, c["x_st"][:, lanes]))
        pairs = []
        for j in range(SSM_RPG // 2):
            h0 = g * SSM_RPG + 2 * j
            xp = c["xdt"][:, 128 * (h0 // 2):128 * (h0 // 2 + 1)]
            pairs.append(_mxu(cb * _pair_decay(c, h0), jnp.where(c["low"], xp, 0.0))
                         + _mxu(cb * _pair_decay(c, h0 + 1), jnp.where(c["low"], 0.0, xp)))
        y_groups.append(jnp.concatenate(pairs, axis=1) + y_in)
    y_pre = jnp.concatenate(y_groups, axis=1) + c["xs"] * d_skip_x
    gated = y_pre * _silu(z)
    normed = [gated[:, g * GROUP_W:(g + 1) * GROUP_W] for g in range(SSM_GROUPS)]
    normed = [yg * lax.rsqrt(jnp.mean(yg * yg, axis=-1, keepdims=True) + RMS_EPS) for yg in normed]
    return jnp.concatenate(normed, axis=1) * norm_g, y_pre, jnp.concatenate(new_states, axis=0)


def _ssd_backward(xc_ref, dtraw, z_ref, prev_ref, ypre_ref, dt_bias, a_log, d_skip_x, norm_g, e_ref, dout_ref, dst_ref,
                  dxc_ref, dz_ref):
    c = _ssd_common(xc_ref, dtraw, dt_bias, a_log, None)
    lane = lax.broadcasted_iota(jnp.int32, (CHUNK, 128), 1)
    sub = lax.broadcasted_iota(jnp.int32, (8, 128), 0)
    dcs_neg = jnp.zeros((CHUNK, 128), F32)
    row_slabs = []
    ddt = jnp.zeros((CHUNK, 128), F32)
    dw_st = jnp.zeros((CHUNK, 128), F32)
    d_decay_in = jnp.zeros((CHUNK, 128), F32)
    dd_skip = jnp.zeros((1, 128), F32)
    d_chunk_decay = jnp.zeros((1, 128), F32)
    dnorm_g = []
    for g in range(SSM_GROUPS):
        lanes = slice(g * GROUP_W, (g + 1) * GROUP_W)
        states = slice(g * SSM_STATE, (g + 1) * SSM_STATE)
        b_cols = slice(SSM_INNER + g * SSM_STATE, SSM_INNER + (g + 1) * SSM_STATE)
        c_cols = slice(SSM_INNER + (SSM_GROUPS + g) * SSM_STATE, SSM_INNER + (SSM_GROUPS + g + 1) * SSM_STATE)
        e_g = e_ref[:, lanes]
        xs = xc_ref[:, lanes]
        z = z_ref[:, lanes].astype(F32)
        y_pre = ypre_ref[:, lanes]
        dout = dout_ref[:, lanes].astype(F32)
        dt_x = _expand_heads(c["dt"], e_g)
        w_st_x = _expand_heads(c["dt"] * c["decay_st"], e_g)
        decay_in_x = _expand_heads(c["decay_in"], e_g)
        chunk_decay_x = decay_in_x[CHUNK - 1:CHUNK, :]
        xdt = xs * dt_x
        x_st = (xs * w_st_x).astype(BF16)
        sig = _sigmoid(z)
        silu_z = z * sig
        yg = y_pre * silu_z
        r = lax.rsqrt(jnp.mean(yg * yg, axis=-1, keepdims=True) + RMS_EPS)
        n = yg * r
        gh = dout * norm_g[:, lanes]
        d_gated = r * (gh - n * jnp.mean(gh * n, axis=-1, keepdims=True))
        dnorm_g.append(jnp.sum(dout * n, axis=0, keepdims=True))
        dy_g = d_gated * silu_z
        dz_ref[:, lanes] = (d_gated * y_pre * (sig * (1.0 + z * (1.0 - sig)))).astype(dz_ref.dtype)
        dxs = dy_g * d_skip_x[:, lanes]
        dd_skip = dd_skip + _reduce_heads_of_column_sums(dy_g * xs, e_g)

        bg = xc_ref[:, b_cols].astype(BF16)
        cg_f = xc_ref[:, c_cols]
        cg = cg_f.astype(BF16)
        pg = prev_ref[states, :]
        dng = dst_ref[states, :]
        cb_t = _mxu(bg, cg, NT_DIMS)
        t1 = (dy_g * decay_in_x).astype(BF16)
        d_decay_in = d_decay_in + _reduce_heads(dy_g * _mxu(cg, pg), e_g, terms=1)
        dc = _mxu(t1, pg, NT_DIMS)
        dst_ref[states, :] = _mxu(cg_f.T, t1) + dng * chunk_decay_x
        d_chunk_decay = d_chunk_decay + _reduce_heads_of_column_sums(dng * pg, e_g)
        db = _mxu(x_st, dng, NT_DIMS)
        dx_st = _mxu(bg, dng)
        dcb_t = jnp.zeros((CHUNK, CHUNK), F32)
        rows = []
        dxdt = []
        for j in range(SSM_RPG // 2):
            h0 = g * SSM_RPG + 2 * j
            blk = slice(128 * j, 128 * (j + 1))
            xp = xdt[:, blk]
            dyp = dy_g[:, blk].astype(BF16)
            pair_dx = []
            for k, xk in enumerate((jnp.where(c["low"], xp, 0.0), jnp.where(c["low"], 0.0, xp))):
                dec_t = _pair_decay_t(c, h0 + k)
                pair_dx.append(_mxu(cb_t * dec_t, dyp))
                dml_t = _mxu(xk, dyp, NT_DIMS) * dec_t
                dcb_t = dcb_t + dml_t
                dseg_t = dml_t * cb_t
                dcs_neg = dcs_neg + jnp.where(lane == h0 + k, jnp.sum(dseg_t, axis=-1, keepdims=True), 0.0)
                rows.append(jnp.sum(dseg_t, axis=0, keepdims=True))
            dxdt.append(jnp.where(c["low"], pair_dx[0], pair_dx[1]))
        slab = jnp.zeros((8, 128), F32)
        for k in range(SSM_RPG):
            slab = slab + jnp.where(sub == k, rows[k], 0.0)
        row_slabs.append(slab)
        dxc_ref[:, c_cols] = dc + _mxu(dcb_t.T, bg)
        dxc_ref[:, b_cols] = db + _mxu(dcb_t, cg)
        dxdt = jnp.concatenate(dxdt, axis=1)
        dxc_ref[:, lanes] = dxs + dxdt * dt_x + dx_st * w_st_x
        ddt = ddt + _reduce_heads(dxdt * xs, e_g, terms=1)
        dw_st = dw_st + _reduce_heads(dx_st * xs, e_g, terms=1)
    by_head = jnp.concatenate(row_slabs + [jnp.zeros((CHUNK - SSM_HEADS, 128), F32)], axis=0)
    dcs = by_head.T - dcs_neg
    dcs = dcs + d_decay_in * c["decay_in"]
    ddt = ddt + dw_st * c["decay_st"]
    d_log_st = dw_st * c["dt"] * c["decay_st"]
    dcs = dcs - d_log_st
    dcs_last = jnp.sum(d_log_st, axis=0, keepdims=True) + d_chunk_decay * c["chunk_decay"]
    row = lax.broadcasted_iota(jnp.int32, (CHUNK, 128), 0)
    dcs = dcs + jnp.where(row == CHUNK - 1, dcs_last, 0.0)
    triu = jnp.where(c["upper"], 1.0, 0.0).astype(F32)
    dda = jnp.dot(triu, dcs, precision=HIGHEST, preferred_element_type=F32)
    ddt = ddt + dda * c["a"]
    da_log = jnp.sum(dda * c["dt"], axis=0, keepdims=True) * c["a"]
    dpre = ddt * _sigmoid(c["pre"])
    return dpre, jnp.sum(dpre, axis=0, keepdims=True), da_log, dd_skip, jnp.concatenate(dnorm_g, axis=1)


def _conv_block(x, w, b):
    rows = lax.broadcasted_iota(jnp.int32, x.shape, 0)
    acc = x * w[SSM_CONV - 1:SSM_CONV, :] + b
    for k in range(SSM_CONV - 1):
        shift = SSM_CONV - 1 - k
        acc = acc + _shift_rows(x, rows, shift) * w[k:k + 1, :]
    return (_silu(acc),)


@functools.partial(jax.custom_vjp, nondiff_argnums=(2,))
def _shift_rows(x, rows, shift):
    return jnp.where(rows >= shift, pltpu.roll(x, shift, 0), 0.0)


def _shift_rows_fwd(x, rows, shift):
    return _shift_rows(x, rows, shift), rows


def _shift_rows_bwd(shift, rows, g):
    n = g.shape[0]
    return jnp.where(rows < n - shift, pltpu.roll(g, n - shift, 0), 0.0), None


_shift_rows.defvjp(_shift_rows_fwd, _shift_rows_bwd)


def _merge_block(gates, br_a, br_b):
    return (_sigmoid(gates[:, :D_MODEL]) * br_a + _sigmoid(gates[:, D_MODEL:]) * br_b,)


def _lnres_block(x, y, g, b):
    return (_layer_norm(ALPHA * x + y, g, b),)


def _memln_block(x, g, b):
    return (_layer_norm(x, g, b),)


def _attn_block(q, kv):
    outs = []
    for h in range(X_HEADS):
        qh = q[:, h * X_HEADDIM:(h + 1) * X_HEADDIM].astype(BF16)
        kh = kv[:, h * X_HEADDIM:(h + 1) * X_HEADDIM].astype(BF16)
        vh = kv[:, D_MODEL + h * X_HEADDIM:D_MODEL + (h + 1) * X_HEADDIM].astype(BF16)
        s = lax.dot_general(qh, kh, (((1,), (1,)), ((), ())), preferred_element_type=F32) * (X_HEADDIM ** -0.5)
        s = s - lax.stop_gradient(jnp.max(s, axis=-1, keepdims=True))
        e = jnp.exp(s)
        p = e / jnp.sum(e, axis=-1, keepdims=True)
        outs.append(jnp.dot(p.astype(BF16), vh, preferred_element_type=F32))
    return (jnp.concatenate(outs, axis=1),)


def _swiglu_block(gu):
    return (_silu(gu[:, :FFN_HIDDEN]) * gu[:, FFN_HIDDEN:],)


class _Comm:
    def __init__(self):
        self.gathers = []
        self.scatters = []

    @staticmethod
    def _rows(ref, rows):
        return ref if rows is None else ref.at[pl.ds(rows[0], rows[1])]

    def operands(self):
        ins = [a for a, _, _ in self.gathers] + [a for a, _ in self.scatters]
        shapes = []
        for a, idx, rows in self.gathers:
            blk = a.shape if idx is None else a.shape[1:]
            shapes.append(jax.ShapeDtypeStruct((N_DEV, blk[0] if rows is None else rows[1]) + tuple(blk[1:]), a.dtype))
        for a, rows in self.scatters:
            shapes.append(jax.ShapeDtypeStruct((N_DEV, a.shape[1] if rows is None else rows[1]) + tuple(a.shape[2:]),
                                               a.dtype))
        scratch = []
        for n in (len(self.gathers), len(self.scatters)):
            if n:
                scratch += [pltpu.SemaphoreType.DMA((7 * n,)), pltpu.SemaphoreType.DMA((7 * n,)),
                            pltpu.SemaphoreType.DMA((n,))]
        return ins, shapes, scratch

    def _split(self, in_refs, out_refs, sems):
        ng = len(self.gathers)
        g_sems = sems[:3] if ng else None
        s_sems = sems[3:] if ng else sems
        return in_refs[:ng], in_refs[ng:], out_refs[:ng], out_refs[ng:], g_sems, s_sems

    def _gather_copies(self, i, src_ref, out_ref, sems):
        send_sems, recv_sems, local_sems = sems
        x, y, c = lax.axis_index("x"), lax.axis_index("y"), lax.axis_index("c")
        me, sibling = (x, y, c), (x, y, 1 - c)
        chips = [(1 - x, y), (x, 1 - y), (1 - x, 1 - y)]
        _, idx, rows = self.gathers[i]
        src = self._rows(src_ref if idx is None else src_ref.at[idx], rows)

        def slot(px, py, pc):
            return out_ref.at[4 * px + 2 * py + pc]

        def copy(k, blk, to, from_src=False):
            return pltpu.make_async_remote_copy(
                src_ref=src if from_src else slot(*blk), dst_ref=slot(*blk), send_sem=send_sems.at[7 * i + k],
                recv_sem=recv_sems.at[7 * i + k], device_id=to, device_id_type=pl.DeviceIdType.MESH)

        mine = pltpu.make_async_copy(src, slot(*me), local_sems.at[i])
        first = [copy(0, me, sibling, True)] + [copy(1 + j, me, (*chip, c), True) for j, chip in enumerate(chips)]
        passed = [copy(4 + j, (*chip, c), sibling) for j, chip in enumerate(chips)]
        arrivals = [copy(1 + j, (*chip, c), me) for j, chip in enumerate(chips)]
        from_sibling = [copy(0, sibling, me)] + [copy(4 + j, (*chip, 1 - c), me) for j, chip in enumerate(chips)]
        return mine, first, passed, arrivals, from_sibling

    def _scatter_copies(self, i, src_ref, out_ref, sems):
        send_sems, recv_sems, local_sems = sems
        x, y, c = lax.axis_index("x"), lax.axis_index("y"), lax.axis_index("c")
        me = 4 * x + 2 * y + c
        rows = self.scatters[i][1]
        mine = pltpu.make_async_copy(self._rows(src_ref.at[me], rows), out_ref.at[me], local_sems.at[i])
        copies = []
        for k in range(1, N_DEV):
            px = 1 - x if k & 4 else x
            py = 1 - y if k & 2 else y
            pc = 1 - c if k & 1 else c
            copies.append(pltpu.make_async_remote_copy(
                src_ref=self._rows(src_ref.at[4 * px + 2 * py + pc], rows), dst_ref=out_ref.at[me],
                send_sem=send_sems.at[7 * i + k - 1], recv_sem=recv_sems.at[7 * i + k - 1], device_id=(px, py, pc),
                device_id_type=pl.DeviceIdType.MESH))
        return mine, copies

    def start(self, in_refs, out_refs, sems):
        g_in, s_in, g_out, s_out, g_sems, s_sems = self._split(in_refs, out_refs, sems)
        for i in range(len(self.gathers)):
            mine, first, _, _, _ = self._gather_copies(i, g_in[i], g_out[i], g_sems)
            mine.start()
            for cp in first:
                cp.start()
        for i in range(len(self.scatters)):
            mine, copies = self._scatter_copies(i, s_in[i], s_out[i], s_sems)
            mine.start()
            for cp in copies:
                cp.start()

    def finish(self, in_refs, out_refs, sems):
        g_in, s_in, g_out, s_out, g_sems, s_sems = self._split(in_refs, out_refs, sems)
        parts = [self._gather_copies(i, g_in[i], g_out[i], g_sems) for i in range(len(self.gathers))]
        for j in range(3):
            for _, _, passed, arrivals, _ in parts:
                arrivals[j].wait_recv()
                passed[j].start()
        for mine, first, passed, _, from_sibling in parts:
            for cp in from_sibling:
                cp.wait_recv()
            for cp in first + passed:
                cp.wait_send()
            mine.wait()
        for i in range(len(self.scatters)):
            mine, copies = self._scatter_copies(i, s_in[i], s_out[i], s_sems)
            for cp in copies:
                cp.wait_recv()
            for cp in copies:
                cp.wait_send()
            mine.wait()


def _params(grid):
    return pltpu.CompilerParams(dimension_semantics=("arbitrary",) * len(grid), vmem_limit_bytes=VMEM_LIMIT)


def _call(name, body, *, grid, ins, in_specs, out_shape, out_specs, scratch=(), comm=None, aliases=None):
    n_in, n_out, n_scr = len(ins), len(out_shape), len(scratch)
    aliases = aliases or {}
    if comm is None:
        outs = pl.pallas_call(body, grid=grid, in_specs=list(in_specs), out_specs=list(out_specs),
                              out_shape=list(out_shape), scratch_shapes=list(scratch), name=name,
                              input_output_aliases=aliases, compiler_params=_params(grid))(*ins)
        return list(outs), []
    c_ins, c_shapes, c_scratch = comm.operands()
    nci, nco = len(c_ins), len(c_shapes)
    anywhere = pl.BlockSpec(memory_space=pl.ANY)

    def carrier(*refs):
        main_in, comm_in = refs[:n_in], refs[n_in:n_in + nci]
        o0 = n_in + nci
        main_out, comm_out = refs[o0:o0 + n_out], refs[o0 + n_out:o0 + n_out + nco]
        s0 = o0 + n_out + nco
        main_scr, comm_scr = refs[s0:s0 + n_scr], refs[s0 + n_scr:]
        first = pl.program_id(0) == 0
        last = pl.program_id(0) == grid[0] - 1
        for ax in range(1, len(grid)):
            first = first & (pl.program_id(ax) == 0)
            last = last & (pl.program_id(ax) == grid[ax] - 1)

        @pl.when(first)
        def _():
            comm.start(comm_in, comm_out, comm_scr)

        body(*main_in, *main_out, *main_scr)

        @pl.when(last)
        def _():
            comm.finish(comm_in, comm_out, comm_scr)

    outs = pl.pallas_call(carrier, grid=grid, in_specs=list(in_specs) + [anywhere] * nci,
                          out_specs=list(out_specs) + [anywhere] * nco, out_shape=list(out_shape) + c_shapes,
                          scratch_shapes=list(scratch) + c_scratch, name=name, input_output_aliases=aliases,
                          compiler_params=_params(grid))(*ins, *c_ins)
    return list(outs[:n_out]), list(outs[n_out:])


def _comm_only(name, comm):
    c_ins, c_shapes, c_scratch = comm.operands()
    nci, nco = len(c_ins), len(c_shapes)
    anywhere = pl.BlockSpec(memory_space=pl.ANY)

    def body(*refs):
        comm.start(refs[:nci], refs[nci:nci + nco], refs[nci + nco:])
        comm.finish(refs[:nci], refs[nci:nci + nco], refs[nci + nco:])

    return list(pl.pallas_call(body, in_specs=[anywhere] * nci, out_specs=[anywhere] * nco, out_shape=c_shapes,
                               scratch_shapes=c_scratch, name=name)(*c_ins))


def _stage_fwd(name, f, grid, ins, in_specs, out_shapes, out_specs, comm=None):
    n_in = len(ins)

    def body(*refs):
        res = f(*[r[...].astype(F32) for r in refs[:n_in]])
        for o_ref, val in zip(refs[n_in:], res):
            o_ref[...] = val.astype(o_ref.dtype)

    return _call(name, body, grid=grid, ins=ins, in_specs=in_specs, out_shape=out_shapes, out_specs=out_specs, comm=comm)


def _stage_bwd(name, f, grid, ins, in_specs, cts, ct_specs, grads, comm=None, ct_product=None):
    n_in = len(ins)
    flat_cts = [c for group in cts for c in group]
    flat_ct_specs = [s for group, spec in zip(cts, ct_specs) for s in (spec,) * len(group)]
    if ct_product is not None:
        assert not cts
        flat_cts = [ct_product[0], ct_product[2]]
        flat_ct_specs = [ct_product[1], pl.BlockSpec(ct_product[2].shape, lambda *_: (0, 0))]
    n_ct = len(flat_cts)
    diff = [g[0] for g in grads]
    buffers = [(k, g[4]) for k, g in enumerate(grads) if len(g) > 4]
    n_buf = len(buffers)

    def body(*refs):
        vals = [r[...].astype(F32) for r in refs[:n_in]]
        ct_refs = refs[n_in:n_in + n_ct]
        g_refs = refs[n_in + n_ct + n_buf:]
        ct_vals, pos = [], 0
        if ct_product is not None:
            ct_vals.append(lax.dot_general(ct_refs[0][...].astype(BF16), ct_refs[1][...].astype(BF16), NT_DIMS,
                                           preferred_element_type=F32))
        for group in cts:
            acc = ct_refs[pos][...].astype(F32)
            for j in range(1, len(group)):
                acc = acc + ct_refs[pos + j][...].astype(F32)
            ct_vals.append(acc)
            pos += len(group)

        def g_fn(*dvals):
            full = list(vals)
            for i, dv in zip(diff, dvals):
                full[i] = dv
            return f(*full)

        _, vjp = jax.vjp(g_fn, *[vals[i] for i in diff])
        gvals = vjp(tuple(ct_vals))
        for gspec, g_ref, gval in zip(grads, g_refs, gvals):
            acc_axes = gspec[1]
            if not acc_axes:
                g_ref[...] = gval.astype(g_ref.dtype)
            else:
                first = pl.program_id(acc_axes[0]) == 0
                for ax in acc_axes[1:]:
                    first = first & (pl.program_id(ax) == 0)

                @pl.when(first)
                def _():
                    g_ref[...] = jnp.zeros_like(g_ref)

                g_ref[...] += gval.astype(g_ref.dtype)

    out_shapes, out_specs = [], []
    for gspec in grads:
        shape, spec = gspec[3] if len(gspec) > 3 else (ins[gspec[0]].shape, in_specs[gspec[0]])
        out_shapes.append(jax.ShapeDtypeStruct(shape, gspec[2]))
        out_specs.append(spec)
    anywhere = pl.BlockSpec(memory_space=pl.ANY)
    return _call(name, body, grid=grid, ins=list(ins) + flat_cts + [b for _, b in buffers],
                 in_specs=list(in_specs) + flat_ct_specs + [anywhere] * n_buf, out_shape=out_shapes, out_specs=out_specs,
                 comm=comm, aliases={n_in + n_ct + j: k for j, (k, _) in enumerate(buffers)})


def _pick_tile(n, candidates):
    for c in candidates:
        if n % c == 0:
            return c
    return n


def _matmul(name, a, b, *, ta=False, tb=False, add=None, extra=None, out_dtype=F32, comm=None):
    if ta:
        k_dim, m = a.shape
    else:
        m, k_dim = a.shape
    n = b.shape[0] if tb else b.shape[1]
    assert (b.shape[1] if tb else b.shape[0]) == k_dim and not (ta and tb)
    tm = _pick_tile(m, (1024, 1408, 512, 256, 128))
    tn = _pick_tile(n, (1024, 1408, 512, 256, 128))
    if ta:
        tk = _pick_tile(k_dim, (1024, 512, 256, 128))
    elif k_dim <= 2816:
        tk = k_dim
    else:
        tk = _pick_tile(k_dim, (1408, 1024, 512, 256, 128))
    nk = k_dim // tk
    grid = (m // tm, n // tn, nk)
    a_spec = pl.BlockSpec((tk, tm), lambda i, j, k: (k, i)) if ta else pl.BlockSpec((tm, tk), lambda i, j, k: (i, k))
    b_spec = pl.BlockSpec((tn, tk), lambda i, j, k: (j, k)) if tb else pl.BlockSpec((tk, tn), lambda i, j, k: (k, j))
    o_spec = pl.BlockSpec((tm, tn), lambda i, j, k: (i, j))
    dims = (((0 if ta else 1,), (1 if tb else 0,)), ((), ()))
    has_add = add is not None
    has_extra = extra is not None

    def body(*refs):
        a_ref, b_ref = refs[0], refs[1]
        add_ref = refs[2] if has_add else None
        o_ref, acc_ref = refs[-2], refs[-1]
        k = pl.program_id(2)
        part = lax.dot_general(a_ref[...].astype(BF16), b_ref[...].astype(BF16), dims, preferred_element_type=F32)

        def finish(res):
            if has_add:
                res = res + add_ref[...].astype(F32)
            if has_extra:
                a2_ref, b2_ref = refs[2 + has_add], refs[3 + has_add]
                res = res + lax.dot_general(a2_ref[...].astype(BF16), b2_ref[...].astype(BF16), NT_DIMS,
                                            preferred_element_type=F32)
            o_ref[...] = res.astype(o_ref.dtype)

        if nk == 1:
            finish(part)
        else:
            @pl.when(k == 0)
            def _():
                acc_ref[...] = part

            @pl.when((k > 0) & (k < nk - 1))
            def _():
                acc_ref[...] += part

            @pl.when(k == nk - 1)
            def _():
                finish(acc_ref[...] + part)

    ins = [a, b] + ([add] if has_add else [])
    in_specs = [a_spec, b_spec] + ([o_spec] if has_add else [])
    if has_extra:
        k2 = extra[0].shape[1]
        ins += list(extra)
        in_specs += [pl.BlockSpec((tm, k2), lambda i, j, k: (i, 0)), pl.BlockSpec((tn, k2), lambda i, j, k: (j, 0))]
    acc_shape = (tm, tn) if nk > 1 else (8, 128)
    outs, comm_outs = _call(name, body, grid=grid, ins=ins, in_specs=in_specs,
                            out_shape=[jax.ShapeDtypeStruct((m, n), out_dtype)], out_specs=[o_spec],
                            scratch=[pltpu.VMEM(acc_shape, F32)], comm=comm)
    return outs[0], comm_outs


def _matmul_lnres(name, a_fn, a_ins, a_specs, tm, b, x, g, beta, comm=None):
    m = x.shape[0]
    k_dim, n = b.shape
    n_a = len(a_ins)
    row = lambda w: pl.BlockSpec((tm, w), lambda i: (i, 0))
    whole = lambda shape: pl.BlockSpec(shape, lambda i: (0, 0))

    def body(*refs):
        b_ref, x_ref, g_ref, beta_ref = refs[n_a:n_a + 4]
        y_ref, h_ref, hb_ref = refs[-3:]
        if a_fn is None:
            a = refs[0][...].astype(BF16)
        else:
            (a,) = a_fn(*[r[...].astype(F32) for r in refs[:n_a]])
            a = a.astype(BF16)
            refs[n_a + 4][...] = a
        y = jnp.dot(a, b_ref[...].astype(BF16), preferred_element_type=F32).astype(y_ref.dtype)
        y_ref[...] = y
        (h,) = _lnres_block(x_ref[...], y.astype(F32), g_ref[...], beta_ref[...])
        h_ref[...] = h
        hb_ref[...] = h.astype(hb_ref.dtype)

    sds = jax.ShapeDtypeStruct
    a_out = ([sds((m, k_dim), BF16)], [row(k_dim)]) if a_fn is not None else ([], [])
    return _call(name, body, grid=(m // tm,), ins=list(a_ins) + [b, x, g, beta],
                 in_specs=list(a_specs) + [whole((k_dim, n)), row(n), whole((1, n)), whole((1, n))],
                 out_shape=a_out[0] + [sds((m, n), BF16), sds((m, n), F32), sds((m, n), BF16)],
                 out_specs=a_out[1] + [row(n), row(n), row(n)], comm=comm)


SSD_STATE = (SSM_GROUPS * SSM_STATE, SSM_RPG * SSM_HEADDIM)


def _ssd_fwd(xc, dt_raw, proj, dt_bias, a_log, d_skip, norm_g, nb, nc, comm=None):
    t = xc.shape[0]
    row = lambda b, c: (b * nc + c, 0)
    par = lambda shape: pl.BlockSpec(shape, lambda b, c: (0, 0))

    def body(xc_ref, dt_ref, z_ref, dtb_ref, al_ref, ds_ref, ng_ref, e_ref, y_ref, ypre_ref, prev_ref, st_ref):
        @pl.when(pl.program_id(1) == 0)
        def _():
            st_ref[...] = jnp.zeros_like(st_ref)

        prev = st_ref[...]
        prev_ref[0, 0] = prev
        y, y_pre, new_state = _ssd_forward(xc_ref[...], dt_ref[...], z_ref[...].astype(F32), prev, dtb_ref[...],
                                           al_ref[...], ds_ref[...], ng_ref[...], e_ref[...])
        y_ref[...] = y.astype(y_ref.dtype)
        ypre_ref[...] = y_pre
        st_ref[...] = new_state

    return _call(
        "ssd_fwd", body, grid=(nb, nc), ins=[xc, dt_raw, proj, dt_bias, a_log, d_skip, norm_g, _head_expander()],
        in_specs=[pl.BlockSpec((CHUNK, SSM_CONV_DIM), row), pl.BlockSpec((CHUNK, 128), row),
                  pl.BlockSpec((CHUNK, SSM_INNER), lambda b, c: (b * nc + c, 1)),
                  par((1, 128)), par((1, 128)), par((1, SSM_INNER)), par((1, SSM_INNER)), par((128, SSM_INNER))],
        out_specs=[pl.BlockSpec((CHUNK, SSM_INNER), row), pl.BlockSpec((CHUNK, SSM_INNER), row),
                   pl.BlockSpec((1, 1) + SSD_STATE, lambda b, c: (b, c, 0, 0))],
        out_shape=[jax.ShapeDtypeStruct((t, SSM_INNER), BF16), jax.ShapeDtypeStruct((t, SSM_INNER), F32),
                   jax.ShapeDtypeStruct((nb, nc) + SSD_STATE, F32)],
        scratch=[pltpu.VMEM(SSD_STATE, F32)], comm=comm)


def _ssd_bwd(xc, dt_raw, proj, prevs, y_pre, dt_bias, a_log, d_skip, norm_g, dy, dproj, nb, nc, comm=None):
    t = xc.shape[0]
    row = lambda b, c: (b * nc + (nc - 1 - c), 0)
    par = lambda shape: pl.BlockSpec(shape, lambda b, c: (0, 0))
    z_spec = pl.BlockSpec((CHUNK, SSM_INNER), lambda b, c: (b * nc + (nc - 1 - c), 1))

    def body(xc_ref, dt_ref, z_ref, prev_ref, ypre_ref, dtb_ref, al_ref, ds_ref, ng_ref, e_ref, dy_ref, _,
             dxc_ref, ddt_ref, dz_ref, ddtb_ref, dal_ref, dds_ref, dng_ref, dst_ref):
        @pl.when(pl.program_id(1) == 0)
        def _():
            dst_ref[...] = jnp.zeros_like(dst_ref)

        @pl.when((pl.program_id(0) == 0) & (pl.program_id(1) == 0))
        def _():
            ddtb_ref[...] = jnp.zeros_like(ddtb_ref)
            dal_ref[...] = jnp.zeros_like(dal_ref)
            dds_ref[...] = jnp.zeros_like(dds_ref)
            dng_ref[...] = jnp.zeros_like(dng_ref)

        ddt, ddtb, dal, dds, dng = _ssd_backward(
            xc_ref, dt_ref[...], z_ref, prev_ref.at[0, 0], ypre_ref, dtb_ref[...], al_ref[...], ds_ref[...], ng_ref[...],
            e_ref, dy_ref, dst_ref, dxc_ref, dz_ref)
        ddt_ref[...] = ddt.astype(ddt_ref.dtype)
        ddtb_ref[...] += ddtb
        dal_ref[...] += dal
        dds_ref[...] += dds
        dng_ref[...] += dng

    return _call(
        "ssd_bwd", body, grid=(nb, nc),
        ins=[xc, dt_raw, proj, prevs, y_pre, dt_bias, a_log, d_skip, norm_g, _head_expander(), dy, dproj],
        in_specs=[pl.BlockSpec((CHUNK, SSM_CONV_DIM), row), pl.BlockSpec((CHUNK, DT_LANES), row), z_spec,
                  pl.BlockSpec((1, 1) + SSD_STATE, lambda b, c: (b, nc - 1 - c, 0, 0)),
                  pl.BlockSpec((CHUNK, SSM_INNER), row),
                  par((1, 128)), par((1, 128)), par((1, SSM_INNER)), par((1, SSM_INNER)), par((128, SSM_INNER)),
                  pl.BlockSpec((CHUNK, SSM_INNER), row), pl.BlockSpec(memory_space=pl.ANY)],
        out_specs=[pl.BlockSpec((CHUNK, SSM_CONV_DIM), row), pl.BlockSpec((CHUNK, DT_LANES), row), z_spec,
                   par((1, 128)), par((1, 128)), par((1, 128)), par((1, SSM_INNER))],
        out_shape=[jax.ShapeDtypeStruct((t, SSM_CONV_DIM), F32), jax.ShapeDtypeStruct((t, DT_LANES), BF16),
                   jax.ShapeDtypeStruct(dproj.shape, dproj.dtype), jax.ShapeDtypeStruct((1, 128), F32),
                   jax.ShapeDtypeStruct((1, 128), F32), jax.ShapeDtypeStruct((1, 128), F32),
                   jax.ShapeDtypeStruct((1, SSM_INNER), F32)],
        scratch=[pltpu.VMEM(SSD_STATE, F32)], comm=comm, aliases={11: 2})


def _loss_head(y, target):
    t, d = y.shape
    tm = _pick_tile(t, (256,))
    blk = pl.BlockSpec((tm, d), lambda i: (i, 0))

    def body(y_ref, t_ref, loss_ref, dy_ref):
        err = y_ref[...] - t_ref[...]
        dy_ref[...] = err * (1.0 / d)

        @pl.when(pl.program_id(0) == 0)
        def _():
            loss_ref[...] = jnp.zeros_like(loss_ref)

        loss_ref[...] += 0.5 * jnp.sum(jnp.mean(err * err, axis=-1, keepdims=True), axis=0, keepdims=True)

    return _call("loss_head", body, grid=(t // tm,), ins=[y, target], in_specs=[blk, blk],
                 out_specs=[pl.BlockSpec((1, 1), lambda i: (0, 0)), blk],
                 out_shape=[jax.ShapeDtypeStruct((1, 1), F32), jax.ShapeDtypeStruct((t, d), F32)])[0]


def _adamw_math(g, w, m, v):
    m_new = ADAM_B1 * m + (1.0 - ADAM_B1) * g
    v_new = ADAM_B2 * v + (1.0 - ADAM_B2) * jnp.square(g)
    m_hat = m_new / (1.0 - ADAM_B1 ** ADAM_STEP)
    v_hat = v_new / (1.0 - ADAM_B2 ** ADAM_STEP)
    delta = -ADAM_LR * (m_hat / (jnp.sqrt(v_hat) + ADAM_EPS) + ADAM_WD * w)
    return delta, m_new, v_new


def _adamw_sharded(name, parts, w, m, v, comm=None):
    _, a, b = w.shape
    tr = _pick_tile(a, (128,))
    nt = a // tr
    part_specs = [pl.BlockSpec((N_DEV, tr, b),
                               (lambda l, i, _k=k: (0, jnp.where(l == _k, i, jnp.where(l > _k, nt - 1, 0)), 0)))
                  for k in range(DEPTH)]
    blk = pl.BlockSpec((1, tr, b), lambda l, i: (l, i, 0))

    def body(*refs):
        p_refs = refs[:DEPTH]
        w_ref, m_ref, v_ref, g_out, d_out, m_out, v_out = refs[DEPTH:]
        for k in range(DEPTH):
            @pl.when(pl.program_id(0) == k)
            def _(p_ref=p_refs[k]):
                g = p_ref[0].astype(F32)
                for p in range(1, N_DEV):
                    g = g + p_ref[p].astype(F32)
                delta, m_new, v_new = _adamw_math(g, w_ref[0], m_ref[0], v_ref[0])
                g_out[0] = g
                d_out[0] = delta
                m_out[0] = m_new
                v_out[0] = v_new

    return _call(name, body, grid=(DEPTH, nt), ins=list(parts) + [w, m, v], in_specs=part_specs + [blk, blk, blk],
                 out_specs=[blk] * 4, out_shape=[jax.ShapeDtypeStruct(w.shape, F32)] * 4, comm=comm)


def _adamw_small(name, g, w, m, v):
    full = pl.BlockSpec(w.shape, lambda i: (0, 0))

    def body(g_ref, w_ref, m_ref, v_ref, d_out, m_out, v_out):
        delta, m_new, v_new = _adamw_math(g_ref[...], w_ref[...], m_ref[...], v_ref[...])
        d_out[...] = delta
        m_out[...] = m_new
        v_out[...] = v_new

    return _call(name, body, grid=(1,), ins=[g, w, m, v], in_specs=[full] * 4, out_specs=[full] * 3,
                 out_shape=[jax.ShapeDtypeStruct(w.shape, F32)] * 3)[0]


def _sum_parts(name, parts):
    n_parts, rows, cols = parts.shape
    tr = _pick_tile(rows, (512, 256, 128, 64, 32, 16, 8))

    def body(p_ref, o_ref):
        acc = p_ref[0]
        for p in range(1, n_parts):
            acc = acc + p_ref[p]
        o_ref[...] = acc

    return _call(name, body, grid=(rows // tr,), ins=[parts],
                 in_specs=[pl.BlockSpec((n_parts, tr, cols), lambda i: (0, i, 0))],
                 out_specs=[pl.BlockSpec((tr, cols), lambda i: (i, 0))],
                 out_shape=[jax.ShapeDtypeStruct((rows, cols), parts.dtype)])[0][0]


W_IN_SHARD = IN_COLS // N_DEV


def _pack_w_in(gathered):
    r = gathered.shape[1]
    tr = _pick_tile(r, (128,))

    def body(g_ref, main_ref, dt_ref):
        w = jnp.concatenate([g_ref[j].astype(F32) for j in range(N_DEV)], axis=1)
        main_ref[...] = jnp.concatenate([w[:, :XBC_COL], w[:, GA_COL:], w[:, XBC_COL:DT_COL]],
                                        axis=1).astype(main_ref.dtype)
        dt_ref[...] = jnp.concatenate([w[:, DT_COL:GA_COL], jnp.zeros((tr, DT_LANES - SSM_HEADS), F32)],
                                      axis=1).astype(dt_ref.dtype)

    return _call("pack_w_in", body, grid=(r // tr,), ins=[gathered],
                 in_specs=[pl.BlockSpec((N_DEV, tr, W_IN_SHARD), lambda i: (0, i, 0))],
                 out_specs=[pl.BlockSpec((tr, P_COLS), lambda i: (i, 0)), pl.BlockSpec((tr, DT_LANES), lambda i: (i, 0))],
                 out_shape=[jax.ShapeDtypeStruct((r, P_COLS), gathered.dtype),
                            jax.ShapeDtypeStruct((r, DT_LANES), gathered.dtype)])[0]


def _unpack_w_in(main, dt):
    r = main.shape[0]
    tr = _pick_tile(r, (128,))

    def body(main_ref, dt_ref, o_ref):
        main = main_ref[...].astype(F32)
        w = jnp.concatenate([main[:, :P_GATE], main[:, P_XBC:], dt_ref[...].astype(F32)[:, :SSM_HEADS],
                             main[:, P_GATE:P_XBC]], axis=1)
        for j in range(N_DEV):
            o_ref[j] = w[:, j * W_IN_SHARD:(j + 1) * W_IN_SHARD].astype(o_ref.dtype)

    return _call("unpack_w_in", body, grid=(r // tr,), ins=[main, dt],
                 in_specs=[pl.BlockSpec((tr, P_COLS), lambda i: (i, 0)), pl.BlockSpec((tr, DT_LANES), lambda i: (i, 0))],
                 out_specs=[pl.BlockSpec((N_DEV, tr, W_IN_SHARD), lambda i: (0, i, 0))],
                 out_shape=[jax.ShapeDtypeStruct((N_DEV, r, W_IN_SHARD), main.dtype)])[0][0]


def _pad_heads(v):
    return jnp.pad(v, (0, 128 - SSM_HEADS)).reshape(1, 128)


def _run_step(x, mem, target, small, ex):
    nb, s, d = x.shape
    t = nb * s
    nc = s // CHUNK
    rows = _pick_tile(t, (256,))
    rows_wide = _pick_tile(t, (512, 256))
    tq = _pick_tile(s, (512, 256))
    vec = lambda a: a.reshape(1, -1)
    full1 = lambda shape: pl.BlockSpec(shape, lambda i: (0,) * len(shape))
    row1 = lambda tm, w: pl.BlockSpec((tm, w), lambda i: (i, 0))
    sds = jax.ShapeDtypeStruct

    def mm(call, l, a, b, **kw):
        comm = ex.before(call, l)
        out, comm_outs = _matmul(call, a, b, comm=comm, **kw)
        if comm is not None:
            ex.after(call, l, comm_outs)
        return out

    def stage_bwd(call, l, *args, **kw):
        comm = ex.before(call, l)
        outs, comm_outs = _stage_bwd(call, *args, comm=comm, **kw)
        if comm is not None:
            ex.after(call, l, comm_outs)
        return outs

    def stage_fwd(call, l, *args):
        comm = ex.before(call, l)
        outs, comm_outs = _stage_fwd(call, *args, comm=comm)
        if comm is not None:
            ex.after(call, l, comm_outs)
        return outs

    mem_specs = [row1(256, d), full1((1, d)), full1((1, d))]
    mem_ins = [mem.reshape(nb * MEM_LEN, d), vec(small["mem_ln_g"]), vec(small["mem_ln_b"])]
    (mem_n,) = stage_fwd("memln_fwd", 0, _memln_block, (nb * MEM_LEN // 256,), mem_ins, mem_specs,
                          [sds((nb * MEM_LEN, d), BF16)], [row1(256, d)])

    h = x.reshape(t, d)
    h_bf = h.astype(BF16)
    ln_specs = [row1(rows_wide, d), row1(rows_wide, d), full1((1, d)), full1((1, d))]
    saved = []
    for l in range(DEPTH):
        sv = {"h_bf": h_bf}
        w_p, w_dt = ex.weight("w_in", l)
        proj = mm("mm_in", l, h_bf, w_p, out_dtype=BF16)
        dt_raw = mm("mm_dt", l, h_bf, w_dt)
        sv["proj"] = proj
        sgu_ins = [proj, vec(small["sg_ln_g"][l]), vec(small["sg_ln_b"][l]), small["sg_w"][l], small["sg_b"][l].T]
        sgu_specs = [pl.BlockSpec((rows, 2 * d), lambda i: (i, 0)), full1((1, d)), full1((1, d)),
                     full1((SG_GROUPS, CHUNK, CHUNK)), full1((CHUNK, SG_GROUPS))]
        (a_out,) = stage_fwd("sgu_fwd", l, _sgu_block, (t // rows,), sgu_ins, sgu_specs, [sds((t, d), BF16)],
                              [row1(rows, d)])
        sv["sgu"] = (sgu_ins, sgu_specs)
        sv["a_out"] = a_out
        cw = 256
        conv_ins = [proj, small["conv_w"][l], vec(small["conv_b"][l])]
        conv_specs = [pl.BlockSpec((s, cw), lambda j, b: (b, P_XBC // cw + j)),
                      pl.BlockSpec((SSM_CONV, cw), lambda j, b: (0, j)), pl.BlockSpec((1, cw), lambda j, b: (0, j))]
        conv_out_spec = pl.BlockSpec((s, cw), lambda j, b: (b, j))
        (xc,) = stage_fwd("conv_fwd", l, _conv_block, (SSM_CONV_DIM // cw, nb), conv_ins, conv_specs,
                           [sds((t, SSM_CONV_DIM), F32)], [conv_out_spec])
        sv["conv"] = (conv_ins, conv_specs, conv_out_spec)
        ssd_par = [_pad_heads(small["dt_bias"][l]), _pad_heads(small["a_log"][l]),
                   vec(jnp.repeat(small["d_skip"][l], SSM_HEADDIM)), vec(small["ssm_norm_g"][l])]
        comm = ex.before("ssd_fwd", l)
        (y_ssd, y_pre, prevs), comm_outs = _ssd_fwd(xc, dt_raw, proj, *ssd_par, nb, nc, comm=comm)
        if comm is not None:
            ex.after("ssd_fwd", l, comm_outs)
        sv["ssd"] = (xc, dt_raw, prevs, y_pre, ssd_par)
        sv["y_ssd"] = y_ssd
        br_a = mm("mm_sq", l, a_out, ex.weight("p_a", l), out_dtype=BF16)
        br_b = mm("mm_pb", l, y_ssd, ex.weight("p_b", l), out_dtype=BF16)
        merge_ins = [proj, br_a, br_b]
        merge_out_spec = row1(rows_wide, d)
        merge_specs = [pl.BlockSpec((rows_wide, 2 * d), lambda i: (i, P_GATE // (2 * d))), merge_out_spec, merge_out_spec]
        sv["merge"] = (merge_ins, merge_specs, merge_out_spec)
        ln_par = [(vec(small["ln_g"][l, k]), vec(small["ln_b"][l, k])) for k in range(3)]

        def fused(call, a_fn, a_ins, a_specs, tm, w, x_in, par):
            comm = ex.before(call, l)
            outs, comm_outs = _matmul_lnres(call, a_fn, a_ins, a_specs, tm, w, x_in, *par, comm=comm)
            if comm is not None:
                ex.after(call, l, comm_outs)
            return outs

        merged, y1, h1, h1_bf = fused("mm_mix_ln", _merge_block, merge_ins, merge_specs, rows_wide,
                                      ex.weight("w_mix_o", l), h, ln_par[0])
        sv["merged"] = merged
        sv["ln1"] = [h, y1, *ln_par[0]]
        q = mm("mm_sq", l, h1_bf, ex.weight("w_xq", l), out_dtype=BF16)
        kv = mm("mm_kv", l, mem_n, ex.weight("w_xkv", l), out_dtype=BF16)
        attn_ins = [q, kv]
        attn_out_spec = pl.BlockSpec((tq, d), lambda b, i: (b * (s // tq) + i, 0))
        attn_specs = [attn_out_spec, pl.BlockSpec((MEM_LEN, 2 * d), lambda b, i: (b, 0))]
        (o,) = stage_fwd("attn_fwd", l, _attn_block, (nb, s // tq), attn_ins, attn_specs, [sds((t, d), BF16)],
                          [attn_out_spec])
        sv["attn"] = (attn_ins, attn_specs, attn_out_spec)
        sv["o"] = o
        sv["h1_bf"] = h1_bf
        y2, h2, h2_bf = fused("mm_xo_ln", None, [o], [row1(rows_wide, d)], rows_wide, ex.weight("w_xo", l), h1, ln_par[1])
        sv["ln2"] = [h1, y2, *ln_par[1]]
        sv["h2_bf"] = h2_bf
        gu = mm("mm_ffn_in", l, h2_bf, ex.weight("w_ffn_in", l), out_dtype=BF16)
        act, y3, h3, h3_bf = fused("mm_ffn_out_ln", _swiglu_block, [gu], [row1(rows, 2 * FFN_HIDDEN)], rows,
                                   ex.weight("w_ffn_out", l), h2, ln_par[2])
        sv["gu"] = gu
        sv["act"] = act
        sv["ln3"] = [h2, y3, *ln_par[2]]
        h, h_bf = h3, h3_bf
        saved.append(sv)

    loss, dh = _loss_head(h, target.reshape(t, d))

    g_small = {n: [None] * DEPTH for n in SMALL_REP + SMALL_SH if n not in ("mem_ln_g", "mem_ln_b")}
    dmem_n = []
    ln_grads = [(0, (), F32), (1, (), BF16), (2, (0,), F32), (3, (0,), F32)]
    for l in reversed(range(DEPTH)):
        sv = saved[l]
        dln_g, dln_b = [None] * 3, [None] * 3
        dres, dy3, dln_g[2], dln_b[2] = stage_bwd("lnres_bwd", l, _lnres_block, (t // rows_wide,), sv["ln3"], ln_specs,
                                                  [(dh,)], [row1(rows_wide, d)], ln_grads)
        ex.grad("w_ffn_out", l, mm("mm_ffn_out_dw", l, sv["act"], dy3, ta=True, out_dtype=BF16))
        (dgu,) = stage_bwd("swiglu_bwd", l, _swiglu_block, (t // rows,), [sv["gu"]], [row1(rows, 2 * FFN_HIDDEN)],
                           [], [], [(0, (), BF16)], ct_product=(dy3, row1(rows, d), ex.weight("w_ffn_out", l)))
        ex.grad("w_ffn_in", l, mm("mm_ffn_in_dw", l, sv["h2_bf"], dgu, ta=True, out_dtype=BF16))
        dh2 = mm("mm_ffn_in_dx", l, dgu, ex.weight("w_ffn_in", l), tb=True, add=dres)
        dres, dy2, dln_g[1], dln_b[1] = stage_bwd("lnres_bwd", l, _lnres_block, (t // rows_wide,), sv["ln2"], ln_specs,
                                                  [(dh2,)], [row1(rows_wide, d)], ln_grads)
        ex.grad("w_xo", l, mm("mm_sq_dw", l, sv["o"], dy2, ta=True, out_dtype=BF16))
        do = mm("mm_sq_dx", l, dy2, ex.weight("w_xo", l), tb=True, out_dtype=BF16)
        attn_ins, attn_specs, attn_out_spec = sv["attn"]
        dq, dkv = stage_bwd("attn_bwd", l, _attn_block, (nb, s // tq), attn_ins, attn_specs, [(do,)], [attn_out_spec],
                            [(0, (), BF16), (1, (1,), F32)])
        ex.grad("w_xq", l, mm("mm_sq_dw", l, sv["h1_bf"], dq, ta=True, out_dtype=BF16))
        dh1 = mm("mm_sq_dx", l, dq, ex.weight("w_xq", l), tb=True, add=dres)
        ex.grad("w_xkv", l, mm("mm_kv_dw", l, mem_n, dkv, ta=True, out_dtype=BF16))
        dmem_n.append(mm("mm_kv_dx", l, dkv, ex.weight("w_xkv", l), tb=True))
        dres, dy1, dln_g[0], dln_b[0] = stage_bwd("lnres_bwd", l, _lnres_block, (t // rows_wide,), sv["ln1"], ln_specs,
                                                  [(dh1,)], [row1(rows_wide, d)], ln_grads)
        g_small["ln_g"][l] = jnp.concatenate(dln_g, axis=0)
        g_small["ln_b"][l] = jnp.concatenate(dln_b, axis=0)
        ex.grad("w_mix_o", l, mm("mm_sq_dw", l, sv["merged"], dy1, ta=True, out_dtype=BF16))
        merge_ins, merge_specs, merge_out_spec = sv["merge"]
        dproj, dbr_a, dbr_b = stage_bwd("merge_bwd", l, _merge_block, (t // rows_wide,), merge_ins, merge_specs, [], [],
                                        [(0, (), BF16, ((t, P_COLS), merge_specs[0])), (1, (), BF16), (2, (), BF16)],
                                        ct_product=(dy1, merge_out_spec, ex.weight("w_mix_o", l)))
        ex.grad("p_a", l, mm("mm_sq_dw", l, sv["a_out"], dbr_a, ta=True, out_dtype=BF16))
        da_out = mm("mm_sq_dx", l, dbr_a, ex.weight("p_a", l), tb=True, out_dtype=BF16)
        ex.grad("p_b", l, mm("mm_pb_dw", l, sv["y_ssd"], dbr_b, ta=True, out_dtype=BF16))
        dy_ssd = mm("mm_pb_dx", l, dbr_b, ex.weight("p_b", l), tb=True, out_dtype=BF16)
        sgu_ins, sgu_specs = sv["sgu"]
        dproj, dsg_ln_g, dsg_ln_b, dsg_w, dsg_b = stage_bwd(
            "sgu_bwd", l, _sgu_block, (t // CHUNK,), sgu_ins, [row1(CHUNK, 2 * d)] + sgu_specs[1:], [(da_out,)],
            [row1(CHUNK, d)],
            [(0, (), BF16, ((t, P_COLS), row1(CHUNK, 2 * d)), dproj), (1, (0,), F32), (2, (0,), F32), (3, (0,), F32),
             (4, (0,), F32)])
        g_small["sg_ln_g"][l], g_small["sg_ln_b"][l], g_small["sg_w"][l], g_small["sg_b"][l] = (
            dsg_ln_g[0], dsg_ln_b[0], dsg_w, dsg_b.T)
        xc, dt_raw, prevs, y_pre, ssd_par = sv["ssd"]
        comm = ex.before("ssd_bwd", l)
        (dxc, ddt, dproj, ddtb, dal, dds, dng), comm_outs = _ssd_bwd(xc, dt_raw, sv["proj"], prevs, y_pre, *ssd_par,
                                                                     dy_ssd, dproj, nb, nc, comm=comm)
        if comm is not None:
            ex.after("ssd_bwd", l, comm_outs)
        g_small["dt_bias"][l], g_small["a_log"][l], g_small["d_skip"][l] = (
            ddtb[0, :SSM_HEADS], dal[0, :SSM_HEADS], dds[0, :SSM_HEADS])
        g_small["ssm_norm_g"][l] = dng[0]
        conv_ins, conv_specs, conv_out_spec = sv["conv"]
        dproj, dconv_w, dconv_b = stage_bwd("conv_bwd", l, _conv_block, (SSM_CONV_DIM // 256, nb), conv_ins, conv_specs,
                                            [(dxc,)], [conv_out_spec],
                                            [(0, (), BF16, ((t, P_COLS), conv_specs[0]), dproj), (1, (1,), F32),
                                             (2, (1,), F32)])
        g_small["conv_w"][l], g_small["conv_b"][l] = dconv_w, dconv_b[0]
        if l == 0:
            dmg, dmb = stage_bwd("memln_bwd", l, _memln_block, (nb * MEM_LEN // 256,), mem_ins, mem_specs,
                                 [tuple(dmem_n)], [row1(256, d)], [(1, (0,), F32), (2, (0,), F32)])
            done = {n: jnp.stack(g, axis=0) for n, g in g_small.items()}
            done["mem_ln_g"], done["mem_ln_b"] = dmg[0], dmb[0]
            ex.small_grads(done)
        w_p, w_dt = ex.weight("w_in", l)
        g_dt = mm("mm_dt_dw", l, sv["h_bf"], ddt, ta=True, out_dtype=BF16)
        ex.grad("w_in", l, _unpack_w_in(mm("mm_in_dw", l, sv["h_bf"], dproj, ta=True, out_dtype=BF16), g_dt))
        dh = mm("mm_in_dx", l, dproj, w_p, tb=True, add=dres, extra=(ddt, w_dt))

    return loss, dh.reshape(nb, s, d)


def _pack_flat(arrays, rows):
    flat = jnp.concatenate([a.reshape(-1) for a in arrays])
    return jnp.pad(flat, (0, rows * 128 - flat.shape[0])).reshape(rows, 128)


def _unpack_flat(packed, shapes):
    lead = packed.shape[:-2]
    flat = packed.reshape(lead + (-1,))
    out, pos = [], 0
    for shape in shapes:
        n = math.prod(shape)
        out.append(flat[..., pos:pos + n].reshape(lead + tuple(shape)))
        pos += n
    return out


def _small_rows(n_elems):
    return -(-n_elems // (128 * SMALL_ROW_TILE)) * SMALL_ROW_TILE


def _from_shards(name, gathered):
    _, a, b = gathered.shape
    if name == "w_in":
        return tuple(_pack_w_in(gathered))
    if name in BIG_COL_SHARDED:
        return _join_columns(gathered)
    return gathered.reshape(N_DEV * a, b)


def _to_shards(name, g):
    if name == "w_in":
        return g
    if name in BIG_COL_SHARDED:
        return _split_columns(g)
    a, b = g.shape
    return g.reshape(N_DEV, a // N_DEV, b)


def _join_columns(gathered):
    _, r, b = gathered.shape
    tr = _pick_tile(r, (128,))

    def body(g_ref, o_ref):
        o_ref[...] = jnp.concatenate([g_ref[j].astype(F32) for j in range(N_DEV)], axis=1).astype(o_ref.dtype)

    return _call("join_columns", body, grid=(r // tr,), ins=[gathered],
                 in_specs=[pl.BlockSpec((N_DEV, tr, b), lambda i: (0, i, 0))],
                 out_specs=[pl.BlockSpec((tr, N_DEV * b), lambda i: (i, 0))],
                 out_shape=[jax.ShapeDtypeStruct((r, N_DEV * b), gathered.dtype)])[0][0]


def _split_columns(full):
    r, nb = full.shape
    b = nb // N_DEV
    tr = _pick_tile(r, (128,))

    def body(f_ref, o_ref):
        w = f_ref[...].astype(F32)
        for j in range(N_DEV):
            o_ref[j] = w[:, j * b:(j + 1) * b].astype(o_ref.dtype)

    return _call("split_columns", body, grid=(r // tr,), ins=[full],
                 in_specs=[pl.BlockSpec((tr, nb), lambda i: (i, 0))],
                 out_specs=[pl.BlockSpec((N_DEV, tr, b), lambda i: (0, i, 0))],
                 out_shape=[jax.ShapeDtypeStruct((N_DEV, r, b), full.dtype)])[0][0]


class _MeshExchange:
    def __init__(self, shards_bf16, first):
        self.shards = shards_bf16
        self.full = dict(first)
        self.pieces = {}
        self.grads = {}
        self.to_send = {}
        self.received = {}
        self.small = None
        self.small_gathered = None

    def weight(self, name, l):
        if (name, l) not in self.full:
            got = jnp.concatenate([self.pieces[(name, l, q)] for q in range(W_IN_PIECES)], axis=1)
            self.full[(name, l)] = _from_shards(name, got)
        return self.full[(name, l)]

    def grad(self, name, l, g):
        self.grads[(name, l)] = g

    def small_grads(self, done):
        self.small = done

    def partial_sums(self, name, l):
        if (name, l, None) in self.received:
            return self.received[(name, l, None)]
        return jnp.concatenate([self.received[(name, l, q)] for q in range(W_IN_PIECES)], axis=1)

    def _slices(self, name, l, piece):
        n_rows = D_MODEL // W_IN_PIECES
        key, rows = (name, l), None if piece is None else (piece * n_rows, n_rows)
        if key not in self.to_send:
            self.to_send[key] = _to_shards(name, self.grads[key])
        return self.to_send[key], rows

    def before(self, call, l):
        comm = _Comm()
        for name, layer, piece in GATHER_PLAN.get((call, l), ()):
            n_rows = D_MODEL // W_IN_PIECES
            comm.gathers.append((self.shards[name], layer, None if piece is None else (piece * n_rows, n_rows)))
        if (call, l) == SMALL_GATHER_CALL:
            names = SMALL_REP + SMALL_SH
            rows = _small_rows(sum(math.prod(self.small[n].shape) for n in names))
            comm.gathers.append((_pack_flat([self.small[n] for n in names], rows), None, None))
        for name, layer, piece in SCATTER_PLAN.get((call, l), ()):
            comm.scatters.append(self._slices(name, layer, piece))
        return comm if comm.gathers or comm.scatters else None

    def after(self, call, l, outs):
        gathers = list(GATHER_PLAN.get((call, l), ()))
        for (name, layer, piece), out in zip(gathers, outs):
            if piece is None:
                self.full[(name, layer)] = _from_shards(name, out)
            else:
                self.pieces[(name, layer, piece)] = out
        outs = outs[len(gathers):]
        if (call, l) == SMALL_GATHER_CALL:
            self.small_gathered = outs[0]
            outs = outs[1:]
        for item, out in zip(SCATTER_PLAN.get((call, l), ()), outs):
            self.received[item] = out


def kernel(x, mem, mem_ln_g, mem_ln_b, w_in, sg_ln_g, sg_ln_b, sg_w, sg_b, conv_w, conv_b, dt_bias, a_log, d_skip, ssm_norm_g, p_a, p_b, w_mix_o, w_xq, w_xkv, w_xo, w_ffn_in, w_ffn_out, ln_g, ln_b, loss_target, m_mem_ln_g, m_mem_ln_b, m_w_in, m_sg_ln_g, m_sg_ln_b, m_sg_w, m_sg_b, m_conv_w, m_conv_b, m_dt_bias, m_a_log, m_d_skip, m_ssm_norm_g, m_p_a, m_p_b, m_w_mix_o, m_w_xq, m_w_xkv, m_w_xo, m_w_ffn_in, m_w_ffn_out, m_ln_g, m_ln_b, v_mem_ln_g, v_mem_ln_b, v_w_in, v_sg_ln_g, v_sg_ln_b, v_sg_w, v_sg_b, v_conv_w, v_conv_b, v_dt_bias, v_a_log, v_d_skip, v_ssm_norm_g, v_p_a, v_p_b, v_w_mix_o, v_w_xq, v_w_xkv, v_w_xo, v_w_ffn_in, v_w_ffn_out, v_ln_g, v_ln_b):
    args = dict(locals())
    w = {n: args[n] for n in WEIGHTS}
    m = {n: args["m_" + n] for n in WEIGHTS}
    v = {n: args["v_" + n] for n in WEIGHTS}
    me = 4 * lax.axis_index("x") + 2 * lax.axis_index("y") + lax.axis_index("c")

    shards = {n: w[n].astype(BF16) for n in BIG}
    sh_shapes = [w[n].shape for n in SMALL_SH]
    first = _Comm()
    first.gathers.append((shards["w_in"], 0, None))
    first.gathers.append((_pack_flat([w[n] for n in SMALL_SH], _small_rows(sum(math.prod(s) for s in sh_shapes))), None,
                          None))
    w_in0, small_sh = _comm_only("gather_first", first)
    small = {n: w[n] for n in SMALL_REP}
    for n, sh in zip(SMALL_SH, _unpack_flat(small_sh, sh_shapes)):
        small[n] = sh.transpose(1, 2, 0, 3).reshape(sh.shape[1], sh.shape[2], N_DEV * sh.shape[3])

    ex = _MeshExchange(shards, {("w_in", 0): _from_shards("w_in", w_in0)})
    loss, grad_x = _run_step(x, mem, loss_target, small, ex)
    loss = lax.psum(loss[0, 0], ("x", "y", "c"))

    out = {}
    for n in BIG[1:] + BIG[:1]:
        comm = ex.before("adamw_" + n, 0)
        out[n], comm_outs = _adamw_sharded("adamw_" + n, [ex.partial_sums(n, l) for l in range(DEPTH)], w[n], m[n], v[n],
                                           comm=comm)
        if comm is not None:
            ex.after("adamw_" + n, 0, comm_outs)
    names = SMALL_REP + SMALL_SH
    g_small = dict(zip(names, _unpack_flat(_sum_parts("sum_small_grads", ex.small_gathered),
                                           [ex.small[n].shape for n in names])))
    for n in names:
        g = g_small[n]
        if n in SMALL_SH:
            width = w[n].shape[-1]
            g = lax.dynamic_slice_in_dim(g, me * width, width, axis=-1)
        two_d = (-1, w[n].shape[-1])
        res = _adamw_small("adamw_" + n, g.reshape(two_d), w[n].reshape(two_d), m[n].reshape(two_d), v[n].reshape(two_d))
        out[n] = [g] + [r.reshape(w[n].shape) for r in res]

    results = []
    for k in range(4):
        results.extend(out[n][k] for n in WEIGHTS)
    return (loss, grad_x, *results)
```
